```python
import jax, jax.numpy as jnp
from jax import lax
import numpy as np

D_MODEL = 1024
BATCH = 8
SEQ = 4096
DEPTH = 1

CHUNK = 64
POOL_WINDOWS = (2, 4, 8, 16)
N_POOL_GROUPS = len(POOL_WINDOWS)
D_POOL = D_MODEL
POOL_GROUP = D_POOL // N_POOL_GROUPS
SGU_BLOCK = 128
N_SGU_HEADS = 4
D_SGU = D_MODEL
SGU_HEAD = D_SGU // N_SGU_HEADS
D_IN = D_POOL + 2 * D_SGU + D_POOL + D_SGU
D_FF = 4 * D_MODEL
EPS = 1e-6

kernel_name = "hybrid_pool_sgu_gated_block"


def rms_norm(x, g):
    xf = x.astype(jnp.float32)
    y = xf * lax.rsqrt(jnp.mean(xf * xf, axis=-1, keepdims=True) + EPS)
    return (y * g.astype(jnp.float32)).astype(x.dtype)


def layer_norm(x, g, b):
    xf = x.astype(jnp.float32)
    mu = jnp.mean(xf, axis=-1, keepdims=True)
    xc = xf - mu
    y = xc * lax.rsqrt(jnp.mean(xc * xc, axis=-1, keepdims=True) + EPS)
    return (y * g.astype(jnp.float32) + b.astype(jnp.float32)).astype(x.dtype)


def multiscale_pool(p):
    s_len = p.shape[1]
    pf = p.astype(jnp.float32)
    csum = jnp.cumsum(pf, axis=1)
    pos1 = jnp.arange(1, s_len + 1)
    outs = []
    for gi, w in enumerate(POOL_WINDOWS):
        sl = slice(gi * POOL_GROUP, (gi + 1) * POOL_GROUP)
        cg = csum[..., sl]
        prev = jnp.pad(cg, ((0, 0), (w, 0), (0, 0)))[:, :s_len]
        cnt = jnp.minimum(pos1, w).astype(jnp.float32)[None, :, None]
        outs.append((cg - prev) / cnt - pf[..., sl])
    return jnp.stack(outs, axis=2).astype(p.dtype)


def chunk_causal_block_mask():
    pos = jnp.arange(SGU_BLOCK)
    return (pos[:, None] // CHUNK) >= (pos[None, :] // CHUNK)


def _fwd_setup_inputs(seed: int = 0) -> dict:
    key = jax.random.key(seed)
    ks = jax.random.split(key, 20)
    f32 = jnp.float32
    nrm = lambda k, shape, s: jax.random.normal(k, shape, f32) * s
    return {
        "x": jax.random.normal(ks[0], (BATCH, SEQ, D_MODEL), f32),
        "norm1_pre_g": 1.0 + nrm(ks[1], (D_MODEL,), 0.05),
        "w_in": nrm(ks[2], (D_MODEL, D_IN), D_MODEL ** -0.5),
        "b_in": nrm(ks[3], (D_IN,), 0.02),
        "w_pool": nrm(ks[4], (N_POOL_GROUPS, POOL_GROUP, POOL_GROUP), POOL_GROUP ** -0.5),
        "pool_scale": 1.0 + nrm(ks[5], (D_POOL,), 0.1),
        "sgu_ln_g": 1.0 + nrm(ks[6], (D_SGU,), 0.05),
        "sgu_ln_b": nrm(ks[7], (D_SGU,), 0.02),
        "w_spatial": nrm(ks[8], (N_SGU_HEADS, SGU_BLOCK, SGU_BLOCK), SGU_BLOCK ** -0.5),
        "b_spatial": 1.0 + nrm(ks[9], (N_SGU_HEADS, SGU_BLOCK), 0.02),
        "w_sgu_proj": nrm(ks[10], (N_SGU_HEADS, SGU_HEAD, SGU_HEAD), SGU_HEAD ** -0.5),
        "w_out": nrm(ks[11], (D_MODEL, D_MODEL), D_MODEL ** -0.5),
        "norm1_post_g": 1.0 + nrm(ks[12], (D_MODEL,), 0.05),
        "norm2_pre_g": 1.0 + nrm(ks[13], (D_MODEL,), 0.05),
        "w_ff1": nrm(ks[14], (D_MODEL, D_FF), D_MODEL ** -0.5),
        "w_ff2": nrm(ks[15], (D_FF, D_MODEL), D_FF ** -0.5),
        "norm2_post_g": 1.0 + nrm(ks[16], (D_MODEL,), 0.05),
    }


def _fwd_reference(x, norm1_pre_g, w_in, b_in, w_pool, pool_scale, sgu_ln_g, sgu_ln_b,
              w_spatial, b_spatial, w_sgu_proj, w_out, norm1_post_g, norm2_pre_g,
              w_ff1, w_ff2, norm2_post_g):
    bsz, s_len, _ = x.shape
    n_blk = s_len // SGU_BLOCK
    h = x
    for _ in range(DEPTH):
        xn = rms_norm(h, norm1_pre_g)
        z = jnp.einsum('bsd,de->bse', xn, w_in) + b_in
        o = 0
        z_pool = z[..., o:o + D_POOL]; o += D_POOL
        z_u = z[..., o:o + D_SGU]; o += D_SGU
        z_v = z[..., o:o + D_SGU]; o += D_SGU
        z_ga = z[..., o:o + D_POOL]; o += D_POOL
        z_gb = z[..., o:o + D_SGU]

        pooled = multiscale_pool(z_pool)
        a = jnp.einsum('bsgc,gcd->bsgd', pooled, w_pool).reshape(bsz, s_len, D_POOL)
        a = a * pool_scale

        u = jax.nn.gelu(z_u)
        v = layer_norm(jax.nn.gelu(z_v), sgu_ln_g, sgu_ln_b)
        vb = v.reshape(bsz, n_blk, SGU_BLOCK, N_SGU_HEADS, SGU_HEAD)
        ws = jnp.where(chunk_causal_block_mask()[None], w_spatial, 0.0).astype(v.dtype)
        sv = jnp.einsum('hij,bnjhc->bnihc', ws, vb) + b_spatial.T[None, None, :, :, None]
        gated = u.reshape(bsz, n_blk, SGU_BLOCK, N_SGU_HEADS, SGU_HEAD) * sv
        bbr = jnp.einsum('bnihc,hcd->bnihd', gated, w_sgu_proj).reshape(bsz, s_len, D_SGU)

        merged = jax.nn.sigmoid(z_ga) * a + jax.nn.sigmoid(z_gb) * bbr
        y = jnp.einsum('bsd,de->bse', merged, w_out)
        h = h + rms_norm(y, norm1_post_g)

        hn = rms_norm(h, norm2_pre_g)
        f = jnp.square(jax.nn.relu(jnp.einsum('bsd,df->bsf', hn, w_ff1)))
        f = jnp.einsum('bsf,fd->bsd', f, w_ff2)
        h = h + rms_norm(f, norm2_post_g)
    return h


import jax as _jax
import jax.numpy as _jnp

TWIN_FORMAT = 'train_step'
FWD_PARAMS = ['x', 'norm1_pre_g', 'w_in', 'b_in', 'w_pool', 'pool_scale', 'sgu_ln_g', 'sgu_ln_b', 'w_spatial', 'b_spatial', 'w_sgu_proj', 'w_out', 'norm1_post_g', 'norm2_pre_g', 'w_ff1', 'w_ff2', 'norm2_post_g']
TWIN_WEIGHTS = ['norm1_pre_g', 'w_in', 'b_in', 'w_pool', 'pool_scale', 'sgu_ln_g', 'sgu_ln_b', 'w_spatial', 'b_spatial', 'w_sgu_proj', 'w_out', 'norm1_post_g', 'norm2_pre_g', 'w_ff1', 'w_ff2', 'norm2_post_g']
TWIN_DIFF_INPUT = 'x'
TWIN_INPUTS = ['x', 'norm1_pre_g', 'w_in', 'b_in', 'w_pool', 'pool_scale', 'sgu_ln_g', 'sgu_ln_b', 'w_spatial', 'b_spatial', 'w_sgu_proj', 'w_out', 'norm1_post_g', 'norm2_pre_g', 'w_ff1', 'w_ff2', 'norm2_post_g', 'loss_target', 'm_norm1_pre_g', 'm_w_in', 'm_b_in', 'm_w_pool', 'm_pool_scale', 'm_sgu_ln_g', 'm_sgu_ln_b', 'm_w_spatial', 'm_b_spatial', 'm_w_sgu_proj', 'm_w_out', 'm_norm1_post_g', 'm_norm2_pre_g', 'm_w_ff1', 'm_w_ff2', 'm_norm2_post_g', 'v_norm1_pre_g', 'v_w_in', 'v_b_in', 'v_w_pool', 'v_pool_scale', 'v_sgu_ln_g', 'v_sgu_ln_b', 'v_w_spatial', 'v_b_spatial', 'v_w_sgu_proj', 'v_w_out', 'v_norm1_post_g', 'v_norm2_pre_g', 'v_w_ff1', 'v_w_ff2', 'v_norm2_post_g']
TWIN_OUTPUTS = ['loss', 'grad_x', 'grad_norm1_pre_g', 'grad_w_in', 'grad_b_in', 'grad_w_pool', 'grad_pool_scale', 'grad_sgu_ln_g', 'grad_sgu_ln_b', 'grad_w_spatial', 'grad_b_spatial', 'grad_w_sgu_proj', 'grad_w_out', 'grad_norm1_post_g', 'grad_norm2_pre_g', 'grad_w_ff1', 'grad_w_ff2', 'grad_norm2_post_g', 'delta_norm1_pre_g', 'delta_w_in', 'delta_b_in', 'delta_w_pool', 'delta_pool_scale', 'delta_sgu_ln_g', 'delta_sgu_ln_b', 'delta_w_spatial', 'delta_b_spatial', 'delta_w_sgu_proj', 'delta_w_out', 'delta_norm1_post_g', 'delta_norm2_pre_g', 'delta_w_ff1', 'delta_w_ff2', 'delta_norm2_post_g', 'new_m_norm1_pre_g', 'new_m_w_in', 'new_m_b_in', 'new_m_w_pool', 'new_m_pool_scale', 'new_m_sgu_ln_g', 'new_m_sgu_ln_b', 'new_m_w_spatial', 'new_m_b_spatial', 'new_m_w_sgu_proj', 'new_m_w_out', 'new_m_norm1_post_g', 'new_m_norm2_pre_g', 'new_m_w_ff1', 'new_m_w_ff2', 'new_m_norm2_post_g', 'new_v_norm1_pre_g', 'new_v_w_in', 'new_v_b_in', 'new_v_w_pool', 'new_v_pool_scale', 'new_v_sgu_ln_g', 'new_v_sgu_ln_b', 'new_v_w_spatial', 'new_v_b_spatial', 'new_v_w_sgu_proj', 'new_v_w_out', 'new_v_norm1_post_g', 'new_v_norm2_pre_g', 'new_v_w_ff1', 'new_v_w_ff2', 'new_v_norm2_post_g']
TWIN_LEAF_KINDS = {'loss': 'loss', 'grad_x': 'grad_x', 'grad_norm1_pre_g': 'grad_w', 'grad_w_in': 'grad_w', 'grad_b_in': 'grad_w', 'grad_w_pool': 'grad_w', 'grad_pool_scale': 'grad_w', 'grad_sgu_ln_g': 'grad_w', 'grad_sgu_ln_b': 'grad_w', 'grad_w_spatial': 'grad_w', 'grad_b_spatial': 'grad_w', 'grad_w_sgu_proj': 'grad_w', 'grad_w_out': 'grad_w', 'grad_norm1_post_g': 'grad_w', 'grad_norm2_pre_g': 'grad_w', 'grad_w_ff1': 'grad_w', 'grad_w_ff2': 'grad_w', 'grad_norm2_post_g': 'grad_w', 'delta_norm1_pre_g': 'delta_w', 'delta_w_in': 'delta_w', 'delta_b_in': 'delta_w', 'delta_w_pool': 'delta_w', 'delta_pool_scale': 'delta_w', 'delta_sgu_ln_g': 'delta_w', 'delta_sgu_ln_b': 'delta_w', 'delta_w_spatial': 'delta_w', 'delta_b_spatial': 'delta_w', 'delta_w_sgu_proj': 'delta_w', 'delta_w_out': 'delta_w', 'delta_norm1_post_g': 'delta_w', 'delta_norm2_pre_g': 'delta_w', 'delta_w_ff1': 'delta_w', 'delta_w_ff2': 'delta_w', 'delta_norm2_post_g': 'delta_w', 'new_m_norm1_pre_g': 'new_m', 'new_m_w_in': 'new_m', 'new_m_b_in': 'new_m', 'new_m_w_pool': 'new_m', 'new_m_pool_scale': 'new_m', 'new_m_sgu_ln_g': 'new_m', 'new_m_sgu_ln_b': 'new_m', 'new_m_w_spatial': 'new_m', 'new_m_b_spatial': 'new_m', 'new_m_w_sgu_proj': 'new_m', 'new_m_w_out': 'new_m', 'new_m_norm1_post_g': 'new_m', 'new_m_norm2_pre_g': 'new_m', 'new_m_w_ff1': 'new_m', 'new_m_w_ff2': 'new_m', 'new_m_norm2_post_g': 'new_m', 'new_v_norm1_pre_g': 'new_v', 'new_v_w_in': 'new_v', 'new_v_b_in': 'new_v', 'new_v_w_pool': 'new_v', 'new_v_pool_scale': 'new_v', 'new_v_sgu_ln_g': 'new_v', 'new_v_sgu_ln_b': 'new_v', 'new_v_w_spatial': 'new_v', 'new_v_b_spatial': 'new_v', 'new_v_w_sgu_proj': 'new_v', 'new_v_w_out': 'new_v', 'new_v_norm1_post_g': 'new_v', 'new_v_norm2_pre_g': 'new_v', 'new_v_w_ff1': 'new_v', 'new_v_w_ff2': 'new_v', 'new_v_norm2_post_g': 'new_v'}


def _forward(args):
    return _fwd_reference(*[args[k] for k in FWD_PARAMS])


def _output_shape():
    out = _jax.eval_shape(lambda: _forward(_fwd_setup_inputs(0)))
    return out.shape, out.dtype

N_MICROBATCH = 1
ADAM_LR = 0.001
ADAM_B1 = 0.9
ADAM_B2 = 0.999
ADAM_EPS = 1e-08
ADAM_WD = 0.01
ADAM_STEP = 10
PER_EXAMPLE_BATCH_AXIS = {'x': 0, 'loss_target': 0}
SHARED_INPUTS = []
_WEIGHT_DTYPES = {'norm1_pre_g': _jnp.float32, 'w_in': _jnp.float32, 'b_in': _jnp.float32, 'w_pool': _jnp.float32, 'pool_scale': _jnp.float32, 'sgu_ln_g': _jnp.float32, 'sgu_ln_b': _jnp.float32, 'w_spatial': _jnp.float32, 'b_spatial': _jnp.float32, 'w_sgu_proj': _jnp.float32, 'w_out': _jnp.float32, 'norm1_post_g': _jnp.float32, 'norm2_pre_g': _jnp.float32, 'w_ff1': _jnp.float32, 'w_ff2': _jnp.float32, 'norm2_post_g': _jnp.float32}
MOMENT_SCALE = {'norm1_pre_g': 7.140185e-01, 'w_in': 3.242743e-01, 'b_in': 2.223271e+00, 'w_pool': 5.994737e-01, 'pool_scale': 7.160588e-01, 'sgu_ln_g': 2.203826e-01, 'sgu_ln_b': 2.133838e-01, 'w_spatial': 2.587374e-01, 'b_spatial': 3.342111e-01, 'w_sgu_proj': 2.777513e+00, 'w_out': 2.746371e+00, 'norm1_post_g': 3.229999e+01, 'norm2_pre_g': 1.072142e+00, 'w_ff1': 5.203809e-01, 'w_ff2': 2.786802e+00, 'norm2_post_g': 3.319762e+01}


def _to_microbatches(a, axis):
    t = _jnp.moveaxis(a, axis, 0)
    t = t.reshape((N_MICROBATCH, t.shape[0] // N_MICROBATCH) + t.shape[1:])
    return _jnp.moveaxis(t, 1, axis + 1)


def setup_inputs(seed: int = 0) -> dict:
    inp = _fwd_setup_inputs(seed)
    key = _jax.random.fold_in(_jax.random.key(seed), 7919)
    shape, _ = _output_shape()
    out = dict(inp)
    out["loss_target"] = _jax.random.normal(_jax.random.fold_in(key, 0), shape, _jnp.float32)
    for i, name in enumerate(TWIN_WEIGHTS):
        w = inp[name].astype(_jnp.float32)
        if MOMENT_SCALE is None:
            s = _jnp.sqrt(_jnp.mean(_jnp.square(w)) + 1e-30)
        else:
            s = MOMENT_SCALE[name]
        km, kv = _jax.random.split(_jax.random.fold_in(key, i + 1))
        out[name] = w
        out["m_" + name] = s * _jax.random.normal(km, w.shape, _jnp.float32)
        out["v_" + name] = (s * s) * _jax.random.uniform(kv, w.shape, _jnp.float32, 0.5, 1.5)
    if N_MICROBATCH > 1:
        for name, axis in PER_EXAMPLE_BATCH_AXIS.items():
            out[name] = _to_microbatches(out[name], axis)
    return {'x': out['x'], 'norm1_pre_g': out['norm1_pre_g'], 'w_in': out['w_in'], 'b_in': out['b_in'], 'w_pool': out['w_pool'], 'pool_scale': out['pool_scale'], 'sgu_ln_g': out['sgu_ln_g'], 'sgu_ln_b': out['sgu_ln_b'], 'w_spatial': out['w_spatial'], 'b_spatial': out['b_spatial'], 'w_sgu_proj': out['w_sgu_proj'], 'w_out': out['w_out'], 'norm1_post_g': out['norm1_post_g'], 'norm2_pre_g': out['norm2_pre_g'], 'w_ff1': out['w_ff1'], 'w_ff2': out['w_ff2'], 'norm2_post_g': out['norm2_post_g'], 'loss_target': out['loss_target'], 'm_norm1_pre_g': out['m_norm1_pre_g'], 'm_w_in': out['m_w_in'], 'm_b_in': out['m_b_in'], 'm_w_pool': out['m_w_pool'], 'm_pool_scale': out['m_pool_scale'], 'm_sgu_ln_g': out['m_sgu_ln_g'], 'm_sgu_ln_b': out['m_sgu_ln_b'], 'm_w_spatial': out['m_w_spatial'], 'm_b_spatial': out['m_b_spatial'], 'm_w_sgu_proj': out['m_w_sgu_proj'], 'm_w_out': out['m_w_out'], 'm_norm1_post_g': out['m_norm1_post_g'], 'm_norm2_pre_g': out['m_norm2_pre_g'], 'm_w_ff1': out['m_w_ff1'], 'm_w_ff2': out['m_w_ff2'], 'm_norm2_post_g': out['m_norm2_post_g'], 'v_norm1_pre_g': out['v_norm1_pre_g'], 'v_w_in': out['v_w_in'], 'v_b_in': out['v_b_in'], 'v_w_pool': out['v_w_pool'], 'v_pool_scale': out['v_pool_scale'], 'v_sgu_ln_g': out['v_sgu_ln_g'], 'v_sgu_ln_b': out['v_sgu_ln_b'], 'v_w_spatial': out['v_w_spatial'], 'v_b_spatial': out['v_b_spatial'], 'v_w_sgu_proj': out['v_w_sgu_proj'], 'v_w_out': out['v_w_out'], 'v_norm1_post_g': out['v_norm1_post_g'], 'v_norm2_pre_g': out['v_norm2_pre_g'], 'v_w_ff1': out['v_w_ff1'], 'v_w_ff2': out['v_w_ff2'], 'v_norm2_post_g': out['v_norm2_post_g']}


def _loss(weights, diff, rest, loss_target):
    with _jax.named_scope("forward"):
        args = {**rest, TWIN_DIFF_INPUT: diff, **{k: w.astype(_WEIGHT_DTYPES[k]) for k, w in weights.items()}}
        y = _forward(args)
    with _jax.named_scope("loss_head"):
        err = _jnp.square(y.astype(_jnp.float32) - loss_target)
        return 0.5 * _jnp.sum(_jnp.mean(err, axis=-1)) if err.ndim else 0.5 * err


def _adamw(w, g, m, v):
    m = ADAM_B1 * m + (1.0 - ADAM_B1) * g
    v = ADAM_B2 * v + (1.0 - ADAM_B2) * _jnp.square(g)
    m_hat = m / (1.0 - ADAM_B1 ** ADAM_STEP)
    v_hat = v / (1.0 - ADAM_B2 ** ADAM_STEP)
    delta = -ADAM_LR * (m_hat / (_jnp.sqrt(v_hat) + ADAM_EPS) + ADAM_WD * w)
    return delta, m, v


def reference(x, norm1_pre_g, w_in, b_in, w_pool, pool_scale, sgu_ln_g, sgu_ln_b, w_spatial, b_spatial, w_sgu_proj, w_out, norm1_post_g, norm2_pre_g, w_ff1, w_ff2, norm2_post_g, loss_target, m_norm1_pre_g, m_w_in, m_b_in, m_w_pool, m_pool_scale, m_sgu_ln_g, m_sgu_ln_b, m_w_spatial, m_b_spatial, m_w_sgu_proj, m_w_out, m_norm1_post_g, m_norm2_pre_g, m_w_ff1, m_w_ff2, m_norm2_post_g, v_norm1_pre_g, v_w_in, v_b_in, v_w_pool, v_pool_scale, v_sgu_ln_g, v_sgu_ln_b, v_w_spatial, v_b_spatial, v_w_sgu_proj, v_w_out, v_norm1_post_g, v_norm2_pre_g, v_w_ff1, v_w_ff2, v_norm2_post_g):
    given = dict(x=x, norm1_pre_g=norm1_pre_g, w_in=w_in, b_in=b_in, w_pool=w_pool, pool_scale=pool_scale, sgu_ln_g=sgu_ln_g, sgu_ln_b=sgu_ln_b, w_spatial=w_spatial, b_spatial=b_spatial, w_sgu_proj=w_sgu_proj, w_out=w_out, norm1_post_g=norm1_post_g, norm2_pre_g=norm2_pre_g, w_ff1=w_ff1, w_ff2=w_ff2, norm2_post_g=norm2_post_g, loss_target=loss_target, m_norm1_pre_g=m_norm1_pre_g, m_w_in=m_w_in, m_b_in=m_b_in, m_w_pool=m_w_pool, m_pool_scale=m_pool_scale, m_sgu_ln_g=m_sgu_ln_g, m_sgu_ln_b=m_sgu_ln_b, m_w_spatial=m_w_spatial, m_b_spatial=m_b_spatial, m_w_sgu_proj=m_w_sgu_proj, m_w_out=m_w_out, m_norm1_post_g=m_norm1_post_g, m_norm2_pre_g=m_norm2_pre_g, m_w_ff1=m_w_ff1, m_w_ff2=m_w_ff2, m_norm2_post_g=m_norm2_post_g, v_norm1_pre_g=v_norm1_pre_g, v_w_in=v_w_in, v_b_in=v_b_in, v_w_pool=v_w_pool, v_pool_scale=v_pool_scale, v_sgu_ln_g=v_sgu_ln_g, v_sgu_ln_b=v_sgu_ln_b, v_w_spatial=v_w_spatial, v_b_spatial=v_b_spatial, v_w_sgu_proj=v_w_sgu_proj, v_w_out=v_w_out, v_norm1_post_g=v_norm1_post_g, v_norm2_pre_g=v_norm2_pre_g, v_w_ff1=v_w_ff1, v_w_ff2=v_w_ff2, v_norm2_post_g=v_norm2_post_g)
    weights = {n: given[n] for n in TWIN_WEIGHTS}
    shared = {n: given[n] for n in SHARED_INPUTS}
    per_example = {n: given[n] for n in ['x']}
    grad_fn = _jax.value_and_grad(_loss, argnums=(0, 1))

    def one_microbatch(ex, loss_target):
        ex = dict(ex)
        diff = ex.pop(TWIN_DIFF_INPUT)
        return grad_fn(weights, diff, {**shared, **ex}, loss_target)

    if N_MICROBATCH == 1:
        loss, (grad_w, grad_x) = one_microbatch(per_example, given["loss_target"])
    else:
        def body(carry, xs):
            loss_sum, grad_sum = carry
            l_k, (gw_k, gx_k) = one_microbatch(xs[0], xs[1])
            with _jax.named_scope("update"):
                return (loss_sum + l_k, _jax.tree.map(_jnp.add, grad_sum, gw_k)), gx_k

        init = (_jnp.zeros((), _jnp.float32), _jax.tree.map(_jnp.zeros_like, weights))
        (loss, grad_w), grad_x = _jax.lax.scan(body, init, (per_example, given["loss_target"]))
    with _jax.named_scope("update"):
        delta_w, new_m, new_v = {}, {}, {}
        for n in TWIN_WEIGHTS:
            delta_w[n], new_m[n], new_v[n] = _adamw(weights[n], grad_w[n], given["m_" + n], given["v_" + n])
    return (loss, grad_x, *[grad_w[n] for n in TWIN_WEIGHTS], *[delta_w[n] for n in TWIN_WEIGHTS],
            *[new_m[n] for n in TWIN_WEIGHTS], *[new_v[n] for n in TWIN_WEIGHTS])
```

```python
import functools
import math

import jax
import jax.numpy as jnp
from jax import lax
from jax.experimental import pallas as pl
from jax.experimental.pallas import tpu as pltpu

F32 = jnp.float32
BF16 = jnp.bfloat16

D = 1024
D_IN = 5 * D
D_FF = 4 * D
N_SHARD = 4
N_CHIP_PEERS = 3
POOL_WINDOWS = (2, 4, 8, 16)
POOL_GROUP = 256
POOL_HALO = 16
SGU_BLOCK = 128
N_HEADS = 4
HEAD = 256
EPS = 1e-6

ADAM_LR = 0.001
ADAM_B1 = 0.9
ADAM_B2 = 0.999
ADAM_EPS = 1e-08
ADAM_WD = 0.01
ADAM_STEP = 10

V7X_VMEM_BYTES = 64 * 1024 * 1024
VMEM_LIMIT = V7X_VMEM_BYTES - 8 * 1024 * 1024
SMALL_ROWS = 616
LANES = 128

MESH = pl.DeviceIdType.MESH
ANY = pl.BlockSpec(memory_space=pl.ANY)

NT_DIMS = (((1,), (1,)), ((), ()))
TN_DIMS = (((0,), (0,)), ((), ()))


def _params(n_axes):
    return pltpu.CompilerParams(dimension_semantics=("arbitrary",) * n_axes, vmem_limit_bytes=VMEM_LIMIT)


def _dot(a, b):
    return jnp.dot(a, b, preferred_element_type=F32)


def _dot_nt(a, b):
    return lax.dot_general(a, b, NT_DIMS, preferred_element_type=F32)


def _dot_tn(a, b):
    return lax.dot_general(a, b, TN_DIMS, preferred_element_type=F32)


def _mean(a):
    return jnp.mean(a, axis=-1, keepdims=True)


def _colsum(a):
    return jnp.sum(a, axis=0, keepdims=True)


_GELU_C0 = math.sqrt(2.0 / math.pi)
_GELU_C1 = 0.044715


def _gelu(a):
    t = jnp.tanh(_GELU_C0 * (a + _GELU_C1 * (a * a * a)))
    return a * (0.5 * (1.0 + t))


def _gelu_and_grad(a):
    a2 = a * a
    t = jnp.tanh(_GELU_C0 * (a + _GELU_C1 * (a2 * a)))
    cdf = 0.5 * (1.0 + t)
    grad = cdf + 0.5 * a * (1.0 - t * t) * (_GELU_C0 * (1.0 + 3.0 * _GELU_C1 * a2))
    return a * cdf, grad


def _rms_bwd(dn, n, r):
    return r * (dn - n * _mean(dn * n))


def _inproj(x, g1, w_in4, b_in, tm):
    t = x.shape[0]
    cw = D_IN // N_SHARD

    def body(x_ref, g_ref, w_ref, b_ref, z_ref, xn_ref):
        @pl.when(pl.program_id(1) == 0)
        def _():
            xv = x_ref[...]
            r = lax.rsqrt(_mean(xv * xv) + EPS)
            xn_ref[...] = (xv * r * g_ref[...]).astype(BF16)

        z_ref[...] = _dot(xn_ref[...], w_ref[0]) + b_ref[...]

    return pl.pallas_call(
        body, name="inproj",
        grid=(t // tm, N_SHARD),
        in_specs=[pl.BlockSpec((tm, D), lambda i, s: (i, 0)),
                  pl.BlockSpec((1, D), lambda i, s: (0, 0)),
                  pl.BlockSpec((1, D, cw), lambda i, s: (s, 0, 0)),
                  pl.BlockSpec((1, cw), lambda i, s: (0, s))],
        out_specs=[pl.BlockSpec((tm, cw), lambda i, s: (i, s)),
                   pl.BlockSpec((tm, D), lambda i, s: (i, 0))],
        out_shape=[jax.ShapeDtypeStruct((t, D_IN), F32), jax.ShapeDtypeStruct((t, D), BF16)],
        compiler_params=_params(2),
    )(x, g1, w_in4, b_in)


def _sgu_forward(z_ref, lng_ref, lnb_ref, ws_ref, bsp_ref, sv_ref, tm, with_grad):
    if with_grad:
        u, du_dz = _gelu_and_grad(z_ref[:, D:2 * D])
        gv, dgv_dz = _gelu_and_grad(z_ref[:, 2 * D:3 * D])
    else:
        u = _gelu(z_ref[:, D:2 * D])
        gv = _gelu(z_ref[:, 2 * D:3 * D])
        du_dz = dgv_dz = None
    xc = gv - _mean(gv)
    rstd = lax.rsqrt(_mean(xc * xc) + EPS)
    vhat = xc * rstd
    vb = (vhat * lng_ref[...] + lnb_ref[...]).astype(BF16)
    for b in range(tm // SGU_BLOCK):
        rows = slice(b * SGU_BLOCK, (b + 1) * SGU_BLOCK)
        for h in range(N_HEADS):
            cols = slice(h * HEAD, (h + 1) * HEAD)
            sv_ref[rows, cols] = _dot(ws_ref[h], vb[rows, cols]) + bsp_ref[h]
    return u, du_dz, dgv_dz, vhat, rstd, vb


def _mix_fwd(z, x, w_pool, pool_scale, ln_g, ln_b, ws, bsp, w_proj, w_out, g1post, tm):
    t = x.shape[0]

    def body(z_ref, x_ref, wp_ref, psc_ref, lng_ref, lnb_ref, ws_ref, bsp_ref, wpr_ref, wo_ref, g1p_ref,
             pooled_ref, merged_ref, y_ref, h1_ref, win_ref, sv_ref):
        i = pl.program_id(0)

        @pl.when(i == 0)
        def _():
            win_ref[0:POOL_HALO, :] = jnp.zeros((POOL_HALO, D), F32)

        win_ref[POOL_HALO:POOL_HALO + tm, :] = z_ref[:, 0:D]
        pos1 = lax.broadcasted_iota(jnp.int32, (tm, 1), 0) + (i * tm + 1)
        a_parts = []
        for g, w in enumerate(POOL_WINDOWS):
            cols = slice(g * POOL_GROUP, (g + 1) * POOL_GROUP)
            ts = win_ref[POOL_HALO:POOL_HALO + tm, cols]
            for k in range(1, w):
                ts = ts + win_ref[POOL_HALO - k:POOL_HALO - k + tm, cols]
            inv_cnt = 1.0 / jnp.minimum(pos1, w).astype(F32)
            pooled = (ts * inv_cnt - z_ref[:, cols]).astype(BF16)
            pooled_ref[:, cols] = pooled
            a_parts.append(_dot(pooled, wp_ref[g]) * psc_ref[:, cols])
        win_ref[0:POOL_HALO, :] = win_ref[tm:tm + POOL_HALO, :]

        u, _, _, _, _, _ = _sgu_forward(z_ref, lng_ref, lnb_ref, ws_ref, bsp_ref, sv_ref, tm, False)
        gated = (u * sv_ref[...]).astype(BF16)
        b_parts = [_dot(gated[:, h * HEAD:(h + 1) * HEAD], wpr_ref[h]) for h in range(N_HEADS)]

        a = jnp.concatenate(a_parts, axis=1)
        bbr = jnp.concatenate(b_parts, axis=1)
        merged = (jax.nn.sigmoid(z_ref[:, 3 * D:4 * D]) * a + jax.nn.sigmoid(z_ref[:, 4 * D:5 * D]) * bbr).astype(BF16)
        merged_ref[...] = merged
        y = _dot(merged, wo_ref[...])
        y_ref[...] = y
        r = lax.rsqrt(_mean(y * y) + EPS)
        h1_ref[...] = x_ref[...] + (y * r) * g1p_ref[...]

    row = lambda i: (i, 0)
    fixed2 = lambda i: (0, 0)
    fixed3 = lambda i: (0, 0, 0)
    vec = pl.BlockSpec((1, D), fixed2)
    return pl.pallas_call(
        body, name="mix_fwd",
        grid=(t // tm,),
        in_specs=[pl.BlockSpec((tm, D_IN), row), pl.BlockSpec((tm, D), row),
                  pl.BlockSpec((N_HEADS, POOL_GROUP, POOL_GROUP), fixed3), vec, vec, vec,
                  pl.BlockSpec((N_HEADS, SGU_BLOCK, SGU_BLOCK), fixed3),
                  pl.BlockSpec((N_HEADS, SGU_BLOCK, 1), fixed3),
                  pl.BlockSpec((N_HEADS, HEAD, HEAD), fixed3),
                  pl.BlockSpec((D, D), fixed2), vec],
        out_specs=[pl.BlockSpec((tm, D), row)] * 4,
        out_shape=[jax.ShapeDtypeStruct((t, D), BF16), jax.ShapeDtypeStruct((t, D), BF16),
                   jax.ShapeDtypeStruct((t, D), F32), jax.ShapeDtypeStruct((t, D), F32)],
        scratch_shapes=[pltpu.VMEM((tm + POOL_HALO, D), F32), pltpu.VMEM((tm, D), F32)],
        compiler_params=_params(1),
    )(z, x, w_pool, pool_scale, ln_g, ln_b, ws, bsp, w_proj, w_out, g1post)


def _ffn_fwd(h1, g2pre, w1, w2, g2post, target, tm):
    t = h1.shape[0]
    n_j = D_FF // D

    def body(h1_ref, g2_ref, w1_ref, w2_ref, g2p_ref, tgt_ref,
             hn_ref, f1_ref, f_ref, df2_ref, dout_ref, vec_ref, acc_ref):
        i, j = pl.program_id(0), pl.program_id(1)

        @pl.when((i == 0) & (j == 0))
        def _():
            vec_ref[...] = jnp.zeros_like(vec_ref)

        @pl.when(j == 0)
        def _():
            h = h1_ref[...]
            r = lax.rsqrt(_mean(h * h) + EPS)
            hn_ref[...] = (h * r * g2_ref[...]).astype(BF16)
            acc_ref[...] = jnp.zeros_like(acc_ref)

        f1 = _dot(hn_ref[...], w1_ref[0])
        f1_ref[...] = f1
        rl = jnp.maximum(f1, 0.0)
        f = (rl * rl).astype(BF16)
        f_ref[...] = f
        acc_ref[...] += _dot(f, w2_ref[...])

        @pl.when(j == n_j - 1)
        def _():
            f2 = acc_ref[...]
            r = lax.rsqrt(_mean(f2 * f2) + EPS)
            n2 = f2 * r
            err = h1_ref[...] + n2 * g2p_ref[...] - tgt_ref[...]
            loss = 0.5 * jnp.sum(_mean(err * err))
            dout = err * (1.0 / D)
            dout_ref[...] = dout
            vec_ref[0:1, :] += _colsum(dout * n2)
            vec_ref[1:2, :] += jnp.full((1, D), loss, F32)
            df2_ref[...] = _rms_bwd(dout * g2p_ref[...], n2, r).astype(BF16)

    row = lambda i, j: (i, 0)
    fixed = lambda i, j: (0, 0)
    vec = pl.BlockSpec((1, D), fixed)
    return pl.pallas_call(
        body, name="ffn_fwd",
        grid=(t // tm, n_j),
        in_specs=[pl.BlockSpec((tm, D), row), vec,
                  pl.BlockSpec((1, D, D), lambda i, j: (j, 0, 0)),
                  pl.BlockSpec((D, D), lambda i, j: (j, 0)), vec,
                  pl.BlockSpec((tm, D), row)],
        out_specs=[pl.BlockSpec((tm, D), row),
                   pl.BlockSpec((tm, D), lambda i, j: (i, j)),
                   pl.BlockSpec((tm, D), lambda i, j: (i, j)),
                   pl.BlockSpec((tm, D), row), pl.BlockSpec((tm, D), row),
                   pl.BlockSpec((8, D), fixed)],
        out_shape=[jax.ShapeDtypeStruct((t, D), BF16), jax.ShapeDtypeStruct((t, D_FF), F32),
                   jax.ShapeDtypeStruct((t, D_FF), BF16), jax.ShapeDtypeStruct((t, D), BF16),
                   jax.ShapeDtypeStruct((t, D), F32), jax.ShapeDtypeStruct((8, D), F32)],
        scratch_shapes=[pltpu.VMEM((tm, D), F32)],
        compiler_params=_params(2),
    )(h1, g2pre, w1, w2, g2post, target)


def _ffn_bwd(df2, f1, w1, w2, h1, dout, y, g2pre, g1post, tm):
    t = h1.shape[0]
    n_j = D_FF // D

    def body(df2_ref, f1_ref, w1_ref, w2_ref, h1_ref, dout_ref, y_ref, g2_ref, g1p_ref,
             df1_ref, dh1_ref, dy_ref, vec_ref, acc_ref):
        i, j = pl.program_id(0), pl.program_id(1)

        @pl.when((i == 0) & (j == 0))
        def _():
            vec_ref[...] = jnp.zeros_like(vec_ref)

        @pl.when(j == 0)
        def _():
            acc_ref[...] = jnp.zeros_like(acc_ref)

        df = _dot_nt(df2_ref[...], w2_ref[...])
        df1 = (df * (2.0 * jnp.maximum(f1_ref[...], 0.0))).astype(BF16)
        df1_ref[...] = df1
        acc_ref[...] += _dot_nt(df1, w1_ref[0])

        @pl.when(j == n_j - 1)
        def _():
            dhn = acc_ref[...]
            h = h1_ref[...]
            r2 = lax.rsqrt(_mean(h * h) + EPS)
            nh = h * r2
            vec_ref[0:1, :] += _colsum(dhn * nh)
            dh1 = dout_ref[...] + _rms_bwd(dhn * g2_ref[...], nh, r2)
            dh1_ref[...] = dh1
            yv = y_ref[...]
            r1 = lax.rsqrt(_mean(yv * yv) + EPS)
            ny = yv * r1
            vec_ref[1:2, :] += _colsum(dh1 * ny)
            dy_ref[...] = _rms_bwd(dh1 * g1p_ref[...], ny, r1).astype(BF16)

    row = lambda i, j: (i, 0)
    fixed = lambda i, j: (0, 0)
    vec = pl.BlockSpec((1, D), fixed)
    return pl.pallas_call(
        body, name="ffn_bwd",
        grid=(t // tm, n_j),
        in_specs=[pl.BlockSpec((tm, D), row),
                  pl.BlockSpec((tm, D), lambda i, j: (i, j)),
                  pl.BlockSpec((1, D, D), lambda i, j: (j, 0, 0)),
                  pl.BlockSpec((D, D), lambda i, j: (j, 0)),
                  pl.BlockSpec((tm, D), row), pl.BlockSpec((tm, D), row), pl.BlockSpec((tm, D), row), vec, vec],
        out_specs=[pl.BlockSpec((tm, D), lambda i, j: (i, j)),
                   pl.BlockSpec((tm, D), row), pl.BlockSpec((tm, D), row),
                   pl.BlockSpec((8, D), fixed)],
        out_shape=[jax.ShapeDtypeStruct((t, D_FF), BF16), jax.ShapeDtypeStruct((t, D), F32),
                   jax.ShapeDtypeStruct((t, D), BF16), jax.ShapeDtypeStruct((8, D), F32)],
        scratch_shapes=[pltpu.VMEM((tm, D), F32)],
        compiler_params=_params(2),
    )(df2, f1, w1, w2, h1, dout, y, g2pre, g1post)


def _mix_bwd(z, pooled, dy, w_pool, pool_scale, ln_g, ln_b, ws, ws_t, bsp, w_proj, w_out, mask, tm):
    t = z.shape[0]
    n_t = t // tm

    def body(z_ref, pooled_ref, dy_ref, wp_ref, psc_ref, lng_ref, lnb_ref, ws_ref, wst_ref, bsp_ref, wpr_ref,
             wo_ref, mask_ref,
             dz_ref, dwp_ref, dwpr_ref, dws_ref, dbsp_ref, vec_ref, dbin_ref,
             win_ref, sv_ref, dsv_ref, dv_ref, dsvsum_ref):
        i = pl.program_id(0)

        @pl.when(i == 0)
        def _():
            win_ref[tm:tm + POOL_HALO, :] = jnp.zeros((POOL_HALO, D), F32)
            dwp_ref[...] = jnp.zeros_like(dwp_ref)
            dwpr_ref[...] = jnp.zeros_like(dwpr_ref)
            dws_ref[...] = jnp.zeros_like(dws_ref)
            vec_ref[...] = jnp.zeros_like(vec_ref)
            dbin_ref[...] = jnp.zeros_like(dbin_ref)
            dsvsum_ref[...] = jnp.zeros_like(dsvsum_ref)

        dmerged = _dot_nt(dy_ref[...], wo_ref[...])

        u, du_dz, dgv_dz, vhat, rstd, vb = _sgu_forward(z_ref, lng_ref, lnb_ref, ws_ref, bsp_ref, sv_ref, tm, True)
        sv = sv_ref[...]
        gated = (u * sv).astype(BF16)

        sa = jax.nn.sigmoid(z_ref[:, 3 * D:4 * D])
        sb = jax.nn.sigmoid(z_ref[:, 4 * D:5 * D])
        da = dmerged * sa
        dbbr = (dmerged * sb).astype(BF16)

        tile = n_t - 1 - i
        pos1 = lax.broadcasted_iota(jnp.int32, (tm, 1), 0) + (tile * tm + 1)
        dzga_parts, dpooled_parts = [], []
        for g, w in enumerate(POOL_WINDOWS):
            cols = slice(g * POOL_GROUP, (g + 1) * POOL_GROUP)
            pg = pooled_ref[:, cols]
            a_pre = _dot(pg, wp_ref[g])
            da_g = da[:, cols]
            psc = psc_ref[:, cols]
            vec_ref[0:1, cols] += _colsum(da_g * a_pre)
            sa_g = sa[:, cols]
            dzga_parts.append(dmerged[:, cols] * (a_pre * psc) * (sa_g * (1.0 - sa_g)))
            da_pre = (da_g * psc).astype(BF16)
            dwp_ref[g] += _dot_tn(pg, da_pre)
            dpooled = _dot_nt(da_pre, wp_ref[g])
            dpooled_parts.append(dpooled)
            inv_cnt = 1.0 / jnp.minimum(pos1, w).astype(F32)
            win_ref[0:tm, cols] = dpooled * inv_cnt
        dzp_parts = []
        for g, w in enumerate(POOL_WINDOWS):
            cols = slice(g * POOL_GROUP, (g + 1) * POOL_GROUP)
            acc = win_ref[0:tm, cols]
            for k in range(1, w):
                acc = acc + win_ref[k:k + tm, cols]
            dzp_parts.append(acc - dpooled_parts[g])
        win_ref[tm:tm + POOL_HALO, :] = win_ref[0:POOL_HALO, :]

        dzgb_parts = []
        for h in range(N_HEADS):
            cols = slice(h * HEAD, (h + 1) * HEAD)
            g_h = gated[:, cols]
            db_h = dbbr[:, cols]
            bbr_h = _dot(g_h, wpr_ref[h])
            sb_h = sb[:, cols]
            dzgb_parts.append(dmerged[:, cols] * bbr_h * (sb_h * (1.0 - sb_h)))
            dwpr_ref[h] += _dot_tn(g_h, db_h)
            dgated_h = _dot_nt(db_h, wpr_ref[h])
            dsv_ref[:, cols] = dgated_h * u[:, cols]
            sv_ref[:, cols] = dgated_h * sv[:, cols]
        dzu = sv_ref[...] * du_dz
        dsv = dsv_ref[...]
        dsvb = dsv.astype(BF16)
        blk_sum = dsv[0:SGU_BLOCK, :]
        for b in range(1, tm // SGU_BLOCK):
            blk_sum = blk_sum + dsv[b * SGU_BLOCK:(b + 1) * SGU_BLOCK, :]
        dsvsum_ref[...] += blk_sum
        for b in range(tm // SGU_BLOCK):
            rows = slice(b * SGU_BLOCK, (b + 1) * SGU_BLOCK)
            for h in range(N_HEADS):
                cols = slice(h * HEAD, (h + 1) * HEAD)
                dws_ref[h] += _dot_nt(dsvb[rows, cols], vb[rows, cols])
                dv_ref[rows, cols] = _dot(wst_ref[h], dsvb[rows, cols])
        dv = dv_ref[...]
        vec_ref[1:2, :] += _colsum(dv * vhat)
        vec_ref[2:3, :] += _colsum(dv)
        dvhat = dv * lng_ref[...]
        dgv = rstd * (dvhat - _mean(dvhat) - vhat * _mean(dvhat * vhat))
        dzv = dgv * dgv_dz

        parts = [jnp.concatenate(dzp_parts, axis=1), dzu, dzv,
                 jnp.concatenate(dzga_parts, axis=1), jnp.concatenate(dzgb_parts, axis=1)]
        for k, part in enumerate(parts):
            dz_ref[:, k * D:(k + 1) * D] = part.astype(BF16)
            dbin_ref[0:1, k * D:(k + 1) * D] += _colsum(part)

        @pl.when(i == n_t - 1)
        def _():
            for h in range(N_HEADS):
                dws_ref[h] = jnp.where(mask_ref[...] > 0.0, dws_ref[h], 0.0)
                tot = jnp.sum(dsvsum_ref[:, h * HEAD:(h + 1) * HEAD], axis=1, keepdims=True)
                dbsp_ref[:, h * LANES:(h + 1) * LANES] = jnp.broadcast_to(tot, (SGU_BLOCK, LANES))

    row = lambda i: (n_t - 1 - i, 0)
    fixed2 = lambda i: (0, 0)
    fixed3 = lambda i: (0, 0, 0)
    vec = pl.BlockSpec((1, D), fixed2)
    sq = pl.BlockSpec((N_HEADS, SGU_BLOCK, SGU_BLOCK), fixed3)
    grp = pl.BlockSpec((N_HEADS, HEAD, HEAD), fixed3)
    return pl.pallas_call(
        body, name="mix_bwd",
        grid=(n_t,),
        in_specs=[pl.BlockSpec((tm, D_IN), row), pl.BlockSpec((tm, D), row), pl.BlockSpec((tm, D), row),
                  grp, vec, vec, vec, sq, sq, pl.BlockSpec((N_HEADS, SGU_BLOCK, 1), fixed3), grp,
                  pl.BlockSpec((D, D), fixed2), pl.BlockSpec((SGU_BLOCK, SGU_BLOCK), fixed2)],
        out_specs=[pl.BlockSpec((tm, D_IN), row), grp, grp, sq,
                   pl.BlockSpec((SGU_BLOCK, N_HEADS * LANES), fixed2),
                   pl.BlockSpec((8, D), fixed2), pl.BlockSpec((8, D_IN), fixed2)],
        out_shape=[jax.ShapeDtypeStruct((t, D_IN), BF16),
                   jax.ShapeDtypeStruct((N_HEADS, HEAD, HEAD), F32), jax.ShapeDtypeStruct((N_HEADS, HEAD, HEAD), F32),
                   jax.ShapeDtypeStruct((N_HEADS, SGU_BLOCK, SGU_BLOCK), F32),
                   jax.ShapeDtypeStruct((SGU_BLOCK, N_HEADS * LANES), F32),
                   jax.ShapeDtypeStruct((8, D), F32), jax.ShapeDtypeStruct((8, D_IN), F32)],
        scratch_shapes=[pltpu.VMEM((tm + POOL_HALO, D), F32), pltpu.VMEM((tm, D), F32), pltpu.VMEM((tm, D), F32),
                        pltpu.VMEM((tm, D), F32), pltpu.VMEM((SGU_BLOCK, D), F32)],
        compiler_params=_params(1),
    )(z, pooled, dy, w_pool, pool_scale, ln_g, ln_b, ws, ws_t, bsp, w_proj, w_out, mask)


def _dx(dz, w_in4, x, dh1, g1, tm):
    t = x.shape[0]
    cw = D_IN // N_SHARD

    def body(dz_ref, w_ref, x_ref, dh1_ref, g_ref, gx_ref, vec_ref, acc_ref):
        i, s = pl.program_id(0), pl.program_id(1)

        @pl.when((i == 0) & (s == 0))
        def _():
            vec_ref[...] = jnp.zeros_like(vec_ref)

        @pl.when(s == 0)
        def _():
            acc_ref[...] = jnp.zeros_like(acc_ref)

        acc_ref[...] += _dot_nt(dz_ref[...], w_ref[0])

        @pl.when(s == N_SHARD - 1)
        def _():
            dxn = acc_ref[...]
            xv = x_ref[...]
            r = lax.rsqrt(_mean(xv * xv) + EPS)
            xh = xv * r
            vec_ref[0:1, :] += _colsum(dxn * xh)
            gx_ref[...] = dh1_ref[...] + _rms_bwd(dxn * g_ref[...], xh, r)

    row = lambda i, s: (i, 0)
    fixed = lambda i, s: (0, 0)
    return pl.pallas_call(
        body, name="dx",
        grid=(t // tm, N_SHARD),
        in_specs=[pl.BlockSpec((tm, cw), lambda i, s: (i, s)),
                  pl.BlockSpec((1, D, cw), lambda i, s: (s, 0, 0)),
                  pl.BlockSpec((tm, D), row), pl.BlockSpec((tm, D), row), pl.BlockSpec((1, D), fixed)],
        out_specs=[pl.BlockSpec((tm, D), row), pl.BlockSpec((8, D), fixed)],
        out_shape=[jax.ShapeDtypeStruct((t, D), F32), jax.ShapeDtypeStruct((8, D), F32)],
        scratch_shapes=[pltpu.VMEM((tm, D), F32)],
        compiler_params=_params(2),
    )(dz, w_in4, x, dh1, g1)


def _dw(a, b, name, tm, tn, tk, out_dims, out_index):
    t, m = a.shape
    n = b.shape[1]
    n_k = t // tk

    def body(a_ref, b_ref, o_ref, acc_ref):
        k = pl.program_id(2)

        @pl.when(k == 0)
        def _():
            acc_ref[...] = jnp.zeros_like(acc_ref)

        acc_ref[...] += _dot_tn(a_ref[...], b_ref[...])

        @pl.when(k == n_k - 1)
        def _():
            o_ref[0, 0] = acc_ref[...].astype(BF16)

    return pl.pallas_call(
        body, name=name,
        grid=(m // tm, n // tn, n_k),
        in_specs=[pl.BlockSpec((tk, tm), lambda i, j, k: (k, i)),
                  pl.BlockSpec((tk, tn), lambda i, j, k: (k, j))],
        out_specs=pl.BlockSpec((1, 1, tm, tn), lambda i, j, k: (*out_index(i, j), 0, 0)),
        out_shape=jax.ShapeDtypeStruct((*out_dims, tm, tn), BF16),
        scratch_shapes=[pltpu.VMEM((tm, tn), F32)],
        compiler_params=_params(3),
    )(a, b)


def _row_block(rows, cols, itemsize, n_bufs):
    budget = VMEM_LIMIT // 4
    rb = rows
    while rb % 16 == 0 and rb * cols * itemsize * n_bufs * 2 > budget:
        rb //= 2
    return rb


def _chip_sum(core, g, recv, name):
    _, n_s, rh, cols = g.shape
    rb = _row_block(rh, cols, 4, 3)

    def body(c_ref, g_ref, r_ref, o_ref):
        o_ref[...] = (g_ref[0].astype(F32) + r_ref[...].astype(F32)).astype(BF16)

    return pl.pallas_call(
        body, name=name,
        grid_spec=pltpu.PrefetchScalarGridSpec(
            num_scalar_prefetch=1, grid=(n_s, rh // rb),
            in_specs=[pl.BlockSpec((1, 1, rb, cols), lambda s, r, c: (c[0], s, r, 0)),
                      pl.BlockSpec((1, rb, cols), lambda s, r, c: (s, r, 0))],
            out_specs=pl.BlockSpec((1, rb, cols), lambda s, r, c: (s, r, 0))),
        out_shape=jax.ShapeDtypeStruct((n_s, rh, cols), BF16),
        compiler_params=_params(2),
    )(core, g, recv)


def _slot_sum(slots, name):
    n_s, rows, cols = slots.shape
    rb = _row_block(rows, cols, 4, n_s + 1)

    def body(s_ref, o_ref):
        acc = s_ref[0].astype(F32)
        for k in range(1, n_s):
            acc = acc + s_ref[k].astype(F32)
        o_ref[...] = acc

    return pl.pallas_call(
        body, name=name,
        grid=(rows // rb,),
        in_specs=[pl.BlockSpec((n_s, rb, cols), lambda r: (0, r, 0))],
        out_specs=pl.BlockSpec((rb, cols), lambda r: (r, 0)),
        out_shape=jax.ShapeDtypeStruct((rows, cols), F32),
        compiler_params=_params(1),
    )(slots)


def _adamw(w, g, m, v, name):
    rows, cols = w.shape
    rb = _row_block(rows, cols, 4, 7)
    c1 = 1.0 - ADAM_B1 ** ADAM_STEP
    c2 = 1.0 - ADAM_B2 ** ADAM_STEP

    def body(w_ref, g_ref, m_ref, v_ref, d_ref, mo_ref, vo_ref):
        gv = g_ref[...]
        m_new = ADAM_B1 * m_ref[...] + (1.0 - ADAM_B1) * gv
        v_new = ADAM_B2 * v_ref[...] + (1.0 - ADAM_B2) * (gv * gv)
        mo_ref[...] = m_new
        vo_ref[...] = v_new
        d_ref[...] = -ADAM_LR * ((m_new / c1) / (jnp.sqrt(v_new / c2) + ADAM_EPS) + ADAM_WD * w_ref[...])

    blk = pl.BlockSpec((rb, cols), lambda r: (r, 0))
    return pl.pallas_call(
        body, name=name,
        grid=(rows // rb,),
        in_specs=[blk] * 4, out_specs=[blk] * 3,
        out_shape=[jax.ShapeDtypeStruct((rows, cols), F32)] * 3,
        compiler_params=_params(1),
    )(w, g, m, v)


def _place():
    x, y, c = lax.axis_index("x"), lax.axis_index("y"), lax.axis_index("c")
    chips = [(1 - x, y), (x, 1 - y), (1 - x, 1 - y)]
    return x, y, c, chips


def _remote(src, dst, send_sem, recv_sem, device):
    return pltpu.make_async_remote_copy(src_ref=src, dst_ref=dst, send_sem=send_sem, recv_sem=recv_sem,
                                        device_id=device, device_id_type=MESH)


def _all_gather_weights(shards):
    n = len(shards)

    def body(*refs):
        ins, outs = refs[:n], refs[n:2 * n]
        send_ici, recv_ici, send_d2d, recv_d2d, local_sem = refs[2 * n:]
        x, y, c, chips = _place()
        me = 2 * x + y
        sibling = (x, y, 1 - c)
        local = [pltpu.make_async_copy(ins[w], outs[w].at[me], local_sem.at[w]) for w in range(n)]
        for cp in local:
            cp.start()
        sends = []
        for j, (px, py) in enumerate(chips):
            for w in range(n):
                cp = _remote(ins[w].at[c], outs[w].at[me, c], send_ici.at[w, j], recv_ici.at[w, j], (px, py, c))
                cp.start()
                sends.append(cp)
        for j, (px, py) in enumerate(chips):
            src_chip = 2 * px + py
            for w in range(n):
                landed = outs[w].at[src_chip, c]
                _remote(landed, landed, send_ici.at[w, j], recv_ici.at[w, j], (px, py, c)).wait_recv()
                cp = _remote(landed, landed, send_d2d.at[w, j], recv_d2d.at[w, j], sibling)
                cp.start()
                sends.append(cp)
        for j, (px, py) in enumerate(chips):
            src_chip = 2 * px + py
            for w in range(n):
                passed = outs[w].at[src_chip, 1 - c]
                _remote(passed, passed, send_d2d.at[w, j], recv_d2d.at[w, j], sibling).wait_recv()
        for cp in sends:
            cp.wait_send()
        for cp in local:
            cp.wait()

    return pl.pallas_call(
        body, name="ag_weights",
        in_specs=[ANY] * n, out_specs=[ANY] * n,
        out_shape=[jax.ShapeDtypeStruct((N_SHARD,) + s.shape, s.dtype) for s in shards],
        scratch_shapes=[pltpu.SemaphoreType.DMA((n, N_CHIP_PEERS))] * 4 + [pltpu.SemaphoreType.DMA((n,))],
    )(*shards)


def _swap_halves(grads, small):
    n = len(grads)
    arrays = list(grads) + [small]

    def body(*refs):
        ins, outs = refs[:n + 1], refs[n + 1:2 * n + 2]
        send_sem, recv_sem = refs[2 * n + 2:]
        x, y, c, _ = _place()
        sibling = (x, y, 1 - c)
        copies = []
        for w in range(n + 1):
            src = ins[w].at[1 - c] if w < n else ins[w]
            cp = _remote(src, outs[w], send_sem.at[w], recv_sem.at[w], sibling)
            cp.start()
            copies.append(cp)
        for cp in copies:
            cp.wait()

    return pl.pallas_call(
        body, name="rs_swap_halves",
        in_specs=[ANY] * (n + 1), out_specs=[ANY] * (n + 1),
        out_shape=[jax.ShapeDtypeStruct(g.shape[1:], g.dtype) for g in grads]
        + [jax.ShapeDtypeStruct(small.shape, small.dtype)],
        scratch_shapes=[pltpu.SemaphoreType.DMA((n + 1,))] * 2,
    )(*arrays)


def _exchange_chip_sums(sums, small):
    n = len(sums)
    arrays = list(sums) + [small]

    def body(*refs):
        ins, outs = refs[:n + 1], refs[n + 1:2 * n + 2]
        send_sem, recv_sem, local_sem = refs[2 * n + 2:]
        x, y, c, chips = _place()
        me = 2 * x + y
        local = []
        for w in range(n + 1):
            src = ins[w].at[me] if w < n else ins[w]
            cp = pltpu.make_async_copy(src, outs[w].at[me], local_sem.at[w])
            cp.start()
            local.append(cp)
        sends = []
        for j, (px, py) in enumerate(chips):
            for w in range(n + 1):
                src = ins[w].at[2 * px + py] if w < n else ins[w]
                cp = _remote(src, outs[w].at[me], send_sem.at[w, j], recv_sem.at[w, j], (px, py, c))
                cp.start()
                sends.append(cp)
        for j, (px, py) in enumerate(chips):
            for w in range(n + 1):
                slot = outs[w].at[2 * px + py]
                _remote(slot, slot, send_sem.at[w, j], recv_sem.at[w, j], (px, py, c)).wait_recv()
        for cp in sends:
            cp.wait_send()
        for cp in local:
            cp.wait()

    return pl.pallas_call(
        body, name="rs_exchange_chip_sums",
        in_specs=[ANY] * (n + 1), out_specs=[ANY] * (n + 1),
        out_shape=[jax.ShapeDtypeStruct(s.shape, s.dtype) for s in sums]
        + [jax.ShapeDtypeStruct((N_SHARD,) + small.shape, small.dtype)],
        scratch_shapes=[pltpu.SemaphoreType.DMA((n + 1, N_CHIP_PEERS))] * 2 + [pltpu.SemaphoreType.DMA((n + 1,))],
    )(*arrays)


def _share_halves(halves):
    n = len(halves)

    def body(*refs):
        ins, outs = refs[:n], refs[n:2 * n]
        send_sem, recv_sem, local_sem = refs[2 * n:]
        x, y, c, _ = _place()
        sibling = (x, y, 1 - c)
        copies = []
        for w in range(n):
            cp = pltpu.make_async_copy(ins[w], outs[w].at[c], local_sem.at[w])
            cp.start()
            copies.append(cp)
        sends = []
        for w in range(n):
            cp = _remote(ins[w], outs[w].at[c], send_sem.at[w], recv_sem.at[w], sibling)
            cp.start()
            sends.append(cp)
        for w in range(n):
            got = outs[w].at[1 - c]
            _remote(got, got, send_sem.at[w], recv_sem.at[w], sibling).wait_recv()
        for cp in sends:
            cp.wait_send()
        for cp in copies:
            cp.wait()

    return pl.pallas_call(
        body, name="rs_share_halves",
        in_specs=[ANY] * n, out_specs=[ANY] * n,
        out_shape=[jax.ShapeDtypeStruct((2,) + h.shape, h.dtype) for h in halves],
        scratch_shapes=[pltpu.SemaphoreType.DMA((n,))] * 3,
    )(*halves)


_SMALL = ("norm1_pre_g", "b_in", "pool_scale", "sgu_ln_g", "sgu_ln_b", "w_spatial", "b_spatial",
          "norm1_post_g", "norm2_pre_g", "norm2_post_g")
_BIG = ("w_in", "w_pool", "w_sgu_proj", "w_out", "w_ff1", "w_ff2")
_ORDER = ("norm1_pre_g", "w_in", "b_in", "w_pool", "pool_scale", "sgu_ln_g", "sgu_ln_b", "w_spatial", "b_spatial",
          "w_sgu_proj", "w_out", "norm1_post_g", "norm2_pre_g", "w_ff1", "w_ff2", "norm2_post_g")


def _pack_small(parts):
    flat = jnp.concatenate([parts[k].reshape(-1).astype(F32) for k in _SMALL])
    flat = jnp.pad(flat, (0, SMALL_ROWS * LANES - flat.shape[0]))
    return flat.reshape(SMALL_ROWS, LANES)


def _unpack_small(packed, like):
    flat = packed.reshape(-1)
    out, off = {}, 0
    for k in _SMALL:
        size = like[k].size
        out[k] = flat[off:off + size].reshape(like[k].shape)
        off += size
    return out


def _halves(a):
    return a.reshape(2, a.shape[0] // 2, a.shape[1])


def _step(p, m, v, x, target):
    t = x.shape[0]
    tm_mm = min(1024, t)
    tm_mix = min(256, t)
    tm_ffn = min(512, t)
    core = lax.axis_index("c").astype(jnp.int32).reshape(1)
    row = lambda a: a.reshape(1, -1)

    local2d = {"w_in": p["w_in"], "w_ff1": p["w_ff1"], "w_ff2": p["w_ff2"], "w_out": p["w_out"],
               "w_pool": p["w_pool"].reshape(N_HEADS * 64, POOL_GROUP),
               "w_sgu_proj": p["w_sgu_proj"].reshape(N_HEADS * 64, HEAD)}
    full = dict(zip(_BIG, _all_gather_weights([_halves(local2d[k].astype(BF16)) for k in _BIG])))
    w_in4 = full["w_in"].reshape(N_SHARD, D, D_IN // N_SHARD)
    w_ff1_4 = full["w_ff1"].reshape(N_SHARD, D, D)
    w_ff2 = full["w_ff2"].reshape(D_FF, D)
    w_out = full["w_out"].reshape(D, D)
    regroup = lambda a: a.reshape(N_SHARD, N_HEADS, 64, 256).transpose(1, 0, 2, 3).reshape(N_HEADS, 256, 256)
    w_pool = regroup(full["w_pool"])
    w_proj = regroup(full["w_sgu_proj"])

    pos = jnp.arange(SGU_BLOCK) // 64
    mask = (pos[:, None] >= pos[None, :]).astype(F32)
    ws = (p["w_spatial"] * mask[None]).astype(BF16)
    ws_t = jnp.swapaxes(ws, 1, 2)
    bsp = p["b_spatial"].reshape(N_HEADS, SGU_BLOCK, 1)

    z, xn = _inproj(x, row(p["norm1_pre_g"]), w_in4, row(p["b_in"]), tm_mm)
    pooled, merged, y, h1 = _mix_fwd(z, x, w_pool, row(p["pool_scale"]), row(p["sgu_ln_g"]), row(p["sgu_ln_b"]),
                                     ws, bsp, w_proj, w_out, row(p["norm1_post_g"]), tm_mix)
    hn, f1, f, df2, dout, vec_a = _ffn_fwd(h1, row(p["norm2_pre_g"]), w_ff1_4, w_ff2, row(p["norm2_post_g"]),
                                           target, tm_ffn)
    df1, dh1, dy, vec_b = _ffn_bwd(df2, f1, w_ff1_4, w_ff2, h1, dout, y, row(p["norm2_pre_g"]),
                                   row(p["norm1_post_g"]), tm_ffn)
    dz, dwp, dwpr, dws, dbsp, vec_c, dbin = _mix_bwd(z, pooled, dy, w_pool, row(p["pool_scale"]), row(p["sgu_ln_g"]),
                                                     row(p["sgu_ln_b"]), ws, ws_t, bsp, w_proj, w_out, mask, tm_mix)
    grad_x, vec_d = _dx(dz, w_in4, x, dh1, row(p["norm1_pre_g"]), tm_mm)

    tk = min(1024, t)
    g_big = {
        "w_in": _dw(xn, dz, "dw_in", D // 2, D_IN // N_SHARD, tk, (2, N_SHARD), lambda i, j: (i, j)),
        "w_ff1": _dw(hn, df1, "dw_ff1", D // 2, D, tk, (2, N_SHARD), lambda i, j: (i, j)),
        "w_ff2": _dw(f, df2, "dw_ff2", D // 2, D, tk, (2, N_SHARD), lambda i, j: (i % 2, i // 2)),
    }
    dw_out = _dw(merged, dy, "dw_out", D, D, tk, (1, 1), lambda i, j: (i, j)).reshape(D, D)
    by_half = lambda a, rows: a.reshape(N_SHARD, 2, rows // 2, a.shape[-1]).transpose(1, 0, 2, 3)
    g_big["w_out"] = by_half(dw_out, D // N_SHARD)
    shard_major = lambda a: a.reshape(N_HEADS, N_SHARD, 64, 256).transpose(1, 0, 2, 3).reshape(N_SHARD * 256, 256)
    g_big["w_pool"] = by_half(shard_major(dwp).astype(BF16), 256)
    g_big["w_sgu_proj"] = by_half(shard_major(dwpr).astype(BF16), 256)

    small_part = _pack_small({
        "norm1_pre_g": vec_d[0], "b_in": dbin[0], "pool_scale": vec_c[0], "sgu_ln_g": vec_c[1], "sgu_ln_b": vec_c[2],
        "w_spatial": dws, "b_spatial": dbsp[:, ::LANES].T, "norm1_post_g": vec_b[1], "norm2_pre_g": vec_b[0],
        "norm2_post_g": vec_a[0]})
    loss = lax.psum(vec_a[1, 0], ("x", "y", "c"))

    swapped = _swap_halves([g_big[k] for k in _BIG], small_part)
    sums = [_chip_sum(core, g_big[k], r, "chip_sum_" + k) for k, r in zip(_BIG, swapped[:-1])]
    small_chip = _slot_sum(jnp.stack([small_part, swapped[-1]]), "chip_sum_small")
    slots = _exchange_chip_sums(sums, small_chip)
    halves = [_slot_sum(s, "final_sum_" + k) for k, s in zip(_BIG, slots[:-1])]
    small_grad = _slot_sum(slots[-1], "final_sum_small")
    shared = _share_halves(halves)
    grads = {k: s.reshape(local2d[k].shape) for k, s in zip(_BIG, shared)}

    delta, new_m, new_v = {}, {}, {}
    for k in _BIG:
        as2d = lambda a: a.reshape(local2d[k].shape)
        d_k, m_k, v_k = _adamw(local2d[k], grads[k], as2d(m[k]), as2d(v[k]), "adamw_" + k)
        delta[k], new_m[k], new_v[k] = (a.reshape(p[k].shape) for a in (d_k, m_k, v_k))
        grads[k] = grads[k].reshape(p[k].shape)
    d_s, m_s, v_s = _adamw(_pack_small(p), small_grad, _pack_small(m), _pack_small(v), "adamw_small")
    grads.update(_unpack_small(small_grad, p))
    delta.update(_unpack_small(d_s, p))
    new_m.update(_unpack_small(m_s, p))
    new_v.update(_unpack_small(v_s, p))

    return (loss, grad_x, *[grads[k] for k in _ORDER], *[delta[k] for k in _ORDER],
            *[new_m[k] for k in _ORDER], *[new_v[k] for k in _ORDER])


def kernel(x, norm1_pre_g, w_in, b_in, w_pool, pool_scale, sgu_ln_g, sgu_ln_b, w_spatial, b_spatial, w_sgu_proj, w_out, norm1_post_g, norm2_pre_g, w_ff1, w_ff2, norm2_post_g, loss_target, m_norm1_pre_g, m_w_in, m_b_in, m_w_pool, m_pool_scale, m_sgu_ln_g, m_sgu_ln_b, m_w_spatial, m_b_spatial, m_w_sgu_proj, m_w_out, m_norm1_post_g, m_norm2_pre_g, m_w_ff1, m_w_ff2, m_norm2_post_g, v_norm1_pre_g, v_w_in, v_b_in, v_w_pool, v_pool_scale, v_sgu_ln_g, v_sgu_ln_b, v_w_spatial, v_b_spatial, v_w_sgu_proj, v_w_out, v_norm1_post_g, v_norm2_pre_g, v_w_ff1, v_w_ff2, v_norm2_post_g):
    p = dict(zip(_ORDER, (norm1_pre_g, w_in, b_in, w_pool, pool_scale, sgu_ln_g, sgu_ln_b, w_spatial, b_spatial,
                          w_sgu_proj, w_out, norm1_post_g, norm2_pre_g, w_ff1, w_ff2, norm2_post_g)))
    m = dict(zip(_ORDER, (m_norm1_pre_g, m_w_in, m_b_in, m_w_pool, m_pool_scale, m_sgu_ln_g, m_sgu_ln_b, m_w_spatial,
                          m_b_spatial, m_w_sgu_proj, m_w_out, m_norm1_post_g, m_norm2_pre_g, m_w_ff1, m_w_ff2,
                          m_norm2_post_g)))
    v = dict(zip(_ORDER, (v_norm1_pre_g, v_w_in, v_b_in, v_w_pool, v_pool_scale, v_sgu_ln_g, v_sgu_ln_b, v_w_spatial,
                          v_b_spatial, v_w_sgu_proj, v_w_out, v_norm1_post_g, v_norm2_pre_g, v_w_ff1, v_w_ff2,
                          v_norm2_post_g)))
    outs = _step(p, m, v, x[0], loss_target[0])
    return (outs[0], outs[1][None], *outs[2:])
```

```python
import math

import jax
import jax.numpy as jnp
from jax import lax
from jax.experimental import pallas as pl
from jax.experimental.pallas import tpu as pltpu

F32 = jnp.float32
BF16 = jnp.bfloat16

D = 1024
D_IN = 5 * D
D_FF = 4 * D
N_SHARD = 4
N_CHIP_PEERS = 3
POOL_WINDOWS = (2, 4, 8, 16)
POOL_GROUP = 256
POOL_HALO = 16
SGU_BLOCK = 128
N_HEADS = 4
HEAD = 256
EPS = 1e-6

ADAM_LR = 0.001
ADAM_B1 = 0.9
ADAM_B2 = 0.999
ADAM_EPS = 1e-08
ADAM_WD = 0.01
ADAM_STEP = 10

V7X_VMEM_BYTES = 64 * 1024 * 1024
VMEM_LIMIT = V7X_VMEM_BYTES - 8 * 1024 * 1024
SMALL_ROWS = 616
LANES = 128

MESH = pl.DeviceIdType.MESH
ANY = pl.BlockSpec(memory_space=pl.ANY)
HBM = pl.BlockSpec(memory_space=pltpu.HBM)
SEM = pl.BlockSpec(memory_space=pltpu.SEMAPHORE)
EFFECT = pltpu.SideEffectType.DATAFLOW_SIDE_EFFECTING

NT_DIMS = (((1,), (1,)), ((), ()))
TN_DIMS = (((0,), (0,)), ((), ()))


def _params(n_axes):
    return pltpu.CompilerParams(dimension_semantics=("arbitrary",) * n_axes, vmem_limit_bytes=VMEM_LIMIT)


def _dot(a, b):
    return jnp.dot(a, b, preferred_element_type=F32)


def _dot_nt(a, b):
    return lax.dot_general(a, b, NT_DIMS, preferred_element_type=F32)


def _dot_tn(a, b):
    return lax.dot_general(a, b, TN_DIMS, preferred_element_type=F32)


def _mean(a):
    return jnp.mean(a, axis=-1, keepdims=True)


def _colsum(a):
    return jnp.sum(a, axis=0, keepdims=True)


_GELU_C0 = math.sqrt(2.0 / math.pi)
_GELU_C1 = 0.044715


def _gelu(a):
    t = jnp.tanh(_GELU_C0 * (a + _GELU_C1 * (a * a * a)))
    return a * (0.5 * (1.0 + t))


def _gelu_and_grad(a):
    a2 = a * a
    t = jnp.tanh(_GELU_C0 * (a + _GELU_C1 * (a2 * a)))
    cdf = 0.5 * (1.0 + t)
    grad = cdf + 0.5 * a * (1.0 - t * t) * (_GELU_C0 * (1.0 + 3.0 * _GELU_C1 * a2))
    return a * cdf, grad


def _rms_bwd(dn, n, r):
    return r * (dn - n * _mean(dn * n))


def _inproj(x, g1, w_in4, b_in, dep, tm):
    t = x.shape[0]
    cw = D_IN // N_SHARD

    def body(x_ref, g_ref, w_ref, b_ref, dep_ref, z_ref, xn_ref):
        @pl.when(pl.program_id(1) == 0)
        def _():
            xv = x_ref[...]
            r = lax.rsqrt(_mean(xv * xv) + EPS)
            xn_ref[...] = (xv * r * g_ref[...]).astype(BF16)

        z_ref[...] = _dot(xn_ref[...], w_ref[0]) + b_ref[...]

    return pl.pallas_call(
        body, name="inproj",
        grid=(t // tm, N_SHARD),
        in_specs=[pl.BlockSpec((tm, D), lambda i, s: (i, 0)),
                  pl.BlockSpec((1, D), lambda i, s: (0, 0)),
                  pl.BlockSpec((1, D, cw), lambda i, s: (s, 0, 0)),
                  pl.BlockSpec((1, cw), lambda i, s: (0, s)),
                  pl.BlockSpec(dep.shape, lambda i, s: (0, 0))],
        out_specs=[pl.BlockSpec((tm, cw), lambda i, s: (i, s)),
                   pl.BlockSpec((tm, D), lambda i, s: (i, 0))],
        out_shape=[jax.ShapeDtypeStruct((t, D_IN), F32), jax.ShapeDtypeStruct((t, D), BF16)],
        compiler_params=_params(2),
    )(x, g1, w_in4, b_in, dep)


def _sgu_forward(z_ref, lng_ref, lnb_ref, ws_ref, bsp_ref, sv_ref, tm, with_grad):
    if with_grad:
        u, du_dz = _gelu_and_grad(z_ref[:, D:2 * D])
        gv, dgv_dz = _gelu_and_grad(z_ref[:, 2 * D:3 * D])
    else:
        u = _gelu(z_ref[:, D:2 * D])
        gv = _gelu(z_ref[:, 2 * D:3 * D])
        du_dz = dgv_dz = None
    xc = gv - _mean(gv)
    rstd = lax.rsqrt(_mean(xc * xc) + EPS)
    vhat = xc * rstd
    vb = (vhat * lng_ref[...] + lnb_ref[...]).astype(BF16)
    for b in range(tm // SGU_BLOCK):
        rows = slice(b * SGU_BLOCK, (b + 1) * SGU_BLOCK)
        for h in range(N_HEADS):
            cols = slice(h * HEAD, (h + 1) * HEAD)
            sv_ref[rows, cols] = _dot(ws_ref[h], vb[rows, cols]) + bsp_ref[h]
    return u, du_dz, dgv_dz, vhat, rstd, vb


def _mix_fwd(z, x, w_pool, pool_scale, ln_g, ln_b, ws, bsp, w_proj, w_out, g1post, tm):
    t = x.shape[0]

    def body(z_ref, x_ref, wp_ref, psc_ref, lng_ref, lnb_ref, ws_ref, bsp_ref, wpr_ref, wo_ref, g1p_ref,
             pooled_ref, merged_ref, y_ref, h1_ref, win_ref, sv_ref):
        i = pl.program_id(0)

        @pl.when(i == 0)
        def _():
            win_ref[0:POOL_HALO, :] = jnp.zeros((POOL_HALO, D), F32)

        win_ref[POOL_HALO:POOL_HALO + tm, :] = z_ref[:, 0:D]
        pos1 = lax.broadcasted_iota(jnp.int32, (tm, 1), 0) + (i * tm + 1)
        a_parts = []
        for g, w in enumerate(POOL_WINDOWS):
            cols = slice(g * POOL_GROUP, (g + 1) * POOL_GROUP)
            ts = win_ref[POOL_HALO:POOL_HALO + tm, cols]
            for k in range(1, w):
                ts = ts + win_ref[POOL_HALO - k:POOL_HALO - k + tm, cols]
            inv_cnt = 1.0 / jnp.minimum(pos1, w).astype(F32)
            pooled = (ts * inv_cnt - z_ref[:, cols]).astype(BF16)
            pooled_ref[:, cols] = pooled
            a_parts.append(_dot(pooled, wp_ref[g]) * psc_ref[:, cols])
        win_ref[0:POOL_HALO, :] = win_ref[tm:tm + POOL_HALO, :]

        u, _, _, _, _, _ = _sgu_forward(z_ref, lng_ref, lnb_ref, ws_ref, bsp_ref, sv_ref, tm, False)
        gated = (u * sv_ref[...]).astype(BF16)
        b_parts = [_dot(gated[:, h * HEAD:(h + 1) * HEAD], wpr_ref[h]) for h in range(N_HEADS)]

        a = jnp.concatenate(a_parts, axis=1)
        bbr = jnp.concatenate(b_parts, axis=1)
        merged = (jax.nn.sigmoid(z_ref[:, 3 * D:4 * D]) * a + jax.nn.sigmoid(z_ref[:, 4 * D:5 * D]) * bbr).astype(BF16)
        merged_ref[...] = merged
        y = _dot(merged, wo_ref[...])
        y_ref[...] = y
        r = lax.rsqrt(_mean(y * y) + EPS)
        h1_ref[...] = x_ref[...] + (y * r) * g1p_ref[...]

    row = lambda i: (i, 0)
    fixed2 = lambda i: (0, 0)
    fixed3 = lambda i: (0, 0, 0)
    vec = pl.BlockSpec((1, D), fixed2)
    return pl.pallas_call(
        body, name="mix_fwd",
        grid=(t // tm,),
        in_specs=[pl.BlockSpec((tm, D_IN), row), pl.BlockSpec((tm, D), row),
                  pl.BlockSpec((N_HEADS, POOL_GROUP, POOL_GROUP), fixed3), vec, vec, vec,
                  pl.BlockSpec((N_HEADS, SGU_BLOCK, SGU_BLOCK), fixed3),
                  pl.BlockSpec((N_HEADS, SGU_BLOCK, 1), fixed3),
                  pl.BlockSpec((N_HEADS, HEAD, HEAD), fixed3),
                  pl.BlockSpec((D, D), fixed2), vec],
        out_specs=[pl.BlockSpec((tm, D), row)] * 4,
        out_shape=[jax.ShapeDtypeStruct((t, D), BF16), jax.ShapeDtypeStruct((t, D), BF16),
                   jax.ShapeDtypeStruct((t, D), F32), jax.ShapeDtypeStruct((t, D), F32)],
        scratch_shapes=[pltpu.VMEM((tm + POOL_HALO, D), F32), pltpu.VMEM((tm, D), F32)],
        compiler_params=_params(1),
    )(z, x, w_pool, pool_scale, ln_g, ln_b, ws, bsp, w_proj, w_out, g1post)


def _ffn_fwd(h1, g2pre, w1, w2, g2post, target, tm):
    t = h1.shape[0]
    n_j = D_FF // D

    def body(h1_ref, g2_ref, w1_ref, w2_ref, g2p_ref, tgt_ref,
             hn_ref, f1_ref, f_ref, df2_ref, dout_ref, vec_ref, acc_ref):
        i, j = pl.program_id(0), pl.program_id(1)

        @pl.when((i == 0) & (j == 0))
        def _():
            vec_ref[...] = jnp.zeros_like(vec_ref)

        @pl.when(j == 0)
        def _():
            h = h1_ref[...]
            r = lax.rsqrt(_mean(h * h) + EPS)
            hn_ref[...] = (h * r * g2_ref[...]).astype(BF16)
            acc_ref[...] = jnp.zeros_like(acc_ref)

        f1 = _dot(hn_ref[...], w1_ref[0])
        f1_ref[...] = f1
        rl = jnp.maximum(f1, 0.0)
        f = (rl * rl).astype(BF16)
        f_ref[...] = f
        acc_ref[...] += _dot(f, w2_ref[...])

        @pl.when(j == n_j - 1)
        def _():
            f2 = acc_ref[...]
            r = lax.rsqrt(_mean(f2 * f2) + EPS)
            n2 = f2 * r
            err = h1_ref[...] + n2 * g2p_ref[...] - tgt_ref[...]
            loss = 0.5 * jnp.sum(_mean(err * err))
            dout = err * (1.0 / D)
            dout_ref[...] = dout
            vec_ref[0:1, :] += _colsum(dout * n2)
            vec_ref[1:2, :] += jnp.full((1, D), loss, F32)
            df2_ref[...] = _rms_bwd(dout * g2p_ref[...], n2, r).astype(BF16)

    row = lambda i, j: (i, 0)
    fixed = lambda i, j: (0, 0)
    vec = pl.BlockSpec((1, D), fixed)
    return pl.pallas_call(
        body, name="ffn_fwd",
        grid=(t // tm, n_j),
        in_specs=[pl.BlockSpec((tm, D), row), vec,
                  pl.BlockSpec((1, D, D), lambda i, j: (j, 0, 0)),
                  pl.BlockSpec((D, D), lambda i, j: (j, 0)), vec,
                  pl.BlockSpec((tm, D), row)],
        out_specs=[pl.BlockSpec((tm, D), row),
                   pl.BlockSpec((tm, D), lambda i, j: (i, j)),
                   pl.BlockSpec((tm, D), lambda i, j: (i, j)),
                   pl.BlockSpec((tm, D), row), pl.BlockSpec((tm, D), row),
                   pl.BlockSpec((8, D), fixed)],
        out_shape=[jax.ShapeDtypeStruct((t, D), BF16), jax.ShapeDtypeStruct((t, D_FF), F32),
                   jax.ShapeDtypeStruct((t, D_FF), BF16), jax.ShapeDtypeStruct((t, D), BF16),
                   jax.ShapeDtypeStruct((t, D), F32), jax.ShapeDtypeStruct((8, D), F32)],
        scratch_shapes=[pltpu.VMEM((tm, D), F32)],
        compiler_params=_params(2),
    )(h1, g2pre, w1, w2, g2post, target)


def _ffn_bwd(df2, f1, w1, w2, h1, dout, y, g2pre, g1post, tm):
    t = h1.shape[0]
    n_j = D_FF // D

    def body(df2_ref, f1_ref, w1_ref, w2_ref, h1_ref, dout_ref, y_ref, g2_ref, g1p_ref,
             df1_ref, dh1_ref, dy_ref, vec_ref, acc_ref):
        i, j = pl.program_id(0), pl.program_id(1)

        @pl.when((i == 0) & (j == 0))
        def _():
            vec_ref[...] = jnp.zeros_like(vec_ref)

        @pl.when(j == 0)
        def _():
            acc_ref[...] = jnp.zeros_like(acc_ref)

        df = _dot_nt(df2_ref[...], w2_ref[...])
        df1 = (df * (2.0 * jnp.maximum(f1_ref[...], 0.0))).astype(BF16)
        df1_ref[...] = df1
        acc_ref[...] += _dot_nt(df1, w1_ref[0])

        @pl.when(j == n_j - 1)
        def _():
            dhn = acc_ref[...]
            h = h1_ref[...]
            r2 = lax.rsqrt(_mean(h * h) + EPS)
            nh = h * r2
            vec_ref[0:1, :] += _colsum(dhn * nh)
            dh1 = dout_ref[...] + _rms_bwd(dhn * g2_ref[...], nh, r2)
            dh1_ref[...] = dh1
            yv = y_ref[...]
            r1 = lax.rsqrt(_mean(yv * yv) + EPS)
            ny = yv * r1
            vec_ref[1:2, :] += _colsum(dh1 * ny)
            dy_ref[...] = _rms_bwd(dh1 * g1p_ref[...], ny, r1).astype(BF16)

    row = lambda i, j: (i, 0)
    fixed = lambda i, j: (0, 0)
    vec = pl.BlockSpec((1, D), fixed)
    return pl.pallas_call(
        body, name="ffn_bwd",
        grid=(t // tm, n_j),
        in_specs=[pl.BlockSpec((tm, D), row),
                  pl.BlockSpec((tm, D), lambda i, j: (i, j)),
                  pl.BlockSpec((1, D, D), lambda i, j: (j, 0, 0)),
                  pl.BlockSpec((D, D), lambda i, j: (j, 0)),
                  pl.BlockSpec((tm, D), row), pl.BlockSpec((tm, D), row), pl.BlockSpec((tm, D), row), vec, vec],
        out_specs=[pl.BlockSpec((tm, D), lambda i, j: (i, j)),
                   pl.BlockSpec((tm, D), row), pl.BlockSpec((tm, D), row),
                   pl.BlockSpec((8, D), fixed)],
        out_shape=[jax.ShapeDtypeStruct((t, D_FF), BF16), jax.ShapeDtypeStruct((t, D), F32),
                   jax.ShapeDtypeStruct((t, D), BF16), jax.ShapeDtypeStruct((8, D), F32)],
        scratch_shapes=[pltpu.VMEM((tm, D), F32)],
        compiler_params=_params(2),
    )(df2, f1, w1, w2, h1, dout, y, g2pre, g1post)


def _mix_bwd(z, pooled, dy, w_pool, pool_scale, ln_g, ln_b, ws, ws_t, bsp, w_proj, w_out, mask, tm):
    t = z.shape[0]
    n_t = t // tm

    def body(z_ref, pooled_ref, dy_ref, wp_ref, psc_ref, lng_ref, lnb_ref, ws_ref, wst_ref, bsp_ref, wpr_ref,
             wo_ref, mask_ref,
             dz_ref, dwp_ref, dwpr_ref, dws_ref, dbsp_ref, vec_ref, dbin_ref,
             win_ref, sv_ref, dsv_ref, dv_ref, dsvsum_ref):
        i = pl.program_id(0)

        @pl.when(i == 0)
        def _():
            win_ref[tm:tm + POOL_HALO, :] = jnp.zeros((POOL_HALO, D), F32)
            dwp_ref[...] = jnp.zeros_like(dwp_ref)
            dwpr_ref[...] = jnp.zeros_like(dwpr_ref)
            dws_ref[...] = jnp.zeros_like(dws_ref)
            vec_ref[...] = jnp.zeros_like(vec_ref)
            dbin_ref[...] = jnp.zeros_like(dbin_ref)
            dsvsum_ref[...] = jnp.zeros_like(dsvsum_ref)

        dmerged = _dot_nt(dy_ref[...], wo_ref[...])

        u, du_dz, dgv_dz, vhat, rstd, vb = _sgu_forward(z_ref, lng_ref, lnb_ref, ws_ref, bsp_ref, sv_ref, tm, True)
        sv = sv_ref[...]
        gated = (u * sv).astype(BF16)

        sa = jax.nn.sigmoid(z_ref[:, 3 * D:4 * D])
        sb = jax.nn.sigmoid(z_ref[:, 4 * D:5 * D])
        da = dmerged * sa
        dbbr = (dmerged * sb).astype(BF16)

        tile = n_t - 1 - i
        pos1 = lax.broadcasted_iota(jnp.int32, (tm, 1), 0) + (tile * tm + 1)
        dzga_parts, dpooled_parts = [], []
        for g, w in enumerate(POOL_WINDOWS):
            cols = slice(g * POOL_GROUP, (g + 1) * POOL_GROUP)
            pg = pooled_ref[:, cols]
            a_pre = _dot(pg, wp_ref[g])
            da_g = da[:, cols]
            psc = psc_ref[:, cols]
            vec_ref[0:1, cols] += _colsum(da_g * a_pre)
            sa_g = sa[:, cols]
            dzga_parts.append(dmerged[:, cols] * (a_pre * psc) * (sa_g * (1.0 - sa_g)))
            da_pre = (da_g * psc).astype(BF16)
            dwp_ref[g] += _dot_tn(pg, da_pre)
            dpooled = _dot_nt(da_pre, wp_ref[g])
            dpooled_parts.append(dpooled)
            inv_cnt = 1.0 / jnp.minimum(pos1, w).astype(F32)
            win_ref[0:tm, cols] = dpooled * inv_cnt
        dzp_parts = []
        for g, w in enumerate(POOL_WINDOWS):
            cols = slice(g * POOL_GROUP, (g + 1) * POOL_GROUP)
            acc = win_ref[0:tm, cols]
            for k in range(1, w):
                acc = acc + win_ref[k:k + tm, cols]
            dzp_parts.append(acc - dpooled_parts[g])
        win_ref[tm:tm + POOL_HALO, :] = win_ref[0:POOL_HALO, :]

        dzgb_parts = []
        for h in range(N_HEADS):
            cols = slice(h * HEAD, (h + 1) * HEAD)
            g_h = gated[:, cols]
            db_h = dbbr[:, cols]
            bbr_h = _dot(g_h, wpr_ref[h])
            sb_h = sb[:, cols]
            dzgb_parts.append(dmerged[:, cols] * bbr_h * (sb_h * (1.0 - sb_h)))
            dwpr_ref[h] += _dot_tn(g_h, db_h)
            dgated_h = _dot_nt(db_h, wpr_ref[h])
            dsv_ref[:, cols] = dgated_h * u[:, cols]
            sv_ref[:, cols] = dgated_h * sv[:, cols]
        dzu = sv_ref[...] * du_dz
        dsv = dsv_ref[...]
        dsvb = dsv.astype(BF16)
        blk_sum = dsv[0:SGU_BLOCK, :]
        for b in range(1, tm // SGU_BLOCK):
            blk_sum = blk_sum + dsv[b * SGU_BLOCK:(b + 1) * SGU_BLOCK, :]
        dsvsum_ref[...] += blk_sum
        for b in range(tm // SGU_BLOCK):
            rows = slice(b * SGU_BLOCK, (b + 1) * SGU_BLOCK)
            for h in range(N_HEADS):
                cols = slice(h * HEAD, (h + 1) * HEAD)
                dws_ref[h] += _dot_nt(dsvb[rows, cols], vb[rows, cols])
                dv_ref[rows, cols] = _dot(wst_ref[h], dsvb[rows, cols])
        dv = dv_ref[...]
        vec_ref[1:2, :] += _colsum(dv * vhat)
        vec_ref[2:3, :] += _colsum(dv)
        dvhat = dv * lng_ref[...]
        dgv = rstd * (dvhat - _mean(dvhat) - vhat * _mean(dvhat * vhat))
        dzv = dgv * dgv_dz

        parts = [jnp.concatenate(dzp_parts, axis=1), dzu, dzv,
                 jnp.concatenate(dzga_parts, axis=1), jnp.concatenate(dzgb_parts, axis=1)]
        for k, part in enumerate(parts):
            dz_ref[:, k * D:(k + 1) * D] = part.astype(BF16)
            dbin_ref[0:1, k * D:(k + 1) * D] += _colsum(part)

        @pl.when(i == n_t - 1)
        def _():
            for h in range(N_HEADS):
                dws_ref[h] = jnp.where(mask_ref[...] > 0.0, dws_ref[h], 0.0)
                tot = jnp.sum(dsvsum_ref[:, h * HEAD:(h + 1) * HEAD], axis=1, keepdims=True)
                dbsp_ref[:, h * LANES:(h + 1) * LANES] = jnp.broadcast_to(tot, (SGU_BLOCK, LANES))

    row = lambda i: (n_t - 1 - i, 0)
    fixed2 = lambda i: (0, 0)
    fixed3 = lambda i: (0, 0, 0)
    vec = pl.BlockSpec((1, D), fixed2)
    sq = pl.BlockSpec((N_HEADS, SGU_BLOCK, SGU_BLOCK), fixed3)
    grp = pl.BlockSpec((N_HEADS, HEAD, HEAD), fixed3)
    return pl.pallas_call(
        body, name="mix_bwd",
        grid=(n_t,),
        in_specs=[pl.BlockSpec((tm, D_IN), row), pl.BlockSpec((tm, D), row), pl.BlockSpec((tm, D), row),
                  grp, vec, vec, vec, sq, sq, pl.BlockSpec((N_HEADS, SGU_BLOCK, 1), fixed3), grp,
                  pl.BlockSpec((D, D), fixed2), pl.BlockSpec((SGU_BLOCK, SGU_BLOCK), fixed2)],
        out_specs=[pl.BlockSpec((tm, D_IN), row), grp, grp, sq,
                   pl.BlockSpec((SGU_BLOCK, N_HEADS * LANES), fixed2),
                   pl.BlockSpec((8, D), fixed2), pl.BlockSpec((8, D_IN), fixed2)],
        out_shape=[jax.ShapeDtypeStruct((t, D_IN), BF16),
                   jax.ShapeDtypeStruct((N_HEADS, HEAD, HEAD), F32), jax.ShapeDtypeStruct((N_HEADS, HEAD, HEAD), F32),
                   jax.ShapeDtypeStruct((N_HEADS, SGU_BLOCK, SGU_BLOCK), F32),
                   jax.ShapeDtypeStruct((SGU_BLOCK, N_HEADS * LANES), F32),
                   jax.ShapeDtypeStruct((8, D), F32), jax.ShapeDtypeStruct((8, D_IN), F32)],
        scratch_shapes=[pltpu.VMEM((tm + POOL_HALO, D), F32), pltpu.VMEM((tm, D), F32), pltpu.VMEM((tm, D), F32),
                        pltpu.VMEM((tm, D), F32), pltpu.VMEM((SGU_BLOCK, D), F32)],
        compiler_params=_params(1),
    )(z, pooled, dy, w_pool, pool_scale, ln_g, ln_b, ws, ws_t, bsp, w_proj, w_out, mask)


def _dx(dz, w_in4, x, dh1, g1, dep, tm):
    t = x.shape[0]
    cw = D_IN // N_SHARD

    def body(dz_ref, w_ref, x_ref, dh1_ref, g_ref, dep_ref, gx_ref, vec_ref, acc_ref):
        i, s = pl.program_id(0), pl.program_id(1)

        @pl.when((i == 0) & (s == 0))
        def _():
            vec_ref[...] = jnp.zeros_like(vec_ref)

        @pl.when(s == 0)
        def _():
            acc_ref[...] = jnp.zeros_like(acc_ref)

        acc_ref[...] += _dot_nt(dz_ref[...], w_ref[0])

        @pl.when(s == N_SHARD - 1)
        def _():
            dxn = acc_ref[...]
            xv = x_ref[...]
            r = lax.rsqrt(_mean(xv * xv) + EPS)
            xh = xv * r
            vec_ref[0:1, :] += _colsum(dxn * xh)
            gx_ref[...] = dh1_ref[...] + _rms_bwd(dxn * g_ref[...], xh, r)

    row = lambda i, s: (i, 0)
    fixed = lambda i, s: (0, 0)
    return pl.pallas_call(
        body, name="dx",
        grid=(t // tm, N_SHARD),
        in_specs=[pl.BlockSpec((tm, cw), lambda i, s: (i, s)),
                  pl.BlockSpec((1, D, cw), lambda i, s: (s, 0, 0)),
                  pl.BlockSpec((tm, D), row), pl.BlockSpec((tm, D), row), pl.BlockSpec((1, D), fixed),
                  pl.BlockSpec(dep.shape, fixed)],
        out_specs=[pl.BlockSpec((tm, D), row), pl.BlockSpec((8, D), fixed)],
        out_shape=[jax.ShapeDtypeStruct((t, D), F32), jax.ShapeDtypeStruct((8, D), F32)],
        scratch_shapes=[pltpu.VMEM((tm, D), F32)],
        compiler_params=_params(2),
    )(dz, w_in4, x, dh1, g1, dep)


def _dw(a, b, name, tm, tn, tk, out_dims, out_index, dep=None):
    t, m = a.shape
    n = b.shape[1]
    n_k = t // tk

    def body(a_ref, b_ref, *rest):
        o_ref, acc_ref = rest[-2:]
        k = pl.program_id(2)

        @pl.when(k == 0)
        def _():
            acc_ref[...] = jnp.zeros_like(acc_ref)

        acc_ref[...] += _dot_tn(a_ref[...], b_ref[...])

        @pl.when(k == n_k - 1)
        def _():
            o_ref[0, 0] = acc_ref[...].astype(BF16)

    return pl.pallas_call(
        body, name=name,
        grid=(m // tm, n // tn, n_k),
        in_specs=[pl.BlockSpec((tk, tm), lambda i, j, k: (k, i)),
                  pl.BlockSpec((tk, tn), lambda i, j, k: (k, j))]
        + ([] if dep is None else [pl.BlockSpec(dep.shape, lambda i, j, k: (0, 0))]),
        out_specs=pl.BlockSpec((1, 1, tm, tn), lambda i, j, k: (*out_index(i, j), 0, 0)),
        out_shape=jax.ShapeDtypeStruct((*out_dims, tm, tn), BF16),
        scratch_shapes=[pltpu.VMEM((tm, tn), F32)],
        compiler_params=_params(3),
    )(*((a, b) if dep is None else (a, b, dep)))


def _row_block(rows, cols, itemsize, n_bufs):
    budget = VMEM_LIMIT // 4
    rb = rows
    while rb % 16 == 0 and rb * cols * itemsize * n_bufs * 2 > budget:
        rb //= 2
    return rb


def _chip_sum(core, g, recv, name):
    _, n_s, rh, cols = g.shape
    rb = _row_block(rh, cols, 4, 3)

    def body(c_ref, g_ref, r_ref, o_ref):
        o_ref[...] = (g_ref[0].astype(F32) + r_ref[...].astype(F32)).astype(BF16)

    return pl.pallas_call(
        body, name=name,
        grid_spec=pltpu.PrefetchScalarGridSpec(
            num_scalar_prefetch=1, grid=(n_s, rh // rb),
            in_specs=[pl.BlockSpec((1, 1, rb, cols), lambda s, r, c: (c[0], s, r, 0)),
                      pl.BlockSpec((1, rb, cols), lambda s, r, c: (s, r, 0))],
            out_specs=pl.BlockSpec((1, rb, cols), lambda s, r, c: (s, r, 0))),
        out_shape=jax.ShapeDtypeStruct((n_s, rh, cols), BF16),
        compiler_params=_params(2),
    )(core, g, recv)


def _slot_sum(slots, name):
    n_s, rows, cols = slots.shape
    rb = _row_block(rows, cols, 4, n_s + 1)

    def body(s_ref, o_ref):
        acc = s_ref[0].astype(F32)
        for k in range(1, n_s):
            acc = acc + s_ref[k].astype(F32)
        o_ref[...] = acc

    return pl.pallas_call(
        body, name=name,
        grid=(rows // rb,),
        in_specs=[pl.BlockSpec((n_s, rb, cols), lambda r: (0, r, 0))],
        out_specs=pl.BlockSpec((rb, cols), lambda r: (r, 0)),
        out_shape=jax.ShapeDtypeStruct((rows, cols), F32),
        compiler_params=_params(1),
    )(slots)


def _adamw(w, g, m, v, name):
    rows, cols = w.shape
    rb = _row_block(rows, cols, 4, 7)
    c1 = 1.0 - ADAM_B1 ** ADAM_STEP
    c2 = 1.0 - ADAM_B2 ** ADAM_STEP

    def body(w_ref, g_ref, m_ref, v_ref, d_ref, mo_ref, vo_ref):
        gv = g_ref[...]
        m_new = ADAM_B1 * m_ref[...] + (1.0 - ADAM_B1) * gv
        v_new = ADAM_B2 * v_ref[...] + (1.0 - ADAM_B2) * (gv * gv)
        mo_ref[...] = m_new
        vo_ref[...] = v_new
        d_ref[...] = -ADAM_LR * ((m_new / c1) / (jnp.sqrt(v_new / c2) + ADAM_EPS) + ADAM_WD * w_ref[...])

    blk = pl.BlockSpec((rb, cols), lambda r: (r, 0))
    return pl.pallas_call(
        body, name=name,
        grid=(rows // rb,),
        in_specs=[blk] * 4, out_specs=[blk] * 3,
        out_shape=[jax.ShapeDtypeStruct((rows, cols), F32)] * 3,
        compiler_params=_params(1),
    )(w, g, m, v)


def _final_sum(chip, own, slots, name):
    n_s, rh, cols = own.shape
    rb = _row_block(rh, cols, 4, n_s + 1)

    def body(chip_ref, own_ref, s1_ref, s2_ref, s3_ref, o_ref):
        acc = own_ref[0].astype(F32) + s1_ref[0].astype(F32)
        acc = acc + s2_ref[0].astype(F32)
        o_ref[...] = acc + s3_ref[0].astype(F32)

    def slot(k):
        return pl.BlockSpec((1, rb, cols), lambda r, chip: (lax.rem(chip[0] + k, n_s), r, 0))

    return pl.pallas_call(
        body, name=name,
        grid_spec=pltpu.PrefetchScalarGridSpec(
            num_scalar_prefetch=1, grid=(rh // rb,),
            in_specs=[slot(0), slot(1), slot(2), slot(3)],
            out_specs=pl.BlockSpec((rb, cols), lambda r, chip: (r, 0))),
        out_shape=jax.ShapeDtypeStruct((rh, cols), F32),
        compiler_params=_params(1),
    )(chip, own, slots, slots, slots)


def _adamw_halves(core, own, sibling, w, m, v, name):
    rows, cols = w.shape
    rh = rows // 2
    rb = _row_block(rh, cols, 4, 9)
    n_r = rh // rb
    c1 = 1.0 - ADAM_B1 ** ADAM_STEP
    c2 = 1.0 - ADAM_B2 ** ADAM_STEP

    def body(c_ref, own_ref, sib_ref, w_ref, m_ref, v_ref, g_ref, d_ref, mo_ref, vo_ref):
        mine = c_ref[0] == pl.program_id(0)

        @pl.when(mine)
        def _():
            g_ref[...] = own_ref[...]

        @pl.when(jnp.logical_not(mine))
        def _():
            g_ref[...] = sib_ref[...]

        gv = g_ref[...]
        m_new = ADAM_B1 * m_ref[...] + (1.0 - ADAM_B1) * gv
        v_new = ADAM_B2 * v_ref[...] + (1.0 - ADAM_B2) * (gv * gv)
        mo_ref[...] = m_new
        vo_ref[...] = v_new
        d_ref[...] = -ADAM_LR * ((m_new / c1) / (jnp.sqrt(v_new / c2) + ADAM_EPS) + ADAM_WD * w_ref[...])

    half = pl.BlockSpec((rb, cols), lambda h, r, c: (r, 0))
    whole = pl.BlockSpec((rb, cols), lambda h, r, c: (h * n_r + r, 0))
    return pl.pallas_call(
        body, name=name,
        grid_spec=pltpu.PrefetchScalarGridSpec(
            num_scalar_prefetch=1, grid=(2, n_r),
            in_specs=[half, half, whole, whole, whole], out_specs=[whole] * 4),
        out_shape=[jax.ShapeDtypeStruct((rows, cols), F32)] * 4,
        compiler_params=_params(2),
    )(core, own, sibling, w, m, v)


def _place():
    x, y, c = lax.axis_index("x"), lax.axis_index("y"), lax.axis_index("c")
    chips = [(1 - x, y), (x, 1 - y), (1 - x, 1 - y)]
    return x, y, c, chips


def _remote(src, dst, send_sem, recv_sem, device):
    return pltpu.make_async_remote_copy(src_ref=src, dst_ref=dst, send_sem=send_sem, recv_sem=recv_sem,
                                        device_id=device, device_id_type=MESH)


def _all_gather_weights(shards):
    n = len(shards)

    def body(*refs):
        ins, outs = refs[:n], refs[n:2 * n]
        send_ici, recv_ici, send_d2d, recv_d2d, send_own, recv_own = refs[2 * n:]
        x, y, c, chips = _place()
        me = 2 * x + y
        sibling = (x, y, 1 - c)
        local = [_remote(ins[w], outs[w].at[me], send_own.at[w], recv_own.at[w], sibling) for w in range(n)]
        for cp in local:
            cp.start()
        sends = []
        for j, (px, py) in enumerate(chips):
            for w in range(n):
                cp = _remote(ins[w].at[c], outs[w].at[me, c], send_ici.at[w, j], recv_ici.at[w, j], (px, py, c))
                cp.start()
                sends.append(cp)
        for j, (px, py) in enumerate(chips):
            src_chip = 2 * px + py
            for w in range(n):
                landed = outs[w].at[src_chip, c]
                _remote(landed, landed, send_ici.at[w, j], recv_ici.at[w, j], (px, py, c)).wait_recv()
                cp = _remote(landed, landed, send_d2d.at[w, j], recv_d2d.at[w, j], sibling)
                cp.start()
                sends.append(cp)
        for j, (px, py) in enumerate(chips):
            src_chip = 2 * px + py
            for w in range(n):
                passed = outs[w].at[src_chip, 1 - c]
                _remote(passed, passed, send_d2d.at[w, j], recv_d2d.at[w, j], sibling).wait_recv()
        for cp in sends:
            cp.wait_send()
        for cp in local:
            cp.wait()

    return pl.pallas_call(
        body, name="ag_weights",
        in_specs=[ANY] * n, out_specs=[ANY] * n,
        out_shape=[jax.ShapeDtypeStruct((N_SHARD,) + s.shape, s.dtype) for s in shards],
        scratch_shapes=[pltpu.SemaphoreType.DMA((n, N_CHIP_PEERS))] * 4 + [pltpu.SemaphoreType.DMA((n,))] * 2,
    )(*shards)


def _split_start(name, arrays, n_copies, plan):
    n = len(arrays)

    def body(*refs):
        ins, send_sem, recv_sem, token = refs[:n], refs[n], refs[n + 1], refs[-1]
        for k, (src, dst, device) in enumerate(plan(ins)):
            _remote(src, dst, send_sem.at[k], recv_sem.at[k], device).start()
        token[...] = jnp.zeros_like(token)

    outs = pl.pallas_call(
        body, name=name,
        in_specs=[HBM] * n,
        out_specs=(SEM, SEM, *[HBM] * n, pl.BlockSpec(memory_space=pltpu.VMEM)),
        out_shape=(pltpu.SemaphoreType.DMA((n_copies,)), pltpu.SemaphoreType.DMA((n_copies,)),
                   *[pltpu.HBM(a.shape, a.dtype) for a in arrays], jax.ShapeDtypeStruct((8, LANES), F32)),
        input_output_aliases={i: i + 2 for i in range(n)},
        compiler_params=pltpu.CompilerParams(has_side_effects=EFFECT),
    )(*[pltpu.with_memory_space_constraint(a, pltpu.HBM) for a in arrays])
    return outs[0], outs[1], list(outs[2:2 + n]), outs[-1]


def _split_wait(name, send_sem, recv_sem, arrays, plan, after):
    n = len(arrays)

    def body(*refs):
        ins, send_ref, recv_ref = refs[:n], refs[n], refs[n + 1]
        for k, (src, dst, device) in enumerate(plan(ins)):
            cp = _remote(src, dst, send_ref.at[k], recv_ref.at[k], device)
            cp.wait_send()
            cp.wait_recv()

    outs = pl.pallas_call(
        body, name=name,
        in_specs=[HBM] * n + [SEM, SEM, ANY], out_specs=[HBM] * n,
        out_shape=[pltpu.HBM(a.shape, a.dtype) for a in arrays],
        input_output_aliases={i: i for i in range(n)},
        compiler_params=pltpu.CompilerParams(has_side_effects=EFFECT),
    )(*arrays, send_sem, recv_sem, after)
    return list(outs)


def _plan_gather_send(n):
    def plan(refs):
        x, y, c, chips = _place()
        me = 2 * x + y
        copies = []
        for w in range(n):
            own, full = refs[w], refs[n + w]
            copies.append((own, full.at[me], (x, y, 1 - c)))
            copies += [(own.at[c], full.at[me, c], (px, py, c)) for px, py in chips]
        return copies
    return plan


def _plan_gather_pass(n):
    def plan(refs):
        x, y, c, chips = _place()
        copies = []
        for w in range(n):
            for px, py in chips:
                landed = refs[w].at[2 * px + py, c]
                copies.append((landed, landed, (x, y, 1 - c)))
        return copies
    return plan


def _plan_exchange(n, with_small):
    def plan(refs):
        x, y, c, chips = _place()
        me = 2 * x + y
        n_in = n + int(with_small)
        copies = []
        for w in range(n):
            copies += [(refs[w].at[2 * px + py], refs[n_in + w].at[me], (px, py, c)) for px, py in chips]
        if with_small:
            small, slots = refs[n], refs[n_in + n]
            copies.append((small, slots.at[me], (x, y, 1 - c)))
            copies += [(small, slots.at[me], (px, py, c)) for px, py in chips]
        return copies
    return plan


def _swap_halves(name, grads, wholes):
    n = len(grads)
    arrays = list(grads) + list(wholes)
    n_all = len(arrays)

    def body(*refs):
        ins, outs = refs[:n_all], refs[n_all:2 * n_all]
        send_sem, recv_sem = refs[2 * n_all:]
        x, y, c, _ = _place()
        sibling = (x, y, 1 - c)
        copies = []
        for w in range(n_all):
            src = ins[w].at[1 - c] if w < n else ins[w]
            cp = _remote(src, outs[w], send_sem.at[w], recv_sem.at[w], sibling)
            cp.start()
            copies.append(cp)
        for cp in copies:
            cp.wait()

    return pl.pallas_call(
        body, name=name,
        in_specs=[ANY] * n_all, out_specs=[ANY] * n_all,
        out_shape=[jax.ShapeDtypeStruct(g.shape[1:], g.dtype) for g in grads]
        + [jax.ShapeDtypeStruct(a.shape, a.dtype) for a in wholes],
        scratch_shapes=[pltpu.SemaphoreType.DMA((n_all,))] * 2,
    )(*arrays)


def _all_reduce_tile(part):
    n_dev = 8

    def body(part_ref, sum_ref, slots_ref, send_sem, recv_sem):
        x, y, c, _ = _place()
        me = 4 * x + 2 * y + c
        slots_ref[me] = part_ref[...]
        flips = [(fx, fy, fc) for fx in (0, 1) for fy in (0, 1) for fc in (0, 1)][1:]
        copies = []
        for k, (fx, fy, fc) in enumerate(flips):
            peer = (x + fx - 2 * x * fx, y + fy - 2 * y * fy, c + fc - 2 * c * fc)
            cp = _remote(part_ref, slots_ref.at[me], send_sem.at[k], recv_sem.at[k], peer)
            cp.start()
            copies.append(cp)
        for cp in copies:
            cp.wait()
        acc = slots_ref[0]
        for k in range(1, n_dev):
            acc = acc + slots_ref[k]
        sum_ref[...] = acc

    return pl.pallas_call(
        body, name="all_reduce_tile",
        in_specs=[pl.BlockSpec(memory_space=pltpu.VMEM)],
        out_specs=pl.BlockSpec(memory_space=pltpu.VMEM),
        out_shape=jax.ShapeDtypeStruct(part.shape, F32),
        scratch_shapes=[pltpu.VMEM((n_dev,) + part.shape, F32),
                        pltpu.SemaphoreType.DMA((n_dev - 1,)), pltpu.SemaphoreType.DMA((n_dev - 1,))],
    )(part)


_SMALL = ("norm1_pre_g", "b_in", "pool_scale", "sgu_ln_g", "sgu_ln_b", "w_spatial", "b_spatial",
          "norm1_post_g", "norm2_pre_g", "norm2_post_g")
_MIX = ("w_in", "w_pool", "w_sgu_proj", "w_out")
_FF = ("w_ff1", "w_ff2")
_BIG = _MIX + _FF
_ORDER = ("norm1_pre_g", "w_in", "b_in", "w_pool", "pool_scale", "sgu_ln_g", "sgu_ln_b", "w_spatial", "b_spatial",
          "w_sgu_proj", "w_out", "norm1_post_g", "norm2_pre_g", "w_ff1", "w_ff2", "norm2_post_g")


def _pack(keys, rows, parts):
    flat = jnp.concatenate([parts[k].reshape(-1).astype(F32) for k in keys])
    flat = jnp.pad(flat, (0, rows * LANES - flat.shape[0]))
    return flat.reshape(rows, LANES)


def _pack_small(parts):
    return _pack(_SMALL, SMALL_ROWS, parts)


def _unpack_small(packed, like):
    flat = packed.reshape(-1)
    out, off = {}, 0
    for k in _SMALL:
        size = like[k].size
        out[k] = flat[off:off + size].reshape(like[k].shape)
        off += size
    return out


def _halves(a):
    return a.reshape(2, a.shape[0] // 2, a.shape[1])


def _step(p, m, v, x, target):
    t = x.shape[0]
    tm_mm = min(1024, t)
    tm_mix = min(256, t)
    tm_ffn = min(512, t)
    core = lax.axis_index("c").astype(jnp.int32).reshape(1)
    row = lambda a: a.reshape(1, -1)

    chip = (2 * lax.axis_index("x") + lax.axis_index("y")).astype(jnp.int32).reshape(1)

    local2d = {"w_in": p["w_in"], "w_ff1": p["w_ff1"], "w_ff2": p["w_ff2"], "w_out": p["w_out"],
               "w_pool": p["w_pool"].reshape(N_HEADS * 64, POOL_GROUP),
               "w_sgu_proj": p["w_sgu_proj"].reshape(N_HEADS * 64, HEAD)}
    shard = {k: _halves(local2d[k].astype(BF16)) for k in _BIG}
    full = dict(zip(_MIX, _all_gather_weights([shard[k] for k in _MIX])))
    ff_arrays = [shard[k] for k in _FF] + [lax.empty((N_SHARD,) + shard[k].shape, BF16) for k in _FF]
    send1, recv1, ff_arrays, token = _split_start("ag_ff_send", ff_arrays, 4 * len(_FF), _plan_gather_send(len(_FF)))
    w_in4 = full["w_in"].reshape(N_SHARD, D, D_IN // N_SHARD)
    w_out = full["w_out"].reshape(D, D)
    regroup = lambda a: a.reshape(N_SHARD, N_HEADS, 64, 256).transpose(1, 0, 2, 3).reshape(N_HEADS, 256, 256)
    w_pool = regroup(full["w_pool"])
    w_proj = regroup(full["w_sgu_proj"])

    pos = jnp.arange(SGU_BLOCK) // 64
    mask = (pos[:, None] >= pos[None, :]).astype(F32)
    ws = (p["w_spatial"] * mask[None]).astype(BF16)
    ws_t = jnp.swapaxes(ws, 1, 2)
    bsp = p["b_spatial"].reshape(N_HEADS, SGU_BLOCK, 1)

    z, xn = _inproj(x, row(p["norm1_pre_g"]), w_in4, row(p["b_in"]), token, tm_mm)
    pooled, merged, y, h1 = _mix_fwd(z, x, w_pool, row(p["pool_scale"]), row(p["sgu_ln_g"]), row(p["sgu_ln_b"]),
                                     ws, bsp, w_proj, w_out, row(p["norm1_post_g"]), tm_mix)
    ff_arrays = _split_wait("ag_ff_land", send1, recv1, ff_arrays, _plan_gather_send(len(_FF)), pooled)
    send2, recv2, ff_full, token = _split_start("ag_ff_pass", ff_arrays[len(_FF):], 3 * len(_FF),
                                                _plan_gather_pass(len(_FF)))
    ff_full = _split_wait("ag_ff_passed", send2, recv2, ff_full, _plan_gather_pass(len(_FF)), token)
    w_ff1_4 = ff_full[0].reshape(N_SHARD, D, D)
    w_ff2 = ff_full[1].reshape(D_FF, D)
    hn, f1, f, df2, dout, vec_a = _ffn_fwd(h1, row(p["norm2_pre_g"]), w_ff1_4, w_ff2, row(p["norm2_post_g"]),
                                           target, tm_ffn)
    loss = lax.psum(vec_a[1, 0], ("x", "y", "c"))

    tk = min(1024, t)
    g_big = {"w_ff2": _dw(f, df2, "dw_ff2", D // 2, D, tk, (2, N_SHARD), lambda i, j: (i % 2, i // 2))}
    df1, dh1, dy, vec_b = _ffn_bwd(df2, f1, w_ff1_4, w_ff2, h1, dout, y, row(p["norm2_pre_g"]),
                                   row(p["norm1_post_g"]), tm_ffn)
    g_big["w_ff1"] = _dw(hn, df1, "dw_ff1", D // 2, D, tk, (2, N_SHARD), lambda i, j: (i, j))

    def reduce_start(name, keys, wholes):
        swapped = _swap_halves("rs_swap_" + name, [g_big[k] for k in keys], wholes)
        sums = [_chip_sum(core, g_big[k], r, "chip_sum_" + k) for k, r in zip(keys, swapped)]
        arrays = list(sums)
        if wholes:
            arrays.append(_slot_sum(jnp.stack([wholes[0], swapped[-1]]), "chip_sum_small"))
        arrays += [lax.empty(s.shape, BF16) for s in sums]
        if wholes:
            arrays.append(lax.empty((N_SHARD,) + wholes[0].shape, F32))
        plan = _plan_exchange(len(keys), bool(wholes))
        send, recv, arrays, tok = _split_start("rs_send_" + name, arrays, 3 * len(keys) + 4 * len(wholes), plan)
        return send, recv, arrays, plan, tok

    def reduce_finish(name, keys, state, after):
        send, recv, arrays, plan, _ = state
        arrays = _split_wait("rs_land_" + name, send, recv, arrays, plan, after)
        n_in = len(arrays) // 2
        halves = {k: _final_sum(chip, arrays[i], arrays[n_in + i], "final_sum_" + k) for i, k in enumerate(keys)}
        return halves, arrays

    ff_state = reduce_start("ff", _FF, [])
    dw_out = _dw(merged, dy, "dw_out", D, D, tk, (1, 1), lambda i, j: (i, j), dep=ff_state[4]).reshape(D, D)
    dz, dwp, dwpr, dws, dbsp, vec_c, dbin = _mix_bwd(z, pooled, dy, w_pool, row(p["pool_scale"]), row(p["sgu_ln_g"]),
                                                     row(p["sgu_ln_b"]), ws, ws_t, bsp, w_proj, w_out, mask, tm_mix)
    g_big["w_in"] = _dw(xn, dz, "dw_in", D // 2, D_IN // N_SHARD, tk, (2, N_SHARD), lambda i, j: (i, j))
    by_half = lambda a, rows: a.reshape(N_SHARD, 2, rows // 2, a.shape[-1]).transpose(1, 0, 2, 3)
    g_big["w_out"] = by_half(dw_out, D // N_SHARD)
    shard_major = lambda a: a.reshape(N_HEADS, N_SHARD, 64, 256).transpose(1, 0, 2, 3).reshape(N_SHARD * 256, 256)
    g_big["w_pool"] = by_half(shard_major(dwp).astype(BF16), 256)
    g_big["w_sgu_proj"] = by_half(shard_major(dwpr).astype(BF16), 256)
    halves, _ = reduce_finish("ff", _FF, ff_state, g_big["w_in"])

    small_part = _pack(_SMALL[1:], SMALL_ROWS - 8, {
        "b_in": dbin[0], "pool_scale": vec_c[0], "sgu_ln_g": vec_c[1], "sgu_ln_b": vec_c[2],
        "w_spatial": dws, "b_spatial": dbsp[:, ::LANES].T, "norm1_post_g": vec_b[1], "norm2_pre_g": vec_b[0],
        "norm2_post_g": vec_a[0]})
    mix_state = reduce_start("mix", _MIX, [small_part])
    grad_x, vec_d = _dx(dz, w_in4, x, dh1, row(p["norm1_pre_g"]), mix_state[4], tm_mm)
    mix_halves, mix_arrays = reduce_finish("mix", _MIX, mix_state, grad_x)
    halves.update(mix_halves)
    small_rest = _slot_sum(mix_arrays[-1], "final_sum_small")
    g1_grad = _all_reduce_tile(vec_d[0].reshape(8, LANES))
    small_grad = jnp.concatenate([g1_grad, small_rest])
    shared = _swap_halves("rs_share_halves", [], [halves[k] for k in _BIG])

    grads, delta, new_m, new_v = {}, {}, {}, {}
    for k, sib in zip(_BIG, shared):
        as2d = lambda a: a.reshape(local2d[k].shape)
        outs = _adamw_halves(core, halves[k], sib, local2d[k], as2d(m[k]), as2d(v[k]), "adamw_" + k)
        grads[k], delta[k], new_m[k], new_v[k] = (a.reshape(p[k].shape) for a in outs)
    d_s, m_s, v_s = _adamw(_pack_small(p), small_grad, _pack_small(m), _pack_small(v), "adamw_small")
    grads.update(_unpack_small(small_grad, p))
    delta.update(_unpack_small(d_s, p))
    new_m.update(_unpack_small(m_s, p))
    new_v.update(_unpack_small(v_s, p))

    return (loss, grad_x, *[grads[k] for k in _ORDER], *[delta[k] for k in _ORDER],
            *[new_m[k] for k in _ORDER], *[new_v[k] for k in _ORDER])


def kernel(x, norm1_pre_g, w_in, b_in, w_pool, pool_scale, sgu_ln_g, sgu_ln_b, w_spatial, b_spatial, w_sgu_proj, w_out, norm1_post_g, norm2_pre_g, w_ff1, w_ff2, norm2_post_g, loss_target, m_norm1_pre_g, m_w_in, m_b_in, m_w_pool, m_pool_scale, m_sgu_ln_g, m_sgu_ln_b, m_w_spatial, m_b_spatial, m_w_sgu_proj, m_w_out, m_norm1_post_g, m_norm2_pre_g, m_w_ff1, m_w_ff2, m_norm2_post_g, v_norm1_pre_g, v_w_in, v_b_in, v_w_pool, v_pool_scale, v_sgu_ln_g, v_sgu_ln_b, v_w_spatial, v_b_spatial, v_w_sgu_proj, v_w_out, v_norm1_post_g, v_norm2_pre_g, v_w_ff1, v_w_ff2, v_norm2_post_g):
    p = dict(zip(_ORDER, (norm1_pre_g, w_in, b_in, w_pool, pool_scale, sgu_ln_g, sgu_ln_b, w_spatial, b_spatial,
                          w_sgu_proj, w_out, norm1_post_g, norm2_pre_g, w_ff1, w_ff2, norm2_post_g)))
    m = dict(zip(_ORDER, (m_norm1_pre_g, m_w_in, m_b_in, m_w_pool, m_pool_scale, m_sgu_ln_g, m_sgu_ln_b, m_w_spatial,
                          m_b_spatial, m_w_sgu_proj, m_w_out, m_norm1_post_g, m_norm2_pre_g, m_w_ff1, m_w_ff2,
                          m_norm2_post_g)))
    v = dict(zip(_ORDER, (v_norm1_pre_g, v_w_in, v_b_in, v_w_pool, v_pool_scale, v_sgu_ln_g, v_sgu_ln_b, v_w_spatial,
                          v_b_spatial, v_w_sgu_proj, v_w_out, v_norm1_post_g, v_norm2_pre_g, v_w_ff1, v_w_ff2,
                          v_norm2_post_g)))
    outs = _step(p, m, v, x[0], loss_target[0])
    return (outs[0], outs[1][None], *outs[2:])
```

```python
import math

import jax
import jax.numpy as jnp
from jax import lax
from jax.experimental import pallas as pl
from jax.experimental.pallas import tpu as pltpu

F32 = jnp.float32
BF16 = jnp.bfloat16

D = 1024
D_IN = 5 * D
D_FF = 4 * D
N_SHARD = 4
N_CHIP_PEERS = 3
POOL_WINDOWS = (2, 4, 8, 16)
POOL_GROUP = 256
POOL_HALO = 16
SGU_BLOCK = 128
N_HEADS = 4
HEAD = 256
EPS = 1e-6

ADAM_LR = 0.001
ADAM_B1 = 0.9
ADAM_B2 = 0.999
ADAM_EPS = 1e-08
ADAM_WD = 0.01
ADAM_STEP = 10

V7X_VMEM_BYTES = 64 * 1024 * 1024
VMEM_LIMIT = V7X_VMEM_BYTES - 8 * 1024 * 1024
SMALL_ROWS = 616
LANES = 128

MESH = pl.DeviceIdType.MESH
ANY = pl.BlockSpec(memory_space=pl.ANY)
HBM = pl.BlockSpec(memory_space=pltpu.HBM)
SEM = pl.BlockSpec(memory_space=pltpu.SEMAPHORE)
EFFECT = pltpu.SideEffectType.DATAFLOW_SIDE_EFFECTING

NT_DIMS = (((1,), (1,)), ((), ()))
TN_DIMS = (((0,), (0,)), ((), ()))


def _params(n_axes):
    return pltpu.CompilerParams(dimension_semantics=("arbitrary",) * n_axes, vmem_limit_bytes=VMEM_LIMIT)


def _dot(a, b):
    return jnp.dot(a, b, preferred_element_type=F32)


def _dot_nt(a, b):
    return lax.dot_general(a, b, NT_DIMS, preferred_element_type=F32)


def _dot_tn(a, b):
    return lax.dot_general(a, b, TN_DIMS, preferred_element_type=F32)


def _mean(a):
    return jnp.mean(a, axis=-1, keepdims=True)


def _colsum(a):
    return jnp.sum(a, axis=0, keepdims=True)


_GELU_C0 = math.sqrt(2.0 / math.pi)
_GELU_C1 = 0.044715


def _gelu(a):
    t = jnp.tanh(_GELU_C0 * (a + _GELU_C1 * (a * a * a)))
    return a * (0.5 * (1.0 + t))


def _gelu_and_grad(a):
    a2 = a * a
    t = jnp.tanh(_GELU_C0 * (a + _GELU_C1 * (a2 * a)))
    cdf = 0.5 * (1.0 + t)
    grad = cdf + 0.5 * a * (1.0 - t * t) * (_GELU_C0 * (1.0 + 3.0 * _GELU_C1 * a2))
    return a * cdf, grad


def _rms_bwd(dn, n, r):
    return r * (dn - n * _mean(dn * n))


def _inproj(x, g1, w_in4, b_in, dep, tm):
    t = x.shape[0]
    cw = D_IN // N_SHARD

    def body(x_ref, g_ref, w_ref, b_ref, dep_ref, z_ref, xn_ref):
        @pl.when(pl.program_id(1) == 0)
        def _():
            xv = x_ref[...]
            r = lax.rsqrt(_mean(xv * xv) + EPS)
            xn_ref[...] = (xv * r * g_ref[...]).astype(BF16)

        z_ref[...] = _dot(xn_ref[...], w_ref[0]) + b_ref[...]

    return pl.pallas_call(
        body, name="inproj",
        grid=(t // tm, N_SHARD),
        in_specs=[pl.BlockSpec((tm, D), lambda i, s: (i, 0)),
                  pl.BlockSpec((1, D), lambda i, s: (0, 0)),
                  pl.BlockSpec((1, D, cw), lambda i, s: (s, 0, 0)),
                  pl.BlockSpec((1, cw), lambda i, s: (0, s)),
                  pl.BlockSpec(dep.shape, lambda i, s: (0, 0))],
        out_specs=[pl.BlockSpec((tm, cw), lambda i, s: (i, s)),
                   pl.BlockSpec((tm, D), lambda i, s: (i, 0))],
        out_shape=[jax.ShapeDtypeStruct((t, D_IN), F32), jax.ShapeDtypeStruct((t, D), BF16)],
        compiler_params=_params(2),
    )(x, g1, w_in4, b_in, dep)


def _sgu_forward(z_ref, lng_ref, lnb_ref, ws_ref, bsp_ref, sv_ref, tm, with_grad):
    if with_grad:
        u, du_dz = _gelu_and_grad(z_ref[:, D:2 * D])
        gv, dgv_dz = _gelu_and_grad(z_ref[:, 2 * D:3 * D])
    else:
        u = _gelu(z_ref[:, D:2 * D])
        gv = _gelu(z_ref[:, 2 * D:3 * D])
        du_dz = dgv_dz = None
    xc = gv - _mean(gv)
    rstd = lax.rsqrt(_mean(xc * xc) + EPS)
    vhat = xc * rstd
    vb = (vhat * lng_ref[...] + lnb_ref[...]).astype(BF16)
    for b in range(tm // SGU_BLOCK):
        rows = slice(b * SGU_BLOCK, (b + 1) * SGU_BLOCK)
        for h in range(N_HEADS):
            cols = slice(h * HEAD, (h + 1) * HEAD)
            sv_ref[rows, cols] = _dot(ws_ref[h], vb[rows, cols]) + bsp_ref[h]
    return u, du_dz, dgv_dz, vhat, rstd, vb


def _mix_fwd(z, x, w_pool, pool_scale, ln_g, ln_b, ws, bsp, w_proj, w_out, g1post, tm):
    t = x.shape[0]

    def body(z_ref, x_ref, wp_ref, psc_ref, lng_ref, lnb_ref, ws_ref, bsp_ref, wpr_ref, wo_ref, g1p_ref,
             pooled_ref, merged_ref, y_ref, h1_ref, win_ref, sv_ref):
        i = pl.program_id(0)

        @pl.when(i == 0)
        def _():
            win_ref[0:POOL_HALO, :] = jnp.zeros((POOL_HALO, D), F32)

        win_ref[POOL_HALO:POOL_HALO + tm, :] = z_ref[:, 0:D]
        pos1 = lax.broadcasted_iota(jnp.int32, (tm, 1), 0) + (i * tm + 1)
        a_parts = []
        for g, w in enumerate(POOL_WINDOWS):
            cols = slice(g * POOL_GROUP, (g + 1) * POOL_GROUP)
            ts = win_ref[POOL_HALO:POOL_HALO + tm, cols]
            for k in range(1, w):
                ts = ts + win_ref[POOL_HALO - k:POOL_HALO - k + tm, cols]
            inv_cnt = 1.0 / jnp.minimum(pos1, w).astype(F32)
            pooled = (ts * inv_cnt - z_ref[:, cols]).astype(BF16)
            pooled_ref[:, cols] = pooled
            a_parts.append(_dot(pooled, wp_ref[g]) * psc_ref[:, cols])
        win_ref[0:POOL_HALO, :] = win_ref[tm:tm + POOL_HALO, :]

        u, _, _, _, _, _ = _sgu_forward(z_ref, lng_ref, lnb_ref, ws_ref, bsp_ref, sv_ref, tm, False)
        gated = (u * sv_ref[...]).astype(BF16)
        b_parts = [_dot(gated[:, h * HEAD:(h + 1) * HEAD], wpr_ref[h]) for h in range(N_HEADS)]

        a = jnp.concatenate(a_parts, axis=1)
        bbr = jnp.concatenate(b_parts, axis=1)
        merged = (jax.nn.sigmoid(z_ref[:, 3 * D:4 * D]) * a + jax.nn.sigmoid(z_ref[:, 4 * D:5 * D]) * bbr).astype(BF16)
        merged_ref[...] = merged
        y = _dot(merged, wo_ref[...])
        y_ref[...] = y
        r = lax.rsqrt(_mean(y * y) + EPS)
        h1_ref[...] = x_ref[...] + (y * r) * g1p_ref[...]

    row = lambda i: (i, 0)
    fixed2 = lambda i: (0, 0)
    fixed3 = lambda i: (0, 0, 0)
    vec = pl.BlockSpec((1, D), fixed2)
    return pl.pallas_call(
        body, name="mix_fwd",
        grid=(t // tm,),
        in_specs=[pl.BlockSpec((tm, D_IN), row), pl.BlockSpec((tm, D), row),
                  pl.BlockSpec((N_HEADS, POOL_GROUP, POOL_GROUP), fixed3), vec, vec, vec,
                  pl.BlockSpec((N_HEADS, SGU_BLOCK, SGU_BLOCK), fixed3),
                  pl.BlockSpec((N_HEADS, SGU_BLOCK, 1), fixed3),
                  pl.BlockSpec((N_HEADS, HEAD, HEAD), fixed3),
                  pl.BlockSpec((D, D), fixed2), vec],
        out_specs=[pl.BlockSpec((tm, D), row)] * 4,
        out_shape=[jax.ShapeDtypeStruct((t, D), BF16), jax.ShapeDtypeStruct((t, D), BF16),
                   jax.ShapeDtypeStruct((t, D), F32), jax.ShapeDtypeStruct((t, D), F32)],
        scratch_shapes=[pltpu.VMEM((tm + POOL_HALO, D), F32), pltpu.VMEM((tm, D), F32)],
        compiler_params=_params(1),
    )(z, x, w_pool, pool_scale, ln_g, ln_b, ws, bsp, w_proj, w_out, g1post)


def _ffn_fwd(h1, g2pre, w1, w2, g2post, target, tm):
    t = h1.shape[0]
    n_j = D_FF // D

    def body(h1_ref, g2_ref, w1_ref, w2_ref, g2p_ref, tgt_ref,
             hn_ref, f1_ref, f_ref, df2_ref, dout_ref, vec_ref, acc_ref):
        i, j = pl.program_id(0), pl.program_id(1)

        @pl.when((i == 0) & (j == 0))
        def _():
            vec_ref[...] = jnp.zeros_like(vec_ref)

        @pl.when(j == 0)
        def _():
            h = h1_ref[...]
            r = lax.rsqrt(_mean(h * h) + EPS)
            hn_ref[...] = (h * r * g2_ref[...]).astype(BF16)
            acc_ref[...] = jnp.zeros_like(acc_ref)

        f1 = _dot(hn_ref[...], w1_ref[0])
        f1_ref[...] = f1
        rl = jnp.maximum(f1, 0.0)
        f = (rl * rl).astype(BF16)
        f_ref[...] = f
        acc_ref[...] += _dot(f, w2_ref[...])

        @pl.when(j == n_j - 1)
        def _():
            f2 = acc_ref[...]
            r = lax.rsqrt(_mean(f2 * f2) + EPS)
            n2 = f2 * r
            err = h1_ref[...] + n2 * g2p_ref[...] - tgt_ref[...]
            loss = 0.5 * jnp.sum(_mean(err * err))
            dout = err * (1.0 / D)
            dout_ref[...] = dout
            vec_ref[0:1, :] += _colsum(dout * n2)
            vec_ref[1:2, :] += jnp.full((1, D), loss, F32)
            df2_ref[...] = _rms_bwd(dout * g2p_ref[...], n2, r).astype(BF16)

    row = lambda i, j: (i, 0)
    fixed = lambda i, j: (0, 0)
    vec = pl.BlockSpec((1, D), fixed)
    return pl.pallas_call(
        body, name="ffn_fwd",
        grid=(t // tm, n_j),
        in_specs=[pl.BlockSpec((tm, D), row), vec,
                  pl.BlockSpec((1, D, D), lambda i, j: (j, 0, 0)),
                  pl.BlockSpec((D, D), lambda i, j: (j, 0)), vec,
                  pl.BlockSpec((tm, D), row)],
        out_specs=[pl.BlockSpec((tm, D), row),
                   pl.BlockSpec((tm, D), lambda i, j: (i, j)),
                   pl.BlockSpec((tm, D), lambda i, j: (i, j)),
                   pl.BlockSpec((tm, D), row), pl.BlockSpec((tm, D), row),
                   pl.BlockSpec((8, D), fixed)],
        out_shape=[jax.ShapeDtypeStruct((t, D), BF16), jax.ShapeDtypeStruct((t, D_FF), F32),
                   jax.ShapeDtypeStruct((t, D_FF), BF16), jax.ShapeDtypeStruct((t, D), BF16),
                   jax.ShapeDtypeStruct((t, D), F32), jax.ShapeDtypeStruct((8, D), F32)],
        scratch_shapes=[pltpu.VMEM((tm, D), F32)],
        compiler_params=_params(2),
    )(h1, g2pre, w1, w2, g2post, target)


def _ffn_bwd(df2, f1, w1, w2, h1, dout, y, g2pre, g1post, tm):
    t = h1.shape[0]
    n_j = D_FF // D

    def body(df2_ref, f1_ref, w1_ref, w2_ref, h1_ref, dout_ref, y_ref, g2_ref, g1p_ref,
             df1_ref, dh1_ref, dy_ref, vec_ref, acc_ref):
        i, j = pl.program_id(0), pl.program_id(1)

        @pl.when((i == 0) & (j == 0))
        def _():
            vec_ref[...] = jnp.zeros_like(vec_ref)

        @pl.when(j == 0)
        def _():
            acc_ref[...] = jnp.zeros_like(acc_ref)

        df = _dot_nt(df2_ref[...], w2_ref[...])
        df1 = (df * (2.0 * jnp.maximum(f1_ref[...], 0.0))).astype(BF16)
        df1_ref[...] = df1
        acc_ref[...] += _dot_nt(df1, w1_ref[0])

        @pl.when(j == n_j - 1)
        def _():
            dhn = acc_ref[...]
            h = h1_ref[...]
            r2 = lax.rsqrt(_mean(h * h) + EPS)
            nh = h * r2
            vec_ref[0:1, :] += _colsum(dhn * nh)
            dh1 = dout_ref[...] + _rms_bwd(dhn * g2_ref[...], nh, r2)
            dh1_ref[...] = dh1
            yv = y_ref[...]
            r1 = lax.rsqrt(_mean(yv * yv) + EPS)
            ny = yv * r1
            vec_ref[1:2, :] += _colsum(dh1 * ny)
            dy_ref[...] = _rms_bwd(dh1 * g1p_ref[...], ny, r1).astype(BF16)

    row = lambda i, j: (i, 0)
    fixed = lambda i, j: (0, 0)
    vec = pl.BlockSpec((1, D), fixed)
    return pl.pallas_call(
        body, name="ffn_bwd",
        grid=(t // tm, n_j),
        in_specs=[pl.BlockSpec((tm, D), row),
                  pl.BlockSpec((tm, D), lambda i, j: (i, j)),
                  pl.BlockSpec((1, D, D), lambda i, j: (j, 0, 0)),
                  pl.BlockSpec((D, D), lambda i, j: (j, 0)),
                  pl.BlockSpec((tm, D), row), pl.BlockSpec((tm, D), row), pl.BlockSpec((tm, D), row), vec, vec],
        out_specs=[pl.BlockSpec((tm, D), lambda i, j: (i, j)),
                   pl.BlockSpec((tm, D), row), pl.BlockSpec((tm, D), row),
                   pl.BlockSpec((8, D), fixed)],
        out_shape=[jax.ShapeDtypeStruct((t, D_FF), BF16), jax.ShapeDtypeStruct((t, D), F32),
                   jax.ShapeDtypeStruct((t, D), BF16), jax.ShapeDtypeStruct((8, D), F32)],
        scratch_shapes=[pltpu.VMEM((tm, D), F32)],
        compiler_params=_params(2),
    )(df2, f1, w1, w2, h1, dout, y, g2pre, g1post)


def _mix_bwd(z, pooled, dy, w_pool, pool_scale, ln_g, ln_b, ws, ws_t, bsp, w_proj, w_out, mask, tm):
    t = z.shape[0]
    n_t = t // tm

    def body(z_ref, pooled_ref, dy_ref, wp_ref, psc_ref, lng_ref, lnb_ref, ws_ref, wst_ref, bsp_ref, wpr_ref,
             wo_ref, mask_ref,
             dz_ref, dwp_ref, dwpr_ref, dws_ref, dbsp_ref, vec_ref, dbin_ref,
             win_ref, sv_ref, dsv_ref, dv_ref, dsvsum_ref):
        i = pl.program_id(0)

        @pl.when(i == 0)
        def _():
            win_ref[tm:tm + POOL_HALO, :] = jnp.zeros((POOL_HALO, D), F32)
            dwp_ref[...] = jnp.zeros_like(dwp_ref)
            dwpr_ref[...] = jnp.zeros_like(dwpr_ref)
            dws_ref[...] = jnp.zeros_like(dws_ref)
            vec_ref[...] = jnp.zeros_like(vec_ref)
            dbin_ref[...] = jnp.zeros_like(dbin_ref)
            dsvsum_ref[...] = jnp.zeros_like(dsvsum_ref)

        dmerged = _dot_nt(dy_ref[...], wo_ref[...])

        u, du_dz, dgv_dz, vhat, rstd, vb = _sgu_forward(z_ref, lng_ref, lnb_ref, ws_ref, bsp_ref, sv_ref, tm, True)
        sv = sv_ref[...]
        gated = (u * sv).astype(BF16)

        sa = jax.nn.sigmoid(z_ref[:, 3 * D:4 * D])
        sb = jax.nn.sigmoid(z_ref[:, 4 * D:5 * D])
        da = dmerged * sa
        dbbr = (dmerged * sb).astype(BF16)

        tile = n_t - 1 - i
        pos1 = lax.broadcasted_iota(jnp.int32, (tm, 1), 0) + (tile * tm + 1)
        dzga_parts, dpooled_parts = [], []
        for g, w in enumerate(POOL_WINDOWS):
            cols = slice(g * POOL_GROUP, (g + 1) * POOL_GROUP)
            pg = pooled_ref[:, cols]
            a_pre = _dot(pg, wp_ref[g])
            da_g = da[:, cols]
            psc = psc_ref[:, cols]
            vec_ref[0:1, cols] += _colsum(da_g * a_pre)
            sa_g = sa[:, cols]
            dzga_parts.append(dmerged[:, cols] * (a_pre * psc) * (sa_g * (1.0 - sa_g)))
            da_pre = (da_g * psc).astype(BF16)
            dwp_ref[g] += _dot_tn(pg, da_pre)
            dpooled = _dot_nt(da_pre, wp_ref[g])
            dpooled_parts.append(dpooled)
            inv_cnt = 1.0 / jnp.minimum(pos1, w).astype(F32)
            win_ref[0:tm, cols] = dpooled * inv_cnt
        dzp_parts = []
        for g, w in enumerate(POOL_WINDOWS):
            cols = slice(g * POOL_GROUP, (g + 1) * POOL_GROUP)
            acc = win_ref[0:tm, cols]
            for k in range(1, w):
                acc = acc + win_ref[k:k + tm, cols]
            dzp_parts.append(acc - dpooled_parts[g])
        win_ref[tm:tm + POOL_HALO, :] = win_ref[0:POOL_HALO, :]

        dzgb_parts = []
        for h in range(N_HEADS):
            cols = slice(h * HEAD, (h + 1) * HEAD)
            g_h = gated[:, cols]
            db_h = dbbr[:, cols]
            bbr_h = _dot(g_h, wpr_ref[h])
            sb_h = sb[:, cols]
            dzgb_parts.append(dmerged[:, cols] * bbr_h * (sb_h * (1.0 - sb_h)))
            dwpr_ref[h] += _dot_tn(g_h, db_h)
            dgated_h = _dot_nt(db_h, wpr_ref[h])
            dsv_ref[:, cols] = dgated_h * u[:, cols]
            sv_ref[:, cols] = dgated_h * sv[:, cols]
        dzu = sv_ref[...] * du_dz
        dsv = dsv_ref[...]
        dsvb = dsv.astype(BF16)
        blk_sum = dsv[0:SGU_BLOCK, :]
        for b in range(1, tm // SGU_BLOCK):
            blk_sum = blk_sum + dsv[b * SGU_BLOCK:(b + 1) * SGU_BLOCK, :]
        dsvsum_ref[...] += blk_sum
        for b in range(tm // SGU_BLOCK):
            rows = slice(b * SGU_BLOCK, (b + 1) * SGU_BLOCK)
            for h in range(N_HEADS):
                cols = slice(h * HEAD, (h + 1) * HEAD)
                dws_ref[h] += _dot_nt(dsvb[rows, cols], vb[rows, cols])
                dv_ref[rows, cols] = _dot(wst_ref[h], dsvb[rows, cols])
        dv = dv_ref[...]
        vec_ref[1:2, :] += _colsum(dv * vhat)
        vec_ref[2:3, :] += _colsum(dv)
        dvhat = dv * lng_ref[...]
        dgv = rstd * (dvhat - _mean(dvhat) - vhat * _mean(dvhat * vhat))
        dzv = dgv * dgv_dz

        parts = [jnp.concatenate(dzp_parts, axis=1), dzu, dzv,
                 jnp.concatenate(dzga_parts, axis=1), jnp.concatenate(dzgb_parts, axis=1)]
        for k, part in enumerate(parts):
            dz_ref[:, k * D:(k + 1) * D] = part.astype(BF16)
            dbin_ref[0:1, k * D:(k + 1) * D] += _colsum(part)

        @pl.when(i == n_t - 1)
        def _():
            for h in range(N_HEADS):
                dws_ref[h] = jnp.where(mask_ref[...] > 0.0, dws_ref[h], 0.0)
                tot = jnp.sum(dsvsum_ref[:, h * HEAD:(h + 1) * HEAD], axis=1, keepdims=True)
                dbsp_ref[:, h * LANES:(h + 1) * LANES] = jnp.broadcast_to(tot, (SGU_BLOCK, LANES))

    row = lambda i: (n_t - 1 - i, 0)
    fixed2 = lambda i: (0, 0)
    fixed3 = lambda i: (0, 0, 0)
    vec = pl.BlockSpec((1, D), fixed2)
    sq = pl.BlockSpec((N_HEADS, SGU_BLOCK, SGU_BLOCK), fixed3)
    grp = pl.BlockSpec((N_HEADS, HEAD, HEAD), fixed3)
    return pl.pallas_call(
        body, name="mix_bwd",
        grid=(n_t,),
        in_specs=[pl.BlockSpec((tm, D_IN), row), pl.BlockSpec((tm, D), row), pl.BlockSpec((tm, D), row),
                  grp, vec, vec, vec, sq, sq, pl.BlockSpec((N_HEADS, SGU_BLOCK, 1), fixed3), grp,
                  pl.BlockSpec((D, D), fixed2), pl.BlockSpec((SGU_BLOCK, SGU_BLOCK), fixed2)],
        out_specs=[pl.BlockSpec((tm, D_IN), row), grp, grp, sq,
                   pl.BlockSpec((SGU_BLOCK, N_HEADS * LANES), fixed2),
                   pl.BlockSpec((8, D), fixed2), pl.BlockSpec((8, D_IN), fixed2)],
        out_shape=[jax.ShapeDtypeStruct((t, D_IN), BF16),
                   jax.ShapeDtypeStruct((N_HEADS, HEAD, HEAD), F32), jax.ShapeDtypeStruct((N_HEADS, HEAD, HEAD), F32),
                   jax.ShapeDtypeStruct((N_HEADS, SGU_BLOCK, SGU_BLOCK), F32),
                   jax.ShapeDtypeStruct((SGU_BLOCK, N_HEADS * LANES), F32),
                   jax.ShapeDtypeStruct((8, D), F32), jax.ShapeDtypeStruct((8, D_IN), F32)],
        scratch_shapes=[pltpu.VMEM((tm + POOL_HALO, D), F32), pltpu.VMEM((tm, D), F32), pltpu.VMEM((tm, D), F32),
                        pltpu.VMEM((tm, D), F32), pltpu.VMEM((SGU_BLOCK, D), F32)],
        compiler_params=_params(1),
    )(z, pooled, dy, w_pool, pool_scale, ln_g, ln_b, ws, ws_t, bsp, w_proj, w_out, mask)


def _dx(dz, w_in4, x, dh1, g1, dep, tm):
    t = x.shape[0]
    cw = D_IN // N_SHARD

    def body(dz_ref, w_ref, x_ref, dh1_ref, g_ref, dep_ref, gx_ref, vec_ref, acc_ref):
        i, s = pl.program_id(0), pl.program_id(1)

        @pl.when((i == 0) & (s == 0))
        def _():
            vec_ref[...] = jnp.zeros_like(vec_ref)

        @pl.when(s == 0)
        def _():
            acc_ref[...] = jnp.zeros_like(acc_ref)

        acc_ref[...] += _dot_nt(dz_ref[...], w_ref[0])

        @pl.when(s == N_SHARD - 1)
        def _():
            dxn = acc_ref[...]
            xv = x_ref[...]
            r = lax.rsqrt(_mean(xv * xv) + EPS)
            xh = xv * r
            vec_ref[0:1, :] += _colsum(dxn * xh)
            gx_ref[...] = dh1_ref[...] + _rms_bwd(dxn * g_ref[...], xh, r)

    row = lambda i, s: (i, 0)
    fixed = lambda i, s: (0, 0)
    return pl.pallas_call(
        body, name="dx",
        grid=(t // tm, N_SHARD),
        in_specs=[pl.BlockSpec((tm, cw), lambda i, s: (i, s)),
                  pl.BlockSpec((1, D, cw), lambda i, s: (s, 0, 0)),
                  pl.BlockSpec((tm, D), row), pl.BlockSpec((tm, D), row), pl.BlockSpec((1, D), fixed),
                  pl.BlockSpec(dep.shape, fixed)],
        out_specs=[pl.BlockSpec((tm, D), row), pl.BlockSpec((8, D), fixed)],
        out_shape=[jax.ShapeDtypeStruct((t, D), F32), jax.ShapeDtypeStruct((8, D), F32)],
        scratch_shapes=[pltpu.VMEM((tm, D), F32)],
        compiler_params=_params(2),
    )(dz, w_in4, x, dh1, g1, dep)


def _dw(a, b, name, tm, tn, tk, out_dims, out_index, dep=None):
    t, m = a.shape
    n = b.shape[1]
    n_k = t // tk

    def body(a_ref, b_ref, *rest):
        o_ref, acc_ref = rest[-2:]
        k = pl.program_id(2)

        @pl.when(k == 0)
        def _():
            acc_ref[...] = jnp.zeros_like(acc_ref)

        acc_ref[...] += _dot_tn(a_ref[...], b_ref[...])

        @pl.when(k == n_k - 1)
        def _():
            o_ref[0, 0] = acc_ref[...].astype(BF16)

    return pl.pallas_call(
        body, name=name,
        grid=(m // tm, n // tn, n_k),
        in_specs=[pl.BlockSpec((tk, tm), lambda i, j, k: (k, i)),
                  pl.BlockSpec((tk, tn), lambda i, j, k: (k, j))]
        + ([] if dep is None else [pl.BlockSpec(dep.shape, lambda i, j, k: (0, 0))]),
        out_specs=pl.BlockSpec((1, 1, tm, tn), lambda i, j, k: (*out_index(i, j), 0, 0)),
        out_shape=jax.ShapeDtypeStruct((*out_dims, tm, tn), BF16),
        scratch_shapes=[pltpu.VMEM((tm, tn), F32)],
        compiler_params=_params(3),
    )(*((a, b) if dep is None else (a, b, dep)))


def _row_block(rows, cols, itemsize, n_bufs):
    budget = VMEM_LIMIT // 4
    rb = rows
    while rb % 16 == 0 and rb * cols * itemsize * n_bufs * 2 > budget:
        rb //= 2
    return rb


def _chip_sum(core, g, recv, name):
    _, n_s, rh, cols = g.shape
    rb = _row_block(rh, cols, 4, 3)

    def body(c_ref, g_ref, r_ref, o_ref):
        o_ref[...] = (g_ref[0].astype(F32) + r_ref[...].astype(F32)).astype(BF16)

    return pl.pallas_call(
        body, name=name,
        grid_spec=pltpu.PrefetchScalarGridSpec(
            num_scalar_prefetch=1, grid=(n_s, rh // rb),
            in_specs=[pl.BlockSpec((1, 1, rb, cols), lambda s, r, c: (c[0], s, r, 0)),
                      pl.BlockSpec((1, rb, cols), lambda s, r, c: (s, r, 0))],
            out_specs=pl.BlockSpec((1, rb, cols), lambda s, r, c: (s, r, 0))),
        out_shape=jax.ShapeDtypeStruct((n_s, rh, cols), BF16),
        compiler_params=_params(2),
    )(core, g, recv)


def _slot_sum(slots, name):
    n_s, rows, cols = slots.shape
    rb = _row_block(rows, cols, 4, n_s + 1)

    def body(s_ref, o_ref):
        acc = s_ref[0].astype(F32)
        for k in range(1, n_s):
            acc = acc + s_ref[k].astype(F32)
        o_ref[...] = acc

    return pl.pallas_call(
        body, name=name,
        grid=(rows // rb,),
        in_specs=[pl.BlockSpec((n_s, rb, cols), lambda r: (0, r, 0))],
        out_specs=pl.BlockSpec((rb, cols), lambda r: (r, 0)),
        out_shape=jax.ShapeDtypeStruct((rows, cols), F32),
        compiler_params=_params(1),
    )(slots)


def _adamw(w, g, m, v, name):
    rows, cols = w.shape
    rb = _row_block(rows, cols, 4, 7)
    c1 = 1.0 - ADAM_B1 ** ADAM_STEP
    c2 = 1.0 - ADAM_B2 ** ADAM_STEP

    def body(w_ref, g_ref, m_ref, v_ref, d_ref, mo_ref, vo_ref):
        gv = g_ref[...]
        m_new = ADAM_B1 * m_ref[...] + (1.0 - ADAM_B1) * gv
        v_new = ADAM_B2 * v_ref[...] + (1.0 - ADAM_B2) * (gv * gv)
        mo_ref[...] = m_new
        vo_ref[...] = v_new
        d_ref[...] = -ADAM_LR * ((m_new / c1) / (jnp.sqrt(v_new / c2) + ADAM_EPS) + ADAM_WD * w_ref[...])

    blk = pl.BlockSpec((rb, cols), lambda r: (r, 0))
    return pl.pallas_call(
        body, name=name,
        grid=(rows // rb,),
        in_specs=[blk] * 4, out_specs=[blk] * 3,
        out_shape=[jax.ShapeDtypeStruct((rows, cols), F32)] * 3,
        compiler_params=_params(1),
    )(w, g, m, v)


def _final_sum(chip, own, slots, name):
    n_s, rh, cols = own.shape
    rb = _row_block(rh, cols, 4, n_s + 1)

    def body(chip_ref, own_ref, s1_ref, s2_ref, s3_ref, o_ref):
        acc = own_ref[0].astype(F32) + s1_ref[0].astype(F32)
        acc = acc + s2_ref[0].astype(F32)
        o_ref[...] = acc + s3_ref[0].astype(F32)

    def slot(k):
        return pl.BlockSpec((1, rb, cols), lambda r, chip: (lax.rem(chip[0] + k, n_s), r, 0))

    return pl.pallas_call(
        body, name=name,
        grid_spec=pltpu.PrefetchScalarGridSpec(
            num_scalar_prefetch=1, grid=(rh // rb,),
            in_specs=[slot(0), slot(1), slot(2), slot(3)],
            out_specs=pl.BlockSpec((rb, cols), lambda r, chip: (r, 0))),
        out_shape=jax.ShapeDtypeStruct((rh, cols), F32),
        compiler_params=_params(1),
    )(chip, own, slots, slots, slots)


def _adamw_halves(core, own, sibling, w, m, v, name):
    rows, cols = w.shape
    rh = rows // 2
    rb = _row_block(rh, cols, 4, 9)
    n_r = rh // rb
    c1 = 1.0 - ADAM_B1 ** ADAM_STEP
    c2 = 1.0 - ADAM_B2 ** ADAM_STEP

    def body(c_ref, own_ref, sib_ref, w_ref, m_ref, v_ref, g_ref, d_ref, mo_ref, vo_ref):
        mine = c_ref[0] == pl.program_id(0)

        @pl.when(mine)
        def _():
            g_ref[...] = own_ref[...]

        @pl.when(jnp.logical_not(mine))
        def _():
            g_ref[...] = sib_ref[...]

        gv = g_ref[...]
        m_new = ADAM_B1 * m_ref[...] + (1.0 - ADAM_B1) * gv
        v_new = ADAM_B2 * v_ref[...] + (1.0 - ADAM_B2) * (gv * gv)
        mo_ref[...] = m_new
        vo_ref[...] = v_new
        d_ref[...] = -ADAM_LR * ((m_new / c1) / (jnp.sqrt(v_new / c2) + ADAM_EPS) + ADAM_WD * w_ref[...])

    half = pl.BlockSpec((rb, cols), lambda h, r, c: (r, 0))
    whole = pl.BlockSpec((rb, cols), lambda h, r, c: (h * n_r + r, 0))
    return pl.pallas_call(
        body, name=name,
        grid_spec=pltpu.PrefetchScalarGridSpec(
            num_scalar_prefetch=1, grid=(2, n_r),
            in_specs=[half, half, whole, whole, whole], out_specs=[whole] * 4),
        out_shape=[jax.ShapeDtypeStruct((rows, cols), F32)] * 4,
        compiler_params=_params(2),
    )(core, own, sibling, w, m, v)


def _place():
    x, y, c = lax.axis_index("x"), lax.axis_index("y"), lax.axis_index("c")
    chips = [(1 - x, y), (x, 1 - y), (1 - x, 1 - y)]
    return x, y, c, chips


def _remote(src, dst, send_sem, recv_sem, device):
    return pltpu.make_async_remote_copy(src_ref=src, dst_ref=dst, send_sem=send_sem, recv_sem=recv_sem,
                                        device_id=device, device_id_type=MESH)


def _all_gather_weights(shards):
    n = len(shards)

    def body(*refs):
        ins, outs = refs[:n], refs[n:2 * n]
        send_ici, recv_ici, send_d2d, recv_d2d, send_own, recv_own = refs[2 * n:]
        x, y, c, chips = _place()
        me = 2 * x + y
        sibling = (x, y, 1 - c)
        local = [_remote(ins[w], outs[w].at[me], send_own.at[w], recv_own.at[w], sibling) for w in range(n)]
        for cp in local:
            cp.start()
        sends = []
        for j, (px, py) in enumerate(chips):
            for w in range(n):
                cp = _remote(ins[w].at[c], outs[w].at[me, c], send_ici.at[w, j], recv_ici.at[w, j], (px, py, c))
                cp.start()
                sends.append(cp)
        for j, (px, py) in enumerate(chips):
            src_chip = 2 * px + py
            for w in range(n):
                landed = outs[w].at[src_chip, c]
                _remote(landed, landed, send_ici.at[w, j], recv_ici.at[w, j], (px, py, c)).wait_recv()
                cp = _remote(landed, landed, send_d2d.at[w, j], recv_d2d.at[w, j], sibling)
                cp.start()
                sends.append(cp)
        for j, (px, py) in enumerate(chips):
            src_chip = 2 * px + py
            for w in range(n):
                passed = outs[w].at[src_chip, 1 - c]
                _remote(passed, passed, send_d2d.at[w, j], recv_d2d.at[w, j], sibling).wait_recv()
        for cp in sends:
            cp.wait_send()
        for cp in local:
            cp.wait()

    return pl.pallas_call(
        body, name="ag_weights",
        in_specs=[ANY] * n, out_specs=[ANY] * n,
        out_shape=[jax.ShapeDtypeStruct((N_SHARD,) + s.shape, s.dtype) for s in shards],
        scratch_shapes=[pltpu.SemaphoreType.DMA((n, N_CHIP_PEERS))] * 4 + [pltpu.SemaphoreType.DMA((n,))] * 2,
    )(*shards)


def _split_start(name, arrays, n_copies, plan, after=None):
    n = len(arrays)
    n_in = n + (after is not None)

    def body(*refs):
        ins, send_sem, recv_sem, token = refs[:n], refs[n_in], refs[n_in + 1], refs[-1]
        for k, (src, dst, device) in enumerate(plan(ins)):
            _remote(src, dst, send_sem.at[k], recv_sem.at[k], device).start()
        token[...] = jnp.zeros_like(token)

    outs = pl.pallas_call(
        body, name=name,
        in_specs=[HBM] * n + [ANY] * (n_in - n),
        out_specs=(SEM, SEM, *[HBM] * n, pl.BlockSpec(memory_space=pltpu.VMEM)),
        out_shape=(pltpu.SemaphoreType.DMA((n_copies,)), pltpu.SemaphoreType.DMA((n_copies,)),
                   *[pltpu.HBM(a.shape, a.dtype) for a in arrays], jax.ShapeDtypeStruct((8, LANES), F32)),
        input_output_aliases={i: i + 2 for i in range(n)},
        compiler_params=pltpu.CompilerParams(has_side_effects=EFFECT),
    )(*[pltpu.with_memory_space_constraint(a, pltpu.HBM) for a in arrays], *([] if after is None else [after]))
    return outs[0], outs[1], list(outs[2:2 + n]), outs[-1]


def _split_wait(name, send_sem, recv_sem, arrays, plan, after):
    n = len(arrays)

    def body(*refs):
        ins, send_ref, recv_ref = refs[:n], refs[n], refs[n + 1]
        for k, (src, dst, device) in enumerate(plan(ins)):
            cp = _remote(src, dst, send_ref.at[k], recv_ref.at[k], device)
            cp.wait_send()
            cp.wait_recv()

    outs = pl.pallas_call(
        body, name=name,
        in_specs=[HBM] * n + [SEM, SEM, ANY], out_specs=[HBM] * n,
        out_shape=[pltpu.HBM(a.shape, a.dtype) for a in arrays],
        input_output_aliases={i: i for i in range(n)},
        compiler_params=pltpu.CompilerParams(has_side_effects=EFFECT),
    )(*arrays, send_sem, recv_sem, after)
    return list(outs)


def _plan_gather_send(n):
    def plan(refs):
        x, y, c, chips = _place()
        me = 2 * x + y
        copies = []
        for w in range(n):
            own, full = refs[w], refs[n + w]
            copies.append((own, full.at[me], (x, y, 1 - c)))
            copies += [(own.at[c], full.at[me, c], (px, py, c)) for px, py in chips]
        return copies
    return plan


def _plan_gather_direct(n):
    def plan(refs):
        x, y, c, chips = _place()
        me = 2 * x + y
        copies = []
        for w in range(n):
            own, full = refs[w], refs[n + w]
            copies.append((own, full.at[me], (x, y, 1 - c)))
            for px, py in chips:
                copies += [(own.at[c], full.at[me, c], (px, py, c)), (own.at[c], full.at[me, c], (px, py, 1 - c))]
        return copies
    return plan


def _plan_gather_pass(n):
    def plan(refs):
        x, y, c, chips = _place()
        copies = []
        for w in range(n):
            for px, py in chips:
                landed = refs[w].at[2 * px + py, c]
                copies.append((landed, landed, (x, y, 1 - c)))
        return copies
    return plan


def _plan_exchange(n, with_small):
    def plan(refs):
        x, y, c, chips = _place()
        me = 2 * x + y
        n_in = n + int(with_small)
        copies = []
        for w in range(n):
            copies += [(refs[w].at[2 * px + py], refs[n_in + w].at[me], (px, py, c)) for px, py in chips]
        if with_small:
            small, slots = refs[n], refs[n_in + n]
            copies.append((small, slots.at[me], (x, y, 1 - c)))
            copies += [(small, slots.at[me], (px, py, c)) for px, py in chips]
        return copies
    return plan


def _swap_halves(name, grads, wholes):
    n = len(grads)
    arrays = list(grads) + list(wholes)
    n_all = len(arrays)

    def body(*refs):
        ins, outs = refs[:n_all], refs[n_all:2 * n_all]
        send_sem, recv_sem = refs[2 * n_all:]
        x, y, c, _ = _place()
        sibling = (x, y, 1 - c)
        copies = []
        for w in range(n_all):
            src = ins[w].at[1 - c] if w < n else ins[w]
            cp = _remote(src, outs[w], send_sem.at[w], recv_sem.at[w], sibling)
            cp.start()
            copies.append(cp)
        for cp in copies:
            cp.wait()

    return pl.pallas_call(
        body, name=name,
        in_specs=[ANY] * n_all, out_specs=[ANY] * n_all,
        out_shape=[jax.ShapeDtypeStruct(g.shape[1:], g.dtype) for g in grads]
        + [jax.ShapeDtypeStruct(a.shape, a.dtype) for a in wholes],
        scratch_shapes=[pltpu.SemaphoreType.DMA((n_all,))] * 2,
    )(*arrays)


def _all_reduce_tile(part):
    n_dev = 8

    def body(part_ref, sum_ref, slots_ref, send_sem, recv_sem):
        x, y, c, _ = _place()
        me = 4 * x + 2 * y + c
        slots_ref[me] = part_ref[...]
        flips = [(fx, fy, fc) for fx in (0, 1) for fy in (0, 1) for fc in (0, 1)][1:]
        copies = []
        for k, (fx, fy, fc) in enumerate(flips):
            peer = (x + fx - 2 * x * fx, y + fy - 2 * y * fy, c + fc - 2 * c * fc)
            cp = _remote(part_ref, slots_ref.at[me], send_sem.at[k], recv_sem.at[k], peer)
            cp.start()
            copies.append(cp)
        for cp in copies:
            cp.wait()
        acc = slots_ref[0]
        for k in range(1, n_dev):
            acc = acc + slots_ref[k]
        sum_ref[...] = acc

    return pl.pallas_call(
        body, name="all_reduce_tile",
        in_specs=[pl.BlockSpec(memory_space=pltpu.VMEM)],
        out_specs=pl.BlockSpec(memory_space=pltpu.VMEM),
        out_shape=jax.ShapeDtypeStruct(part.shape, F32),
        scratch_shapes=[pltpu.VMEM((n_dev,) + part.shape, F32),
                        pltpu.SemaphoreType.DMA((n_dev - 1,)), pltpu.SemaphoreType.DMA((n_dev - 1,))],
    )(part)


_SMALL = ("norm1_pre_g", "b_in", "pool_scale", "sgu_ln_g", "sgu_ln_b", "w_spatial", "b_spatial",
          "norm1_post_g", "norm2_pre_g", "norm2_post_g")
_MIX = ("w_in", "w_pool", "w_sgu_proj", "w_out")
_FF = ("w_ff1", "w_ff2")
_BIG = _MIX + _FF
_ORDER = ("norm1_pre_g", "w_in", "b_in", "w_pool", "pool_scale", "sgu_ln_g", "sgu_ln_b", "w_spatial", "b_spatial",
          "w_sgu_proj", "w_out", "norm1_post_g", "norm2_pre_g", "w_ff1", "w_ff2", "norm2_post_g")


def _pack(keys, rows, parts):
    flat = jnp.concatenate([parts[k].reshape(-1).astype(F32) for k in keys])
    flat = jnp.pad(flat, (0, rows * LANES - flat.shape[0]))
    return flat.reshape(rows, LANES)


def _pack_small(parts):
    return _pack(_SMALL, SMALL_ROWS, parts)


def _unpack_small(packed, like):
    flat = packed.reshape(-1)
    out, off = {}, 0
    for k in _SMALL:
        size = like[k].size
        out[k] = flat[off:off + size].reshape(like[k].shape)
        off += size
    return out


def _halves(a):
    return a.reshape(2, a.shape[0] // 2, a.shape[1])


def _step(p, m, v, x, target):
    t = x.shape[0]
    tm_mm = min(1024, t)
    tm_mix = min(256, t)
    tm_ffn = min(512, t)
    core = lax.axis_index("c").astype(jnp.int32).reshape(1)
    row = lambda a: a.reshape(1, -1)

    chip = (2 * lax.axis_index("x") + lax.axis_index("y")).astype(jnp.int32).reshape(1)

    local2d = {"w_in": p["w_in"], "w_ff1": p["w_ff1"], "w_ff2": p["w_ff2"], "w_out": p["w_out"],
               "w_pool": p["w_pool"].reshape(N_HEADS * 64, POOL_GROUP),
               "w_sgu_proj": p["w_sgu_proj"].reshape(N_HEADS * 64, HEAD)}
    shard = {k: _halves(local2d[k].astype(BF16)) for k in _BIG}
    landing = lambda keys: [lax.empty((N_SHARD,) + shard[k].shape, BF16) for k in keys]
    w_in4 = _all_gather_weights([shard["w_in"]])[0].reshape(N_SHARD, D, D_IN // N_SHARD)
    mix_keys = _MIX[1:]
    mix_arrays = [shard[k] for k in mix_keys] + landing(mix_keys)
    send0, recv0, mix_arrays, token = _split_start("ag_mix_send", mix_arrays, 7 * len(mix_keys),
                                                   _plan_gather_direct(len(mix_keys)), after=w_in4)
    ff_arrays = [shard[k] for k in _FF] + landing(_FF)
    send1, recv1, ff_arrays, token = _split_start("ag_ff_send", ff_arrays, 4 * len(_FF), _plan_gather_send(len(_FF)),
                                                  after=token)

    pos = jnp.arange(SGU_BLOCK) // 64
    mask = (pos[:, None] >= pos[None, :]).astype(F32)
    ws = (p["w_spatial"] * mask[None]).astype(BF16)
    ws_t = jnp.swapaxes(ws, 1, 2)
    bsp = p["b_spatial"].reshape(N_HEADS, SGU_BLOCK, 1)

    z, xn = _inproj(x, row(p["norm1_pre_g"]), w_in4, row(p["b_in"]), token, tm_mm)
    mix_arrays = _split_wait("ag_mix_land", send0, recv0, mix_arrays, _plan_gather_direct(len(mix_keys)), z)
    full = dict(zip(mix_keys, mix_arrays[len(mix_keys):]))
    w_out = full["w_out"].reshape(D, D)
    regroup = lambda a: a.reshape(N_SHARD, N_HEADS, 64, 256).transpose(1, 0, 2, 3).reshape(N_HEADS, 256, 256)
    w_pool = regroup(full["w_pool"])
    w_proj = regroup(full["w_sgu_proj"])
    pooled, merged, y, h1 = _mix_fwd(z, x, w_pool, row(p["pool_scale"]), row(p["sgu_ln_g"]), row(p["sgu_ln_b"]),
                                     ws, bsp, w_proj, w_out, row(p["norm1_post_g"]), tm_mix)
    ff_arrays = _split_wait("ag_ff_land", send1, recv1, ff_arrays, _plan_gather_send(len(_FF)), pooled)
    send2, recv2, ff_full, token = _split_start("ag_ff_pass", ff_arrays[len(_FF):], 3 * len(_FF),
                                                _plan_gather_pass(len(_FF)))
    ff_full = _split_wait("ag_ff_passed", send2, recv2, ff_full, _plan_gather_pass(len(_FF)), token)
    w_ff1_4 = ff_full[0].reshape(N_SHARD, D, D)
    w_ff2 = ff_full[1].reshape(D_FF, D)
    hn, f1, f, df2, dout, vec_a = _ffn_fwd(h1, row(p["norm2_pre_g"]), w_ff1_4, w_ff2, row(p["norm2_post_g"]),
                                           target, tm_ffn)
    loss = lax.psum(vec_a[1, 0], ("x", "y", "c"))

    tk = min(1024, t)
    g_big = {"w_ff2": _dw(f, df2, "dw_ff2", D // 2, D, tk, (2, N_SHARD), lambda i, j: (i % 2, i // 2))}
    df1, dh1, dy, vec_b = _ffn_bwd(df2, f1, w_ff1_4, w_ff2, h1, dout, y, row(p["norm2_pre_g"]),
                                   row(p["norm1_post_g"]), tm_ffn)
    g_big["w_ff1"] = _dw(hn, df1, "dw_ff1", D // 2, D, tk, (2, N_SHARD), lambda i, j: (i, j))

    def reduce_start(name, keys, wholes):
        swapped = _swap_halves("rs_swap_" + name, [g_big[k] for k in keys], wholes)
        sums = [_chip_sum(core, g_big[k], r, "chip_sum_" + k) for k, r in zip(keys, swapped)]
        arrays = list(sums)
        if wholes:
            arrays.append(_slot_sum(jnp.stack([wholes[0], swapped[-1]]), "chip_sum_small"))
        arrays += [lax.empty(s.shape, BF16) for s in sums]
        if wholes:
            arrays.append(lax.empty((N_SHARD,) + wholes[0].shape, F32))
        plan = _plan_exchange(len(keys), bool(wholes))
        send, recv, arrays, tok = _split_start("rs_send_" + name, arrays, 3 * len(keys) + 4 * len(wholes), plan)
        return send, recv, arrays, plan, tok

    def reduce_finish(name, keys, state, after):
        send, recv, arrays, plan, _ = state
        arrays = _split_wait("rs_land_" + name, send, recv, arrays, plan, after)
        n_in = len(arrays) // 2
        halves = {k: _final_sum(chip, arrays[i], arrays[n_in + i], "final_sum_" + k) for i, k in enumerate(keys)}
        return halves, arrays

    ff_state = reduce_start("ff", _FF, [])
    dw_out = _dw(merged, dy, "dw_out", D, D, tk, (1, 1), lambda i, j: (i, j), dep=ff_state[4]).reshape(D, D)
    dz, dwp, dwpr, dws, dbsp, vec_c, dbin = _mix_bwd(z, pooled, dy, w_pool, row(p["pool_scale"]), row(p["sgu_ln_g"]),
                                                     row(p["sgu_ln_b"]), ws, ws_t, bsp, w_proj, w_out, mask, tm_mix)
    g_big["w_in"] = _dw(xn, dz, "dw_in", D // 2, D_IN // N_SHARD, tk, (2, N_SHARD), lambda i, j: (i, j))
    by_half = lambda a, rows: a.reshape(N_SHARD, 2, rows // 2, a.shape[-1]).transpose(1, 0, 2, 3)
    g_big["w_out"] = by_half(dw_out, D // N_SHARD)
    shard_major = lambda a: a.reshape(N_HEADS, N_SHARD, 64, 256).transpose(1, 0, 2, 3).reshape(N_SHARD * 256, 256)
    g_big["w_pool"] = by_half(shard_major(dwp).astype(BF16), 256)
    g_big["w_sgu_proj"] = by_half(shard_major(dwpr).astype(BF16), 256)
    halves, _ = reduce_finish("ff", _FF, ff_state, g_big["w_in"])

    small_part = _pack(_SMALL[1:], SMALL_ROWS - 8, {
        "b_in": dbin[0], "pool_scale": vec_c[0], "sgu_ln_g": vec_c[1], "sgu_ln_b": vec_c[2],
        "w_spatial": dws, "b_spatial": dbsp[:, ::LANES].T, "norm1_post_g": vec_b[1], "norm2_pre_g": vec_b[0],
        "norm2_post_g": vec_a[0]})
    mix_state = reduce_start("mix", _MIX, [small_part])
    grad_x, vec_d = _dx(dz, w_in4, x, dh1, row(p["norm1_pre_g"]), mix_state[4], tm_mm)
    g1_grad = _all_reduce_tile(vec_d[0].reshape(8, LANES))

    grads, delta, new_m, new_v = {}, {}, {}, {}

    def finish(name, keys):
        shared = _swap_halves("rs_share_" + name, [], [halves[k] for k in keys])
        for k, sib in zip(keys, shared):
            as2d = lambda a: a.reshape(local2d[k].shape)
            outs = _adamw_halves(core, halves[k], sib, local2d[k], as2d(m[k]), as2d(v[k]), "adamw_" + k)
            grads[k], delta[k], new_m[k], new_v[k] = (a.reshape(p[k].shape) for a in outs)

    finish("ff", _FF)
    mix_halves, mix_arrays = reduce_finish("mix", _MIX, mix_state, new_v[_FF[-1]])
    halves.update(mix_halves)
    finish("mix", _MIX)
    small_grad = jnp.concatenate([g1_grad, _slot_sum(mix_arrays[-1], "final_sum_small")])
    d_s, m_s, v_s = _adamw(_pack_small(p), small_grad, _pack_small(m), _pack_small(v), "adamw_small")
    grads.update(_unpack_small(small_grad, p))
    delta.update(_unpack_small(d_s, p))
    new_m.update(_unpack_small(m_s, p))
    new_v.update(_unpack_small(v_s, p))

    return (loss, grad_x, *[grads[k] for k in _ORDER], *[delta[k] for k in _ORDER],
            *[new_m[k] for k in _ORDER], *[new_v[k] for k in _ORDER])


def kernel(x, norm1_pre_g, w_in, b_in, w_pool, pool_scale, sgu_ln_g, sgu_ln_b, w_spatial, b_spatial, w_sgu_proj, w_out, norm1_post_g, norm2_pre_g, w_ff1, w_ff2, norm2_post_g, loss_target, m_norm1_pre_g, m_w_in, m_b_in, m_w_pool, m_pool_scale, m_sgu_ln_g, m_sgu_ln_b, m_w_spatial, m_b_spatial, m_w_sgu_proj, m_w_out, m_norm1_post_g, m_norm2_pre_g, m_w_ff1, m_w_ff2, m_norm2_post_g, v_norm1_pre_g, v_w_in, v_b_in, v_w_pool, v_pool_scale, v_sgu_ln_g, v_sgu_ln_b, v_w_spatial, v_b_spatial, v_w_sgu_proj, v_w_out, v_norm1_post_g, v_norm2_pre_g, v_w_ff1, v_w_ff2, v_norm2_post_g):
    p = dict(zip(_ORDER, (norm1_pre_g, w_in, b_in, w_pool, pool_scale, sgu_ln_g, sgu_ln_b, w_spatial, b_spatial,
                          w_sgu_proj, w_out, norm1_post_g, norm2_pre_g, w_ff1, w_ff2, norm2_post_g)))
    m = dict(zip(_ORDER, (m_norm1_pre_g, m_w_in, m_b_in, m_w_pool, m_pool_scale, m_sgu_ln_g, m_sgu_ln_b, m_w_spatial,
                          m_b_spatial, m_w_sgu_proj, m_w_out, m_norm1_post_g, m_norm2_pre_g, m_w_ff1, m_w_ff2,
                          m_norm2_post_g)))
    v = dict(zip(_ORDER, (v_norm1_pre_g, v_w_in, v_b_in, v_w_pool, v_pool_scale, v_sgu_ln_g, v_sgu_ln_b, v_w_spatial,
                          v_b_spatial, v_w_sgu_proj, v_w_out, v_norm1_post_g, v_norm2_pre_g, v_w_ff1, v_w_ff2,
                          v_norm2_post_g)))
    outs = _step(p, m, v, x[0], loss_target[0])
    return (outs[0], outs[1][None], *outs[2:])
```

```python
import math

import jax
import jax.numpy as jnp
from jax import lax
from jax.experimental import pallas as pl
from jax.experimental.pallas import tpu as pltpu

F32 = jnp.float32
BF16 = jnp.bfloat16

D = 1024
D_IN = 5 * D
D_FF = 4 * D
N_SHARD = 4
N_CHIP_PEERS = 3
POOL_WINDOWS = (2, 4, 8, 16)
POOL_GROUP = 256
POOL_HALO = 16
SGU_BLOCK = 128
N_HEADS = 4
HEAD = 256
EPS = 1e-6

ADAM_LR = 0.001
ADAM_B1 = 0.9
ADAM_B2 = 0.999
ADAM_EPS = 1e-08
ADAM_WD = 0.01
ADAM_STEP = 10

V7X_VMEM_BYTES = 64 * 1024 * 1024
VMEM_LIMIT = V7X_VMEM_BYTES - 8 * 1024 * 1024
SMALL_ROWS = 616
LANES = 128

MESH = pl.DeviceIdType.MESH
ANY = pl.BlockSpec(memory_space=pl.ANY)
HBM = pl.BlockSpec(memory_space=pltpu.HBM)
SEM = pl.BlockSpec(memory_space=pltpu.SEMAPHORE)
EFFECT = pltpu.SideEffectType.DATAFLOW_SIDE_EFFECTING

NT_DIMS = (((1,), (1,)), ((), ()))
TN_DIMS = (((0,), (0,)), ((), ()))


def _params(n_axes):
    return pltpu.CompilerParams(dimension_semantics=("arbitrary",) * n_axes, vmem_limit_bytes=VMEM_LIMIT)


def _dot(a, b):
    return jnp.dot(a, b, preferred_element_type=F32)


def _dot_nt(a, b):
    return lax.dot_general(a, b, NT_DIMS, preferred_element_type=F32)


def _dot_tn(a, b):
    return lax.dot_general(a, b, TN_DIMS, preferred_element_type=F32)


def _mean(a):
    return jnp.mean(a, axis=-1, keepdims=True)


def _colsum(a):
    return jnp.sum(a, axis=0, keepdims=True)


_GELU_C0 = math.sqrt(2.0 / math.pi)
_GELU_C1 = 0.044715


def _gelu(a):
    t = jnp.tanh(a * (_GELU_C0 + (_GELU_C0 * _GELU_C1) * (a * a)))
    return a * (0.5 * t + 0.5)


def _gelu_and_grad(a):
    a2 = a * a
    t = jnp.tanh(a * (_GELU_C0 + (_GELU_C0 * _GELU_C1) * a2))
    cdf = 0.5 * t + 0.5
    grad = cdf + (0.5 * a) * (1.0 - t * t) * (_GELU_C0 + (3.0 * _GELU_C0 * _GELU_C1) * a2)
    return a * cdf, grad


def _sigmoid(a):
    return 0.5 * jnp.tanh(0.5 * a) + 0.5


def _window_sum(a, w, ahead):
    n = a.shape[0]
    step = 1
    while step < w:
        a = a + pltpu.roll(a, n - step if ahead else step, 0)
        step *= 2
    return a


def _rms_bwd(dn, n, r):
    return r * (dn - n * _mean(dn * n))


def _inproj(x, g1, w_in4, b_in, dep, tm):
    t = x.shape[0]
    cw = D_IN // N_SHARD

    def body(x_ref, g_ref, w_ref, b_ref, dep_ref, z_ref, xn_ref):
        @pl.when(pl.program_id(1) == 0)
        def _():
            xv = x_ref[...]
            r = lax.rsqrt(_mean(xv * xv) + EPS)
            xn_ref[...] = (xv * r * g_ref[...]).astype(BF16)

        z_ref[...] = _dot(xn_ref[...], w_ref[0]) + b_ref[...]

    return pl.pallas_call(
        body, name="inproj",
        grid=(t // tm, N_SHARD),
        in_specs=[pl.BlockSpec((tm, D), lambda i, s: (i, 0)),
                  pl.BlockSpec((1, D), lambda i, s: (0, 0)),
                  pl.BlockSpec((1, D, cw), lambda i, s: (s, 0, 0)),
                  pl.BlockSpec((1, cw), lambda i, s: (0, s)),
                  pl.BlockSpec(dep.shape, lambda i, s: (0, 0))],
        out_specs=[pl.BlockSpec((tm, cw), lambda i, s: (i, s)),
                   pl.BlockSpec((tm, D), lambda i, s: (i, 0))],
        out_shape=[jax.ShapeDtypeStruct((t, D_IN), F32), jax.ShapeDtypeStruct((t, D), BF16)],
        compiler_params=_params(2),
    )(x, g1, w_in4, b_in, dep)


def _sgu_forward(z_ref, lng_ref, lnb_ref, ws_ref, bsp_ref, sv_ref, tm, with_grad):
    if with_grad:
        u, du_dz = _gelu_and_grad(z_ref[:, D:2 * D])
        gv, dgv_dz = _gelu_and_grad(z_ref[:, 2 * D:3 * D])
    else:
        u = _gelu(z_ref[:, D:2 * D])
        gv = _gelu(z_ref[:, 2 * D:3 * D])
        du_dz = dgv_dz = None
    xc = gv - _mean(gv)
    rstd = lax.rsqrt(_mean(xc * xc) + EPS)
    vhat = xc * rstd
    vb = (vhat * lng_ref[...] + lnb_ref[...]).astype(BF16)
    for b in range(tm // SGU_BLOCK):
        rows = slice(b * SGU_BLOCK, (b + 1) * SGU_BLOCK)
        for h in range(N_HEADS):
            cols = slice(h * HEAD, (h + 1) * HEAD)
            sv_ref[rows, cols] = _dot(ws_ref[h], vb[rows, cols]) + bsp_ref[h]
    return u, du_dz, dgv_dz, vhat, rstd, vb


def _mix_fwd(z, x, w_pool, pool_scale, ln_g, ln_b, ws, bsp, w_proj, w_out, g1post, tm):
    t = x.shape[0]

    def body(z_ref, x_ref, wp_ref, psc_ref, lng_ref, lnb_ref, ws_ref, bsp_ref, wpr_ref, wo_ref, g1p_ref,
             pooled_ref, merged_ref, y_ref, h1_ref, win_ref, sv_ref):
        i = pl.program_id(0)

        @pl.when(i == 0)
        def _():
            win_ref[0:POOL_HALO, :] = jnp.zeros((POOL_HALO, D), F32)

        win_ref[POOL_HALO:POOL_HALO + tm, :] = z_ref[:, 0:D]
        pos1 = lax.broadcasted_iota(jnp.int32, (tm, 1), 0) + (i * tm + 1)
        a_parts = []
        for g, w in enumerate(POOL_WINDOWS):
            cols = slice(g * POOL_GROUP, (g + 1) * POOL_GROUP)
            ts = _window_sum(win_ref[:, cols], w, False)[POOL_HALO:POOL_HALO + tm, :]
            inv_cnt = 1.0 / jnp.minimum(pos1, w).astype(F32)
            pooled = (ts * inv_cnt - z_ref[:, cols]).astype(BF16)
            pooled_ref[:, cols] = pooled
            a_parts.append(_dot(pooled, wp_ref[g]) * psc_ref[:, cols])
        win_ref[0:POOL_HALO, :] = win_ref[tm:tm + POOL_HALO, :]

        u, _, _, _, _, _ = _sgu_forward(z_ref, lng_ref, lnb_ref, ws_ref, bsp_ref, sv_ref, tm, False)
        gated = (u * sv_ref[...]).astype(BF16)
        b_parts = [_dot(gated[:, h * HEAD:(h + 1) * HEAD], wpr_ref[h]) for h in range(N_HEADS)]

        a = jnp.concatenate(a_parts, axis=1)
        bbr = jnp.concatenate(b_parts, axis=1)
        merged = (_sigmoid(z_ref[:, 3 * D:4 * D]) * a + _sigmoid(z_ref[:, 4 * D:5 * D]) * bbr).astype(BF16)
        merged_ref[...] = merged
        y = _dot(merged, wo_ref[...])
        y_ref[...] = y
        r = lax.rsqrt(_mean(y * y) + EPS)
        h1_ref[...] = x_ref[...] + (y * r) * g1p_ref[...]

    row = lambda i: (i, 0)
    fixed2 = lambda i: (0, 0)
    fixed3 = lambda i: (0, 0, 0)
    vec = pl.BlockSpec((1, D), fixed2)
    return pl.pallas_call(
        body, name="mix_fwd",
        grid=(t // tm,),
        in_specs=[pl.BlockSpec((tm, D_IN), row), pl.BlockSpec((tm, D), row),
                  pl.BlockSpec((N_HEADS, POOL_GROUP, POOL_GROUP), fixed3), vec, vec, vec,
                  pl.BlockSpec((N_HEADS, SGU_BLOCK, SGU_BLOCK), fixed3),
                  pl.BlockSpec((N_HEADS, SGU_BLOCK, 1), fixed3),
                  pl.BlockSpec((N_HEADS, HEAD, HEAD), fixed3),
                  pl.BlockSpec((D, D), fixed2), vec],
        out_specs=[pl.BlockSpec((tm, D), row)] * 4,
        out_shape=[jax.ShapeDtypeStruct((t, D), BF16), jax.ShapeDtypeStruct((t, D), BF16),
                   jax.ShapeDtypeStruct((t, D), F32), jax.ShapeDtypeStruct((t, D), F32)],
        scratch_shapes=[pltpu.VMEM((tm + POOL_HALO, D), F32), pltpu.VMEM((tm, D), F32)],
        compiler_params=_params(1),
    )(z, x, w_pool, pool_scale, ln_g, ln_b, ws, bsp, w_proj, w_out, g1post)


def _ffn_fwd(h1, g2pre, w1, w2, g2post, target, tm):
    t = h1.shape[0]
    n_j = D_FF // D

    def body(h1_ref, g2_ref, w1_ref, w2_ref, g2p_ref, tgt_ref,
             hn_ref, f1_ref, f_ref, df2_ref, dout_ref, vec_ref, acc_ref):
        i, j = pl.program_id(0), pl.program_id(1)

        @pl.when((i == 0) & (j == 0))
        def _():
            vec_ref[...] = jnp.zeros_like(vec_ref)

        @pl.when(j == 0)
        def _():
            h = h1_ref[...]
            r = lax.rsqrt(_mean(h * h) + EPS)
            hn_ref[...] = (h * r * g2_ref[...]).astype(BF16)
            acc_ref[...] = jnp.zeros_like(acc_ref)

        f1 = _dot(hn_ref[...], w1_ref[0])
        f1_ref[...] = f1
        rl = jnp.maximum(f1, 0.0)
        f = (rl * rl).astype(BF16)
        f_ref[...] = f
        acc_ref[...] += _dot(f, w2_ref[...])

        @pl.when(j == n_j - 1)
        def _():
            f2 = acc_ref[...]
            r = lax.rsqrt(_mean(f2 * f2) + EPS)
            n2 = f2 * r
            err = h1_ref[...] + n2 * g2p_ref[...] - tgt_ref[...]
            loss = 0.5 * jnp.sum(_mean(err * err))
            dout = err * (1.0 / D)
            dout_ref[...] = dout
            vec_ref[0:1, :] += _colsum(dout * n2)
            vec_ref[1:2, :] += jnp.full((1, D), loss, F32)
            df2_ref[...] = _rms_bwd(dout * g2p_ref[...], n2, r).astype(BF16)

    row = lambda i, j: (i, 0)
    fixed = lambda i, j: (0, 0)
    vec = pl.BlockSpec((1, D), fixed)
    return pl.pallas_call(
        body, name="ffn_fwd",
        grid=(t // tm, n_j),
        in_specs=[pl.BlockSpec((tm, D), row), vec,
                  pl.BlockSpec((1, D, D), lambda i, j: (j, 0, 0)),
                  pl.BlockSpec((D, D), lambda i, j: (j, 0)), vec,
                  pl.BlockSpec((tm, D), row)],
        out_specs=[pl.BlockSpec((tm, D), row),
                   pl.BlockSpec((tm, D), lambda i, j: (i, j)),
                   pl.BlockSpec((tm, D), lambda i, j: (i, j)),
                   pl.BlockSpec((tm, D), row), pl.BlockSpec((tm, D), row),
                   pl.BlockSpec((8, D), fixed)],
        out_shape=[jax.ShapeDtypeStruct((t, D), BF16), jax.ShapeDtypeStruct((t, D_FF), F32),
                   jax.ShapeDtypeStruct((t, D_FF), BF16), jax.ShapeDtypeStruct((t, D), BF16),
                   jax.ShapeDtypeStruct((t, D), F32), jax.ShapeDtypeStruct((8, D), F32)],
        scratch_shapes=[pltpu.VMEM((tm, D), F32)],
        compiler_params=_params(2),
    )(h1, g2pre, w1, w2, g2post, target)


def _ffn_bwd(df2, f1, w1, w2, h1, dout, y, g2pre, g1post, tm):
    t = h1.shape[0]
    n_j = D_FF // D

    def body(df2_ref, f1_ref, w1_ref, w2_ref, h1_ref, dout_ref, y_ref, g2_ref, g1p_ref,
             df1_ref, dh1_ref, dy_ref, vec_ref, acc_ref):
        i, j = pl.program_id(0), pl.program_id(1)

        @pl.when((i == 0) & (j == 0))
        def _():
            vec_ref[...] = jnp.zeros_like(vec_ref)

        @pl.when(j == 0)
        def _():
            acc_ref[...] = jnp.zeros_like(acc_ref)

        df = _dot_nt(df2_ref[...], w2_ref[...])
        df1 = (df * (2.0 * jnp.maximum(f1_ref[...], 0.0))).astype(BF16)
        df1_ref[...] = df1
        acc_ref[...] += _dot_nt(df1, w1_ref[0])

        @pl.when(j == n_j - 1)
        def _():
            dhn = acc_ref[...]
            h = h1_ref[...]
            r2 = lax.rsqrt(_mean(h * h) + EPS)
            nh = h * r2
            vec_ref[0:1, :] += _colsum(dhn * nh)
            dh1 = dout_ref[...] + _rms_bwd(dhn * g2_ref[...], nh, r2)
            dh1_ref[...] = dh1
            yv = y_ref[...]
            r1 = lax.rsqrt(_mean(yv * yv) + EPS)
            ny = yv * r1
            vec_ref[1:2, :] += _colsum(dh1 * ny)
            dy_ref[...] = _rms_bwd(dh1 * g1p_ref[...], ny, r1).astype(BF16)

    row = lambda i, j: (i, 0)
    fixed = lambda i, j: (0, 0)
    vec = pl.BlockSpec((1, D), fixed)
    return pl.pallas_call(
        body, name="ffn_bwd",
        grid=(t // tm, n_j),
        in_specs=[pl.BlockSpec((tm, D), row),
                  pl.BlockSpec((tm, D), lambda i, j: (i, j)),
                  pl.BlockSpec((1, D, D), lambda i, j: (j, 0, 0)),
                  pl.BlockSpec((D, D), lambda i, j: (j, 0)),
                  pl.BlockSpec((tm, D), row), pl.BlockSpec((tm, D), row), pl.BlockSpec((tm, D), row), vec, vec],
        out_specs=[pl.BlockSpec((tm, D), lambda i, j: (i, j)),
                   pl.BlockSpec((tm, D), row), pl.BlockSpec((tm, D), row),
                   pl.BlockSpec((8, D), fixed)],
        out_shape=[jax.ShapeDtypeStruct((t, D_FF), BF16), jax.ShapeDtypeStruct((t, D), F32),
                   jax.ShapeDtypeStruct((t, D), BF16), jax.ShapeDtypeStruct((8, D), F32)],
        scratch_shapes=[pltpu.VMEM((tm, D), F32)],
        compiler_params=_params(2),
    )(df2, f1, w1, w2, h1, dout, y, g2pre, g1post)


def _mix_bwd(z, pooled, dy, w_pool, pool_scale, ln_g, ln_b, ws, ws_t, bsp, w_proj, w_out, mask, tm):
    t = z.shape[0]
    n_t = t // tm

    def body(z_ref, pooled_ref, dy_ref, wp_ref, psc_ref, lng_ref, lnb_ref, ws_ref, wst_ref, bsp_ref, wpr_ref,
             wo_ref, mask_ref,
             dz_ref, dwp_ref, dwpr_ref, dws_ref, dbsp_ref, vec_ref, dbin_ref,
             win_ref, sv_ref, dsv_ref, dv_ref, dsvsum_ref):
        i = pl.program_id(0)

        @pl.when(i == 0)
        def _():
            win_ref[tm:tm + POOL_HALO, :] = jnp.zeros((POOL_HALO, D), F32)
            dwp_ref[...] = jnp.zeros_like(dwp_ref)
            dwpr_ref[...] = jnp.zeros_like(dwpr_ref)
            dws_ref[...] = jnp.zeros_like(dws_ref)
            vec_ref[...] = jnp.zeros_like(vec_ref)
            dbin_ref[...] = jnp.zeros_like(dbin_ref)
            dsvsum_ref[...] = jnp.zeros_like(dsvsum_ref)

        dmerged = _dot_nt(dy_ref[...], wo_ref[...])

        u, du_dz, dgv_dz, vhat, rstd, vb = _sgu_forward(z_ref, lng_ref, lnb_ref, ws_ref, bsp_ref, sv_ref, tm, True)
        sv = sv_ref[...]
        gated = (u * sv).astype(BF16)

        sa = _sigmoid(z_ref[:, 3 * D:4 * D])
        sb = _sigmoid(z_ref[:, 4 * D:5 * D])
        da = dmerged * sa
        dbbr = (dmerged * sb).astype(BF16)

        tile = n_t - 1 - i
        pos1 = lax.broadcasted_iota(jnp.int32, (tm, 1), 0) + (tile * tm + 1)
        dzga_parts, dpooled_parts = [], []
        for g, w in enumerate(POOL_WINDOWS):
            cols = slice(g * POOL_GROUP, (g + 1) * POOL_GROUP)
            pg = pooled_ref[:, cols]
            a_pre = _dot(pg, wp_ref[g])
            da_g = da[:, cols]
            psc = psc_ref[:, cols]
            vec_ref[0:1, cols] += _colsum(da_g * a_pre)
            sa_g = sa[:, cols]
            dzga_parts.append(dmerged[:, cols] * (a_pre * psc) * (sa_g * (1.0 - sa_g)))
            da_pre = (da_g * psc).astype(BF16)
            dwp_ref[g] += _dot_tn(pg, da_pre)
            dpooled = _dot_nt(da_pre, wp_ref[g])
            dpooled_parts.append(dpooled)
            inv_cnt = 1.0 / jnp.minimum(pos1, w).astype(F32)
            win_ref[0:tm, cols] = dpooled * inv_cnt
        dzp_parts = []
        for g, w in enumerate(POOL_WINDOWS):
            cols = slice(g * POOL_GROUP, (g + 1) * POOL_GROUP)
            acc = _window_sum(win_ref[:, cols], w, True)[0:tm, :]
            dzp_parts.append(acc - dpooled_parts[g])
        win_ref[tm:tm + POOL_HALO, :] = win_ref[0:POOL_HALO, :]

        dzgb_parts = []
        for h in range(N_HEADS):
            cols = slice(h * HEAD, (h + 1) * HEAD)
            g_h = gated[:, cols]
            db_h = dbbr[:, cols]
            bbr_h = _dot(g_h, wpr_ref[h])
            sb_h = sb[:, cols]
            dzgb_parts.append(dmerged[:, cols] * bbr_h * (sb_h * (1.0 - sb_h)))
            dwpr_ref[h] += _dot_tn(g_h, db_h)
            dgated_h = _dot_nt(db_h, wpr_ref[h])
            dsv_ref[:, cols] = dgated_h * u[:, cols]
            sv_ref[:, cols] = dgated_h * sv[:, cols]
        dzu = sv_ref[...] * du_dz
        dsv = dsv_ref[...]
        dsvb = dsv.astype(BF16)
        blk_sum = dsv[0:SGU_BLOCK, :]
        for b in range(1, tm // SGU_BLOCK):
            blk_sum = blk_sum + dsv[b * SGU_BLOCK:(b + 1) * SGU_BLOCK, :]
        dsvsum_ref[...] += blk_sum
        for b in range(tm // SGU_BLOCK):
            rows = slice(b * SGU_BLOCK, (b + 1) * SGU_BLOCK)
            for h in range(N_HEADS):
                cols = slice(h * HEAD, (h + 1) * HEAD)
                dws_ref[h] += _dot_nt(dsvb[rows, cols], vb[rows, cols])
                dv_ref[rows, cols] = _dot(wst_ref[h], dsvb[rows, cols])
        dv = dv_ref[...]
        vec_ref[1:2, :] += _colsum(dv * vhat)
        vec_ref[2:3, :] += _colsum(dv)
        dvhat = dv * lng_ref[...]
        dgv = rstd * (dvhat - _mean(dvhat) - vhat * _mean(dvhat * vhat))
        dzv = dgv * dgv_dz

        parts = [jnp.concatenate(dzp_parts, axis=1), dzu, dzv,
                 jnp.concatenate(dzga_parts, axis=1), jnp.concatenate(dzgb_parts, axis=1)]
        for k, part in enumerate(parts):
            dz_ref[:, k * D:(k + 1) * D] = part.astype(BF16)
            dbin_ref[0:1, k * D:(k + 1) * D] += _colsum(part)

        @pl.when(i == n_t - 1)
        def _():
            for h in range(N_HEADS):
                dws_ref[h] = jnp.where(mask_ref[...] > 0.0, dws_ref[h], 0.0)
                tot = jnp.sum(dsvsum_ref[:, h * HEAD:(h + 1) * HEAD], axis=1, keepdims=True)
                dbsp_ref[:, h * LANES:(h + 1) * LANES] = jnp.broadcast_to(tot, (SGU_BLOCK, LANES))

    row = lambda i: (n_t - 1 - i, 0)
    fixed2 = lambda i: (0, 0)
    fixed3 = lambda i: (0, 0, 0)
    vec = pl.BlockSpec((1, D), fixed2)
    sq = pl.BlockSpec((N_HEADS, SGU_BLOCK, SGU_BLOCK), fixed3)
    grp = pl.BlockSpec((N_HEADS, HEAD, HEAD), fixed3)
    return pl.pallas_call(
        body, name="mix_bwd",
        grid=(n_t,),
        in_specs=[pl.BlockSpec((tm, D_IN), row), pl.BlockSpec((tm, D), row), pl.BlockSpec((tm, D), row),
                  grp, vec, vec, vec, sq, sq, pl.BlockSpec((N_HEADS, SGU_BLOCK, 1), fixed3), grp,
                  pl.BlockSpec((D, D), fixed2), pl.BlockSpec((SGU_BLOCK, SGU_BLOCK), fixed2)],
        out_specs=[pl.BlockSpec((tm, D_IN), row), grp, grp, sq,
                   pl.BlockSpec((SGU_BLOCK, N_HEADS * LANES), fixed2),
                   pl.BlockSpec((8, D), fixed2), pl.BlockSpec((8, D_IN), fixed2)],
        out_shape=[jax.ShapeDtypeStruct((t, D_IN), BF16),
                   jax.ShapeDtypeStruct((N_HEADS, HEAD, HEAD), F32), jax.ShapeDtypeStruct((N_HEADS, HEAD, HEAD), F32),
                   jax.ShapeDtypeStruct((N_HEADS, SGU_BLOCK, SGU_BLOCK), F32),
                   jax.ShapeDtypeStruct((SGU_BLOCK, N_HEADS * LANES), F32),
                   jax.ShapeDtypeStruct((8, D), F32), jax.ShapeDtypeStruct((8, D_IN), F32)],
        scratch_shapes=[pltpu.VMEM((tm + POOL_HALO, D), F32), pltpu.VMEM((tm, D), F32), pltpu.VMEM((tm, D), F32),
                        pltpu.VMEM((tm, D), F32), pltpu.VMEM((SGU_BLOCK, D), F32)],
        compiler_params=_params(1),
    )(z, pooled, dy, w_pool, pool_scale, ln_g, ln_b, ws, ws_t, bsp, w_proj, w_out, mask)


def _dx(dz, w_in4, x, dh1, g1, dep, tm):
    t = x.shape[0]
    cw = D_IN // N_SHARD

    def body(dz_ref, w_ref, x_ref, dh1_ref, g_ref, dep_ref, gx_ref, vec_ref, acc_ref):
        i, s = pl.program_id(0), pl.program_id(1)

        @pl.when((i == 0) & (s == 0))
        def _():
            vec_ref[...] = jnp.zeros_like(vec_ref)

        @pl.when(s == 0)
        def _():
            acc_ref[...] = jnp.zeros_like(acc_ref)

        acc_ref[...] += _dot_nt(dz_ref[...], w_ref[0])

        @pl.when(s == N_SHARD - 1)
        def _():
            dxn = acc_ref[...]
            xv = x_ref[...]
            r = lax.rsqrt(_mean(xv * xv) + EPS)
            xh = xv * r
            vec_ref[0:1, :] += _colsum(dxn * xh)
            gx_ref[...] = dh1_ref[...] + _rms_bwd(dxn * g_ref[...], xh, r)

    row = lambda i, s: (i, 0)
    fixed = lambda i, s: (0, 0)
    return pl.pallas_call(
        body, name="dx",
        grid=(t // tm, N_SHARD),
        in_specs=[pl.BlockSpec((tm, cw), lambda i, s: (i, s)),
                  pl.BlockSpec((1, D, cw), lambda i, s: (s, 0, 0)),
                  pl.BlockSpec((tm, D), row), pl.BlockSpec((tm, D), row), pl.BlockSpec((1, D), fixed),
                  pl.BlockSpec(dep.shape, fixed)],
        out_specs=[pl.BlockSpec((tm, D), row), pl.BlockSpec((8, D), fixed)],
        out_shape=[jax.ShapeDtypeStruct((t, D), F32), jax.ShapeDtypeStruct((8, D), F32)],
        scratch_shapes=[pltpu.VMEM((tm, D), F32)],
        compiler_params=_params(2),
    )(dz, w_in4, x, dh1, g1, dep)


def _dw(a, b, name, n_split, tn, tk, n_out, out_block, dep=None):
    t, m = a.shape
    n = b.shape[1]
    tm = min(m, D)
    rows = tm // n_split
    n_k = t // tk

    def body(a_ref, b_ref, *rest):
        o_ref, acc_ref = rest[-2:]
        k = pl.program_id(2)

        @pl.when(k == 0)
        def _():
            acc_ref[...] = jnp.zeros_like(acc_ref)

        acc_ref[...] += _dot_tn(a_ref[...], b_ref[...])

        @pl.when(k == n_k - 1)
        def _():
            for h in range(n_split):
                o_ref[h, 0] = acc_ref[h * rows:(h + 1) * rows, :].astype(BF16)

    return pl.pallas_call(
        body, name=name,
        grid=(m // tm, n // tn, n_k),
        in_specs=[pl.BlockSpec((tk, tm), lambda i, j, k: (k, i)),
                  pl.BlockSpec((tk, tn), lambda i, j, k: (k, j))]
        + ([] if dep is None else [pl.BlockSpec(dep.shape, lambda i, j, k: (0, 0))]),
        out_specs=pl.BlockSpec((n_split, 1, rows, tn), lambda i, j, k: (0, out_block(i, j), 0, 0)),
        out_shape=jax.ShapeDtypeStruct((n_split, n_out, rows, tn), BF16),
        scratch_shapes=[pltpu.VMEM((tm, tn), F32)],
        compiler_params=_params(3),
    )(*((a, b) if dep is None else (a, b, dep)))


def _row_block(rows, cols, itemsize, n_bufs):
    budget = VMEM_LIMIT // 4
    rb = rows
    while rb % 16 == 0 and rb * cols * itemsize * n_bufs * 2 > budget:
        rb //= 2
    return rb


def _chip_sum(core, g, recv, name):
    _, n_s, rh, cols = g.shape
    rb = _row_block(rh, cols, 4, 3)

    def body(c_ref, g_ref, r_ref, o_ref):
        o_ref[...] = (g_ref[0].astype(F32) + r_ref[...].astype(F32)).astype(BF16)

    return pl.pallas_call(
        body, name=name,
        grid_spec=pltpu.PrefetchScalarGridSpec(
            num_scalar_prefetch=1, grid=(n_s, rh // rb),
            in_specs=[pl.BlockSpec((1, 1, rb, cols), lambda s, r, c: (c[0], s, r, 0)),
                      pl.BlockSpec((1, rb, cols), lambda s, r, c: (s, r, 0))],
            out_specs=pl.BlockSpec((1, rb, cols), lambda s, r, c: (s, r, 0))),
        out_shape=jax.ShapeDtypeStruct((n_s, rh, cols), BF16),
        compiler_params=_params(2),
    )(core, g, recv)


def _slot_sum(slots, name):
    n_s, rows, cols = slots.shape
    rb = _row_block(rows, cols, 4, n_s + 1)

    def body(s_ref, o_ref):
        acc = s_ref[0].astype(F32)
        for k in range(1, n_s):
            acc = acc + s_ref[k].astype(F32)
        o_ref[...] = acc

    return pl.pallas_call(
        body, name=name,
        grid=(rows // rb,),
        in_specs=[pl.BlockSpec((n_s, rb, cols), lambda r: (0, r, 0))],
        out_specs=pl.BlockSpec((rb, cols), lambda r: (r, 0)),
        out_shape=jax.ShapeDtypeStruct((rows, cols), F32),
        compiler_params=_params(1),
    )(slots)


def _adamw(w, g, m, v, name):
    rows, cols = w.shape
    rb = _row_block(rows, cols, 4, 7)
    c1 = 1.0 - ADAM_B1 ** ADAM_STEP
    c2 = 1.0 - ADAM_B2 ** ADAM_STEP

    def body(w_ref, g_ref, m_ref, v_ref, d_ref, mo_ref, vo_ref):
        gv = g_ref[...]
        m_new = ADAM_B1 * m_ref[...] + (1.0 - ADAM_B1) * gv
        v_new = ADAM_B2 * v_ref[...] + (1.0 - ADAM_B2) * (gv * gv)
        mo_ref[...] = m_new
        vo_ref[...] = v_new
        d_ref[...] = -ADAM_LR * ((m_new / c1) / (jnp.sqrt(v_new / c2) + ADAM_EPS) + ADAM_WD * w_ref[...])

    blk = pl.BlockSpec((rb, cols), lambda r: (r, 0))
    return pl.pallas_call(
        body, name=name,
        grid=(rows // rb,),
        in_specs=[blk] * 4, out_specs=[blk] * 3,
        out_shape=[jax.ShapeDtypeStruct((rows, cols), F32)] * 3,
        compiler_params=_params(1),
    )(w, g, m, v)


def _final_sum(chip, own, slots, name):
    n_s, rh, cols = own.shape
    rb = _row_block(rh, cols, 4, n_s + 1)

    def body(chip_ref, own_ref, s1_ref, s2_ref, s3_ref, o_ref):
        acc = own_ref[0].astype(F32) + s1_ref[0].astype(F32)
        acc = acc + s2_ref[0].astype(F32)
        o_ref[...] = acc + s3_ref[0].astype(F32)

    def slot(k):
        return pl.BlockSpec((1, rb, cols), lambda r, chip: (lax.rem(chip[0] + k, n_s), r, 0))

    return pl.pallas_call(
        body, name=name,
        grid_spec=pltpu.PrefetchScalarGridSpec(
            num_scalar_prefetch=1, grid=(rh // rb,),
            in_specs=[slot(0), slot(1), slot(2), slot(3)],
            out_specs=pl.BlockSpec((rb, cols), lambda r, chip: (r, 0))),
        out_shape=jax.ShapeDtypeStruct((rh, cols), F32),
        compiler_params=_params(1),
    )(chip, own, slots, slots, slots)


def _adamw_halves(core, own, sibling, w, m, v, name):
    rows, cols = w.shape
    rh = rows // 2
    rb = _row_block(rh, cols, 4, 9)
    n_r = rh // rb
    c1 = 1.0 - ADAM_B1 ** ADAM_STEP
    c2 = 1.0 - ADAM_B2 ** ADAM_STEP

    def body(c_ref, own_ref, sib_ref, w_ref, m_ref, v_ref, g_ref, d_ref, mo_ref, vo_ref):
        mine = c_ref[0] == pl.program_id(0)

        @pl.when(mine)
        def _():
            g_ref[...] = own_ref[...]

        @pl.when(jnp.logical_not(mine))
        def _():
            g_ref[...] = sib_ref[...]

        gv = g_ref[...]
        m_new = ADAM_B1 * m_ref[...] + (1.0 - ADAM_B1) * gv
        v_new = ADAM_B2 * v_ref[...] + (1.0 - ADAM_B2) * (gv * gv)
        mo_ref[...] = m_new
        vo_ref[...] = v_new
        d_ref[...] = -ADAM_LR * ((m_new / c1) / (jnp.sqrt(v_new / c2) + ADAM_EPS) + ADAM_WD * w_ref[...])

    half = pl.BlockSpec((rb, cols), lambda h, r, c: (r, 0))
    whole = pl.BlockSpec((rb, cols), lambda h, r, c: (h * n_r + r, 0))
    return pl.pallas_call(
        body, name=name,
        grid_spec=pltpu.PrefetchScalarGridSpec(
            num_scalar_prefetch=1, grid=(2, n_r),
            in_specs=[half, half, whole, whole, whole], out_specs=[whole] * 4),
        out_shape=[jax.ShapeDtypeStruct((rows, cols), F32)] * 4,
        compiler_params=_params(2),
    )(core, own, sibling, w, m, v)


def _place():
    x, y, c = lax.axis_index("x"), lax.axis_index("y"), lax.axis_index("c")
    chips = [(1 - x, y), (x, 1 - y), (1 - x, 1 - y)]
    return x, y, c, chips


def _remote(src, dst, send_sem, recv_sem, device):
    return pltpu.make_async_remote_copy(src_ref=src, dst_ref=dst, send_sem=send_sem, recv_sem=recv_sem,
                                        device_id=device, device_id_type=MESH)


def _all_gather_weights(shards):
    n = len(shards)

    def body(*refs):
        ins, outs = refs[:n], refs[n:2 * n]
        send_ici, recv_ici, send_d2d, recv_d2d, send_own, recv_own = refs[2 * n:]
        x, y, c, chips = _place()
        me = 2 * x + y
        sibling = (x, y, 1 - c)
        local = [_remote(ins[w], outs[w].at[me], send_own.at[w], recv_own.at[w], sibling) for w in range(n)]
        for cp in local:
            cp.start()
        sends = []
        for j, (px, py) in enumerate(chips):
            for w in range(n):
                cp = _remote(ins[w].at[c], outs[w].at[me, c], send_ici.at[w, j], recv_ici.at[w, j], (px, py, c))
                cp.start()
                sends.append(cp)
        for j, (px, py) in enumerate(chips):
            src_chip = 2 * px + py
            for w in range(n):
                landed = outs[w].at[src_chip, c]
                _remote(landed, landed, send_ici.at[w, j], recv_ici.at[w, j], (px, py, c)).wait_recv()
                cp = _remote(landed, landed, send_d2d.at[w, j], recv_d2d.at[w, j], sibling)
                cp.start()
                sends.append(cp)
        for j, (px, py) in enumerate(chips):
            src_chip = 2 * px + py
            for w in range(n):
                passed = outs[w].at[src_chip, 1 - c]
                _remote(passed, passed, send_d2d.at[w, j], recv_d2d.at[w, j], sibling).wait_recv()
        for cp in sends:
            cp.wait_send()
        for cp in local:
            cp.wait()

    return pl.pallas_call(
        body, name="ag_weights",
        in_specs=[ANY] * n, out_specs=[ANY] * n,
        out_shape=[jax.ShapeDtypeStruct((N_SHARD,) + s.shape, s.dtype) for s in shards],
        scratch_shapes=[pltpu.SemaphoreType.DMA((n, N_CHIP_PEERS))] * 4 + [pltpu.SemaphoreType.DMA((n,))] * 2,
    )(*shards)


def _split_start(name, arrays, n_copies, plan, after=None):
    n = len(arrays)
    n_in = n + (after is not None)

    def body(*refs):
        ins, send_sem, recv_sem, token = refs[:n], refs[n_in], refs[n_in + 1], refs[-1]
        for k, (src, dst, device) in enumerate(plan(ins)):
            _remote(src, dst, send_sem.at[k], recv_sem.at[k], device).start()
        token[...] = jnp.zeros_like(token)

    outs = pl.pallas_call(
        body, name=name,
        in_specs=[HBM] * n + [ANY] * (n_in - n),
        out_specs=(SEM, SEM, *[HBM] * n, pl.BlockSpec(memory_space=pltpu.VMEM)),
        out_shape=(pltpu.SemaphoreType.DMA((n_copies,)), pltpu.SemaphoreType.DMA((n_copies,)),
                   *[pltpu.HBM(a.shape, a.dtype) for a in arrays], jax.ShapeDtypeStruct((8, LANES), F32)),
        input_output_aliases={i: i + 2 for i in range(n)},
        compiler_params=pltpu.CompilerParams(has_side_effects=EFFECT),
    )(*[pltpu.with_memory_space_constraint(a, pltpu.HBM) for a in arrays], *([] if after is None else [after]))
    return outs[0], outs[1], list(outs[2:2 + n]), outs[-1]


def _split_wait(name, send_sem, recv_sem, arrays, plan, *after):
    n = len(arrays)

    def body(*refs):
        ins, send_ref, recv_ref = refs[:n], refs[n], refs[n + 1]
        for k, (src, dst, device) in enumerate(plan(ins)):
            cp = _remote(src, dst, send_ref.at[k], recv_ref.at[k], device)
            cp.wait_send()
            cp.wait_recv()

    outs = pl.pallas_call(
        body, name=name,
        in_specs=[HBM] * n + [SEM, SEM] + [ANY] * len(after), out_specs=[HBM] * n,
        out_shape=[pltpu.HBM(a.shape, a.dtype) for a in arrays],
        input_output_aliases={i: i for i in range(n)},
        compiler_params=pltpu.CompilerParams(has_side_effects=EFFECT),
    )(*arrays, send_sem, recv_sem, *after)
    return list(outs)


def _plan_gather_send(n):
    def plan(refs):
        x, y, c, chips = _place()
        me = 2 * x + y
        copies = []
        for w in range(n):
            own, full = refs[w], refs[n + w]
            copies.append((own, full.at[me], (x, y, 1 - c)))
            copies += [(own.at[c], full.at[me, c], (px, py, c)) for px, py in chips]
        return copies
    return plan


def _plan_gather_direct(n):
    def plan(refs):
        x, y, c, chips = _place()
        me = 2 * x + y
        copies = []
        for w in range(n):
            own, full = refs[w], refs[n + w]
            copies.append((own, full.at[me], (x, y, 1 - c)))
            for px, py in chips:
                copies += [(own.at[c], full.at[me, c], (px, py, c)), (own.at[c], full.at[me, c], (px, py, 1 - c))]
        return copies
    return plan


def _plan_gather_pass(n):
    def plan(refs):
        x, y, c, chips = _place()
        copies = []
        for w in range(n):
            for px, py in chips:
                landed = refs[w].at[2 * px + py, c]
                copies.append((landed, landed, (x, y, 1 - c)))
        return copies
    return plan


def _plan_exchange(n, with_small):
    def plan(refs):
        x, y, c, chips = _place()
        me = 2 * x + y
        n_in = n + int(with_small)
        copies = []
        for w in range(n):
            copies += [(refs[w].at[2 * px + py], refs[n_in + w].at[me], (px, py, c)) for px, py in chips]
        if with_small:
            small, slots = refs[n], refs[n_in + n]
            copies.append((small, slots.at[me], (x, y, 1 - c)))
            copies += [(small, slots.at[me], (px, py, c)) for px, py in chips]
        return copies
    return plan


def _swap_halves(name, grads, wholes):
    n = len(grads)
    arrays = list(grads) + list(wholes)
    n_all = len(arrays)

    def body(*refs):
        ins, outs = refs[:n_all], refs[n_all:2 * n_all]
        send_sem, recv_sem = refs[2 * n_all:]
        x, y, c, _ = _place()
        sibling = (x, y, 1 - c)
        copies = []
        for w in range(n_all):
            src = ins[w].at[1 - c] if w < n else ins[w]
            cp = _remote(src, outs[w], send_sem.at[w], recv_sem.at[w], sibling)
            cp.start()
            copies.append(cp)
        for cp in copies:
            cp.wait()

    return pl.pallas_call(
        body, name=name,
        in_specs=[ANY] * n_all, out_specs=[ANY] * n_all,
        out_shape=[jax.ShapeDtypeStruct(g.shape[1:], g.dtype) for g in grads]
        + [jax.ShapeDtypeStruct(a.shape, a.dtype) for a in wholes],
        scratch_shapes=[pltpu.SemaphoreType.DMA((n_all,))] * 2,
    )(*arrays)


def _all_reduce_tile(part):
    n_dev = 8

    def body(part_ref, sum_ref, slots_ref, send_sem, recv_sem):
        x, y, c, _ = _place()
        me = 4 * x + 2 * y + c
        slots_ref[me] = part_ref[...]
        flips = [(fx, fy, fc) for fx in (0, 1) for fy in (0, 1) for fc in (0, 1)][1:]
        copies = []
        for k, (fx, fy, fc) in enumerate(flips):
            peer = (x + fx - 2 * x * fx, y + fy - 2 * y * fy, c + fc - 2 * c * fc)
            cp = _remote(part_ref, slots_ref.at[me], send_sem.at[k], recv_sem.at[k], peer)
            cp.start()
            copies.append(cp)
        for cp in copies:
            cp.wait()
        acc = slots_ref[0]
        for k in range(1, n_dev):
            acc = acc + slots_ref[k]
        sum_ref[...] = acc

    return pl.pallas_call(
        body, name="all_reduce_tile",
        in_specs=[pl.BlockSpec(memory_space=pltpu.VMEM)],
        out_specs=pl.BlockSpec(memory_space=pltpu.VMEM),
        out_shape=jax.ShapeDtypeStruct(part.shape, F32),
        scratch_shapes=[pltpu.VMEM((n_dev,) + part.shape, F32),
                        pltpu.SemaphoreType.DMA((n_dev - 1,)), pltpu.SemaphoreType.DMA((n_dev - 1,))],
    )(part)


_SMALL = ("norm1_pre_g", "b_in", "pool_scale", "sgu_ln_g", "sgu_ln_b", "w_spatial", "b_spatial",
          "norm1_post_g", "norm2_pre_g", "norm2_post_g")
_MIX = ("w_in", "w_pool", "w_sgu_proj", "w_out")
_FF = ("w_ff1", "w_ff2")
_BIG = _MIX + _FF
_ORDER = ("norm1_pre_g", "w_in", "b_in", "w_pool", "pool_scale", "sgu_ln_g", "sgu_ln_b", "w_spatial", "b_spatial",
          "w_sgu_proj", "w_out", "norm1_post_g", "norm2_pre_g", "w_ff1", "w_ff2", "norm2_post_g")


def _pack(keys, rows, parts):
    flat = jnp.concatenate([parts[k].reshape(-1).astype(F32) for k in keys])
    flat = jnp.pad(flat, (0, rows * LANES - flat.shape[0]))
    return flat.reshape(rows, LANES)


def _pack_small(parts):
    return _pack(_SMALL, SMALL_ROWS, parts)


def _unpack_small(packed, like):
    flat = packed.reshape(-1)
    out, off = {}, 0
    for k in _SMALL:
        size = like[k].size
        out[k] = flat[off:off + size].reshape(like[k].shape)
        off += size
    return out


def _halves(a):
    return a.reshape(2, a.shape[0] // 2, a.shape[1])


def _step(p, m, v, x, target):
    t = x.shape[0]
    tm_mm = min(1024, t)
    tm_mix = min(256, t)
    tm_ffn = min(512, t)
    core = lax.axis_index("c").astype(jnp.int32).reshape(1)
    row = lambda a: a.reshape(1, -1)

    chip = (2 * lax.axis_index("x") + lax.axis_index("y")).astype(jnp.int32).reshape(1)

    local2d = {"w_in": p["w_in"], "w_ff1": p["w_ff1"], "w_ff2": p["w_ff2"], "w_out": p["w_out"],
               "w_pool": p["w_pool"].reshape(N_HEADS * 64, POOL_GROUP),
               "w_sgu_proj": p["w_sgu_proj"].reshape(N_HEADS * 64, HEAD)}
    shard = {k: _halves(local2d[k].astype(BF16)) for k in _BIG}
    landing = lambda keys: [lax.empty((N_SHARD,) + shard[k].shape, BF16) for k in keys]
    w_in4 = _all_gather_weights([shard["w_in"]])[0].reshape(N_SHARD, D, D_IN // N_SHARD)
    mix_keys = _MIX[1:]
    mix_arrays = [shard[k] for k in mix_keys] + landing(mix_keys)
    send0, recv0, mix_arrays, token = _split_start("ag_mix_send", mix_arrays, 7 * len(mix_keys),
                                                   _plan_gather_direct(len(mix_keys)), after=w_in4)
    ff_arrays = [shard[k] for k in _FF] + landing(_FF)
    send1, recv1, ff_arrays, token = _split_start("ag_ff_send", ff_arrays, 4 * len(_FF), _plan_gather_send(len(_FF)),
                                                  after=token)

    pos = jnp.arange(SGU_BLOCK) // 64
    mask = (pos[:, None] >= pos[None, :]).astype(F32)
    ws = (p["w_spatial"] * mask[None]).astype(BF16)
    ws_t = jnp.swapaxes(ws, 1, 2)
    bsp = p["b_spatial"].reshape(N_HEADS, SGU_BLOCK, 1)

    z, xn = _inproj(x, row(p["norm1_pre_g"]), w_in4, row(p["b_in"]), token, tm_mm)
    mix_arrays = _split_wait("ag_mix_land", send0, recv0, mix_arrays, _plan_gather_direct(len(mix_keys)), z)
    full = dict(zip(mix_keys, mix_arrays[len(mix_keys):]))
    w_out = full["w_out"].reshape(D, D)
    regroup = lambda a: a.reshape(N_SHARD, N_HEADS, 64, 256).transpose(1, 0, 2, 3).reshape(N_HEADS, 256, 256)
    w_pool = regroup(full["w_pool"])
    w_proj = regroup(full["w_sgu_proj"])
    pooled, merged, y, h1 = _mix_fwd(z, x, w_pool, row(p["pool_scale"]), row(p["sgu_ln_g"]), row(p["sgu_ln_b"]),
                                     ws, bsp, w_proj, w_out, row(p["norm1_post_g"]), tm_mix)
    ff_arrays = _split_wait("ag_ff_land", send1, recv1, ff_arrays, _plan_gather_send(len(_FF)), pooled)
    send2, recv2, ff_full, token = _split_start("ag_ff_pass", ff_arrays[len(_FF):], 3 * len(_FF),
                                                _plan_gather_pass(len(_FF)))
    ff_full = _split_wait("ag_ff_passed", send2, recv2, ff_full, _plan_gather_pass(len(_FF)), token)
    w_ff1_4 = ff_full[0].reshape(N_SHARD, D, D)
    w_ff2 = ff_full[1].reshape(D_FF, D)
    hn, f1, f, df2, dout, vec_a = _ffn_fwd(h1, row(p["norm2_pre_g"]), w_ff1_4, w_ff2, row(p["norm2_post_g"]),
                                           target, tm_ffn)

    tk = min(1024, t)
    g_big = {"w_ff2": _dw(f, df2, "dw_ff2", 2, D, tk, N_SHARD, lambda i, j: i)}
    df1, dh1, dy, vec_b = _ffn_bwd(df2, f1, w_ff1_4, w_ff2, h1, dout, y, row(p["norm2_pre_g"]),
                                   row(p["norm1_post_g"]), tm_ffn)
    g_big["w_ff1"] = _dw(hn, df1, "dw_ff1", 2, D, tk, N_SHARD, lambda i, j: j)

    def reduce_start(name, keys, wholes):
        swapped = _swap_halves("rs_swap_" + name, [g_big[k] for k in keys], wholes)
        sums = [_chip_sum(core, g_big[k], r, "chip_sum_" + k) for k, r in zip(keys, swapped)]
        arrays = list(sums)
        if wholes:
            arrays.append(_slot_sum(jnp.stack([wholes[0], swapped[-1]]), "chip_sum_small"))
        arrays += [lax.empty(s.shape, BF16) for s in sums]
        if wholes:
            arrays.append(lax.empty((N_SHARD,) + wholes[0].shape, F32))
        plan = _plan_exchange(len(keys), bool(wholes))
        send, recv, arrays, tok = _split_start("rs_send_" + name, arrays, 3 * len(keys) + 4 * len(wholes), plan)
        return send, recv, arrays, plan, tok

    def reduce_finish(name, keys, state, *after):
        send, recv, arrays, plan, _ = state
        arrays = _split_wait("rs_land_" + name, send, recv, arrays, plan, *after)
        n_in = len(arrays) // 2
        halves = {k: _final_sum(chip, arrays[i], arrays[n_in + i], "final_sum_" + k) for i, k in enumerate(keys)}
        return halves, arrays

    ff_state = reduce_start("ff", _FF, [])
    dw_out = _dw(merged, dy, "dw_out", 1, D, tk, 1, lambda i, j: 0, dep=ff_state[4]).reshape(D, D)
    dz, dwp, dwpr, dws, dbsp, vec_c, dbin = _mix_bwd(z, pooled, dy, w_pool, row(p["pool_scale"]), row(p["sgu_ln_g"]),
                                                     row(p["sgu_ln_b"]), ws, ws_t, bsp, w_proj, w_out, mask, tm_mix)
    g_big["w_in"] = _dw(xn, dz, "dw_in", 2, D_IN // N_SHARD, tk, N_SHARD, lambda i, j: j)
    by_half = lambda a, rows: a.reshape(N_SHARD, 2, rows // 2, a.shape[-1]).transpose(1, 0, 2, 3)
    g_big["w_out"] = by_half(dw_out, D // N_SHARD)
    shard_major = lambda a: a.reshape(N_HEADS, N_SHARD, 64, 256).transpose(1, 0, 2, 3).reshape(N_SHARD * 256, 256)
    g_big["w_pool"] = by_half(shard_major(dwp).astype(BF16), 256)
    g_big["w_sgu_proj"] = by_half(shard_major(dwpr).astype(BF16), 256)
    halves, _ = reduce_finish("ff", _FF, ff_state, g_big["w_in"])

    small_part = _pack(_SMALL[1:], SMALL_ROWS - 8, {
        "b_in": dbin[0], "pool_scale": vec_c[0], "sgu_ln_g": vec_c[1], "sgu_ln_b": vec_c[2],
        "w_spatial": dws, "b_spatial": dbsp[:, ::LANES].T, "norm1_post_g": vec_b[1], "norm2_pre_g": vec_b[0],
        "norm2_post_g": vec_a[0]})
    mix_state = reduce_start("mix", _MIX, [small_part])
    grad_x, vec_d = _dx(dz, w_in4, x, dh1, row(p["norm1_pre_g"]), mix_state[4], tm_mm)
    tiles = _all_reduce_tile(jnp.concatenate([vec_d[0].reshape(8, LANES),
                                              jnp.broadcast_to(vec_a[1:2, :LANES], (8, LANES))]))
    g1_grad, loss = tiles[:8], tiles[8, 0]

    grads, delta, new_m, new_v = {}, {}, {}, {}

    def finish(name, keys):
        shared = _swap_halves("rs_share_" + name, [], [halves[k] for k in keys])
        for k, sib in zip(keys, shared):
            as2d = lambda a: a.reshape(local2d[k].shape)
            outs = _adamw_halves(core, halves[k], sib, local2d[k], as2d(m[k]), as2d(v[k]), "adamw_" + k)
            grads[k], delta[k], new_m[k], new_v[k] = (a.reshape(p[k].shape) for a in outs)

    finish("ff", _FF)
    mix_halves, mix_arrays = reduce_finish("mix", _MIX, mix_state, grad_x, *[new_v[k] for k in _FF])
    halves.update(mix_halves)
    finish("mix", _MIX)
    small_grad = jnp.concatenate([g1_grad, _slot_sum(mix_arrays[-1], "final_sum_small")])
    d_s, m_s, v_s = _adamw(_pack_small(p), small_grad, _pack_small(m), _pack_small(v), "adamw_small")
    grads.update(_unpack_small(small_grad, p))
    delta.update(_unpack_small(d_s, p))
    new_m.update(_unpack_small(m_s, p))
    new_v.update(_unpack_small(v_s, p))

    return (loss, grad_x, *[grads[k] for k in _ORDER], *[delta[k] for k in _ORDER],
            *[new_m[k] for k in _ORDER], *[new_v[k] for k in _ORDER])


def kernel(x, norm1_pre_g, w_in, b_in, w_pool, pool_scale, sgu_ln_g, sgu_ln_b, w_spatial, b_spatial, w_sgu_proj, w_out, norm1_post_g, norm2_pre_g, w_ff1, w_ff2, norm2_post_g, loss_target, m_norm1_pre_g, m_w_in, m_b_in, m_w_pool, m_pool_scale, m_sgu_ln_g, m_sgu_ln_b, m_w_spatial, m_b_spatial, m_w_sgu_proj, m_w_out, m_norm1_post_g, m_norm2_pre_g, m_w_ff1, m_w_ff2, m_norm2_post_g, v_norm1_pre_g, v_w_in, v_b_in, v_w_pool, v_pool_scale, v_sgu_ln_g, v_sgu_ln_b, v_w_spatial, v_b_spatial, v_w_sgu_proj, v_w_out, v_norm1_post_g, v_norm2_pre_g, v_w_ff1, v_w_ff2, v_norm2_post_g):
    p = dict(zip(_ORDER, (norm1_pre_g, w_in, b_in, w_pool, pool_scale, sgu_ln_g, sgu_ln_b, w_spatial, b_spatial,
                          w_sgu_proj, w_out, norm1_post_g, norm2_pre_g, w_ff1, w_ff2, norm2_post_g)))
    m = dict(zip(_ORDER, (m_norm1_pre_g, m_w_in, m_b_in, m_w_pool, m_pool_scale, m_sgu_ln_g, m_sgu_ln_b, m_w_spatial,
                          m_b_spatial, m_w_sgu_proj, m_w_out, m_norm1_post_g, m_norm2_pre_g, m_w_ff1, m_w_ff2,
                          m_norm2_post_g)))
    v = dict(zip(_ORDER, (v_norm1_pre_g, v_w_in, v_b_in, v_w_pool, v_pool_scale, v_sgu_ln_g, v_sgu_ln_b, v_w_spatial,
                          v_b_spatial, v_w_sgu_proj, v_w_out, v_norm1_post_g, v_norm2_pre_g, v_w_ff1, v_w_ff2,
                          v_norm2_post_g)))
    outs = _step(p, m, v, x[0], loss_target[0])
    return (outs[0], outs[1][None], *outs[2:])
```

```python
import math

import jax
import jax.numpy as jnp
from jax import lax
from jax.experimental import pallas as pl
from jax.experimental.pallas import tpu as pltpu

F32 = jnp.float32
BF16 = jnp.bfloat16

D = 1024
D_IN = 5 * D
D_FF = 4 * D
N_SHARD = 4
N_CHIP_PEERS = 3
POOL_WINDOWS = (2, 4, 8, 16)
POOL_GROUP = 256
POOL_HALO = 16
SGU_BLOCK = 128
N_HEADS = 4
HEAD = 256
EPS = 1e-6

ADAM_LR = 0.001
ADAM_B1 = 0.9
ADAM_B2 = 0.999
ADAM_EPS = 1e-08
ADAM_WD = 0.01
ADAM_STEP = 10

V7X_VMEM_BYTES = 64 * 1024 * 1024
VMEM_LIMIT = V7X_VMEM_BYTES - 8 * 1024 * 1024
SMALL_ROWS = 616
LANES = 128

MESH = pl.DeviceIdType.MESH
ANY = pl.BlockSpec(memory_space=pl.ANY)
HBM = pl.BlockSpec(memory_space=pltpu.HBM)
SEM = pl.BlockSpec(memory_space=pltpu.SEMAPHORE)
EFFECT = pltpu.SideEffectType.DATAFLOW_SIDE_EFFECTING

NT_DIMS = (((1,), (1,)), ((), ()))
TN_DIMS = (((0,), (0,)), ((), ()))


def _params(n_axes):
    return pltpu.CompilerParams(dimension_semantics=("arbitrary",) * n_axes, vmem_limit_bytes=VMEM_LIMIT)


def _resident(shape, n_axes):
    zeros = (0,) * len(shape)
    index = (lambda i: zeros) if n_axes == 1 else (lambda i, j: zeros)
    return pl.BlockSpec(shape, index, pipeline_mode=pl.Buffered(1))


def _dot(a, b):
    return jnp.dot(a, b, preferred_element_type=F32)


def _dot_nt(a, b):
    return lax.dot_general(a, b, NT_DIMS, preferred_element_type=F32)


def _dot_tn(a, b):
    return lax.dot_general(a, b, TN_DIMS, preferred_element_type=F32)


def _mean(a):
    return jnp.mean(a, axis=-1, keepdims=True)


def _colsum(a):
    return jnp.sum(a, axis=0, keepdims=True)


_GELU_C0 = math.sqrt(2.0 / math.pi)
_GELU_C1 = 0.044715


def _gelu(a):
    t = jnp.tanh(a * (_GELU_C0 + (_GELU_C0 * _GELU_C1) * (a * a)))
    return a * (0.5 * t + 0.5)


def _gelu_and_grad(a):
    a2 = a * a
    t = jnp.tanh(a * (_GELU_C0 + (_GELU_C0 * _GELU_C1) * a2))
    cdf = 0.5 * t + 0.5
    grad = cdf + (0.5 * a) * (1.0 - t * t) * (_GELU_C0 + (3.0 * _GELU_C0 * _GELU_C1) * a2)
    return a * cdf, grad


def _sigmoid(a):
    return 0.5 * jnp.tanh(0.5 * a) + 0.5


def _window_sum(a, w, ahead):
    n = a.shape[0]
    step = 1
    while step < w:
        a = a + pltpu.roll(a, n - step if ahead else step, 0)
        step *= 2
    return a


def _rms_bwd(dn, n, r):
    return r * (dn - n * _mean(dn * n))


def _inproj(x, g1, w_in4, b_in, dep, tm):
    t = x.shape[0]
    cw = D_IN // N_SHARD

    def body(x_ref, g_ref, w_ref, b_ref, dep_ref, z_ref, xn_ref):
        @pl.when(pl.program_id(1) == 0)
        def _():
            xv = x_ref[...]
            r = lax.rsqrt(_mean(xv * xv) + EPS)
            xn_ref[...] = (xv * r * g_ref[...]).astype(BF16)

        z_ref[...] = _dot(xn_ref[...], w_ref[pl.program_id(1)]) + b_ref[...]

    return pl.pallas_call(
        body, name="inproj",
        grid=(t // tm, N_SHARD),
        in_specs=[pl.BlockSpec((tm, D), lambda i, s: (i, 0)),
                  pl.BlockSpec((1, D), lambda i, s: (0, 0)),
                  _resident((N_SHARD, D, cw), 2),
                  pl.BlockSpec((1, cw), lambda i, s: (0, s)),
                  pl.BlockSpec(dep.shape, lambda i, s: (0, 0))],
        out_specs=[pl.BlockSpec((tm, cw), lambda i, s: (i, s)),
                   pl.BlockSpec((tm, D), lambda i, s: (i, 0))],
        out_shape=[jax.ShapeDtypeStruct((t, D_IN), F32), jax.ShapeDtypeStruct((t, D), BF16)],
        compiler_params=_params(2),
    )(x, g1, w_in4, b_in, dep)


def _sgu_forward(z_ref, lng_ref, lnb_ref, ws_ref, bsp_ref, sv_ref, tm, with_grad):
    if with_grad:
        u, du_dz = _gelu_and_grad(z_ref[:, D:2 * D])
        gv, dgv_dz = _gelu_and_grad(z_ref[:, 2 * D:3 * D])
    else:
        u = _gelu(z_ref[:, D:2 * D])
        gv = _gelu(z_ref[:, 2 * D:3 * D])
        du_dz = dgv_dz = None
    xc = gv - _mean(gv)
    rstd = lax.rsqrt(_mean(xc * xc) + EPS)
    vhat = xc * rstd
    vb = (vhat * lng_ref[...] + lnb_ref[...]).astype(BF16)
    for b in range(tm // SGU_BLOCK):
        rows = slice(b * SGU_BLOCK, (b + 1) * SGU_BLOCK)
        for h in range(N_HEADS):
            cols = slice(h * HEAD, (h + 1) * HEAD)
            sv_ref[rows, cols] = _dot(ws_ref[h], vb[rows, cols]) + bsp_ref[h]
    return u, du_dz, dgv_dz, vhat, rstd, vb


def _mix_fwd(z, x, w_pool, pool_scale, ln_g, ln_b, ws, bsp, w_proj, w_out, g1post, tm):
    t = x.shape[0]

    def body(z_ref, x_ref, wp_ref, psc_ref, lng_ref, lnb_ref, ws_ref, bsp_ref, wpr_ref, wo_ref, g1p_ref,
             pooled_ref, merged_ref, y_ref, h1_ref, win_ref, sv_ref):
        i = pl.program_id(0)

        @pl.when(i == 0)
        def _():
            win_ref[0:POOL_HALO, :] = jnp.zeros((POOL_HALO, D), F32)

        win_ref[POOL_HALO:POOL_HALO + tm, :] = z_ref[:, 0:D]
        pos1 = lax.broadcasted_iota(jnp.int32, (tm, 1), 0) + (i * tm + 1)
        a_parts = []
        for g, w in enumerate(POOL_WINDOWS):
            cols = slice(g * POOL_GROUP, (g + 1) * POOL_GROUP)
            ts = _window_sum(win_ref[:, cols], w, False)[POOL_HALO:POOL_HALO + tm, :]
            inv_cnt = 1.0 / jnp.minimum(pos1, w).astype(F32)
            pooled = (ts * inv_cnt - z_ref[:, cols]).astype(BF16)
            pooled_ref[:, cols] = pooled
            a_parts.append(_dot(pooled, wp_ref[g]) * psc_ref[:, cols])
        win_ref[0:POOL_HALO, :] = win_ref[tm:tm + POOL_HALO, :]

        u, _, _, _, _, _ = _sgu_forward(z_ref, lng_ref, lnb_ref, ws_ref, bsp_ref, sv_ref, tm, False)
        gated = (u * sv_ref[...]).astype(BF16)
        b_parts = [_dot(gated[:, h * HEAD:(h + 1) * HEAD], wpr_ref[h]) for h in range(N_HEADS)]

        a = jnp.concatenate(a_parts, axis=1)
        bbr = jnp.concatenate(b_parts, axis=1)
        merged = (_sigmoid(z_ref[:, 3 * D:4 * D]) * a + _sigmoid(z_ref[:, 4 * D:5 * D]) * bbr).astype(BF16)
        merged_ref[...] = merged
        y = _dot(merged, wo_ref[...])
        y_ref[...] = y
        r = lax.rsqrt(_mean(y * y) + EPS)
        h1_ref[...] = x_ref[...] + (y * r) * g1p_ref[...]

    row = lambda i: (i, 0)
    fixed2 = lambda i: (0, 0)
    fixed3 = lambda i: (0, 0, 0)
    vec = pl.BlockSpec((1, D), fixed2)
    return pl.pallas_call(
        body, name="mix_fwd",
        grid=(t // tm,),
        in_specs=[pl.BlockSpec((tm, D_IN), row), pl.BlockSpec((tm, D), row),
                  pl.BlockSpec((N_HEADS, POOL_GROUP, POOL_GROUP), fixed3), vec, vec, vec,
                  pl.BlockSpec((N_HEADS, SGU_BLOCK, SGU_BLOCK), fixed3),
                  pl.BlockSpec((N_HEADS, SGU_BLOCK, 1), fixed3),
                  pl.BlockSpec((N_HEADS, HEAD, HEAD), fixed3),
                  pl.BlockSpec((D, D), fixed2), vec],
        out_specs=[pl.BlockSpec((tm, D), row)] * 4,
        out_shape=[jax.ShapeDtypeStruct((t, D), BF16), jax.ShapeDtypeStruct((t, D), BF16),
                   jax.ShapeDtypeStruct((t, D), F32), jax.ShapeDtypeStruct((t, D), F32)],
        scratch_shapes=[pltpu.VMEM((tm + POOL_HALO, D), F32), pltpu.VMEM((tm, D), F32)],
        compiler_params=_params(1),
    )(z, x, w_pool, pool_scale, ln_g, ln_b, ws, bsp, w_proj, w_out, g1post)


def _ffn_fwd(h1, g2pre, w1, w2, g2post, target, tm):
    t = h1.shape[0]
    n_j = D_FF // D

    def body(h1_ref, g2_ref, w1_ref, w2_ref, g2p_ref, tgt_ref,
             hn_ref, f1_ref, f_ref, df2_ref, dout_ref, vec_ref, acc_ref):
        i, j = pl.program_id(0), pl.program_id(1)

        @pl.when((i == 0) & (j == 0))
        def _():
            vec_ref[...] = jnp.zeros_like(vec_ref)

        @pl.when(j == 0)
        def _():
            h = h1_ref[...]
            r = lax.rsqrt(_mean(h * h) + EPS)
            hn_ref[...] = (h * r * g2_ref[...]).astype(BF16)
            acc_ref[...] = jnp.zeros_like(acc_ref)

        f1 = _dot(hn_ref[...], w1_ref[j])
        f1_ref[...] = f1
        rl = jnp.maximum(f1, 0.0)
        f = (rl * rl).astype(BF16)
        f_ref[...] = f
        acc_ref[...] += _dot(f, w2_ref[j])

        @pl.when(j == n_j - 1)
        def _():
            f2 = acc_ref[...]
            r = lax.rsqrt(_mean(f2 * f2) + EPS)
            n2 = f2 * r
            err = h1_ref[...] + n2 * g2p_ref[...] - tgt_ref[...]
            loss = 0.5 * jnp.sum(_mean(err * err))
            dout = err * (1.0 / D)
            dout_ref[...] = dout
            vec_ref[0:1, :] += _colsum(dout * n2)
            vec_ref[1:2, :] += jnp.full((1, D), loss, F32)
            df2_ref[...] = _rms_bwd(dout * g2p_ref[...], n2, r).astype(BF16)

    row = lambda i, j: (i, 0)
    fixed = lambda i, j: (0, 0)
    vec = pl.BlockSpec((1, D), fixed)
    return pl.pallas_call(
        body, name="ffn_fwd",
        grid=(t // tm, n_j),
        in_specs=[pl.BlockSpec((tm, D), row), vec,
                  _resident((n_j, D, D), 2), _resident((n_j, D, D), 2), vec,
                  pl.BlockSpec((tm, D), row)],
        out_specs=[pl.BlockSpec((tm, D), row),
                   pl.BlockSpec((tm, D), lambda i, j: (i, j)),
                   pl.BlockSpec((tm, D), lambda i, j: (i, j)),
                   pl.BlockSpec((tm, D), row), pl.BlockSpec((tm, D), row),
                   pl.BlockSpec((8, D), fixed)],
        out_shape=[jax.ShapeDtypeStruct((t, D), BF16), jax.ShapeDtypeStruct((t, D_FF), F32),
                   jax.ShapeDtypeStruct((t, D_FF), BF16), jax.ShapeDtypeStruct((t, D), BF16),
                   jax.ShapeDtypeStruct((t, D), F32), jax.ShapeDtypeStruct((8, D), F32)],
        scratch_shapes=[pltpu.VMEM((tm, D), F32)],
        compiler_params=_params(2),
    )(h1, g2pre, w1, w2, g2post, target)


def _ffn_bwd(df2, f1, w1, w2, h1, dout, y, g2pre, g1post, tm):
    t = h1.shape[0]
    n_j = D_FF // D

    def body(df2_ref, f1_ref, w1_ref, w2_ref, h1_ref, dout_ref, y_ref, g2_ref, g1p_ref,
             df1_ref, dh1_ref, dy_ref, vec_ref, acc_ref):
        i, j = pl.program_id(0), pl.program_id(1)

        @pl.when((i == 0) & (j == 0))
        def _():
            vec_ref[...] = jnp.zeros_like(vec_ref)

        @pl.when(j == 0)
        def _():
            acc_ref[...] = jnp.zeros_like(acc_ref)

        df = _dot_nt(df2_ref[...], w2_ref[j])
        df1 = (df * (2.0 * jnp.maximum(f1_ref[...], 0.0))).astype(BF16)
        df1_ref[...] = df1
        acc_ref[...] += _dot_nt(df1, w1_ref[j])

        @pl.when(j == n_j - 1)
        def _():
            dhn = acc_ref[...]
            h = h1_ref[...]
            r2 = lax.rsqrt(_mean(h * h) + EPS)
            nh = h * r2
            vec_ref[0:1, :] += _colsum(dhn * nh)
            dh1 = dout_ref[...] + _rms_bwd(dhn * g2_ref[...], nh, r2)
            dh1_ref[...] = dh1
            yv = y_ref[...]
            r1 = lax.rsqrt(_mean(yv * yv) + EPS)
            ny = yv * r1
            vec_ref[1:2, :] += _colsum(dh1 * ny)
            dy_ref[...] = _rms_bwd(dh1 * g1p_ref[...], ny, r1).astype(BF16)

    row = lambda i, j: (i, 0)
    fixed = lambda i, j: (0, 0)
    vec = pl.BlockSpec((1, D), fixed)
    return pl.pallas_call(
        body, name="ffn_bwd",
        grid=(t // tm, n_j),
        in_specs=[pl.BlockSpec((tm, D), row),
                  pl.BlockSpec((tm, D), lambda i, j: (i, j)),
                  _resident((n_j, D, D), 2), _resident((n_j, D, D), 2),
                  pl.BlockSpec((tm, D), row), pl.BlockSpec((tm, D), row), pl.BlockSpec((tm, D), row), vec, vec],
        out_specs=[pl.BlockSpec((tm, D), lambda i, j: (i, j)),
                   pl.BlockSpec((tm, D), row), pl.BlockSpec((tm, D), row),
                   pl.BlockSpec((8, D), fixed)],
        out_shape=[jax.ShapeDtypeStruct((t, D_FF), BF16), jax.ShapeDtypeStruct((t, D), F32),
                   jax.ShapeDtypeStruct((t, D), BF16), jax.ShapeDtypeStruct((8, D), F32)],
        scratch_shapes=[pltpu.VMEM((tm, D), F32)],
        compiler_params=_params(2),
    )(df2, f1, w1, w2, h1, dout, y, g2pre, g1post)


def _mix_bwd(z, pooled, dy, w_pool, pool_scale, ln_g, ln_b, ws, ws_t, bsp, w_proj, w_out, mask, tm):
    t = z.shape[0]
    n_t = t // tm

    def body(z_ref, pooled_ref, dy_ref, wp_ref, psc_ref, lng_ref, lnb_ref, ws_ref, wst_ref, bsp_ref, wpr_ref,
             wo_ref, mask_ref,
             dz_ref, dwp_ref, dwpr_ref, dws_ref, dbsp_ref, vec_ref, dbin_ref,
             win_ref, sv_ref, dsv_ref, dv_ref, dsvsum_ref):
        i = pl.program_id(0)

        @pl.when(i == 0)
        def _():
            win_ref[tm:tm + POOL_HALO, :] = jnp.zeros((POOL_HALO, D), F32)
            dwp_ref[...] = jnp.zeros_like(dwp_ref)
            dwpr_ref[...] = jnp.zeros_like(dwpr_ref)
            dws_ref[...] = jnp.zeros_like(dws_ref)
            vec_ref[...] = jnp.zeros_like(vec_ref)
            dbin_ref[...] = jnp.zeros_like(dbin_ref)
            dsvsum_ref[...] = jnp.zeros_like(dsvsum_ref)

        dmerged = _dot_nt(dy_ref[...], wo_ref[...])

        u, du_dz, dgv_dz, vhat, rstd, vb = _sgu_forward(z_ref, lng_ref, lnb_ref, ws_ref, bsp_ref, sv_ref, tm, True)
        sv = sv_ref[...]
        gated = (u * sv).astype(BF16)

        sa = _sigmoid(z_ref[:, 3 * D:4 * D])
        sb = _sigmoid(z_ref[:, 4 * D:5 * D])
        da = dmerged * sa
        dbbr = (dmerged * sb).astype(BF16)

        tile = n_t - 1 - i
        pos1 = lax.broadcasted_iota(jnp.int32, (tm, 1), 0) + (tile * tm + 1)
        dzga_parts, dpooled_parts = [], []
        for g, w in enumerate(POOL_WINDOWS):
            cols = slice(g * POOL_GROUP, (g + 1) * POOL_GROUP)
            pg = pooled_ref[:, cols]
            a_pre = _dot(pg, wp_ref[g])
            da_g = da[:, cols]
            psc = psc_ref[:, cols]
            vec_ref[0:1, cols] += _colsum(da_g * a_pre)
            sa_g = sa[:, cols]
            dzga_parts.append(dmerged[:, cols] * (a_pre * psc) * (sa_g * (1.0 - sa_g)))
            da_pre = (da_g * psc).astype(BF16)
            dwp_ref[g] += _dot_tn(pg, da_pre)
            dpooled = _dot_nt(da_pre, wp_ref[g])
            dpooled_parts.append(dpooled)
            inv_cnt = 1.0 / jnp.minimum(pos1, w).astype(F32)
            win_ref[0:tm, cols] = dpooled * inv_cnt
        dzp_parts = []
        for g, w in enumerate(POOL_WINDOWS):
            cols = slice(g * POOL_GROUP, (g + 1) * POOL_GROUP)
            acc = _window_sum(win_ref[:, cols], w, True)[0:tm, :]
            dzp_parts.append(acc - dpooled_parts[g])
        win_ref[tm:tm + POOL_HALO, :] = win_ref[0:POOL_HALO, :]

        dzgb_parts = []
        for h in range(N_HEADS):
            cols = slice(h * HEAD, (h + 1) * HEAD)
            g_h = gated[:, cols]
            db_h = dbbr[:, cols]
            bbr_h = _dot(g_h, wpr_ref[h])
            sb_h = sb[:, cols]
            dzgb_parts.append(dmerged[:, cols] * bbr_h * (sb_h * (1.0 - sb_h)))
            dwpr_ref[h] += _dot_tn(g_h, db_h)
            dgated_h = _dot_nt(db_h, wpr_ref[h])
            dsv_ref[:, cols] = dgated_h * u[:, cols]
            sv_ref[:, cols] = dgated_h * sv[:, cols]
        dzu = sv_ref[...] * du_dz
        dsv = dsv_ref[...]
        dsvb = dsv.astype(BF16)
        blk_sum = dsv[0:SGU_BLOCK, :]
        for b in range(1, tm // SGU_BLOCK):
            blk_sum = blk_sum + dsv[b * SGU_BLOCK:(b + 1) * SGU_BLOCK, :]
        dsvsum_ref[...] += blk_sum
        for b in range(tm // SGU_BLOCK):
            rows = slice(b * SGU_BLOCK, (b + 1) * SGU_BLOCK)
            for h in range(N_HEADS):
                cols = slice(h * HEAD, (h + 1) * HEAD)
                dws_ref[h] += _dot_nt(dsvb[rows, cols], vb[rows, cols])
                dv_ref[rows, cols] = _dot(wst_ref[h], dsvb[rows, cols])
        dv = dv_ref[...]
        vec_ref[1:2, :] += _colsum(dv * vhat)
        vec_ref[2:3, :] += _colsum(dv)
        dvhat = dv * lng_ref[...]
        dgv = rstd * (dvhat - _mean(dvhat) - vhat * _mean(dvhat * vhat))
        dzv = dgv * dgv_dz

        parts = [jnp.concatenate(dzp_parts, axis=1), dzu, dzv,
                 jnp.concatenate(dzga_parts, axis=1), jnp.concatenate(dzgb_parts, axis=1)]
        for k, part in enumerate(parts):
            dz_ref[:, k * D:(k + 1) * D] = part.astype(BF16)
            dbin_ref[0:1, k * D:(k + 1) * D] += _colsum(part)

        @pl.when(i == n_t - 1)
        def _():
            for h in range(N_HEADS):
                dws_ref[h] = jnp.where(mask_ref[...] > 0.0, dws_ref[h], 0.0)
                tot = jnp.sum(dsvsum_ref[:, h * HEAD:(h + 1) * HEAD], axis=1, keepdims=True)
                dbsp_ref[:, h * LANES:(h + 1) * LANES] = jnp.broadcast_to(tot, (SGU_BLOCK, LANES))

    row = lambda i: (n_t - 1 - i, 0)
    fixed2 = lambda i: (0, 0)
    fixed3 = lambda i: (0, 0, 0)
    vec = pl.BlockSpec((1, D), fixed2)
    sq = pl.BlockSpec((N_HEADS, SGU_BLOCK, SGU_BLOCK), fixed3)
    grp = pl.BlockSpec((N_HEADS, HEAD, HEAD), fixed3)
    return pl.pallas_call(
        body, name="mix_bwd",
        grid=(n_t,),
        in_specs=[pl.BlockSpec((tm, D_IN), row), pl.BlockSpec((tm, D), row), pl.BlockSpec((tm, D), row),
                  grp, vec, vec, vec, sq, sq, pl.BlockSpec((N_HEADS, SGU_BLOCK, 1), fixed3), grp,
                  pl.BlockSpec((D, D), fixed2), pl.BlockSpec((SGU_BLOCK, SGU_BLOCK), fixed2)],
        out_specs=[pl.BlockSpec((tm, D_IN), row), grp, grp, sq,
                   pl.BlockSpec((SGU_BLOCK, N_HEADS * LANES), fixed2),
                   pl.BlockSpec((8, D), fixed2), pl.BlockSpec((8, D_IN), fixed2)],
        out_shape=[jax.ShapeDtypeStruct((t, D_IN), BF16),
                   jax.ShapeDtypeStruct((N_HEADS, HEAD, HEAD), F32), jax.ShapeDtypeStruct((N_HEADS, HEAD, HEAD), F32),
                   jax.ShapeDtypeStruct((N_HEADS, SGU_BLOCK, SGU_BLOCK), F32),
                   jax.ShapeDtypeStruct((SGU_BLOCK, N_HEADS * LANES), F32),
                   jax.ShapeDtypeStruct((8, D), F32), jax.ShapeDtypeStruct((8, D_IN), F32)],
        scratch_shapes=[pltpu.VMEM((tm + POOL_HALO, D), F32), pltpu.VMEM((tm, D), F32), pltpu.VMEM((tm, D), F32),
                        pltpu.VMEM((tm, D), F32), pltpu.VMEM((SGU_BLOCK, D), F32)],
        compiler_params=_params(1),
    )(z, pooled, dy, w_pool, pool_scale, ln_g, ln_b, ws, ws_t, bsp, w_proj, w_out, mask)


def _dx(dz, w_in4, x, dh1, g1, dep, tm):
    t = x.shape[0]
    cw = D_IN // N_SHARD

    def body(dz_ref, w_ref, x_ref, dh1_ref, g_ref, dep_ref, gx_ref, vec_ref, acc_ref):
        i, s = pl.program_id(0), pl.program_id(1)

        @pl.when((i == 0) & (s == 0))
        def _():
            vec_ref[...] = jnp.zeros_like(vec_ref)

        @pl.when(s == 0)
        def _():
            acc_ref[...] = jnp.zeros_like(acc_ref)

        acc_ref[...] += _dot_nt(dz_ref[...], w_ref[s])

        @pl.when(s == N_SHARD - 1)
        def _():
            dxn = acc_ref[...]
            xv = x_ref[...]
            r = lax.rsqrt(_mean(xv * xv) + EPS)
            xh = xv * r
            vec_ref[0:1, :] += _colsum(dxn * xh)
            gx_ref[...] = dh1_ref[...] + _rms_bwd(dxn * g_ref[...], xh, r)

    row = lambda i, s: (i, 0)
    fixed = lambda i, s: (0, 0)
    return pl.pallas_call(
        body, name="dx",
        grid=(t // tm, N_SHARD),
        in_specs=[pl.BlockSpec((tm, cw), lambda i, s: (i, s)),
                  _resident((N_SHARD, D, cw), 2),
                  pl.BlockSpec((tm, D), row), pl.BlockSpec((tm, D), row), pl.BlockSpec((1, D), fixed),
                  pl.BlockSpec(dep.shape, fixed)],
        out_specs=[pl.BlockSpec((tm, D), row), pl.BlockSpec((8, D), fixed)],
        out_shape=[jax.ShapeDtypeStruct((t, D), F32), jax.ShapeDtypeStruct((8, D), F32)],
        scratch_shapes=[pltpu.VMEM((tm, D), F32)],
        compiler_params=_params(2),
    )(dz, w_in4, x, dh1, g1, dep)


def _dw(a, b, name, n_split, tn, tk, n_out, out_block, dep=None):
    t, m = a.shape
    n = b.shape[1]
    tm = min(m, D)
    rows = tm // n_split
    n_k = t // tk

    def body(a_ref, b_ref, *rest):
        o_ref, acc_ref = rest[-2:]
        k = pl.program_id(2)

        @pl.when(k == 0)
        def _():
            acc_ref[...] = jnp.zeros_like(acc_ref)

        acc_ref[...] += _dot_tn(a_ref[...], b_ref[...])

        @pl.when(k == n_k - 1)
        def _():
            for h in range(n_split):
                o_ref[h, 0] = acc_ref[h * rows:(h + 1) * rows, :].astype(BF16)

    return pl.pallas_call(
        body, name=name,
        grid=(m // tm, n // tn, n_k),
        in_specs=[pl.BlockSpec((tk, tm), lambda i, j, k: (k, i)),
                  pl.BlockSpec((tk, tn), lambda i, j, k: (k, j))]
        + ([] if dep is None else [pl.BlockSpec(dep.shape, lambda i, j, k: (0, 0))]),
        out_specs=pl.BlockSpec((n_split, 1, rows, tn), lambda i, j, k: (0, out_block(i, j), 0, 0)),
        out_shape=jax.ShapeDtypeStruct((n_split, n_out, rows, tn), BF16),
        scratch_shapes=[pltpu.VMEM((tm, tn), F32)],
        compiler_params=_params(3),
    )(*((a, b) if dep is None else (a, b, dep)))


def _row_block(rows, cols, itemsize, n_bufs):
    budget = VMEM_LIMIT // 4
    rb = rows
    while rb % 16 == 0 and rb * cols * itemsize * n_bufs * 2 > budget:
        rb //= 2
    return rb


def _chip_sum(core, g, recv, name):
    _, n_s, rh, cols = g.shape
    rb = _row_block(rh, cols, 4, 3)

    def body(c_ref, g_ref, r_ref, o_ref):
        o_ref[...] = (g_ref[0].astype(F32) + r_ref[...].astype(F32)).astype(BF16)

    return pl.pallas_call(
        body, name=name,
        grid_spec=pltpu.PrefetchScalarGridSpec(
            num_scalar_prefetch=1, grid=(n_s, rh // rb),
            in_specs=[pl.BlockSpec((1, 1, rb, cols), lambda s, r, c: (c[0], s, r, 0)),
                      pl.BlockSpec((1, rb, cols), lambda s, r, c: (s, r, 0))],
            out_specs=pl.BlockSpec((1, rb, cols), lambda s, r, c: (s, r, 0))),
        out_shape=jax.ShapeDtypeStruct((n_s, rh, cols), BF16),
        compiler_params=_params(2),
    )(core, g, recv)


def _slot_sum(slots, name):
    n_s, rows, cols = slots.shape
    rb = _row_block(rows, cols, 4, n_s + 1)

    def body(s_ref, o_ref):
        acc = s_ref[0].astype(F32)
        for k in range(1, n_s):
            acc = acc + s_ref[k].astype(F32)
        o_ref[...] = acc

    return pl.pallas_call(
        body, name=name,
        grid=(rows // rb,),
        in_specs=[pl.BlockSpec((n_s, rb, cols), lambda r: (0, r, 0))],
        out_specs=pl.BlockSpec((rb, cols), lambda r: (r, 0)),
        out_shape=jax.ShapeDtypeStruct((rows, cols), F32),
        compiler_params=_params(1),
    )(slots)


def _adamw(w, g, m, v, name):
    rows, cols = w.shape
    rb = _row_block(rows, cols, 4, 7)
    c1 = 1.0 - ADAM_B1 ** ADAM_STEP
    c2 = 1.0 - ADAM_B2 ** ADAM_STEP

    def body(w_ref, g_ref, m_ref, v_ref, d_ref, mo_ref, vo_ref):
        gv = g_ref[...]
        m_new = ADAM_B1 * m_ref[...] + (1.0 - ADAM_B1) * gv
        v_new = ADAM_B2 * v_ref[...] + (1.0 - ADAM_B2) * (gv * gv)
        mo_ref[...] = m_new
        vo_ref[...] = v_new
        d_ref[...] = -ADAM_LR * ((m_new / c1) / (jnp.sqrt(v_new / c2) + ADAM_EPS) + ADAM_WD * w_ref[...])

    blk = pl.BlockSpec((rb, cols), lambda r: (r, 0))
    return pl.pallas_call(
        body, name=name,
        grid=(rows // rb,),
        in_specs=[blk] * 4, out_specs=[blk] * 3,
        out_shape=[jax.ShapeDtypeStruct((rows, cols), F32)] * 3,
        compiler_params=_params(1),
    )(w, g, m, v)


def _final_sum(chip, own, slots, name):
    n_s, rh, cols = own.shape
    rb = _row_block(rh, cols, 4, n_s + 1)

    def body(chip_ref, own_ref, s1_ref, s2_ref, s3_ref, o_ref):
        acc = own_ref[0].astype(F32) + s1_ref[0].astype(F32)
        acc = acc + s2_ref[0].astype(F32)
        o_ref[...] = acc + s3_ref[0].astype(F32)

    def slot(k):
        return pl.BlockSpec((1, rb, cols), lambda r, chip: (lax.rem(chip[0] + k, n_s), r, 0))

    return pl.pallas_call(
        body, name=name,
        grid_spec=pltpu.PrefetchScalarGridSpec(
            num_scalar_prefetch=1, grid=(rh // rb,),
            in_specs=[slot(0), slot(1), slot(2), slot(3)],
            out_specs=pl.BlockSpec((rb, cols), lambda r, chip: (r, 0))),
        out_shape=jax.ShapeDtypeStruct((rh, cols), F32),
        compiler_params=_params(1),
    )(chip, own, slots, slots, slots)


def _adamw_halves(core, own, sibling, w, m, v, name):
    rows, cols = w.shape
    rh = rows // 2
    rb = _row_block(rh, cols, 4, 9)
    n_r = rh // rb
    c1 = 1.0 - ADAM_B1 ** ADAM_STEP
    c2 = 1.0 - ADAM_B2 ** ADAM_STEP

    def body(c_ref, own_ref, sib_ref, w_ref, m_ref, v_ref, g_ref, d_ref, mo_ref, vo_ref):
        mine = c_ref[0] == pl.program_id(0)

        @pl.when(mine)
        def _():
            g_ref[...] = own_ref[...]

        @pl.when(jnp.logical_not(mine))
        def _():
            g_ref[...] = sib_ref[...]

        gv = g_ref[...]
        m_new = ADAM_B1 * m_ref[...] + (1.0 - ADAM_B1) * gv
        v_new = ADAM_B2 * v_ref[...] + (1.0 - ADAM_B2) * (gv * gv)
        mo_ref[...] = m_new
        vo_ref[...] = v_new
        d_ref[...] = -ADAM_LR * ((m_new / c1) / (jnp.sqrt(v_new / c2) + ADAM_EPS) + ADAM_WD * w_ref[...])

    half = pl.BlockSpec((rb, cols), lambda h, r, c: (r, 0))
    whole = pl.BlockSpec((rb, cols), lambda h, r, c: (h * n_r + r, 0))
    return pl.pallas_call(
        body, name=name,
        grid_spec=pltpu.PrefetchScalarGridSpec(
            num_scalar_prefetch=1, grid=(2, n_r),
            in_specs=[half, half, whole, whole, whole], out_specs=[whole] * 4),
        out_shape=[jax.ShapeDtypeStruct((rows, cols), F32)] * 4,
        compiler_params=_params(2),
    )(core, own, sibling, w, m, v)


def _place():
    x, y, c = lax.axis_index("x"), lax.axis_index("y"), lax.axis_index("c")
    chips = [(1 - x, y), (x, 1 - y), (1 - x, 1 - y)]
    return x, y, c, chips


def _remote(src, dst, send_sem, recv_sem, device):
    return pltpu.make_async_remote_copy(src_ref=src, dst_ref=dst, send_sem=send_sem, recv_sem=recv_sem,
                                        device_id=device, device_id_type=MESH)


def _all_gather_weights(shards):
    n = len(shards)

    def body(*refs):
        ins, outs = refs[:n], refs[n:2 * n]
        send_ici, recv_ici, send_d2d, recv_d2d, send_own, recv_own = refs[2 * n:]
        x, y, c, chips = _place()
        me = 2 * x + y
        sibling = (x, y, 1 - c)
        local = [_remote(ins[w], outs[w].at[me], send_own.at[w], recv_own.at[w], sibling) for w in range(n)]
        for cp in local:
            cp.start()
        sends = []
        for j, (px, py) in enumerate(chips):
            for w in range(n):
                cp = _remote(ins[w].at[c], outs[w].at[me, c], send_ici.at[w, j], recv_ici.at[w, j], (px, py, c))
                cp.start()
                sends.append(cp)
        for j, (px, py) in enumerate(chips):
            src_chip = 2 * px + py
            for w in range(n):
                landed = outs[w].at[src_chip, c]
                _remote(landed, landed, send_ici.at[w, j], recv_ici.at[w, j], (px, py, c)).wait_recv()
                cp = _remote(landed, landed, send_d2d.at[w, j], recv_d2d.at[w, j], sibling)
                cp.start()
                sends.append(cp)
        for j, (px, py) in enumerate(chips):
            src_chip = 2 * px + py
            for w in range(n):
                passed = outs[w].at[src_chip, 1 - c]
                _remote(passed, passed, send_d2d.at[w, j], recv_d2d.at[w, j], sibling).wait_recv()
        for cp in sends:
            cp.wait_send()
        for cp in local:
            cp.wait()

    return pl.pallas_call(
        body, name="ag_weights",
        in_specs=[ANY] * n, out_specs=[ANY] * n,
        out_shape=[jax.ShapeDtypeStruct((N_SHARD,) + s.shape, s.dtype) for s in shards],
        scratch_shapes=[pltpu.SemaphoreType.DMA((n, N_CHIP_PEERS))] * 4 + [pltpu.SemaphoreType.DMA((n,))] * 2,
    )(*shards)


def _split_start(name, arrays, n_copies, plan, after=None):
    n = len(arrays)
    n_in = n + (after is not None)

    def body(*refs):
        ins, send_sem, recv_sem, token = refs[:n], refs[n_in], refs[n_in + 1], refs[-1]
        for k, (src, dst, device) in enumerate(plan(ins)):
            _remote(src, dst, send_sem.at[k], recv_sem.at[k], device).start()
        token[...] = jnp.zeros_like(token)

    outs = pl.pallas_call(
        body, name=name,
        in_specs=[HBM] * n + [ANY] * (n_in - n),
        out_specs=(SEM, SEM, *[HBM] * n, pl.BlockSpec(memory_space=pltpu.VMEM)),
        out_shape=(pltpu.SemaphoreType.DMA((n_copies,)), pltpu.SemaphoreType.DMA((n_copies,)),
                   *[pltpu.HBM(a.shape, a.dtype) for a in arrays], jax.ShapeDtypeStruct((8, LANES), F32)),
        input_output_aliases={i: i + 2 for i in range(n)},
        compiler_params=pltpu.CompilerParams(has_side_effects=EFFECT),
    )(*[pltpu.with_memory_space_constraint(a, pltpu.HBM) for a in arrays], *([] if after is None else [after]))
    return outs[0], outs[1], list(outs[2:2 + n]), outs[-1]


def _split_wait(name, send_sem, recv_sem, arrays, plan, *after):
    n = len(arrays)

    def body(*refs):
        ins, send_ref, recv_ref = refs[:n], refs[n], refs[n + 1]
        for k, (src, dst, device) in enumerate(plan(ins)):
            cp = _remote(src, dst, send_ref.at[k], recv_ref.at[k], device)
            cp.wait_send()
            cp.wait_recv()

    outs = pl.pallas_call(
        body, name=name,
        in_specs=[HBM] * n + [SEM, SEM] + [ANY] * len(after), out_specs=[HBM] * n,
        out_shape=[pltpu.HBM(a.shape, a.dtype) for a in arrays],
        input_output_aliases={i: i for i in range(n)},
        compiler_params=pltpu.CompilerParams(has_side_effects=EFFECT),
    )(*arrays, send_sem, recv_sem, *after)
    return list(outs)


def _plan_gather_send(n):
    def plan(refs):
        x, y, c, chips = _place()
        me = 2 * x + y
        copies = []
        for w in range(n):
            own, full = refs[w], refs[n + w]
            copies.append((own, full.at[me], (x, y, 1 - c)))
            copies += [(own.at[c], full.at[me, c], (px, py, c)) for px, py in chips]
        return copies
    return plan


def _plan_gather_direct(n):
    def plan(refs):
        x, y, c, chips = _place()
        me = 2 * x + y
        copies = []
        for w in range(n):
            own, full = refs[w], refs[n + w]
            copies.append((own, full.at[me], (x, y, 1 - c)))
            for px, py in chips:
                copies += [(own.at[c], full.at[me, c], (px, py, c)), (own.at[c], full.at[me, c], (px, py, 1 - c))]
        return copies
    return plan


def _plan_gather_pass(n):
    def plan(refs):
        x, y, c, chips = _place()
        copies = []
        for w in range(n):
            for px, py in chips:
                landed = refs[w].at[2 * px + py, c]
                copies.append((landed, landed, (x, y, 1 - c)))
        return copies
    return plan


def _plan_exchange(n, with_small):
    def plan(refs):
        x, y, c, chips = _place()
        me = 2 * x + y
        n_in = n + int(with_small)
        copies = []
        for w in range(n):
            copies += [(refs[w].at[2 * px + py], refs[n_in + w].at[me], (px, py, c)) for px, py in chips]
        if with_small:
            small, slots = refs[n], refs[n_in + n]
            copies.append((small, slots.at[me], (x, y, 1 - c)))
            copies += [(small, slots.at[me], (px, py, c)) for px, py in chips]
        return copies
    return plan


def _swap_halves(name, grads, wholes):
    n = len(grads)
    arrays = list(grads) + list(wholes)
    n_all = len(arrays)

    def body(*refs):
        ins, outs = refs[:n_all], refs[n_all:2 * n_all]
        send_sem, recv_sem = refs[2 * n_all:]
        x, y, c, _ = _place()
        sibling = (x, y, 1 - c)
        copies = []
        for w in range(n_all):
            src = ins[w].at[1 - c] if w < n else ins[w]
            cp = _remote(src, outs[w], send_sem.at[w], recv_sem.at[w], sibling)
            cp.start()
            copies.append(cp)
        for cp in copies:
            cp.wait()

    return pl.pallas_call(
        body, name=name,
        in_specs=[ANY] * n_all, out_specs=[ANY] * n_all,
        out_shape=[jax.ShapeDtypeStruct(g.shape[1:], g.dtype) for g in grads]
        + [jax.ShapeDtypeStruct(a.shape, a.dtype) for a in wholes],
        scratch_shapes=[pltpu.SemaphoreType.DMA((n_all,))] * 2,
    )(*arrays)


def _all_reduce_tile(part):
    n_dev = 8

    def body(part_ref, sum_ref, slots_ref, send_sem, recv_sem):
        x, y, c, _ = _place()
        me = 4 * x + 2 * y + c
        slots_ref[me] = part_ref[...]
        flips = [(fx, fy, fc) for fx in (0, 1) for fy in (0, 1) for fc in (0, 1)][1:]
        copies = []
        for k, (fx, fy, fc) in enumerate(flips):
            peer = (x + fx - 2 * x * fx, y + fy - 2 * y * fy, c + fc - 2 * c * fc)
            cp = _remote(part_ref, slots_ref.at[me], send_sem.at[k], recv_sem.at[k], peer)
            cp.start()
            copies.append(cp)
        for cp in copies:
            cp.wait()
        acc = slots_ref[0]
        for k in range(1, n_dev):
            acc = acc + slots_ref[k]
        sum_ref[...] = acc

    return pl.pallas_call(
        body, name="all_reduce_tile",
        in_specs=[pl.BlockSpec(memory_space=pltpu.VMEM)],
        out_specs=pl.BlockSpec(memory_space=pltpu.VMEM),
        out_shape=jax.ShapeDtypeStruct(part.shape, F32),
        scratch_shapes=[pltpu.VMEM((n_dev,) + part.shape, F32),
                        pltpu.SemaphoreType.DMA((n_dev - 1,)), pltpu.SemaphoreType.DMA((n_dev - 1,))],
    )(part)


_SMALL = ("norm1_pre_g", "b_in", "pool_scale", "sgu_ln_g", "sgu_ln_b", "w_spatial", "b_spatial",
          "norm1_post_g", "norm2_pre_g", "norm2_post_g")
_MIX = ("w_in", "w_pool", "w_sgu_proj", "w_out")
_FF = ("w_ff1", "w_ff2")
_BIG = _MIX + _FF
_ORDER = ("norm1_pre_g", "w_in", "b_in", "w_pool", "pool_scale", "sgu_ln_g", "sgu_ln_b", "w_spatial", "b_spatial",
          "w_sgu_proj", "w_out", "norm1_post_g", "norm2_pre_g", "w_ff1", "w_ff2", "norm2_post_g")


def _pack(keys, rows, parts):
    flat = jnp.concatenate([parts[k].reshape(-1).astype(F32) for k in keys])
    flat = jnp.pad(flat, (0, rows * LANES - flat.shape[0]))
    return flat.reshape(rows, LANES)


def _pack_small(parts):
    return _pack(_SMALL, SMALL_ROWS, parts)


def _unpack_small(packed, like):
    flat = packed.reshape(-1)
    out, off = {}, 0
    for k in _SMALL:
        size = like[k].size
        out[k] = flat[off:off + size].reshape(like[k].shape)
        off += size
    return out


def _halves(a):
    return a.reshape(2, a.shape[0] // 2, a.shape[1])


def _step(p, m, v, x, target):
    t = x.shape[0]
    tm_mm = min(1024, t)
    tm_mix = min(256, t)
    tm_ffn = min(512, t)
    core = lax.axis_index("c").astype(jnp.int32).reshape(1)
    row = lambda a: a.reshape(1, -1)

    chip = (2 * lax.axis_index("x") + lax.axis_index("y")).astype(jnp.int32).reshape(1)

    local2d = {"w_in": p["w_in"], "w_ff1": p["w_ff1"], "w_ff2": p["w_ff2"], "w_out": p["w_out"],
               "w_pool": p["w_pool"].reshape(N_HEADS * 64, POOL_GROUP),
               "w_sgu_proj": p["w_sgu_proj"].reshape(N_HEADS * 64, HEAD)}
    shard = {k: _halves(local2d[k].astype(BF16)) for k in _BIG}
    landing = lambda keys: [lax.empty((N_SHARD,) + shard[k].shape, BF16) for k in keys]
    w_in4 = _all_gather_weights([shard["w_in"]])[0].reshape(N_SHARD, D, D_IN // N_SHARD)
    mix_keys = _MIX[1:]
    mix_arrays = [shard[k] for k in mix_keys] + landing(mix_keys)
    send0, recv0, mix_arrays, token = _split_start("ag_mix_send", mix_arrays, 7 * len(mix_keys),
                                                   _plan_gather_direct(len(mix_keys)), after=w_in4)
    ff_arrays = [shard[k] for k in _FF] + landing(_FF)
    send1, recv1, ff_arrays, token = _split_start("ag_ff_send", ff_arrays, 4 * len(_FF), _plan_gather_send(len(_FF)),
                                                  after=token)

    pos = jnp.arange(SGU_BLOCK) // 64
    mask = (pos[:, None] >= pos[None, :]).astype(F32)
    ws = (p["w_spatial"] * mask[None]).astype(BF16)
    ws_t = jnp.swapaxes(ws, 1, 2)
    bsp = p["b_spatial"].reshape(N_HEADS, SGU_BLOCK, 1)

    z, xn = _inproj(x, row(p["norm1_pre_g"]), w_in4, row(p["b_in"]), token, tm_mm)
    mix_arrays = _split_wait("ag_mix_land", send0, recv0, mix_arrays, _plan_gather_direct(len(mix_keys)), z)
    full = dict(zip(mix_keys, mix_arrays[len(mix_keys):]))
    w_out = full["w_out"].reshape(D, D)
    regroup = lambda a: a.reshape(N_SHARD, N_HEADS, 64, 256).transpose(1, 0, 2, 3).reshape(N_HEADS, 256, 256)
    w_pool = regroup(full["w_pool"])
    w_proj = regroup(full["w_sgu_proj"])
    pooled, merged, y, h1 = _mix_fwd(z, x, w_pool, row(p["pool_scale"]), row(p["sgu_ln_g"]), row(p["sgu_ln_b"]),
                                     ws, bsp, w_proj, w_out, row(p["norm1_post_g"]), tm_mix)
    ff_arrays = _split_wait("ag_ff_land", send1, recv1, ff_arrays, _plan_gather_send(len(_FF)), pooled)
    send2, recv2, ff_full, token = _split_start("ag_ff_pass", ff_arrays[len(_FF):], 3 * len(_FF),
                                                _plan_gather_pass(len(_FF)))
    ff_full = _split_wait("ag_ff_passed", send2, recv2, ff_full, _plan_gather_pass(len(_FF)), token)
    w_ff1_4 = ff_full[0].reshape(N_SHARD, D, D)
    w_ff2 = ff_full[1].reshape(N_SHARD, D, D)
    hn, f1, f, df2, dout, vec_a = _ffn_fwd(h1, row(p["norm2_pre_g"]), w_ff1_4, w_ff2, row(p["norm2_post_g"]),
                                           target, tm_ffn)

    tk = min(1024, t)
    g_big = {"w_ff2": _dw(f, df2, "dw_ff2", 2, D, tk, N_SHARD, lambda i, j: i)}
    df1, dh1, dy, vec_b = _ffn_bwd(df2, f1, w_ff1_4, w_ff2, h1, dout, y, row(p["norm2_pre_g"]),
                                   row(p["norm1_post_g"]), tm_ffn)
    g_big["w_ff1"] = _dw(hn, df1, "dw_ff1", 2, D, tk, N_SHARD, lambda i, j: j)

    def reduce_start(name, keys, wholes):
        swapped = _swap_halves("rs_swap_" + name, [g_big[k] for k in keys], wholes)
        sums = [_chip_sum(core, g_big[k], r, "chip_sum_" + k) for k, r in zip(keys, swapped)]
        arrays = list(sums)
        if wholes:
            arrays.append(_slot_sum(jnp.stack([wholes[0], swapped[-1]]), "chip_sum_small"))
        arrays += [lax.empty(s.shape, BF16) for s in sums]
        if wholes:
            arrays.append(lax.empty((N_SHARD,) + wholes[0].shape, F32))
        plan = _plan_exchange(len(keys), bool(wholes))
        send, recv, arrays, tok = _split_start("rs_send_" + name, arrays, 3 * len(keys) + 4 * len(wholes), plan)
        return send, recv, arrays, plan, tok

    def reduce_finish(name, keys, state, *after):
        send, recv, arrays, plan, _ = state
        arrays = _split_wait("rs_land_" + name, send, recv, arrays, plan, *after)
        n_in = len(arrays) // 2
        halves = {k: _final_sum(chip, arrays[i], arrays[n_in + i], "final_sum_" + k) for i, k in enumerate(keys)}
        return halves, arrays

    ff_state = reduce_start("ff", _FF, [])
    dw_out = _dw(merged, dy, "dw_out", 1, D, tk, 1, lambda i, j: 0, dep=ff_state[4]).reshape(D, D)
    dz, dwp, dwpr, dws, dbsp, vec_c, dbin = _mix_bwd(z, pooled, dy, w_pool, row(p["pool_scale"]), row(p["sgu_ln_g"]),
                                                     row(p["sgu_ln_b"]), ws, ws_t, bsp, w_proj, w_out, mask, tm_mix)
    g_big["w_in"] = _dw(xn, dz, "dw_in", 2, D_IN // N_SHARD, tk, N_SHARD, lambda i, j: j)
    by_half = lambda a, rows: a.reshape(N_SHARD, 2, rows // 2, a.shape[-1]).transpose(1, 0, 2, 3)
    g_big["w_out"] = by_half(dw_out, D // N_SHARD)
    shard_major = lambda a: a.reshape(N_HEADS, N_SHARD, 64, 256).transpose(1, 0, 2, 3).reshape(N_SHARD * 256, 256)
    g_big["w_pool"] = by_half(shard_major(dwp).astype(BF16), 256)
    g_big["w_sgu_proj"] = by_half(shard_major(dwpr).astype(BF16), 256)
    halves, _ = reduce_finish("ff", _FF, ff_state, g_big["w_in"])

    small_part = _pack(_SMALL[1:], SMALL_ROWS - 8, {
        "b_in": dbin[0], "pool_scale": vec_c[0], "sgu_ln_g": vec_c[1], "sgu_ln_b": vec_c[2],
        "w_spatial": dws, "b_spatial": dbsp[:, ::LANES].T, "norm1_post_g": vec_b[1], "norm2_pre_g": vec_b[0],
        "norm2_post_g": vec_a[0]})
    mix_state = reduce_start("mix", _MIX, [small_part])
    grad_x, vec_d = _dx(dz, w_in4, x, dh1, row(p["norm1_pre_g"]), mix_state[4], tm_mm)
    tiles = _all_reduce_tile(jnp.concatenate([vec_d[0].reshape(8, LANES),
                                              jnp.broadcast_to(vec_a[1:2, :LANES], (8, LANES))]))
    g1_grad, loss = tiles[:8], tiles[8, 0]

    grads, delta, new_m, new_v = {}, {}, {}, {}

    def finish(name, keys):
        shared = _swap_halves("rs_share_" + name, [], [halves[k] for k in keys])
        for k, sib in zip(keys, shared):
            as2d = lambda a: a.reshape(local2d[k].shape)
            outs = _adamw_halves(core, halves[k], sib, local2d[k], as2d(m[k]), as2d(v[k]), "adamw_" + k)
            grads[k], delta[k], new_m[k], new_v[k] = (a.reshape(p[k].shape) for a in outs)

    finish("ff", _FF)
    mix_halves, mix_arrays = reduce_finish("mix", _MIX, mix_state, grad_x, *[new_v[k] for k in _FF])
    halves.update(mix_halves)
    finish("mix", _MIX)
    small_grad = jnp.concatenate([g1_grad, _slot_sum(mix_arrays[-1], "final_sum_small")])
    d_s, m_s, v_s = _adamw(_pack_small(p), small_grad, _pack_small(m), _pack_small(v), "adamw_small")
    grads.update(_unpack_small(small_grad, p))
    delta.update(_unpack_small(d_s, p))
    new_m.update(_unpack_small(m_s, p))
    new_v.update(_unpack_small(v_s, p))

    return (loss, grad_x, *[grads[k] for k in _ORDER], *[delta[k] for k in _ORDER],
            *[new_m[k] for k in _ORDER], *[new_v[k] for k in _ORDER])


def kernel(x, norm1_pre_g, w_in, b_in, w_pool, pool_scale, sgu_ln_g, sgu_ln_b, w_spatial, b_spatial, w_sgu_proj, w_out, norm1_post_g, norm2_pre_g, w_ff1, w_ff2, norm2_post_g, loss_target, m_norm1_pre_g, m_w_in, m_b_in, m_w_pool, m_pool_scale, m_sgu_ln_g, m_sgu_ln_b, m_w_spatial, m_b_spatial, m_w_sgu_proj, m_w_out, m_norm1_post_g, m_norm2_pre_g, m_w_ff1, m_w_ff2, m_norm2_post_g, v_norm1_pre_g, v_w_in, v_b_in, v_w_pool, v_pool_scale, v_sgu_ln_g, v_sgu_ln_b, v_w_spatial, v_b_spatial, v_w_sgu_proj, v_w_out, v_norm1_post_g, v_norm2_pre_g, v_w_ff1, v_w_ff2, v_norm2_post_g):
    p = dict(zip(_ORDER, (norm1_pre_g, w_in, b_in, w_pool, pool_scale, sgu_ln_g, sgu_ln_b, w_spatial, b_spatial,
                          w_sgu_proj, w_out, norm1_post_g, norm2_pre_g, w_ff1, w_ff2, norm2_post_g)))
    m = dict(zip(_ORDER, (m_norm1_pre_g, m_w_in, m_b_in, m_w_pool, m_pool_scale, m_sgu_ln_g, m_sgu_ln_b, m_w_spatial,
                          m_b_spatial, m_w_sgu_proj, m_w_out, m_norm1_post_g, m_norm2_pre_g, m_w_ff1, m_w_ff2,
                          m_norm2_post_g)))
    v = dict(zip(_ORDER, (v_norm1_pre_g, v_w_in, v_b_in, v_w_pool, v_pool_scale, v_sgu_ln_g, v_sgu_ln_b, v_w_spatial,
                          v_b_spatial, v_w_sgu_proj, v_w_out, v_norm1_post_g, v_norm2_pre_g, v_w_ff1, v_w_ff2,
                          v_norm2_post_g)))
    outs = _step(p, m, v, x[0], loss_target[0])
    return (outs[0], outs[1][None], *outs[2:])
```

```python
import math

import jax
import jax.numpy as jnp
from jax import lax
from jax.experimental import pallas as pl
from jax.experimental.pallas import tpu as pltpu

F32 = jnp.float32
BF16 = jnp.bfloat16

D = 1024
D_IN = 5 * D
D_FF = 4 * D
N_SHARD = 4
N_CHIP_PEERS = 3
POOL_WINDOWS = (2, 4, 8, 16)
POOL_GROUP = 256
POOL_HALO = 16
SGU_BLOCK = 128
N_HEADS = 4
HEAD = 256
EPS = 1e-6

ADAM_LR = 0.001
ADAM_B1 = 0.9
ADAM_B2 = 0.999
ADAM_EPS = 1e-08
ADAM_WD = 0.01
ADAM_STEP = 10

V7X_VMEM_BYTES = 64 * 1024 * 1024
VMEM_LIMIT = V7X_VMEM_BYTES - 8 * 1024 * 1024
SMALL_ROWS = 616
LANES = 128

MESH = pl.DeviceIdType.MESH
ANY = pl.BlockSpec(memory_space=pl.ANY)
HBM = pl.BlockSpec(memory_space=pltpu.HBM)
SEM = pl.BlockSpec(memory_space=pltpu.SEMAPHORE)
EFFECT = pltpu.SideEffectType.DATAFLOW_SIDE_EFFECTING

NT_DIMS = (((1,), (1,)), ((), ()))
TN_DIMS = (((0,), (0,)), ((), ()))


def _params(n_axes):
    return pltpu.CompilerParams(dimension_semantics=("arbitrary",) * n_axes, vmem_limit_bytes=VMEM_LIMIT)


def _resident(shape, n_axes):
    zeros = (0,) * len(shape)
    index = (lambda i: zeros) if n_axes == 1 else (lambda i, j: zeros)
    return pl.BlockSpec(shape, index, pipeline_mode=pl.Buffered(1))


def _dot(a, b):
    return jnp.dot(a, b, preferred_element_type=F32)


def _dot_nt(a, b):
    return lax.dot_general(a, b, NT_DIMS, preferred_element_type=F32)


def _dot_tn(a, b):
    return lax.dot_general(a, b, TN_DIMS, preferred_element_type=F32)


def _mean(a):
    return jnp.mean(a, axis=-1, keepdims=True)


def _colsum(a):
    return jnp.sum(a, axis=0, keepdims=True)


_GELU_C0 = math.sqrt(2.0 / math.pi)
_GELU_C1 = 0.044715


def _gelu(a):
    t = jnp.tanh(a * (_GELU_C0 + (_GELU_C0 * _GELU_C1) * (a * a)))
    return a * (0.5 * t + 0.5)


def _gelu_and_grad(a):
    a2 = a * a
    t = jnp.tanh(a * (_GELU_C0 + (_GELU_C0 * _GELU_C1) * a2))
    cdf = 0.5 * t + 0.5
    grad = cdf + (0.5 * a) * (1.0 - t * t) * (_GELU_C0 + (3.0 * _GELU_C0 * _GELU_C1) * a2)
    return a * cdf, grad


def _sigmoid(a):
    return 0.5 * jnp.tanh(0.5 * a) + 0.5


def _window_sum(a, w, ahead):
    n = a.shape[0]
    step = 1
    while step < w:
        a = a + pltpu.roll(a, n - step if ahead else step, 0)
        step *= 2
    return a


def _rms_bwd(dn, n, r):
    return r * (dn - n * _mean(dn * n))


def _inproj(x, g1, w_in4, b_in, dep, tm):
    t = x.shape[0]
    cw = D_IN // N_SHARD

    def body(x_ref, g_ref, w_ref, b_ref, dep_ref, z_ref, xn_ref):
        @pl.when(pl.program_id(1) == 0)
        def _():
            xv = x_ref[...]
            r = lax.rsqrt(_mean(xv * xv) + EPS)
            xn_ref[...] = (xv * r * g_ref[...]).astype(BF16)

        z_ref[...] = _dot(xn_ref[...], w_ref[pl.program_id(1)]) + b_ref[...]

    return pl.pallas_call(
        body, name="inproj",
        grid=(t // tm, N_SHARD),
        in_specs=[pl.BlockSpec((tm, D), lambda i, s: (i, 0)),
                  pl.BlockSpec((1, D), lambda i, s: (0, 0)),
                  _resident((N_SHARD, D, cw), 2),
                  pl.BlockSpec((1, cw), lambda i, s: (0, s)),
                  pl.BlockSpec(dep.shape, lambda i, s: (0, 0))],
        out_specs=[pl.BlockSpec((tm, cw), lambda i, s: (i, s)),
                   pl.BlockSpec((tm, D), lambda i, s: (i, 0))],
        out_shape=[jax.ShapeDtypeStruct((t, D_IN), F32), jax.ShapeDtypeStruct((t, D), BF16)],
        compiler_params=_params(2),
    )(x, g1, w_in4, b_in, dep)


def _sgu_forward(z_ref, lng_ref, lnb_ref, ws_ref, bsp_ref, sv_ref, tm, with_grad):
    if with_grad:
        u, du_dz = _gelu_and_grad(z_ref[:, D:2 * D])
        gv, dgv_dz = _gelu_and_grad(z_ref[:, 2 * D:3 * D])
    else:
        u = _gelu(z_ref[:, D:2 * D])
        gv = _gelu(z_ref[:, 2 * D:3 * D])
        du_dz = dgv_dz = None
    xc = gv - _mean(gv)
    rstd = lax.rsqrt(_mean(xc * xc) + EPS)
    vhat = xc * rstd
    vb = (vhat * lng_ref[...] + lnb_ref[...]).astype(BF16)
    for b in range(tm // SGU_BLOCK):
        rows = slice(b * SGU_BLOCK, (b + 1) * SGU_BLOCK)
        for h in range(N_HEADS):
            cols = slice(h * HEAD, (h + 1) * HEAD)
            sv_ref[rows, cols] = _dot(ws_ref[h], vb[rows, cols]) + bsp_ref[h]
    return u, du_dz, dgv_dz, vhat, rstd, vb


def _mix_fwd(z, x, w_pool, pool_scale, ln_g, ln_b, ws, bsp, w_proj, w_out, g1post, tm):
    t = x.shape[0]

    def body(z_ref, x_ref, wp_ref, psc_ref, lng_ref, lnb_ref, ws_ref, bsp_ref, wpr_ref, wo_ref, g1p_ref,
             pooled_ref, merged_ref, y_ref, h1_ref, win_ref, sv_ref):
        i = pl.program_id(0)

        @pl.when(i == 0)
        def _():
            win_ref[0:POOL_HALO, :] = jnp.zeros((POOL_HALO, D), F32)

        win_ref[POOL_HALO:POOL_HALO + tm, :] = z_ref[:, 0:D]
        pos1 = lax.broadcasted_iota(jnp.int32, (tm, 1), 0) + (i * tm + 1)
        a_parts = []
        for g, w in enumerate(POOL_WINDOWS):
            cols = slice(g * POOL_GROUP, (g + 1) * POOL_GROUP)
            ts = _window_sum(win_ref[:, cols], w, False)[POOL_HALO:POOL_HALO + tm, :]
            inv_cnt = 1.0 / jnp.minimum(pos1, w).astype(F32)
            pooled = (ts * inv_cnt - z_ref[:, cols]).astype(BF16)
            pooled_ref[:, cols] = pooled
            a_parts.append(_dot(pooled, wp_ref[g]) * psc_ref[:, cols])
        win_ref[0:POOL_HALO, :] = win_ref[tm:tm + POOL_HALO, :]

        u, _, _, _, _, _ = _sgu_forward(z_ref, lng_ref, lnb_ref, ws_ref, bsp_ref, sv_ref, tm, False)
        gated = (u * sv_ref[...]).astype(BF16)
        b_parts = [_dot(gated[:, h * HEAD:(h + 1) * HEAD], wpr_ref[h]) for h in range(N_HEADS)]

        a = jnp.concatenate(a_parts, axis=1)
        bbr = jnp.concatenate(b_parts, axis=1)
        merged = (_sigmoid(z_ref[:, 3 * D:4 * D]) * a + _sigmoid(z_ref[:, 4 * D:5 * D]) * bbr).astype(BF16)
        merged_ref[...] = merged
        y = _dot(merged, wo_ref[...])
        y_ref[...] = y
        r = lax.rsqrt(_mean(y * y) + EPS)
        h1_ref[...] = x_ref[...] + (y * r) * g1p_ref[...]

    row = lambda i: (i, 0)
    fixed2 = lambda i: (0, 0)
    fixed3 = lambda i: (0, 0, 0)
    vec = pl.BlockSpec((1, D), fixed2)
    return pl.pallas_call(
        body, name="mix_fwd",
        grid=(t // tm,),
        in_specs=[pl.BlockSpec((tm, D_IN), row), pl.BlockSpec((tm, D), row),
                  pl.BlockSpec((N_HEADS, POOL_GROUP, POOL_GROUP), fixed3), vec, vec, vec,
                  pl.BlockSpec((N_HEADS, SGU_BLOCK, SGU_BLOCK), fixed3),
                  pl.BlockSpec((N_HEADS, SGU_BLOCK, 1), fixed3),
                  pl.BlockSpec((N_HEADS, HEAD, HEAD), fixed3),
                  pl.BlockSpec((D, D), fixed2), vec],
        out_specs=[pl.BlockSpec((tm, D), row)] * 4,
        out_shape=[jax.ShapeDtypeStruct((t, D), BF16), jax.ShapeDtypeStruct((t, D), BF16),
                   jax.ShapeDtypeStruct((t, D), F32), jax.ShapeDtypeStruct((t, D), F32)],
        scratch_shapes=[pltpu.VMEM((tm + POOL_HALO, D), F32), pltpu.VMEM((tm, D), F32)],
        compiler_params=_params(1),
    )(z, x, w_pool, pool_scale, ln_g, ln_b, ws, bsp, w_proj, w_out, g1post)


def _ffn_fwd(h1, g2pre, w1, w2, g2post, target, tm):
    t = h1.shape[0]
    n_j = D_FF // D

    def body(h1_ref, g2_ref, w1_ref, w2_ref, g2p_ref, tgt_ref,
             hn_ref, f1_ref, f_ref, df2_ref, dout_ref, vec_ref, acc_ref):
        i, j = pl.program_id(0), pl.program_id(1)

        @pl.when((i == 0) & (j == 0))
        def _():
            vec_ref[...] = jnp.zeros_like(vec_ref)

        @pl.when(j == 0)
        def _():
            h = h1_ref[...]
            r = lax.rsqrt(_mean(h * h) + EPS)
            hn_ref[...] = (h * r * g2_ref[...]).astype(BF16)
            acc_ref[...] = jnp.zeros_like(acc_ref)

        f1 = _dot(hn_ref[...], w1_ref[j])
        f1_ref[...] = f1
        rl = jnp.maximum(f1, 0.0)
        f = (rl * rl).astype(BF16)
        f_ref[...] = f
        acc_ref[...] += _dot(f, w2_ref[j])

        @pl.when(j == n_j - 1)
        def _():
            f2 = acc_ref[...]
            r = lax.rsqrt(_mean(f2 * f2) + EPS)
            n2 = f2 * r
            err = h1_ref[...] + n2 * g2p_ref[...] - tgt_ref[...]
            loss = 0.5 * jnp.sum(_mean(err * err))
            dout = err * (1.0 / D)
            dout_ref[...] = dout
            vec_ref[0:1, :] += _colsum(dout * n2)
            vec_ref[1:2, :] += jnp.full((1, D), loss, F32)
            df2_ref[...] = _rms_bwd(dout * g2p_ref[...], n2, r).astype(BF16)

    row = lambda i, j: (i, 0)
    fixed = lambda i, j: (0, 0)
    vec = pl.BlockSpec((1, D), fixed)
    return pl.pallas_call(
        body, name="ffn_fwd",
        grid=(t // tm, n_j),
        in_specs=[pl.BlockSpec((tm, D), row), vec,
                  _resident((n_j, D, D), 2), _resident((n_j, D, D), 2), vec,
                  pl.BlockSpec((tm, D), row)],
        out_specs=[pl.BlockSpec((tm, D), row),
                   pl.BlockSpec((tm, D), lambda i, j: (i, j)),
                   pl.BlockSpec((tm, D), lambda i, j: (i, j)),
                   pl.BlockSpec((tm, D), row), pl.BlockSpec((tm, D), row),
                   pl.BlockSpec((8, D), fixed)],
        out_shape=[jax.ShapeDtypeStruct((t, D), BF16), jax.ShapeDtypeStruct((t, D_FF), F32),
                   jax.ShapeDtypeStruct((t, D_FF), BF16), jax.ShapeDtypeStruct((t, D), BF16),
                   jax.ShapeDtypeStruct((t, D), F32), jax.ShapeDtypeStruct((8, D), F32)],
        scratch_shapes=[pltpu.VMEM((tm, D), F32)],
        compiler_params=_params(2),
    )(h1, g2pre, w1, w2, g2post, target)


def _ffn_bwd(df2, f1, w1, w2, h1, dout, y, g2pre, g1post, tm):
    t = h1.shape[0]
    n_j = D_FF // D

    def body(df2_ref, f1_ref, w1_ref, w2_ref, h1_ref, dout_ref, y_ref, g2_ref, g1p_ref,
             df1_ref, dh1_ref, dy_ref, vec_ref, acc_ref):
        i, j = pl.program_id(0), pl.program_id(1)

        @pl.when((i == 0) & (j == 0))
        def _():
            vec_ref[...] = jnp.zeros_like(vec_ref)

        @pl.when(j == 0)
        def _():
            acc_ref[...] = jnp.zeros_like(acc_ref)

        df = _dot_nt(df2_ref[...], w2_ref[j])
        df1 = (df * (2.0 * jnp.maximum(f1_ref[...], 0.0))).astype(BF16)
        df1_ref[...] = df1
        acc_ref[...] += _dot_nt(df1, w1_ref[j])

        @pl.when(j == n_j - 1)
        def _():
            dhn = acc_ref[...]
            h = h1_ref[...]
            r2 = lax.rsqrt(_mean(h * h) + EPS)
            nh = h * r2
            vec_ref[0:1, :] += _colsum(dhn * nh)
            dh1 = dout_ref[...] + _rms_bwd(dhn * g2_ref[...], nh, r2)
            dh1_ref[...] = dh1
            yv = y_ref[...]
            r1 = lax.rsqrt(_mean(yv * yv) + EPS)
            ny = yv * r1
            vec_ref[1:2, :] += _colsum(dh1 * ny)
            dy_ref[...] = _rms_bwd(dh1 * g1p_ref[...], ny, r1).astype(BF16)

    row = lambda i, j: (i, 0)
    fixed = lambda i, j: (0, 0)
    vec = pl.BlockSpec((1, D), fixed)
    return pl.pallas_call(
        body, name="ffn_bwd",
        grid=(t // tm, n_j),
        in_specs=[pl.BlockSpec((tm, D), row),
                  pl.BlockSpec((tm, D), lambda i, j: (i, j)),
                  _resident((n_j, D, D), 2), _resident((n_j, D, D), 2),
                  pl.BlockSpec((tm, D), row), pl.BlockSpec((tm, D), row), pl.BlockSpec((tm, D), row), vec, vec],
        out_specs=[pl.BlockSpec((tm, D), lambda i, j: (i, j)),
                   pl.BlockSpec((tm, D), row), pl.BlockSpec((tm, D), row),
                   pl.BlockSpec((8, D), fixed)],
        out_shape=[jax.ShapeDtypeStruct((t, D_FF), BF16), jax.ShapeDtypeStruct((t, D), F32),
                   jax.ShapeDtypeStruct((t, D), BF16), jax.ShapeDtypeStruct((8, D), F32)],
        scratch_shapes=[pltpu.VMEM((tm, D), F32)],
        compiler_params=_params(2),
    )(df2, f1, w1, w2, h1, dout, y, g2pre, g1post)


def _mix_bwd(z, pooled, dy, w_pool, pool_scale, ln_g, ln_b, ws, ws_t, bsp, w_proj, w_out, mask, tm, dep):
    t = z.shape[0]
    n_t = t // tm

    def body(z_ref, pooled_ref, dy_ref, wp_ref, psc_ref, lng_ref, lnb_ref, ws_ref, wst_ref, bsp_ref, wpr_ref,
             wo_ref, mask_ref, dep_ref,
             dz_ref, dwp_ref, dwpr_ref, dws_ref, dbsp_ref, vec_ref, dbin_ref,
             win_ref, sv_ref, dsv_ref, dv_ref, dsvsum_ref):
        i = pl.program_id(0)

        @pl.when(i == 0)
        def _():
            win_ref[tm:tm + POOL_HALO, :] = jnp.zeros((POOL_HALO, D), F32)
            dwp_ref[...] = jnp.zeros_like(dwp_ref)
            dwpr_ref[...] = jnp.zeros_like(dwpr_ref)
            dws_ref[...] = jnp.zeros_like(dws_ref)
            vec_ref[...] = jnp.zeros_like(vec_ref)
            dbin_ref[...] = jnp.zeros_like(dbin_ref)
            dsvsum_ref[...] = jnp.zeros_like(dsvsum_ref)

        dmerged = _dot_nt(dy_ref[...], wo_ref[...])

        u, du_dz, dgv_dz, vhat, rstd, vb = _sgu_forward(z_ref, lng_ref, lnb_ref, ws_ref, bsp_ref, sv_ref, tm, True)
        sv = sv_ref[...]
        gated = (u * sv).astype(BF16)

        sa = _sigmoid(z_ref[:, 3 * D:4 * D])
        sb = _sigmoid(z_ref[:, 4 * D:5 * D])
        da = dmerged * sa
        dbbr = (dmerged * sb).astype(BF16)

        tile = n_t - 1 - i
        pos1 = lax.broadcasted_iota(jnp.int32, (tm, 1), 0) + (tile * tm + 1)
        dzga_parts, dpooled_parts = [], []
        for g, w in enumerate(POOL_WINDOWS):
            cols = slice(g * POOL_GROUP, (g + 1) * POOL_GROUP)
            pg = pooled_ref[:, cols]
            a_pre = _dot(pg, wp_ref[g])
            da_g = da[:, cols]
            psc = psc_ref[:, cols]
            vec_ref[0:1, cols] += _colsum(da_g * a_pre)
            sa_g = sa[:, cols]
            dzga_parts.append(dmerged[:, cols] * (a_pre * psc) * (sa_g * (1.0 - sa_g)))
            da_pre = (da_g * psc).astype(BF16)
            dwp_ref[g] += _dot_tn(pg, da_pre)
            dpooled = _dot_nt(da_pre, wp_ref[g])
            dpooled_parts.append(dpooled)
            inv_cnt = 1.0 / jnp.minimum(pos1, w).astype(F32)
            win_ref[0:tm, cols] = dpooled * inv_cnt
        dzp_parts = []
        for g, w in enumerate(POOL_WINDOWS):
            cols = slice(g * POOL_GROUP, (g + 1) * POOL_GROUP)
            acc = _window_sum(win_ref[:, cols], w, True)[0:tm, :]
            dzp_parts.append(acc - dpooled_parts[g])
        win_ref[tm:tm + POOL_HALO, :] = win_ref[0:POOL_HALO, :]

        dzgb_parts = []
        for h in range(N_HEADS):
            cols = slice(h * HEAD, (h + 1) * HEAD)
            g_h = gated[:, cols]
            db_h = dbbr[:, cols]
            bbr_h = _dot(g_h, wpr_ref[h])
            sb_h = sb[:, cols]
            dzgb_parts.append(dmerged[:, cols] * bbr_h * (sb_h * (1.0 - sb_h)))
            dwpr_ref[h] += _dot_tn(g_h, db_h)
            dgated_h = _dot_nt(db_h, wpr_ref[h])
            dsv_ref[:, cols] = dgated_h * u[:, cols]
            sv_ref[:, cols] = dgated_h * sv[:, cols]
        dzu = sv_ref[...] * du_dz
        dsv = dsv_ref[...]
        dsvb = dsv.astype(BF16)
        blk_sum = dsv[0:SGU_BLOCK, :]
        for b in range(1, tm // SGU_BLOCK):
            blk_sum = blk_sum + dsv[b * SGU_BLOCK:(b + 1) * SGU_BLOCK, :]
        dsvsum_ref[...] += blk_sum
        for b in range(tm // SGU_BLOCK):
            rows = slice(b * SGU_BLOCK, (b + 1) * SGU_BLOCK)
            for h in range(N_HEADS):
                cols = slice(h * HEAD, (h + 1) * HEAD)
                dws_ref[h] += _dot_nt(dsvb[rows, cols], vb[rows, cols])
                dv_ref[rows, cols] = _dot(wst_ref[h], dsvb[rows, cols])
        dv = dv_ref[...]
        vec_ref[1:2, :] += _colsum(dv * vhat)
        vec_ref[2:3, :] += _colsum(dv)
        dvhat = dv * lng_ref[...]
        dgv = rstd * (dvhat - _mean(dvhat) - vhat * _mean(dvhat * vhat))
        dzv = dgv * dgv_dz

        parts = [jnp.concatenate(dzp_parts, axis=1), dzu, dzv,
                 jnp.concatenate(dzga_parts, axis=1), jnp.concatenate(dzgb_parts, axis=1)]
        for k, part in enumerate(parts):
            dz_ref[:, k * D:(k + 1) * D] = part.astype(BF16)
            dbin_ref[0:1, k * D:(k + 1) * D] += _colsum(part)

        @pl.when(i == n_t - 1)
        def _():
            for h in range(N_HEADS):
                dws_ref[h] = jnp.where(mask_ref[...] > 0.0, dws_ref[h], 0.0)
                tot = jnp.sum(dsvsum_ref[:, h * HEAD:(h + 1) * HEAD], axis=1, keepdims=True)
                dbsp_ref[:, h * LANES:(h + 1) * LANES] = jnp.broadcast_to(tot, (SGU_BLOCK, LANES))

    row = lambda i: (n_t - 1 - i, 0)
    fixed2 = lambda i: (0, 0)
    fixed3 = lambda i: (0, 0, 0)
    vec = pl.BlockSpec((1, D), fixed2)
    sq = pl.BlockSpec((N_HEADS, SGU_BLOCK, SGU_BLOCK), fixed3)
    grp = pl.BlockSpec((N_HEADS, HEAD, HEAD), fixed3)
    return pl.pallas_call(
        body, name="mix_bwd",
        grid=(n_t,),
        in_specs=[pl.BlockSpec((tm, D_IN), row), pl.BlockSpec((tm, D), row), pl.BlockSpec((tm, D), row),
                  grp, vec, vec, vec, sq, sq, pl.BlockSpec((N_HEADS, SGU_BLOCK, 1), fixed3), grp,
                  pl.BlockSpec((D, D), fixed2), pl.BlockSpec((SGU_BLOCK, SGU_BLOCK), fixed2),
                  pl.BlockSpec(dep.shape, fixed2)],
        out_specs=[pl.BlockSpec((tm, D_IN), row), grp, grp, sq,
                   pl.BlockSpec((SGU_BLOCK, N_HEADS * LANES), fixed2),
                   pl.BlockSpec((8, D), fixed2), pl.BlockSpec((8, D_IN), fixed2)],
        out_shape=[jax.ShapeDtypeStruct((t, D_IN), BF16),
                   jax.ShapeDtypeStruct((N_HEADS, HEAD, HEAD), F32), jax.ShapeDtypeStruct((N_HEADS, HEAD, HEAD), F32),
                   jax.ShapeDtypeStruct((N_HEADS, SGU_BLOCK, SGU_BLOCK), F32),
                   jax.ShapeDtypeStruct((SGU_BLOCK, N_HEADS * LANES), F32),
                   jax.ShapeDtypeStruct((8, D), F32), jax.ShapeDtypeStruct((8, D_IN), F32)],
        scratch_shapes=[pltpu.VMEM((tm + POOL_HALO, D), F32), pltpu.VMEM((tm, D), F32), pltpu.VMEM((tm, D), F32),
                        pltpu.VMEM((tm, D), F32), pltpu.VMEM((SGU_BLOCK, D), F32)],
        compiler_params=_params(1),
    )(z, pooled, dy, w_pool, pool_scale, ln_g, ln_b, ws, ws_t, bsp, w_proj, w_out, mask, dep)


def _dx(dz, w_in4, x, dh1, g1, dep, tm):
    t = x.shape[0]
    cw = D_IN // N_SHARD

    def body(dz_ref, w_ref, x_ref, dh1_ref, g_ref, dep_ref, gx_ref, vec_ref, acc_ref):
        i, s = pl.program_id(0), pl.program_id(1)

        @pl.when((i == 0) & (s == 0))
        def _():
            vec_ref[...] = jnp.zeros_like(vec_ref)

        @pl.when(s == 0)
        def _():
            acc_ref[...] = jnp.zeros_like(acc_ref)

        acc_ref[...] += _dot_nt(dz_ref[...], w_ref[s])

        @pl.when(s == N_SHARD - 1)
        def _():
            dxn = acc_ref[...]
            xv = x_ref[...]
            r = lax.rsqrt(_mean(xv * xv) + EPS)
            xh = xv * r
            vec_ref[0:1, :] += _colsum(dxn * xh)
            gx_ref[...] = dh1_ref[...] + _rms_bwd(dxn * g_ref[...], xh, r)

    row = lambda i, s: (i, 0)
    fixed = lambda i, s: (0, 0)
    return pl.pallas_call(
        body, name="dx",
        grid=(t // tm, N_SHARD),
        in_specs=[pl.BlockSpec((tm, cw), lambda i, s: (i, s)),
                  _resident((N_SHARD, D, cw), 2),
                  pl.BlockSpec((tm, D), row), pl.BlockSpec((tm, D), row), pl.BlockSpec((1, D), fixed),
                  pl.BlockSpec(dep.shape, fixed)],
        out_specs=[pl.BlockSpec((tm, D), row), pl.BlockSpec((8, D), fixed)],
        out_shape=[jax.ShapeDtypeStruct((t, D), F32), jax.ShapeDtypeStruct((8, D), F32)],
        scratch_shapes=[pltpu.VMEM((tm, D), F32)],
        compiler_params=_params(2),
    )(dz, w_in4, x, dh1, g1, dep)


def _dw(a, b, name, n_split, tn, tk, n_out, out_block, dep=None):
    t, m = a.shape
    n = b.shape[1]
    tm = min(m, D)
    rows = tm // n_split
    n_k = t // tk

    def body(a_ref, b_ref, *rest):
        o_ref, acc_ref = rest[-2:]
        k = pl.program_id(2)

        @pl.when(k == 0)
        def _():
            acc_ref[...] = jnp.zeros_like(acc_ref)

        acc_ref[...] += _dot_tn(a_ref[...], b_ref[...])

        @pl.when(k == n_k - 1)
        def _():
            for h in range(n_split):
                o_ref[h, 0] = acc_ref[h * rows:(h + 1) * rows, :].astype(BF16)

    return pl.pallas_call(
        body, name=name,
        grid=(m // tm, n // tn, n_k),
        in_specs=[pl.BlockSpec((tk, tm), lambda i, j, k: (k, i)),
                  pl.BlockSpec((tk, tn), lambda i, j, k: (k, j))]
        + ([] if dep is None else [pl.BlockSpec(dep.shape, lambda i, j, k: (0, 0))]),
        out_specs=pl.BlockSpec((n_split, 1, rows, tn), lambda i, j, k: (0, out_block(i, j), 0, 0)),
        out_shape=jax.ShapeDtypeStruct((n_split, n_out, rows, tn), BF16),
        scratch_shapes=[pltpu.VMEM((tm, tn), F32)],
        compiler_params=_params(3),
    )(*((a, b) if dep is None else (a, b, dep)))


def _row_block(rows, cols, itemsize, n_bufs):
    budget = VMEM_LIMIT // 4
    rb = rows
    while rb % 16 == 0 and rb * cols * itemsize * n_bufs * 2 > budget:
        rb //= 2
    return rb


def _chip_sum(core, g, recv, name):
    _, n_s, rh, cols = g.shape
    rb = _row_block(rh, cols, 4, 3)

    def body(c_ref, g_ref, r_ref, o_ref):
        o_ref[...] = (g_ref[0].astype(F32) + r_ref[...].astype(F32)).astype(BF16)

    return pl.pallas_call(
        body, name=name,
        grid_spec=pltpu.PrefetchScalarGridSpec(
            num_scalar_prefetch=1, grid=(n_s, rh // rb),
            in_specs=[pl.BlockSpec((1, 1, rb, cols), lambda s, r, c: (c[0], s, r, 0)),
                      pl.BlockSpec((1, rb, cols), lambda s, r, c: (s, r, 0))],
            out_specs=pl.BlockSpec((1, rb, cols), lambda s, r, c: (s, r, 0))),
        out_shape=jax.ShapeDtypeStruct((n_s, rh, cols), BF16),
        compiler_params=_params(2),
    )(core, g, recv)


def _slot_sum(slots, name):
    n_s, rows, cols = slots.shape
    rb = _row_block(rows, cols, 4, n_s + 1)

    def body(s_ref, o_ref):
        acc = s_ref[0].astype(F32)
        for k in range(1, n_s):
            acc = acc + s_ref[k].astype(F32)
        o_ref[...] = acc

    return pl.pallas_call(
        body, name=name,
        grid=(rows // rb,),
        in_specs=[pl.BlockSpec((n_s, rb, cols), lambda r: (0, r, 0))],
        out_specs=pl.BlockSpec((rb, cols), lambda r: (r, 0)),
        out_shape=jax.ShapeDtypeStruct((rows, cols), F32),
        compiler_params=_params(1),
    )(slots)


def _adamw(w, g, m, v, name):
    rows, cols = w.shape
    rb = _row_block(rows, cols, 4, 7)
    c1 = 1.0 - ADAM_B1 ** ADAM_STEP
    c2 = 1.0 - ADAM_B2 ** ADAM_STEP

    def body(w_ref, g_ref, m_ref, v_ref, d_ref, mo_ref, vo_ref):
        gv = g_ref[...]
        m_new = ADAM_B1 * m_ref[...] + (1.0 - ADAM_B1) * gv
        v_new = ADAM_B2 * v_ref[...] + (1.0 - ADAM_B2) * (gv * gv)
        mo_ref[...] = m_new
        vo_ref[...] = v_new
        d_ref[...] = -ADAM_LR * ((m_new / c1) / (jnp.sqrt(v_new / c2) + ADAM_EPS) + ADAM_WD * w_ref[...])

    blk = pl.BlockSpec((rb, cols), lambda r: (r, 0))
    return pl.pallas_call(
        body, name=name,
        grid=(rows // rb,),
        in_specs=[blk] * 4, out_specs=[blk] * 3,
        out_shape=[jax.ShapeDtypeStruct((rows, cols), F32)] * 3,
        compiler_params=_params(1),
    )(w, g, m, v)


def _final_sum(chip, own, slots, name):
    n_s, rh, cols = own.shape
    rb = _row_block(rh, cols, 4, n_s + 1)

    def body(chip_ref, own_ref, s1_ref, s2_ref, s3_ref, o_ref):
        acc = own_ref[0].astype(F32) + s1_ref[0].astype(F32)
        acc = acc + s2_ref[0].astype(F32)
        o_ref[...] = acc + s3_ref[0].astype(F32)

    def slot(k):
        return pl.BlockSpec((1, rb, cols), lambda r, chip: (lax.rem(chip[0] + k, n_s), r, 0))

    return pl.pallas_call(
        body, name=name,
        grid_spec=pltpu.PrefetchScalarGridSpec(
            num_scalar_prefetch=1, grid=(rh // rb,),
            in_specs=[slot(0), slot(1), slot(2), slot(3)],
            out_specs=pl.BlockSpec((rb, cols), lambda r, chip: (r, 0))),
        out_shape=jax.ShapeDtypeStruct((rh, cols), F32),
        compiler_params=_params(1),
    )(chip, own, slots, slots, slots)


def _adamw_halves(core, own, sibling, w, m, v, name):
    rows, cols = w.shape
    rh = rows // 2
    rb = _row_block(rh, cols, 4, 9)
    n_r = rh // rb
    c1 = 1.0 - ADAM_B1 ** ADAM_STEP
    c2 = 1.0 - ADAM_B2 ** ADAM_STEP

    def body(c_ref, own_ref, sib_ref, w_ref, m_ref, v_ref, g_ref, d_ref, mo_ref, vo_ref):
        mine = c_ref[0] == pl.program_id(0)

        @pl.when(mine)
        def _():
            g_ref[...] = own_ref[...]

        @pl.when(jnp.logical_not(mine))
        def _():
            g_ref[...] = sib_ref[...]

        gv = g_ref[...]
        m_new = ADAM_B1 * m_ref[...] + (1.0 - ADAM_B1) * gv
        v_new = ADAM_B2 * v_ref[...] + (1.0 - ADAM_B2) * (gv * gv)
        mo_ref[...] = m_new
        vo_ref[...] = v_new
        d_ref[...] = -ADAM_LR * ((m_new / c1) / (jnp.sqrt(v_new / c2) + ADAM_EPS) + ADAM_WD * w_ref[...])

    half = pl.BlockSpec((rb, cols), lambda h, r, c: (r, 0))
    whole = pl.BlockSpec((rb, cols), lambda h, r, c: (h * n_r + r, 0))
    return pl.pallas_call(
        body, name=name,
        grid_spec=pltpu.PrefetchScalarGridSpec(
            num_scalar_prefetch=1, grid=(2, n_r),
            in_specs=[half, half, whole, whole, whole], out_specs=[whole] * 4),
        out_shape=[jax.ShapeDtypeStruct((rows, cols), F32)] * 4,
        compiler_params=_params(2),
    )(core, own, sibling, w, m, v)


def _place():
    x, y, c = lax.axis_index("x"), lax.axis_index("y"), lax.axis_index("c")
    chips = [(1 - x, y), (x, 1 - y), (1 - x, 1 - y)]
    return x, y, c, chips


def _remote(src, dst, send_sem, recv_sem, device):
    return pltpu.make_async_remote_copy(src_ref=src, dst_ref=dst, send_sem=send_sem, recv_sem=recv_sem,
                                        device_id=device, device_id_type=MESH)


def _all_gather_weights(shards):
    n = len(shards)

    def body(*refs):
        ins, outs = refs[:n], refs[n:2 * n]
        send_ici, recv_ici, send_d2d, recv_d2d, send_own, recv_own = refs[2 * n:]
        x, y, c, chips = _place()
        me = 2 * x + y
        sibling = (x, y, 1 - c)
        local = [_remote(ins[w], outs[w].at[me], send_own.at[w], recv_own.at[w], sibling) for w in range(n)]
        for cp in local:
            cp.start()
        sends = []
        for j, (px, py) in enumerate(chips):
            for w in range(n):
                cp = _remote(ins[w].at[c], outs[w].at[me, c], send_ici.at[w, j], recv_ici.at[w, j], (px, py, c))
                cp.start()
                sends.append(cp)
        for j, (px, py) in enumerate(chips):
            src_chip = 2 * px + py
            for w in range(n):
                landed = outs[w].at[src_chip, c]
                _remote(landed, landed, send_ici.at[w, j], recv_ici.at[w, j], (px, py, c)).wait_recv()
                cp = _remote(landed, landed, send_d2d.at[w, j], recv_d2d.at[w, j], sibling)
                cp.start()
                sends.append(cp)
        for j, (px, py) in enumerate(chips):
            src_chip = 2 * px + py
            for w in range(n):
                passed = outs[w].at[src_chip, 1 - c]
                _remote(passed, passed, send_d2d.at[w, j], recv_d2d.at[w, j], sibling).wait_recv()
        for cp in sends:
            cp.wait_send()
        for cp in local:
            cp.wait()

    return pl.pallas_call(
        body, name="ag_weights",
        in_specs=[ANY] * n, out_specs=[ANY] * n,
        out_shape=[jax.ShapeDtypeStruct((N_SHARD,) + s.shape, s.dtype) for s in shards],
        scratch_shapes=[pltpu.SemaphoreType.DMA((n, N_CHIP_PEERS))] * 4 + [pltpu.SemaphoreType.DMA((n,))] * 2,
    )(*shards)


def _split_start(name, arrays, n_copies, plan, after=None):
    n = len(arrays)
    n_in = n + (after is not None)

    def body(*refs):
        ins, send_sem, recv_sem, token = refs[:n], refs[n_in], refs[n_in + 1], refs[-1]
        for k, (src, dst, device) in enumerate(plan(ins)):
            _remote(src, dst, send_sem.at[k], recv_sem.at[k], device).start()
        token[...] = jnp.zeros_like(token)

    outs = pl.pallas_call(
        body, name=name,
        in_specs=[HBM] * n + [ANY] * (n_in - n),
        out_specs=(SEM, SEM, *[HBM] * n, pl.BlockSpec(memory_space=pltpu.VMEM)),
        out_shape=(pltpu.SemaphoreType.DMA((n_copies,)), pltpu.SemaphoreType.DMA((n_copies,)),
                   *[pltpu.HBM(a.shape, a.dtype) for a in arrays], jax.ShapeDtypeStruct((8, LANES), F32)),
        input_output_aliases={i: i + 2 for i in range(n)},
        compiler_params=pltpu.CompilerParams(has_side_effects=EFFECT),
    )(*[pltpu.with_memory_space_constraint(a, pltpu.HBM) for a in arrays], *([] if after is None else [after]))
    return outs[0], outs[1], list(outs[2:2 + n]), outs[-1]


def _split_wait(name, send_sem, recv_sem, arrays, plan, *after):
    n = len(arrays)

    def body(*refs):
        ins, send_ref, recv_ref = refs[:n], refs[n], refs[n + 1]
        for k, (src, dst, device) in enumerate(plan(ins)):
            cp = _remote(src, dst, send_ref.at[k], recv_ref.at[k], device)
            cp.wait_send()
            cp.wait_recv()

    outs = pl.pallas_call(
        body, name=name,
        in_specs=[HBM] * n + [SEM, SEM] + [ANY] * len(after), out_specs=[HBM] * n,
        out_shape=[pltpu.HBM(a.shape, a.dtype) for a in arrays],
        input_output_aliases={i: i for i in range(n)},
        compiler_params=pltpu.CompilerParams(has_side_effects=EFFECT),
    )(*arrays, send_sem, recv_sem, *after)
    return list(outs)


def _plan_gather_send(n):
    def plan(refs):
        x, y, c, chips = _place()
        me = 2 * x + y
        copies = []
        for w in range(n):
            own, full = refs[w], refs[n + w]
            copies.append((own, full.at[me], (x, y, 1 - c)))
            copies += [(own.at[c], full.at[me, c], (px, py, c)) for px, py in chips]
        return copies
    return plan


def _plan_gather_direct(n):
    def plan(refs):
        x, y, c, chips = _place()
        me = 2 * x + y
        copies = []
        for w in range(n):
            own, full = refs[w], refs[n + w]
            copies.append((own, full.at[me], (x, y, 1 - c)))
            for px, py in chips:
                copies += [(own.at[c], full.at[me, c], (px, py, c)), (own.at[c], full.at[me, c], (px, py, 1 - c))]
        return copies
    return plan


def _plan_gather_pass(n):
    def plan(refs):
        x, y, c, chips = _place()
        copies = []
        for w in range(n):
            for px, py in chips:
                landed = refs[w].at[2 * px + py, c]
                copies.append((landed, landed, (x, y, 1 - c)))
        return copies
    return plan


def _plan_exchange(n, with_small):
    def plan(refs):
        x, y, c, chips = _place()
        me = 2 * x + y
        n_in = n + int(with_small)
        copies = []
        for w in range(n):
            copies += [(refs[w].at[2 * px + py], refs[n_in + w].at[me], (px, py, c)) for px, py in chips]
        if with_small:
            small, slots = refs[n], refs[n_in + n]
            copies.append((small, slots.at[me], (x, y, 1 - c)))
            copies += [(small, slots.at[me], (px, py, c)) for px, py in chips]
        return copies
    return plan


def _swap_halves(name, grads, wholes, after=None):
    n = len(grads)
    arrays = list(grads) + list(wholes)
    n_all = len(arrays)
    n_in = n_all + (after is not None)

    def body(*refs):
        ins, outs = refs[:n_all], refs[n_in:n_in + n_all]
        send_sem, recv_sem = refs[n_in + n_all:]
        x, y, c, _ = _place()
        sibling = (x, y, 1 - c)
        copies = []
        for w in range(n_all):
            src = ins[w].at[1 - c] if w < n else ins[w]
            cp = _remote(src, outs[w], send_sem.at[w], recv_sem.at[w], sibling)
            cp.start()
            copies.append(cp)
        for cp in copies:
            cp.wait()

    return pl.pallas_call(
        body, name=name,
        in_specs=[ANY] * n_in, out_specs=[ANY] * n_all,
        out_shape=[jax.ShapeDtypeStruct(g.shape[1:], g.dtype) for g in grads]
        + [jax.ShapeDtypeStruct(a.shape, a.dtype) for a in wholes],
        scratch_shapes=[pltpu.SemaphoreType.DMA((n_all,))] * 2,
    )(*arrays, *([] if after is None else [after]))


def _all_reduce_tile(part):
    n_dev = 8

    def body(part_ref, sum_ref, slots_ref, send_sem, recv_sem):
        x, y, c, _ = _place()
        me = 4 * x + 2 * y + c
        slots_ref[me] = part_ref[...]
        flips = [(fx, fy, fc) for fx in (0, 1) for fy in (0, 1) for fc in (0, 1)][1:]
        copies = []
        for k, (fx, fy, fc) in enumerate(flips):
            peer = (x + fx - 2 * x * fx, y + fy - 2 * y * fy, c + fc - 2 * c * fc)
            cp = _remote(part_ref, slots_ref.at[me], send_sem.at[k], recv_sem.at[k], peer)
            cp.start()
            copies.append(cp)
        for cp in copies:
            cp.wait()
        acc = slots_ref[0]
        for k in range(1, n_dev):
            acc = acc + slots_ref[k]
        sum_ref[...] = acc

    return pl.pallas_call(
        body, name="all_reduce_tile",
        in_specs=[pl.BlockSpec(memory_space=pltpu.VMEM)],
        out_specs=pl.BlockSpec(memory_space=pltpu.VMEM),
        out_shape=jax.ShapeDtypeStruct(part.shape, F32),
        scratch_shapes=[pltpu.VMEM((n_dev,) + part.shape, F32),
                        pltpu.SemaphoreType.DMA((n_dev - 1,)), pltpu.SemaphoreType.DMA((n_dev - 1,))],
    )(part)


_SMALL = ("norm1_pre_g", "b_in", "pool_scale", "sgu_ln_g", "sgu_ln_b", "w_spatial", "b_spatial",
          "norm1_post_g", "norm2_pre_g", "norm2_post_g")
_MIX = ("w_in", "w_pool", "w_sgu_proj", "w_out")
_FF = ("w_ff1", "w_ff2")
_BIG = _MIX + _FF
_ORDER = ("norm1_pre_g", "w_in", "b_in", "w_pool", "pool_scale", "sgu_ln_g", "sgu_ln_b", "w_spatial", "b_spatial",
          "w_sgu_proj", "w_out", "norm1_post_g", "norm2_pre_g", "w_ff1", "w_ff2", "norm2_post_g")


def _pack(keys, rows, parts):
    flat = jnp.concatenate([parts[k].reshape(-1).astype(F32) for k in keys])
    flat = jnp.pad(flat, (0, rows * LANES - flat.shape[0]))
    return flat.reshape(rows, LANES)


def _pack_small(parts):
    return _pack(_SMALL, SMALL_ROWS, parts)


def _unpack_small(packed, like):
    flat = packed.reshape(-1)
    out, off = {}, 0
    for k in _SMALL:
        size = like[k].size
        out[k] = flat[off:off + size].reshape(like[k].shape)
        off += size
    return out


def _halves(a):
    return a.reshape(2, a.shape[0] // 2, a.shape[1])


def _step(p, m, v, x, target):
    t = x.shape[0]
    tm_mm = min(1024, t)
    tm_mix = min(256, t)
    tm_ffn = min(512, t)
    core = lax.axis_index("c").astype(jnp.int32).reshape(1)
    row = lambda a: a.reshape(1, -1)

    chip = (2 * lax.axis_index("x") + lax.axis_index("y")).astype(jnp.int32).reshape(1)

    local2d = {"w_in": p["w_in"], "w_ff1": p["w_ff1"], "w_ff2": p["w_ff2"], "w_out": p["w_out"],
               "w_pool": p["w_pool"].reshape(N_HEADS * 64, POOL_GROUP),
               "w_sgu_proj": p["w_sgu_proj"].reshape(N_HEADS * 64, HEAD)}
    shard = {k: _halves(local2d[k].astype(BF16)) for k in _BIG}
    landing = lambda keys: [lax.empty((N_SHARD,) + shard[k].shape, BF16) for k in keys]
    w_in4 = _all_gather_weights([shard["w_in"]])[0].reshape(N_SHARD, D, D_IN // N_SHARD)
    mix_keys = _MIX[1:]
    mix_arrays = [shard[k] for k in mix_keys] + landing(mix_keys)
    send0, recv0, mix_arrays, token = _split_start("ag_mix_send", mix_arrays, 7 * len(mix_keys),
                                                   _plan_gather_direct(len(mix_keys)), after=w_in4)
    ff_arrays = [shard[k] for k in _FF] + landing(_FF)
    send1, recv1, ff_arrays, token = _split_start("ag_ff_send", ff_arrays, 4 * len(_FF), _plan_gather_send(len(_FF)),
                                                  after=token)

    pos = jnp.arange(SGU_BLOCK) // 64
    mask = (pos[:, None] >= pos[None, :]).astype(F32)
    ws = (p["w_spatial"] * mask[None]).astype(BF16)
    ws_t = jnp.swapaxes(ws, 1, 2)
    bsp = p["b_spatial"].reshape(N_HEADS, SGU_BLOCK, 1)

    z, xn = _inproj(x, row(p["norm1_pre_g"]), w_in4, row(p["b_in"]), token, tm_mm)
    mix_arrays = _split_wait("ag_mix_land", send0, recv0, mix_arrays, _plan_gather_direct(len(mix_keys)), z)
    full = dict(zip(mix_keys, mix_arrays[len(mix_keys):]))
    w_out = full["w_out"].reshape(D, D)
    regroup = lambda a: a.reshape(N_SHARD, N_HEADS, 64, 256).transpose(1, 0, 2, 3).reshape(N_HEADS, 256, 256)
    w_pool = regroup(full["w_pool"])
    w_proj = regroup(full["w_sgu_proj"])
    pooled, merged, y, h1 = _mix_fwd(z, x, w_pool, row(p["pool_scale"]), row(p["sgu_ln_g"]), row(p["sgu_ln_b"]),
                                     ws, bsp, w_proj, w_out, row(p["norm1_post_g"]), tm_mix)
    ff_arrays = _split_wait("ag_ff_land", send1, recv1, ff_arrays, _plan_gather_send(len(_FF)), pooled)
    send2, recv2, ff_full, token = _split_start("ag_ff_pass", ff_arrays[len(_FF):], 3 * len(_FF),
                                                _plan_gather_pass(len(_FF)))
    ff_full = _split_wait("ag_ff_passed", send2, recv2, ff_full, _plan_gather_pass(len(_FF)), token)
    w_ff1_4 = ff_full[0].reshape(N_SHARD, D, D)
    w_ff2 = ff_full[1].reshape(N_SHARD, D, D)
    hn, f1, f, df2, dout, vec_a = _ffn_fwd(h1, row(p["norm2_pre_g"]), w_ff1_4, w_ff2, row(p["norm2_post_g"]),
                                           target, tm_ffn)

    tk = min(1024, t)
    g_big = {"w_ff2": _dw(f, df2, "dw_ff2", 2, D, tk, N_SHARD, lambda i, j: i)}
    df1, dh1, dy, vec_b = _ffn_bwd(df2, f1, w_ff1_4, w_ff2, h1, dout, y, row(p["norm2_pre_g"]),
                                   row(p["norm1_post_g"]), tm_ffn)
    g_big["w_ff1"] = _dw(hn, df1, "dw_ff1", 2, D, tk, N_SHARD, lambda i, j: j)

    def reduce_start(name, keys, wholes):
        swapped = _swap_halves("rs_swap_" + name, [g_big[k] for k in keys], wholes)
        sums = [_chip_sum(core, g_big[k], r, "chip_sum_" + k) for k, r in zip(keys, swapped)]
        arrays = list(sums)
        if wholes:
            arrays.append(_slot_sum(jnp.stack([wholes[0], swapped[-1]]), "chip_sum_small"))
        arrays += [lax.empty(s.shape, BF16) for s in sums]
        if wholes:
            arrays.append(lax.empty((N_SHARD,) + wholes[0].shape, F32))
        plan = _plan_exchange(len(keys), bool(wholes))
        send, recv, arrays, tok = _split_start("rs_send_" + name, arrays, 3 * len(keys) + 4 * len(wholes), plan)
        return send, recv, arrays, plan, tok

    def reduce_finish(name, keys, state, *after):
        send, recv, arrays, plan, _ = state
        arrays = _split_wait("rs_land_" + name, send, recv, arrays, plan, *after)
        n_in = len(arrays) // 2
        halves = {k: _final_sum(chip, arrays[i], arrays[n_in + i], "final_sum_" + k) for i, k in enumerate(keys)}
        return halves, arrays

    by_half = lambda a, rows: a.reshape(N_SHARD, 2, rows // 2, a.shape[-1]).transpose(1, 0, 2, 3)
    dw_out = _dw(merged, dy, "dw_out", 1, D, tk, 1, lambda i, j: 0).reshape(D, D)
    g_big["w_out"] = by_half(dw_out, D // N_SHARD)
    early = _FF + ("w_out",)
    late = ("w_in", "w_pool", "w_sgu_proj")
    early_state = reduce_start("early", early, [])
    dz, dwp, dwpr, dws, dbsp, vec_c, dbin = _mix_bwd(z, pooled, dy, w_pool, row(p["pool_scale"]), row(p["sgu_ln_g"]),
                                                     row(p["sgu_ln_b"]), ws, ws_t, bsp, w_proj, w_out, mask, tm_mix,
                                                     early_state[4])
    g_big["w_in"] = _dw(xn, dz, "dw_in", 2, D_IN // N_SHARD, tk, N_SHARD, lambda i, j: j)
    shard_major = lambda a: a.reshape(N_HEADS, N_SHARD, 64, 256).transpose(1, 0, 2, 3).reshape(N_SHARD * 256, 256)
    g_big["w_pool"] = by_half(shard_major(dwp).astype(BF16), 256)
    g_big["w_sgu_proj"] = by_half(shard_major(dwpr).astype(BF16), 256)
    halves, _ = reduce_finish("early", early, early_state, g_big["w_in"])

    small_part = _pack(_SMALL[1:], SMALL_ROWS - 8, {
        "b_in": dbin[0], "pool_scale": vec_c[0], "sgu_ln_g": vec_c[1], "sgu_ln_b": vec_c[2],
        "w_spatial": dws, "b_spatial": dbsp[:, ::LANES].T, "norm1_post_g": vec_b[1], "norm2_pre_g": vec_b[0],
        "norm2_post_g": vec_a[0]})
    late_state = reduce_start("late", late, [small_part])
    grad_x, vec_d = _dx(dz, w_in4, x, dh1, row(p["norm1_pre_g"]), late_state[4], tm_mm)
    tiles = _all_reduce_tile(jnp.concatenate([vec_d[0].reshape(8, LANES),
                                              jnp.broadcast_to(vec_a[1:2, :LANES], (8, LANES))]))
    g1_grad, loss = tiles[:8], tiles[8, 0]

    grads, delta, new_m, new_v = {}, {}, {}, {}

    def finish(name, keys, after):
        shared = _swap_halves("rs_share_" + name, [], [halves[k] for k in keys], after)
        for k, sib in zip(keys, shared):
            as2d = lambda a: a.reshape(local2d[k].shape)
            outs = _adamw_halves(core, halves[k], sib, local2d[k], as2d(m[k]), as2d(v[k]), "adamw_" + k)
            grads[k], delta[k], new_m[k], new_v[k] = (a.reshape(p[k].shape) for a in outs)

    finish("early", early, late_state[4])
    late_halves, late_arrays = reduce_finish("late", late, late_state, grad_x, *[new_v[k] for k in early])
    halves.update(late_halves)
    finish("late", late, None)
    small_grad = jnp.concatenate([g1_grad, _slot_sum(late_arrays[-1], "final_sum_small")])
    d_s, m_s, v_s = _adamw(_pack_small(p), small_grad, _pack_small(m), _pack_small(v), "adamw_small")
    grads.update(_unpack_small(small_grad, p))
    delta.update(_unpack_small(d_s, p))
    new_m.update(_unpack_small(m_s, p))
    new_v.update(_unpack_small(v_s, p))

    return (loss, grad_x, *[grads[k] for k in _ORDER], *[delta[k] for k in _ORDER],
            *[new_m[k] for k in _ORDER], *[new_v[k] for k in _ORDER])


def kernel(x, norm1_pre_g, w_in, b_in, w_pool, pool_scale, sgu_ln_g, sgu_ln_b, w_spatial, b_spatial, w_sgu_proj, w_out, norm1_post_g, norm2_pre_g, w_ff1, w_ff2, norm2_post_g, loss_target, m_norm1_pre_g, m_w_in, m_b_in, m_w_pool, m_pool_scale, m_sgu_ln_g, m_sgu_ln_b, m_w_spatial, m_b_spatial, m_w_sgu_proj, m_w_out, m_norm1_post_g, m_norm2_pre_g, m_w_ff1, m_w_ff2, m_norm2_post_g, v_norm1_pre_g, v_w_in, v_b_in, v_w_pool, v_pool_scale, v_sgu_ln_g, v_sgu_ln_b, v_w_spatial, v_b_spatial, v_w_sgu_proj, v_w_out, v_norm1_post_g, v_norm2_pre_g, v_w_ff1, v_w_ff2, v_norm2_post_g):
    p = dict(zip(_ORDER, (norm1_pre_g, w_in, b_in, w_pool, pool_scale, sgu_ln_g, sgu_ln_b, w_spatial, b_spatial,
                          w_sgu_proj, w_out, norm1_post_g, norm2_pre_g, w_ff1, w_ff2, norm2_post_g)))
    m = dict(zip(_ORDER, (m_norm1_pre_g, m_w_in, m_b_in, m_w_pool, m_pool_scale, m_sgu_ln_g, m_sgu_ln_b, m_w_spatial,
                          m_b_spatial, m_w_sgu_proj, m_w_out, m_norm1_post_g, m_norm2_pre_g, m_w_ff1, m_w_ff2,
                          m_norm2_post_g)))
    v = dict(zip(_ORDER, (v_norm1_pre_g, v_w_in, v_b_in, v_w_pool, v_pool_scale, v_sgu_ln_g, v_sgu_ln_b, v_w_spatial,
                          v_b_spatial, v_w_sgu_proj, v_w_out, v_norm1_post_g, v_norm2_pre_g, v_w_ff1, v_w_ff2,
                          v_norm2_post_g)))
    outs = _step(p, m, v, x[0], loss_target[0])
    return (outs[0], outs[1][None], *outs[2:])
```

```python
import math

import jax
import jax.numpy as jnp
from jax import lax
from jax.experimental import pallas as pl
from jax.experimental.pallas import tpu as pltpu

F32 = jnp.float32
BF16 = jnp.bfloat16

D = 1024
D_IN = 5 * D
D_FF = 4 * D
N_SHARD = 4
N_CHIP_PEERS = 3
POOL_WINDOWS = (2, 4, 8, 16)
POOL_GROUP = 256
POOL_HALO = 16
SGU_BLOCK = 128
N_HEADS = 4
HEAD = 256
EPS = 1e-6

ADAM_LR = 0.001
ADAM_B1 = 0.9
ADAM_B2 = 0.999
ADAM_EPS = 1e-08
ADAM_WD = 0.01
ADAM_STEP = 10

V7X_VMEM_BYTES = 64 * 1024 * 1024
VMEM_LIMIT = V7X_VMEM_BYTES - 8 * 1024 * 1024
SMALL_ROWS = 616
LANES = 128

MESH = pl.DeviceIdType.MESH
ANY = pl.BlockSpec(memory_space=pl.ANY)
HBM = pl.BlockSpec(memory_space=pltpu.HBM)
SEM = pl.BlockSpec(memory_space=pltpu.SEMAPHORE)
EFFECT = pltpu.SideEffectType.DATAFLOW_SIDE_EFFECTING

NT_DIMS = (((1,), (1,)), ((), ()))
TN_DIMS = (((0,), (0,)), ((), ()))


def _params(n_axes):
    return pltpu.CompilerParams(dimension_semantics=("arbitrary",) * n_axes, vmem_limit_bytes=VMEM_LIMIT)


def _resident(shape, n_axes):
    zeros = (0,) * len(shape)
    index = (lambda i: zeros) if n_axes == 1 else (lambda i, j: zeros)
    return pl.BlockSpec(shape, index, pipeline_mode=pl.Buffered(1))


def _dot(a, b):
    return jnp.dot(a, b, preferred_element_type=F32)


def _dot_nt(a, b):
    return lax.dot_general(a, b, NT_DIMS, preferred_element_type=F32)


def _dot_tn(a, b):
    return lax.dot_general(a, b, TN_DIMS, preferred_element_type=F32)


def _mean(a):
    return jnp.mean(a, axis=-1, keepdims=True)


def _colsum(a):
    return jnp.sum(a, axis=0, keepdims=True)


_GELU_C0 = math.sqrt(2.0 / math.pi)
_GELU_C1 = 0.044715


def _gelu(a):
    t = jnp.tanh(a * (_GELU_C0 + (_GELU_C0 * _GELU_C1) * (a * a)))
    return a * (0.5 * t + 0.5)


def _gelu_and_grad(a):
    a2 = a * a
    t = jnp.tanh(a * (_GELU_C0 + (_GELU_C0 * _GELU_C1) * a2))
    cdf = 0.5 * t + 0.5
    grad = cdf + (0.5 * a) * (1.0 - t * t) * (_GELU_C0 + (3.0 * _GELU_C0 * _GELU_C1) * a2)
    return a * cdf, grad


def _sigmoid(a):
    return 0.5 * jnp.tanh(0.5 * a) + 0.5


def _window_sum(a, w, ahead):
    n = a.shape[0]
    step = 1
    while step < w:
        a = a + pltpu.roll(a, n - step if ahead else step, 0)
        step *= 2
    return a


def _rms_bwd(dn, n, r):
    return r * (dn - n * _mean(dn * n))


def _inproj(x, g1, w_in4, b_in, dep, tm):
    t = x.shape[0]
    cw = D_IN // N_SHARD

    def body(x_ref, g_ref, w_ref, b_ref, dep_ref, z_ref, xn_ref):
        @pl.when(pl.program_id(1) == 0)
        def _():
            xv = x_ref[...]
            r = lax.rsqrt(_mean(xv * xv) + EPS)
            xn_ref[...] = (xv * r * g_ref[...]).astype(BF16)

        z_ref[...] = _dot(xn_ref[...], w_ref[pl.program_id(1)]) + b_ref[...]

    return pl.pallas_call(
        body, name="inproj",
        grid=(t // tm, N_SHARD),
        in_specs=[pl.BlockSpec((tm, D), lambda i, s: (i, 0)),
                  pl.BlockSpec((1, D), lambda i, s: (0, 0)),
                  _resident((N_SHARD, D, cw), 2),
                  pl.BlockSpec((1, cw), lambda i, s: (0, s)),
                  pl.BlockSpec(dep.shape, lambda i, s: (0, 0))],
        out_specs=[pl.BlockSpec((tm, cw), lambda i, s: (i, s)),
                   pl.BlockSpec((tm, D), lambda i, s: (i, 0))],
        out_shape=[jax.ShapeDtypeStruct((t, D_IN), F32), jax.ShapeDtypeStruct((t, D), BF16)],
        compiler_params=_params(2),
    )(x, g1, w_in4, b_in, dep)


def _sgu_forward(z_ref, lng_ref, lnb_ref, ws_ref, bsp_ref, sv_ref, tm, with_grad):
    if with_grad:
        u, du_dz = _gelu_and_grad(z_ref[:, D:2 * D])
        gv, dgv_dz = _gelu_and_grad(z_ref[:, 2 * D:3 * D])
    else:
        u = _gelu(z_ref[:, D:2 * D])
        gv = _gelu(z_ref[:, 2 * D:3 * D])
        du_dz = dgv_dz = None
    xc = gv - _mean(gv)
    rstd = lax.rsqrt(_mean(xc * xc) + EPS)
    vhat = xc * rstd
    vb = (vhat * lng_ref[...] + lnb_ref[...]).astype(BF16)
    for b in range(tm // SGU_BLOCK):
        rows = slice(b * SGU_BLOCK, (b + 1) * SGU_BLOCK)
        for h in range(N_HEADS):
            cols = slice(h * HEAD, (h + 1) * HEAD)
            sv_ref[rows, cols] = _dot(ws_ref[h], vb[rows, cols]) + bsp_ref[h]
    return u, du_dz, dgv_dz, vhat, rstd, vb


def _mix_fwd(z, x, w_pool, pool_scale, ln_g, ln_b, ws, bsp, w_proj, w_out, g1post, tm):
    t = x.shape[0]

    def body(z_ref, x_ref, wp_ref, psc_ref, lng_ref, lnb_ref, ws_ref, bsp_ref, wpr_ref, wo_ref, g1p_ref,
             pooled_ref, merged_ref, y_ref, h1_ref, win_ref, sv_ref):
        i = pl.program_id(0)

        @pl.when(i == 0)
        def _():
            win_ref[0:POOL_HALO, :] = jnp.zeros((POOL_HALO, D), F32)

        win_ref[POOL_HALO:POOL_HALO + tm, :] = z_ref[:, 0:D]
        pos1 = lax.broadcasted_iota(jnp.int32, (tm, 1), 0) + (i * tm + 1)
        a_parts = []
        for g, w in enumerate(POOL_WINDOWS):
            cols = slice(g * POOL_GROUP, (g + 1) * POOL_GROUP)
            ts = _window_sum(win_ref[:, cols], w, False)[POOL_HALO:POOL_HALO + tm, :]
            inv_cnt = 1.0 / jnp.minimum(pos1, w).astype(F32)
            pooled = (ts * inv_cnt - z_ref[:, cols]).astype(BF16)
            pooled_ref[:, cols] = pooled
            a_parts.append(_dot(pooled, wp_ref[g]) * psc_ref[:, cols])
        win_ref[0:POOL_HALO, :] = win_ref[tm:tm + POOL_HALO, :]

        u, _, _, _, _, _ = _sgu_forward(z_ref, lng_ref, lnb_ref, ws_ref, bsp_ref, sv_ref, tm, False)
        gated = (u * sv_ref[...]).astype(BF16)
        b_parts = [_dot(gated[:, h * HEAD:(h + 1) * HEAD], wpr_ref[h]) for h in range(N_HEADS)]

        a = jnp.concatenate(a_parts, axis=1)
        bbr = jnp.concatenate(b_parts, axis=1)
        merged = (_sigmoid(z_ref[:, 3 * D:4 * D]) * a + _sigmoid(z_ref[:, 4 * D:5 * D]) * bbr).astype(BF16)
        merged_ref[...] = merged
        y = _dot(merged, wo_ref[...])
        y_ref[...] = y
        r = lax.rsqrt(_mean(y * y) + EPS)
        h1_ref[...] = x_ref[...] + (y * r) * g1p_ref[...]

    row = lambda i: (i, 0)
    fixed2 = lambda i: (0, 0)
    fixed3 = lambda i: (0, 0, 0)
    vec = pl.BlockSpec((1, D), fixed2)
    return pl.pallas_call(
        body, name="mix_fwd",
        grid=(t // tm,),
        in_specs=[pl.BlockSpec((tm, D_IN), row), pl.BlockSpec((tm, D), row),
                  pl.BlockSpec((N_HEADS, POOL_GROUP, POOL_GROUP), fixed3), vec, vec, vec,
                  pl.BlockSpec((N_HEADS, SGU_BLOCK, SGU_BLOCK), fixed3),
                  pl.BlockSpec((N_HEADS, SGU_BLOCK, 1), fixed3),
                  pl.BlockSpec((N_HEADS, HEAD, HEAD), fixed3),
                  pl.BlockSpec((D, D), fixed2), vec],
        out_specs=[pl.BlockSpec((tm, D), row)] * 4,
        out_shape=[jax.ShapeDtypeStruct((t, D), BF16), jax.ShapeDtypeStruct((t, D), BF16),
                   jax.ShapeDtypeStruct((t, D), F32), jax.ShapeDtypeStruct((t, D), F32)],
        scratch_shapes=[pltpu.VMEM((tm + POOL_HALO, D), F32), pltpu.VMEM((tm, D), F32)],
        compiler_params=_params(1),
    )(z, x, w_pool, pool_scale, ln_g, ln_b, ws, bsp, w_proj, w_out, g1post)


def _ffn_fwd(h1, g2pre, w1, w2, g2post, target, tm):
    t = h1.shape[0]
    n_j = D_FF // D

    def body(h1_ref, g2_ref, w1_ref, w2_ref, g2p_ref, tgt_ref,
             hn_ref, f1_ref, f_ref, df2_ref, dout_ref, vec_ref, acc_ref):
        i, j = pl.program_id(0), pl.program_id(1)

        @pl.when((i == 0) & (j == 0))
        def _():
            vec_ref[...] = jnp.zeros_like(vec_ref)

        @pl.when(j == 0)
        def _():
            h = h1_ref[...]
            r = lax.rsqrt(_mean(h * h) + EPS)
            hn_ref[...] = (h * r * g2_ref[...]).astype(BF16)
            acc_ref[...] = jnp.zeros_like(acc_ref)

        f1 = _dot(hn_ref[...], w1_ref[j])
        f1_ref[...] = f1
        rl = jnp.maximum(f1, 0.0)
        f = (rl * rl).astype(BF16)
        f_ref[...] = f
        acc_ref[...] += _dot(f, w2_ref[j])

        @pl.when(j == n_j - 1)
        def _():
            f2 = acc_ref[...]
            r = lax.rsqrt(_mean(f2 * f2) + EPS)
            n2 = f2 * r
            err = h1_ref[...] + n2 * g2p_ref[...] - tgt_ref[...]
            loss = 0.5 * jnp.sum(_mean(err * err))
            dout = err * (1.0 / D)
            dout_ref[...] = dout
            vec_ref[0:1, :] += _colsum(dout * n2)
            vec_ref[1:2, :] += jnp.full((1, D), loss, F32)
            df2_ref[...] = _rms_bwd(dout * g2p_ref[...], n2, r).astype(BF16)

    row = lambda i, j: (i, 0)
    fixed = lambda i, j: (0, 0)
    vec = pl.BlockSpec((1, D), fixed)
    return pl.pallas_call(
        body, name="ffn_fwd",
        grid=(t // tm, n_j),
        in_specs=[pl.BlockSpec((tm, D), row), vec,
                  _resident((n_j, D, D), 2), _resident((n_j, D, D), 2), vec,
                  pl.BlockSpec((tm, D), row)],
        out_specs=[pl.BlockSpec((tm, D), row),
                   pl.BlockSpec((tm, D), lambda i, j: (i, j)),
                   pl.BlockSpec((tm, D), lambda i, j: (i, j)),
                   pl.BlockSpec((tm, D), row), pl.BlockSpec((tm, D), row),
                   pl.BlockSpec((8, D), fixed)],
        out_shape=[jax.ShapeDtypeStruct((t, D), BF16), jax.ShapeDtypeStruct((t, D_FF), F32),
                   jax.ShapeDtypeStruct((t, D_FF), BF16), jax.ShapeDtypeStruct((t, D), BF16),
                   jax.ShapeDtypeStruct((t, D), F32), jax.ShapeDtypeStruct((8, D), F32)],
        scratch_shapes=[pltpu.VMEM((tm, D), F32)],
        compiler_params=_params(2),
    )(h1, g2pre, w1, w2, g2post, target)


def _ffn_bwd(df2, f1, w1, w2, h1, dout, y, g2pre, g1post, tm):
    t = h1.shape[0]
    n_j = D_FF // D

    def body(df2_ref, f1_ref, w1_ref, w2_ref, h1_ref, dout_ref, y_ref, g2_ref, g1p_ref,
             df1_ref, dh1_ref, dy_ref, vec_ref, acc_ref):
        i, j = pl.program_id(0), pl.program_id(1)

        @pl.when((i == 0) & (j == 0))
        def _():
            vec_ref[...] = jnp.zeros_like(vec_ref)

        @pl.when(j == 0)
        def _():
            acc_ref[...] = jnp.zeros_like(acc_ref)

        df = _dot_nt(df2_ref[...], w2_ref[j])
        df1 = (df * (2.0 * jnp.maximum(f1_ref[...], 0.0))).astype(BF16)
        df1_ref[...] = df1
        acc_ref[...] += _dot_nt(df1, w1_ref[j])

        @pl.when(j == n_j - 1)
        def _():
            dhn = acc_ref[...]
            h = h1_ref[...]
            r2 = lax.rsqrt(_mean(h * h) + EPS)
            nh = h * r2
            vec_ref[0:1, :] += _colsum(dhn * nh)
            dh1 = dout_ref[...] + _rms_bwd(dhn * g2_ref[...], nh, r2)
            dh1_ref[...] = dh1
            yv = y_ref[...]
            r1 = lax.rsqrt(_mean(yv * yv) + EPS)
            ny = yv * r1
            vec_ref[1:2, :] += _colsum(dh1 * ny)
            dy_ref[...] = _rms_bwd(dh1 * g1p_ref[...], ny, r1).astype(BF16)

    row = lambda i, j: (i, 0)
    fixed = lambda i, j: (0, 0)
    vec = pl.BlockSpec((1, D), fixed)
    return pl.pallas_call(
        body, name="ffn_bwd",
        grid=(t // tm, n_j),
        in_specs=[pl.BlockSpec((tm, D), row),
                  pl.BlockSpec((tm, D), lambda i, j: (i, j)),
                  _resident((n_j, D, D), 2), _resident((n_j, D, D), 2),
                  pl.BlockSpec((tm, D), row), pl.BlockSpec((tm, D), row), pl.BlockSpec((tm, D), row), vec, vec],
        out_specs=[pl.BlockSpec((tm, D), lambda i, j: (i, j)),
                   pl.BlockSpec((tm, D), row), pl.BlockSpec((tm, D), row),
                   pl.BlockSpec((8, D), fixed)],
        out_shape=[jax.ShapeDtypeStruct((t, D_FF), BF16), jax.ShapeDtypeStruct((t, D), F32),
                   jax.ShapeDtypeStruct((t, D), BF16), jax.ShapeDtypeStruct((8, D), F32)],
        scratch_shapes=[pltpu.VMEM((tm, D), F32)],
        compiler_params=_params(2),
    )(df2, f1, w1, w2, h1, dout, y, g2pre, g1post)


def _mix_bwd(z, pooled, dy, w_pool, pool_scale, ln_g, ln_b, ws, ws_t, bsp, w_proj, w_out, mask, tm, dep):
    t = z.shape[0]
    n_t = t // tm

    def body(z_ref, pooled_ref, dy_ref, wp_ref, psc_ref, lng_ref, lnb_ref, ws_ref, wst_ref, bsp_ref, wpr_ref,
             wo_ref, mask_ref, dep_ref,
             dz_ref, dwp_ref, dwpr_ref, dws_ref, dbsp_ref, vec_ref, dbin_ref,
             win_ref, sv_ref, dsv_ref, dv_ref, dsvsum_ref):
        i = pl.program_id(0)

        @pl.when(i == 0)
        def _():
            win_ref[tm:tm + POOL_HALO, :] = jnp.zeros((POOL_HALO, D), F32)
            dwp_ref[...] = jnp.zeros_like(dwp_ref)
            dwpr_ref[...] = jnp.zeros_like(dwpr_ref)
            dws_ref[...] = jnp.zeros_like(dws_ref)
            vec_ref[...] = jnp.zeros_like(vec_ref)
            dbin_ref[...] = jnp.zeros_like(dbin_ref)
            dsvsum_ref[...] = jnp.zeros_like(dsvsum_ref)

        dmerged = _dot_nt(dy_ref[...], wo_ref[...])

        u, du_dz, dgv_dz, vhat, rstd, vb = _sgu_forward(z_ref, lng_ref, lnb_ref, ws_ref, bsp_ref, sv_ref, tm, True)
        sv = sv_ref[...]
        gated = (u * sv).astype(BF16)

        sa = _sigmoid(z_ref[:, 3 * D:4 * D])
        sb = _sigmoid(z_ref[:, 4 * D:5 * D])
        da = dmerged * sa
        dbbr = (dmerged * sb).astype(BF16)

        tile = n_t - 1 - i
        pos1 = lax.broadcasted_iota(jnp.int32, (tm, 1), 0) + (tile * tm + 1)
        dzga_parts, dpooled_parts = [], []
        for g, w in enumerate(POOL_WINDOWS):
            cols = slice(g * POOL_GROUP, (g + 1) * POOL_GROUP)
            pg = pooled_ref[:, cols]
            a_pre = _dot(pg, wp_ref[g])
            da_g = da[:, cols]
            psc = psc_ref[:, cols]
            vec_ref[0:1, cols] += _colsum(da_g * a_pre)
            sa_g = sa[:, cols]
            dzga_parts.append(dmerged[:, cols] * (a_pre * psc) * (sa_g * (1.0 - sa_g)))
            da_pre = (da_g * psc).astype(BF16)
            dwp_ref[g] += _dot_tn(pg, da_pre)
            dpooled = _dot_nt(da_pre, wp_ref[g])
            dpooled_parts.append(dpooled)
            inv_cnt = 1.0 / jnp.minimum(pos1, w).astype(F32)
            win_ref[0:tm, cols] = dpooled * inv_cnt
        dzp_parts = []
        for g, w in enumerate(POOL_WINDOWS):
            cols = slice(g * POOL_GROUP, (g + 1) * POOL_GROUP)
            acc = _window_sum(win_ref[:, cols], w, True)[0:tm, :]
            dzp_parts.append(acc - dpooled_parts[g])
        win_ref[tm:tm + POOL_HALO, :] = win_ref[0:POOL_HALO, :]

        dzgb_parts = []
        for h in range(N_HEADS):
            cols = slice(h * HEAD, (h + 1) * HEAD)
            g_h = gated[:, cols]
            db_h = dbbr[:, cols]
            bbr_h = _dot(g_h, wpr_ref[h])
            sb_h = sb[:, cols]
            dzgb_parts.append(dmerged[:, cols] * bbr_h * (sb_h * (1.0 - sb_h)))
            dwpr_ref[h] += _dot_tn(g_h, db_h)
            dgated_h = _dot_nt(db_h, wpr_ref[h])
            dsv_ref[:, cols] = dgated_h * u[:, cols]
            sv_ref[:, cols] = dgated_h * sv[:, cols]
        dzu = sv_ref[...] * du_dz
        dsv = dsv_ref[...]
        dsvb = dsv.astype(BF16)
        blk_sum = dsv[0:SGU_BLOCK, :]
        for b in range(1, tm // SGU_BLOCK):
            blk_sum = blk_sum + dsv[b * SGU_BLOCK:(b + 1) * SGU_BLOCK, :]
        dsvsum_ref[...] += blk_sum
        for b in range(tm // SGU_BLOCK):
            rows = slice(b * SGU_BLOCK, (b + 1) * SGU_BLOCK)
            for h in range(N_HEADS):
                cols = slice(h * HEAD, (h + 1) * HEAD)
                dws_ref[h] += _dot_nt(dsvb[rows, cols], vb[rows, cols])
                dv_ref[rows, cols] = _dot(wst_ref[h], dsvb[rows, cols])
        dv = dv_ref[...]
        vec_ref[1:2, :] += _colsum(dv * vhat)
        vec_ref[2:3, :] += _colsum(dv)
        dvhat = dv * lng_ref[...]
        dgv = rstd * (dvhat - _mean(dvhat) - vhat * _mean(dvhat * vhat))
        dzv = dgv * dgv_dz

        parts = [jnp.concatenate(dzp_parts, axis=1), dzu, dzv,
                 jnp.concatenate(dzga_parts, axis=1), jnp.concatenate(dzgb_parts, axis=1)]
        for k, part in enumerate(parts):
            dz_ref[:, k * D:(k + 1) * D] = part.astype(BF16)
            dbin_ref[0:1, k * D:(k + 1) * D] += _colsum(part)

        @pl.when(i == n_t - 1)
        def _():
            for h in range(N_HEADS):
                dws_ref[h] = jnp.where(mask_ref[...] > 0.0, dws_ref[h], 0.0)
                tot = jnp.sum(dsvsum_ref[:, h * HEAD:(h + 1) * HEAD], axis=1, keepdims=True)
                dbsp_ref[:, h * LANES:(h + 1) * LANES] = jnp.broadcast_to(tot, (SGU_BLOCK, LANES))

    row = lambda i: (n_t - 1 - i, 0)
    fixed2 = lambda i: (0, 0)
    fixed3 = lambda i: (0, 0, 0)
    vec = pl.BlockSpec((1, D), fixed2)
    sq = pl.BlockSpec((N_HEADS, SGU_BLOCK, SGU_BLOCK), fixed3)
    grp = pl.BlockSpec((N_HEADS, HEAD, HEAD), fixed3)
    return pl.pallas_call(
        body, name="mix_bwd",
        grid=(n_t,),
        in_specs=[pl.BlockSpec((tm, D_IN), row), pl.BlockSpec((tm, D), row), pl.BlockSpec((tm, D), row),
                  grp, vec, vec, vec, sq, sq, pl.BlockSpec((N_HEADS, SGU_BLOCK, 1), fixed3), grp,
                  pl.BlockSpec((D, D), fixed2), pl.BlockSpec((SGU_BLOCK, SGU_BLOCK), fixed2),
                  pl.BlockSpec(dep.shape, fixed2)],
        out_specs=[pl.BlockSpec((tm, D_IN), row), grp, grp, sq,
                   pl.BlockSpec((SGU_BLOCK, N_HEADS * LANES), fixed2),
                   pl.BlockSpec((8, D), fixed2), pl.BlockSpec((8, D_IN), fixed2)],
        out_shape=[jax.ShapeDtypeStruct((t, D_IN), BF16),
                   jax.ShapeDtypeStruct((N_HEADS, HEAD, HEAD), F32), jax.ShapeDtypeStruct((N_HEADS, HEAD, HEAD), F32),
                   jax.ShapeDtypeStruct((N_HEADS, SGU_BLOCK, SGU_BLOCK), F32),
                   jax.ShapeDtypeStruct((SGU_BLOCK, N_HEADS * LANES), F32),
                   jax.ShapeDtypeStruct((8, D), F32), jax.ShapeDtypeStruct((8, D_IN), F32)],
        scratch_shapes=[pltpu.VMEM((tm + POOL_HALO, D), F32), pltpu.VMEM((tm, D), F32), pltpu.VMEM((tm, D), F32),
                        pltpu.VMEM((tm, D), F32), pltpu.VMEM((SGU_BLOCK, D), F32)],
        compiler_params=_params(1),
    )(z, pooled, dy, w_pool, pool_scale, ln_g, ln_b, ws, ws_t, bsp, w_proj, w_out, mask, dep)


def _dx(dz, w_in4, x, dh1, g1, dep, tm):
    t = x.shape[0]
    cw = D_IN // N_SHARD

    def body(dz_ref, w_ref, x_ref, dh1_ref, g_ref, dep_ref, gx_ref, vec_ref, acc_ref):
        i, s = pl.program_id(0), pl.program_id(1)

        @pl.when((i == 0) & (s == 0))
        def _():
            vec_ref[...] = jnp.zeros_like(vec_ref)

        @pl.when(s == 0)
        def _():
            acc_ref[...] = jnp.zeros_like(acc_ref)

        acc_ref[...] += _dot_nt(dz_ref[...], w_ref[s])

        @pl.when(s == N_SHARD - 1)
        def _():
            dxn = acc_ref[...]
            xv = x_ref[...]
            r = lax.rsqrt(_mean(xv * xv) + EPS)
            xh = xv * r
            vec_ref[0:1, :] += _colsum(dxn * xh)
            gx_ref[...] = dh1_ref[...] + _rms_bwd(dxn * g_ref[...], xh, r)

    row = lambda i, s: (i, 0)
    fixed = lambda i, s: (0, 0)
    return pl.pallas_call(
        body, name="dx",
        grid=(t // tm, N_SHARD),
        in_specs=[pl.BlockSpec((tm, cw), lambda i, s: (i, s)),
                  _resident((N_SHARD, D, cw), 2),
                  pl.BlockSpec((tm, D), row), pl.BlockSpec((tm, D), row), pl.BlockSpec((1, D), fixed),
                  pl.BlockSpec(dep.shape, fixed)],
        out_specs=[pl.BlockSpec((tm, D), row), pl.BlockSpec((8, D), fixed)],
        out_shape=[jax.ShapeDtypeStruct((t, D), F32), jax.ShapeDtypeStruct((8, D), F32)],
        scratch_shapes=[pltpu.VMEM((tm, D), F32)],
        compiler_params=_params(2),
    )(dz, w_in4, x, dh1, g1, dep)


def _dw(a, b, name, n_split, tn, tk, n_out, out_block, dep=None):
    t, m = a.shape
    n = b.shape[1]
    tm = min(m, D)
    rows = tm // n_split
    n_k = t // tk

    def body(a_ref, b_ref, *rest):
        o_ref, acc_ref = rest[-2:]
        k = pl.program_id(2)

        @pl.when(k == 0)
        def _():
            acc_ref[...] = jnp.zeros_like(acc_ref)

        acc_ref[...] += _dot_tn(a_ref[...], b_ref[...])

        @pl.when(k == n_k - 1)
        def _():
            for h in range(n_split):
                o_ref[h, 0] = acc_ref[h * rows:(h + 1) * rows, :].astype(BF16)

    return pl.pallas_call(
        body, name=name,
        grid=(m // tm, n // tn, n_k),
        in_specs=[pl.BlockSpec((tk, tm), lambda i, j, k: (k, i)),
                  pl.BlockSpec((tk, tn), lambda i, j, k: (k, j))]
        + ([] if dep is None else [pl.BlockSpec(dep.shape, lambda i, j, k: (0, 0))]),
        out_specs=pl.BlockSpec((n_split, 1, rows, tn), lambda i, j, k: (0, out_block(i, j), 0, 0)),
        out_shape=jax.ShapeDtypeStruct((n_split, n_out, rows, tn), BF16),
        scratch_shapes=[pltpu.VMEM((tm, tn), F32)],
        compiler_params=_params(3),
    )(*((a, b) if dep is None else (a, b, dep)))


def _row_block(rows, cols, itemsize, n_bufs):
    budget = VMEM_LIMIT // 4
    rb = rows
    while rb % 16 == 0 and rb * cols * itemsize * n_bufs * 2 > budget:
        rb //= 2
    return rb


ROW_STEPS = 4


def _chip_sums(core, grads, recvs, name):
    n = len(grads)

    def body(c_ref, *refs):
        for w in range(n):
            refs[2 * n + w][...] = (refs[w][0].astype(F32) + refs[n + w][...].astype(F32)).astype(BF16)

    def blocks(g):
        _, _, rh, cols = g.shape
        rb = rh // ROW_STEPS
        return (pl.BlockSpec((1, 1, rb, cols), lambda s, r, c: (c[0], s, r, 0)),
                pl.BlockSpec((1, rb, cols), lambda s, r, c: (s, r, 0)))

    specs = [blocks(g) for g in grads]
    return pl.pallas_call(
        body, name=name,
        grid_spec=pltpu.PrefetchScalarGridSpec(
            num_scalar_prefetch=1, grid=(N_SHARD, ROW_STEPS),
            in_specs=[s[0] for s in specs] + [s[1] for s in specs],
            out_specs=[s[1] for s in specs]),
        out_shape=[jax.ShapeDtypeStruct(g.shape[1:], BF16) for g in grads],
        compiler_params=_params(2),
    )(core, *grads, *recvs)


def _slot_sum(slots, name):
    n_s, rows, cols = slots.shape
    rb = _row_block(rows, cols, 4, n_s + 1)

    def body(s_ref, o_ref):
        acc = s_ref[0].astype(F32)
        for k in range(1, n_s):
            acc = acc + s_ref[k].astype(F32)
        o_ref[...] = acc

    return pl.pallas_call(
        body, name=name,
        grid=(rows // rb,),
        in_specs=[pl.BlockSpec((n_s, rb, cols), lambda r: (0, r, 0))],
        out_specs=pl.BlockSpec((rb, cols), lambda r: (r, 0)),
        out_shape=jax.ShapeDtypeStruct((rows, cols), F32),
        compiler_params=_params(1),
    )(slots)


def _adamw_update(g, w, m, v):
    m_new = ADAM_B1 * m + (1.0 - ADAM_B1) * g
    v_new = ADAM_B2 * v + (1.0 - ADAM_B2) * (g * g)
    m_hat = m_new / (1.0 - ADAM_B1 ** ADAM_STEP)
    v_hat = v_new / (1.0 - ADAM_B2 ** ADAM_STEP)
    return -ADAM_LR * (m_hat / (jnp.sqrt(v_hat) + ADAM_EPS) + ADAM_WD * w), m_new, v_new


def _adamw_small(grad_parts, ws, ms, vs):
    n, n_g = len(ws), len(grad_parts)

    def body(*refs):
        g_refs, ins, outs = refs[:n_g], refs[n_g:n_g + 3 * n], refs[n_g + 3 * n:]
        part, off = 0, 0
        for k in range(n):
            rows = ws[k].shape[0]
            if off == grad_parts[part].shape[0]:
                part, off = part + 1, 0
            g = g_refs[part][off:off + rows, :]
            off += rows
            delta, m_new, v_new = _adamw_update(g, ins[k][...], ins[n + k][...], ins[2 * n + k][...])
            for o, val in zip(outs[4 * k:4 * k + 4], (g, delta, m_new, v_new)):
                o[...] = val

    vmem = pl.BlockSpec(memory_space=pltpu.VMEM)
    outs = pl.pallas_call(
        body, name="adamw_small",
        in_specs=[vmem] * (n_g + 3 * n), out_specs=[vmem] * (4 * n),
        out_shape=[jax.ShapeDtypeStruct(w.shape, F32) for w in ws for _ in range(4)],
    )(*grad_parts, *ws, *ms, *vs)
    return [outs[4 * k:4 * k + 4] for k in range(n)]


def _final_sums(chip, owns, slots, name):
    n = len(owns)

    def body(chip_ref, *refs):
        for w in range(n):
            own_ref, s1_ref, s2_ref, s3_ref = refs[4 * w:4 * w + 4]
            acc = own_ref[0].astype(F32) + s1_ref[0].astype(F32)
            acc = acc + s2_ref[0].astype(F32)
            refs[4 * n + w][...] = acc + s3_ref[0].astype(F32)

    def slot(a, k):
        n_s, rh, cols = a.shape
        return pl.BlockSpec((1, rh // ROW_STEPS, cols), lambda r, chip: (lax.rem(chip[0] + k, n_s), r, 0))

    operands, in_specs = [], []
    for own, slots_w in zip(owns, slots):
        operands += [own, slots_w, slots_w, slots_w]
        in_specs += [slot(own, k) for k in range(4)]
    return pl.pallas_call(
        body, name=name,
        grid_spec=pltpu.PrefetchScalarGridSpec(
            num_scalar_prefetch=1, grid=(ROW_STEPS,),
            in_specs=in_specs,
            out_specs=[pl.BlockSpec((a.shape[1] // ROW_STEPS, a.shape[2]), lambda r, chip: (r, 0)) for a in owns]),
        out_shape=[jax.ShapeDtypeStruct(a.shape[1:], F32) for a in owns],
        compiler_params=_params(1),
    )(chip, *operands)


def _adamw_halves(core, owns, siblings, ws, ms, vs, name):
    n = len(ws)

    def body(c_ref, *refs):
        ins, outs = refs[:5 * n], refs[5 * n:]
        mine = c_ref[0] == pl.program_id(0)
        for k in range(n):
            own_ref, sib_ref, w_ref, m_ref, v_ref = ins[5 * k:5 * k + 5]
            g_ref, d_ref, mo_ref, vo_ref = outs[4 * k:4 * k + 4]

            @pl.when(mine)
            def _():
                g_ref[...] = own_ref[...]

            @pl.when(jnp.logical_not(mine))
            def _():
                g_ref[...] = sib_ref[...]

            d_ref[...], mo_ref[...], vo_ref[...] = _adamw_update(g_ref[...], w_ref[...], m_ref[...], v_ref[...])

    operands, in_specs, out_specs = [], [], []
    for own, sib, w, m, v in zip(owns, siblings, ws, ms, vs):
        rows, cols = w.shape
        rb = rows // 2 // ROW_STEPS
        half = pl.BlockSpec((rb, cols), lambda h, r, c: (r, 0))
        whole = pl.BlockSpec((rb, cols), lambda h, r, c: (h * ROW_STEPS + r, 0))
        operands += [own, sib, w, m, v]
        in_specs += [half, half, whole, whole, whole]
        out_specs += [whole] * 4
    outs = pl.pallas_call(
        body, name=name,
        grid_spec=pltpu.PrefetchScalarGridSpec(
            num_scalar_prefetch=1, grid=(2, ROW_STEPS), in_specs=in_specs, out_specs=out_specs),
        out_shape=[jax.ShapeDtypeStruct(w.shape, F32) for w in ws for _ in range(4)],
        compiler_params=_params(2),
    )(core, *operands)
    return [outs[4 * k:4 * k + 4] for k in range(n)]


def _place():
    x, y, c = lax.axis_index("x"), lax.axis_index("y"), lax.axis_index("c")
    chips = [(1 - x, y), (x, 1 - y), (1 - x, 1 - y)]
    return x, y, c, chips


def _remote(src, dst, send_sem, recv_sem, device):
    return pltpu.make_async_remote_copy(src_ref=src, dst_ref=dst, send_sem=send_sem, recv_sem=recv_sem,
                                        device_id=device, device_id_type=MESH)


def _all_gather_weights(shards):
    n = len(shards)

    def body(*refs):
        ins, outs = refs[:n], refs[n:2 * n]
        send_ici, recv_ici, send_d2d, recv_d2d, send_own, recv_own = refs[2 * n:]
        x, y, c, chips = _place()
        me = 2 * x + y
        sibling = (x, y, 1 - c)
        local = [_remote(ins[w], outs[w].at[me], send_own.at[w], recv_own.at[w], sibling) for w in range(n)]
        for cp in local:
            cp.start()
        sends = []
        for j, (px, py) in enumerate(chips):
            for w in range(n):
                cp = _remote(ins[w].at[c], outs[w].at[me, c], send_ici.at[w, j], recv_ici.at[w, j], (px, py, c))
                cp.start()
                sends.append(cp)
        for j, (px, py) in enumerate(chips):
            src_chip = 2 * px + py
            for w in range(n):
                landed = outs[w].at[src_chip, c]
                _remote(landed, landed, send_ici.at[w, j], recv_ici.at[w, j], (px, py, c)).wait_recv()
                cp = _remote(landed, landed, send_d2d.at[w, j], recv_d2d.at[w, j], sibling)
                cp.start()
                sends.append(cp)
        for j, (px, py) in enumerate(chips):
            src_chip = 2 * px + py
            for w in range(n):
                passed = outs[w].at[src_chip, 1 - c]
                _remote(passed, passed, send_d2d.at[w, j], recv_d2d.at[w, j], sibling).wait_recv()
        for cp in sends:
            cp.wait_send()
        for cp in local:
            cp.wait()

    return pl.pallas_call(
        body, name="ag_weights",
        in_specs=[ANY] * n, out_specs=[ANY] * n,
        out_shape=[jax.ShapeDtypeStruct((N_SHARD,) + s.shape, s.dtype) for s in shards],
        scratch_shapes=[pltpu.SemaphoreType.DMA((n, N_CHIP_PEERS))] * 4 + [pltpu.SemaphoreType.DMA((n,))] * 2,
    )(*shards)


def _split_start(name, arrays, n_copies, plan, after=None):
    n = len(arrays)
    n_in = n + (after is not None)

    def body(*refs):
        ins, send_sem, recv_sem, token = refs[:n], refs[n_in], refs[n_in + 1], refs[-1]
        for k, (src, dst, device) in enumerate(plan(ins)):
            _remote(src, dst, send_sem.at[k], recv_sem.at[k], device).start()
        token[...] = jnp.zeros_like(token)

    outs = pl.pallas_call(
        body, name=name,
        in_specs=[HBM] * n + [ANY] * (n_in - n),
        out_specs=(SEM, SEM, *[HBM] * n, pl.BlockSpec(memory_space=pltpu.VMEM)),
        out_shape=(pltpu.SemaphoreType.DMA((n_copies,)), pltpu.SemaphoreType.DMA((n_copies,)),
                   *[pltpu.HBM(a.shape, a.dtype) for a in arrays], jax.ShapeDtypeStruct((8, LANES), F32)),
        input_output_aliases={i: i + 2 for i in range(n)},
        compiler_params=pltpu.CompilerParams(has_side_effects=EFFECT),
    )(*[pltpu.with_memory_space_constraint(a, pltpu.HBM) for a in arrays], *([] if after is None else [after]))
    return outs[0], outs[1], list(outs[2:2 + n]), outs[-1]


def _split_wait(name, send_sem, recv_sem, arrays, plan, *after):
    n = len(arrays)

    def body(*refs):
        ins, send_ref, recv_ref = refs[:n], refs[n], refs[n + 1]
        for k, (src, dst, device) in enumerate(plan(ins)):
            cp = _remote(src, dst, send_ref.at[k], recv_ref.at[k], device)
            cp.wait_send()
            cp.wait_recv()

    outs = pl.pallas_call(
        body, name=name,
        in_specs=[HBM] * n + [SEM, SEM] + [ANY] * len(after), out_specs=[HBM] * n,
        out_shape=[pltpu.HBM(a.shape, a.dtype) for a in arrays],
        input_output_aliases={i: i for i in range(n)},
        compiler_params=pltpu.CompilerParams(has_side_effects=EFFECT),
    )(*arrays, send_sem, recv_sem, *after)
    return list(outs)


def _plan_gather_send(n):
    def plan(refs):
        x, y, c, chips = _place()
        me = 2 * x + y
        copies = []
        for w in range(n):
            own, full = refs[w], refs[n + w]
            copies.append((own, full.at[me], (x, y, 1 - c)))
            copies += [(own.at[c], full.at[me, c], (px, py, c)) for px, py in chips]
        return copies
    return plan


def _plan_gather_direct(n):
    def plan(refs):
        x, y, c, chips = _place()
        me = 2 * x + y
        copies = []
        for w in range(n):
            own, full = refs[w], refs[n + w]
            copies.append((own, full.at[me], (x, y, 1 - c)))
            for px, py in chips:
                copies += [(own.at[c], full.at[me, c], (px, py, c)), (own.at[c], full.at[me, c], (px, py, 1 - c))]
        return copies
    return plan


def _plan_gather_pass(n):
    def plan(refs):
        x, y, c, chips = _place()
        copies = []
        for w in range(n):
            for px, py in chips:
                landed = refs[w].at[2 * px + py, c]
                copies.append((landed, landed, (x, y, 1 - c)))
        return copies
    return plan


def _plan_exchange(n, with_small):
    def plan(refs):
        x, y, c, chips = _place()
        me = 2 * x + y
        n_in = n + int(with_small)
        copies = []
        for w in range(n):
            copies += [(refs[w].at[2 * px + py], refs[n_in + w].at[me], (px, py, c)) for px, py in chips]
        if with_small:
            small, slots = refs[n], refs[n_in + n]
            copies.append((small, slots.at[me], (x, y, 1 - c)))
            copies += [(small, slots.at[me], (px, py, c)) for px, py in chips]
        return copies
    return plan


def _swap_halves(name, grads, wholes, after=None):
    n = len(grads)
    arrays = list(grads) + list(wholes)
    n_all = len(arrays)
    n_in = n_all + (after is not None)

    def body(*refs):
        ins, outs = refs[:n_all], refs[n_in:n_in + n_all]
        send_sem, recv_sem = refs[n_in + n_all:]
        x, y, c, _ = _place()
        sibling = (x, y, 1 - c)
        copies = []
        for w in range(n_all):
            src = ins[w].at[1 - c] if w < n else ins[w]
            cp = _remote(src, outs[w], send_sem.at[w], recv_sem.at[w], sibling)
            cp.start()
            copies.append(cp)
        for cp in copies:
            cp.wait()

    return pl.pallas_call(
        body, name=name,
        in_specs=[ANY] * n_in, out_specs=[ANY] * n_all,
        out_shape=[jax.ShapeDtypeStruct(g.shape[1:], g.dtype) for g in grads]
        + [jax.ShapeDtypeStruct(a.shape, a.dtype) for a in wholes],
        scratch_shapes=[pltpu.SemaphoreType.DMA((n_all,))] * 2,
    )(*arrays, *([] if after is None else [after]))


def _all_reduce_tile(part):
    n_dev = 8

    def body(part_ref, sum_ref, slots_ref, send_sem, recv_sem):
        x, y, c, _ = _place()
        me = 4 * x + 2 * y + c
        slots_ref[me] = part_ref[...]
        flips = [(fx, fy, fc) for fx in (0, 1) for fy in (0, 1) for fc in (0, 1)][1:]
        copies = []
        for k, (fx, fy, fc) in enumerate(flips):
            peer = (x + fx - 2 * x * fx, y + fy - 2 * y * fy, c + fc - 2 * c * fc)
            cp = _remote(part_ref, slots_ref.at[me], send_sem.at[k], recv_sem.at[k], peer)
            cp.start()
            copies.append(cp)
        for cp in copies:
            cp.wait()
        acc = slots_ref[0]
        for k in range(1, n_dev):
            acc = acc + slots_ref[k]
        sum_ref[...] = acc

    return pl.pallas_call(
        body, name="all_reduce_tile",
        in_specs=[pl.BlockSpec(memory_space=pltpu.VMEM)],
        out_specs=pl.BlockSpec(memory_space=pltpu.VMEM),
        out_shape=jax.ShapeDtypeStruct(part.shape, F32),
        scratch_shapes=[pltpu.VMEM((n_dev,) + part.shape, F32),
                        pltpu.SemaphoreType.DMA((n_dev - 1,)), pltpu.SemaphoreType.DMA((n_dev - 1,))],
    )(part)


_SMALL = ("norm1_pre_g", "b_in", "pool_scale", "sgu_ln_g", "sgu_ln_b", "w_spatial",
          "norm1_post_g", "norm2_pre_g", "norm2_post_g", "b_spatial")
_MIX = ("w_in", "w_pool", "w_sgu_proj", "w_out")
_FF = ("w_ff1", "w_ff2")
_BIG = _MIX + _FF
_ORDER = ("norm1_pre_g", "w_in", "b_in", "w_pool", "pool_scale", "sgu_ln_g", "sgu_ln_b", "w_spatial", "b_spatial",
          "w_sgu_proj", "w_out", "norm1_post_g", "norm2_pre_g", "w_ff1", "w_ff2", "norm2_post_g")


def _pack(keys, rows, parts):
    flat = jnp.concatenate([parts[k].reshape(-1).astype(F32) for k in keys])
    flat = jnp.pad(flat, (0, rows * LANES - flat.shape[0]))
    return flat.reshape(rows, LANES)


def _halves(a):
    return a.reshape(2, a.shape[0] // 2, a.shape[1])


def _step(p, m, v, x, target):
    t = x.shape[0]
    tm_mm = min(1024, t)
    tm_mix = min(256, t)
    tm_ffn = min(512, t)
    core = lax.axis_index("c").astype(jnp.int32).reshape(1)
    row = lambda a: a.reshape(1, -1)

    chip = (2 * lax.axis_index("x") + lax.axis_index("y")).astype(jnp.int32).reshape(1)

    local2d = {"w_in": p["w_in"], "w_ff1": p["w_ff1"], "w_ff2": p["w_ff2"], "w_out": p["w_out"],
               "w_pool": p["w_pool"].reshape(N_HEADS * 64, POOL_GROUP),
               "w_sgu_proj": p["w_sgu_proj"].reshape(N_HEADS * 64, HEAD)}
    shard = {k: _halves(local2d[k].astype(BF16)) for k in _BIG}
    landing = lambda keys: [lax.empty((N_SHARD,) + shard[k].shape, BF16) for k in keys]
    w_in4 = _all_gather_weights([shard["w_in"]])[0].reshape(N_SHARD, D, D_IN // N_SHARD)
    mix_keys = _MIX[1:]
    mix_arrays = [shard[k] for k in mix_keys] + landing(mix_keys)
    send0, recv0, mix_arrays, token = _split_start("ag_mix_send", mix_arrays, 7 * len(mix_keys),
                                                   _plan_gather_direct(len(mix_keys)), after=w_in4)
    ff_arrays = [shard[k] for k in _FF] + landing(_FF)
    send1, recv1, ff_arrays, token = _split_start("ag_ff_send", ff_arrays, 4 * len(_FF), _plan_gather_send(len(_FF)),
                                                  after=token)

    pos = jnp.arange(SGU_BLOCK) // 64
    mask = (pos[:, None] >= pos[None, :]).astype(F32)
    ws = (p["w_spatial"] * mask[None]).astype(BF16)
    ws_t = jnp.swapaxes(ws, 1, 2)
    bsp = p["b_spatial"].reshape(N_HEADS, SGU_BLOCK, 1)

    z, xn = _inproj(x, row(p["norm1_pre_g"]), w_in4, row(p["b_in"]), token, tm_mm)
    mix_arrays = _split_wait("ag_mix_land", send0, recv0, mix_arrays, _plan_gather_direct(len(mix_keys)), z)
    full = dict(zip(mix_keys, mix_arrays[len(mix_keys):]))
    w_out = full["w_out"].reshape(D, D)
    regroup = lambda a: a.reshape(N_SHARD, N_HEADS, 64, 256).transpose(1, 0, 2, 3).reshape(N_HEADS, 256, 256)
    w_pool = regroup(full["w_pool"])
    w_proj = regroup(full["w_sgu_proj"])
    pooled, merged, y, h1 = _mix_fwd(z, x, w_pool, row(p["pool_scale"]), row(p["sgu_ln_g"]), row(p["sgu_ln_b"]),
                                     ws, bsp, w_proj, w_out, row(p["norm1_post_g"]), tm_mix)
    ff_arrays = _split_wait("ag_ff_land", send1, recv1, ff_arrays, _plan_gather_send(len(_FF)), pooled)
    send2, recv2, ff_full, token = _split_start("ag_ff_pass", ff_arrays[len(_FF):], 3 * len(_FF),
                                                _plan_gather_pass(len(_FF)))
    ff_full = _split_wait("ag_ff_passed", send2, recv2, ff_full, _plan_gather_pass(len(_FF)), token)
    w_ff1_4 = ff_full[0].reshape(N_SHARD, D, D)
    w_ff2 = ff_full[1].reshape(N_SHARD, D, D)
    hn, f1, f, df2, dout, vec_a = _ffn_fwd(h1, row(p["norm2_pre_g"]), w_ff1_4, w_ff2, row(p["norm2_post_g"]),
                                           target, tm_ffn)

    tk = min(1024, t)
    g_big = {"w_ff2": _dw(f, df2, "dw_ff2", 2, D, tk, N_SHARD, lambda i, j: i)}
    df1, dh1, dy, vec_b = _ffn_bwd(df2, f1, w_ff1_4, w_ff2, h1, dout, y, row(p["norm2_pre_g"]),
                                   row(p["norm1_post_g"]), tm_ffn)
    g_big["w_ff1"] = _dw(hn, df1, "dw_ff1", 2, D, tk, N_SHARD, lambda i, j: j)

    def reduce_start(name, keys, wholes):
        swapped = _swap_halves("rs_swap_" + name, [g_big[k] for k in keys], wholes)
        sums = _chip_sums(core, [g_big[k] for k in keys], swapped[:len(keys)], "chip_sums_" + name)
        arrays = list(sums)
        if wholes:
            arrays.append(_slot_sum(jnp.stack([wholes[0], swapped[-1]]), "chip_sum_small"))
        arrays += [lax.empty(s.shape, BF16) for s in sums]
        if wholes:
            arrays.append(lax.empty((N_SHARD,) + wholes[0].shape, F32))
        plan = _plan_exchange(len(keys), bool(wholes))
        send, recv, arrays, tok = _split_start("rs_send_" + name, arrays, 3 * len(keys) + 4 * len(wholes), plan)
        return send, recv, arrays, plan, tok

    def reduce_finish(name, keys, state, *after):
        send, recv, arrays, plan, _ = state
        arrays = _split_wait("rs_land_" + name, send, recv, arrays, plan, *after)
        n_in = len(arrays) // 2
        n = len(keys)
        finished = _final_sums(chip, arrays[:n], arrays[n_in:n_in + n], "final_sums_" + name)
        return dict(zip(keys, finished)), arrays

    by_half = lambda a, rows: a.reshape(N_SHARD, 2, rows // 2, a.shape[-1]).transpose(1, 0, 2, 3)
    dw_out = _dw(merged, dy, "dw_out", 1, D, tk, 1, lambda i, j: 0).reshape(D, D)
    g_big["w_out"] = by_half(dw_out, D // N_SHARD)
    early = _FF + ("w_out",)
    late = ("w_in", "w_pool", "w_sgu_proj")
    early_state = reduce_start("early", early, [])
    dz, dwp, dwpr, dws, dbsp, vec_c, dbin = _mix_bwd(z, pooled, dy, w_pool, row(p["pool_scale"]), row(p["sgu_ln_g"]),
                                                     row(p["sgu_ln_b"]), ws, ws_t, bsp, w_proj, w_out, mask, tm_mix,
                                                     early_state[4])
    g_big["w_in"] = _dw(xn, dz, "dw_in", 2, D_IN // N_SHARD, tk, N_SHARD, lambda i, j: j)
    shard_major = lambda a: a.reshape(N_HEADS, N_SHARD, 64, 256).transpose(1, 0, 2, 3).reshape(N_SHARD * 256, 256)
    g_big["w_pool"] = by_half(shard_major(dwp).astype(BF16), 256)
    g_big["w_sgu_proj"] = by_half(shard_major(dwpr).astype(BF16), 256)
    halves, _ = reduce_finish("early", early, early_state, g_big["w_in"])

    small_part = _pack(_SMALL[1:], SMALL_ROWS - 8, {
        "b_in": dbin[0], "pool_scale": vec_c[0], "sgu_ln_g": vec_c[1], "sgu_ln_b": vec_c[2],
        "w_spatial": dws, "b_spatial": dbsp[:, ::LANES].T, "norm1_post_g": vec_b[1], "norm2_pre_g": vec_b[0],
        "norm2_post_g": vec_a[0]})
    late_state = reduce_start("late", late, [small_part])
    grad_x, vec_d = _dx(dz, w_in4, x, dh1, row(p["norm1_pre_g"]), late_state[4], tm_mm)
    tiles = _all_reduce_tile(jnp.concatenate([vec_d[0].reshape(8, LANES),
                                              jnp.broadcast_to(vec_a[1:2, :LANES], (8, LANES))]))
    g1_grad, loss = tiles[:8], tiles[8, 0]

    grads, delta, new_m, new_v = {}, {}, {}, {}

    def finish(name, keys, after):
        shared = _swap_halves("rs_share_" + name, [], [halves[k] for k in keys], after)
        as2d = lambda a, k: a.reshape(local2d[k].shape)
        outs = _adamw_halves(core, [halves[k] for k in keys], shared, [local2d[k] for k in keys],
                             [as2d(m[k], k) for k in keys], [as2d(v[k], k) for k in keys], "adamw_" + name)
        for k, outs_k in zip(keys, outs):
            grads[k], delta[k], new_m[k], new_v[k] = (a.reshape(p[k].shape) for a in outs_k)

    finish("early", early, late_state[4])
    late_halves, late_arrays = reduce_finish("late", late, late_state, grad_x, *[new_v[k] for k in early])
    halves.update(late_halves)
    finish("late", late, None)
    as_rows = lambda d: [d[k].reshape(-1, LANES) for k in _SMALL]
    small_outs = _adamw_small([g1_grad, _slot_sum(late_arrays[-1], "final_sum_small")],
                              as_rows(p), as_rows(m), as_rows(v))
    for k, outs_k in zip(_SMALL, small_outs):
        grads[k], delta[k], new_m[k], new_v[k] = (a.reshape(p[k].shape) for a in outs_k)

    return (loss, grad_x, *[grads[k] for k in _ORDER], *[delta[k] for k in _ORDER],
            *[new_m[k] for k in _ORDER], *[new_v[k] for k in _ORDER])


def kernel(x, norm1_pre_g, w_in, b_in, w_pool, pool_scale, sgu_ln_g, sgu_ln_b, w_spatial, b_spatial, w_sgu_proj, w_out, norm1_post_g, norm2_pre_g, w_ff1, w_ff2, norm2_post_g, loss_target, m_norm1_pre_g, m_w_in, m_b_in, m_w_pool, m_pool_scale, m_sgu_ln_g, m_sgu_ln_b, m_w_spatial, m_b_spatial, m_w_sgu_proj, m_w_out, m_norm1_post_g, m_norm2_pre_g, m_w_ff1, m_w_ff2, m_norm2_post_g, v_norm1_pre_g, v_w_in, v_b_in, v_w_pool, v_pool_scale, v_sgu_ln_g, v_sgu_ln_b, v_w_spatial, v_b_spatial, v_w_sgu_proj, v_w_out, v_norm1_post_g, v_norm2_pre_g, v_w_ff1, v_w_ff2, v_norm2_post_g):
    p = dict(zip(_ORDER, (norm1_pre_g, w_in, b_in, w_pool, pool_scale, sgu_ln_g, sgu_ln_b, w_spatial, b_spatial,
                          w_sgu_proj, w_out, norm1_post_g, norm2_pre_g, w_ff1, w_ff2, norm2_post_g)))
    m = dict(zip(_ORDER, (m_norm1_pre_g, m_w_in, m_b_in, m_w_pool, m_pool_scale, m_sgu_ln_g, m_sgu_ln_b, m_w_spatial,
                          m_b_spatial, m_w_sgu_proj, m_w_out, m_norm1_post_g, m_norm2_pre_g, m_w_ff1, m_w_ff2,
                          m_norm2_post_g)))
    v = dict(zip(_ORDER, (v_norm1_pre_g, v_w_in, v_b_in, v_w_pool, v_pool_scale, v_sgu_ln_g, v_sgu_ln_b, v_w_spatial,
                          v_b_spatial, v_w_sgu_proj, v_w_out, v_norm1_post_g, v_norm2_pre_g, v_w_ff1, v_w_ff2,
                          v_norm2_post_g)))
    outs = _step(p, m, v, x[0], loss_target[0])
    return (outs[0], outs[1][None], *outs[2:])
```

```python
import math

import jax
import jax.numpy as jnp
from jax import lax
from jax.experimental import pallas as pl
from jax.experimental.pallas import tpu as pltpu

F32 = jnp.float32
BF16 = jnp.bfloat16

D = 1024
D_IN = 5 * D
D_FF = 4 * D
N_SHARD = 4
N_CHIP_PEERS = 3
POOL_WINDOWS = (2, 4, 8, 16)
POOL_GROUP = 256
POOL_HALO = 16
SGU_BLOCK = 128
N_HEADS = 4
HEAD = 256
EPS = 1e-6

ADAM_LR = 0.001
ADAM_B1 = 0.9
ADAM_B2 = 0.999
ADAM_EPS = 1e-08
ADAM_WD = 0.01
ADAM_STEP = 10

V7X_VMEM_BYTES = 64 * 1024 * 1024
VMEM_LIMIT = V7X_VMEM_BYTES - 8 * 1024 * 1024
SMALL_ROWS = 616
LANES = 128

MESH = pl.DeviceIdType.MESH
ANY = pl.BlockSpec(memory_space=pl.ANY)
HBM = pl.BlockSpec(memory_space=pltpu.HBM)
SEM = pl.BlockSpec(memory_space=pltpu.SEMAPHORE)
EFFECT = pltpu.SideEffectType.DATAFLOW_SIDE_EFFECTING

NT_DIMS = (((1,), (1,)), ((), ()))
TN_DIMS = (((0,), (0,)), ((), ()))


def _params(n_axes):
    return pltpu.CompilerParams(dimension_semantics=("arbitrary",) * n_axes, vmem_limit_bytes=VMEM_LIMIT)


def _resident(shape, n_axes):
    zeros = (0,) * len(shape)
    index = (lambda i: zeros) if n_axes == 1 else (lambda i, j: zeros)
    return pl.BlockSpec(shape, index, pipeline_mode=pl.Buffered(1))


def _dot(a, b):
    return jnp.dot(a, b, preferred_element_type=F32)


def _dot_nt(a, b):
    return lax.dot_general(a, b, NT_DIMS, preferred_element_type=F32)


def _dot_tn(a, b):
    return lax.dot_general(a, b, TN_DIMS, preferred_element_type=F32)


def _mean(a):
    return jnp.mean(a, axis=-1, keepdims=True)


def _colsum(a):
    return jnp.sum(a, axis=0, keepdims=True)


_GELU_C0 = math.sqrt(2.0 / math.pi)
_GELU_C1 = 0.044715


def _gelu(a):
    t = jnp.tanh(a * (_GELU_C0 + (_GELU_C0 * _GELU_C1) * (a * a)))
    return a * (0.5 * t + 0.5)


def _gelu_and_grad(a):
    a2 = a * a
    t = jnp.tanh(a * (_GELU_C0 + (_GELU_C0 * _GELU_C1) * a2))
    cdf = 0.5 * t + 0.5
    grad = cdf + (0.5 * a) * (1.0 - t * t) * (_GELU_C0 + (3.0 * _GELU_C0 * _GELU_C1) * a2)
    return a * cdf, grad


def _sigmoid(a):
    return 0.5 * jnp.tanh(0.5 * a) + 0.5


def _rms_norm_bf16(a, g):
    return (a * lax.rsqrt(_mean(a * a) + EPS) * g).astype(BF16)


def _relu_sq_bf16(a):
    r = jnp.maximum(a, 0.0)
    return (r * r).astype(BF16)


def _window_sum(a, w, ahead):
    n = a.shape[0]
    step = 1
    while step < w:
        a = a + pltpu.roll(a, n - step if ahead else step, 0)
        step *= 2
    return a


def _rms_bwd(dn, n, r):
    return r * (dn - n * _mean(dn * n))


def _inproj(x, g1, w_in4, b_in, dep, tm):
    t = x.shape[0]
    cw = D_IN // N_SHARD

    def body(x_ref, g_ref, w_ref, b_ref, dep_ref, z_ref, xn_ref):
        @pl.when(pl.program_id(1) == 0)
        def _():
            xn_ref[...] = _rms_norm_bf16(x_ref[...], g_ref[...])

        z_ref[...] = _dot(xn_ref[...], w_ref[pl.program_id(1)]) + b_ref[...]

    return pl.pallas_call(
        body, name="inproj",
        grid=(t // tm, N_SHARD),
        in_specs=[pl.BlockSpec((tm, D), lambda i, s: (i, 0)),
                  pl.BlockSpec((1, D), lambda i, s: (0, 0)),
                  _resident((N_SHARD, D, cw), 2),
                  pl.BlockSpec((1, cw), lambda i, s: (0, s)),
                  pl.BlockSpec(dep.shape, lambda i, s: (0, 0))],
        out_specs=pl.BlockSpec((tm, cw), lambda i, s: (i, s)),
        out_shape=jax.ShapeDtypeStruct((t, D_IN), F32),
        scratch_shapes=[pltpu.VMEM((tm, D), BF16)],
        compiler_params=_params(2),
    )(x, g1, w_in4, b_in, dep)


def _sgu_forward(z_ref, lng_ref, lnb_ref, ws_ref, bsp_ref, sv_ref, tm, with_grad):
    if with_grad:
        u, du_dz = _gelu_and_grad(z_ref[:, D:2 * D])
        gv, dgv_dz = _gelu_and_grad(z_ref[:, 2 * D:3 * D])
    else:
        u = _gelu(z_ref[:, D:2 * D])
        gv = _gelu(z_ref[:, 2 * D:3 * D])
        du_dz = dgv_dz = None
    xc = gv - _mean(gv)
    rstd = lax.rsqrt(_mean(xc * xc) + EPS)
    vhat = xc * rstd
    vb = (vhat * lng_ref[...] + lnb_ref[...]).astype(BF16)
    for b in range(tm // SGU_BLOCK):
        rows = slice(b * SGU_BLOCK, (b + 1) * SGU_BLOCK)
        for h in range(N_HEADS):
            cols = slice(h * HEAD, (h + 1) * HEAD)
            sv_ref[rows, cols] = _dot(ws_ref[h], vb[rows, cols]) + bsp_ref[h]
    return u, du_dz, dgv_dz, vhat, rstd, vb


def _mix_fwd(z, x, w_pool, pool_scale, ln_g, ln_b, ws, bsp, w_proj, w_out, g1post, tm):
    t = x.shape[0]

    def body(z_ref, x_ref, wp_ref, psc_ref, lng_ref, lnb_ref, ws_ref, bsp_ref, wpr_ref, wo_ref, g1p_ref,
             pooled_ref, merged_ref, y_ref, h1_ref, win_ref, sv_ref):
        i = pl.program_id(0)

        @pl.when(i == 0)
        def _():
            win_ref[0:POOL_HALO, :] = jnp.zeros((POOL_HALO, D), F32)

        win_ref[POOL_HALO:POOL_HALO + tm, :] = z_ref[:, 0:D]
        pos1 = lax.broadcasted_iota(jnp.int32, (tm, 1), 0) + (i * tm + 1)
        a_parts = []
        for g, w in enumerate(POOL_WINDOWS):
            cols = slice(g * POOL_GROUP, (g + 1) * POOL_GROUP)
            ts = _window_sum(win_ref[:, cols], w, False)[POOL_HALO:POOL_HALO + tm, :]
            inv_cnt = 1.0 / jnp.minimum(pos1, w).astype(F32)
            pooled = (ts * inv_cnt - z_ref[:, cols]).astype(BF16)
            pooled_ref[:, cols] = pooled
            a_parts.append(_dot(pooled, wp_ref[g]) * psc_ref[:, cols])
        win_ref[0:POOL_HALO, :] = win_ref[tm:tm + POOL_HALO, :]

        u, _, _, _, _, _ = _sgu_forward(z_ref, lng_ref, lnb_ref, ws_ref, bsp_ref, sv_ref, tm, False)
        gated = (u * sv_ref[...]).astype(BF16)
        b_parts = [_dot(gated[:, h * HEAD:(h + 1) * HEAD], wpr_ref[h]) for h in range(N_HEADS)]

        a = jnp.concatenate(a_parts, axis=1)
        bbr = jnp.concatenate(b_parts, axis=1)
        merged = (_sigmoid(z_ref[:, 3 * D:4 * D]) * a + _sigmoid(z_ref[:, 4 * D:5 * D]) * bbr).astype(BF16)
        merged_ref[...] = merged
        y = _dot(merged, wo_ref[...])
        y_ref[...] = y
        r = lax.rsqrt(_mean(y * y) + EPS)
        h1_ref[...] = x_ref[...] + (y * r) * g1p_ref[...]

    row = lambda i: (i, 0)
    fixed2 = lambda i: (0, 0)
    fixed3 = lambda i: (0, 0, 0)
    vec = pl.BlockSpec((1, D), fixed2)
    return pl.pallas_call(
        body, name="mix_fwd",
        grid=(t // tm,),
        in_specs=[pl.BlockSpec((tm, D_IN), row), pl.BlockSpec((tm, D), row),
                  pl.BlockSpec((N_HEADS, POOL_GROUP, POOL_GROUP), fixed3), vec, vec, vec,
                  pl.BlockSpec((N_HEADS, SGU_BLOCK, SGU_BLOCK), fixed3),
                  pl.BlockSpec((N_HEADS, SGU_BLOCK, 1), fixed3),
                  pl.BlockSpec((N_HEADS, HEAD, HEAD), fixed3),
                  pl.BlockSpec((D, D), fixed2), vec],
        out_specs=[pl.BlockSpec((tm, D), row)] * 4,
        out_shape=[jax.ShapeDtypeStruct((t, D), BF16), jax.ShapeDtypeStruct((t, D), BF16),
                   jax.ShapeDtypeStruct((t, D), F32), jax.ShapeDtypeStruct((t, D), F32)],
        scratch_shapes=[pltpu.VMEM((tm + POOL_HALO, D), F32), pltpu.VMEM((tm, D), F32)],
        compiler_params=_params(1),
    )(z, x, w_pool, pool_scale, ln_g, ln_b, ws, bsp, w_proj, w_out, g1post)


def _ffn_fwd(h1, g2pre, w1, w2, g2post, target, tm):
    t = h1.shape[0]
    n_j = D_FF // D

    def body(h1_ref, g2_ref, w1_ref, w2_ref, g2p_ref, tgt_ref,
             f1_ref, df2_ref, dout_ref, vec_ref, hn_ref, acc_ref):
        i, j = pl.program_id(0), pl.program_id(1)

        @pl.when((i == 0) & (j == 0))
        def _():
            vec_ref[...] = jnp.zeros_like(vec_ref)

        @pl.when(j == 0)
        def _():
            hn_ref[...] = _rms_norm_bf16(h1_ref[...], g2_ref[...])
            acc_ref[...] = jnp.zeros_like(acc_ref)

        f1 = _dot(hn_ref[...], w1_ref[j])
        f1_ref[...] = f1
        acc_ref[...] += _dot(_relu_sq_bf16(f1), w2_ref[j])

        @pl.when(j == n_j - 1)
        def _():
            f2 = acc_ref[...]
            r = lax.rsqrt(_mean(f2 * f2) + EPS)
            n2 = f2 * r
            err = h1_ref[...] + n2 * g2p_ref[...] - tgt_ref[...]
            loss = 0.5 * jnp.sum(_mean(err * err))
            dout = err * (1.0 / D)
            dout_ref[...] = dout
            vec_ref[0:1, :] += _colsum(dout * n2)
            vec_ref[1:2, :] += jnp.full((1, D), loss, F32)
            df2_ref[...] = _rms_bwd(dout * g2p_ref[...], n2, r).astype(BF16)

    row = lambda i, j: (i, 0)
    fixed = lambda i, j: (0, 0)
    vec = pl.BlockSpec((1, D), fixed)
    return pl.pallas_call(
        body, name="ffn_fwd",
        grid=(t // tm, n_j),
        in_specs=[pl.BlockSpec((tm, D), row), vec,
                  _resident((n_j, D, D), 2), _resident((n_j, D, D), 2), vec,
                  pl.BlockSpec((tm, D), row)],
        out_specs=[pl.BlockSpec((tm, D), lambda i, j: (i, j)),
                   pl.BlockSpec((tm, D), row), pl.BlockSpec((tm, D), row),
                   pl.BlockSpec((8, D), fixed)],
        out_shape=[jax.ShapeDtypeStruct((t, D_FF), F32), jax.ShapeDtypeStruct((t, D), BF16),
                   jax.ShapeDtypeStruct((t, D), F32), jax.ShapeDtypeStruct((8, D), F32)],
        scratch_shapes=[pltpu.VMEM((tm, D), BF16), pltpu.VMEM((tm, D), F32)],
        compiler_params=_params(2),
    )(h1, g2pre, w1, w2, g2post, target)


def _ffn_bwd(df2, f1, w1, w2, h1, dout, y, g2pre, g1post, tm):
    t = h1.shape[0]
    n_j = D_FF // D

    def body(df2_ref, f1_ref, w1_ref, w2_ref, h1_ref, dout_ref, y_ref, g2_ref, g1p_ref,
             df1_ref, dh1_ref, dy_ref, vec_ref, acc_ref):
        i, j = pl.program_id(0), pl.program_id(1)

        @pl.when((i == 0) & (j == 0))
        def _():
            vec_ref[...] = jnp.zeros_like(vec_ref)

        @pl.when(j == 0)
        def _():
            acc_ref[...] = jnp.zeros_like(acc_ref)

        df = _dot_nt(df2_ref[...], w2_ref[j])
        df1 = (df * (2.0 * jnp.maximum(f1_ref[...], 0.0))).astype(BF16)
        df1_ref[...] = df1
        acc_ref[...] += _dot_nt(df1, w1_ref[j])

        @pl.when(j == n_j - 1)
        def _():
            dhn = acc_ref[...]
            h = h1_ref[...]
            r2 = lax.rsqrt(_mean(h * h) + EPS)
            nh = h * r2
            vec_ref[0:1, :] += _colsum(dhn * nh)
            dh1 = dout_ref[...] + _rms_bwd(dhn * g2_ref[...], nh, r2)
            dh1_ref[...] = dh1
            yv = y_ref[...]
            r1 = lax.rsqrt(_mean(yv * yv) + EPS)
            ny = yv * r1
            vec_ref[1:2, :] += _colsum(dh1 * ny)
            dy_ref[...] = _rms_bwd(dh1 * g1p_ref[...], ny, r1).astype(BF16)

    row = lambda i, j: (i, 0)
    fixed = lambda i, j: (0, 0)
    vec = pl.BlockSpec((1, D), fixed)
    return pl.pallas_call(
        body, name="ffn_bwd",
        grid=(t // tm, n_j),
        in_specs=[pl.BlockSpec((tm, D), row),
                  pl.BlockSpec((tm, D), lambda i, j: (i, j)),
                  _resident((n_j, D, D), 2), _resident((n_j, D, D), 2),
                  pl.BlockSpec((tm, D), row), pl.BlockSpec((tm, D), row), pl.BlockSpec((tm, D), row), vec, vec],
        out_specs=[pl.BlockSpec((tm, D), lambda i, j: (i, j)),
                   pl.BlockSpec((tm, D), row), pl.BlockSpec((tm, D), row),
                   pl.BlockSpec((8, D), fixed)],
        out_shape=[jax.ShapeDtypeStruct((t, D_FF), BF16), jax.ShapeDtypeStruct((t, D), F32),
                   jax.ShapeDtypeStruct((t, D), BF16), jax.ShapeDtypeStruct((8, D), F32)],
        scratch_shapes=[pltpu.VMEM((tm, D), F32)],
        compiler_params=_params(2),
    )(df2, f1, w1, w2, h1, dout, y, g2pre, g1post)


def _mix_bwd(z, pooled, dy, w_pool, pool_scale, ln_g, ln_b, ws, ws_t, bsp, w_proj, w_out, mask, tm, dep):
    t = z.shape[0]
    n_t = t // tm

    def body(z_ref, pooled_ref, dy_ref, wp_ref, psc_ref, lng_ref, lnb_ref, ws_ref, wst_ref, bsp_ref, wpr_ref,
             wo_ref, mask_ref, dep_ref,
             dz_ref, dwp_ref, dwpr_ref, dws_ref, dbsp_ref, vec_ref, dbin_ref,
             win_ref, sv_ref, dsv_ref, dv_ref, dsvsum_ref):
        i = pl.program_id(0)

        @pl.when(i == 0)
        def _():
            win_ref[tm:tm + POOL_HALO, :] = jnp.zeros((POOL_HALO, D), F32)
            dwp_ref[...] = jnp.zeros_like(dwp_ref)
            dwpr_ref[...] = jnp.zeros_like(dwpr_ref)
            dws_ref[...] = jnp.zeros_like(dws_ref)
            vec_ref[...] = jnp.zeros_like(vec_ref)
            dbin_ref[...] = jnp.zeros_like(dbin_ref)
            dsvsum_ref[...] = jnp.zeros_like(dsvsum_ref)

        dmerged = _dot_nt(dy_ref[...], wo_ref[...])

        u, du_dz, dgv_dz, vhat, rstd, vb = _sgu_forward(z_ref, lng_ref, lnb_ref, ws_ref, bsp_ref, sv_ref, tm, True)
        sv = sv_ref[...]
        gated = (u * sv).astype(BF16)

        sa = _sigmoid(z_ref[:, 3 * D:4 * D])
        sb = _sigmoid(z_ref[:, 4 * D:5 * D])
        da = dmerged * sa
        dbbr = (dmerged * sb).astype(BF16)

        tile = n_t - 1 - i
        pos1 = lax.broadcasted_iota(jnp.int32, (tm, 1), 0) + (tile * tm + 1)
        dzga_parts, dpooled_parts = [], []
        for g, w in enumerate(POOL_WINDOWS):
            cols = slice(g * POOL_GROUP, (g + 1) * POOL_GROUP)
            pg = pooled_ref[:, cols]
            a_pre = _dot(pg, wp_ref[g])
            da_g = da[:, cols]
            psc = psc_ref[:, cols]
            vec_ref[0:1, cols] += _colsum(da_g * a_pre)
            sa_g = sa[:, cols]
            dzga_parts.append(dmerged[:, cols] * (a_pre * psc) * (sa_g * (1.0 - sa_g)))
            da_pre = (da_g * psc).astype(BF16)
            dwp_ref[g] += _dot_tn(pg, da_pre)
            dpooled = _dot_nt(da_pre, wp_ref[g])
            dpooled_parts.append(dpooled)
            inv_cnt = 1.0 / jnp.minimum(pos1, w).astype(F32)
            win_ref[0:tm, cols] = dpooled * inv_cnt
        dzp_parts = []
        for g, w in enumerate(POOL_WINDOWS):
            cols = slice(g * POOL_GROUP, (g + 1) * POOL_GROUP)
            acc = _window_sum(win_ref[:, cols], w, True)[0:tm, :]
            dzp_parts.append(acc - dpooled_parts[g])
        win_ref[tm:tm + POOL_HALO, :] = win_ref[0:POOL_HALO, :]

        dzgb_parts = []
        for h in range(N_HEADS):
            cols = slice(h * HEAD, (h + 1) * HEAD)
            g_h = gated[:, cols]
            db_h = dbbr[:, cols]
            bbr_h = _dot(g_h, wpr_ref[h])
            sb_h = sb[:, cols]
            dzgb_parts.append(dmerged[:, cols] * bbr_h * (sb_h * (1.0 - sb_h)))
            dwpr_ref[h] += _dot_tn(g_h, db_h)
            dgated_h = _dot_nt(db_h, wpr_ref[h])
            dsv_ref[:, cols] = dgated_h * u[:, cols]
            sv_ref[:, cols] = dgated_h * sv[:, cols]
        dzu = sv_ref[...] * du_dz
        dsv = dsv_ref[...]
        dsvb = dsv.astype(BF16)
        blk_sum = dsv[0:SGU_BLOCK, :]
        for b in range(1, tm // SGU_BLOCK):
            blk_sum = blk_sum + dsv[b * SGU_BLOCK:(b + 1) * SGU_BLOCK, :]
        dsvsum_ref[...] += blk_sum
        for b in range(tm // SGU_BLOCK):
            rows = slice(b * SGU_BLOCK, (b + 1) * SGU_BLOCK)
            for h in range(N_HEADS):
                cols = slice(h * HEAD, (h + 1) * HEAD)
                dws_ref[h] += _dot_nt(dsvb[rows, cols], vb[rows, cols])
                dv_ref[rows, cols] = _dot(wst_ref[h], dsvb[rows, cols])
        dv = dv_ref[...]
        vec_ref[1:2, :] += _colsum(dv * vhat)
        vec_ref[2:3, :] += _colsum(dv)
        dvhat = dv * lng_ref[...]
        dgv = rstd * (dvhat - _mean(dvhat) - vhat * _mean(dvhat * vhat))
        dzv = dgv * dgv_dz

        parts = [jnp.concatenate(dzp_parts, axis=1), dzu, dzv,
                 jnp.concatenate(dzga_parts, axis=1), jnp.concatenate(dzgb_parts, axis=1)]
        for k, part in enumerate(parts):
            dz_ref[:, k * D:(k + 1) * D] = part.astype(BF16)
            dbin_ref[0:1, k * D:(k + 1) * D] += _colsum(part)

        @pl.when(i == n_t - 1)
        def _():
            for h in range(N_HEADS):
                dws_ref[h] = jnp.where(mask_ref[...] > 0.0, dws_ref[h], 0.0)
                tot = jnp.sum(dsvsum_ref[:, h * HEAD:(h + 1) * HEAD], axis=1, keepdims=True)
                dbsp_ref[:, h * LANES:(h + 1) * LANES] = jnp.broadcast_to(tot, (SGU_BLOCK, LANES))

    row = lambda i: (n_t - 1 - i, 0)
    fixed2 = lambda i: (0, 0)
    fixed3 = lambda i: (0, 0, 0)
    vec = pl.BlockSpec((1, D), fixed2)
    sq = pl.BlockSpec((N_HEADS, SGU_BLOCK, SGU_BLOCK), fixed3)
    grp = pl.BlockSpec((N_HEADS, HEAD, HEAD), fixed3)
    return pl.pallas_call(
        body, name="mix_bwd",
        grid=(n_t,),
        in_specs=[pl.BlockSpec((tm, D_IN), row), pl.BlockSpec((tm, D), row), pl.BlockSpec((tm, D), row),
                  grp, vec, vec, vec, sq, sq, pl.BlockSpec((N_HEADS, SGU_BLOCK, 1), fixed3), grp,
                  pl.BlockSpec((D, D), fixed2), pl.BlockSpec((SGU_BLOCK, SGU_BLOCK), fixed2),
                  pl.BlockSpec(dep.shape, fixed2)],
        out_specs=[pl.BlockSpec((tm, D_IN), row), grp, grp, sq,
                   pl.BlockSpec((SGU_BLOCK, N_HEADS * LANES), fixed2),
                   pl.BlockSpec((8, D), fixed2), pl.BlockSpec((8, D_IN), fixed2)],
        out_shape=[jax.ShapeDtypeStruct((t, D_IN), BF16),
                   jax.ShapeDtypeStruct((N_HEADS, HEAD, HEAD), F32), jax.ShapeDtypeStruct((N_HEADS, HEAD, HEAD), F32),
                   jax.ShapeDtypeStruct((N_HEADS, SGU_BLOCK, SGU_BLOCK), F32),
                   jax.ShapeDtypeStruct((SGU_BLOCK, N_HEADS * LANES), F32),
                   jax.ShapeDtypeStruct((8, D), F32), jax.ShapeDtypeStruct((8, D_IN), F32)],
        scratch_shapes=[pltpu.VMEM((tm + POOL_HALO, D), F32), pltpu.VMEM((tm, D), F32), pltpu.VMEM((tm, D), F32),
                        pltpu.VMEM((tm, D), F32), pltpu.VMEM((SGU_BLOCK, D), F32)],
        compiler_params=_params(1),
    )(z, pooled, dy, w_pool, pool_scale, ln_g, ln_b, ws, ws_t, bsp, w_proj, w_out, mask, dep)


def _dx(dz, w_in4, x, dh1, g1, dep, tm):
    t = x.shape[0]
    cw = D_IN // N_SHARD

    def body(dz_ref, w_ref, x_ref, dh1_ref, g_ref, dep_ref, gx_ref, vec_ref, acc_ref):
        i, s = pl.program_id(0), pl.program_id(1)

        @pl.when((i == 0) & (s == 0))
        def _():
            vec_ref[...] = jnp.zeros_like(vec_ref)

        @pl.when(s == 0)
        def _():
            acc_ref[...] = jnp.zeros_like(acc_ref)

        acc_ref[...] += _dot_nt(dz_ref[...], w_ref[s])

        @pl.when(s == N_SHARD - 1)
        def _():
            dxn = acc_ref[...]
            xv = x_ref[...]
            r = lax.rsqrt(_mean(xv * xv) + EPS)
            xh = xv * r
            vec_ref[0:1, :] += _colsum(dxn * xh)
            gx_ref[...] = dh1_ref[...] + _rms_bwd(dxn * g_ref[...], xh, r)

    row = lambda i, s: (i, 0)
    fixed = lambda i, s: (0, 0)
    return pl.pallas_call(
        body, name="dx",
        grid=(t // tm, N_SHARD),
        in_specs=[pl.BlockSpec((tm, cw), lambda i, s: (i, s)),
                  _resident((N_SHARD, D, cw), 2),
                  pl.BlockSpec((tm, D), row), pl.BlockSpec((tm, D), row), pl.BlockSpec((1, D), fixed),
                  pl.BlockSpec(dep.shape, fixed)],
        out_specs=[pl.BlockSpec((tm, D), row), pl.BlockSpec((8, D), fixed)],
        out_shape=[jax.ShapeDtypeStruct((t, D), F32), jax.ShapeDtypeStruct((8, D), F32)],
        scratch_shapes=[pltpu.VMEM((tm, D), F32)],
        compiler_params=_params(2),
    )(dz, w_in4, x, dh1, g1, dep)


def _dw(a, b, name, n_split, tn, tk, n_out, out_block, a_map=None, a_gain=None):
    t, m = a.shape
    n = b.shape[1]
    tm = min(m, D)
    rows = tm // n_split
    n_k = t // tk
    gain = [] if a_gain is None else [a_gain]

    def body(a_ref, b_ref, *rest):
        o_ref, acc_ref = rest[-2:]
        k = pl.program_id(2)

        @pl.when(k == 0)
        def _():
            acc_ref[...] = jnp.zeros_like(acc_ref)

        lhs = a_ref[...]
        if a_map is not None:
            lhs = a_map(lhs) if a_gain is None else a_map(lhs, rest[0][...])
        acc_ref[...] += _dot_tn(lhs, b_ref[...])

        @pl.when(k == n_k - 1)
        def _():
            for h in range(n_split):
                o_ref[h, 0] = acc_ref[h * rows:(h + 1) * rows, :].astype(BF16)

    return pl.pallas_call(
        body, name=name,
        grid=(m // tm, n // tn, n_k),
        in_specs=[pl.BlockSpec((tk, tm), lambda i, j, k: (k, i)),
                  pl.BlockSpec((tk, tn), lambda i, j, k: (k, j))]
        + [pl.BlockSpec(g.shape, lambda i, j, k: (0, 0)) for g in gain],
        out_specs=pl.BlockSpec((n_split, 1, rows, tn), lambda i, j, k: (0, out_block(i, j), 0, 0)),
        out_shape=jax.ShapeDtypeStruct((n_split, n_out, rows, tn), BF16),
        scratch_shapes=[pltpu.VMEM((tm, tn), F32)],
        compiler_params=_params(3),
    )(a, b, *gain)


def _row_block(rows, cols, itemsize, n_bufs):
    budget = VMEM_LIMIT // 4
    rb = rows
    while rb % 16 == 0 and rb * cols * itemsize * n_bufs * 2 > budget:
        rb //= 2
    return rb


ROW_STEPS = 4


def _chip_sums(core, grads, recvs, name):
    n = len(grads)

    def body(c_ref, *refs):
        for w in range(n):
            refs[2 * n + w][...] = (refs[w][0].astype(F32) + refs[n + w][...].astype(F32)).astype(BF16)

    def blocks(g):
        _, _, rh, cols = g.shape
        rb = rh // ROW_STEPS
        return (pl.BlockSpec((1, 1, rb, cols), lambda s, r, c: (c[0], s, r, 0)),
                pl.BlockSpec((1, rb, cols), lambda s, r, c: (s, r, 0)))

    specs = [blocks(g) for g in grads]
    return pl.pallas_call(
        body, name=name,
        grid_spec=pltpu.PrefetchScalarGridSpec(
            num_scalar_prefetch=1, grid=(N_SHARD, ROW_STEPS),
            in_specs=[s[0] for s in specs] + [s[1] for s in specs],
            out_specs=[s[1] for s in specs]),
        out_shape=[jax.ShapeDtypeStruct(g.shape[1:], BF16) for g in grads],
        compiler_params=_params(2),
    )(core, *grads, *recvs)


def _slot_sum(slots, name):
    n_s, rows, cols = slots.shape
    rb = _row_block(rows, cols, 4, n_s + 1)

    def body(s_ref, o_ref):
        acc = s_ref[0].astype(F32)
        for k in range(1, n_s):
            acc = acc + s_ref[k].astype(F32)
        o_ref[...] = acc

    return pl.pallas_call(
        body, name=name,
        grid=(rows // rb,),
        in_specs=[pl.BlockSpec((n_s, rb, cols), lambda r: (0, r, 0))],
        out_specs=pl.BlockSpec((rb, cols), lambda r: (r, 0)),
        out_shape=jax.ShapeDtypeStruct((rows, cols), F32),
        compiler_params=_params(1),
    )(slots)


def _adamw_update(g, w, m, v):
    m_new = ADAM_B1 * m + (1.0 - ADAM_B1) * g
    v_new = ADAM_B2 * v + (1.0 - ADAM_B2) * (g * g)
    m_hat = m_new / (1.0 - ADAM_B1 ** ADAM_STEP)
    v_hat = v_new / (1.0 - ADAM_B2 ** ADAM_STEP)
    return -ADAM_LR * (m_hat / (jnp.sqrt(v_hat) + ADAM_EPS) + ADAM_WD * w), m_new, v_new


def _adamw_small(grad_parts, ws, ms, vs):
    n, n_g = len(ws), len(grad_parts)

    def body(*refs):
        g_refs, ins, outs = refs[:n_g], refs[n_g:n_g + 3 * n], refs[n_g + 3 * n:]
        part, off = 0, 0
        for k in range(n):
            rows = ws[k].shape[0]
            if off == grad_parts[part].shape[0]:
                part, off = part + 1, 0
            g = g_refs[part][off:off + rows, :]
            off += rows
            delta, m_new, v_new = _adamw_update(g, ins[k][...], ins[n + k][...], ins[2 * n + k][...])
            for o, val in zip(outs[4 * k:4 * k + 4], (g, delta, m_new, v_new)):
                o[...] = val

    vmem = pl.BlockSpec(memory_space=pltpu.VMEM)
    outs = pl.pallas_call(
        body, name="adamw_small",
        in_specs=[vmem] * (n_g + 3 * n), out_specs=[vmem] * (4 * n),
        out_shape=[jax.ShapeDtypeStruct(w.shape, F32) for w in ws for _ in range(4)],
    )(*grad_parts, *ws, *ms, *vs)
    return [outs[4 * k:4 * k + 4] for k in range(n)]


def _final_sums(chip, owns, slots, name):
    n = len(owns)

    def body(chip_ref, *refs):
        for w in range(n):
            own_ref, s1_ref, s2_ref, s3_ref = refs[4 * w:4 * w + 4]
            acc = own_ref[0].astype(F32) + s1_ref[0].astype(F32)
            acc = acc + s2_ref[0].astype(F32)
            refs[4 * n + w][...] = acc + s3_ref[0].astype(F32)

    def slot(a, k):
        n_s, rh, cols = a.shape
        return pl.BlockSpec((1, rh // ROW_STEPS, cols), lambda r, chip: (lax.rem(chip[0] + k, n_s), r, 0))

    operands, in_specs = [], []
    for own, slots_w in zip(owns, slots):
        operands += [own, slots_w, slots_w, slots_w]
        in_specs += [slot(own, k) for k in range(4)]
    return pl.pallas_call(
        body, name=name,
        grid_spec=pltpu.PrefetchScalarGridSpec(
            num_scalar_prefetch=1, grid=(ROW_STEPS,),
            in_specs=in_specs,
            out_specs=[pl.BlockSpec((a.shape[1] // ROW_STEPS, a.shape[2]), lambda r, chip: (r, 0)) for a in owns]),
        out_shape=[jax.ShapeDtypeStruct(a.shape[1:], F32) for a in owns],
        compiler_params=_params(1),
    )(chip, *operands)


def _adamw_halves(core, owns, siblings, ws, ms, vs, name):
    n = len(ws)

    def body(c_ref, *refs):
        ins, outs = refs[:5 * n], refs[5 * n:]
        mine = c_ref[0] == pl.program_id(0)
        for k in range(n):
            own_ref, sib_ref, w_ref, m_ref, v_ref = ins[5 * k:5 * k + 5]
            g_ref, d_ref, mo_ref, vo_ref = outs[4 * k:4 * k + 4]

            @pl.when(mine)
            def _():
                g_ref[...] = own_ref[...]

            @pl.when(jnp.logical_not(mine))
            def _():
                g_ref[...] = sib_ref[...]

            d_ref[...], mo_ref[...], vo_ref[...] = _adamw_update(g_ref[...], w_ref[...], m_ref[...], v_ref[...])

    operands, in_specs, out_specs = [], [], []
    for own, sib, w, m, v in zip(owns, siblings, ws, ms, vs):
        rows, cols = w.shape
        rb = rows // 2 // ROW_STEPS
        half = pl.BlockSpec((rb, cols), lambda h, r, c: (r, 0))
        whole = pl.BlockSpec((rb, cols), lambda h, r, c: (h * ROW_STEPS + r, 0))
        operands += [own, sib, w, m, v]
        in_specs += [half, half, whole, whole, whole]
        out_specs += [whole] * 4
    outs = pl.pallas_call(
        body, name=name,
        grid_spec=pltpu.PrefetchScalarGridSpec(
            num_scalar_prefetch=1, grid=(2, ROW_STEPS), in_specs=in_specs, out_specs=out_specs),
        out_shape=[jax.ShapeDtypeStruct(w.shape, F32) for w in ws for _ in range(4)],
        compiler_params=_params(2),
    )(core, *operands)
    return [outs[4 * k:4 * k + 4] for k in range(n)]


def _place():
    x, y, c = lax.axis_index("x"), lax.axis_index("y"), lax.axis_index("c")
    chips = [(1 - x, y), (x, 1 - y), (1 - x, 1 - y)]
    return x, y, c, chips


def _remote(src, dst, send_sem, recv_sem, device):
    return pltpu.make_async_remote_copy(src_ref=src, dst_ref=dst, send_sem=send_sem, recv_sem=recv_sem,
                                        device_id=device, device_id_type=MESH)


def _all_gather_weights(shards):
    n = len(shards)

    def body(*refs):
        ins, outs = refs[:n], refs[n:2 * n]
        send_ici, recv_ici, send_d2d, recv_d2d, send_own, recv_own = refs[2 * n:]
        x, y, c, chips = _place()
        me = 2 * x + y
        sibling = (x, y, 1 - c)
        local = [_remote(ins[w], outs[w].at[me], send_own.at[w], recv_own.at[w], sibling) for w in range(n)]
        for cp in local:
            cp.start()
        sends = []
        for j, (px, py) in enumerate(chips):
            for w in range(n):
                cp = _remote(ins[w].at[c], outs[w].at[me, c], send_ici.at[w, j], recv_ici.at[w, j], (px, py, c))
                cp.start()
                sends.append(cp)
        for j, (px, py) in enumerate(chips):
            src_chip = 2 * px + py
            for w in range(n):
                landed = outs[w].at[src_chip, c]
                _remote(landed, landed, send_ici.at[w, j], recv_ici.at[w, j], (px, py, c)).wait_recv()
                cp = _remote(landed, landed, send_d2d.at[w, j], recv_d2d.at[w, j], sibling)
                cp.start()
                sends.append(cp)
        for j, (px, py) in enumerate(chips):
            src_chip = 2 * px + py
            for w in range(n):
                passed = outs[w].at[src_chip, 1 - c]
                _remote(passed, passed, send_d2d.at[w, j], recv_d2d.at[w, j], sibling).wait_recv()
        for cp in sends:
            cp.wait_send()
        for cp in local:
            cp.wait()

    return pl.pallas_call(
        body, name="ag_weights",
        in_specs=[ANY] * n, out_specs=[ANY] * n,
        out_shape=[jax.ShapeDtypeStruct((N_SHARD,) + s.shape, s.dtype) for s in shards],
        scratch_shapes=[pltpu.SemaphoreType.DMA((n, N_CHIP_PEERS))] * 4 + [pltpu.SemaphoreType.DMA((n,))] * 2,
    )(*shards)


def _split_start(name, arrays, n_copies, plan, after=None):
    n = len(arrays)
    n_in = n + (after is not None)

    def body(*refs):
        ins, send_sem, recv_sem, token = refs[:n], refs[n_in], refs[n_in + 1], refs[-1]
        for k, (src, dst, device) in enumerate(plan(ins)):
            _remote(src, dst, send_sem.at[k], recv_sem.at[k], device).start()
        token[...] = jnp.zeros_like(token)

    outs = pl.pallas_call(
        body, name=name,
        in_specs=[HBM] * n + [ANY] * (n_in - n),
        out_specs=(SEM, SEM, *[HBM] * n, pl.BlockSpec(memory_space=pltpu.VMEM)),
        out_shape=(pltpu.SemaphoreType.DMA((n_copies,)), pltpu.SemaphoreType.DMA((n_copies,)),
                   *[pltpu.HBM(a.shape, a.dtype) for a in arrays], jax.ShapeDtypeStruct((8, LANES), F32)),
        input_output_aliases={i: i + 2 for i in range(n)},
        compiler_params=pltpu.CompilerParams(has_side_effects=EFFECT),
    )(*[pltpu.with_memory_space_constraint(a, pltpu.HBM) for a in arrays], *([] if after is None else [after]))
    return outs[0], outs[1], list(outs[2:2 + n]), outs[-1]


def _split_wait(name, send_sem, recv_sem, arrays, plan, *after):
    n = len(arrays)

    def body(*refs):
        ins, send_ref, recv_ref = refs[:n], refs[n], refs[n + 1]
        for k, (src, dst, device) in enumerate(plan(ins)):
            cp = _remote(src, dst, send_ref.at[k], recv_ref.at[k], device)
            cp.wait_send()
            cp.wait_recv()

    outs = pl.pallas_call(
        body, name=name,
        in_specs=[HBM] * n + [SEM, SEM] + [ANY] * len(after), out_specs=[HBM] * n,
        out_shape=[pltpu.HBM(a.shape, a.dtype) for a in arrays],
        input_output_aliases={i: i for i in range(n)},
        compiler_params=pltpu.CompilerParams(has_side_effects=EFFECT),
    )(*arrays, send_sem, recv_sem, *after)
    return list(outs)


def _plan_gather_send(n):
    def plan(refs):
        x, y, c, chips = _place()
        me = 2 * x + y
        copies = []
        for w in range(n):
            own, full = refs[w], refs[n + w]
            copies.append((own, full.at[me], (x, y, 1 - c)))
            copies += [(own.at[c], full.at[me, c], (px, py, c)) for px, py in chips]
        return copies
    return plan


def _plan_gather_direct(n):
    def plan(refs):
        x, y, c, chips = _place()
        me = 2 * x + y
        copies = []
        for w in range(n):
            own, full = refs[w], refs[n + w]
            copies.append((own, full.at[me], (x, y, 1 - c)))
            for px, py in chips:
                copies += [(own.at[c], full.at[me, c], (px, py, c)), (own.at[c], full.at[me, c], (px, py, 1 - c))]
        return copies
    return plan


def _plan_gather_pass(n):
    def plan(refs):
        x, y, c, chips = _place()
        copies = []
        for w in range(n):
            for px, py in chips:
                landed = refs[w].at[2 * px + py, c]
                copies.append((landed, landed, (x, y, 1 - c)))
        return copies
    return plan


def _plan_exchange(n, with_small):
    def plan(refs):
        x, y, c, chips = _place()
        me = 2 * x + y
        n_in = n + int(with_small)
        copies = []
        for w in range(n):
            copies += [(refs[w].at[2 * px + py], refs[n_in + w].at[me], (px, py, c)) for px, py in chips]
        if with_small:
            small, slots = refs[n], refs[n_in + n]
            copies.append((small, slots.at[me], (x, y, 1 - c)))
            copies += [(small, slots.at[me], (px, py, c)) for px, py in chips]
        return copies
    return plan


def _swap_halves(name, grads, wholes, after=None):
    n = len(grads)
    arrays = list(grads) + list(wholes)
    n_all = len(arrays)
    n_in = n_all + (after is not None)

    def body(*refs):
        ins, outs = refs[:n_all], refs[n_in:n_in + n_all]
        send_sem, recv_sem = refs[n_in + n_all:]
        x, y, c, _ = _place()
        sibling = (x, y, 1 - c)
        copies = []
        for w in range(n_all):
            src = ins[w].at[1 - c] if w < n else ins[w]
            cp = _remote(src, outs[w], send_sem.at[w], recv_sem.at[w], sibling)
            cp.start()
            copies.append(cp)
        for cp in copies:
            cp.wait()

    return pl.pallas_call(
        body, name=name,
        in_specs=[ANY] * n_in, out_specs=[ANY] * n_all,
        out_shape=[jax.ShapeDtypeStruct(g.shape[1:], g.dtype) for g in grads]
        + [jax.ShapeDtypeStruct(a.shape, a.dtype) for a in wholes],
        scratch_shapes=[pltpu.SemaphoreType.DMA((n_all,))] * 2,
    )(*arrays, *([] if after is None else [after]))


def _all_reduce_tile(part):
    n_dev = 8

    def body(part_ref, sum_ref, slots_ref, send_sem, recv_sem):
        x, y, c, _ = _place()
        me = 4 * x + 2 * y + c
        slots_ref[me] = part_ref[...]
        flips = [(fx, fy, fc) for fx in (0, 1) for fy in (0, 1) for fc in (0, 1)][1:]
        copies = []
        for k, (fx, fy, fc) in enumerate(flips):
            peer = (x + fx - 2 * x * fx, y + fy - 2 * y * fy, c + fc - 2 * c * fc)
            cp = _remote(part_ref, slots_ref.at[me], send_sem.at[k], recv_sem.at[k], peer)
            cp.start()
            copies.append(cp)
        for cp in copies:
            cp.wait()
        acc = slots_ref[0]
        for k in range(1, n_dev):
            acc = acc + slots_ref[k]
        sum_ref[...] = acc

    return pl.pallas_call(
        body, name="all_reduce_tile",
        in_specs=[pl.BlockSpec(memory_space=pltpu.VMEM)],
        out_specs=pl.BlockSpec(memory_space=pltpu.VMEM),
        out_shape=jax.ShapeDtypeStruct(part.shape, F32),
        scratch_shapes=[pltpu.VMEM((n_dev,) + part.shape, F32),
                        pltpu.SemaphoreType.DMA((n_dev - 1,)), pltpu.SemaphoreType.DMA((n_dev - 1,))],
    )(part)


_SMALL = ("norm1_pre_g", "b_in", "pool_scale", "sgu_ln_g", "sgu_ln_b", "w_spatial",
          "norm1_post_g", "norm2_pre_g", "norm2_post_g", "b_spatial")
_MIX = ("w_in", "w_pool", "w_sgu_proj", "w_out")
_FF = ("w_ff1", "w_ff2")
_BIG = _MIX + _FF
_ORDER = ("norm1_pre_g", "w_in", "b_in", "w_pool", "pool_scale", "sgu_ln_g", "sgu_ln_b", "w_spatial", "b_spatial",
          "w_sgu_proj", "w_out", "norm1_post_g", "norm2_pre_g", "w_ff1", "w_ff2", "norm2_post_g")


def _pack(keys, rows, parts):
    flat = jnp.concatenate([parts[k].reshape(-1).astype(F32) for k in keys])
    flat = jnp.pad(flat, (0, rows * LANES - flat.shape[0]))
    return flat.reshape(rows, LANES)


def _halves(a):
    return a.reshape(2, a.shape[0] // 2, a.shape[1])


def _step(p, m, v, x, target):
    t = x.shape[0]
    tm_mm = min(1024, t)
    tm_mix = min(256, t)
    tm_ffn = min(512, t)
    core = lax.axis_index("c").astype(jnp.int32).reshape(1)
    row = lambda a: a.reshape(1, -1)

    chip = (2 * lax.axis_index("x") + lax.axis_index("y")).astype(jnp.int32).reshape(1)

    local2d = {"w_in": p["w_in"], "w_ff1": p["w_ff1"], "w_ff2": p["w_ff2"], "w_out": p["w_out"],
               "w_pool": p["w_pool"].reshape(N_HEADS * 64, POOL_GROUP),
               "w_sgu_proj": p["w_sgu_proj"].reshape(N_HEADS * 64, HEAD)}
    shard = {k: _halves(local2d[k].astype(BF16)) for k in _BIG}
    landing = lambda keys: [lax.empty((N_SHARD,) + shard[k].shape, BF16) for k in keys]
    w_in4 = _all_gather_weights([shard["w_in"]])[0].reshape(N_SHARD, D, D_IN // N_SHARD)
    mix_keys = _MIX[1:]
    mix_arrays = [shard[k] for k in mix_keys] + landing(mix_keys)
    send0, recv0, mix_arrays, token = _split_start("ag_mix_send", mix_arrays, 7 * len(mix_keys),
                                                   _plan_gather_direct(len(mix_keys)), after=w_in4)
    ff_arrays = [shard[k] for k in _FF] + landing(_FF)
    send1, recv1, ff_arrays, token = _split_start("ag_ff_send", ff_arrays, 4 * len(_FF), _plan_gather_send(len(_FF)),
                                                  after=token)

    pos = jnp.arange(SGU_BLOCK) // 64
    mask = (pos[:, None] >= pos[None, :]).astype(F32)
    ws = (p["w_spatial"] * mask[None]).astype(BF16)
    ws_t = jnp.swapaxes(ws, 1, 2)
    bsp = p["b_spatial"].reshape(N_HEADS, SGU_BLOCK, 1)

    z = _inproj(x, row(p["norm1_pre_g"]), w_in4, row(p["b_in"]), token, tm_mm)
    mix_arrays = _split_wait("ag_mix_land", send0, recv0, mix_arrays, _plan_gather_direct(len(mix_keys)), z)
    full = dict(zip(mix_keys, mix_arrays[len(mix_keys):]))
    w_out = full["w_out"].reshape(D, D)
    regroup = lambda a: a.reshape(N_SHARD, N_HEADS, 64, 256).transpose(1, 0, 2, 3).reshape(N_HEADS, 256, 256)
    w_pool = regroup(full["w_pool"])
    w_proj = regroup(full["w_sgu_proj"])
    pooled, merged, y, h1 = _mix_fwd(z, x, w_pool, row(p["pool_scale"]), row(p["sgu_ln_g"]), row(p["sgu_ln_b"]),
                                     ws, bsp, w_proj, w_out, row(p["norm1_post_g"]), tm_mix)
    ff_arrays = _split_wait("ag_ff_land", send1, recv1, ff_arrays, _plan_gather_send(len(_FF)), pooled)
    send2, recv2, ff_full, token = _split_start("ag_ff_pass", ff_arrays[len(_FF):], 3 * len(_FF),
                                                _plan_gather_pass(len(_FF)))
    ff_full = _split_wait("ag_ff_passed", send2, recv2, ff_full, _plan_gather_pass(len(_FF)), token)
    w_ff1_4 = ff_full[0].reshape(N_SHARD, D, D)
    w_ff2 = ff_full[1].reshape(N_SHARD, D, D)
    f1, df2, dout, vec_a = _ffn_fwd(h1, row(p["norm2_pre_g"]), w_ff1_4, w_ff2, row(p["norm2_post_g"]),
                                           target, tm_ffn)

    tk = min(1024, t)
    g_big = {"w_ff2": _dw(f1, df2, "dw_ff2", 2, D, tk, N_SHARD, lambda i, j: i, _relu_sq_bf16)}
    df1, dh1, dy, vec_b = _ffn_bwd(df2, f1, w_ff1_4, w_ff2, h1, dout, y, row(p["norm2_pre_g"]),
                                   row(p["norm1_post_g"]), tm_ffn)
    g_big["w_ff1"] = _dw(h1, df1, "dw_ff1", 2, D, tk, N_SHARD, lambda i, j: j, _rms_norm_bf16,
                         row(p["norm2_pre_g"]))

    def reduce_start(name, keys, wholes):
        swapped = _swap_halves("rs_swap_" + name, [g_big[k] for k in keys], wholes)
        sums = _chip_sums(core, [g_big[k] for k in keys], swapped[:len(keys)], "chip_sums_" + name)
        arrays = list(sums)
        if wholes:
            arrays.append(_slot_sum(jnp.stack([wholes[0], swapped[-1]]), "chip_sum_small"))
        arrays += [lax.empty(s.shape, BF16) for s in sums]
        if wholes:
            arrays.append(lax.empty((N_SHARD,) + wholes[0].shape, F32))
        plan = _plan_exchange(len(keys), bool(wholes))
        send, recv, arrays, tok = _split_start("rs_send_" + name, arrays, 3 * len(keys) + 4 * len(wholes), plan)
        return send, recv, arrays, plan, tok

    def reduce_finish(name, keys, state, *after):
        send, recv, arrays, plan, _ = state
        arrays = _split_wait("rs_land_" + name, send, recv, arrays, plan, *after)
        n_in = len(arrays) // 2
        n = len(keys)
        finished = _final_sums(chip, arrays[:n], arrays[n_in:n_in + n], "final_sums_" + name)
        return dict(zip(keys, finished)), arrays

    by_half = lambda a, rows: a.reshape(N_SHARD, 2, rows // 2, a.shape[-1]).transpose(1, 0, 2, 3)
    dw_out = _dw(merged, dy, "dw_out", 1, D, tk, 1, lambda i, j: 0).reshape(D, D)
    g_big["w_out"] = by_half(dw_out, D // N_SHARD)
    early = _FF + ("w_out",)
    late = ("w_in", "w_pool", "w_sgu_proj")
    early_state = reduce_start("early", early, [])
    dz, dwp, dwpr, dws, dbsp, vec_c, dbin = _mix_bwd(z, pooled, dy, w_pool, row(p["pool_scale"]), row(p["sgu_ln_g"]),
                                                     row(p["sgu_ln_b"]), ws, ws_t, bsp, w_proj, w_out, mask, tm_mix,
                                                     early_state[4])
    g_big["w_in"] = _dw(x, dz, "dw_in", 2, D_IN // N_SHARD, tk, N_SHARD, lambda i, j: j, _rms_norm_bf16,
                        row(p["norm1_pre_g"]))
    shard_major = lambda a: a.reshape(N_HEADS, N_SHARD, 64, 256).transpose(1, 0, 2, 3).reshape(N_SHARD * 256, 256)
    g_big["w_pool"] = by_half(shard_major(dwp).astype(BF16), 256)
    g_big["w_sgu_proj"] = by_half(shard_major(dwpr).astype(BF16), 256)
    halves, _ = reduce_finish("early", early, early_state, g_big["w_in"])

    small_part = _pack(_SMALL[1:], SMALL_ROWS - 8, {
        "b_in": dbin[0], "pool_scale": vec_c[0], "sgu_ln_g": vec_c[1], "sgu_ln_b": vec_c[2],
        "w_spatial": dws, "b_spatial": dbsp[:, ::LANES].T, "norm1_post_g": vec_b[1], "norm2_pre_g": vec_b[0],
        "norm2_post_g": vec_a[0]})
    late_state = reduce_start("late", late, [small_part])
    grad_x, vec_d = _dx(dz, w_in4, x, dh1, row(p["norm1_pre_g"]), late_state[4], tm_mm)
    tiles = _all_reduce_tile(jnp.concatenate([vec_d[0].reshape(8, LANES),
                                              jnp.broadcast_to(vec_a[1:2, :LANES], (8, LANES))]))
    g1_grad, loss = tiles[:8], tiles[8, 0]

    grads, delta, new_m, new_v = {}, {}, {}, {}

    def finish(name, keys, after):
        shared = _swap_halves("rs_share_" + name, [], [halves[k] for k in keys], after)
        as2d = lambda a, k: a.reshape(local2d[k].shape)
        outs = _adamw_halves(core, [halves[k] for k in keys], shared, [local2d[k] for k in keys],
                             [as2d(m[k], k) for k in keys], [as2d(v[k], k) for k in keys], "adamw_" + name)
        for k, outs_k in zip(keys, outs):
            grads[k], delta[k], new_m[k], new_v[k] = (a.reshape(p[k].shape) for a in outs_k)

    finish("early", early, late_state[4])
    late_halves, late_arrays = reduce_finish("late", late, late_state, grad_x, *[new_v[k] for k in early])
    halves.update(late_halves)
    finish("late", late, None)
    as_rows = lambda d: [d[k].reshape(-1, LANES) for k in _SMALL]
    small_outs = _adamw_small([g1_grad, _slot_sum(late_arrays[-1], "final_sum_small")],
                              as_rows(p), as_rows(m), as_rows(v))
    for k, outs_k in zip(_SMALL, small_outs):
        grads[k], delta[k], new_m[k], new_v[k] = (a.reshape(p[k].shape) for a in outs_k)

    return (loss, grad_x, *[grads[k] for k in _ORDER], *[delta[k] for k in _ORDER],
            *[new_m[k] for k in _ORDER], *[new_v[k] for k in _ORDER])


def kernel(x, norm1_pre_g, w_in, b_in, w_pool, pool_scale, sgu_ln_g, sgu_ln_b, w_spatial, b_spatial, w_sgu_proj, w_out, norm1_post_g, norm2_pre_g, w_ff1, w_ff2, norm2_post_g, loss_target, m_norm1_pre_g, m_w_in, m_b_in, m_w_pool, m_pool_scale, m_sgu_ln_g, m_sgu_ln_b, m_w_spatial, m_b_spatial, m_w_sgu_proj, m_w_out, m_norm1_post_g, m_norm2_pre_g, m_w_ff1, m_w_ff2, m_norm2_post_g, v_norm1_pre_g, v_w_in, v_b_in, v_w_pool, v_pool_scale, v_sgu_ln_g, v_sgu_ln_b, v_w_spatial, v_b_spatial, v_w_sgu_proj, v_w_out, v_norm1_post_g, v_norm2_pre_g, v_w_ff1, v_w_ff2, v_norm2_post_g):
    p = dict(zip(_ORDER, (norm1_pre_g, w_in, b_in, w_pool, pool_scale, sgu_ln_g, sgu_ln_b, w_spatial, b_spatial,
                          w_sgu_proj, w_out, norm1_post_g, norm2_pre_g, w_ff1, w_ff2, norm2_post_g)))
    m = dict(zip(_ORDER, (m_norm1_pre_g, m_w_in, m_b_in, m_w_pool, m_pool_scale, m_sgu_ln_g, m_sgu_ln_b, m_w_spatial,
                          m_b_spatial, m_w_sgu_proj, m_w_out, m_norm1_post_g, m_norm2_pre_g, m_w_ff1, m_w_ff2,
                          m_norm2_post_g)))
    v = dict(zip(_ORDER, (v_norm1_pre_g, v_w_in, v_b_in, v_w_pool, v_pool_scale, v_sgu_ln_g, v_sgu_ln_b, v_w_spatial,
                          v_b_spatial, v_w_sgu_proj, v_w_out, v_norm1_post_g, v_norm2_pre_g, v_w_ff1, v_w_ff2,
                          v_norm2_post_g)))
    outs = _step(p, m, v, x[0], loss_target[0])
    return (outs[0], outs[1][None], *outs[2:])
```

```python
import math

import jax
import jax.numpy as jnp
from jax import lax
from jax.experimental import pallas as pl
from jax.experimental.pallas import tpu as pltpu

F32 = jnp.float32
BF16 = jnp.bfloat16

D = 1024
D_IN = 5 * D
D_FF = 4 * D
N_SHARD = 4
POOL_WINDOWS = (2, 4, 8, 16)
POOL_GROUP = 256
POOL_HALO = 16
SGU_BLOCK = 128
N_HEADS = 4
HEAD = 256
EPS = 1e-6

ADAM_LR = 0.001
ADAM_B1 = 0.9
ADAM_B2 = 0.999
ADAM_EPS = 1e-08
ADAM_WD = 0.01
ADAM_STEP = 10

V7X_VMEM_BYTES = 64 * 1024 * 1024
VMEM_LIMIT = V7X_VMEM_BYTES - 8 * 1024 * 1024
SMALL_ROWS = 616
LANES = 128

MESH = pl.DeviceIdType.MESH
ANY = pl.BlockSpec(memory_space=pl.ANY)
HBM = pl.BlockSpec(memory_space=pltpu.HBM)
SEM = pl.BlockSpec(memory_space=pltpu.SEMAPHORE)
EFFECT = pltpu.SideEffectType.DATAFLOW_SIDE_EFFECTING

NT_DIMS = (((1,), (1,)), ((), ()))
TN_DIMS = (((0,), (0,)), ((), ()))


def _params(n_axes):
    return pltpu.CompilerParams(dimension_semantics=("arbitrary",) * n_axes, vmem_limit_bytes=VMEM_LIMIT)


def _resident(shape, n_axes):
    zeros = (0,) * len(shape)
    index = (lambda i: zeros) if n_axes == 1 else (lambda i, j: zeros)
    return pl.BlockSpec(shape, index, pipeline_mode=pl.Buffered(1))


def _dot(a, b):
    return jnp.dot(a, b, preferred_element_type=F32)


def _dot_nt(a, b):
    return lax.dot_general(a, b, NT_DIMS, preferred_element_type=F32)


def _dot_tn(a, b):
    return lax.dot_general(a, b, TN_DIMS, preferred_element_type=F32)


def _mean(a):
    return jnp.mean(a, axis=-1, keepdims=True)


def _colsum(a):
    return jnp.sum(a, axis=0, keepdims=True)


_GELU_C0 = math.sqrt(2.0 / math.pi)
_GELU_C1 = 0.044715


def _gelu(a):
    t = jnp.tanh(a * (_GELU_C0 + (_GELU_C0 * _GELU_C1) * (a * a)))
    return a * (0.5 * t + 0.5)


def _gelu_and_grad(a):
    a2 = a * a
    t = jnp.tanh(a * (_GELU_C0 + (_GELU_C0 * _GELU_C1) * a2))
    cdf = 0.5 * t + 0.5
    grad = cdf + (0.5 * a) * (1.0 - t * t) * (_GELU_C0 + (3.0 * _GELU_C0 * _GELU_C1) * a2)
    return a * cdf, grad


def _sigmoid(a):
    return 0.5 * jnp.tanh(0.5 * a) + 0.5


def _rms_norm_bf16(a, g):
    return (a * lax.rsqrt(_mean(a * a) + EPS) * g).astype(BF16)


def _relu_sq_bf16(a):
    r = jnp.maximum(a, 0.0)
    return (r * r).astype(BF16)


def _window_sum(a, w, ahead):
    n = a.shape[0]
    step = 1
    while step < w:
        a = a + pltpu.roll(a, n - step if ahead else step, 0)
        step *= 2
    return a


def _rms_bwd(dn, n, r):
    return r * (dn - n * _mean(dn * n))


def _inproj_own(x, g1, w_own, b_in, chip, dep, tm):
    t = x.shape[0]
    cw = D_IN // N_SHARD

    def body(chip_ref, x_ref, g_ref, w_ref, b_ref, dep_ref, z_ref, xn_ref):
        xn = _rms_norm_bf16(x_ref[...], g_ref[...])
        xn_ref[...] = xn
        z_ref[...] = _dot(xn, w_ref[...]) + b_ref[...]

    return pl.pallas_call(
        body, name="inproj_own",
        grid_spec=pltpu.PrefetchScalarGridSpec(
            num_scalar_prefetch=1, grid=(t // tm,),
            in_specs=[pl.BlockSpec((tm, D), lambda i, chip: (i, 0)),
                      pl.BlockSpec((1, D), lambda i, chip: (0, 0)),
                      pl.BlockSpec((D, cw), lambda i, chip: (0, 0)),
                      pl.BlockSpec((1, cw), lambda i, chip: (0, chip[0])),
                      pl.BlockSpec(dep.shape, lambda i, chip: (0, 0))],
            out_specs=[pl.BlockSpec((tm, cw), lambda i, chip: (i, chip[0])),
                       pl.BlockSpec((tm, D), lambda i, chip: (i, 0))]),
        out_shape=[jax.ShapeDtypeStruct((t, D_IN), F32), jax.ShapeDtypeStruct((t, D), BF16)],
        compiler_params=_params(1),
    )(chip, x, g1, w_own, b_in, dep)


def _inproj_rest(xn, w_in4, b_in, chip, z, tm):
    t = xn.shape[0]
    cw = D_IN // N_SHARD
    shard_of = lambda s, chip: lax.rem(chip[0] + 1 + s, N_SHARD)

    def body(chip_ref, xn_ref, w_ref, b_ref, z_in_ref, z_ref):
        z_ref[...] = _dot(xn_ref[...], w_ref[shard_of(pl.program_id(1), chip_ref)]) + b_ref[...]

    return pl.pallas_call(
        body, name="inproj_rest",
        grid_spec=pltpu.PrefetchScalarGridSpec(
            num_scalar_prefetch=1, grid=(t // tm, N_SHARD - 1),
            in_specs=[pl.BlockSpec((tm, D), lambda i, s, chip: (i, 0)),
                      pl.BlockSpec((N_SHARD, D, cw), lambda i, s, chip: (0, 0, 0), pipeline_mode=pl.Buffered(1)),
                      pl.BlockSpec((1, cw), lambda i, s, chip: (0, shard_of(s, chip))),
                      ANY],
            out_specs=pl.BlockSpec((tm, cw), lambda i, s, chip: (i, shard_of(s, chip)))),
        out_shape=jax.ShapeDtypeStruct(z.shape, F32),
        input_output_aliases={4: 0},
        compiler_params=_params(2),
    )(chip, xn, w_in4, b_in, z)


def _sgu_forward(z_ref, lng_ref, lnb_ref, ws_ref, bsp_ref, sv_ref, tm, with_grad):
    if with_grad:
        u, du_dz = _gelu_and_grad(z_ref[:, D:2 * D])
        gv, dgv_dz = _gelu_and_grad(z_ref[:, 2 * D:3 * D])
    else:
        u = _gelu(z_ref[:, D:2 * D])
        gv = _gelu(z_ref[:, 2 * D:3 * D])
        du_dz = dgv_dz = None
    xc = gv - _mean(gv)
    rstd = lax.rsqrt(_mean(xc * xc) + EPS)
    vhat = xc * rstd
    vb = (vhat * lng_ref[...] + lnb_ref[...]).astype(BF16)
    for b in range(tm // SGU_BLOCK):
        rows = slice(b * SGU_BLOCK, (b + 1) * SGU_BLOCK)
        for h in range(N_HEADS):
            cols = slice(h * HEAD, (h + 1) * HEAD)
            sv_ref[rows, cols] = _dot(ws_ref[h], vb[rows, cols]) + bsp_ref[h]
    return u, du_dz, dgv_dz, vhat, rstd, vb


def _mix_fwd(z, x, w_pool, pool_scale, ln_g, ln_b, ws, bsp, w_proj, w_out, g1post, tm):
    t = x.shape[0]

    def body(z_ref, x_ref, wp_ref, psc_ref, lng_ref, lnb_ref, ws_ref, bsp_ref, wpr_ref, wo_ref, g1p_ref,
             pooled_ref, merged_ref, y_ref, h1_ref, win_ref, sv_ref):
        i = pl.program_id(0)

        @pl.when(i == 0)
        def _():
            win_ref[0:POOL_HALO, :] = jnp.zeros((POOL_HALO, D), F32)

        win_ref[POOL_HALO:POOL_HALO + tm, :] = z_ref[:, 0:D]
        pos1 = lax.broadcasted_iota(jnp.int32, (tm, 1), 0) + (i * tm + 1)
        a_parts = []
        for g, w in enumerate(POOL_WINDOWS):
            cols = slice(g * POOL_GROUP, (g + 1) * POOL_GROUP)
            ts = _window_sum(win_ref[:, cols], w, False)[POOL_HALO:POOL_HALO + tm, :]
            inv_cnt = 1.0 / jnp.minimum(pos1, w).astype(F32)
            pooled = (ts * inv_cnt - z_ref[:, cols]).astype(BF16)
            pooled_ref[:, cols] = pooled
            a_parts.append(_dot(pooled, wp_ref[g]) * psc_ref[:, cols])
        win_ref[0:POOL_HALO, :] = win_ref[tm:tm + POOL_HALO, :]

        u, _, _, _, _, _ = _sgu_forward(z_ref, lng_ref, lnb_ref, ws_ref, bsp_ref, sv_ref, tm, False)
        gated = (u * sv_ref[...]).astype(BF16)
        for h in range(N_HEADS):
            lo = h * HEAD
            bbr = _dot(gated[:, lo:lo + HEAD], wpr_ref[h])
            gate_a = _sigmoid(z_ref[:, 3 * D + lo:3 * D + lo + HEAD])
            gate_b = _sigmoid(z_ref[:, 4 * D + lo:4 * D + lo + HEAD])
            merged_ref[:, lo:lo + HEAD] = (gate_a * a_parts[h] + gate_b * bbr).astype(BF16)
        merged = merged_ref[...]
        y = _dot(merged, wo_ref[...])
        y_ref[...] = y
        r = lax.rsqrt(_mean(y * y) + EPS)
        h1_ref[...] = x_ref[...] + (y * r) * g1p_ref[...]

    row = lambda i: (i, 0)
    fixed2 = lambda i: (0, 0)
    fixed3 = lambda i: (0, 0, 0)
    vec = pl.BlockSpec((1, D), fixed2)
    return pl.pallas_call(
        body, name="mix_fwd",
        grid=(t // tm,),
        in_specs=[pl.BlockSpec((tm, D_IN), row), pl.BlockSpec((tm, D), row),
                  pl.BlockSpec((N_HEADS, POOL_GROUP, POOL_GROUP), fixed3), vec, vec, vec,
                  pl.BlockSpec((N_HEADS, SGU_BLOCK, SGU_BLOCK), fixed3),
                  pl.BlockSpec((N_HEADS, SGU_BLOCK, 1), fixed3),
                  pl.BlockSpec((N_HEADS, HEAD, HEAD), fixed3),
                  pl.BlockSpec((D, D), fixed2), vec],
        out_specs=[pl.BlockSpec((tm, D), row)] * 4,
        out_shape=[jax.ShapeDtypeStruct((t, D), BF16), jax.ShapeDtypeStruct((t, D), BF16),
                   jax.ShapeDtypeStruct((t, D), F32), jax.ShapeDtypeStruct((t, D), F32)],
        scratch_shapes=[pltpu.VMEM((tm + POOL_HALO, D), F32), pltpu.VMEM((tm, D), F32)],
        compiler_params=_params(1),
    )(z, x, w_pool, pool_scale, ln_g, ln_b, ws, bsp, w_proj, w_out, g1post)


def _ffn_fwd(h1, g2pre, w1, w2, g2post, target, tm):
    t = h1.shape[0]
    n_j = D_FF // D

    def body(h1_ref, g2_ref, w1_ref, w2_ref, g2p_ref, tgt_ref,
             hn_ref, f1_ref, f_ref, df2_ref, dout_ref, vec_ref, acc_ref):
        i, j = pl.program_id(0), pl.program_id(1)

        @pl.when((i == 0) & (j == 0))
        def _():
            vec_ref[...] = jnp.zeros_like(vec_ref)

        @pl.when(j == 0)
        def _():
            hn_ref[...] = _rms_norm_bf16(h1_ref[...], g2_ref[...])
            acc_ref[...] = jnp.zeros_like(acc_ref)

        f1 = _dot(hn_ref[...], w1_ref[j])
        f1_ref[...] = f1
        f = _relu_sq_bf16(f1)
        f_ref[...] = f
        acc_ref[...] += _dot(f, w2_ref[j])

        @pl.when(j == n_j - 1)
        def _():
            f2 = acc_ref[...]
            r = lax.rsqrt(_mean(f2 * f2) + EPS)
            n2 = f2 * r
            err = h1_ref[...] + n2 * g2p_ref[...] - tgt_ref[...]
            loss = 0.5 * jnp.sum(_mean(err * err))
            dout = err * (1.0 / D)
            dout_ref[...] = dout
            vec_ref[0:1, :] += _colsum(dout * n2)
            vec_ref[1:2, :] += jnp.full((1, D), loss, F32)
            df2_ref[...] = _rms_bwd(dout * g2p_ref[...], n2, r).astype(BF16)

    row = lambda i, j: (i, 0)
    fixed = lambda i, j: (0, 0)
    vec = pl.BlockSpec((1, D), fixed)
    return pl.pallas_call(
        body, name="ffn_fwd",
        grid=(t // tm, n_j),
        in_specs=[pl.BlockSpec((tm, D), row), vec,
                  _resident((n_j, D, D), 2), _resident((n_j, D, D), 2), vec,
                  pl.BlockSpec((tm, D), row)],
        out_specs=[pl.BlockSpec((tm, D), row),
                   pl.BlockSpec((tm, D), lambda i, j: (i, j)),
                   pl.BlockSpec((tm, D), lambda i, j: (i, j)),
                   pl.BlockSpec((tm, D), row), pl.BlockSpec((tm, D), row),
                   pl.BlockSpec((8, D), fixed)],
        out_shape=[jax.ShapeDtypeStruct((t, D), BF16), jax.ShapeDtypeStruct((t, D_FF), F32),
                   jax.ShapeDtypeStruct((t, D_FF), BF16), jax.ShapeDtypeStruct((t, D), BF16),
                   jax.ShapeDtypeStruct((t, D), F32), jax.ShapeDtypeStruct((8, D), F32)],
        scratch_shapes=[pltpu.VMEM((tm, D), F32)],
        compiler_params=_params(2),
    )(h1, g2pre, w1, w2, g2post, target)


def _ffn_bwd(df2, f1, w1, w2, h1, dout, y, g2pre, g1post, tm):
    t = h1.shape[0]
    n_j = D_FF // D

    def body(df2_ref, f1_ref, w1_ref, w2_ref, h1_ref, dout_ref, y_ref, g2_ref, g1p_ref,
             df1_ref, dh1_ref, dy_ref, vec_ref, acc_ref):
        i, j = pl.program_id(0), pl.program_id(1)

        @pl.when((i == 0) & (j == 0))
        def _():
            vec_ref[...] = jnp.zeros_like(vec_ref)

        @pl.when(j == 0)
        def _():
            acc_ref[...] = jnp.zeros_like(acc_ref)

        df = _dot_nt(df2_ref[...], w2_ref[j])
        df1 = (df * (2.0 * jnp.maximum(f1_ref[...], 0.0))).astype(BF16)
        df1_ref[...] = df1
        acc_ref[...] += _dot_nt(df1, w1_ref[j])

        @pl.when(j == n_j - 1)
        def _():
            dhn = acc_ref[...]
            h = h1_ref[...]
            r2 = lax.rsqrt(_mean(h * h) + EPS)
            nh = h * r2
            vec_ref[0:1, :] += _colsum(dhn * nh)
            dh1 = dout_ref[...] + _rms_bwd(dhn * g2_ref[...], nh, r2)
            dh1_ref[...] = dh1
            yv = y_ref[...]
            r1 = lax.rsqrt(_mean(yv * yv) + EPS)
            ny = yv * r1
            vec_ref[1:2, :] += _colsum(dh1 * ny)
            dy_ref[...] = _rms_bwd(dh1 * g1p_ref[...], ny, r1).astype(BF16)

    row = lambda i, j: (i, 0)
    fixed = lambda i, j: (0, 0)
    vec = pl.BlockSpec((1, D), fixed)
    return pl.pallas_call(
        body, name="ffn_bwd",
        grid=(t // tm, n_j),
        in_specs=[pl.BlockSpec((tm, D), row),
                  pl.BlockSpec((tm, D), lambda i, j: (i, j)),
                  _resident((n_j, D, D), 2), _resident((n_j, D, D), 2),
                  pl.BlockSpec((tm, D), row), pl.BlockSpec((tm, D), row), pl.BlockSpec((tm, D), row), vec, vec],
        out_specs=[pl.BlockSpec((tm, D), lambda i, j: (i, j)),
                   pl.BlockSpec((tm, D), row), pl.BlockSpec((tm, D), row),
                   pl.BlockSpec((8, D), fixed)],
        out_shape=[jax.ShapeDtypeStruct((t, D_FF), BF16), jax.ShapeDtypeStruct((t, D), F32),
                   jax.ShapeDtypeStruct((t, D), BF16), jax.ShapeDtypeStruct((8, D), F32)],
        scratch_shapes=[pltpu.VMEM((tm, D), F32)],
        compiler_params=_params(2),
    )(df2, f1, w1, w2, h1, dout, y, g2pre, g1post)


def _mix_bwd(z, pooled, dy, w_pool, pool_scale, ln_g, ln_b, ws, ws_t, bsp, w_proj, w_out, mask, tm, dep):
    t = z.shape[0]
    n_t = t // tm

    def body(z_ref, pooled_ref, dy_ref, wp_ref, psc_ref, lng_ref, lnb_ref, ws_ref, wst_ref, bsp_ref, wpr_ref,
             wo_ref, mask_ref, dep_ref,
             dz_ref, dwp_ref, dwpr_ref, dws_ref, dbsp_ref, vec_ref, dbin_ref,
             win_ref, sv_ref, dsv_ref, dv_ref, dsvsum_ref):
        i = pl.program_id(0)

        @pl.when(i == 0)
        def _():
            win_ref[tm:tm + POOL_HALO, :] = jnp.zeros((POOL_HALO, D), F32)
            dwp_ref[...] = jnp.zeros_like(dwp_ref)
            dwpr_ref[...] = jnp.zeros_like(dwpr_ref)
            dws_ref[...] = jnp.zeros_like(dws_ref)
            vec_ref[...] = jnp.zeros_like(vec_ref)
            dbin_ref[...] = jnp.zeros_like(dbin_ref)
            dsvsum_ref[...] = jnp.zeros_like(dsvsum_ref)

        dmerged = _dot_nt(dy_ref[...], wo_ref[...])

        u, du_dz, dgv_dz, vhat, rstd, vb = _sgu_forward(z_ref, lng_ref, lnb_ref, ws_ref, bsp_ref, sv_ref, tm, True)
        sv = sv_ref[...]
        gated = (u * sv).astype(BF16)

        sa = _sigmoid(z_ref[:, 3 * D:4 * D])
        sb = _sigmoid(z_ref[:, 4 * D:5 * D])
        da = dmerged * sa
        dbbr = (dmerged * sb).astype(BF16)

        tile = n_t - 1 - i
        pos1 = lax.broadcasted_iota(jnp.int32, (tm, 1), 0) + (tile * tm + 1)
        def put(k, lo, part):
            cols = slice(k * D + lo, k * D + lo + part.shape[1])
            dz_ref[:, cols] = part.astype(BF16)
            dbin_ref[0:1, cols] += _colsum(part)

        dpooled_parts = []
        for g, w in enumerate(POOL_WINDOWS):
            cols = slice(g * POOL_GROUP, (g + 1) * POOL_GROUP)
            pg = pooled_ref[:, cols]
            a_pre = _dot(pg, wp_ref[g])
            da_g = da[:, cols]
            psc = psc_ref[:, cols]
            vec_ref[0:1, cols] += _colsum(da_g * a_pre)
            sa_g = sa[:, cols]
            put(3, g * POOL_GROUP, dmerged[:, cols] * (a_pre * psc) * (sa_g * (1.0 - sa_g)))
            da_pre = (da_g * psc).astype(BF16)
            dwp_ref[g] += _dot_tn(pg, da_pre)
            dpooled = _dot_nt(da_pre, wp_ref[g])
            dpooled_parts.append(dpooled)
            inv_cnt = 1.0 / jnp.minimum(pos1, w).astype(F32)
            win_ref[0:tm, cols] = dpooled * inv_cnt
        for g, w in enumerate(POOL_WINDOWS):
            cols = slice(g * POOL_GROUP, (g + 1) * POOL_GROUP)
            acc = _window_sum(win_ref[:, cols], w, True)[0:tm, :]
            put(0, g * POOL_GROUP, acc - dpooled_parts[g])
        win_ref[tm:tm + POOL_HALO, :] = win_ref[0:POOL_HALO, :]

        for h in range(N_HEADS):
            cols = slice(h * HEAD, (h + 1) * HEAD)
            g_h = gated[:, cols]
            db_h = dbbr[:, cols]
            bbr_h = _dot(g_h, wpr_ref[h])
            sb_h = sb[:, cols]
            put(4, h * HEAD, dmerged[:, cols] * bbr_h * (sb_h * (1.0 - sb_h)))
            dwpr_ref[h] += _dot_tn(g_h, db_h)
            dgated_h = _dot_nt(db_h, wpr_ref[h])
            dsv_ref[:, cols] = dgated_h * u[:, cols]
            sv_ref[:, cols] = dgated_h * sv[:, cols]
        put(1, 0, sv_ref[...] * du_dz)
        dsv = dsv_ref[...]
        dsvb = dsv.astype(BF16)
        blk_sum = dsv[0:SGU_BLOCK, :]
        for b in range(1, tm // SGU_BLOCK):
            blk_sum = blk_sum + dsv[b * SGU_BLOCK:(b + 1) * SGU_BLOCK, :]
        dsvsum_ref[...] += blk_sum
        for b in range(tm // SGU_BLOCK):
            rows = slice(b * SGU_BLOCK, (b + 1) * SGU_BLOCK)
            for h in range(N_HEADS):
                cols = slice(h * HEAD, (h + 1) * HEAD)
                dws_ref[h] += _dot_nt(dsvb[rows, cols], vb[rows, cols])
                dv_ref[rows, cols] = _dot(wst_ref[h], dsvb[rows, cols])
        dv = dv_ref[...]
        vec_ref[1:2, :] += _colsum(dv * vhat)
        vec_ref[2:3, :] += _colsum(dv)
        dvhat = dv * lng_ref[...]
        dgv = rstd * (dvhat - _mean(dvhat) - vhat * _mean(dvhat * vhat))
        put(2, 0, dgv * dgv_dz)

        @pl.when(i == n_t - 1)
        def _():
            for h in range(N_HEADS):
                dws_ref[h] = jnp.where(mask_ref[...] > 0.0, dws_ref[h], 0.0)
                tot = jnp.sum(dsvsum_ref[:, h * HEAD:(h + 1) * HEAD], axis=1, keepdims=True)
                dbsp_ref[:, h * LANES:(h + 1) * LANES] = jnp.broadcast_to(tot, (SGU_BLOCK, LANES))

    row = lambda i: (n_t - 1 - i, 0)
    fixed2 = lambda i: (0, 0)
    fixed3 = lambda i: (0, 0, 0)
    vec = pl.BlockSpec((1, D), fixed2)
    sq = pl.BlockSpec((N_HEADS, SGU_BLOCK, SGU_BLOCK), fixed3)
    grp = pl.BlockSpec((N_HEADS, HEAD, HEAD), fixed3)
    return pl.pallas_call(
        body, name="mix_bwd",
        grid=(n_t,),
        in_specs=[pl.BlockSpec((tm, D_IN), row), pl.BlockSpec((tm, D), row), pl.BlockSpec((tm, D), row),
                  grp, vec, vec, vec, sq, sq, pl.BlockSpec((N_HEADS, SGU_BLOCK, 1), fixed3), grp,
                  pl.BlockSpec((D, D), fixed2), pl.BlockSpec((SGU_BLOCK, SGU_BLOCK), fixed2),
                  pl.BlockSpec(dep.shape, fixed2)],
        out_specs=[pl.BlockSpec((tm, D_IN), row), grp, grp, sq,
                   pl.BlockSpec((SGU_BLOCK, N_HEADS * LANES), fixed2),
                   pl.BlockSpec((8, D), fixed2), pl.BlockSpec((8, D_IN), fixed2)],
        out_shape=[jax.ShapeDtypeStruct((t, D_IN), BF16),
                   jax.ShapeDtypeStruct((N_HEADS, HEAD, HEAD), F32), jax.ShapeDtypeStruct((N_HEADS, HEAD, HEAD), F32),
                   jax.ShapeDtypeStruct((N_HEADS, SGU_BLOCK, SGU_BLOCK), F32),
                   jax.ShapeDtypeStruct((SGU_BLOCK, N_HEADS * LANES), F32),
                   jax.ShapeDtypeStruct((8, D), F32), jax.ShapeDtypeStruct((8, D_IN), F32)],
        scratch_shapes=[pltpu.VMEM((tm + POOL_HALO, D), F32), pltpu.VMEM((tm, D), F32), pltpu.VMEM((tm, D), F32),
                        pltpu.VMEM((tm, D), F32), pltpu.VMEM((SGU_BLOCK, D), F32)],
        compiler_params=_params(1),
    )(z, pooled, dy, w_pool, pool_scale, ln_g, ln_b, ws, ws_t, bsp, w_proj, w_out, mask, dep)


def _dx(dz, w_in4, x, dh1, g1, dep, tm):
    t = x.shape[0]
    cw = D_IN // N_SHARD

    def body(dz_ref, w_ref, x_ref, dh1_ref, g_ref, dep_ref, gx_ref, vec_ref, acc_ref):
        i, s = pl.program_id(0), pl.program_id(1)

        @pl.when((i == 0) & (s == 0))
        def _():
            vec_ref[...] = jnp.zeros_like(vec_ref)

        @pl.when(s == 0)
        def _():
            acc_ref[...] = jnp.zeros_like(acc_ref)

        acc_ref[...] += _dot_nt(dz_ref[...], w_ref[s])

        @pl.when(s == N_SHARD - 1)
        def _():
            dxn = acc_ref[...]
            xv = x_ref[...]
            r = lax.rsqrt(_mean(xv * xv) + EPS)
            xh = xv * r
            vec_ref[0:1, :] += _colsum(dxn * xh)
            gx_ref[...] = dh1_ref[...] + _rms_bwd(dxn * g_ref[...], xh, r)

    row = lambda i, s: (i, 0)
    fixed = lambda i, s: (0, 0)
    return pl.pallas_call(
        body, name="dx",
        grid=(t // tm, N_SHARD),
        in_specs=[pl.BlockSpec((tm, cw), lambda i, s: (i, s)),
                  _resident((N_SHARD, D, cw), 2),
                  pl.BlockSpec((tm, D), row), pl.BlockSpec((tm, D), row), pl.BlockSpec((1, D), fixed),
                  pl.BlockSpec(dep.shape, fixed)],
        out_specs=[pl.BlockSpec((tm, D), row), pl.BlockSpec((8, D), fixed)],
        out_shape=[jax.ShapeDtypeStruct((t, D), F32), jax.ShapeDtypeStruct((8, D), F32)],
        scratch_shapes=[pltpu.VMEM((tm, D), F32)],
        compiler_params=_params(2),
    )(dz, w_in4, x, dh1, g1, dep)


def _dw(a, b, name, n_split, tn, tk, n_out, out_block):
    t, m = a.shape
    n = b.shape[1]
    tm = min(m, D)
    rows = tm // n_split
    n_k = t // tk

    def body(a_ref, b_ref, o_ref, acc_ref):
        k = pl.program_id(2)

        @pl.when(k == 0)
        def _():
            acc_ref[...] = jnp.zeros_like(acc_ref)

        acc_ref[...] += _dot_tn(a_ref[...], b_ref[...])

        @pl.when(k == n_k - 1)
        def _():
            for h in range(n_split):
                o_ref[h, 0] = acc_ref[h * rows:(h + 1) * rows, :].astype(BF16)

    return pl.pallas_call(
        body, name=name,
        grid=(m // tm, n // tn, n_k),
        in_specs=[pl.BlockSpec((tk, tm), lambda i, j, k: (k, i)),
                  pl.BlockSpec((tk, tn), lambda i, j, k: (k, j))],
        out_specs=pl.BlockSpec((n_split, 1, rows, tn), lambda i, j, k: (0, out_block(i, j), 0, 0)),
        out_shape=jax.ShapeDtypeStruct((n_split, n_out, rows, tn), BF16),
        scratch_shapes=[pltpu.VMEM((tm, tn), F32)],
        compiler_params=_params(3),
    )(a, b)


def _row_block(rows, cols, itemsize, n_bufs):
    budget = VMEM_LIMIT // 4
    rb = rows
    while rb % 16 == 0 and rb * cols * itemsize * n_bufs * 2 > budget:
        rb //= 2
    return rb


ROW_STEPS = 4


def _chip_sums(core, grads, recvs, name):
    n = len(grads)

    def body(c_ref, *refs):
        for w in range(n):
            refs[2 * n + w][...] = (refs[w][0].astype(F32) + refs[n + w][...].astype(F32)).astype(BF16)

    def blocks(g):
        _, _, rh, cols = g.shape
        rb = rh // ROW_STEPS
        return (pl.BlockSpec((1, 1, rb, cols), lambda s, r, c: (c[0], s, r, 0)),
                pl.BlockSpec((1, rb, cols), lambda s, r, c: (s, r, 0)))

    specs = [blocks(g) for g in grads]
    return pl.pallas_call(
        body, name=name,
        grid_spec=pltpu.PrefetchScalarGridSpec(
            num_scalar_prefetch=1, grid=(N_SHARD, ROW_STEPS),
            in_specs=[s[0] for s in specs] + [s[1] for s in specs],
            out_specs=[s[1] for s in specs]),
        out_shape=[jax.ShapeDtypeStruct(g.shape[1:], BF16) for g in grads],
        compiler_params=_params(2),
    )(core, *grads, *recvs)


def _slot_sum(slots, name):
    n_s, rows, cols = slots.shape
    rb = _row_block(rows, cols, 4, n_s + 1)

    def body(s_ref, o_ref):
        acc = s_ref[0].astype(F32)
        for k in range(1, n_s):
            acc = acc + s_ref[k].astype(F32)
        o_ref[...] = acc

    return pl.pallas_call(
        body, name=name,
        grid=(rows // rb,),
        in_specs=[pl.BlockSpec((n_s, rb, cols), lambda r: (0, r, 0))],
        out_specs=pl.BlockSpec((rb, cols), lambda r: (r, 0)),
        out_shape=jax.ShapeDtypeStruct((rows, cols), F32),
        compiler_params=_params(1),
    )(slots)


def _adamw_update(g, w, m, v):
    m_new = ADAM_B1 * m + (1.0 - ADAM_B1) * g
    v_new = ADAM_B2 * v + (1.0 - ADAM_B2) * (g * g)
    m_hat = m_new / (1.0 - ADAM_B1 ** ADAM_STEP)
    v_hat = v_new / (1.0 - ADAM_B2 ** ADAM_STEP)
    return -ADAM_LR * (m_hat / (jnp.sqrt(v_hat) + ADAM_EPS) + ADAM_WD * w), m_new, v_new


def _adamw_small(grad_parts, ws, ms, vs):
    n, n_g = len(ws), len(grad_parts)

    def body(*refs):
        g_refs, ins, outs = refs[:n_g], refs[n_g:n_g + 3 * n], refs[n_g + 3 * n:]
        part, off = 0, 0
        for k in range(n):
            rows = ws[k].shape[0]
            if off == grad_parts[part].shape[0]:
                part, off = part + 1, 0
            g = g_refs[part][off:off + rows, :]
            off += rows
            delta, m_new, v_new = _adamw_update(g, ins[k][...], ins[n + k][...], ins[2 * n + k][...])
            for o, val in zip(outs[4 * k:4 * k + 4], (g, delta, m_new, v_new)):
                o[...] = val

    vmem = pl.BlockSpec(memory_space=pltpu.VMEM)
    outs = pl.pallas_call(
        body, name="adamw_small",
        in_specs=[vmem] * (n_g + 3 * n), out_specs=[vmem] * (4 * n),
        out_shape=[jax.ShapeDtypeStruct(w.shape, F32) for w in ws for _ in range(4)],
    )(*grad_parts, *ws, *ms, *vs)
    return [outs[4 * k:4 * k + 4] for k in range(n)]


def _final_sums(chip, owns, slots, name):
    n = len(owns)

    def body(chip_ref, *refs):
        for w in range(n):
            own_ref, s1_ref, s2_ref, s3_ref = refs[4 * w:4 * w + 4]
            acc = own_ref[0].astype(F32) + s1_ref[0].astype(F32)
            acc = acc + s2_ref[0].astype(F32)
            refs[4 * n + w][...] = acc + s3_ref[0].astype(F32)

    def slot(a, k):
        n_s, rh, cols = a.shape
        return pl.BlockSpec((1, rh // ROW_STEPS, cols), lambda r, chip: (lax.rem(chip[0] + k, n_s), r, 0))

    operands, in_specs = [], []
    for own, slots_w in zip(owns, slots):
        operands += [own, slots_w, slots_w, slots_w]
        in_specs += [slot(own, k) for k in range(4)]
    return pl.pallas_call(
        body, name=name,
        grid_spec=pltpu.PrefetchScalarGridSpec(
            num_scalar_prefetch=1, grid=(ROW_STEPS,),
            in_specs=in_specs,
            out_specs=[pl.BlockSpec((a.shape[1] // ROW_STEPS, a.shape[2]), lambda r, chip: (r, 0)) for a in owns]),
        out_shape=[jax.ShapeDtypeStruct(a.shape[1:], F32) for a in owns],
        compiler_params=_params(1),
    )(chip, *operands)


def _adamw_halves(core, owns, siblings, ws, ms, vs, name):
    n = len(ws)

    def body(c_ref, *refs):
        ins, outs = refs[:5 * n], refs[5 * n:]
        mine = c_ref[0] == pl.program_id(0)
        for k in range(n):
            own_ref, sib_ref, w_ref, m_ref, v_ref = ins[5 * k:5 * k + 5]
            g_ref, d_ref, mo_ref, vo_ref = outs[4 * k:4 * k + 4]

            @pl.when(mine)
            def _():
                g_ref[...] = own_ref[...]

            @pl.when(jnp.logical_not(mine))
            def _():
                g_ref[...] = sib_ref[...]

            d_ref[...], mo_ref[...], vo_ref[...] = _adamw_update(g_ref[...], w_ref[...], m_ref[...], v_ref[...])

    operands, in_specs, out_specs = [], [], []
    for own, sib, w, m, v in zip(owns, siblings, ws, ms, vs):
        rows, cols = w.shape
        rb = rows // 2 // ROW_STEPS
        half = pl.BlockSpec((rb, cols), lambda h, r, c: (r, 0))
        whole = pl.BlockSpec((rb, cols), lambda h, r, c: (h * ROW_STEPS + r, 0))
        operands += [own, sib, w, m, v]
        in_specs += [half, half, whole, whole, whole]
        out_specs += [whole] * 4
    outs = pl.pallas_call(
        body, name=name,
        grid_spec=pltpu.PrefetchScalarGridSpec(
            num_scalar_prefetch=1, grid=(2, ROW_STEPS), in_specs=in_specs, out_specs=out_specs),
        out_shape=[jax.ShapeDtypeStruct(w.shape, F32) for w in ws for _ in range(4)],
        compiler_params=_params(2),
    )(core, *operands)
    return [outs[4 * k:4 * k + 4] for k in range(n)]


def _place():
    x, y, c = lax.axis_index("x"), lax.axis_index("y"), lax.axis_index("c")
    chips = [(1 - x, y), (x, 1 - y), (1 - x, 1 - y)]
    return x, y, c, chips


def _remote(src, dst, send_sem, recv_sem, device):
    return pltpu.make_async_remote_copy(src_ref=src, dst_ref=dst, send_sem=send_sem, recv_sem=recv_sem,
                                        device_id=device, device_id_type=MESH)


def _split_start(name, arrays, n_copies, plan, after=None):
    n = len(arrays)
    n_in = n + (after is not None)

    def body(*refs):
        ins, send_sem, recv_sem, token = refs[:n], refs[n_in], refs[n_in + 1], refs[-1]
        for k, (src, dst, device) in enumerate(plan(ins)):
            _remote(src, dst, send_sem.at[k], recv_sem.at[k], device).start()
        token[...] = jnp.zeros_like(token)

    outs = pl.pallas_call(
        body, name=name,
        in_specs=[HBM] * n + [ANY] * (n_in - n),
        out_specs=(SEM, SEM, *[HBM] * n, pl.BlockSpec(memory_space=pltpu.VMEM)),
        out_shape=(pltpu.SemaphoreType.DMA((n_copies,)), pltpu.SemaphoreType.DMA((n_copies,)),
                   *[pltpu.HBM(a.shape, a.dtype) for a in arrays], jax.ShapeDtypeStruct((8, LANES), F32)),
        input_output_aliases={i: i + 2 for i in range(n)},
        compiler_params=pltpu.CompilerParams(has_side_effects=EFFECT),
    )(*[pltpu.with_memory_space_constraint(a, pltpu.HBM) for a in arrays], *([] if after is None else [after]))
    return outs[0], outs[1], list(outs[2:2 + n]), outs[-1]


def _split_wait(name, send_sem, recv_sem, arrays, plan, *after):
    n = len(arrays)

    def body(*refs):
        ins, send_ref, recv_ref = refs[:n], refs[n], refs[n + 1]
        for k, (src, dst, device) in enumerate(plan(ins)):
            cp = _remote(src, dst, send_ref.at[k], recv_ref.at[k], device)
            cp.wait_send()
            cp.wait_recv()

    outs = pl.pallas_call(
        body, name=name,
        in_specs=[HBM] * n + [SEM, SEM] + [ANY] * len(after), out_specs=[HBM] * n,
        out_shape=[pltpu.HBM(a.shape, a.dtype) for a in arrays],
        input_output_aliases={i: i for i in range(n)},
        compiler_params=pltpu.CompilerParams(has_side_effects=EFFECT),
    )(*arrays, send_sem, recv_sem, *after)
    return list(outs)


def _plan_gather_send(n):
    def plan(refs):
        x, y, c, chips = _place()
        me = 2 * x + y
        copies = []
        for w in range(n):
            own, full = refs[w], refs[n + w]
            copies.append((own, full.at[me], (x, y, 1 - c)))
            copies += [(own.at[c], full.at[me, c], (px, py, c)) for px, py in chips]
        return copies
    return plan


def _plan_gather_direct(n):
    def plan(refs):
        x, y, c, chips = _place()
        me = 2 * x + y
        copies = []
        for w in range(n):
            own, full = refs[w], refs[n + w]
            copies.append((own, full.at[me], (x, y, 1 - c)))
            for px, py in chips:
                copies += [(own.at[c], full.at[me, c], (px, py, c)), (own.at[c], full.at[me, c], (px, py, 1 - c))]
        return copies
    return plan


def _plan_gather_pass(n):
    def plan(refs):
        x, y, c, chips = _place()
        copies = []
        for w in range(n):
            for px, py in chips:
                landed = refs[w].at[2 * px + py, c]
                copies.append((landed, landed, (x, y, 1 - c)))
        return copies
    return plan


def _plan_share(n):
    def plan(refs):
        x, y, c, _ = _place()
        return [(refs[w], refs[n + w], (x, y, 1 - c)) for w in range(n)]
    return plan


def _plan_exchange(n, with_small):
    def plan(refs):
        x, y, c, chips = _place()
        me = 2 * x + y
        n_in = n + int(with_small)
        copies = []
        for w in range(n):
            copies += [(refs[w].at[2 * px + py], refs[n_in + w].at[me], (px, py, c)) for px, py in chips]
        if with_small:
            small, slots = refs[n], refs[n_in + n]
            copies.append((small, slots.at[me], (x, y, 1 - c)))
            copies += [(small, slots.at[me], (px, py, c)) for px, py in chips]
        return copies
    return plan


def _swap_halves(name, grads, wholes):
    n = len(grads)
    arrays = list(grads) + list(wholes)
    n_all = len(arrays)

    def body(*refs):
        ins, outs = refs[:n_all], refs[n_all:2 * n_all]
        send_sem, recv_sem = refs[2 * n_all:]
        x, y, c, _ = _place()
        sibling = (x, y, 1 - c)
        copies = []
        for w in range(n_all):
            src = ins[w].at[1 - c] if w < n else ins[w]
            cp = _remote(src, outs[w], send_sem.at[w], recv_sem.at[w], sibling)
            cp.start()
            copies.append(cp)
        for cp in copies:
            cp.wait()

    return pl.pallas_call(
        body, name=name,
        in_specs=[ANY] * n_all, out_specs=[ANY] * n_all,
        out_shape=[jax.ShapeDtypeStruct(g.shape[1:], g.dtype) for g in grads]
        + [jax.ShapeDtypeStruct(a.shape, a.dtype) for a in wholes],
        scratch_shapes=[pltpu.SemaphoreType.DMA((n_all,))] * 2,
    )(*arrays)


def _all_reduce_tile(part):
    n_dev = 8

    def body(part_ref, sum_ref, slots_ref, send_sem, recv_sem):
        x, y, c, _ = _place()
        me = 4 * x + 2 * y + c
        slots_ref[me] = part_ref[...]
        flips = [(fx, fy, fc) for fx in (0, 1) for fy in (0, 1) for fc in (0, 1)][1:]
        copies = []
        for k, (fx, fy, fc) in enumerate(flips):
            peer = (x + fx - 2 * x * fx, y + fy - 2 * y * fy, c + fc - 2 * c * fc)
            cp = _remote(part_ref, slots_ref.at[me], send_sem.at[k], recv_sem.at[k], peer)
            cp.start()
            copies.append(cp)
        for cp in copies:
            cp.wait()
        acc = slots_ref[0]
        for k in range(1, n_dev):
            acc = acc + slots_ref[k]
        sum_ref[...] = acc

    return pl.pallas_call(
        body, name="all_reduce_tile",
        in_specs=[pl.BlockSpec(memory_space=pltpu.VMEM)],
        out_specs=pl.BlockSpec(memory_space=pltpu.VMEM),
        out_shape=jax.ShapeDtypeStruct(part.shape, F32),
        scratch_shapes=[pltpu.VMEM((n_dev,) + part.shape, F32),
                        pltpu.SemaphoreType.DMA((n_dev - 1,)), pltpu.SemaphoreType.DMA((n_dev - 1,))],
    )(part)


_SMALL = ("norm1_pre_g", "b_in", "pool_scale", "sgu_ln_g", "sgu_ln_b", "w_spatial",
          "norm1_post_g", "norm2_pre_g", "norm2_post_g", "b_spatial")
_MIX = ("w_in", "w_pool", "w_sgu_proj", "w_out")
_FF = ("w_ff1", "w_ff2")
_BIG = _MIX + _FF
_ORDER = ("norm1_pre_g", "w_in", "b_in", "w_pool", "pool_scale", "sgu_ln_g", "sgu_ln_b", "w_spatial", "b_spatial",
          "w_sgu_proj", "w_out", "norm1_post_g", "norm2_pre_g", "w_ff1", "w_ff2", "norm2_post_g")


def _pack(keys, rows, parts):
    flat = jnp.concatenate([parts[k].reshape(-1).astype(F32) for k in keys])
    flat = jnp.pad(flat, (0, rows * LANES - flat.shape[0]))
    return flat.reshape(rows, LANES)


def _halves(a):
    return a.reshape(2, a.shape[0] // 2, a.shape[1])


def _step(p, m, v, x, target):
    t = x.shape[0]
    tm_mm = min(1024, t)
    tm_mix = min(256, t)
    tm_ffn = min(512, t)
    core = lax.axis_index("c").astype(jnp.int32).reshape(1)
    row = lambda a: a.reshape(1, -1)

    chip = (2 * lax.axis_index("x") + lax.axis_index("y")).astype(jnp.int32).reshape(1)

    local2d = {"w_in": p["w_in"], "w_ff1": p["w_ff1"], "w_ff2": p["w_ff2"], "w_out": p["w_out"],
               "w_pool": p["w_pool"].reshape(N_HEADS * 64, POOL_GROUP),
               "w_sgu_proj": p["w_sgu_proj"].reshape(N_HEADS * 64, HEAD)}
    shard = {k: _halves(local2d[k].astype(BF16)) for k in _BIG}
    landing = lambda keys: [lax.empty((N_SHARD,) + shard[k].shape, BF16) for k in keys]
    in_arrays = [shard["w_in"]] + landing(["w_in"])
    send_a, recv_a, in_arrays, token = _split_start("ag_in_send", in_arrays, 4, _plan_gather_send(1))
    z, xn = _inproj_own(x, row(p["norm1_pre_g"]), shard["w_in"].reshape(D, D_IN // N_SHARD), row(p["b_in"]), chip,
                        token, tm_mm)
    in_arrays = _split_wait("ag_in_land", send_a, recv_a, in_arrays, _plan_gather_send(1), z)
    send_b, recv_b, in_full, token = _split_start("ag_in_pass", in_arrays[1:], 3, _plan_gather_pass(1))
    mix_keys = _MIX[1:]
    mix_arrays = [shard[k] for k in mix_keys] + landing(mix_keys)
    send0, recv0, mix_arrays, token = _split_start("ag_mix_send", mix_arrays, 7 * len(mix_keys),
                                                   _plan_gather_direct(len(mix_keys)), after=token)
    ff_arrays = [shard[k] for k in _FF] + landing(_FF)
    send1, recv1, ff_arrays, token = _split_start("ag_ff_send", ff_arrays, 4 * len(_FF), _plan_gather_send(len(_FF)),
                                                  after=token)
    in_full = _split_wait("ag_in_passed", send_b, recv_b, in_full, _plan_gather_pass(1), token)
    w_in4 = in_full[0].reshape(N_SHARD, D, D_IN // N_SHARD)

    pos = jnp.arange(SGU_BLOCK) // 64
    mask = (pos[:, None] >= pos[None, :]).astype(F32)
    ws = (p["w_spatial"] * mask[None]).astype(BF16)
    ws_t = jnp.swapaxes(ws, 1, 2)
    bsp = p["b_spatial"].reshape(N_HEADS, SGU_BLOCK, 1)

    z = _inproj_rest(xn, w_in4, row(p["b_in"]), chip, z, tm_mm)
    mix_arrays = _split_wait("ag_mix_land", send0, recv0, mix_arrays, _plan_gather_direct(len(mix_keys)), z)
    full = dict(zip(mix_keys, mix_arrays[len(mix_keys):]))
    w_out = full["w_out"].reshape(D, D)
    regroup = lambda a: a.reshape(N_SHARD, N_HEADS, 64, 256).transpose(1, 0, 2, 3).reshape(N_HEADS, 256, 256)
    w_pool = regroup(full["w_pool"])
    w_proj = regroup(full["w_sgu_proj"])
    pooled, merged, y, h1 = _mix_fwd(z, x, w_pool, row(p["pool_scale"]), row(p["sgu_ln_g"]), row(p["sgu_ln_b"]),
                                     ws, bsp, w_proj, w_out, row(p["norm1_post_g"]), tm_mix)
    ff_arrays = _split_wait("ag_ff_land", send1, recv1, ff_arrays, _plan_gather_send(len(_FF)), pooled)
    send2, recv2, ff_full, token = _split_start("ag_ff_pass", ff_arrays[len(_FF):], 3 * len(_FF),
                                                _plan_gather_pass(len(_FF)))
    ff_full = _split_wait("ag_ff_passed", send2, recv2, ff_full, _plan_gather_pass(len(_FF)), token)
    w_ff1_4 = ff_full[0].reshape(N_SHARD, D, D)
    w_ff2 = ff_full[1].reshape(N_SHARD, D, D)
    hn, f1, f, df2, dout, vec_a = _ffn_fwd(h1, row(p["norm2_pre_g"]), w_ff1_4, w_ff2, row(p["norm2_post_g"]),
                                           target, tm_ffn)

    tk = min(1024, t)
    g_big = {"w_ff2": _dw(f, df2, "dw_ff2", 2, D, tk, N_SHARD, lambda i, j: i)}
    df1, dh1, dy, vec_b = _ffn_bwd(df2, f1, w_ff1_4, w_ff2, h1, dout, y, row(p["norm2_pre_g"]),
                                   row(p["norm1_post_g"]), tm_ffn)
    g_big["w_ff1"] = _dw(hn, df1, "dw_ff1", 2, D, tk, N_SHARD, lambda i, j: j)

    def reduce_start(name, keys, wholes):
        swapped = _swap_halves("rs_swap_" + name, [g_big[k] for k in keys], wholes)
        sums = _chip_sums(core, [g_big[k] for k in keys], swapped[:len(keys)], "chip_sums_" + name)
        arrays = list(sums)
        if wholes:
            arrays.append(_slot_sum(jnp.stack([wholes[0], swapped[-1]]), "chip_sum_small"))
        arrays += [lax.empty(s.shape, BF16) for s in sums]
        if wholes:
            arrays.append(lax.empty((N_SHARD,) + wholes[0].shape, F32))
        plan = _plan_exchange(len(keys), bool(wholes))
        send, recv, arrays, tok = _split_start("rs_send_" + name, arrays, 3 * len(keys) + 4 * len(wholes), plan)
        return send, recv, arrays, plan, tok

    def reduce_finish(name, keys, state, *after):
        send, recv, arrays, plan, _ = state
        arrays = _split_wait("rs_land_" + name, send, recv, arrays, plan, *after)
        n_in = len(arrays) // 2
        n = len(keys)
        finished = _final_sums(chip, arrays[:n], arrays[n_in:n_in + n], "final_sums_" + name)
        return dict(zip(keys, finished)), arrays

    by_half = lambda a, rows: a.reshape(N_SHARD, 2, rows // 2, a.shape[-1]).transpose(1, 0, 2, 3)
    dw_out = _dw(merged, dy, "dw_out", 1, D, tk, 1, lambda i, j: 0).reshape(D, D)
    g_big["w_out"] = by_half(dw_out, D // N_SHARD)
    early = _FF + ("w_out",)
    late = ("w_in", "w_pool", "w_sgu_proj")
    early_state = reduce_start("early", early, [])
    dz, dwp, dwpr, dws, dbsp, vec_c, dbin = _mix_bwd(z, pooled, dy, w_pool, row(p["pool_scale"]), row(p["sgu_ln_g"]),
                                                     row(p["sgu_ln_b"]), ws, ws_t, bsp, w_proj, w_out, mask, tm_mix,
                                                     early_state[4])
    g_big["w_in"] = _dw(xn, dz, "dw_in", 2, D_IN // N_SHARD, tk, N_SHARD, lambda i, j: j)
    shard_major = lambda a: a.reshape(N_HEADS, N_SHARD, 64, 256).transpose(1, 0, 2, 3).reshape(N_SHARD * 256, 256)
    g_big["w_pool"] = by_half(shard_major(dwp).astype(BF16), 256)
    g_big["w_sgu_proj"] = by_half(shard_major(dwpr).astype(BF16), 256)
    halves, _ = reduce_finish("early", early, early_state, g_big["w_in"])

    small_part = _pack(_SMALL[1:], SMALL_ROWS - 8, {
        "b_in": dbin[0], "pool_scale": vec_c[0], "sgu_ln_g": vec_c[1], "sgu_ln_b": vec_c[2],
        "w_spatial": dws, "b_spatial": dbsp[:, ::LANES].T, "norm1_post_g": vec_b[1], "norm2_pre_g": vec_b[0],
        "norm2_post_g": vec_a[0]})
    late_state = reduce_start("late", late, [small_part])
    share = [halves[k] for k in early] + [lax.empty(halves[k].shape, F32) for k in early]
    send_s, recv_s, share, token = _split_start("rs_share_early_send", share, len(early), _plan_share(len(early)),
                                                after=late_state[4])
    grad_x, vec_d = _dx(dz, w_in4, x, dh1, row(p["norm1_pre_g"]), token, tm_mm)
    share = _split_wait("rs_share_early_land", send_s, recv_s, share, _plan_share(len(early)), grad_x)
    tiles = _all_reduce_tile(jnp.concatenate([vec_d[0].reshape(8, LANES),
                                              jnp.broadcast_to(vec_a[1:2, :LANES], (8, LANES))]))
    g1_grad, loss = tiles[:8], tiles[8, 0]

    grads, delta, new_m, new_v = {}, {}, {}, {}

    def update(name, keys, shared):
        as2d = lambda a, k: a.reshape(local2d[k].shape)
        outs = _adamw_halves(core, [halves[k] for k in keys], shared, [local2d[k] for k in keys],
                             [as2d(m[k], k) for k in keys], [as2d(v[k], k) for k in keys], "adamw_" + name)
        for k, outs_k in zip(keys, outs):
            grads[k], delta[k], new_m[k], new_v[k] = (a.reshape(p[k].shape) for a in outs_k)

    update("early", early, share[len(early):])
    late_halves, late_arrays = reduce_finish("late", late, late_state, grad_x, *[new_v[k] for k in early])
    halves.update(late_halves)
    update("late", late, _swap_halves("rs_share_late", [], [halves[k] for k in late]))
    as_rows = lambda d: [d[k].reshape(-1, LANES) for k in _SMALL]
    small_outs = _adamw_small([g1_grad, _slot_sum(late_arrays[-1], "final_sum_small")],
                              as_rows(p), as_rows(m), as_rows(v))
    for k, outs_k in zip(_SMALL, small_outs):
        grads[k], delta[k], new_m[k], new_v[k] = (a.reshape(p[k].shape) for a in outs_k)

    return (loss, grad_x, *[grads[k] for k in _ORDER], *[delta[k] for k in _ORDER],
            *[new_m[k] for k in _ORDER], *[new_v[k] for k in _ORDER])


def kernel(x, norm1_pre_g, w_in, b_in, w_pool, pool_scale, sgu_ln_g, sgu_ln_b, w_spatial, b_spatial, w_sgu_proj, w_out, norm1_post_g, norm2_pre_g, w_ff1, w_ff2, norm2_post_g, loss_target, m_norm1_pre_g, m_w_in, m_b_in, m_w_pool, m_pool_scale, m_sgu_ln_g, m_sgu_ln_b, m_w_spatial, m_b_spatial, m_w_sgu_proj, m_w_out, m_norm1_post_g, m_norm2_pre_g, m_w_ff1, m_w_ff2, m_norm2_post_g, v_norm1_pre_g, v_w_in, v_b_in, v_w_pool, v_pool_scale, v_sgu_ln_g, v_sgu_ln_b, v_w_spatial, v_b_spatial, v_w_sgu_proj, v_w_out, v_norm1_post_g, v_norm2_pre_g, v_w_ff1, v_w_ff2, v_norm2_post_g):
    p = dict(zip(_ORDER, (norm1_pre_g, w_in, b_in, w_pool, pool_scale, sgu_ln_g, sgu_ln_b, w_spatial, b_spatial,
                          w_sgu_proj, w_out, norm1_post_g, norm2_pre_g, w_ff1, w_ff2, norm2_post_g)))
    m = dict(zip(_ORDER, (m_norm1_pre_g, m_w_in, m_b_in, m_w_pool, m_pool_scale, m_sgu_ln_g, m_sgu_ln_b, m_w_spatial,
                          m_b_spatial, m_w_sgu_proj, m_w_out, m_norm1_post_g, m_norm2_pre_g, m_w_ff1, m_w_ff2,
                          m_norm2_post_g)))
    v = dict(zip(_ORDER, (v_norm1_pre_g, v_w_in, v_b_in, v_w_pool, v_pool_scale, v_sgu_ln_g, v_sgu_ln_b, v_w_spatial,
                          v_b_spatial, v_w_sgu_proj, v_w_out, v_norm1_post_g, v_norm2_pre_g, v_w_ff1, v_w_ff2,
                          v_norm2_post_g)))
    outs = _step(p, m, v, x[0], loss_target[0])
    return (outs[0], outs[1][None], *outs[2:])
```

```python
import math

import jax
import jax.numpy as jnp
from jax import lax
from jax.experimental import pallas as pl
from jax.experimental.pallas import tpu as pltpu

F32 = jnp.float32
BF16 = jnp.bfloat16

D = 1024
D_IN = 5 * D
D_FF = 4 * D
N_SHARD = 4
POOL_WINDOWS = (2, 4, 8, 16)
POOL_GROUP = 256
POOL_HALO = 16
SGU_BLOCK = 128
N_HEADS = 4
HEAD = 256
EPS = 1e-6

ADAM_LR = 0.001
ADAM_B1 = 0.9
ADAM_B2 = 0.999
ADAM_EPS = 1e-08
ADAM_WD = 0.01
ADAM_STEP = 10

V7X_VMEM_BYTES = 64 * 1024 * 1024
VMEM_LIMIT = V7X_VMEM_BYTES - 8 * 1024 * 1024
SMALL_ROWS = 616
LANES = 128

MESH = pl.DeviceIdType.MESH
ANY = pl.BlockSpec(memory_space=pl.ANY)
HBM = pl.BlockSpec(memory_space=pltpu.HBM)
SEM = pl.BlockSpec(memory_space=pltpu.SEMAPHORE)
EFFECT = pltpu.SideEffectType.DATAFLOW_SIDE_EFFECTING

NT_DIMS = (((1,), (1,)), ((), ()))
TN_DIMS = (((0,), (0,)), ((), ()))


def _params(n_axes):
    return pltpu.CompilerParams(dimension_semantics=("arbitrary",) * n_axes, vmem_limit_bytes=VMEM_LIMIT)


def _resident(shape, n_axes):
    zeros = (0,) * len(shape)
    index = (lambda i: zeros) if n_axes == 1 else (lambda i, j: zeros)
    return pl.BlockSpec(shape, index, pipeline_mode=pl.Buffered(1))


def _dot(a, b):
    return jnp.dot(a, b, preferred_element_type=F32)


def _dot_nt(a, b):
    return lax.dot_general(a, b, NT_DIMS, preferred_element_type=F32)


def _dot_tn(a, b):
    return lax.dot_general(a, b, TN_DIMS, preferred_element_type=F32)


def _mean(a):
    return jnp.mean(a, axis=-1, keepdims=True)


def _colsum(a):
    return jnp.sum(a, axis=0, keepdims=True)


_GELU_C0 = math.sqrt(2.0 / math.pi)
_GELU_C1 = 0.044715


def _gelu(a):
    t = jnp.tanh(a * (_GELU_C0 + (_GELU_C0 * _GELU_C1) * (a * a)))
    return a * (0.5 * t + 0.5)


def _gelu_and_grad(a):
    a2 = a * a
    t = jnp.tanh(a * (_GELU_C0 + (_GELU_C0 * _GELU_C1) * a2))
    cdf = 0.5 * t + 0.5
    grad = cdf + (0.5 * a) * (1.0 - t * t) * (_GELU_C0 + (3.0 * _GELU_C0 * _GELU_C1) * a2)
    return a * cdf, grad


def _sigmoid(a):
    return 0.5 * jnp.tanh(0.5 * a) + 0.5


def _rms_norm_bf16(a, g):
    return (a * lax.rsqrt(_mean(a * a) + EPS) * g).astype(BF16)


def _relu_sq_bf16(a):
    r = jnp.maximum(a, 0.0)
    return (r * r).astype(BF16)


def _window_sum(a, w, ahead):
    n = a.shape[0]
    step = 1
    while step < w:
        a = a + pltpu.roll(a, n - step if ahead else step, 0)
        step *= 2
    return a


def _rms_bwd(dn, n, r):
    return r * (dn - n * _mean(dn * n))


def _inproj_own(x, g1, w_own, b_in, chip, dep, tm):
    t = x.shape[0]
    cw = D_IN // N_SHARD

    def body(chip_ref, x_ref, g_ref, w_ref, b_ref, dep_ref, z_ref, xn_ref):
        xn = _rms_norm_bf16(x_ref[...], g_ref[...])
        xn_ref[...] = xn
        z_ref[...] = _dot(xn, w_ref[...]) + b_ref[...]

    return pl.pallas_call(
        body, name="inproj_own",
        grid_spec=pltpu.PrefetchScalarGridSpec(
            num_scalar_prefetch=1, grid=(t // tm,),
            in_specs=[pl.BlockSpec((tm, D), lambda i, chip: (i, 0)),
                      pl.BlockSpec((1, D), lambda i, chip: (0, 0)),
                      pl.BlockSpec((D, cw), lambda i, chip: (0, 0)),
                      pl.BlockSpec((1, cw), lambda i, chip: (0, chip[0])),
                      pl.BlockSpec(dep.shape, lambda i, chip: (0, 0))],
            out_specs=[pl.BlockSpec((tm, cw), lambda i, chip: (i, chip[0])),
                       pl.BlockSpec((tm, D), lambda i, chip: (i, 0))]),
        out_shape=[jax.ShapeDtypeStruct((t, D_IN), F32), jax.ShapeDtypeStruct((t, D), BF16)],
        compiler_params=_params(1),
    )(chip, x, g1, w_own, b_in, dep)


def _inproj_rest(xn, w_in4, b_in, chip, z, tm):
    t = xn.shape[0]
    cw = D_IN // N_SHARD
    shard_of = lambda s, chip: lax.rem(chip[0] + 1 + s, N_SHARD)

    def body(chip_ref, xn_ref, w_ref, b_ref, z_in_ref, z_ref):
        z_ref[...] = _dot(xn_ref[...], w_ref[shard_of(pl.program_id(1), chip_ref)]) + b_ref[...]

    return pl.pallas_call(
        body, name="inproj_rest",
        grid_spec=pltpu.PrefetchScalarGridSpec(
            num_scalar_prefetch=1, grid=(t // tm, N_SHARD - 1),
            in_specs=[pl.BlockSpec((tm, D), lambda i, s, chip: (i, 0)),
                      pl.BlockSpec((N_SHARD, D, cw), lambda i, s, chip: (0, 0, 0), pipeline_mode=pl.Buffered(1)),
                      pl.BlockSpec((1, cw), lambda i, s, chip: (0, shard_of(s, chip))),
                      ANY],
            out_specs=pl.BlockSpec((tm, cw), lambda i, s, chip: (i, shard_of(s, chip)))),
        out_shape=jax.ShapeDtypeStruct(z.shape, F32),
        input_output_aliases={4: 0},
        compiler_params=_params(2),
    )(chip, xn, w_in4, b_in, z)


def _sgu_forward(z_ref, lng_ref, lnb_ref, ws_ref, bsp_ref, sv_ref, tm, with_grad):
    if with_grad:
        u, du_dz = _gelu_and_grad(z_ref[:, D:2 * D])
        gv, dgv_dz = _gelu_and_grad(z_ref[:, 2 * D:3 * D])
    else:
        u = _gelu(z_ref[:, D:2 * D])
        gv = _gelu(z_ref[:, 2 * D:3 * D])
        du_dz = dgv_dz = None
    xc = gv - _mean(gv)
    rstd = lax.rsqrt(_mean(xc * xc) + EPS)
    vhat = xc * rstd
    vb = (vhat * lng_ref[...] + lnb_ref[...]).astype(BF16)
    for b in range(tm // SGU_BLOCK):
        rows = slice(b * SGU_BLOCK, (b + 1) * SGU_BLOCK)
        for h in range(N_HEADS):
            cols = slice(h * HEAD, (h + 1) * HEAD)
            sv_ref[rows, cols] = _dot(ws_ref[h], vb[rows, cols]) + bsp_ref[h]
    return u, du_dz, dgv_dz, vhat, rstd, vb


def _mix_fwd(z, x, w_pool, pool_scale, ln_g, ln_b, ws, bsp, w_proj, w_out, g1post, tm):
    t = x.shape[0]

    def body(z_ref, x_ref, wp_ref, psc_ref, lng_ref, lnb_ref, ws_ref, bsp_ref, wpr_ref, wo_ref, g1p_ref,
             pooled_ref, merged_ref, y_ref, h1_ref, win_ref, sv_ref):
        i = pl.program_id(0)

        @pl.when(i == 0)
        def _():
            win_ref[0:POOL_HALO, :] = jnp.zeros((POOL_HALO, D), F32)

        win_ref[POOL_HALO:POOL_HALO + tm, :] = z_ref[:, 0:D]
        pos1 = lax.broadcasted_iota(jnp.int32, (tm, 1), 0) + (i * tm + 1)
        a_parts = []
        for g, w in enumerate(POOL_WINDOWS):
            cols = slice(g * POOL_GROUP, (g + 1) * POOL_GROUP)
            ts = _window_sum(win_ref[:, cols], w, False)[POOL_HALO:POOL_HALO + tm, :]
            inv_cnt = 1.0 / jnp.minimum(pos1, w).astype(F32)
            pooled = (ts * inv_cnt - z_ref[:, cols]).astype(BF16)
            pooled_ref[:, cols] = pooled
            a_parts.append(_dot(pooled, wp_ref[g]) * psc_ref[:, cols])
        win_ref[0:POOL_HALO, :] = win_ref[tm:tm + POOL_HALO, :]

        u, _, _, _, _, _ = _sgu_forward(z_ref, lng_ref, lnb_ref, ws_ref, bsp_ref, sv_ref, tm, False)
        gated = (u * sv_ref[...]).astype(BF16)
        for h in range(N_HEADS):
            lo = h * HEAD
            bbr = _dot(gated[:, lo:lo + HEAD], wpr_ref[h])
            gate_a = _sigmoid(z_ref[:, 3 * D + lo:3 * D + lo + HEAD])
            gate_b = _sigmoid(z_ref[:, 4 * D + lo:4 * D + lo + HEAD])
            merged_ref[:, lo:lo + HEAD] = (gate_a * a_parts[h] + gate_b * bbr).astype(BF16)
        merged = merged_ref[...]
        y = _dot(merged, wo_ref[...])
        y_ref[...] = y
        r = lax.rsqrt(_mean(y * y) + EPS)
        h1_ref[...] = x_ref[...] + (y * r) * g1p_ref[...]

    row = lambda i: (i, 0)
    fixed2 = lambda i: (0, 0)
    fixed3 = lambda i: (0, 0, 0)
    vec = pl.BlockSpec((1, D), fixed2)
    return pl.pallas_call(
        body, name="mix_fwd",
        grid=(t // tm,),
        in_specs=[pl.BlockSpec((tm, D_IN), row), pl.BlockSpec((tm, D), row),
                  pl.BlockSpec((N_HEADS, POOL_GROUP, POOL_GROUP), fixed3), vec, vec, vec,
                  pl.BlockSpec((N_HEADS, SGU_BLOCK, SGU_BLOCK), fixed3),
                  pl.BlockSpec((N_HEADS, SGU_BLOCK, 1), fixed3),
                  pl.BlockSpec((N_HEADS, HEAD, HEAD), fixed3),
                  pl.BlockSpec((D, D), fixed2), vec],
        out_specs=[pl.BlockSpec((tm, D), row)] * 4,
        out_shape=[jax.ShapeDtypeStruct((t, D), BF16), jax.ShapeDtypeStruct((t, D), BF16),
                   jax.ShapeDtypeStruct((t, D), F32), jax.ShapeDtypeStruct((t, D), F32)],
        scratch_shapes=[pltpu.VMEM((tm + POOL_HALO, D), F32), pltpu.VMEM((tm, D), F32)],
        compiler_params=_params(1),
    )(z, x, w_pool, pool_scale, ln_g, ln_b, ws, bsp, w_proj, w_out, g1post)


def _ffn_fwd(h1, g2pre, w1, w2, g2post, target, tm):
    t = h1.shape[0]
    n_j = D_FF // D

    def body(h1_ref, g2_ref, w1_ref, w2_ref, g2p_ref, tgt_ref,
             hn_ref, f1_ref, f_ref, df2_ref, dout_ref, vec_ref, acc_ref):
        i, j = pl.program_id(0), pl.program_id(1)

        @pl.when((i == 0) & (j == 0))
        def _():
            vec_ref[...] = jnp.zeros_like(vec_ref)

        @pl.when(j == 0)
        def _():
            hn_ref[...] = _rms_norm_bf16(h1_ref[...], g2_ref[...])
            acc_ref[...] = jnp.zeros_like(acc_ref)

        f1 = _dot(hn_ref[...], w1_ref[j])
        f1_ref[...] = f1
        f = _relu_sq_bf16(f1)
        f_ref[...] = f
        acc_ref[...] += _dot(f, w2_ref[j])

        @pl.when(j == n_j - 1)
        def _():
            f2 = acc_ref[...]
            r = lax.rsqrt(_mean(f2 * f2) + EPS)
            n2 = f2 * r
            err = h1_ref[...] + n2 * g2p_ref[...] - tgt_ref[...]
            loss = 0.5 * jnp.sum(_mean(err * err))
            dout = err * (1.0 / D)
            dout_ref[...] = dout
            vec_ref[0:1, :] += _colsum(dout * n2)
            vec_ref[1:2, :] += jnp.full((1, D), loss, F32)
            df2_ref[...] = _rms_bwd(dout * g2p_ref[...], n2, r).astype(BF16)

    row = lambda i, j: (i, 0)
    fixed = lambda i, j: (0, 0)
    vec = pl.BlockSpec((1, D), fixed)
    return pl.pallas_call(
        body, name="ffn_fwd",
        grid=(t // tm, n_j),
        in_specs=[pl.BlockSpec((tm, D), row), vec,
                  _resident((n_j, D, D), 2), _resident((n_j, D, D), 2), vec,
                  pl.BlockSpec((tm, D), row)],
        out_specs=[pl.BlockSpec((tm, D), row),
                   pl.BlockSpec((tm, D), lambda i, j: (i, j)),
                   pl.BlockSpec((tm, D), lambda i, j: (i, j)),
                   pl.BlockSpec((tm, D), row), pl.BlockSpec((tm, D), row),
                   pl.BlockSpec((8, D), fixed)],
        out_shape=[jax.ShapeDtypeStruct((t, D), BF16), jax.ShapeDtypeStruct((t, D_FF), F32),
                   jax.ShapeDtypeStruct((t, D_FF), BF16), jax.ShapeDtypeStruct((t, D), BF16),
                   jax.ShapeDtypeStruct((t, D), F32), jax.ShapeDtypeStruct((8, D), F32)],
        scratch_shapes=[pltpu.VMEM((tm, D), F32)],
        compiler_params=_params(2),
    )(h1, g2pre, w1, w2, g2post, target)


def _ffn_bwd(df2, f1, w1, w2, h1, dout, y, g2pre, g1post, tm):
    t = h1.shape[0]
    n_j = D_FF // D

    def body(df2_ref, f1_ref, w1_ref, w2_ref, h1_ref, dout_ref, y_ref, g2_ref, g1p_ref,
             df1_ref, dh1_ref, dy_ref, vec_ref, acc_ref):
        i, j = pl.program_id(0), pl.program_id(1)

        @pl.when((i == 0) & (j == 0))
        def _():
            vec_ref[...] = jnp.zeros_like(vec_ref)

        @pl.when(j == 0)
        def _():
            acc_ref[...] = jnp.zeros_like(acc_ref)

        df = _dot_nt(df2_ref[...], w2_ref[j])
        df1 = (df * (2.0 * jnp.maximum(f1_ref[...], 0.0))).astype(BF16)
        df1_ref[...] = df1
        acc_ref[...] += _dot_nt(df1, w1_ref[j])

        @pl.when(j == n_j - 1)
        def _():
            dhn = acc_ref[...]
            h = h1_ref[...]
            r2 = lax.rsqrt(_mean(h * h) + EPS)
            nh = h * r2
            vec_ref[0:1, :] += _colsum(dhn * nh)
            dh1 = dout_ref[...] + _rms_bwd(dhn * g2_ref[...], nh, r2)
            dh1_ref[...] = dh1
            yv = y_ref[...]
            r1 = lax.rsqrt(_mean(yv * yv) + EPS)
            ny = yv * r1
            vec_ref[1:2, :] += _colsum(dh1 * ny)
            dy_ref[...] = _rms_bwd(dh1 * g1p_ref[...], ny, r1).astype(BF16)

    row = lambda i, j: (i, 0)
    fixed = lambda i, j: (0, 0)
    vec = pl.BlockSpec((1, D), fixed)
    return pl.pallas_call(
        body, name="ffn_bwd",
        grid=(t // tm, n_j),
        in_specs=[pl.BlockSpec((tm, D), row),
                  pl.BlockSpec((tm, D), lambda i, j: (i, j)),
                  _resident((n_j, D, D), 2), _resident((n_j, D, D), 2),
                  pl.BlockSpec((tm, D), row), pl.BlockSpec((tm, D), row), pl.BlockSpec((tm, D), row), vec, vec],
        out_specs=[pl.BlockSpec((tm, D), lambda i, j: (i, j)),
                   pl.BlockSpec((tm, D), row), pl.BlockSpec((tm, D), row),
                   pl.BlockSpec((8, D), fixed)],
        out_shape=[jax.ShapeDtypeStruct((t, D_FF), BF16), jax.ShapeDtypeStruct((t, D), F32),
                   jax.ShapeDtypeStruct((t, D), BF16), jax.ShapeDtypeStruct((8, D), F32)],
        scratch_shapes=[pltpu.VMEM((tm, D), F32)],
        compiler_params=_params(2),
    )(df2, f1, w1, w2, h1, dout, y, g2pre, g1post)


def _mix_bwd(z, pooled, dy, w_pool, pool_scale, ln_g, ln_b, ws, ws_t, bsp, w_proj, w_out, mask, tm, dep):
    t = z.shape[0]
    n_t = t // tm

    def body(z_ref, pooled_ref, dy_ref, wp_ref, psc_ref, lng_ref, lnb_ref, ws_ref, wst_ref, bsp_ref, wpr_ref,
             wo_ref, mask_ref, dep_ref,
             dz_ref, dwp_ref, dwpr_ref, dws_ref, dbsp_ref, vec_ref, dbin_ref,
             win_ref, sv_ref, dsv_ref, dv_ref, dsvsum_ref):
        i = pl.program_id(0)

        @pl.when(i == 0)
        def _():
            win_ref[tm:tm + POOL_HALO, :] = jnp.zeros((POOL_HALO, D), F32)
            dwp_ref[...] = jnp.zeros_like(dwp_ref)
            dwpr_ref[...] = jnp.zeros_like(dwpr_ref)
            dws_ref[...] = jnp.zeros_like(dws_ref)
            vec_ref[...] = jnp.zeros_like(vec_ref)
            dbin_ref[...] = jnp.zeros_like(dbin_ref)
            dsvsum_ref[...] = jnp.zeros_like(dsvsum_ref)

        dmerged = _dot_nt(dy_ref[...], wo_ref[...])

        u, du_dz, dgv_dz, vhat, rstd, vb = _sgu_forward(z_ref, lng_ref, lnb_ref, ws_ref, bsp_ref, sv_ref, tm, True)
        sv = sv_ref[...]
        gated = (u * sv).astype(BF16)

        sa = _sigmoid(z_ref[:, 3 * D:4 * D])
        sb = _sigmoid(z_ref[:, 4 * D:5 * D])
        da = dmerged * sa
        dbbr = (dmerged * sb).astype(BF16)

        tile = n_t - 1 - i
        pos1 = lax.broadcasted_iota(jnp.int32, (tm, 1), 0) + (tile * tm + 1)
        def put(k, lo, part):
            cols = slice(k * D + lo, k * D + lo + part.shape[1])
            dz_ref[:, cols] = part.astype(BF16)
            dbin_ref[0:1, cols] += _colsum(part)

        dpooled_parts = []
        for g, w in enumerate(POOL_WINDOWS):
            cols = slice(g * POOL_GROUP, (g + 1) * POOL_GROUP)
            pg = pooled_ref[:, cols]
            a_pre = _dot(pg, wp_ref[g])
            da_g = da[:, cols]
            psc = psc_ref[:, cols]
            vec_ref[0:1, cols] += _colsum(da_g * a_pre)
            sa_g = sa[:, cols]
            put(3, g * POOL_GROUP, dmerged[:, cols] * (a_pre * psc) * (sa_g * (1.0 - sa_g)))
            da_pre = (da_g * psc).astype(BF16)
            dwp_ref[g] += _dot_tn(pg, da_pre)
            dpooled = _dot_nt(da_pre, wp_ref[g])
            dpooled_parts.append(dpooled)
            inv_cnt = 1.0 / jnp.minimum(pos1, w).astype(F32)
            win_ref[0:tm, cols] = dpooled * inv_cnt
        for g, w in enumerate(POOL_WINDOWS):
            cols = slice(g * POOL_GROUP, (g + 1) * POOL_GROUP)
            acc = _window_sum(win_ref[:, cols], w, True)[0:tm, :]
            put(0, g * POOL_GROUP, acc - dpooled_parts[g])
        win_ref[tm:tm + POOL_HALO, :] = win_ref[0:POOL_HALO, :]

        for h in range(N_HEADS):
            cols = slice(h * HEAD, (h + 1) * HEAD)
            g_h = gated[:, cols]
            db_h = dbbr[:, cols]
            bbr_h = _dot(g_h, wpr_ref[h])
            sb_h = sb[:, cols]
            put(4, h * HEAD, dmerged[:, cols] * bbr_h * (sb_h * (1.0 - sb_h)))
            dwpr_ref[h] += _dot_tn(g_h, db_h)
            dgated_h = _dot_nt(db_h, wpr_ref[h])
            dsv_ref[:, cols] = dgated_h * u[:, cols]
            sv_ref[:, cols] = dgated_h * sv[:, cols]
        put(1, 0, sv_ref[...] * du_dz)
        dsv = dsv_ref[...]
        dsvb = dsv.astype(BF16)
        blk_sum = dsv[0:SGU_BLOCK, :]
        for b in range(1, tm // SGU_BLOCK):
            blk_sum = blk_sum + dsv[b * SGU_BLOCK:(b + 1) * SGU_BLOCK, :]
        dsvsum_ref[...] += blk_sum
        for b in range(tm // SGU_BLOCK):
            rows = slice(b * SGU_BLOCK, (b + 1) * SGU_BLOCK)
            for h in range(N_HEADS):
                cols = slice(h * HEAD, (h + 1) * HEAD)
                dws_ref[h] += _dot_nt(dsvb[rows, cols], vb[rows, cols])
                dv_ref[rows, cols] = _dot(wst_ref[h], dsvb[rows, cols])
        dv = dv_ref[...]
        vec_ref[1:2, :] += _colsum(dv * vhat)
        vec_ref[2:3, :] += _colsum(dv)
        dvhat = dv * lng_ref[...]
        dgv = rstd * (dvhat - _mean(dvhat) - vhat * _mean(dvhat * vhat))
        put(2, 0, dgv * dgv_dz)

        @pl.when(i == n_t - 1)
        def _():
            for h in range(N_HEADS):
                dws_ref[h] = jnp.where(mask_ref[...] > 0.0, dws_ref[h], 0.0)
                tot = jnp.sum(dsvsum_ref[:, h * HEAD:(h + 1) * HEAD], axis=1, keepdims=True)
                dbsp_ref[:, h * LANES:(h + 1) * LANES] = jnp.broadcast_to(tot, (SGU_BLOCK, LANES))

    row = lambda i: (n_t - 1 - i, 0)
    fixed2 = lambda i: (0, 0)
    fixed3 = lambda i: (0, 0, 0)
    vec = pl.BlockSpec((1, D), fixed2)
    sq = pl.BlockSpec((N_HEADS, SGU_BLOCK, SGU_BLOCK), fixed3)
    grp = pl.BlockSpec((N_HEADS, HEAD, HEAD), fixed3)
    return pl.pallas_call(
        body, name="mix_bwd",
        grid=(n_t,),
        in_specs=[pl.BlockSpec((tm, D_IN), row), pl.BlockSpec((tm, D), row), pl.BlockSpec((tm, D), row),
                  grp, vec, vec, vec, sq, sq, pl.BlockSpec((N_HEADS, SGU_BLOCK, 1), fixed3), grp,
                  pl.BlockSpec((D, D), fixed2), pl.BlockSpec((SGU_BLOCK, SGU_BLOCK), fixed2),
                  pl.BlockSpec(dep.shape, fixed2)],
        out_specs=[pl.BlockSpec((tm, D_IN), row), grp, grp, sq,
                   pl.BlockSpec((SGU_BLOCK, N_HEADS * LANES), fixed2),
                   pl.BlockSpec((8, D), fixed2), pl.BlockSpec((8, D_IN), fixed2)],
        out_shape=[jax.ShapeDtypeStruct((t, D_IN), BF16),
                   jax.ShapeDtypeStruct((N_HEADS, HEAD, HEAD), F32), jax.ShapeDtypeStruct((N_HEADS, HEAD, HEAD), F32),
                   jax.ShapeDtypeStruct((N_HEADS, SGU_BLOCK, SGU_BLOCK), F32),
                   jax.ShapeDtypeStruct((SGU_BLOCK, N_HEADS * LANES), F32),
                   jax.ShapeDtypeStruct((8, D), F32), jax.ShapeDtypeStruct((8, D_IN), F32)],
        scratch_shapes=[pltpu.VMEM((tm + POOL_HALO, D), F32), pltpu.VMEM((tm, D), F32), pltpu.VMEM((tm, D), F32),
                        pltpu.VMEM((tm, D), F32), pltpu.VMEM((SGU_BLOCK, D), F32)],
        compiler_params=_params(1),
    )(z, pooled, dy, w_pool, pool_scale, ln_g, ln_b, ws, ws_t, bsp, w_proj, w_out, mask, dep)


def _dx(dz, w_in4, x, dh1, g1, dep, tm):
    t = x.shape[0]
    cw = D_IN // N_SHARD

    def body(dz_ref, w_ref, x_ref, dh1_ref, g_ref, dep_ref, gx_ref, vec_ref, acc_ref):
        i, s = pl.program_id(0), pl.program_id(1)

        @pl.when((i == 0) & (s == 0))
        def _():
            vec_ref[...] = jnp.zeros_like(vec_ref)

        @pl.when(s == 0)
        def _():
            acc_ref[...] = jnp.zeros_like(acc_ref)

        acc_ref[...] += _dot_nt(dz_ref[...], w_ref[s])

        @pl.when(s == N_SHARD - 1)
        def _():
            dxn = acc_ref[...]
            xv = x_ref[...]
            r = lax.rsqrt(_mean(xv * xv) + EPS)
            xh = xv * r
            vec_ref[0:1, :] += _colsum(dxn * xh)
            gx_ref[...] = dh1_ref[...] + _rms_bwd(dxn * g_ref[...], xh, r)

    row = lambda i, s: (i, 0)
    fixed = lambda i, s: (0, 0)
    return pl.pallas_call(
        body, name="dx",
        grid=(t // tm, N_SHARD),
        in_specs=[pl.BlockSpec((tm, cw), lambda i, s: (i, s)),
                  _resident((N_SHARD, D, cw), 2),
                  pl.BlockSpec((tm, D), row), pl.BlockSpec((tm, D), row), pl.BlockSpec((1, D), fixed),
                  pl.BlockSpec(dep.shape, fixed)],
        out_specs=[pl.BlockSpec((tm, D), row), pl.BlockSpec((8, D), fixed)],
        out_shape=[jax.ShapeDtypeStruct((t, D), F32), jax.ShapeDtypeStruct((8, D), F32)],
        scratch_shapes=[pltpu.VMEM((tm, D), F32)],
        compiler_params=_params(2),
    )(dz, w_in4, x, dh1, g1, dep)


def _dw(a, b, name, n_split, tn, tk, n_out, out_block):
    t, m = a.shape
    n = b.shape[1]
    tm = min(m, D)
    rows = tm // n_split
    n_k = t // tk

    def body(a_ref, b_ref, o_ref, acc_ref):
        k = pl.program_id(2)

        @pl.when(k == 0)
        def _():
            acc_ref[...] = jnp.zeros_like(acc_ref)

        acc_ref[...] += _dot_tn(a_ref[...], b_ref[...])

        @pl.when(k == n_k - 1)
        def _():
            for q in range(n_split):
                o_ref[q % 2, q // 2] = acc_ref[q * rows:(q + 1) * rows, :].astype(BF16)

    return pl.pallas_call(
        body, name=name,
        grid=(m // tm, n // tn, n_k),
        in_specs=[pl.BlockSpec((tk, tm), lambda i, j, k: (k, i)),
                  pl.BlockSpec((tk, tn), lambda i, j, k: (k, j))],
        out_specs=pl.BlockSpec((2, n_split // 2, rows, tn), lambda i, j, k: (0, out_block(i, j), 0, 0)),
        out_shape=jax.ShapeDtypeStruct((2, n_out, rows, tn), BF16),
        scratch_shapes=[pltpu.VMEM((tm, tn), F32)],
        compiler_params=_params(3),
    )(a, b)


def _row_block(rows, cols, itemsize, n_bufs):
    budget = VMEM_LIMIT // 4
    rb = rows
    while rb % 16 == 0 and rb * cols * itemsize * n_bufs * 2 > budget:
        rb //= 2
    return rb


ROW_STEPS = 4


def _chip_sums(core, grads, recvs, name):
    n = len(grads)

    def body(c_ref, *refs):
        for w in range(n):
            refs[2 * n + w][...] = (refs[w][0].astype(F32) + refs[n + w][...].astype(F32)).astype(BF16)

    def blocks(g):
        _, _, rh, cols = g.shape
        rb = rh // ROW_STEPS
        return (pl.BlockSpec((1, 1, rb, cols), lambda s, r, c: (c[0], s, r, 0)),
                pl.BlockSpec((1, rb, cols), lambda s, r, c: (s, r, 0)))

    specs = [blocks(g) for g in grads]
    return pl.pallas_call(
        body, name=name,
        grid_spec=pltpu.PrefetchScalarGridSpec(
            num_scalar_prefetch=1, grid=(N_SHARD, ROW_STEPS),
            in_specs=[s[0] for s in specs] + [s[1] for s in specs],
            out_specs=[s[1] for s in specs]),
        out_shape=[jax.ShapeDtypeStruct(g.shape[1:], BF16) for g in grads],
        compiler_params=_params(2),
    )(core, *grads, *recvs)


def _slot_sum(slots, name):
    n_s, rows, cols = slots.shape
    rb = _row_block(rows, cols, 4, n_s + 1)

    def body(s_ref, o_ref):
        acc = s_ref[0].astype(F32)
        for k in range(1, n_s):
            acc = acc + s_ref[k].astype(F32)
        o_ref[...] = acc

    return pl.pallas_call(
        body, name=name,
        grid=(rows // rb,),
        in_specs=[pl.BlockSpec((n_s, rb, cols), lambda r: (0, r, 0))],
        out_specs=pl.BlockSpec((rb, cols), lambda r: (r, 0)),
        out_shape=jax.ShapeDtypeStruct((rows, cols), F32),
        compiler_params=_params(1),
    )(slots)


def _adamw_update(g, w, m, v):
    m_new = ADAM_B1 * m + (1.0 - ADAM_B1) * g
    v_new = ADAM_B2 * v + (1.0 - ADAM_B2) * (g * g)
    m_hat = m_new / (1.0 - ADAM_B1 ** ADAM_STEP)
    v_hat = v_new / (1.0 - ADAM_B2 ** ADAM_STEP)
    return -ADAM_LR * (m_hat / (jnp.sqrt(v_hat) + ADAM_EPS) + ADAM_WD * w), m_new, v_new


def _adamw_small(grad_parts, ws, ms, vs):
    n, n_g = len(ws), len(grad_parts)

    def body(*refs):
        g_refs, ins, outs = refs[:n_g], refs[n_g:n_g + 3 * n], refs[n_g + 3 * n:]
        part, off = 0, 0
        for k in range(n):
            rows = ws[k].shape[0]
            if off == grad_parts[part].shape[0]:
                part, off = part + 1, 0
            g = g_refs[part][off:off + rows, :]
            off += rows
            delta, m_new, v_new = _adamw_update(g, ins[k][...], ins[n + k][...], ins[2 * n + k][...])
            for o, val in zip(outs[4 * k:4 * k + 4], (g, delta, m_new, v_new)):
                o[...] = val

    vmem = pl.BlockSpec(memory_space=pltpu.VMEM)
    outs = pl.pallas_call(
        body, name="adamw_small",
        in_specs=[vmem] * (n_g + 3 * n), out_specs=[vmem] * (4 * n),
        out_shape=[jax.ShapeDtypeStruct(w.shape, F32) for w in ws for _ in range(4)],
    )(*grad_parts, *ws, *ms, *vs)
    return [outs[4 * k:4 * k + 4] for k in range(n)]


def _final_sums(chip, owns, slots, name):
    n = len(owns)

    def body(chip_ref, *refs):
        for w in range(n):
            own_ref, s1_ref, s2_ref, s3_ref = refs[4 * w:4 * w + 4]
            acc = own_ref[0].astype(F32) + s1_ref[0].astype(F32)
            acc = acc + s2_ref[0].astype(F32)
            refs[4 * n + w][...] = acc + s3_ref[0].astype(F32)

    def slot(a, k):
        n_s, rh, cols = a.shape
        return pl.BlockSpec((1, rh // ROW_STEPS, cols), lambda r, chip: (lax.rem(chip[0] + k, n_s), r, 0))

    operands, in_specs = [], []
    for own, slots_w in zip(owns, slots):
        operands += [own, slots_w, slots_w, slots_w]
        in_specs += [slot(own, k) for k in range(4)]
    return pl.pallas_call(
        body, name=name,
        grid_spec=pltpu.PrefetchScalarGridSpec(
            num_scalar_prefetch=1, grid=(ROW_STEPS,),
            in_specs=in_specs,
            out_specs=[pl.BlockSpec((a.shape[1] // ROW_STEPS, a.shape[2]), lambda r, chip: (r, 0)) for a in owns]),
        out_shape=[jax.ShapeDtypeStruct(a.shape[1:], F32) for a in owns],
        compiler_params=_params(1),
    )(chip, *operands)


def _adamw_halves(core, owns, siblings, ws, ms, vs, name):
    n = len(ws)

    def body(c_ref, *refs):
        ins, outs = refs[:5 * n], refs[5 * n:]
        mine = c_ref[0] == pl.program_id(0)
        for k in range(n):
            own_ref, sib_ref, w_ref, m_ref, v_ref = ins[5 * k:5 * k + 5]
            g_ref, d_ref, mo_ref, vo_ref = outs[4 * k:4 * k + 4]

            @pl.when(mine)
            def _():
                g_ref[...] = own_ref[...]

            @pl.when(jnp.logical_not(mine))
            def _():
                g_ref[...] = sib_ref[...]

            d_ref[...], mo_ref[...], vo_ref[...] = _adamw_update(g_ref[...], w_ref[...], m_ref[...], v_ref[...])

    operands, in_specs, out_specs = [], [], []
    for own, sib, w, m, v in zip(owns, siblings, ws, ms, vs):
        rows, cols = w.shape
        rb = rows // 2 // ROW_STEPS
        half = pl.BlockSpec((rb, cols), lambda h, r, c: (r, 0))
        whole = pl.BlockSpec((rb, cols), lambda h, r, c: (h * ROW_STEPS + r, 0))
        operands += [own, sib, w, m, v]
        in_specs += [half, half, whole, whole, whole]
        out_specs += [whole] * 4
    outs = pl.pallas_call(
        body, name=name,
        grid_spec=pltpu.PrefetchScalarGridSpec(
            num_scalar_prefetch=1, grid=(2, ROW_STEPS), in_specs=in_specs, out_specs=out_specs),
        out_shape=[jax.ShapeDtypeStruct(w.shape, F32) for w in ws for _ in range(4)],
        compiler_params=_params(2),
    )(core, *operands)
    return [outs[4 * k:4 * k + 4] for k in range(n)]


def _place():
    x, y, c = lax.axis_index("x"), lax.axis_index("y"), lax.axis_index("c")
    chips = [(1 - x, y), (x, 1 - y), (1 - x, 1 - y)]
    return x, y, c, chips


def _remote(src, dst, send_sem, recv_sem, device):
    return pltpu.make_async_remote_copy(src_ref=src, dst_ref=dst, send_sem=send_sem, recv_sem=recv_sem,
                                        device_id=device, device_id_type=MESH)


def _split_start(name, arrays, n_copies, plan, after=None):
    n = len(arrays)
    n_in = n + (after is not None)

    def body(*refs):
        ins, send_sem, recv_sem, token = refs[:n], refs[n_in], refs[n_in + 1], refs[-1]
        for k, (src, dst, device) in enumerate(plan(ins)):
            _remote(src, dst, send_sem.at[k], recv_sem.at[k], device).start()
        token[...] = jnp.zeros_like(token)

    outs = pl.pallas_call(
        body, name=name,
        in_specs=[HBM] * n + [ANY] * (n_in - n),
        out_specs=(SEM, SEM, *[HBM] * n, pl.BlockSpec(memory_space=pltpu.VMEM)),
        out_shape=(pltpu.SemaphoreType.DMA((n_copies,)), pltpu.SemaphoreType.DMA((n_copies,)),
                   *[pltpu.HBM(a.shape, a.dtype) for a in arrays], jax.ShapeDtypeStruct((8, LANES), F32)),
        input_output_aliases={i: i + 2 for i in range(n)},
        compiler_params=pltpu.CompilerParams(has_side_effects=EFFECT),
    )(*[pltpu.with_memory_space_constraint(a, pltpu.HBM) for a in arrays], *([] if after is None else [after]))
    return outs[0], outs[1], list(outs[2:2 + n]), outs[-1]


def _split_wait(name, send_sem, recv_sem, arrays, plan, *after):
    n = len(arrays)

    def body(*refs):
        ins, send_ref, recv_ref = refs[:n], refs[n], refs[n + 1]
        for k, (src, dst, device) in enumerate(plan(ins)):
            cp = _remote(src, dst, send_ref.at[k], recv_ref.at[k], device)
            cp.wait_send()
            cp.wait_recv()

    outs = pl.pallas_call(
        body, name=name,
        in_specs=[HBM] * n + [SEM, SEM] + [ANY] * len(after), out_specs=[HBM] * n,
        out_shape=[pltpu.HBM(a.shape, a.dtype) for a in arrays],
        input_output_aliases={i: i for i in range(n)},
        compiler_params=pltpu.CompilerParams(has_side_effects=EFFECT),
    )(*arrays, send_sem, recv_sem, *after)
    return list(outs)


def _plan_gather_send(n):
    def plan(refs):
        x, y, c, chips = _place()
        me = 2 * x + y
        copies = []
        for w in range(n):
            own, full = refs[w], refs[n + w]
            copies.append((own, full.at[me], (x, y, 1 - c)))
            copies += [(own.at[c], full.at[me, c], (px, py, c)) for px, py in chips]
        return copies
    return plan


def _plan_gather_direct(n):
    def plan(refs):
        x, y, c, chips = _place()
        me = 2 * x + y
        copies = []
        for w in range(n):
            own, full = refs[w], refs[n + w]
            copies.append((own, full.at[me], (x, y, 1 - c)))
            for px, py in chips:
                copies += [(own.at[c], full.at[me, c], (px, py, c)), (own.at[c], full.at[me, c], (px, py, 1 - c))]
        return copies
    return plan


def _plan_gather_pass(n):
    def plan(refs):
        x, y, c, chips = _place()
        copies = []
        for w in range(n):
            for px, py in chips:
                landed = refs[w].at[2 * px + py, c]
                copies.append((landed, landed, (x, y, 1 - c)))
        return copies
    return plan


def _plan_share(n):
    def plan(refs):
        x, y, c, _ = _place()
        return [(refs[w], refs[n + w], (x, y, 1 - c)) for w in range(n)]
    return plan


def _plan_exchange(n, with_small):
    def plan(refs):
        x, y, c, chips = _place()
        me = 2 * x + y
        n_in = n + int(with_small)
        copies = []
        for w in range(n):
            copies += [(refs[w].at[2 * px + py], refs[n_in + w].at[me], (px, py, c)) for px, py in chips]
        if with_small:
            small, slots = refs[n], refs[n_in + n]
            copies.append((small, slots.at[me], (x, y, 1 - c)))
            copies += [(small, slots.at[me], (px, py, c)) for px, py in chips]
        return copies
    return plan


def _swap_halves(name, grads, wholes):
    n = len(grads)
    arrays = list(grads) + list(wholes)
    n_all = len(arrays)

    def body(*refs):
        ins, outs = refs[:n_all], refs[n_all:2 * n_all]
        send_sem, recv_sem = refs[2 * n_all:]
        x, y, c, _ = _place()
        sibling = (x, y, 1 - c)
        copies = []
        for w in range(n_all):
            src = ins[w].at[1 - c] if w < n else ins[w]
            cp = _remote(src, outs[w], send_sem.at[w], recv_sem.at[w], sibling)
            cp.start()
            copies.append(cp)
        for cp in copies:
            cp.wait()

    return pl.pallas_call(
        body, name=name,
        in_specs=[ANY] * n_all, out_specs=[ANY] * n_all,
        out_shape=[jax.ShapeDtypeStruct(g.shape[1:], g.dtype) for g in grads]
        + [jax.ShapeDtypeStruct(a.shape, a.dtype) for a in wholes],
        scratch_shapes=[pltpu.SemaphoreType.DMA((n_all,))] * 2,
    )(*arrays)


def _all_reduce_tile(part):
    n_dev = 8

    def body(part_ref, sum_ref, slots_ref, send_sem, recv_sem):
        x, y, c, _ = _place()
        me = 4 * x + 2 * y + c
        slots_ref[me] = part_ref[...]
        flips = [(fx, fy, fc) for fx in (0, 1) for fy in (0, 1) for fc in (0, 1)][1:]
        copies = []
        for k, (fx, fy, fc) in enumerate(flips):
            peer = (x + fx - 2 * x * fx, y + fy - 2 * y * fy, c + fc - 2 * c * fc)
            cp = _remote(part_ref, slots_ref.at[me], send_sem.at[k], recv_sem.at[k], peer)
            cp.start()
            copies.append(cp)
        for cp in copies:
            cp.wait()
        acc = slots_ref[0]
        for k in range(1, n_dev):
            acc = acc + slots_ref[k]
        sum_ref[...] = acc

    return pl.pallas_call(
        body, name="all_reduce_tile",
        in_specs=[pl.BlockSpec(memory_space=pltpu.VMEM)],
        out_specs=pl.BlockSpec(memory_space=pltpu.VMEM),
        out_shape=jax.ShapeDtypeStruct(part.shape, F32),
        scratch_shapes=[pltpu.VMEM((n_dev,) + part.shape, F32),
                        pltpu.SemaphoreType.DMA((n_dev - 1,)), pltpu.SemaphoreType.DMA((n_dev - 1,))],
    )(part)


_SMALL = ("norm1_pre_g", "b_in", "pool_scale", "sgu_ln_g", "sgu_ln_b", "w_spatial",
          "norm1_post_g", "norm2_pre_g", "norm2_post_g", "b_spatial")
_MIX = ("w_in", "w_pool", "w_sgu_proj", "w_out")
_FF = ("w_ff1", "w_ff2")
_BIG = _MIX + _FF
_ORDER = ("norm1_pre_g", "w_in", "b_in", "w_pool", "pool_scale", "sgu_ln_g", "sgu_ln_b", "w_spatial", "b_spatial",
          "w_sgu_proj", "w_out", "norm1_post_g", "norm2_pre_g", "w_ff1", "w_ff2", "norm2_post_g")


def _pack(keys, rows, parts):
    flat = jnp.concatenate([parts[k].reshape(-1).astype(F32) for k in keys])
    flat = jnp.pad(flat, (0, rows * LANES - flat.shape[0]))
    return flat.reshape(rows, LANES)


def _halves(a):
    return a.reshape(2, a.shape[0] // 2, a.shape[1])


def _step(p, m, v, x, target):
    t = x.shape[0]
    tm_mm = min(1024, t)
    tm_mix = min(256, t)
    tm_ffn = min(512, t)
    core = lax.axis_index("c").astype(jnp.int32).reshape(1)
    row = lambda a: a.reshape(1, -1)

    chip = (2 * lax.axis_index("x") + lax.axis_index("y")).astype(jnp.int32).reshape(1)

    local2d = {"w_in": p["w_in"], "w_ff1": p["w_ff1"], "w_ff2": p["w_ff2"], "w_out": p["w_out"],
               "w_pool": p["w_pool"].reshape(N_HEADS * 64, POOL_GROUP),
               "w_sgu_proj": p["w_sgu_proj"].reshape(N_HEADS * 64, HEAD)}
    shard = {k: _halves(local2d[k].astype(BF16)) for k in _BIG}
    landing = lambda keys: [lax.empty((N_SHARD,) + shard[k].shape, BF16) for k in keys]
    in_arrays = [shard["w_in"]] + landing(["w_in"])
    send_a, recv_a, in_arrays, token = _split_start("ag_in_send", in_arrays, 4, _plan_gather_send(1))
    z, xn = _inproj_own(x, row(p["norm1_pre_g"]), in_arrays[0].reshape(D, D_IN // N_SHARD), row(p["b_in"]), chip,
                        token, tm_mm)
    in_arrays = _split_wait("ag_in_land", send_a, recv_a, in_arrays, _plan_gather_send(1), z)
    send_b, recv_b, in_full, token = _split_start("ag_in_pass", in_arrays[1:], 3, _plan_gather_pass(1))
    mix_keys = _MIX[1:]
    mix_arrays = [shard[k] for k in mix_keys] + landing(mix_keys)
    send0, recv0, mix_arrays, token = _split_start("ag_mix_send", mix_arrays, 7 * len(mix_keys),
                                                   _plan_gather_direct(len(mix_keys)), after=token)
    ff_arrays = [shard[k] for k in _FF] + landing(_FF)
    send1, recv1, ff_arrays, token = _split_start("ag_ff_send", ff_arrays, 4 * len(_FF), _plan_gather_send(len(_FF)),
                                                  after=token)
    in_full = _split_wait("ag_in_passed", send_b, recv_b, in_full, _plan_gather_pass(1), token)
    w_in4 = in_full[0].reshape(N_SHARD, D, D_IN // N_SHARD)

    pos = jnp.arange(SGU_BLOCK) // 64
    mask = (pos[:, None] >= pos[None, :]).astype(F32)
    ws = (p["w_spatial"] * mask[None]).astype(BF16)
    ws_t = jnp.swapaxes(ws, 1, 2)
    bsp = p["b_spatial"].reshape(N_HEADS, SGU_BLOCK, 1)

    z = _inproj_rest(xn, w_in4, row(p["b_in"]), chip, z, tm_mm)
    mix_arrays = _split_wait("ag_mix_land", send0, recv0, mix_arrays, _plan_gather_direct(len(mix_keys)), z)
    full = dict(zip(mix_keys, mix_arrays[len(mix_keys):]))
    w_out = full["w_out"].reshape(D, D)
    regroup = lambda a: a.reshape(N_SHARD, N_HEADS, 64, 256).transpose(1, 0, 2, 3).reshape(N_HEADS, 256, 256)
    w_pool = regroup(full["w_pool"])
    w_proj = regroup(full["w_sgu_proj"])
    pooled, merged, y, h1 = _mix_fwd(z, x, w_pool, row(p["pool_scale"]), row(p["sgu_ln_g"]), row(p["sgu_ln_b"]),
                                     ws, bsp, w_proj, w_out, row(p["norm1_post_g"]), tm_mix)
    ff_arrays = _split_wait("ag_ff_land", send1, recv1, ff_arrays, _plan_gather_send(len(_FF)), pooled)
    send2, recv2, ff_full, token = _split_start("ag_ff_pass", ff_arrays[len(_FF):], 3 * len(_FF),
                                                _plan_gather_pass(len(_FF)))
    ff_full = _split_wait("ag_ff_passed", send2, recv2, ff_full, _plan_gather_pass(len(_FF)), token)
    w_ff1_4 = ff_full[0].reshape(N_SHARD, D, D)
    w_ff2 = ff_full[1].reshape(N_SHARD, D, D)
    hn, f1, f, df2, dout, vec_a = _ffn_fwd(h1, row(p["norm2_pre_g"]), w_ff1_4, w_ff2, row(p["norm2_post_g"]),
                                           target, tm_ffn)

    tk = min(1024, t)
    g_big = {"w_ff2": _dw(f, df2, "dw_ff2", 2, D, tk, N_SHARD, lambda i, j: i)}
    df1, dh1, dy, vec_b = _ffn_bwd(df2, f1, w_ff1_4, w_ff2, h1, dout, y, row(p["norm2_pre_g"]),
                                   row(p["norm1_post_g"]), tm_ffn)
    g_big["w_ff1"] = _dw(hn, df1, "dw_ff1", 2, D, tk, N_SHARD, lambda i, j: j)

    def reduce_start(name, keys, wholes):
        swapped = _swap_halves("rs_swap_" + name, [g_big[k] for k in keys], wholes)
        sums = _chip_sums(core, [g_big[k] for k in keys], swapped[:len(keys)], "chip_sums_" + name)
        arrays = list(sums)
        if wholes:
            arrays.append(_slot_sum(jnp.stack([wholes[0], swapped[-1]]), "chip_sum_small"))
        arrays += [lax.empty(s.shape, BF16) for s in sums]
        if wholes:
            arrays.append(lax.empty((N_SHARD,) + wholes[0].shape, F32))
        plan = _plan_exchange(len(keys), bool(wholes))
        send, recv, arrays, tok = _split_start("rs_send_" + name, arrays, 3 * len(keys) + 4 * len(wholes), plan)
        return send, recv, arrays, plan, tok

    def reduce_finish(name, keys, state, *after):
        send, recv, arrays, plan, _ = state
        arrays = _split_wait("rs_land_" + name, send, recv, arrays, plan, *after)
        n_in = len(arrays) // 2
        n = len(keys)
        finished = _final_sums(chip, arrays[:n], arrays[n_in:n_in + n], "final_sums_" + name)
        return dict(zip(keys, finished)), arrays

    by_half = lambda a, rows: a.reshape(N_SHARD, 2, rows // 2, a.shape[-1]).transpose(1, 0, 2, 3)
    g_big["w_out"] = _dw(merged, dy, "dw_out", 2 * N_SHARD, D, tk, N_SHARD, lambda i, j: 0)
    early = _FF + ("w_out",)
    late = ("w_in", "w_pool", "w_sgu_proj")
    early_state = reduce_start("early", early, [])
    dz, dwp, dwpr, dws, dbsp, vec_c, dbin = _mix_bwd(z, pooled, dy, w_pool, row(p["pool_scale"]), row(p["sgu_ln_g"]),
                                                     row(p["sgu_ln_b"]), ws, ws_t, bsp, w_proj, w_out, mask, tm_mix,
                                                     early_state[4])
    g_big["w_in"] = _dw(xn, dz, "dw_in", 2, D_IN // N_SHARD, tk, N_SHARD, lambda i, j: j)
    shard_major = lambda a: a.reshape(N_HEADS, N_SHARD, 64, 256).transpose(1, 0, 2, 3).reshape(N_SHARD * 256, 256)
    g_big["w_pool"] = by_half(shard_major(dwp).astype(BF16), 256)
    g_big["w_sgu_proj"] = by_half(shard_major(dwpr).astype(BF16), 256)
    halves, _ = reduce_finish("early", early, early_state, g_big["w_in"])

    small_part = _pack(_SMALL[1:], SMALL_ROWS - 8, {
        "b_in": dbin[0], "pool_scale": vec_c[0], "sgu_ln_g": vec_c[1], "sgu_ln_b": vec_c[2],
        "w_spatial": dws, "b_spatial": dbsp[:, ::LANES].T, "norm1_post_g": vec_b[1], "norm2_pre_g": vec_b[0],
        "norm2_post_g": vec_a[0]})
    late_state = reduce_start("late", late, [small_part])
    share = [halves[k] for k in early] + [lax.empty(halves[k].shape, F32) for k in early]
    send_s, recv_s, share, token = _split_start("rs_share_early_send", share, len(early), _plan_share(len(early)),
                                                after=late_state[4])
    grad_x, vec_d = _dx(dz, w_in4, x, dh1, row(p["norm1_pre_g"]), token, tm_mm)
    share = _split_wait("rs_share_early_land", send_s, recv_s, share, _plan_share(len(early)), grad_x)
    tiles = _all_reduce_tile(jnp.concatenate([vec_d[0].reshape(8, LANES),
                                              jnp.broadcast_to(vec_a[1:2, :LANES], (8, LANES))]))
    g1_grad, loss = tiles[:8], tiles[8, 0]

    grads, delta, new_m, new_v = {}, {}, {}, {}

    def update(name, keys, own, shared):
        as2d = lambda a, k: a.reshape(local2d[k].shape)
        outs = _adamw_halves(core, own, shared, [local2d[k] for k in keys],
                             [as2d(m[k], k) for k in keys], [as2d(v[k], k) for k in keys], "adamw_" + name)
        for k, outs_k in zip(keys, outs):
            grads[k], delta[k], new_m[k], new_v[k] = (a.reshape(p[k].shape) for a in outs_k)

    update("early", early, share[:len(early)], share[len(early):])
    late_halves, late_arrays = reduce_finish("late", late, late_state, grad_x, *[new_v[k] for k in early])
    own_late = [late_halves[k] for k in late]
    update("late", late, own_late, _swap_halves("rs_share_late", [], own_late))
    as_rows = lambda d: [d[k].reshape(-1, LANES) for k in _SMALL]
    small_outs = _adamw_small([g1_grad, _slot_sum(late_arrays[-1], "final_sum_small")],
                              as_rows(p), as_rows(m), as_rows(v))
    for k, outs_k in zip(_SMALL, small_outs):
        grads[k], delta[k], new_m[k], new_v[k] = (a.reshape(p[k].shape) for a in outs_k)

    return (loss, grad_x, *[grads[k] for k in _ORDER], *[delta[k] for k in _ORDER],
            *[new_m[k] for k in _ORDER], *[new_v[k] for k in _ORDER])


def kernel(x, norm1_pre_g, w_in, b_in, w_pool, pool_scale, sgu_ln_g, sgu_ln_b, w_spatial, b_spatial, w_sgu_proj, w_out, norm1_post_g, norm2_pre_g, w_ff1, w_ff2, norm2_post_g, loss_target, m_norm1_pre_g, m_w_in, m_b_in, m_w_pool, m_pool_scale, m_sgu_ln_g, m_sgu_ln_b, m_w_spatial, m_b_spatial, m_w_sgu_proj, m_w_out, m_norm1_post_g, m_norm2_pre_g, m_w_ff1, m_w_ff2, m_norm2_post_g, v_norm1_pre_g, v_w_in, v_b_in, v_w_pool, v_pool_scale, v_sgu_ln_g, v_sgu_ln_b, v_w_spatial, v_b_spatial, v_w_sgu_proj, v_w_out, v_norm1_post_g, v_norm2_pre_g, v_w_ff1, v_w_ff2, v_norm2_post_g):
    p = dict(zip(_ORDER, (norm1_pre_g, w_in, b_in, w_pool, pool_scale, sgu_ln_g, sgu_ln_b, w_spatial, b_spatial,
                          w_sgu_proj, w_out, norm1_post_g, norm2_pre_g, w_ff1, w_ff2, norm2_post_g)))
    m = dict(zip(_ORDER, (m_norm1_pre_g, m_w_in, m_b_in, m_w_pool, m_pool_scale, m_sgu_ln_g, m_sgu_ln_b, m_w_spatial,
                          m_b_spatial, m_w_sgu_proj, m_w_out, m_norm1_post_g, m_norm2_pre_g, m_w_ff1, m_w_ff2,
                          m_norm2_post_g)))
    v = dict(zip(_ORDER, (v_norm1_pre_g, v_w_in, v_b_in, v_w_pool, v_pool_scale, v_sgu_ln_g, v_sgu_ln_b, v_w_spatial,
                          v_b_spatial, v_w_sgu_proj, v_w_out, v_norm1_post_g, v_norm2_pre_g, v_w_ff1, v_w_ff2,
                          v_norm2_post_g)))
    outs = _step(p, m, v, x[0], loss_target[0])
    return (outs[0], outs[1][None], *outs[2:])
```

```python
import math

import jax
import jax.numpy as jnp
from jax import lax
from jax.experimental import pallas as pl
from jax.experimental.pallas import tpu as pltpu

F32 = jnp.float32
BF16 = jnp.bfloat16

D = 1024
D_IN = 5 * D
D_FF = 4 * D
N_SHARD = 4
POOL_WINDOWS = (2, 4, 8, 16)
POOL_GROUP = 256
POOL_HALO = 16
SGU_BLOCK = 128
N_HEADS = 4
HEAD = 256
EPS = 1e-6

ADAM_LR = 0.001
ADAM_B1 = 0.9
ADAM_B2 = 0.999
ADAM_EPS = 1e-08
ADAM_WD = 0.01
ADAM_STEP = 10

V7X_VMEM_BYTES = 64 * 1024 * 1024
VMEM_LIMIT = V7X_VMEM_BYTES - 8 * 1024 * 1024
SMALL_ROWS = 616
LANES = 128

MESH = pl.DeviceIdType.MESH
ANY = pl.BlockSpec(memory_space=pl.ANY)
HBM = pl.BlockSpec(memory_space=pltpu.HBM)
SEM = pl.BlockSpec(memory_space=pltpu.SEMAPHORE)
EFFECT = pltpu.SideEffectType.DATAFLOW_SIDE_EFFECTING

NT_DIMS = (((1,), (1,)), ((), ()))
TN_DIMS = (((0,), (0,)), ((), ()))


def _params(n_axes):
    return pltpu.CompilerParams(dimension_semantics=("arbitrary",) * n_axes, vmem_limit_bytes=VMEM_LIMIT)


def _resident(shape, n_axes):
    zeros = (0,) * len(shape)
    index = (lambda i: zeros) if n_axes == 1 else (lambda i, j: zeros)
    return pl.BlockSpec(shape, index, pipeline_mode=pl.Buffered(1))


def _dot(a, b):
    return jnp.dot(a, b, preferred_element_type=F32)


def _dot_nt(a, b):
    return lax.dot_general(a, b, NT_DIMS, preferred_element_type=F32)


def _dot_tn(a, b):
    return lax.dot_general(a, b, TN_DIMS, preferred_element_type=F32)


def _mean(a):
    return jnp.mean(a, axis=-1, keepdims=True)


def _colsum(a):
    return jnp.sum(a, axis=0, keepdims=True)


_GELU_C0 = math.sqrt(2.0 / math.pi)
_GELU_C1 = 0.044715


def _gelu(a):
    t = jnp.tanh(a * (_GELU_C0 + (_GELU_C0 * _GELU_C1) * (a * a)))
    return a * (0.5 * t + 0.5)


def _gelu_and_grad(a):
    a2 = a * a
    t = jnp.tanh(a * (_GELU_C0 + (_GELU_C0 * _GELU_C1) * a2))
    cdf = 0.5 * t + 0.5
    grad = cdf + (0.5 * a) * (1.0 - t * t) * (_GELU_C0 + (3.0 * _GELU_C0 * _GELU_C1) * a2)
    return a * cdf, grad


def _sigmoid(a):
    return 0.5 * jnp.tanh(0.5 * a) + 0.5


def _rms_norm_bf16(a, g):
    return (a * lax.rsqrt(_mean(a * a) + EPS) * g).astype(BF16)


def _relu_sq_bf16(a):
    r = jnp.maximum(a, 0.0)
    return (r * r).astype(BF16)


def _window_sum(a, w, ahead):
    n = a.shape[0]
    step = 1
    while step < w:
        a = a + pltpu.roll(a, n - step if ahead else step, 0)
        step *= 2
    return a


def _rms_bwd(dn, n, r):
    return r * (dn - n * _mean(dn * n))


def _inproj_own(x, g1, w_own, b_in, chip, dep, tm):
    t = x.shape[0]
    cw = D_IN // N_SHARD

    def body(chip_ref, x_ref, g_ref, w_ref, b_ref, dep_ref, z_ref, xn_ref):
        xn = _rms_norm_bf16(x_ref[...], g_ref[...])
        xn_ref[...] = xn
        z_ref[...] = _dot(xn, w_ref[...]) + b_ref[...]

    return pl.pallas_call(
        body, name="inproj_own",
        grid_spec=pltpu.PrefetchScalarGridSpec(
            num_scalar_prefetch=1, grid=(t // tm,),
            in_specs=[pl.BlockSpec((tm, D), lambda i, chip: (i, 0)),
                      pl.BlockSpec((1, D), lambda i, chip: (0, 0)),
                      pl.BlockSpec((D, cw), lambda i, chip: (0, 0)),
                      pl.BlockSpec((1, cw), lambda i, chip: (0, chip[0])),
                      pl.BlockSpec(dep.shape, lambda i, chip: (0, 0))],
            out_specs=[pl.BlockSpec((tm, cw), lambda i, chip: (i, chip[0])),
                       pl.BlockSpec((tm, D), lambda i, chip: (i, 0))]),
        out_shape=[jax.ShapeDtypeStruct((t, D_IN), F32), jax.ShapeDtypeStruct((t, D), BF16)],
        compiler_params=_params(1),
    )(chip, x, g1, w_own, b_in, dep)


def _inproj_rest(xn, w_in4, b_in, chip, z, tm):
    t = xn.shape[0]
    cw = D_IN // N_SHARD
    shard_of = lambda s, chip: lax.rem(chip[0] + 1 + s, N_SHARD)

    def body(chip_ref, xn_ref, w_ref, b_ref, z_in_ref, z_ref):
        z_ref[...] = _dot(xn_ref[...], w_ref[shard_of(pl.program_id(1), chip_ref)]) + b_ref[...]

    return pl.pallas_call(
        body, name="inproj_rest",
        grid_spec=pltpu.PrefetchScalarGridSpec(
            num_scalar_prefetch=1, grid=(t // tm, N_SHARD - 1),
            in_specs=[pl.BlockSpec((tm, D), lambda i, s, chip: (i, 0)),
                      pl.BlockSpec((N_SHARD, D, cw), lambda i, s, chip: (0, 0, 0), pipeline_mode=pl.Buffered(1)),
                      pl.BlockSpec((1, cw), lambda i, s, chip: (0, shard_of(s, chip))),
                      ANY],
            out_specs=pl.BlockSpec((tm, cw), lambda i, s, chip: (i, shard_of(s, chip)))),
        out_shape=jax.ShapeDtypeStruct(z.shape, F32),
        input_output_aliases={4: 0},
        compiler_params=_params(2),
    )(chip, xn, w_in4, b_in, z)


def _sgu_forward(z_ref, lng_ref, lnb_ref, ws_ref, bsp_ref, sv_ref, tm, with_grad):
    if with_grad:
        u, du_dz = _gelu_and_grad(z_ref[:, D:2 * D])
        gv, dgv_dz = _gelu_and_grad(z_ref[:, 2 * D:3 * D])
    else:
        u = _gelu(z_ref[:, D:2 * D])
        gv = _gelu(z_ref[:, 2 * D:3 * D])
        du_dz = dgv_dz = None
    xc = gv - _mean(gv)
    rstd = lax.rsqrt(_mean(xc * xc) + EPS)
    vhat = xc * rstd
    vb = (vhat * lng_ref[...] + lnb_ref[...]).astype(BF16)
    for b in range(tm // SGU_BLOCK):
        rows = slice(b * SGU_BLOCK, (b + 1) * SGU_BLOCK)
        for h in range(N_HEADS):
            cols = slice(h * HEAD, (h + 1) * HEAD)
            sv_ref[rows, cols] = _dot(ws_ref[h], vb[rows, cols]) + bsp_ref[h]
    return u, du_dz, dgv_dz, vhat, rstd, vb


def _mix_fwd(z, x, w_pool, pool_scale, ln_g, ln_b, ws, bsp, w_proj, w_out, g1post, tm):
    t = x.shape[0]

    def body(z_ref, x_ref, wp_ref, psc_ref, lng_ref, lnb_ref, ws_ref, bsp_ref, wpr_ref, wo_ref, g1p_ref,
             pooled_ref, merged_ref, y_ref, h1_ref, win_ref, sv_ref):
        i = pl.program_id(0)

        @pl.when(i == 0)
        def _():
            win_ref[0:POOL_HALO, :] = jnp.zeros((POOL_HALO, D), F32)

        win_ref[POOL_HALO:POOL_HALO + tm, :] = z_ref[:, 0:D]
        pos1 = lax.broadcasted_iota(jnp.int32, (tm, 1), 0) + (i * tm + 1)
        a_parts = []
        for g, w in enumerate(POOL_WINDOWS):
            cols = slice(g * POOL_GROUP, (g + 1) * POOL_GROUP)
            ts = _window_sum(win_ref[:, cols], w, False)[POOL_HALO:POOL_HALO + tm, :]
            inv_cnt = 1.0 / jnp.minimum(pos1, w).astype(F32)
            pooled = (ts * inv_cnt - z_ref[:, cols]).astype(BF16)
            pooled_ref[:, cols] = pooled
            a_parts.append(_dot(pooled, wp_ref[g]) * psc_ref[:, cols])
        win_ref[0:POOL_HALO, :] = win_ref[tm:tm + POOL_HALO, :]

        u, _, _, _, _, _ = _sgu_forward(z_ref, lng_ref, lnb_ref, ws_ref, bsp_ref, sv_ref, tm, False)
        gated = (u * sv_ref[...]).astype(BF16)
        for h in range(N_HEADS):
            lo = h * HEAD
            bbr = _dot(gated[:, lo:lo + HEAD], wpr_ref[h])
            gate_a = _sigmoid(z_ref[:, 3 * D + lo:3 * D + lo + HEAD])
            gate_b = _sigmoid(z_ref[:, 4 * D + lo:4 * D + lo + HEAD])
            merged_ref[:, lo:lo + HEAD] = (gate_a * a_parts[h] + gate_b * bbr).astype(BF16)
        merged = merged_ref[...]
        y = _dot(merged, wo_ref[...])
        y_ref[...] = y
        r = lax.rsqrt(_mean(y * y) + EPS)
        h1_ref[...] = x_ref[...] + (y * r) * g1p_ref[...]

    row = lambda i: (i, 0)
    fixed2 = lambda i: (0, 0)
    fixed3 = lambda i: (0, 0, 0)
    vec = pl.BlockSpec((1, D), fixed2)
    return pl.pallas_call(
        body, name="mix_fwd",
        grid=(t // tm,),
        in_specs=[pl.BlockSpec((tm, D_IN), row), pl.BlockSpec((tm, D), row),
                  pl.BlockSpec((N_HEADS, POOL_GROUP, POOL_GROUP), fixed3), vec, vec, vec,
                  pl.BlockSpec((N_HEADS, SGU_BLOCK, SGU_BLOCK), fixed3),
                  pl.BlockSpec((N_HEADS, SGU_BLOCK, 1), fixed3),
                  pl.BlockSpec((N_HEADS, HEAD, HEAD), fixed3),
                  pl.BlockSpec((D, D), fixed2), vec],
        out_specs=[pl.BlockSpec((tm, D), row)] * 4,
        out_shape=[jax.ShapeDtypeStruct((t, D), BF16), jax.ShapeDtypeStruct((t, D), BF16),
                   jax.ShapeDtypeStruct((t, D), F32), jax.ShapeDtypeStruct((t, D), F32)],
        scratch_shapes=[pltpu.VMEM((tm + POOL_HALO, D), F32), pltpu.VMEM((tm, D), F32)],
        compiler_params=_params(1),
    )(z, x, w_pool, pool_scale, ln_g, ln_b, ws, bsp, w_proj, w_out, g1post)


def _ffn_fwd(h1, g2pre, w1, w2, g2post, target, tm):
    t = h1.shape[0]
    n_j = D_FF // D

    def body(h1_ref, g2_ref, w1_ref, w2_ref, g2p_ref, tgt_ref,
             hn_ref, f1_ref, f_ref, df2_ref, dout_ref, vec_ref, acc_ref):
        i, j = pl.program_id(0), pl.program_id(1)

        @pl.when((i == 0) & (j == 0))
        def _():
            vec_ref[...] = jnp.zeros_like(vec_ref)

        @pl.when(j == 0)
        def _():
            hn_ref[...] = _rms_norm_bf16(h1_ref[...], g2_ref[...])
            acc_ref[...] = jnp.zeros_like(acc_ref)

        f1 = _dot(hn_ref[...], w1_ref[j])
        f1_ref[...] = f1
        f = _relu_sq_bf16(f1)
        f_ref[...] = f
        acc_ref[...] += _dot(f, w2_ref[j])

        @pl.when(j == n_j - 1)
        def _():
            f2 = acc_ref[...]
            r = lax.rsqrt(_mean(f2 * f2) + EPS)
            n2 = f2 * r
            err = h1_ref[...] + n2 * g2p_ref[...] - tgt_ref[...]
            loss = 0.5 * jnp.sum(_mean(err * err))
            dout = err * (1.0 / D)
            dout_ref[...] = dout
            vec_ref[0:1, :] += _colsum(dout * n2)
            vec_ref[1:2, :] += jnp.full((1, D), loss, F32)
            df2_ref[...] = _rms_bwd(dout * g2p_ref[...], n2, r).astype(BF16)

    row = lambda i, j: (i, 0)
    fixed = lambda i, j: (0, 0)
    vec = pl.BlockSpec((1, D), fixed)
    return pl.pallas_call(
        body, name="ffn_fwd",
        grid=(t // tm, n_j),
        in_specs=[pl.BlockSpec((tm, D), row), vec,
                  _resident((n_j, D, D), 2), _resident((n_j, D, D), 2), vec,
                  pl.BlockSpec((tm, D), row)],
        out_specs=[pl.BlockSpec((tm, D), row),
                   pl.BlockSpec((tm, D), lambda i, j: (i, j)),
                   pl.BlockSpec((tm, D), lambda i, j: (i, j)),
                   pl.BlockSpec((tm, D), row), pl.BlockSpec((tm, D), row),
                   pl.BlockSpec((8, D), fixed)],
        out_shape=[jax.ShapeDtypeStruct((t, D), BF16), jax.ShapeDtypeStruct((t, D_FF), F32),
                   jax.ShapeDtypeStruct((t, D_FF), BF16), jax.ShapeDtypeStruct((t, D), BF16),
                   jax.ShapeDtypeStruct((t, D), F32), jax.ShapeDtypeStruct((8, D), F32)],
        scratch_shapes=[pltpu.VMEM((tm, D), F32)],
        compiler_params=_params(2),
    )(h1, g2pre, w1, w2, g2post, target)


def _ffn_bwd(df2, f1, w1, w2, h1, dout, y, g2pre, g1post, tm):
    t = h1.shape[0]
    n_j = D_FF // D

    def body(df2_ref, f1_ref, w1_ref, w2_ref, h1_ref, dout_ref, y_ref, g2_ref, g1p_ref,
             df1_ref, dh1_ref, dy_ref, vec_ref, acc_ref):
        i, j = pl.program_id(0), pl.program_id(1)

        @pl.when((i == 0) & (j == 0))
        def _():
            vec_ref[...] = jnp.zeros_like(vec_ref)

        @pl.when(j == 0)
        def _():
            acc_ref[...] = jnp.zeros_like(acc_ref)

        df = _dot_nt(df2_ref[...], w2_ref[j])
        df1 = (df * (2.0 * jnp.maximum(f1_ref[...], 0.0))).astype(BF16)
        df1_ref[...] = df1
        acc_ref[...] += _dot_nt(df1, w1_ref[j])

        @pl.when(j == n_j - 1)
        def _():
            dhn = acc_ref[...]
            h = h1_ref[...]
            r2 = lax.rsqrt(_mean(h * h) + EPS)
            nh = h * r2
            vec_ref[0:1, :] += _colsum(dhn * nh)
            dh1 = dout_ref[...] + _rms_bwd(dhn * g2_ref[...], nh, r2)
            dh1_ref[...] = dh1
            yv = y_ref[...]
            r1 = lax.rsqrt(_mean(yv * yv) + EPS)
            ny = yv * r1
            vec_ref[1:2, :] += _colsum(dh1 * ny)
            dy_ref[...] = _rms_bwd(dh1 * g1p_ref[...], ny, r1).astype(BF16)

    row = lambda i, j: (i, 0)
    fixed = lambda i, j: (0, 0)
    vec = pl.BlockSpec((1, D), fixed)
    return pl.pallas_call(
        body, name="ffn_bwd",
        grid=(t // tm, n_j),
        in_specs=[pl.BlockSpec((tm, D), row),
                  pl.BlockSpec((tm, D), lambda i, j: (i, j)),
                  _resident((n_j, D, D), 2), _resident((n_j, D, D), 2),
                  pl.BlockSpec((tm, D), row), pl.BlockSpec((tm, D), row), pl.BlockSpec((tm, D), row), vec, vec],
        out_specs=[pl.BlockSpec((tm, D), lambda i, j: (i, j)),
                   pl.BlockSpec((tm, D), row), pl.BlockSpec((tm, D), row),
                   pl.BlockSpec((8, D), fixed)],
        out_shape=[jax.ShapeDtypeStruct((t, D_FF), BF16), jax.ShapeDtypeStruct((t, D), F32),
                   jax.ShapeDtypeStruct((t, D), BF16), jax.ShapeDtypeStruct((8, D), F32)],
        scratch_shapes=[pltpu.VMEM((tm, D), F32)],
        compiler_params=_params(2),
    )(df2, f1, w1, w2, h1, dout, y, g2pre, g1post)


def _mix_bwd(z, pooled, dy, w_pool, pool_scale, ln_g, ln_b, ws, ws_t, bsp, w_proj, w_out, mask, tm, dep):
    t = z.shape[0]
    n_t = t // tm

    def body(z_ref, pooled_ref, dy_ref, wp_ref, psc_ref, lng_ref, lnb_ref, ws_ref, wst_ref, bsp_ref, wpr_ref,
             wo_ref, mask_ref, dep_ref,
             dz_ref, dwp_ref, dwpr_ref, dws_ref, dbsp_ref, vec_ref, dbin_ref,
             win_ref, sv_ref, dsv_ref, dv_ref, dsvsum_ref):
        i = pl.program_id(0)

        @pl.when(i == 0)
        def _():
            win_ref[tm:tm + POOL_HALO, :] = jnp.zeros((POOL_HALO, D), F32)
            dwp_ref[...] = jnp.zeros_like(dwp_ref)
            dwpr_ref[...] = jnp.zeros_like(dwpr_ref)
            dws_ref[...] = jnp.zeros_like(dws_ref)
            vec_ref[...] = jnp.zeros_like(vec_ref)
            dbin_ref[...] = jnp.zeros_like(dbin_ref)
            dsvsum_ref[...] = jnp.zeros_like(dsvsum_ref)

        dmerged = _dot_nt(dy_ref[...], wo_ref[...])

        u, du_dz, dgv_dz, vhat, rstd, vb = _sgu_forward(z_ref, lng_ref, lnb_ref, ws_ref, bsp_ref, sv_ref, tm, True)
        sv = sv_ref[...]
        gated = (u * sv).astype(BF16)

        sa = _sigmoid(z_ref[:, 3 * D:4 * D])
        sb = _sigmoid(z_ref[:, 4 * D:5 * D])
        da = dmerged * sa
        dbbr = (dmerged * sb).astype(BF16)

        tile = n_t - 1 - i
        pos1 = lax.broadcasted_iota(jnp.int32, (tm, 1), 0) + (tile * tm + 1)
        def put(k, lo, part):
            cols = slice(k * D + lo, k * D + lo + part.shape[1])
            dz_ref[:, cols] = part.astype(BF16)
            dbin_ref[0:1, cols] += _colsum(part)

        dpooled_parts = []
        for g, w in enumerate(POOL_WINDOWS):
            cols = slice(g * POOL_GROUP, (g + 1) * POOL_GROUP)
            pg = pooled_ref[:, cols]
            a_pre = _dot(pg, wp_ref[g])
            da_g = da[:, cols]
            psc = psc_ref[:, cols]
            vec_ref[0:1, cols] += _colsum(da_g * a_pre)
            sa_g = sa[:, cols]
            put(3, g * POOL_GROUP, dmerged[:, cols] * (a_pre * psc) * (sa_g * (1.0 - sa_g)))
            da_pre = (da_g * psc).astype(BF16)
            dwp_ref[g] += _dot_tn(pg, da_pre)
            dpooled = _dot_nt(da_pre, wp_ref[g])
            dpooled_parts.append(dpooled)
            inv_cnt = 1.0 / jnp.minimum(pos1, w).astype(F32)
            win_ref[0:tm, cols] = dpooled * inv_cnt
        for g, w in enumerate(POOL_WINDOWS):
            cols = slice(g * POOL_GROUP, (g + 1) * POOL_GROUP)
            acc = _window_sum(win_ref[:, cols], w, True)[0:tm, :]
            put(0, g * POOL_GROUP, acc - dpooled_parts[g])
        win_ref[tm:tm + POOL_HALO, :] = win_ref[0:POOL_HALO, :]

        for h in range(N_HEADS):
            cols = slice(h * HEAD, (h + 1) * HEAD)
            g_h = gated[:, cols]
            db_h = dbbr[:, cols]
            bbr_h = _dot(g_h, wpr_ref[h])
            sb_h = sb[:, cols]
            put(4, h * HEAD, dmerged[:, cols] * bbr_h * (sb_h * (1.0 - sb_h)))
            dwpr_ref[h] += _dot_tn(g_h, db_h)
            dgated_h = _dot_nt(db_h, wpr_ref[h])
            dsv_ref[:, cols] = dgated_h * u[:, cols]
            sv_ref[:, cols] = dgated_h * sv[:, cols]
        put(1, 0, sv_ref[...] * du_dz)
        dsv = dsv_ref[...]
        dsvb = dsv.astype(BF16)
        blk_sum = dsv[0:SGU_BLOCK, :]
        for b in range(1, tm // SGU_BLOCK):
            blk_sum = blk_sum + dsv[b * SGU_BLOCK:(b + 1) * SGU_BLOCK, :]
        dsvsum_ref[...] += blk_sum
        for b in range(tm // SGU_BLOCK):
            rows = slice(b * SGU_BLOCK, (b + 1) * SGU_BLOCK)
            for h in range(N_HEADS):
                cols = slice(h * HEAD, (h + 1) * HEAD)
                dws_ref[h] += _dot_nt(dsvb[rows, cols], vb[rows, cols])
                dv_ref[rows, cols] = _dot(wst_ref[h], dsvb[rows, cols])
        dv = dv_ref[...]
        vec_ref[1:2, :] += _colsum(dv * vhat)
        vec_ref[2:3, :] += _colsum(dv)
        dvhat = dv * lng_ref[...]
        dgv = rstd * (dvhat - _mean(dvhat) - vhat * _mean(dvhat * vhat))
        put(2, 0, dgv * dgv_dz)

        @pl.when(i == n_t - 1)
        def _():
            for h in range(N_HEADS):
                dws_ref[h] = jnp.where(mask_ref[...] > 0.0, dws_ref[h], 0.0)
                tot = jnp.sum(dsvsum_ref[:, h * HEAD:(h + 1) * HEAD], axis=1, keepdims=True)
                dbsp_ref[:, h * LANES:(h + 1) * LANES] = jnp.broadcast_to(tot, (SGU_BLOCK, LANES))

    row = lambda i: (n_t - 1 - i, 0)
    fixed2 = lambda i: (0, 0)
    fixed3 = lambda i: (0, 0, 0)
    vec = pl.BlockSpec((1, D), fixed2)
    sq = pl.BlockSpec((N_HEADS, SGU_BLOCK, SGU_BLOCK), fixed3)
    grp = pl.BlockSpec((N_HEADS, HEAD, HEAD), fixed3)
    return pl.pallas_call(
        body, name="mix_bwd",
        grid=(n_t,),
        in_specs=[pl.BlockSpec((tm, D_IN), row), pl.BlockSpec((tm, D), row), pl.BlockSpec((tm, D), row),
                  grp, vec, vec, vec, sq, sq, pl.BlockSpec((N_HEADS, SGU_BLOCK, 1), fixed3), grp,
                  pl.BlockSpec((D, D), fixed2), pl.BlockSpec((SGU_BLOCK, SGU_BLOCK), fixed2),
                  pl.BlockSpec(dep.shape, fixed2)],
        out_specs=[pl.BlockSpec((tm, D_IN), row), grp, grp, sq,
                   pl.BlockSpec((SGU_BLOCK, N_HEADS * LANES), fixed2),
                   pl.BlockSpec((8, D), fixed2), pl.BlockSpec((8, D_IN), fixed2)],
        out_shape=[jax.ShapeDtypeStruct((t, D_IN), BF16),
                   jax.ShapeDtypeStruct((N_HEADS, HEAD, HEAD), F32), jax.ShapeDtypeStruct((N_HEADS, HEAD, HEAD), F32),
                   jax.ShapeDtypeStruct((N_HEADS, SGU_BLOCK, SGU_BLOCK), F32),
                   jax.ShapeDtypeStruct((SGU_BLOCK, N_HEADS * LANES), F32),
                   jax.ShapeDtypeStruct((8, D), F32), jax.ShapeDtypeStruct((8, D_IN), F32)],
        scratch_shapes=[pltpu.VMEM((tm + POOL_HALO, D), F32), pltpu.VMEM((tm, D), F32), pltpu.VMEM((tm, D), F32),
                        pltpu.VMEM((tm, D), F32), pltpu.VMEM((SGU_BLOCK, D), F32)],
        compiler_params=_params(1),
    )(z, pooled, dy, w_pool, pool_scale, ln_g, ln_b, ws, ws_t, bsp, w_proj, w_out, mask, dep)


def _dx(dz, w_in4, x, dh1, g1, dep, tm):
    t = x.shape[0]
    cw = D_IN // N_SHARD

    def body(dz_ref, w_ref, x_ref, dh1_ref, g_ref, dep_ref, gx_ref, vec_ref, acc_ref):
        i, s = pl.program_id(0), pl.program_id(1)

        @pl.when((i == 0) & (s == 0))
        def _():
            vec_ref[...] = jnp.zeros_like(vec_ref)

        @pl.when(s == 0)
        def _():
            acc_ref[...] = jnp.zeros_like(acc_ref)

        acc_ref[...] += _dot_nt(dz_ref[...], w_ref[s])

        @pl.when(s == N_SHARD - 1)
        def _():
            dxn = acc_ref[...]
            xv = x_ref[...]
            r = lax.rsqrt(_mean(xv * xv) + EPS)
            xh = xv * r
            vec_ref[0:1, :] += _colsum(dxn * xh)
            gx_ref[...] = dh1_ref[...] + _rms_bwd(dxn * g_ref[...], xh, r)

    row = lambda i, s: (i, 0)
    fixed = lambda i, s: (0, 0)
    return pl.pallas_call(
        body, name="dx",
        grid=(t // tm, N_SHARD),
        in_specs=[pl.BlockSpec((tm, cw), lambda i, s: (i, s)),
                  _resident((N_SHARD, D, cw), 2),
                  pl.BlockSpec((tm, D), row), pl.BlockSpec((tm, D), row), pl.BlockSpec((1, D), fixed),
                  pl.BlockSpec(dep.shape, fixed)],
        out_specs=[pl.BlockSpec((tm, D), row), pl.BlockSpec((8, D), fixed)],
        out_shape=[jax.ShapeDtypeStruct((t, D), F32), jax.ShapeDtypeStruct((8, D), F32)],
        scratch_shapes=[pltpu.VMEM((tm, D), F32)],
        compiler_params=_params(2),
    )(dz, w_in4, x, dh1, g1, dep)


def _dw(a, b, name, n_split, tn, tk, n_out, out_block):
    t, m = a.shape
    n = b.shape[1]
    tm = min(m, D)
    rows = tm // n_split
    n_k = t // tk

    def body(a_ref, b_ref, o_ref, acc_ref):
        k = pl.program_id(2)

        @pl.when(k == 0)
        def _():
            acc_ref[...] = jnp.zeros_like(acc_ref)

        acc_ref[...] += _dot_tn(a_ref[...], b_ref[...])

        @pl.when(k == n_k - 1)
        def _():
            for q in range(n_split):
                o_ref[q % 2, q // 2] = acc_ref[q * rows:(q + 1) * rows, :].astype(BF16)

    return pl.pallas_call(
        body, name=name,
        grid=(m // tm, n // tn, n_k),
        in_specs=[pl.BlockSpec((tk, tm), lambda i, j, k: (k, i)),
                  pl.BlockSpec((tk, tn), lambda i, j, k: (k, j))],
        out_specs=pl.BlockSpec((2, n_split // 2, rows, tn), lambda i, j, k: (0, out_block(i, j), 0, 0)),
        out_shape=jax.ShapeDtypeStruct((2, n_out, rows, tn), BF16),
        scratch_shapes=[pltpu.VMEM((tm, tn), F32)],
        compiler_params=_params(3),
    )(a, b)


def _row_block(rows, cols, itemsize, n_bufs):
    budget = VMEM_LIMIT // 4
    rb = rows
    while rb % 16 == 0 and rb * cols * itemsize * n_bufs * 2 > budget:
        rb //= 2
    return rb


ROW_STEPS = 2


def _chip_sums(core, grads, recvs, name):
    n = len(grads)

    def body(c_ref, *refs):
        for w in range(n):
            refs[2 * n + w][...] = (refs[w][0].astype(F32) + refs[n + w][...].astype(F32)).astype(BF16)

    def blocks(g):
        _, _, rh, cols = g.shape
        rb = rh // ROW_STEPS
        return (pl.BlockSpec((1, 1, rb, cols), lambda s, r, c: (c[0], s, r, 0)),
                pl.BlockSpec((1, rb, cols), lambda s, r, c: (s, r, 0)))

    specs = [blocks(g) for g in grads]
    return pl.pallas_call(
        body, name=name,
        grid_spec=pltpu.PrefetchScalarGridSpec(
            num_scalar_prefetch=1, grid=(N_SHARD, ROW_STEPS),
            in_specs=[s[0] for s in specs] + [s[1] for s in specs],
            out_specs=[s[1] for s in specs]),
        out_shape=[jax.ShapeDtypeStruct(g.shape[1:], BF16) for g in grads],
        compiler_params=_params(2),
    )(core, *grads, *recvs)


def _slot_sum(slots, name):
    n_s, rows, cols = slots.shape
    rb = _row_block(rows, cols, 4, n_s + 1)

    def body(s_ref, o_ref):
        acc = s_ref[0].astype(F32)
        for k in range(1, n_s):
            acc = acc + s_ref[k].astype(F32)
        o_ref[...] = acc

    return pl.pallas_call(
        body, name=name,
        grid=(rows // rb,),
        in_specs=[pl.BlockSpec((n_s, rb, cols), lambda r: (0, r, 0))],
        out_specs=pl.BlockSpec((rb, cols), lambda r: (r, 0)),
        out_shape=jax.ShapeDtypeStruct((rows, cols), F32),
        compiler_params=_params(1),
    )(slots)


def _adamw_update(g, w, m, v):
    m_new = ADAM_B1 * m + (1.0 - ADAM_B1) * g
    v_new = ADAM_B2 * v + (1.0 - ADAM_B2) * (g * g)
    m_hat = m_new / (1.0 - ADAM_B1 ** ADAM_STEP)
    v_hat = v_new / (1.0 - ADAM_B2 ** ADAM_STEP)
    return -ADAM_LR * (m_hat / (jnp.sqrt(v_hat) + ADAM_EPS) + ADAM_WD * w), m_new, v_new


def _adamw_small(grad_parts, ws, ms, vs):
    n, n_g = len(ws), len(grad_parts)

    def rows_of(ref, lo, rows):
        if len(ref.shape) == 2:
            return ref[lo:lo + rows, :]
        acc = ref[0, lo:lo + rows, :]
        for s in range(1, ref.shape[0]):
            acc = acc + ref[s, lo:lo + rows, :]
        return acc

    def body(*refs):
        g_refs, ins, outs = refs[:n_g], refs[n_g:n_g + 3 * n], refs[n_g + 3 * n:]
        part, off = 0, 0
        for k in range(n):
            rows = ws[k].shape[0]
            if off + rows > grad_parts[part].shape[-2]:
                part, off = part + 1, 0
            g = rows_of(g_refs[part], off, rows)
            off += rows
            delta, m_new, v_new = _adamw_update(g, ins[k][...], ins[n + k][...], ins[2 * n + k][...])
            for o, val in zip(outs[4 * k:4 * k + 4], (g, delta, m_new, v_new)):
                o[...] = val

    vmem = pl.BlockSpec(memory_space=pltpu.VMEM)
    outs = pl.pallas_call(
        body, name="adamw_small",
        in_specs=[vmem] * (n_g + 3 * n), out_specs=[vmem] * (4 * n),
        out_shape=[jax.ShapeDtypeStruct(w.shape, F32) for w in ws for _ in range(4)],
    )(*grad_parts, *ws, *ms, *vs)
    return [outs[4 * k:4 * k + 4] for k in range(n)]


def _final_sums(chip, owns, slots, name):
    n = len(owns)

    def body(chip_ref, *refs):
        for w in range(n):
            own_ref, s1_ref, s2_ref, s3_ref = refs[4 * w:4 * w + 4]
            acc = own_ref[0].astype(F32) + s1_ref[0].astype(F32)
            acc = acc + s2_ref[0].astype(F32)
            refs[4 * n + w][...] = acc + s3_ref[0].astype(F32)

    def slot(a, k):
        n_s, rh, cols = a.shape
        return pl.BlockSpec((1, rh // ROW_STEPS, cols), lambda r, chip: (lax.rem(chip[0] + k, n_s), r, 0))

    operands, in_specs = [], []
    for own, slots_w in zip(owns, slots):
        operands += [own, slots_w, slots_w, slots_w]
        in_specs += [slot(own, k) for k in range(4)]
    return pl.pallas_call(
        body, name=name,
        grid_spec=pltpu.PrefetchScalarGridSpec(
            num_scalar_prefetch=1, grid=(ROW_STEPS,),
            in_specs=in_specs,
            out_specs=[pl.BlockSpec((a.shape[1] // ROW_STEPS, a.shape[2]), lambda r, chip: (r, 0)) for a in owns]),
        out_shape=[jax.ShapeDtypeStruct(a.shape[1:], F32) for a in owns],
        compiler_params=_params(1),
    )(chip, *operands)


def _adamw_halves(core, owns, siblings, ws, ms, vs, name):
    n = len(ws)

    def body(c_ref, *refs):
        ins, outs = refs[:5 * n], refs[5 * n:]
        mine = c_ref[0] == pl.program_id(0)
        for k in range(n):
            own_ref, sib_ref, w_ref, m_ref, v_ref = ins[5 * k:5 * k + 5]
            g_ref, d_ref, mo_ref, vo_ref = outs[4 * k:4 * k + 4]

            @pl.when(mine)
            def _():
                g_ref[...] = own_ref[...]

            @pl.when(jnp.logical_not(mine))
            def _():
                g_ref[...] = sib_ref[...]

            d_ref[...], mo_ref[...], vo_ref[...] = _adamw_update(g_ref[...], w_ref[...], m_ref[...], v_ref[...])

    operands, in_specs, out_specs = [], [], []
    for own, sib, w, m, v in zip(owns, siblings, ws, ms, vs):
        rows, cols = w.shape
        rb = rows // 2 // ROW_STEPS
        half = pl.BlockSpec((rb, cols), lambda h, r, c: (r, 0))
        whole = pl.BlockSpec((rb, cols), lambda h, r, c: (h * ROW_STEPS + r, 0))
        operands += [own, sib, w, m, v]
        in_specs += [half, half, whole, whole, whole]
        out_specs += [whole] * 4
    outs = pl.pallas_call(
        body, name=name,
        grid_spec=pltpu.PrefetchScalarGridSpec(
            num_scalar_prefetch=1, grid=(2, ROW_STEPS), in_specs=in_specs, out_specs=out_specs),
        out_shape=[jax.ShapeDtypeStruct(w.shape, F32) for w in ws for _ in range(4)],
        compiler_params=_params(2),
    )(core, *operands)
    return [outs[4 * k:4 * k + 4] for k in range(n)]


def _place():
    x, y, c = lax.axis_index("x"), lax.axis_index("y"), lax.axis_index("c")
    chips = [(1 - x, y), (x, 1 - y), (1 - x, 1 - y)]
    return x, y, c, chips


def _remote(src, dst, send_sem, recv_sem, device):
    return pltpu.make_async_remote_copy(src_ref=src, dst_ref=dst, send_sem=send_sem, recv_sem=recv_sem,
                                        device_id=device, device_id_type=MESH)


def _split_start(name, arrays, n_copies, plan, after=None):
    n = len(arrays)
    n_in = n + (after is not None)

    def body(*refs):
        ins, send_sem, recv_sem, token = refs[:n], refs[n_in], refs[n_in + 1], refs[-1]
        for k, (src, dst, device) in enumerate(plan(ins)):
            _remote(src, dst, send_sem.at[k], recv_sem.at[k], device).start()
        token[...] = jnp.zeros_like(token)

    outs = pl.pallas_call(
        body, name=name,
        in_specs=[HBM] * n + [ANY] * (n_in - n),
        out_specs=(SEM, SEM, *[HBM] * n, pl.BlockSpec(memory_space=pltpu.VMEM)),
        out_shape=(pltpu.SemaphoreType.DMA((n_copies,)), pltpu.SemaphoreType.DMA((n_copies,)),
                   *[pltpu.HBM(a.shape, a.dtype) for a in arrays], jax.ShapeDtypeStruct((8, LANES), F32)),
        input_output_aliases={i: i + 2 for i in range(n)},
        compiler_params=pltpu.CompilerParams(has_side_effects=EFFECT),
    )(*[pltpu.with_memory_space_constraint(a, pltpu.HBM) for a in arrays], *([] if after is None else [after]))
    return outs[0], outs[1], list(outs[2:2 + n]), outs[-1]


def _split_wait(name, send_sem, recv_sem, arrays, plan, *after):
    n = len(arrays)

    def body(*refs):
        ins, send_ref, recv_ref = refs[:n], refs[n], refs[n + 1]
        for k, (src, dst, device) in enumerate(plan(ins)):
            cp = _remote(src, dst, send_ref.at[k], recv_ref.at[k], device)
            cp.wait_send()
            cp.wait_recv()

    outs = pl.pallas_call(
        body, name=name,
        in_specs=[HBM] * n + [SEM, SEM] + [ANY] * len(after), out_specs=[HBM] * n,
        out_shape=[pltpu.HBM(a.shape, a.dtype) for a in arrays],
        input_output_aliases={i: i for i in range(n)},
        compiler_params=pltpu.CompilerParams(has_side_effects=EFFECT),
    )(*arrays, send_sem, recv_sem, *after)
    return list(outs)


def _plan_gather_send(n):
    def plan(refs):
        x, y, c, chips = _place()
        me = 2 * x + y
        copies = []
        for w in range(n):
            own, full = refs[w], refs[n + w]
            copies.append((own, full.at[me], (x, y, 1 - c)))
            copies += [(own.at[c], full.at[me, c], (px, py, c)) for px, py in chips]
        return copies
    return plan


def _plan_gather_direct(n):
    def plan(refs):
        x, y, c, chips = _place()
        me = 2 * x + y
        copies = []
        for w in range(n):
            own, full = refs[w], refs[n + w]
            copies.append((own, full.at[me], (x, y, 1 - c)))
            for px, py in chips:
                copies += [(own.at[c], full.at[me, c], (px, py, c)), (own.at[c], full.at[me, c], (px, py, 1 - c))]
        return copies
    return plan


def _plan_gather_pass(n):
    def plan(refs):
        x, y, c, chips = _place()
        copies = []
        for w in range(n):
            for px, py in chips:
                landed = refs[w].at[2 * px + py, c]
                copies.append((landed, landed, (x, y, 1 - c)))
        return copies
    return plan


def _plan_share(n):
    def plan(refs):
        x, y, c, _ = _place()
        return [(refs[w], refs[n + w], (x, y, 1 - c)) for w in range(n)]
    return plan


def _plan_exchange(n, with_small):
    def plan(refs):
        x, y, c, chips = _place()
        me = 2 * x + y
        n_in = n + int(with_small)
        copies = []
        for w in range(n):
            copies += [(refs[w].at[2 * px + py], refs[n_in + w].at[me], (px, py, c)) for px, py in chips]
        if with_small:
            small, slots = refs[n], refs[n_in + n]
            copies.append((small, slots.at[me], (x, y, 1 - c)))
            copies += [(small, slots.at[me], (px, py, c)) for px, py in chips]
        return copies
    return plan


def _swap_halves(name, grads, wholes):
    n = len(grads)
    arrays = list(grads) + list(wholes)
    n_all = len(arrays)

    def body(*refs):
        ins, outs = refs[:n_all], refs[n_all:2 * n_all]
        send_sem, recv_sem = refs[2 * n_all:]
        x, y, c, _ = _place()
        sibling = (x, y, 1 - c)
        copies = []
        for w in range(n_all):
            src = ins[w].at[1 - c] if w < n else ins[w]
            cp = _remote(src, outs[w], send_sem.at[w], recv_sem.at[w], sibling)
            cp.start()
            copies.append(cp)
        for cp in copies:
            cp.wait()

    return pl.pallas_call(
        body, name=name,
        in_specs=[ANY] * n_all, out_specs=[ANY] * n_all,
        out_shape=[jax.ShapeDtypeStruct(g.shape[1:], g.dtype) for g in grads]
        + [jax.ShapeDtypeStruct(a.shape, a.dtype) for a in wholes],
        scratch_shapes=[pltpu.SemaphoreType.DMA((n_all,))] * 2,
    )(*arrays)


def _share_and_all_reduce(wholes, part):
    n = len(wholes)
    n_dev = 8

    def body(*refs):
        ins, part_ref, outs, sum_ref = refs[:n], refs[n], refs[n + 1:2 * n + 1], refs[2 * n + 1]
        slots_ref, send_sem, recv_sem, send_tile, recv_tile = refs[2 * n + 2:]
        x, y, c, _ = _place()
        me = 4 * x + 2 * y + c
        copies = []
        for w in range(n):
            copies.append(_remote(ins[w], outs[w], send_sem.at[w], recv_sem.at[w], (x, y, 1 - c)))
        slots_ref[me] = part_ref[...]
        flips = [(fx, fy, fc) for fx in (0, 1) for fy in (0, 1) for fc in (0, 1)][1:]
        for k, (fx, fy, fc) in enumerate(flips):
            peer = (x + fx - 2 * x * fx, y + fy - 2 * y * fy, c + fc - 2 * c * fc)
            copies.append(_remote(part_ref, slots_ref.at[me], send_tile.at[k], recv_tile.at[k], peer))
        for cp in copies:
            cp.start()
        for cp in copies:
            cp.wait()
        acc = slots_ref[0]
        for k in range(1, n_dev):
            acc = acc + slots_ref[k]
        sum_ref[...] = acc

    vmem = pl.BlockSpec(memory_space=pltpu.VMEM)
    outs = pl.pallas_call(
        body, name="rs_share_late",
        in_specs=[ANY] * n + [vmem], out_specs=[ANY] * n + [vmem],
        out_shape=[jax.ShapeDtypeStruct(a.shape, a.dtype) for a in wholes] + [jax.ShapeDtypeStruct(part.shape, F32)],
        scratch_shapes=[pltpu.VMEM((n_dev,) + part.shape, F32),
                        pltpu.SemaphoreType.DMA((n,)), pltpu.SemaphoreType.DMA((n,)),
                        pltpu.SemaphoreType.DMA((n_dev - 1,)), pltpu.SemaphoreType.DMA((n_dev - 1,))],
    )(*wholes, part)
    return outs[:n], outs[n]


_SMALL = ("norm1_pre_g", "b_in", "pool_scale", "sgu_ln_g", "sgu_ln_b", "w_spatial",
          "norm1_post_g", "norm2_pre_g", "norm2_post_g", "b_spatial")
_MIX = ("w_in", "w_pool", "w_sgu_proj", "w_out")
_FF = ("w_ff1", "w_ff2")
_BIG = _MIX + _FF
_ORDER = ("norm1_pre_g", "w_in", "b_in", "w_pool", "pool_scale", "sgu_ln_g", "sgu_ln_b", "w_spatial", "b_spatial",
          "w_sgu_proj", "w_out", "norm1_post_g", "norm2_pre_g", "w_ff1", "w_ff2", "norm2_post_g")


def _pack(keys, rows, parts):
    flat = jnp.concatenate([parts[k].reshape(-1).astype(F32) for k in keys])
    flat = jnp.pad(flat, (0, rows * LANES - flat.shape[0]))
    return flat.reshape(rows, LANES)


def _halves(a):
    return a.reshape(2, a.shape[0] // 2, a.shape[1])


def _step(p, m, v, x, target):
    t = x.shape[0]
    tm_mm = min(1024, t)
    tm_mix = min(256, t)
    tm_ffn = min(512, t)
    core = lax.axis_index("c").astype(jnp.int32).reshape(1)
    row = lambda a: a.reshape(1, -1)

    chip = (2 * lax.axis_index("x") + lax.axis_index("y")).astype(jnp.int32).reshape(1)

    local2d = {"w_in": p["w_in"], "w_ff1": p["w_ff1"], "w_ff2": p["w_ff2"], "w_out": p["w_out"],
               "w_pool": p["w_pool"].reshape(N_HEADS * 64, POOL_GROUP),
               "w_sgu_proj": p["w_sgu_proj"].reshape(N_HEADS * 64, HEAD)}
    shard = {k: _halves(local2d[k].astype(BF16)) for k in _BIG}
    landing = lambda keys: [lax.empty((N_SHARD,) + shard[k].shape, BF16) for k in keys]
    in_arrays = [shard["w_in"]] + landing(["w_in"])
    send_a, recv_a, in_arrays, token = _split_start("ag_in_send", in_arrays, 4, _plan_gather_send(1))
    z, xn = _inproj_own(x, row(p["norm1_pre_g"]), in_arrays[0].reshape(D, D_IN // N_SHARD), row(p["b_in"]), chip,
                        token, tm_mm)
    in_arrays = _split_wait("ag_in_land", send_a, recv_a, in_arrays, _plan_gather_send(1), z)
    send_b, recv_b, in_full, token = _split_start("ag_in_pass", in_arrays[1:], 3, _plan_gather_pass(1))
    mix_keys = _MIX[1:]
    mix_arrays = [shard[k] for k in mix_keys] + landing(mix_keys)
    send0, recv0, mix_arrays, token = _split_start("ag_mix_send", mix_arrays, 7 * len(mix_keys),
                                                   _plan_gather_direct(len(mix_keys)), after=token)
    ff_arrays = [shard[k] for k in _FF] + landing(_FF)
    send1, recv1, ff_arrays, token = _split_start("ag_ff_send", ff_arrays, 4 * len(_FF), _plan_gather_send(len(_FF)),
                                                  after=token)
    in_full = _split_wait("ag_in_passed", send_b, recv_b, in_full, _plan_gather_pass(1), token)
    w_in4 = in_full[0].reshape(N_SHARD, D, D_IN // N_SHARD)

    pos = jnp.arange(SGU_BLOCK) // 64
    mask = (pos[:, None] >= pos[None, :]).astype(F32)
    ws = (p["w_spatial"] * mask[None]).astype(BF16)
    ws_t = jnp.swapaxes(ws, 1, 2)
    bsp = p["b_spatial"].reshape(N_HEADS, SGU_BLOCK, 1)

    z = _inproj_rest(xn, w_in4, row(p["b_in"]), chip, z, tm_mm)
    mix_arrays = _split_wait("ag_mix_land", send0, recv0, mix_arrays, _plan_gather_direct(len(mix_keys)), z)
    full = dict(zip(mix_keys, mix_arrays[len(mix_keys):]))
    w_out = full["w_out"].reshape(D, D)
    regroup = lambda a: a.reshape(N_SHARD, N_HEADS, 64, 256).transpose(1, 0, 2, 3).reshape(N_HEADS, 256, 256)
    w_pool = regroup(full["w_pool"])
    w_proj = regroup(full["w_sgu_proj"])
    pooled, merged, y, h1 = _mix_fwd(z, x, w_pool, row(p["pool_scale"]), row(p["sgu_ln_g"]), row(p["sgu_ln_b"]),
                                     ws, bsp, w_proj, w_out, row(p["norm1_post_g"]), tm_mix)
    ff_arrays = _split_wait("ag_ff_land", send1, recv1, ff_arrays, _plan_gather_send(len(_FF)), pooled)
    send2, recv2, ff_full, token = _split_start("ag_ff_pass", ff_arrays[len(_FF):], 3 * len(_FF),
                                                _plan_gather_pass(len(_FF)))
    ff_full = _split_wait("ag_ff_passed", send2, recv2, ff_full, _plan_gather_pass(len(_FF)), token)
    w_ff1_4 = ff_full[0].reshape(N_SHARD, D, D)
    w_ff2 = ff_full[1].reshape(N_SHARD, D, D)
    hn, f1, f, df2, dout, vec_a = _ffn_fwd(h1, row(p["norm2_pre_g"]), w_ff1_4, w_ff2, row(p["norm2_post_g"]),
                                           target, tm_ffn)

    tk = min(1024, t)
    g_big = {"w_ff2": _dw(f, df2, "dw_ff2", 2, D, tk, N_SHARD, lambda i, j: i)}
    df1, dh1, dy, vec_b = _ffn_bwd(df2, f1, w_ff1_4, w_ff2, h1, dout, y, row(p["norm2_pre_g"]),
                                   row(p["norm1_post_g"]), tm_ffn)
    g_big["w_ff1"] = _dw(hn, df1, "dw_ff1", 2, D, tk, N_SHARD, lambda i, j: j)

    def reduce_start(name, keys, wholes):
        swapped = _swap_halves("rs_swap_" + name, [g_big[k] for k in keys], wholes)
        sums = _chip_sums(core, [g_big[k] for k in keys], swapped[:len(keys)], "chip_sums_" + name)
        arrays = list(sums)
        if wholes:
            arrays.append(_slot_sum(jnp.stack([wholes[0], swapped[-1]]), "chip_sum_small"))
        arrays += [lax.empty(s.shape, BF16) for s in sums]
        if wholes:
            arrays.append(lax.empty((N_SHARD,) + wholes[0].shape, F32))
        plan = _plan_exchange(len(keys), bool(wholes))
        send, recv, arrays, tok = _split_start("rs_send_" + name, arrays, 3 * len(keys) + 4 * len(wholes), plan)
        return send, recv, arrays, plan, tok

    def reduce_finish(name, keys, state, *after):
        send, recv, arrays, plan, _ = state
        arrays = _split_wait("rs_land_" + name, send, recv, arrays, plan, *after)
        n_in = len(arrays) // 2
        n = len(keys)
        finished = _final_sums(chip, arrays[:n], arrays[n_in:n_in + n], "final_sums_" + name)
        return dict(zip(keys, finished)), arrays

    by_half = lambda a, rows: a.reshape(N_SHARD, 2, rows // 2, a.shape[-1]).transpose(1, 0, 2, 3)
    g_big["w_out"] = _dw(merged, dy, "dw_out", 2 * N_SHARD, D, tk, N_SHARD, lambda i, j: 0)
    early = _FF + ("w_out",)
    late = ("w_in", "w_pool", "w_sgu_proj")
    early_state = reduce_start("early", early, [])
    dz, dwp, dwpr, dws, dbsp, vec_c, dbin = _mix_bwd(z, pooled, dy, w_pool, row(p["pool_scale"]), row(p["sgu_ln_g"]),
                                                     row(p["sgu_ln_b"]), ws, ws_t, bsp, w_proj, w_out, mask, tm_mix,
                                                     early_state[4])
    g_big["w_in"] = _dw(xn, dz, "dw_in", 2, D_IN // N_SHARD, tk, N_SHARD, lambda i, j: j)
    shard_major = lambda a: a.reshape(N_HEADS, N_SHARD, 64, 256).transpose(1, 0, 2, 3).reshape(N_SHARD * 256, 256)
    g_big["w_pool"] = by_half(shard_major(dwp).astype(BF16), 256)
    g_big["w_sgu_proj"] = by_half(shard_major(dwpr).astype(BF16), 256)
    halves, _ = reduce_finish("early", early, early_state, g_big["w_in"])

    small_part = _pack(_SMALL[1:], SMALL_ROWS - 8, {
        "b_in": dbin[0], "pool_scale": vec_c[0], "sgu_ln_g": vec_c[1], "sgu_ln_b": vec_c[2],
        "w_spatial": dws, "b_spatial": dbsp[:, ::LANES].T, "norm1_post_g": vec_b[1], "norm2_pre_g": vec_b[0],
        "norm2_post_g": vec_a[0]})
    late_state = reduce_start("late", late, [small_part])
    share = [halves[k] for k in early] + [lax.empty(halves[k].shape, F32) for k in early]
    send_s, recv_s, share, token = _split_start("rs_share_early_send", share, len(early), _plan_share(len(early)),
                                                after=late_state[4])
    grad_x, vec_d = _dx(dz, w_in4, x, dh1, row(p["norm1_pre_g"]), token, tm_mm)
    share = _split_wait("rs_share_early_land", send_s, recv_s, share, _plan_share(len(early)), grad_x)

    grads, delta, new_m, new_v = {}, {}, {}, {}

    def update(name, keys, own, shared):
        as2d = lambda a, k: a.reshape(local2d[k].shape)
        outs = _adamw_halves(core, own, shared, [local2d[k] for k in keys],
                             [as2d(m[k], k) for k in keys], [as2d(v[k], k) for k in keys], "adamw_" + name)
        for k, outs_k in zip(keys, outs):
            grads[k], delta[k], new_m[k], new_v[k] = (a.reshape(p[k].shape) for a in outs_k)

    update("early", early, share[:len(early)], share[len(early):])
    late_halves, late_arrays = reduce_finish("late", late, late_state, grad_x, *[new_v[k] for k in early])
    own_late = [late_halves[k] for k in late]
    shared_late, tiles = _share_and_all_reduce(
        own_late, jnp.concatenate([vec_d[0].reshape(8, LANES), jnp.broadcast_to(vec_a[1:2, :LANES], (8, LANES))]))
    g1_grad, loss = tiles[:8], tiles[8, 0]
    update("late", late, own_late, shared_late)
    as_rows = lambda d: [d[k].reshape(-1, LANES) for k in _SMALL]
    small_outs = _adamw_small([g1_grad, late_arrays[-1]], as_rows(p), as_rows(m), as_rows(v))
    for k, outs_k in zip(_SMALL, small_outs):
        grads[k], delta[k], new_m[k], new_v[k] = (a.reshape(p[k].shape) for a in outs_k)

    return (loss, grad_x, *[grads[k] for k in _ORDER], *[delta[k] for k in _ORDER],
            *[new_m[k] for k in _ORDER], *[new_v[k] for k in _ORDER])


def kernel(x, norm1_pre_g, w_in, b_in, w_pool, pool_scale, sgu_ln_g, sgu_ln_b, w_spatial, b_spatial, w_sgu_proj, w_out, norm1_post_g, norm2_pre_g, w_ff1, w_ff2, norm2_post_g, loss_target, m_norm1_pre_g, m_w_in, m_b_in, m_w_pool, m_pool_scale, m_sgu_ln_g, m_sgu_ln_b, m_w_spatial, m_b_spatial, m_w_sgu_proj, m_w_out, m_norm1_post_g, m_norm2_pre_g, m_w_ff1, m_w_ff2, m_norm2_post_g, v_norm1_pre_g, v_w_in, v_b_in, v_w_pool, v_pool_scale, v_sgu_ln_g, v_sgu_ln_b, v_w_spatial, v_b_spatial, v_w_sgu_proj, v_w_out, v_norm1_post_g, v_norm2_pre_g, v_w_ff1, v_w_ff2, v_norm2_post_g):
    p = dict(zip(_ORDER, (norm1_pre_g, w_in, b_in, w_pool, pool_scale, sgu_ln_g, sgu_ln_b, w_spatial, b_spatial,
                          w_sgu_proj, w_out, norm1_post_g, norm2_pre_g, w_ff1, w_ff2, norm2_post_g)))
    m = dict(zip(_ORDER, (m_norm1_pre_g, m_w_in, m_b_in, m_w_pool, m_pool_scale, m_sgu_ln_g, m_sgu_ln_b, m_w_spatial,
                          m_b_spatial, m_w_sgu_proj, m_w_out, m_norm1_post_g, m_norm2_pre_g, m_w_ff1, m_w_ff2,
                          m_norm2_post_g)))
    v = dict(zip(_ORDER, (v_norm1_pre_g, v_w_in, v_b_in, v_w_pool, v_pool_scale, v_sgu_ln_g, v_sgu_ln_b, v_w_spatial,
                          v_b_spatial, v_w_sgu_proj, v_w_out, v_norm1_post_g, v_norm2_pre_g, v_w_ff1, v_w_ff2,
                          v_norm2_post_g)))
    outs = _step(p, m, v, x[0], loss_target[0])
    return (outs[0], outs[1][None], *outs[2:])
```

```python
import math

import jax
import jax.numpy as jnp
from jax import lax
from jax.experimental import pallas as pl
from jax.experimental.pallas import tpu as pltpu

F32 = jnp.float32
BF16 = jnp.bfloat16

D = 1024
D_IN = 5 * D
D_FF = 4 * D
N_SHARD = 4
POOL_WINDOWS = (2, 4, 8, 16)
POOL_GROUP = 256
POOL_HALO = 16
SGU_BLOCK = 128
N_HEADS = 4
HEAD = 256
EPS = 1e-6

ADAM_LR = 0.001
ADAM_B1 = 0.9
ADAM_B2 = 0.999
ADAM_EPS = 1e-08
ADAM_WD = 0.01
ADAM_STEP = 10

V7X_VMEM_BYTES = 64 * 1024 * 1024
VMEM_LIMIT = V7X_VMEM_BYTES - 8 * 1024 * 1024
SMALL_ROWS = 616
LANES = 128

MESH = pl.DeviceIdType.MESH
ANY = pl.BlockSpec(memory_space=pl.ANY)
HBM = pl.BlockSpec(memory_space=pltpu.HBM)
SEM = pl.BlockSpec(memory_space=pltpu.SEMAPHORE)
EFFECT = pltpu.SideEffectType.DATAFLOW_SIDE_EFFECTING

NT_DIMS = (((1,), (1,)), ((), ()))
TN_DIMS = (((0,), (0,)), ((), ()))


def _params(n_axes):
    return pltpu.CompilerParams(dimension_semantics=("arbitrary",) * n_axes, vmem_limit_bytes=VMEM_LIMIT)


def _resident(shape, n_axes):
    zeros = (0,) * len(shape)
    index = (lambda i: zeros) if n_axes == 1 else (lambda i, j: zeros)
    return pl.BlockSpec(shape, index, pipeline_mode=pl.Buffered(1))


def _dot(a, b):
    return jnp.dot(a, b, preferred_element_type=F32)


def _dot_nt(a, b):
    return lax.dot_general(a, b, NT_DIMS, preferred_element_type=F32)


def _dot_tn(a, b):
    return lax.dot_general(a, b, TN_DIMS, preferred_element_type=F32)


def _mean(a):
    return jnp.mean(a, axis=-1, keepdims=True)


def _colsum(a):
    return jnp.sum(a, axis=0, keepdims=True)


_GELU_C0 = math.sqrt(2.0 / math.pi)
_GELU_C1 = 0.044715


def _gelu(a):
    t = jnp.tanh(a * (_GELU_C0 + (_GELU_C0 * _GELU_C1) * (a * a)))
    return a * (0.5 * t + 0.5)


def _gelu_and_grad(a):
    a2 = a * a
    t = jnp.tanh(a * (_GELU_C0 + (_GELU_C0 * _GELU_C1) * a2))
    cdf = 0.5 * t + 0.5
    grad = cdf + (0.5 * a) * (1.0 - t * t) * (_GELU_C0 + (3.0 * _GELU_C0 * _GELU_C1) * a2)
    return a * cdf, grad


def _sigmoid(a):
    return 0.5 * jnp.tanh(0.5 * a) + 0.5


def _rms_norm_bf16(a, g):
    return (a * lax.rsqrt(_mean(a * a) + EPS) * g).astype(BF16)


def _relu_sq_bf16(a):
    r = jnp.maximum(a, 0.0)
    return (r * r).astype(BF16)


def _window_sum(a, w, ahead):
    n = a.shape[0]
    step = 1
    while step < w:
        a = a + pltpu.roll(a, n - step if ahead else step, 0)
        step *= 2
    return a


def _rms_bwd(dn, n, r):
    return r * (dn - n * _mean(dn * n))


def _inproj_own(x, g1, w_own, b_in, chip, dep, tm):
    t = x.shape[0]
    cw = D_IN // N_SHARD

    def body(chip_ref, x_ref, g_ref, w_ref, b_ref, dep_ref, z_ref, xn_ref):
        xn = _rms_norm_bf16(x_ref[...], g_ref[...])
        xn_ref[...] = xn
        z_ref[...] = _dot(xn, w_ref[...]) + b_ref[...]

    return pl.pallas_call(
        body, name="inproj_own",
        grid_spec=pltpu.PrefetchScalarGridSpec(
            num_scalar_prefetch=1, grid=(t // tm,),
            in_specs=[pl.BlockSpec((tm, D), lambda i, chip: (i, 0)),
                      pl.BlockSpec((1, D), lambda i, chip: (0, 0)),
                      pl.BlockSpec((D, cw), lambda i, chip: (0, 0)),
                      pl.BlockSpec((1, cw), lambda i, chip: (0, chip[0])),
                      pl.BlockSpec(dep.shape, lambda i, chip: (0, 0))],
            out_specs=[pl.BlockSpec((tm, cw), lambda i, chip: (i, chip[0])),
                       pl.BlockSpec((tm, D), lambda i, chip: (i, 0))]),
        out_shape=[jax.ShapeDtypeStruct((t, D_IN), F32), jax.ShapeDtypeStruct((t, D), BF16)],
        compiler_params=_params(1),
    )(chip, x, g1, w_own, b_in, dep)


def _inproj_rest(xn, w_in4, b_in, chip, z, tm):
    t = xn.shape[0]
    cw = D_IN // N_SHARD
    shard_of = lambda s, chip: lax.rem(chip[0] + 1 + s, N_SHARD)

    def body(chip_ref, xn_ref, w_ref, b_ref, z_in_ref, z_ref):
        z_ref[...] = _dot(xn_ref[...], w_ref[shard_of(pl.program_id(1), chip_ref)]) + b_ref[...]

    return pl.pallas_call(
        body, name="inproj_rest",
        grid_spec=pltpu.PrefetchScalarGridSpec(
            num_scalar_prefetch=1, grid=(t // tm, N_SHARD - 1),
            in_specs=[pl.BlockSpec((tm, D), lambda i, s, chip: (i, 0)),
                      pl.BlockSpec((N_SHARD, D, cw), lambda i, s, chip: (0, 0, 0), pipeline_mode=pl.Buffered(1)),
                      pl.BlockSpec((1, cw), lambda i, s, chip: (0, shard_of(s, chip))),
                      ANY],
            out_specs=pl.BlockSpec((tm, cw), lambda i, s, chip: (i, shard_of(s, chip)))),
        out_shape=jax.ShapeDtypeStruct(z.shape, F32),
        input_output_aliases={4: 0},
        compiler_params=_params(2),
    )(chip, xn, w_in4, b_in, z)


def _sgu_forward(z_ref, lng_ref, lnb_ref, ws_ref, bsp_ref, sv_ref, tm, with_grad):
    if with_grad:
        u, du_dz = _gelu_and_grad(z_ref[:, D:2 * D])
        gv, dgv_dz = _gelu_and_grad(z_ref[:, 2 * D:3 * D])
    else:
        u = _gelu(z_ref[:, D:2 * D])
        gv = _gelu(z_ref[:, 2 * D:3 * D])
        du_dz = dgv_dz = None
    xc = gv - _mean(gv)
    rstd = lax.rsqrt(_mean(xc * xc) + EPS)
    vhat = xc * rstd
    vb = (vhat * lng_ref[...] + lnb_ref[...]).astype(BF16)
    for b in range(tm // SGU_BLOCK):
        rows = slice(b * SGU_BLOCK, (b + 1) * SGU_BLOCK)
        for h in range(N_HEADS):
            cols = slice(h * HEAD, (h + 1) * HEAD)
            sv_ref[rows, cols] = _dot(ws_ref[h], vb[rows, cols]) + bsp_ref[h]
    return u, du_dz, dgv_dz, vhat, rstd, vb


def _mix_fwd(z, x, w_pool, pool_scale, ln_g, ln_b, ws, bsp, w_proj, w_out, g1post, tm):
    t = x.shape[0]

    def body(z_ref, x_ref, wp_ref, psc_ref, lng_ref, lnb_ref, ws_ref, bsp_ref, wpr_ref, wo_ref, g1p_ref,
             pooled_ref, merged_ref, y_ref, h1_ref, win_ref, sv_ref):
        i = pl.program_id(0)

        @pl.when(i == 0)
        def _():
            win_ref[0:POOL_HALO, :] = jnp.zeros((POOL_HALO, D), F32)

        win_ref[POOL_HALO:POOL_HALO + tm, :] = z_ref[:, 0:D]
        pos1 = lax.broadcasted_iota(jnp.int32, (tm, 1), 0) + (i * tm + 1)
        a_parts = []
        for g, w in enumerate(POOL_WINDOWS):
            cols = slice(g * POOL_GROUP, (g + 1) * POOL_GROUP)
            ts = _window_sum(win_ref[:, cols], w, False)[POOL_HALO:POOL_HALO + tm, :]
            inv_cnt = 1.0 / jnp.minimum(pos1, w).astype(F32)
            pooled = (ts * inv_cnt - z_ref[:, cols]).astype(BF16)
            pooled_ref[:, cols] = pooled
            a_parts.append(_dot(pooled, wp_ref[g]) * psc_ref[:, cols])
        win_ref[0:POOL_HALO, :] = win_ref[tm:tm + POOL_HALO, :]

        u, _, _, _, _, _ = _sgu_forward(z_ref, lng_ref, lnb_ref, ws_ref, bsp_ref, sv_ref, tm, False)
        gated = (u * sv_ref[...]).astype(BF16)
        for h in range(N_HEADS):
            lo = h * HEAD
            bbr = _dot(gated[:, lo:lo + HEAD], wpr_ref[h])
            gate_a = _sigmoid(z_ref[:, 3 * D + lo:3 * D + lo + HEAD])
            gate_b = _sigmoid(z_ref[:, 4 * D + lo:4 * D + lo + HEAD])
            merged_ref[:, lo:lo + HEAD] = (gate_a * a_parts[h] + gate_b * bbr).astype(BF16)
        merged = merged_ref[...]
        y = _dot(merged, wo_ref[...])
        y_ref[...] = y
        r = lax.rsqrt(_mean(y * y) + EPS)
        h1_ref[...] = x_ref[...] + (y * r) * g1p_ref[...]

    row = lambda i: (i, 0)
    fixed2 = lambda i: (0, 0)
    fixed3 = lambda i: (0, 0, 0)
    vec = pl.BlockSpec((1, D), fixed2)
    return pl.pallas_call(
        body, name="mix_fwd",
        grid=(t // tm,),
        in_specs=[pl.BlockSpec((tm, D_IN), row), pl.BlockSpec((tm, D), row),
                  pl.BlockSpec((N_HEADS, POOL_GROUP, POOL_GROUP), fixed3), vec, vec, vec,
                  pl.BlockSpec((N_HEADS, SGU_BLOCK, SGU_BLOCK), fixed3),
                  pl.BlockSpec((N_HEADS, SGU_BLOCK, 1), fixed3),
                  pl.BlockSpec((N_HEADS, HEAD, HEAD), fixed3),
                  pl.BlockSpec((D, D), fixed2), vec],
        out_specs=[pl.BlockSpec((tm, D), row)] * 4,
        out_shape=[jax.ShapeDtypeStruct((t, D), BF16), jax.ShapeDtypeStruct((t, D), BF16),
                   jax.ShapeDtypeStruct((t, D), F32), jax.ShapeDtypeStruct((t, D), F32)],
        scratch_shapes=[pltpu.VMEM((tm + POOL_HALO, D), F32), pltpu.VMEM((tm, D), F32)],
        compiler_params=_params(1),
    )(z, x, w_pool, pool_scale, ln_g, ln_b, ws, bsp, w_proj, w_out, g1post)


def _ffn_fwd(h1, g2pre, w1, w2, g2post, target, tm):
    t = h1.shape[0]
    n_j = D_FF // D

    def body(h1_ref, g2_ref, w1_ref, w2_ref, g2p_ref, tgt_ref,
             hn_ref, f1_ref, f_ref, df2_ref, dout_ref, vec_ref, acc_ref):
        i, j = pl.program_id(0), pl.program_id(1)

        @pl.when((i == 0) & (j == 0))
        def _():
            vec_ref[...] = jnp.zeros_like(vec_ref)

        @pl.when(j == 0)
        def _():
            hn_ref[...] = _rms_norm_bf16(h1_ref[...], g2_ref[...])
            acc_ref[...] = jnp.zeros_like(acc_ref)

        f1 = _dot(hn_ref[...], w1_ref[j])
        f1_ref[...] = f1
        f = _relu_sq_bf16(f1)
        f_ref[...] = f
        acc_ref[...] += _dot(f, w2_ref[j])

        @pl.when(j == n_j - 1)
        def _():
            f2 = acc_ref[...]
            r = lax.rsqrt(_mean(f2 * f2) + EPS)
            n2 = f2 * r
            err = h1_ref[...] + n2 * g2p_ref[...] - tgt_ref[...]
            loss = 0.5 * jnp.sum(_mean(err * err))
            dout = err * (1.0 / D)
            dout_ref[...] = dout
            vec_ref[0:1, :] += _colsum(dout * n2)
            vec_ref[1:2, :] += jnp.full((1, D), loss, F32)
            df2_ref[...] = _rms_bwd(dout * g2p_ref[...], n2, r).astype(BF16)

    row = lambda i, j: (i, 0)
    fixed = lambda i, j: (0, 0)
    vec = pl.BlockSpec((1, D), fixed)
    return pl.pallas_call(
        body, name="ffn_fwd",
        grid=(t // tm, n_j),
        in_specs=[pl.BlockSpec((tm, D), row), vec,
                  _resident((n_j, D, D), 2), _resident((n_j, D, D), 2), vec,
                  pl.BlockSpec((tm, D), row)],
        out_specs=[pl.BlockSpec((tm, D), row),
                   pl.BlockSpec((tm, D), lambda i, j: (i, j)),
                   pl.BlockSpec((tm, D), lambda i, j: (i, j)),
                   pl.BlockSpec((tm, D), row), pl.BlockSpec((tm, D), row),
                   pl.BlockSpec((8, D), fixed)],
        out_shape=[jax.ShapeDtypeStruct((t, D), BF16), jax.ShapeDtypeStruct((t, D_FF), F32),
                   jax.ShapeDtypeStruct((t, D_FF), BF16), jax.ShapeDtypeStruct((t, D), BF16),
                   jax.ShapeDtypeStruct((t, D), F32), jax.ShapeDtypeStruct((8, D), F32)],
        scratch_shapes=[pltpu.VMEM((tm, D), F32)],
        compiler_params=_params(2),
    )(h1, g2pre, w1, w2, g2post, target)


def _ffn_bwd(df2, f1, w1, w2, h1, dout, y, g2pre, g1post, tm):
    t = h1.shape[0]
    n_j = D_FF // D

    def body(df2_ref, f1_ref, w1_ref, w2_ref, h1_ref, dout_ref, y_ref, g2_ref, g1p_ref,
             df1_ref, dh1_ref, dy_ref, vec_ref, acc_ref):
        i, j = pl.program_id(0), pl.program_id(1)

        @pl.when((i == 0) & (j == 0))
        def _():
            vec_ref[...] = jnp.zeros_like(vec_ref)

        @pl.when(j == 0)
        def _():
            acc_ref[...] = jnp.zeros_like(acc_ref)

        df = _dot_nt(df2_ref[...], w2_ref[j])
        df1 = (df * (2.0 * jnp.maximum(f1_ref[...], 0.0))).astype(BF16)
        df1_ref[...] = df1
        acc_ref[...] += _dot_nt(df1, w1_ref[j])

        @pl.when(j == n_j - 1)
        def _():
            dhn = acc_ref[...]
            h = h1_ref[...]
            r2 = lax.rsqrt(_mean(h * h) + EPS)
            nh = h * r2
            vec_ref[0:1, :] += _colsum(dhn * nh)
            dh1 = dout_ref[...] + _rms_bwd(dhn * g2_ref[...], nh, r2)
            dh1_ref[...] = dh1
            yv = y_ref[...]
            r1 = lax.rsqrt(_mean(yv * yv) + EPS)
            ny = yv * r1
            vec_ref[1:2, :] += _colsum(dh1 * ny)
            dy_ref[...] = _rms_bwd(dh1 * g1p_ref[...], ny, r1).astype(BF16)

    row = lambda i, j: (i, 0)
    fixed = lambda i, j: (0, 0)
    vec = pl.BlockSpec((1, D), fixed)
    return pl.pallas_call(
        body, name="ffn_bwd",
        grid=(t // tm, n_j),
        in_specs=[pl.BlockSpec((tm, D), row),
                  pl.BlockSpec((tm, D), lambda i, j: (i, j)),
                  _resident((n_j, D, D), 2), _resident((n_j, D, D), 2),
                  pl.BlockSpec((tm, D), row), pl.BlockSpec((tm, D), row), pl.BlockSpec((tm, D), row), vec, vec],
        out_specs=[pl.BlockSpec((tm, D), lambda i, j: (i, j)),
                   pl.BlockSpec((tm, D), row), pl.BlockSpec((tm, D), row),
                   pl.BlockSpec((8, D), fixed)],
        out_shape=[jax.ShapeDtypeStruct((t, D_FF), BF16), jax.ShapeDtypeStruct((t, D), F32),
                   jax.ShapeDtypeStruct((t, D), BF16), jax.ShapeDtypeStruct((8, D), F32)],
        scratch_shapes=[pltpu.VMEM((tm, D), F32)],
        compiler_params=_params(2),
    )(df2, f1, w1, w2, h1, dout, y, g2pre, g1post)


def _mix_bwd(z, pooled, dy, w_pool, pool_scale, ln_g, ln_b, ws, ws_t, bsp, w_proj, w_out, mask, tm, dep):
    t = z.shape[0]
    n_t = t // tm

    def body(z_ref, pooled_ref, dy_ref, wp_ref, psc_ref, lng_ref, lnb_ref, ws_ref, wst_ref, bsp_ref, wpr_ref,
             wo_ref, mask_ref, dep_ref,
             dz_ref, dwp_ref, dwpr_ref, dws_ref, dbsp_ref, vec_ref, dbin_ref,
             win_ref, sv_ref, dsv_ref, dv_ref, dsvsum_ref):
        i = pl.program_id(0)

        @pl.when(i == 0)
        def _():
            win_ref[tm:tm + POOL_HALO, :] = jnp.zeros((POOL_HALO, D), F32)
            dwp_ref[...] = jnp.zeros_like(dwp_ref)
            dwpr_ref[...] = jnp.zeros_like(dwpr_ref)
            dws_ref[...] = jnp.zeros_like(dws_ref)
            vec_ref[...] = jnp.zeros_like(vec_ref)
            dbin_ref[...] = jnp.zeros_like(dbin_ref)
            dsvsum_ref[...] = jnp.zeros_like(dsvsum_ref)

        dmerged = _dot_nt(dy_ref[...], wo_ref[...])

        u, du_dz, dgv_dz, vhat, rstd, vb = _sgu_forward(z_ref, lng_ref, lnb_ref, ws_ref, bsp_ref, sv_ref, tm, True)
        sv = sv_ref[...]
        gated = (u * sv).astype(BF16)

        sa = _sigmoid(z_ref[:, 3 * D:4 * D])
        sb = _sigmoid(z_ref[:, 4 * D:5 * D])
        da = dmerged * sa
        dbbr = (dmerged * sb).astype(BF16)

        tile = n_t - 1 - i
        pos1 = lax.broadcasted_iota(jnp.int32, (tm, 1), 0) + (tile * tm + 1)
        def put(k, lo, part):
            cols = slice(k * D + lo, k * D + lo + part.shape[1])
            dz_ref[:, cols] = part.astype(BF16)
            dbin_ref[0:1, cols] += _colsum(part)

        dpooled_parts = []
        for g, w in enumerate(POOL_WINDOWS):
            cols = slice(g * POOL_GROUP, (g + 1) * POOL_GROUP)
            pg = pooled_ref[:, cols]
            a_pre = _dot(pg, wp_ref[g])
            da_g = da[:, cols]
            psc = psc_ref[:, cols]
            vec_ref[0:1, cols] += _colsum(da_g * a_pre)
            sa_g = sa[:, cols]
            put(3, g * POOL_GROUP, dmerged[:, cols] * (a_pre * psc) * (sa_g * (1.0 - sa_g)))
            da_pre = (da_g * psc).astype(BF16)
            dwp_ref[g] += _dot_tn(pg, da_pre)
            dpooled = _dot_nt(da_pre, wp_ref[g])
            dpooled_parts.append(dpooled)
            inv_cnt = 1.0 / jnp.minimum(pos1, w).astype(F32)
            win_ref[0:tm, cols] = dpooled * inv_cnt
        for g, w in enumerate(POOL_WINDOWS):
            cols = slice(g * POOL_GROUP, (g + 1) * POOL_GROUP)
            acc = _window_sum(win_ref[:, cols], w, True)[0:tm, :]
            put(0, g * POOL_GROUP, acc - dpooled_parts[g])
        win_ref[tm:tm + POOL_HALO, :] = win_ref[0:POOL_HALO, :]

        for h in range(N_HEADS):
            cols = slice(h * HEAD, (h + 1) * HEAD)
            g_h = gated[:, cols]
            db_h = dbbr[:, cols]
            bbr_h = _dot(g_h, wpr_ref[h])
            sb_h = sb[:, cols]
            put(4, h * HEAD, dmerged[:, cols] * bbr_h * (sb_h * (1.0 - sb_h)))
            dwpr_ref[h] += _dot_tn(g_h, db_h)
            dgated_h = _dot_nt(db_h, wpr_ref[h])
            dsv_ref[:, cols] = dgated_h * u[:, cols]
            sv_ref[:, cols] = dgated_h * sv[:, cols]
        put(1, 0, sv_ref[...] * du_dz)
        dsv = dsv_ref[...]
        dsvb = dsv.astype(BF16)
        blk_sum = dsv[0:SGU_BLOCK, :]
        for b in range(1, tm // SGU_BLOCK):
            blk_sum = blk_sum + dsv[b * SGU_BLOCK:(b + 1) * SGU_BLOCK, :]
        dsvsum_ref[...] += blk_sum
        for b in range(tm // SGU_BLOCK):
            rows = slice(b * SGU_BLOCK, (b + 1) * SGU_BLOCK)
            for h in range(N_HEADS):
                cols = slice(h * HEAD, (h + 1) * HEAD)
                dws_ref[h] += _dot_nt(dsvb[rows, cols], vb[rows, cols])
                dv_ref[rows, cols] = _dot(wst_ref[h], dsvb[rows, cols])
        dv = dv_ref[...]
        vec_ref[1:2, :] += _colsum(dv * vhat)
        vec_ref[2:3, :] += _colsum(dv)
        dvhat = dv * lng_ref[...]
        dgv = rstd * (dvhat - _mean(dvhat) - vhat * _mean(dvhat * vhat))
        put(2, 0, dgv * dgv_dz)

        @pl.when(i == n_t - 1)
        def _():
            for h in range(N_HEADS):
                dws_ref[h] = jnp.where(mask_ref[...] > 0.0, dws_ref[h], 0.0)
                tot = jnp.sum(dsvsum_ref[:, h * HEAD:(h + 1) * HEAD], axis=1, keepdims=True)
                dbsp_ref[:, h * LANES:(h + 1) * LANES] = jnp.broadcast_to(tot, (SGU_BLOCK, LANES))

    row = lambda i: (n_t - 1 - i, 0)
    fixed2 = lambda i: (0, 0)
    fixed3 = lambda i: (0, 0, 0)
    vec = pl.BlockSpec((1, D), fixed2)
    sq = pl.BlockSpec((N_HEADS, SGU_BLOCK, SGU_BLOCK), fixed3)
    grp = pl.BlockSpec((N_HEADS, HEAD, HEAD), fixed3)
    return pl.pallas_call(
        body, name="mix_bwd",
        grid=(n_t,),
        in_specs=[pl.BlockSpec((tm, D_IN), row), pl.BlockSpec((tm, D), row), pl.BlockSpec((tm, D), row),
                  grp, vec, vec, vec, sq, sq, pl.BlockSpec((N_HEADS, SGU_BLOCK, 1), fixed3), grp,
                  pl.BlockSpec((D, D), fixed2), pl.BlockSpec((SGU_BLOCK, SGU_BLOCK), fixed2),
                  pl.BlockSpec(dep.shape, fixed2)],
        out_specs=[pl.BlockSpec((tm, D_IN), row), grp, grp, sq,
                   pl.BlockSpec((SGU_BLOCK, N_HEADS * LANES), fixed2),
                   pl.BlockSpec((8, D), fixed2), pl.BlockSpec((8, D_IN), fixed2)],
        out_shape=[jax.ShapeDtypeStruct((t, D_IN), BF16),
                   jax.ShapeDtypeStruct((N_HEADS, HEAD, HEAD), F32), jax.ShapeDtypeStruct((N_HEADS, HEAD, HEAD), F32),
                   jax.ShapeDtypeStruct((N_HEADS, SGU_BLOCK, SGU_BLOCK), F32),
                   jax.ShapeDtypeStruct((SGU_BLOCK, N_HEADS * LANES), F32),
                   jax.ShapeDtypeStruct((8, D), F32), jax.ShapeDtypeStruct((8, D_IN), F32)],
        scratch_shapes=[pltpu.VMEM((tm + POOL_HALO, D), F32), pltpu.VMEM((tm, D), F32), pltpu.VMEM((tm, D), F32),
                        pltpu.VMEM((tm, D), F32), pltpu.VMEM((SGU_BLOCK, D), F32)],
        compiler_params=_params(1),
    )(z, pooled, dy, w_pool, pool_scale, ln_g, ln_b, ws, ws_t, bsp, w_proj, w_out, mask, dep)


def _dx(dz, w_in4, x, dh1, g1, dep, tm):
    t = x.shape[0]
    cw = D_IN // N_SHARD

    def body(dz_ref, w_ref, x_ref, dh1_ref, g_ref, dep_ref, gx_ref, vec_ref):
        @pl.when(pl.program_id(0) == 0)
        def _():
            vec_ref[...] = jnp.zeros_like(vec_ref)

        dxn = _dot_nt(dz_ref[:, 0:cw], w_ref[0])
        for s in range(1, N_SHARD):
            dxn = dxn + _dot_nt(dz_ref[:, s * cw:(s + 1) * cw], w_ref[s])
        xv = x_ref[...]
        r = lax.rsqrt(_mean(xv * xv) + EPS)
        xh = xv * r
        vec_ref[0:1, :] += _colsum(dxn * xh)
        gx_ref[...] = dh1_ref[...] + _rms_bwd(dxn * g_ref[...], xh, r)

    row = lambda i: (i, 0)
    fixed = lambda i: (0, 0)
    return pl.pallas_call(
        body, name="dx",
        grid=(t // tm,),
        in_specs=[pl.BlockSpec((tm, D_IN), row),
                  _resident((N_SHARD, D, cw), 1),
                  pl.BlockSpec((tm, D), row), pl.BlockSpec((tm, D), row), pl.BlockSpec((1, D), fixed),
                  pl.BlockSpec(dep.shape, fixed)],
        out_specs=[pl.BlockSpec((tm, D), row), pl.BlockSpec((8, D), fixed)],
        out_shape=[jax.ShapeDtypeStruct((t, D), F32), jax.ShapeDtypeStruct((8, D), F32)],
        compiler_params=_params(1),
    )(dz, w_in4, x, dh1, g1, dep)


def _dw(a, b, name, n_split, tn, tk, n_out, out_block):
    t, m = a.shape
    n = b.shape[1]
    tm = min(m, D)
    rows = tm // n_split
    n_k = t // tk

    def body(a_ref, b_ref, o_ref, acc_ref):
        k = pl.program_id(2)

        @pl.when(k == 0)
        def _():
            acc_ref[...] = jnp.zeros_like(acc_ref)

        acc_ref[...] += _dot_tn(a_ref[...], b_ref[...])

        @pl.when(k == n_k - 1)
        def _():
            for q in range(n_split):
                o_ref[q % 2, q // 2] = acc_ref[q * rows:(q + 1) * rows, :].astype(BF16)

    return pl.pallas_call(
        body, name=name,
        grid=(m // tm, n // tn, n_k),
        in_specs=[pl.BlockSpec((tk, tm), lambda i, j, k: (k, i)),
                  pl.BlockSpec((tk, tn), lambda i, j, k: (k, j))],
        out_specs=pl.BlockSpec((2, n_split // 2, rows, tn), lambda i, j, k: (0, out_block(i, j), 0, 0)),
        out_shape=jax.ShapeDtypeStruct((2, n_out, rows, tn), BF16),
        scratch_shapes=[pltpu.VMEM((tm, tn), F32)],
        compiler_params=_params(3),
    )(a, b)


def _row_block(rows, cols, itemsize, n_bufs):
    budget = VMEM_LIMIT // 4
    rb = rows
    while rb % 16 == 0 and rb * cols * itemsize * n_bufs * 2 > budget:
        rb //= 2
    return rb


ROW_STEPS = 2


def _chip_sums(core, grads, recvs, name):
    n = len(grads)

    def body(c_ref, *refs):
        for w in range(n):
            refs[2 * n + w][...] = (refs[w][0].astype(F32) + refs[n + w][...].astype(F32)).astype(BF16)

    def blocks(g):
        _, _, rh, cols = g.shape
        rb = rh // ROW_STEPS
        return (pl.BlockSpec((1, 1, rb, cols), lambda s, r, c: (c[0], s, r, 0)),
                pl.BlockSpec((1, rb, cols), lambda s, r, c: (s, r, 0)))

    specs = [blocks(g) for g in grads]
    return pl.pallas_call(
        body, name=name,
        grid_spec=pltpu.PrefetchScalarGridSpec(
            num_scalar_prefetch=1, grid=(N_SHARD, ROW_STEPS),
            in_specs=[s[0] for s in specs] + [s[1] for s in specs],
            out_specs=[s[1] for s in specs]),
        out_shape=[jax.ShapeDtypeStruct(g.shape[1:], BF16) for g in grads],
        compiler_params=_params(2),
    )(core, *grads, *recvs)


def _slot_sum(slots, name):
    n_s, rows, cols = slots.shape
    rb = _row_block(rows, cols, 4, n_s + 1)

    def body(s_ref, o_ref):
        acc = s_ref[0].astype(F32)
        for k in range(1, n_s):
            acc = acc + s_ref[k].astype(F32)
        o_ref[...] = acc

    return pl.pallas_call(
        body, name=name,
        grid=(rows // rb,),
        in_specs=[pl.BlockSpec((n_s, rb, cols), lambda r: (0, r, 0))],
        out_specs=pl.BlockSpec((rb, cols), lambda r: (r, 0)),
        out_shape=jax.ShapeDtypeStruct((rows, cols), F32),
        compiler_params=_params(1),
    )(slots)


def _adamw_update(g, w, m, v):
    m_new = ADAM_B1 * m + (1.0 - ADAM_B1) * g
    v_new = ADAM_B2 * v + (1.0 - ADAM_B2) * (g * g)
    m_hat = m_new / (1.0 - ADAM_B1 ** ADAM_STEP)
    v_hat = v_new / (1.0 - ADAM_B2 ** ADAM_STEP)
    return -ADAM_LR * (m_hat / (jnp.sqrt(v_hat) + ADAM_EPS) + ADAM_WD * w), m_new, v_new


def _adamw_small(grad_parts, ws, ms, vs):
    n, n_g = len(ws), len(grad_parts)

    def rows_of(ref, lo, rows):
        if len(ref.shape) == 2:
            return ref[lo:lo + rows, :]
        acc = ref[0, lo:lo + rows, :]
        for s in range(1, ref.shape[0]):
            acc = acc + ref[s, lo:lo + rows, :]
        return acc

    def body(*refs):
        g_refs, ins, outs = refs[:n_g], refs[n_g:n_g + 3 * n], refs[n_g + 3 * n:]
        part, off = 0, 0
        for k in range(n):
            rows = ws[k].shape[0]
            if off + rows > grad_parts[part].shape[-2]:
                part, off = part + 1, 0
            g = rows_of(g_refs[part], off, rows)
            off += rows
            delta, m_new, v_new = _adamw_update(g, ins[k][...], ins[n + k][...], ins[2 * n + k][...])
            for o, val in zip(outs[4 * k:4 * k + 4], (g, delta, m_new, v_new)):
                o[...] = val

    vmem = pl.BlockSpec(memory_space=pltpu.VMEM)
    outs = pl.pallas_call(
        body, name="adamw_small",
        in_specs=[vmem] * (n_g + 3 * n), out_specs=[vmem] * (4 * n),
        out_shape=[jax.ShapeDtypeStruct(w.shape, F32) for w in ws for _ in range(4)],
    )(*grad_parts, *ws, *ms, *vs)
    return [outs[4 * k:4 * k + 4] for k in range(n)]


def _final_sums(chip, owns, slots, name):
    n = len(owns)

    def body(chip_ref, *refs):
        for w in range(n):
            own_ref, s1_ref, s2_ref, s3_ref = refs[4 * w:4 * w + 4]
            acc = own_ref[0].astype(F32) + s1_ref[0].astype(F32)
            acc = acc + s2_ref[0].astype(F32)
            refs[4 * n + w][...] = acc + s3_ref[0].astype(F32)

    def slot(a, k):
        n_s, rh, cols = a.shape
        return pl.BlockSpec((1, rh // ROW_STEPS, cols), lambda r, chip: (lax.rem(chip[0] + k, n_s), r, 0))

    operands, in_specs = [], []
    for own, slots_w in zip(owns, slots):
        operands += [own, slots_w, slots_w, slots_w]
        in_specs += [slot(own, k) for k in range(4)]
    return pl.pallas_call(
        body, name=name,
        grid_spec=pltpu.PrefetchScalarGridSpec(
            num_scalar_prefetch=1, grid=(ROW_STEPS,),
            in_specs=in_specs,
            out_specs=[pl.BlockSpec((a.shape[1] // ROW_STEPS, a.shape[2]), lambda r, chip: (r, 0)) for a in owns]),
        out_shape=[jax.ShapeDtypeStruct(a.shape[1:], F32) for a in owns],
        compiler_params=_params(1),
    )(chip, *operands)


def _adamw_halves(core, owns, siblings, ws, ms, vs, name):
    n = len(ws)

    def body(c_ref, *refs):
        ins, outs = refs[:5 * n], refs[5 * n:]
        mine = c_ref[0] == pl.program_id(0)
        for k in range(n):
            own_ref, sib_ref, w_ref, m_ref, v_ref = ins[5 * k:5 * k + 5]
            g_ref, d_ref, mo_ref, vo_ref = outs[4 * k:4 * k + 4]

            @pl.when(mine)
            def _():
                g_ref[...] = own_ref[...]

            @pl.when(jnp.logical_not(mine))
            def _():
                g_ref[...] = sib_ref[...]

            d_ref[...], mo_ref[...], vo_ref[...] = _adamw_update(g_ref[...], w_ref[...], m_ref[...], v_ref[...])

    operands, in_specs, out_specs = [], [], []
    for own, sib, w, m, v in zip(owns, siblings, ws, ms, vs):
        rows, cols = w.shape
        rb = rows // 2 // ROW_STEPS
        half = pl.BlockSpec((rb, cols), lambda h, r, c: (r, 0))
        whole = pl.BlockSpec((rb, cols), lambda h, r, c: (h * ROW_STEPS + r, 0))
        operands += [own, sib, w, m, v]
        in_specs += [half, half, whole, whole, whole]
        out_specs += [whole] * 4
    outs = pl.pallas_call(
        body, name=name,
        grid_spec=pltpu.PrefetchScalarGridSpec(
            num_scalar_prefetch=1, grid=(2, ROW_STEPS), in_specs=in_specs, out_specs=out_specs),
        out_shape=[jax.ShapeDtypeStruct(w.shape, F32) for w in ws for _ in range(4)],
        compiler_params=_params(2),
    )(core, *operands)
    return [outs[4 * k:4 * k + 4] for k in range(n)]


def _place():
    x, y, c = lax.axis_index("x"), lax.axis_index("y"), lax.axis_index("c")
    chips = [(1 - x, y), (x, 1 - y), (1 - x, 1 - y)]
    return x, y, c, chips


def _remote(src, dst, send_sem, recv_sem, device):
    return pltpu.make_async_remote_copy(src_ref=src, dst_ref=dst, send_sem=send_sem, recv_sem=recv_sem,
                                        device_id=device, device_id_type=MESH)


def _split_start(name, arrays, n_copies, plan, after=None):
    n = len(arrays)
    n_in = n + (after is not None)

    def body(*refs):
        ins, send_sem, recv_sem, token = refs[:n], refs[n_in], refs[n_in + 1], refs[-1]
        for k, (src, dst, device) in enumerate(plan(ins)):
            _remote(src, dst, send_sem.at[k], recv_sem.at[k], device).start()
        token[...] = jnp.zeros_like(token)

    outs = pl.pallas_call(
        body, name=name,
        in_specs=[HBM] * n + [ANY] * (n_in - n),
        out_specs=(SEM, SEM, *[HBM] * n, pl.BlockSpec(memory_space=pltpu.VMEM)),
        out_shape=(pltpu.SemaphoreType.DMA((n_copies,)), pltpu.SemaphoreType.DMA((n_copies,)),
                   *[pltpu.HBM(a.shape, a.dtype) for a in arrays], jax.ShapeDtypeStruct((8, LANES), F32)),
        input_output_aliases={i: i + 2 for i in range(n)},
        compiler_params=pltpu.CompilerParams(has_side_effects=EFFECT),
    )(*[pltpu.with_memory_space_constraint(a, pltpu.HBM) for a in arrays], *([] if after is None else [after]))
    return outs[0], outs[1], list(outs[2:2 + n]), outs[-1]


def _split_wait(name, send_sem, recv_sem, arrays, plan, *after):
    n = len(arrays)

    def body(*refs):
        ins, send_ref, recv_ref = refs[:n], refs[n], refs[n + 1]
        for k, (src, dst, device) in enumerate(plan(ins)):
            cp = _remote(src, dst, send_ref.at[k], recv_ref.at[k], device)
            cp.wait_send()
            cp.wait_recv()

    outs = pl.pallas_call(
        body, name=name,
        in_specs=[HBM] * n + [SEM, SEM] + [ANY] * len(after), out_specs=[HBM] * n,
        out_shape=[pltpu.HBM(a.shape, a.dtype) for a in arrays],
        input_output_aliases={i: i for i in range(n)},
        compiler_params=pltpu.CompilerParams(has_side_effects=EFFECT),
    )(*arrays, send_sem, recv_sem, *after)
    return list(outs)


def _plan_gather_send(n):
    def plan(refs):
        x, y, c, chips = _place()
        me = 2 * x + y
        copies = []
        for w in range(n):
            own, full = refs[w], refs[n + w]
            copies.append((own, full.at[me], (x, y, 1 - c)))
            copies += [(own.at[c], full.at[me, c], (px, py, c)) for px, py in chips]
        return copies
    return plan


def _plan_gather_direct(n):
    def plan(refs):
        x, y, c, chips = _place()
        me = 2 * x + y
        copies = []
        for w in range(n):
            own, full = refs[w], refs[n + w]
            copies.append((own, full.at[me], (x, y, 1 - c)))
            for px, py in chips:
                copies += [(own.at[c], full.at[me, c], (px, py, c)), (own.at[c], full.at[me, c], (px, py, 1 - c))]
        return copies
    return plan


def _plan_gather_pass(n):
    def plan(refs):
        x, y, c, chips = _place()
        copies = []
        for w in range(n):
            for px, py in chips:
                landed = refs[w].at[2 * px + py, c]
                copies.append((landed, landed, (x, y, 1 - c)))
        return copies
    return plan


def _plan_share(n):
    def plan(refs):
        x, y, c, _ = _place()
        return [(refs[w], refs[n + w], (x, y, 1 - c)) for w in range(n)]
    return plan


def _plan_exchange(n, with_small):
    def plan(refs):
        x, y, c, chips = _place()
        me = 2 * x + y
        n_in = n + int(with_small)
        copies = []
        for w in range(n):
            copies += [(refs[w].at[2 * px + py], refs[n_in + w].at[me], (px, py, c)) for px, py in chips]
        if with_small:
            small, slots = refs[n], refs[n_in + n]
            copies.append((small, slots.at[me], (x, y, 1 - c)))
            copies += [(small, slots.at[me], (px, py, c)) for px, py in chips]
        return copies
    return plan


def _swap_halves(name, grads, wholes):
    n = len(grads)
    arrays = list(grads) + list(wholes)
    n_all = len(arrays)

    def body(*refs):
        ins, outs = refs[:n_all], refs[n_all:2 * n_all]
        send_sem, recv_sem = refs[2 * n_all:]
        x, y, c, _ = _place()
        sibling = (x, y, 1 - c)
        copies = []
        for w in range(n_all):
            src = ins[w].at[1 - c] if w < n else ins[w]
            cp = _remote(src, outs[w], send_sem.at[w], recv_sem.at[w], sibling)
            cp.start()
            copies.append(cp)
        for cp in copies:
            cp.wait()

    return pl.pallas_call(
        body, name=name,
        in_specs=[ANY] * n_all, out_specs=[ANY] * n_all,
        out_shape=[jax.ShapeDtypeStruct(g.shape[1:], g.dtype) for g in grads]
        + [jax.ShapeDtypeStruct(a.shape, a.dtype) for a in wholes],
        scratch_shapes=[pltpu.SemaphoreType.DMA((n_all,))] * 2,
    )(*arrays)


def _share_and_all_reduce(wholes, part):
    n = len(wholes)
    n_dev = 8

    def body(*refs):
        ins, part_ref, outs, sum_ref = refs[:n], refs[n], refs[n + 1:2 * n + 1], refs[2 * n + 1]
        slots_ref, send_sem, recv_sem, send_tile, recv_tile = refs[2 * n + 2:]
        x, y, c, _ = _place()
        me = 4 * x + 2 * y + c
        copies = []
        for w in range(n):
            copies.append(_remote(ins[w], outs[w], send_sem.at[w], recv_sem.at[w], (x, y, 1 - c)))
        slots_ref[me] = part_ref[...]
        flips = [(fx, fy, fc) for fx in (0, 1) for fy in (0, 1) for fc in (0, 1)][1:]
        for k, (fx, fy, fc) in enumerate(flips):
            peer = (x + fx - 2 * x * fx, y + fy - 2 * y * fy, c + fc - 2 * c * fc)
            copies.append(_remote(part_ref, slots_ref.at[me], send_tile.at[k], recv_tile.at[k], peer))
        for cp in copies:
            cp.start()
        for cp in copies:
            cp.wait()
        acc = slots_ref[0]
        for k in range(1, n_dev):
            acc = acc + slots_ref[k]
        sum_ref[...] = acc

    vmem = pl.BlockSpec(memory_space=pltpu.VMEM)
    outs = pl.pallas_call(
        body, name="rs_share_late",
        in_specs=[ANY] * n + [vmem], out_specs=[ANY] * n + [vmem],
        out_shape=[jax.ShapeDtypeStruct(a.shape, a.dtype) for a in wholes] + [jax.ShapeDtypeStruct(part.shape, F32)],
        scratch_shapes=[pltpu.VMEM((n_dev,) + part.shape, F32),
                        pltpu.SemaphoreType.DMA((n,)), pltpu.SemaphoreType.DMA((n,)),
                        pltpu.SemaphoreType.DMA((n_dev - 1,)), pltpu.SemaphoreType.DMA((n_dev - 1,))],
    )(*wholes, part)
    return outs[:n], outs[n]


_SMALL = ("norm1_pre_g", "b_in", "pool_scale", "sgu_ln_g", "sgu_ln_b", "w_spatial",
          "norm1_post_g", "norm2_pre_g", "norm2_post_g", "b_spatial")
_MIX = ("w_in", "w_pool", "w_sgu_proj", "w_out")
_FF = ("w_ff1", "w_ff2")
_BIG = _MIX + _FF
_ORDER = ("norm1_pre_g", "w_in", "b_in", "w_pool", "pool_scale", "sgu_ln_g", "sgu_ln_b", "w_spatial", "b_spatial",
          "w_sgu_proj", "w_out", "norm1_post_g", "norm2_pre_g", "w_ff1", "w_ff2", "norm2_post_g")


def _pack(keys, rows, parts):
    flat = jnp.concatenate([parts[k].reshape(-1).astype(F32) for k in keys])
    flat = jnp.pad(flat, (0, rows * LANES - flat.shape[0]))
    return flat.reshape(rows, LANES)


def _halves(a):
    return a.reshape(2, a.shape[0] // 2, a.shape[1])


def _step(p, m, v, x, target):
    t = x.shape[0]
    tm_mm = min(1024, t)
    tm_mix = min(256, t)
    tm_ffn = min(512, t)
    core = lax.axis_index("c").astype(jnp.int32).reshape(1)
    row = lambda a: a.reshape(1, -1)

    chip = (2 * lax.axis_index("x") + lax.axis_index("y")).astype(jnp.int32).reshape(1)

    local2d = {"w_in": p["w_in"], "w_ff1": p["w_ff1"], "w_ff2": p["w_ff2"], "w_out": p["w_out"],
               "w_pool": p["w_pool"].reshape(N_HEADS * 64, POOL_GROUP),
               "w_sgu_proj": p["w_sgu_proj"].reshape(N_HEADS * 64, HEAD)}
    shard = {k: _halves(local2d[k].astype(BF16)) for k in _BIG}
    landing = lambda keys: [lax.empty((N_SHARD,) + shard[k].shape, BF16) for k in keys]
    in_arrays = [shard["w_in"]] + landing(["w_in"])
    send_a, recv_a, in_arrays, token = _split_start("ag_in_send", in_arrays, 4, _plan_gather_send(1))
    z, xn = _inproj_own(x, row(p["norm1_pre_g"]), in_arrays[0].reshape(D, D_IN // N_SHARD), row(p["b_in"]), chip,
                        token, tm_mm)
    in_arrays = _split_wait("ag_in_land", send_a, recv_a, in_arrays, _plan_gather_send(1), z)
    send_b, recv_b, in_full, token = _split_start("ag_in_pass", in_arrays[1:], 3, _plan_gather_pass(1))
    mix_keys = _MIX[1:]
    mix_arrays = [shard[k] for k in mix_keys] + landing(mix_keys)
    send0, recv0, mix_arrays, token = _split_start("ag_mix_send", mix_arrays, 7 * len(mix_keys),
                                                   _plan_gather_direct(len(mix_keys)), after=token)
    ff_arrays = [shard[k] for k in _FF] + landing(_FF)
    send1, recv1, ff_arrays, token = _split_start("ag_ff_send", ff_arrays, 4 * len(_FF), _plan_gather_send(len(_FF)),
                                                  after=token)
    in_full = _split_wait("ag_in_passed", send_b, recv_b, in_full, _plan_gather_pass(1), token)
    w_in4 = in_full[0].reshape(N_SHARD, D, D_IN // N_SHARD)

    pos = jnp.arange(SGU_BLOCK) // 64
    mask = (pos[:, None] >= pos[None, :]).astype(F32)
    ws = (p["w_spatial"] * mask[None]).astype(BF16)
    ws_t = jnp.swapaxes(ws, 1, 2)
    bsp = p["b_spatial"].reshape(N_HEADS, SGU_BLOCK, 1)

    z = _inproj_rest(xn, w_in4, row(p["b_in"]), chip, z, tm_mm)
    mix_arrays = _split_wait("ag_mix_land", send0, recv0, mix_arrays, _plan_gather_direct(len(mix_keys)), z)
    full = dict(zip(mix_keys, mix_arrays[len(mix_keys):]))
    w_out = full["w_out"].reshape(D, D)
    regroup = lambda a: a.reshape(N_SHARD, N_HEADS, 64, 256).transpose(1, 0, 2, 3).reshape(N_HEADS, 256, 256)
    w_pool = regroup(full["w_pool"])
    w_proj = regroup(full["w_sgu_proj"])
    pooled, merged, y, h1 = _mix_fwd(z, x, w_pool, row(p["pool_scale"]), row(p["sgu_ln_g"]), row(p["sgu_ln_b"]),
                                     ws, bsp, w_proj, w_out, row(p["norm1_post_g"]), tm_mix)
    ff_arrays = _split_wait("ag_ff_land", send1, recv1, ff_arrays, _plan_gather_send(len(_FF)), pooled)
    send2, recv2, ff_full, token = _split_start("ag_ff_pass", ff_arrays[len(_FF):], 3 * len(_FF),
                                                _plan_gather_pass(len(_FF)))
    ff_full = _split_wait("ag_ff_passed", send2, recv2, ff_full, _plan_gather_pass(len(_FF)), token)
    w_ff1_4 = ff_full[0].reshape(N_SHARD, D, D)
    w_ff2 = ff_full[1].reshape(N_SHARD, D, D)
    hn, f1, f, df2, dout, vec_a = _ffn_fwd(h1, row(p["norm2_pre_g"]), w_ff1_4, w_ff2, row(p["norm2_post_g"]),
                                           target, tm_ffn)

    tk = min(2048, t)
    g_big = {"w_ff2": _dw(f, df2, "dw_ff2", 2, D, tk, N_SHARD, lambda i, j: i)}
    df1, dh1, dy, vec_b = _ffn_bwd(df2, f1, w_ff1_4, w_ff2, h1, dout, y, row(p["norm2_pre_g"]),
                                   row(p["norm1_post_g"]), tm_ffn)
    g_big["w_ff1"] = _dw(hn, df1, "dw_ff1", 2, D, tk, N_SHARD, lambda i, j: j)

    def reduce_start(name, keys, wholes):
        swapped = _swap_halves("rs_swap_" + name, [g_big[k] for k in keys], wholes)
        sums = _chip_sums(core, [g_big[k] for k in keys], swapped[:len(keys)], "chip_sums_" + name)
        arrays = list(sums)
        if wholes:
            arrays.append(_slot_sum(jnp.stack([wholes[0], swapped[-1]]), "chip_sum_small"))
        arrays += [lax.empty(s.shape, BF16) for s in sums]
        if wholes:
            arrays.append(lax.empty((N_SHARD,) + wholes[0].shape, F32))
        plan = _plan_exchange(len(keys), bool(wholes))
        send, recv, arrays, tok = _split_start("rs_send_" + name, arrays, 3 * len(keys) + 4 * len(wholes), plan)
        return send, recv, arrays, plan, tok

    def reduce_finish(name, keys, state, *after):
        send, recv, arrays, plan, _ = state
        arrays = _split_wait("rs_land_" + name, send, recv, arrays, plan, *after)
        n_in = len(arrays) // 2
        n = len(keys)
        finished = _final_sums(chip, arrays[:n], arrays[n_in:n_in + n], "final_sums_" + name)
        return dict(zip(keys, finished)), arrays

    by_half = lambda a, rows: a.reshape(N_SHARD, 2, rows // 2, a.shape[-1]).transpose(1, 0, 2, 3)
    g_big["w_out"] = _dw(merged, dy, "dw_out", 2 * N_SHARD, D, tk, N_SHARD, lambda i, j: 0)
    early = _FF + ("w_out",)
    late = ("w_in", "w_pool", "w_sgu_proj")
    early_state = reduce_start("early", early, [])
    dz, dwp, dwpr, dws, dbsp, vec_c, dbin = _mix_bwd(z, pooled, dy, w_pool, row(p["pool_scale"]), row(p["sgu_ln_g"]),
                                                     row(p["sgu_ln_b"]), ws, ws_t, bsp, w_proj, w_out, mask, tm_mix,
                                                     early_state[4])
    g_big["w_in"] = _dw(xn, dz, "dw_in", 2, D_IN // N_SHARD, tk, N_SHARD, lambda i, j: j)
    shard_major = lambda a: a.reshape(N_HEADS, N_SHARD, 64, 256).transpose(1, 0, 2, 3).reshape(N_SHARD * 256, 256)
    g_big["w_pool"] = by_half(shard_major(dwp).astype(BF16), 256)
    g_big["w_sgu_proj"] = by_half(shard_major(dwpr).astype(BF16), 256)
    halves, _ = reduce_finish("early", early, early_state, g_big["w_in"])

    small_part = _pack(_SMALL[1:], SMALL_ROWS - 8, {
        "b_in": dbin[0], "pool_scale": vec_c[0], "sgu_ln_g": vec_c[1], "sgu_ln_b": vec_c[2],
        "w_spatial": dws, "b_spatial": dbsp[:, ::LANES].T, "norm1_post_g": vec_b[1], "norm2_pre_g": vec_b[0],
        "norm2_post_g": vec_a[0]})
    late_state = reduce_start("late", late, [small_part])
    share = [halves[k] for k in early] + [lax.empty(halves[k].shape, F32) for k in early]
    send_s, recv_s, share, token = _split_start("rs_share_early_send", share, len(early), _plan_share(len(early)),
                                                after=late_state[4])
    grad_x, vec_d = _dx(dz, w_in4, x, dh1, row(p["norm1_pre_g"]), token, tm_ffn)
    share = _split_wait("rs_share_early_land", send_s, recv_s, share, _plan_share(len(early)), grad_x)

    grads, delta, new_m, new_v = {}, {}, {}, {}

    def update(name, keys, own, shared):
        as2d = lambda a, k: a.reshape(local2d[k].shape)
        outs = _adamw_halves(core, own, shared, [local2d[k] for k in keys],
                             [as2d(m[k], k) for k in keys], [as2d(v[k], k) for k in keys], "adamw_" + name)
        for k, outs_k in zip(keys, outs):
            grads[k], delta[k], new_m[k], new_v[k] = (a.reshape(p[k].shape) for a in outs_k)

    update("early", early, share[:len(early)], share[len(early):])
    late_halves, late_arrays = reduce_finish("late", late, late_state, grad_x, *[new_v[k] for k in early])
    own_late = [late_halves[k] for k in late]
    shared_late, tiles = _share_and_all_reduce(
        own_late, jnp.concatenate([vec_d[0].reshape(8, LANES), jnp.broadcast_to(vec_a[1:2, :LANES], (8, LANES))]))
    g1_grad, loss = tiles[:8], tiles[8, 0]
    update("late", late, own_late, shared_late)
    as_rows = lambda d: [d[k].reshape(-1, LANES) for k in _SMALL]
    small_outs = _adamw_small([g1_grad, late_arrays[-1]], as_rows(p), as_rows(m), as_rows(v))
    for k, outs_k in zip(_SMALL, small_outs):
        grads[k], delta[k], new_m[k], new_v[k] = (a.reshape(p[k].shape) for a in outs_k)

    return (loss, grad_x, *[grads[k] for k in _ORDER], *[delta[k] for k in _ORDER],
            *[new_m[k] for k in _ORDER], *[new_v[k] for k in _ORDER])


def kernel(x, norm1_pre_g, w_in, b_in, w_pool, pool_scale, sgu_ln_g, sgu_ln_b, w_spatial, b_spatial, w_sgu_proj, w_out, norm1_post_g, norm2_pre_g, w_ff1, w_ff2, norm2_post_g, loss_target, m_norm1_pre_g, m_w_in, m_b_in, m_w_pool, m_pool_scale, m_sgu_ln_g, m_sgu_ln_b, m_w_spatial, m_b_spatial, m_w_sgu_proj, m_w_out, m_norm1_post_g, m_norm2_pre_g, m_w_ff1, m_w_ff2, m_norm2_post_g, v_norm1_pre_g, v_w_in, v_b_in, v_w_pool, v_pool_scale, v_sgu_ln_g, v_sgu_ln_b, v_w_spatial, v_b_spatial, v_w_sgu_proj, v_w_out, v_norm1_post_g, v_norm2_pre_g, v_w_ff1, v_w_ff2, v_norm2_post_g):
    p = dict(zip(_ORDER, (norm1_pre_g, w_in, b_in, w_pool, pool_scale, sgu_ln_g, sgu_ln_b, w_spatial, b_spatial,
                          w_sgu_proj, w_out, norm1_post_g, norm2_pre_g, w_ff1, w_ff2, norm2_post_g)))
    m = dict(zip(_ORDER, (m_norm1_pre_g, m_w_in, m_b_in, m_w_pool, m_pool_scale, m_sgu_ln_g, m_sgu_ln_b, m_w_spatial,
                          m_b_spatial, m_w_sgu_proj, m_w_out, m_norm1_post_g, m_norm2_pre_g, m_w_ff1, m_w_ff2,
                          m_norm2_post_g)))
    v = dict(zip(_ORDER, (v_norm1_pre_g, v_w_in, v_b_in, v_w_pool, v_pool_scale, v_sgu_ln_g, v_sgu_ln_b, v_w_spatial,
                          v_b_spatial, v_w_sgu_proj, v_w_out, v_norm1_post_g, v_norm2_pre_g, v_w_ff1, v_w_ff2,
                          v_norm2_post_g)))
    outs = _step(p, m, v, x[0], loss_target[0])
    return (outs[0], outs[1][None], *outs[2:])
```

```python
import math

import jax
import jax.numpy as jnp
from jax import lax
from jax.experimental import pallas as pl
from jax.experimental.pallas import tpu as pltpu

F32 = jnp.float32
BF16 = jnp.bfloat16

D = 1024
D_IN = 5 * D
D_FF = 4 * D
N_SHARD = 4
POOL_WINDOWS = (2, 4, 8, 16)
POOL_GROUP = 256
POOL_HALO = 16
SGU_BLOCK = 128
N_HEADS = 4
HEAD = 256
EPS = 1e-6

ADAM_LR = 0.001
ADAM_B1 = 0.9
ADAM_B2 = 0.999
ADAM_EPS = 1e-08
ADAM_WD = 0.01
ADAM_STEP = 10

V7X_VMEM_BYTES = 64 * 1024 * 1024
VMEM_LIMIT = V7X_VMEM_BYTES - 8 * 1024 * 1024
SMALL_ROWS = 616
LANES = 128

MESH = pl.DeviceIdType.MESH
ANY = pl.BlockSpec(memory_space=pl.ANY)
HBM = pl.BlockSpec(memory_space=pltpu.HBM)
SEM = pl.BlockSpec(memory_space=pltpu.SEMAPHORE)
EFFECT = pltpu.SideEffectType.DATAFLOW_SIDE_EFFECTING

NT_DIMS = (((1,), (1,)), ((), ()))
TN_DIMS = (((0,), (0,)), ((), ()))


def _params(n_axes):
    return pltpu.CompilerParams(dimension_semantics=("arbitrary",) * n_axes, vmem_limit_bytes=VMEM_LIMIT)


def _resident(shape, n_axes):
    zeros = (0,) * len(shape)
    index = (lambda i: zeros) if n_axes == 1 else (lambda i, j: zeros)
    return pl.BlockSpec(shape, index, pipeline_mode=pl.Buffered(1))


def _dot(a, b):
    return jnp.dot(a, b, preferred_element_type=F32)


def _dot_nt(a, b):
    return lax.dot_general(a, b, NT_DIMS, preferred_element_type=F32)


def _dot_tn(a, b):
    return lax.dot_general(a, b, TN_DIMS, preferred_element_type=F32)


def _mean(a):
    return jnp.mean(a, axis=-1, keepdims=True)


def _colsum(a):
    return jnp.sum(a, axis=0, keepdims=True)


_GELU_C0 = math.sqrt(2.0 / math.pi)
_GELU_C1 = 0.044715


def _gelu(a):
    t = jnp.tanh(a * (_GELU_C0 + (_GELU_C0 * _GELU_C1) * (a * a)))
    return a * (0.5 * t + 0.5)


def _gelu_and_grad(a):
    a2 = a * a
    t = jnp.tanh(a * (_GELU_C0 + (_GELU_C0 * _GELU_C1) * a2))
    cdf = 0.5 * t + 0.5
    grad = cdf + (0.5 * a) * (1.0 - t * t) * (_GELU_C0 + (3.0 * _GELU_C0 * _GELU_C1) * a2)
    return a * cdf, grad


def _sigmoid(a):
    return 0.5 * jnp.tanh(0.5 * a) + 0.5


def _rms_norm_bf16(a, g):
    return (a * lax.rsqrt(_mean(a * a) + EPS) * g).astype(BF16)


def _relu_sq_bf16(a):
    r = jnp.maximum(a, 0.0)
    return (r * r).astype(BF16)


def _window_sum(a, w, ahead):
    n = a.shape[0]
    step = 1
    while step < w:
        a = a + pltpu.roll(a, n - step if ahead else step, 0)
        step *= 2
    return a


def _rms_bwd(dn, n, r):
    return r * (dn - n * _mean(dn * n))


def _inproj_own(x, g1, w_own, b_in, chip, dep, tm):
    t = x.shape[0]
    cw = D_IN // N_SHARD

    def body(chip_ref, x_ref, g_ref, w_ref, b_ref, dep_ref, z_ref, xn_ref):
        xn = _rms_norm_bf16(x_ref[...], g_ref[...])
        xn_ref[...] = xn
        z_ref[...] = _dot(xn, w_ref[...]) + b_ref[...]

    return pl.pallas_call(
        body, name="inproj_own",
        grid_spec=pltpu.PrefetchScalarGridSpec(
            num_scalar_prefetch=1, grid=(t // tm,),
            in_specs=[pl.BlockSpec((tm, D), lambda i, chip: (i, 0)),
                      pl.BlockSpec((1, D), lambda i, chip: (0, 0)),
                      pl.BlockSpec((D, cw), lambda i, chip: (0, 0)),
                      pl.BlockSpec((1, cw), lambda i, chip: (0, chip[0])),
                      pl.BlockSpec(dep.shape, lambda i, chip: (0, 0))],
            out_specs=[pl.BlockSpec((tm, cw), lambda i, chip: (i, chip[0])),
                       pl.BlockSpec((tm, D), lambda i, chip: (i, 0))]),
        out_shape=[jax.ShapeDtypeStruct((t, D_IN), F32), jax.ShapeDtypeStruct((t, D), BF16)],
        compiler_params=_params(1),
    )(chip, x, g1, w_own, b_in, dep)


def _inproj_rest(xn, w_in4, b_in, chip, z, tm):
    t = xn.shape[0]
    cw = D_IN // N_SHARD
    shard_of = lambda s, chip: lax.rem(chip[0] + 1 + s, N_SHARD)

    def body(chip_ref, xn_ref, w_ref, b_ref, z_in_ref, z_ref):
        z_ref[...] = _dot(xn_ref[...], w_ref[shard_of(pl.program_id(1), chip_ref)]) + b_ref[...]

    return pl.pallas_call(
        body, name="inproj_rest",
        grid_spec=pltpu.PrefetchScalarGridSpec(
            num_scalar_prefetch=1, grid=(t // tm, N_SHARD - 1),
            in_specs=[pl.BlockSpec((tm, D), lambda i, s, chip: (i, 0)),
                      pl.BlockSpec((N_SHARD, D, cw), lambda i, s, chip: (0, 0, 0), pipeline_mode=pl.Buffered(1)),
                      pl.BlockSpec((1, cw), lambda i, s, chip: (0, shard_of(s, chip))),
                      ANY],
            out_specs=pl.BlockSpec((tm, cw), lambda i, s, chip: (i, shard_of(s, chip)))),
        out_shape=jax.ShapeDtypeStruct(z.shape, F32),
        input_output_aliases={4: 0},
        compiler_params=_params(2),
    )(chip, xn, w_in4, b_in, z)


def _sgu_forward(z_ref, lng_ref, lnb_ref, ws_ref, bsp_ref, sv_ref, tm, with_grad):
    if with_grad:
        u, du_dz = _gelu_and_grad(z_ref[:, D:2 * D])
        gv, dgv_dz = _gelu_and_grad(z_ref[:, 2 * D:3 * D])
    else:
        u = _gelu(z_ref[:, D:2 * D])
        gv = _gelu(z_ref[:, 2 * D:3 * D])
        du_dz = dgv_dz = None
    xc = gv - _mean(gv)
    rstd = lax.rsqrt(_mean(xc * xc) + EPS)
    vhat = xc * rstd
    vb = (vhat * lng_ref[...] + lnb_ref[...]).astype(BF16)
    for b in range(tm // SGU_BLOCK):
        rows = slice(b * SGU_BLOCK, (b + 1) * SGU_BLOCK)
        for h in range(N_HEADS):
            cols = slice(h * HEAD, (h + 1) * HEAD)
            sv_ref[rows, cols] = _dot(ws_ref[h], vb[rows, cols]) + bsp_ref[h]
    return u, du_dz, dgv_dz, vhat, rstd, vb


def _mix_fwd(z, x, w_pool, pool_scale, ln_g, ln_b, ws, bsp, w_proj, w_out, g1post, tm):
    t = x.shape[0]

    def body(z_ref, x_ref, wp_ref, psc_ref, lng_ref, lnb_ref, ws_ref, bsp_ref, wpr_ref, wo_ref, g1p_ref,
             pooled_ref, merged_ref, y_ref, h1_ref, win_ref, sv_ref):
        i = pl.program_id(0)

        @pl.when(i == 0)
        def _():
            win_ref[0:POOL_HALO, :] = jnp.zeros((POOL_HALO, D), F32)

        win_ref[POOL_HALO:POOL_HALO + tm, :] = z_ref[:, 0:D]
        pos1 = lax.broadcasted_iota(jnp.int32, (tm, 1), 0) + (i * tm + 1)
        a_parts = []
        for g, w in enumerate(POOL_WINDOWS):
            cols = slice(g * POOL_GROUP, (g + 1) * POOL_GROUP)
            ts = _window_sum(win_ref[:, cols], w, False)[POOL_HALO:POOL_HALO + tm, :]
            inv_cnt = 1.0 / jnp.minimum(pos1, w).astype(F32)
            pooled = (ts * inv_cnt - z_ref[:, cols]).astype(BF16)
            pooled_ref[:, cols] = pooled
            a_parts.append(_dot(pooled, wp_ref[g]) * psc_ref[:, cols])
        win_ref[0:POOL_HALO, :] = win_ref[tm:tm + POOL_HALO, :]

        u, _, _, _, _, _ = _sgu_forward(z_ref, lng_ref, lnb_ref, ws_ref, bsp_ref, sv_ref, tm, False)
        gated = (u * sv_ref[...]).astype(BF16)
        for h in range(N_HEADS):
            lo = h * HEAD
            bbr = _dot(gated[:, lo:lo + HEAD], wpr_ref[h])
            gate_a = _sigmoid(z_ref[:, 3 * D + lo:3 * D + lo + HEAD])
            gate_b = _sigmoid(z_ref[:, 4 * D + lo:4 * D + lo + HEAD])
            merged_ref[:, lo:lo + HEAD] = (gate_a * a_parts[h] + gate_b * bbr).astype(BF16)
        merged = merged_ref[...]
        y = _dot(merged, wo_ref[...])
        y_ref[...] = y
        r = lax.rsqrt(_mean(y * y) + EPS)
        h1_ref[...] = x_ref[...] + (y * r) * g1p_ref[...]

    row = lambda i: (i, 0)
    fixed2 = lambda i: (0, 0)
    fixed3 = lambda i: (0, 0, 0)
    vec = pl.BlockSpec((1, D), fixed2)
    return pl.pallas_call(
        body, name="mix_fwd",
        grid=(t // tm,),
        in_specs=[pl.BlockSpec((tm, D_IN), row), pl.BlockSpec((tm, D), row),
                  pl.BlockSpec((N_HEADS, POOL_GROUP, POOL_GROUP), fixed3), vec, vec, vec,
                  pl.BlockSpec((N_HEADS, SGU_BLOCK, SGU_BLOCK), fixed3),
                  pl.BlockSpec((N_HEADS, SGU_BLOCK, 1), fixed3),
                  pl.BlockSpec((N_HEADS, HEAD, HEAD), fixed3),
                  pl.BlockSpec((D, D), fixed2), vec],
        out_specs=[pl.BlockSpec((tm, D), row)] * 4,
        out_shape=[jax.ShapeDtypeStruct((t, D), BF16), jax.ShapeDtypeStruct((t, D), BF16),
                   jax.ShapeDtypeStruct((t, D), F32), jax.ShapeDtypeStruct((t, D), F32)],
        scratch_shapes=[pltpu.VMEM((tm + POOL_HALO, D), F32), pltpu.VMEM((tm, D), F32)],
        compiler_params=_params(1),
    )(z, x, w_pool, pool_scale, ln_g, ln_b, ws, bsp, w_proj, w_out, g1post)


def _ffn_fwd(h1, g2pre, w1, w2, g2post, target, tm):
    t = h1.shape[0]
    n_j = D_FF // D

    def body(h1_ref, g2_ref, w1_ref, w2_ref, g2p_ref, tgt_ref,
             hn_ref, f1_ref, f_ref, df2_ref, dout_ref, vec_ref, acc_ref):
        i, j = pl.program_id(0), pl.program_id(1)

        @pl.when((i == 0) & (j == 0))
        def _():
            vec_ref[...] = jnp.zeros_like(vec_ref)

        @pl.when(j == 0)
        def _():
            hn_ref[...] = _rms_norm_bf16(h1_ref[...], g2_ref[...])
            acc_ref[...] = jnp.zeros_like(acc_ref)

        f1 = _dot(hn_ref[...], w1_ref[j])
        f1_ref[...] = f1
        f = _relu_sq_bf16(f1)
        f_ref[...] = f
        acc_ref[...] += _dot(f, w2_ref[j])

        @pl.when(j == n_j - 1)
        def _():
            f2 = acc_ref[...]
            r = lax.rsqrt(_mean(f2 * f2) + EPS)
            n2 = f2 * r
            err = h1_ref[...] + n2 * g2p_ref[...] - tgt_ref[...]
            loss = 0.5 * jnp.sum(_mean(err * err))
            dout = err * (1.0 / D)
            dout_ref[...] = dout
            vec_ref[0:1, :] += _colsum(dout * n2)
            vec_ref[1:2, :] += jnp.full((1, D), loss, F32)
            df2_ref[...] = _rms_bwd(dout * g2p_ref[...], n2, r).astype(BF16)

    row = lambda i, j: (i, 0)
    fixed = lambda i, j: (0, 0)
    vec = pl.BlockSpec((1, D), fixed)
    return pl.pallas_call(
        body, name="ffn_fwd",
        grid=(t // tm, n_j),
        in_specs=[pl.BlockSpec((tm, D), row), vec,
                  _resident((n_j, D, D), 2), _resident((n_j, D, D), 2), vec,
                  pl.BlockSpec((tm, D), row)],
        out_specs=[pl.BlockSpec((tm, D), row),
                   pl.BlockSpec((tm, D), lambda i, j: (i, j)),
                   pl.BlockSpec((tm, D), lambda i, j: (i, j)),
                   pl.BlockSpec((tm, D), row), pl.BlockSpec((tm, D), row),
                   pl.BlockSpec((8, D), fixed)],
        out_shape=[jax.ShapeDtypeStruct((t, D), BF16), jax.ShapeDtypeStruct((t, D_FF), F32),
                   jax.ShapeDtypeStruct((t, D_FF), BF16), jax.ShapeDtypeStruct((t, D), BF16),
                   jax.ShapeDtypeStruct((t, D), F32), jax.ShapeDtypeStruct((8, D), F32)],
        scratch_shapes=[pltpu.VMEM((tm, D), F32)],
        compiler_params=_params(2),
    )(h1, g2pre, w1, w2, g2post, target)


def _ffn_bwd(df2, f1, w1, w2, h1, dout, y, g2pre, g1post, tm):
    t = h1.shape[0]
    n_j = D_FF // D

    def body(df2_ref, f1_ref, w1_ref, w2_ref, h1_ref, dout_ref, y_ref, g2_ref, g1p_ref,
             df1_ref, dh1_ref, dy_ref, vec_ref, acc_ref):
        i, j = pl.program_id(0), pl.program_id(1)

        @pl.when((i == 0) & (j == 0))
        def _():
            vec_ref[...] = jnp.zeros_like(vec_ref)

        @pl.when(j == 0)
        def _():
            acc_ref[...] = jnp.zeros_like(acc_ref)

        df = _dot_nt(df2_ref[...], w2_ref[j])
        df1 = (df * (2.0 * jnp.maximum(f1_ref[...], 0.0))).astype(BF16)
        df1_ref[...] = df1
        acc_ref[...] += _dot_nt(df1, w1_ref[j])

        @pl.when(j == n_j - 1)
        def _():
            dhn = acc_ref[...]
            h = h1_ref[...]
            r2 = lax.rsqrt(_mean(h * h) + EPS)
            nh = h * r2
            vec_ref[0:1, :] += _colsum(dhn * nh)
            dh1 = dout_ref[...] + _rms_bwd(dhn * g2_ref[...], nh, r2)
            dh1_ref[...] = dh1
            yv = y_ref[...]
            r1 = lax.rsqrt(_mean(yv * yv) + EPS)
            ny = yv * r1
            vec_ref[1:2, :] += _colsum(dh1 * ny)
            dy_ref[...] = _rms_bwd(dh1 * g1p_ref[...], ny, r1).astype(BF16)

    row = lambda i, j: (i, 0)
    fixed = lambda i, j: (0, 0)
    vec = pl.BlockSpec((1, D), fixed)
    return pl.pallas_call(
        body, name="ffn_bwd",
        grid=(t // tm, n_j),
        in_specs=[pl.BlockSpec((tm, D), row),
                  pl.BlockSpec((tm, D), lambda i, j: (i, j)),
                  _resident((n_j, D, D), 2), _resident((n_j, D, D), 2),
                  pl.BlockSpec((tm, D), row), pl.BlockSpec((tm, D), row), pl.BlockSpec((tm, D), row), vec, vec],
        out_specs=[pl.BlockSpec((tm, D), lambda i, j: (i, j)),
                   pl.BlockSpec((tm, D), row), pl.BlockSpec((tm, D), row),
                   pl.BlockSpec((8, D), fixed)],
        out_shape=[jax.ShapeDtypeStruct((t, D_FF), BF16), jax.ShapeDtypeStruct((t, D), F32),
                   jax.ShapeDtypeStruct((t, D), BF16), jax.ShapeDtypeStruct((8, D), F32)],
        scratch_shapes=[pltpu.VMEM((tm, D), F32)],
        compiler_params=_params(2),
    )(df2, f1, w1, w2, h1, dout, y, g2pre, g1post)


def _mix_bwd(z, pooled, dy, w_pool, pool_scale, ln_g, ln_b, ws, ws_t, bsp, w_proj, w_out, mask, tm, dep):
    t = z.shape[0]
    n_t = t // tm

    def body(z_ref, pooled_ref, dy_ref, wp_ref, psc_ref, lng_ref, lnb_ref, ws_ref, wst_ref, bsp_ref, wpr_ref,
             wo_ref, mask_ref, dep_ref,
             dz_ref, dwp_ref, dwpr_ref, dws_ref, dbsp_ref, vec_ref, dbin_ref,
             win_ref, sv_ref, dsv_ref, dv_ref, dsvsum_ref):
        i = pl.program_id(0)

        @pl.when(i == 0)
        def _():
            win_ref[tm:tm + POOL_HALO, :] = jnp.zeros((POOL_HALO, D), F32)
            dwp_ref[...] = jnp.zeros_like(dwp_ref)
            dwpr_ref[...] = jnp.zeros_like(dwpr_ref)
            dws_ref[...] = jnp.zeros_like(dws_ref)
            vec_ref[...] = jnp.zeros_like(vec_ref)
            dbin_ref[...] = jnp.zeros_like(dbin_ref)
            dsvsum_ref[...] = jnp.zeros_like(dsvsum_ref)

        dmerged = _dot_nt(dy_ref[...], wo_ref[...])

        u, du_dz, dgv_dz, vhat, rstd, vb = _sgu_forward(z_ref, lng_ref, lnb_ref, ws_ref, bsp_ref, sv_ref, tm, True)
        sv = sv_ref[...]
        gated = (u * sv).astype(BF16)

        sa = _sigmoid(z_ref[:, 3 * D:4 * D])
        sb = _sigmoid(z_ref[:, 4 * D:5 * D])
        da = dmerged * sa
        dbbr = (dmerged * sb).astype(BF16)

        tile = n_t - 1 - i
        pos1 = lax.broadcasted_iota(jnp.int32, (tm, 1), 0) + (tile * tm + 1)
        def put(k, lo, part):
            cols = slice(k * D + lo, k * D + lo + part.shape[1])
            dz_ref[:, cols] = part.astype(BF16)
            dbin_ref[0:1, cols] += _colsum(part)

        dpooled_parts = []
        for g, w in enumerate(POOL_WINDOWS):
            cols = slice(g * POOL_GROUP, (g + 1) * POOL_GROUP)
            pg = pooled_ref[:, cols]
            a_pre = _dot(pg, wp_ref[g])
            da_g = da[:, cols]
            psc = psc_ref[:, cols]
            vec_ref[0:1, cols] += _colsum(da_g * a_pre)
            sa_g = sa[:, cols]
            put(3, g * POOL_GROUP, dmerged[:, cols] * (a_pre * psc) * (sa_g * (1.0 - sa_g)))
            da_pre = (da_g * psc).astype(BF16)
            dwp_ref[g] += _dot_tn(pg, da_pre)
            dpooled = _dot_nt(da_pre, wp_ref[g])
            dpooled_parts.append(dpooled)
            inv_cnt = 1.0 / jnp.minimum(pos1, w).astype(F32)
            win_ref[0:tm, cols] = dpooled * inv_cnt
        for g, w in enumerate(POOL_WINDOWS):
            cols = slice(g * POOL_GROUP, (g + 1) * POOL_GROUP)
            acc = _window_sum(win_ref[:, cols], w, True)[0:tm, :]
            put(0, g * POOL_GROUP, acc - dpooled_parts[g])
        win_ref[tm:tm + POOL_HALO, :] = win_ref[0:POOL_HALO, :]

        for h in range(N_HEADS):
            cols = slice(h * HEAD, (h + 1) * HEAD)
            g_h = gated[:, cols]
            db_h = dbbr[:, cols]
            bbr_h = _dot(g_h, wpr_ref[h])
            sb_h = sb[:, cols]
            put(4, h * HEAD, dmerged[:, cols] * bbr_h * (sb_h * (1.0 - sb_h)))
            dwpr_ref[h] += _dot_tn(g_h, db_h)
            dgated_h = _dot_nt(db_h, wpr_ref[h])
            dsv_ref[:, cols] = dgated_h * u[:, cols]
            sv_ref[:, cols] = dgated_h * sv[:, cols]
        put(1, 0, sv_ref[...] * du_dz)
        dsv = dsv_ref[...]
        dsvb = dsv.astype(BF16)
        blk_sum = dsv[0:SGU_BLOCK, :]
        for b in range(1, tm // SGU_BLOCK):
            blk_sum = blk_sum + dsv[b * SGU_BLOCK:(b + 1) * SGU_BLOCK, :]
        dsvsum_ref[...] += blk_sum
        for b in range(tm // SGU_BLOCK):
            rows = slice(b * SGU_BLOCK, (b + 1) * SGU_BLOCK)
            for h in range(N_HEADS):
                cols = slice(h * HEAD, (h + 1) * HEAD)
                dws_ref[h] += _dot_nt(dsvb[rows, cols], vb[rows, cols])
                dv_ref[rows, cols] = _dot(wst_ref[h], dsvb[rows, cols])
        dv = dv_ref[...]
        vec_ref[1:2, :] += _colsum(dv * vhat)
        vec_ref[2:3, :] += _colsum(dv)
        dvhat = dv * lng_ref[...]
        dgv = rstd * (dvhat - _mean(dvhat) - vhat * _mean(dvhat * vhat))
        put(2, 0, dgv * dgv_dz)

        @pl.when(i == n_t - 1)
        def _():
            for h in range(N_HEADS):
                dws_ref[h] = jnp.where(mask_ref[...] > 0.0, dws_ref[h], 0.0)
                tot = jnp.sum(dsvsum_ref[:, h * HEAD:(h + 1) * HEAD], axis=1, keepdims=True)
                dbsp_ref[:, h * LANES:(h + 1) * LANES] = jnp.broadcast_to(tot, (SGU_BLOCK, LANES))

    row = lambda i: (n_t - 1 - i, 0)
    fixed2 = lambda i: (0, 0)
    fixed3 = lambda i: (0, 0, 0)
    vec = pl.BlockSpec((1, D), fixed2)
    sq = pl.BlockSpec((N_HEADS, SGU_BLOCK, SGU_BLOCK), fixed3)
    grp = pl.BlockSpec((N_HEADS, HEAD, HEAD), fixed3)
    return pl.pallas_call(
        body, name="mix_bwd",
        grid=(n_t,),
        in_specs=[pl.BlockSpec((tm, D_IN), row), pl.BlockSpec((tm, D), row), pl.BlockSpec((tm, D), row),
                  grp, vec, vec, vec, sq, sq, pl.BlockSpec((N_HEADS, SGU_BLOCK, 1), fixed3), grp,
                  pl.BlockSpec((D, D), fixed2), pl.BlockSpec((SGU_BLOCK, SGU_BLOCK), fixed2),
                  pl.BlockSpec(dep.shape, fixed2)],
        out_specs=[pl.BlockSpec((tm, D_IN), row), grp, grp, sq,
                   pl.BlockSpec((SGU_BLOCK, N_HEADS * LANES), fixed2),
                   pl.BlockSpec((8, D), fixed2), pl.BlockSpec((8, D_IN), fixed2)],
        out_shape=[jax.ShapeDtypeStruct((t, D_IN), BF16),
                   jax.ShapeDtypeStruct((N_HEADS, HEAD, HEAD), F32), jax.ShapeDtypeStruct((N_HEADS, HEAD, HEAD), F32),
                   jax.ShapeDtypeStruct((N_HEADS, SGU_BLOCK, SGU_BLOCK), F32),
                   jax.ShapeDtypeStruct((SGU_BLOCK, N_HEADS * LANES), F32),
                   jax.ShapeDtypeStruct((8, D), F32), jax.ShapeDtypeStruct((8, D_IN), F32)],
        scratch_shapes=[pltpu.VMEM((tm + POOL_HALO, D), F32), pltpu.VMEM((tm, D), F32), pltpu.VMEM((tm, D), F32),
                        pltpu.VMEM((tm, D), F32), pltpu.VMEM((SGU_BLOCK, D), F32)],
        compiler_params=_params(1),
    )(z, pooled, dy, w_pool, pool_scale, ln_g, ln_b, ws, ws_t, bsp, w_proj, w_out, mask, dep)


def _dx(dz, w_in4, x, dh1, g1, dep, tm):
    t = x.shape[0]
    cw = D_IN // N_SHARD

    def body(dz_ref, w_ref, x_ref, dh1_ref, g_ref, dep_ref, gx_ref, vec_ref):
        @pl.when(pl.program_id(0) == 0)
        def _():
            vec_ref[...] = jnp.zeros_like(vec_ref)

        dxn = _dot_nt(dz_ref[:, 0:cw], w_ref[0])
        for s in range(1, N_SHARD):
            dxn = dxn + _dot_nt(dz_ref[:, s * cw:(s + 1) * cw], w_ref[s])
        xv = x_ref[...]
        r = lax.rsqrt(_mean(xv * xv) + EPS)
        xh = xv * r
        vec_ref[0:1, :] += _colsum(dxn * xh)
        gx_ref[...] = dh1_ref[...] + _rms_bwd(dxn * g_ref[...], xh, r)

    row = lambda i: (i, 0)
    fixed = lambda i: (0, 0)
    return pl.pallas_call(
        body, name="dx",
        grid=(t // tm,),
        in_specs=[pl.BlockSpec((tm, D_IN), row),
                  _resident((N_SHARD, D, cw), 1),
                  pl.BlockSpec((tm, D), row), pl.BlockSpec((tm, D), row), pl.BlockSpec((1, D), fixed),
                  pl.BlockSpec(dep.shape, fixed)],
        out_specs=[pl.BlockSpec((tm, D), row), pl.BlockSpec((8, D), fixed)],
        out_shape=[jax.ShapeDtypeStruct((t, D), F32), jax.ShapeDtypeStruct((8, D), F32)],
        compiler_params=_params(1),
    )(dz, w_in4, x, dh1, g1, dep)


def _dw(a, b, name, n_split, tn, tk, n_out, out_block):
    t, m = a.shape
    n = b.shape[1]
    tm = min(m, D)
    rows = tm // n_split
    n_k = t // tk

    def body(a_ref, b_ref, o_ref, acc_ref):
        k = pl.program_id(2)

        @pl.when(k == 0)
        def _():
            acc_ref[...] = jnp.zeros_like(acc_ref)

        acc_ref[...] += _dot_tn(a_ref[...], b_ref[...])

        @pl.when(k == n_k - 1)
        def _():
            for q in range(n_split):
                o_ref[q % 2, q // 2] = acc_ref[q * rows:(q + 1) * rows, :].astype(BF16)

    return pl.pallas_call(
        body, name=name,
        grid=(m // tm, n // tn, n_k),
        in_specs=[pl.BlockSpec((tk, tm), lambda i, j, k: (k, i)),
                  pl.BlockSpec((tk, tn), lambda i, j, k: (k, j))],
        out_specs=pl.BlockSpec((2, n_split // 2, rows, tn), lambda i, j, k: (0, out_block(i, j), 0, 0)),
        out_shape=jax.ShapeDtypeStruct((2, n_out, rows, tn), BF16),
        scratch_shapes=[pltpu.VMEM((tm, tn), F32)],
        compiler_params=_params(3),
    )(a, b)


def _row_block(rows, cols, itemsize, n_bufs):
    budget = VMEM_LIMIT // 4
    rb = rows
    while rb % 16 == 0 and rb * cols * itemsize * n_bufs * 2 > budget:
        rb //= 2
    return rb


ROW_STEPS = 2


def _chip_sums(core, grads, recvs, name):
    n = len(grads)

    def body(c_ref, *refs):
        for w in range(n):
            refs[2 * n + w][...] = (refs[w][0].astype(F32) + refs[n + w][...].astype(F32)).astype(BF16)

    def blocks(g):
        _, _, rh, cols = g.shape
        rb = rh // ROW_STEPS
        return (pl.BlockSpec((1, 1, rb, cols), lambda s, r, c: (c[0], s, r, 0)),
                pl.BlockSpec((1, rb, cols), lambda s, r, c: (s, r, 0)))

    specs = [blocks(g) for g in grads]
    return pl.pallas_call(
        body, name=name,
        grid_spec=pltpu.PrefetchScalarGridSpec(
            num_scalar_prefetch=1, grid=(N_SHARD, ROW_STEPS),
            in_specs=[s[0] for s in specs] + [s[1] for s in specs],
            out_specs=[s[1] for s in specs]),
        out_shape=[jax.ShapeDtypeStruct(g.shape[1:], BF16) for g in grads],
        compiler_params=_params(2),
    )(core, *grads, *recvs)


def _slot_sum(slots, name):
    n_s, rows, cols = slots.shape
    rb = _row_block(rows, cols, 4, n_s + 1)

    def body(s_ref, o_ref):
        acc = s_ref[0].astype(F32)
        for k in range(1, n_s):
            acc = acc + s_ref[k].astype(F32)
        o_ref[...] = acc

    return pl.pallas_call(
        body, name=name,
        grid=(rows // rb,),
        in_specs=[pl.BlockSpec((n_s, rb, cols), lambda r: (0, r, 0))],
        out_specs=pl.BlockSpec((rb, cols), lambda r: (r, 0)),
        out_shape=jax.ShapeDtypeStruct((rows, cols), F32),
        compiler_params=_params(1),
    )(slots)


def _adamw_update(g, w, m, v):
    m_new = ADAM_B1 * m + (1.0 - ADAM_B1) * g
    v_new = ADAM_B2 * v + (1.0 - ADAM_B2) * (g * g)
    m_hat = m_new / (1.0 - ADAM_B1 ** ADAM_STEP)
    v_hat = v_new / (1.0 - ADAM_B2 ** ADAM_STEP)
    return -ADAM_LR * (m_hat / (jnp.sqrt(v_hat) + ADAM_EPS) + ADAM_WD * w), m_new, v_new


def _adamw_small(grad_parts, ws, ms, vs):
    n, n_g = len(ws), len(grad_parts)

    def rows_of(ref, lo, rows):
        if len(ref.shape) == 2:
            return ref[lo:lo + rows, :]
        acc = ref[0, lo:lo + rows, :]
        for s in range(1, ref.shape[0]):
            acc = acc + ref[s, lo:lo + rows, :]
        return acc

    def body(*refs):
        g_refs, ins, outs = refs[:n_g], refs[n_g:n_g + 3 * n], refs[n_g + 3 * n:]
        part, off = 0, 0
        for k in range(n):
            rows = ws[k].shape[0]
            if off + rows > grad_parts[part].shape[-2]:
                part, off = part + 1, 0
            g = rows_of(g_refs[part], off, rows)
            off += rows
            delta, m_new, v_new = _adamw_update(g, ins[k][...], ins[n + k][...], ins[2 * n + k][...])
            for o, val in zip(outs[4 * k:4 * k + 4], (g, delta, m_new, v_new)):
                o[...] = val

    vmem = pl.BlockSpec(memory_space=pltpu.VMEM)
    outs = pl.pallas_call(
        body, name="adamw_small",
        in_specs=[vmem] * (n_g + 3 * n), out_specs=[vmem] * (4 * n),
        out_shape=[jax.ShapeDtypeStruct(w.shape, F32) for w in ws for _ in range(4)],
    )(*grad_parts, *ws, *ms, *vs)
    return [outs[4 * k:4 * k + 4] for k in range(n)]


def _final_sums(chip, owns, slots, name):
    n = len(owns)

    def body(chip_ref, *refs):
        for w in range(n):
            own_ref, s1_ref, s2_ref, s3_ref = refs[4 * w:4 * w + 4]
            acc = own_ref[0].astype(F32) + s1_ref[0].astype(F32)
            acc = acc + s2_ref[0].astype(F32)
            refs[4 * n + w][...] = acc + s3_ref[0].astype(F32)

    def slot(a, k):
        n_s, rh, cols = a.shape
        return pl.BlockSpec((1, rh // ROW_STEPS, cols), lambda r, chip: (lax.rem(chip[0] + k, n_s), r, 0))

    operands, in_specs = [], []
    for own, slots_w in zip(owns, slots):
        operands += [own, slots_w, slots_w, slots_w]
        in_specs += [slot(own, k) for k in range(4)]
    return pl.pallas_call(
        body, name=name,
        grid_spec=pltpu.PrefetchScalarGridSpec(
            num_scalar_prefetch=1, grid=(ROW_STEPS,),
            in_specs=in_specs,
            out_specs=[pl.BlockSpec((a.shape[1] // ROW_STEPS, a.shape[2]), lambda r, chip: (r, 0)) for a in owns]),
        out_shape=[jax.ShapeDtypeStruct(a.shape[1:], F32) for a in owns],
        compiler_params=_params(1),
    )(chip, *operands)


def _adamw_halves(core, owns, siblings, ws, ms, vs, name):
    n = len(ws)

    def body(c_ref, *refs):
        ins, outs = refs[:5 * n], refs[5 * n:]
        mine = c_ref[0] == pl.program_id(0)
        for k in range(n):
            own_ref, sib_ref, w_ref, m_ref, v_ref = ins[5 * k:5 * k + 5]
            g_ref, d_ref, mo_ref, vo_ref = outs[4 * k:4 * k + 4]

            @pl.when(mine)
            def _():
                g_ref[...] = own_ref[...]

            @pl.when(jnp.logical_not(mine))
            def _():
                g_ref[...] = sib_ref[...]

            d_ref[...], mo_ref[...], vo_ref[...] = _adamw_update(g_ref[...], w_ref[...], m_ref[...], v_ref[...])

    operands, in_specs, out_specs = [], [], []
    for own, sib, w, m, v in zip(owns, siblings, ws, ms, vs):
        rows, cols = w.shape
        rb = rows // 2 // ROW_STEPS
        half = pl.BlockSpec((rb, cols), lambda h, r, c: (r, 0))
        whole = pl.BlockSpec((rb, cols), lambda h, r, c: (h * ROW_STEPS + r, 0))
        operands += [own, sib, w, m, v]
        in_specs += [half, half, whole, whole, whole]
        out_specs += [whole] * 4
    outs = pl.pallas_call(
        body, name=name,
        grid_spec=pltpu.PrefetchScalarGridSpec(
            num_scalar_prefetch=1, grid=(2, ROW_STEPS), in_specs=in_specs, out_specs=out_specs),
        out_shape=[jax.ShapeDtypeStruct(w.shape, F32) for w in ws for _ in range(4)],
        compiler_params=_params(2),
    )(core, *operands)
    return [outs[4 * k:4 * k + 4] for k in range(n)]


def _place():
    x, y, c = lax.axis_index("x"), lax.axis_index("y"), lax.axis_index("c")
    chips = [(1 - x, y), (x, 1 - y), (1 - x, 1 - y)]
    return x, y, c, chips


def _remote(src, dst, send_sem, recv_sem, device):
    return pltpu.make_async_remote_copy(src_ref=src, dst_ref=dst, send_sem=send_sem, recv_sem=recv_sem,
                                        device_id=device, device_id_type=MESH)


def _split_start(name, arrays, n_copies, plan):
    n = len(arrays)

    def body(*refs):
        ins, send_sem, recv_sem, token = refs[:n], refs[n], refs[n + 1], refs[-1]
        for k, (src, dst, device) in enumerate(plan(ins)):
            _remote(src, dst, send_sem.at[k], recv_sem.at[k], device).start()
        token[...] = jnp.zeros_like(token)

    outs = pl.pallas_call(
        body, name=name,
        in_specs=[HBM] * n,
        out_specs=(SEM, SEM, *[HBM] * n, pl.BlockSpec(memory_space=pltpu.VMEM)),
        out_shape=(pltpu.SemaphoreType.DMA((n_copies,)), pltpu.SemaphoreType.DMA((n_copies,)),
                   *[pltpu.HBM(a.shape, a.dtype) for a in arrays], jax.ShapeDtypeStruct((8, LANES), F32)),
        input_output_aliases={i: i + 2 for i in range(n)},
        compiler_params=pltpu.CompilerParams(has_side_effects=EFFECT),
    )(*[pltpu.with_memory_space_constraint(a, pltpu.HBM) for a in arrays])
    return outs[0], outs[1], list(outs[2:2 + n]), outs[-1]


def _split_wait(name, send_sem, recv_sem, arrays, plan, first, *after):
    n = len(arrays)

    def body(*refs):
        ins, send_ref, recv_ref = refs[:n], refs[n], refs[n + 1]
        for k, (src, dst, device) in enumerate(plan(ins)):
            cp = _remote(src, dst, send_ref.at[first + k], recv_ref.at[first + k], device)
            cp.wait_send()
            cp.wait_recv()

    outs = pl.pallas_call(
        body, name=name,
        in_specs=[HBM] * n + [SEM, SEM] + [ANY] * len(after), out_specs=[HBM] * n,
        out_shape=[pltpu.HBM(a.shape, a.dtype) for a in arrays],
        input_output_aliases={i: i for i in range(n)},
        compiler_params=pltpu.CompilerParams(has_side_effects=EFFECT),
    )(*arrays, send_sem, recv_sem, *after)
    return list(outs)


def _plans(*parts):
    def plan(refs):
        copies, lo = [], 0
        for part, n in parts:
            copies += part(refs[lo:lo + n])
            lo += n
        return copies
    return plan


def _plan_gather_send(n):
    def plan(refs):
        x, y, c, chips = _place()
        me = 2 * x + y
        copies = []
        for w in range(n):
            own, full = refs[w], refs[n + w]
            copies.append((own, full.at[me], (x, y, 1 - c)))
            copies += [(own.at[c], full.at[me, c], (px, py, c)) for px, py in chips]
        return copies
    return plan


def _plan_gather_direct(n):
    def plan(refs):
        x, y, c, chips = _place()
        me = 2 * x + y
        copies = []
        for w in range(n):
            own, full = refs[w], refs[n + w]
            copies.append((own, full.at[me], (x, y, 1 - c)))
            for px, py in chips:
                copies += [(own.at[c], full.at[me, c], (px, py, c)), (own.at[c], full.at[me, c], (px, py, 1 - c))]
        return copies
    return plan


def _plan_gather_pass(n):
    def plan(refs):
        x, y, c, chips = _place()
        copies = []
        for w in range(n):
            for px, py in chips:
                landed = refs[w].at[2 * px + py, c]
                copies.append((landed, landed, (x, y, 1 - c)))
        return copies
    return plan


def _plan_share(n):
    def plan(refs):
        x, y, c, _ = _place()
        return [(refs[w], refs[n + w], (x, y, 1 - c)) for w in range(n)]
    return plan


def _plan_exchange(n, with_small):
    def plan(refs):
        x, y, c, chips = _place()
        me = 2 * x + y
        n_in = n + int(with_small)
        copies = []
        for w in range(n):
            copies += [(refs[w].at[2 * px + py], refs[n_in + w].at[me], (px, py, c)) for px, py in chips]
        if with_small:
            small, slots = refs[n], refs[n_in + n]
            copies.append((small, slots.at[me], (x, y, 1 - c)))
            copies += [(small, slots.at[me], (px, py, c)) for px, py in chips]
        return copies
    return plan


def _swap_halves(name, grads, wholes):
    n = len(grads)
    arrays = list(grads) + list(wholes)
    n_all = len(arrays)

    def body(*refs):
        ins, outs = refs[:n_all], refs[n_all:2 * n_all]
        send_sem, recv_sem = refs[2 * n_all:]
        x, y, c, _ = _place()
        sibling = (x, y, 1 - c)
        copies = []
        for w in range(n_all):
            src = ins[w].at[1 - c] if w < n else ins[w]
            cp = _remote(src, outs[w], send_sem.at[w], recv_sem.at[w], sibling)
            cp.start()
            copies.append(cp)
        for cp in copies:
            cp.wait()

    return pl.pallas_call(
        body, name=name,
        in_specs=[ANY] * n_all, out_specs=[ANY] * n_all,
        out_shape=[jax.ShapeDtypeStruct(g.shape[1:], g.dtype) for g in grads]
        + [jax.ShapeDtypeStruct(a.shape, a.dtype) for a in wholes],
        scratch_shapes=[pltpu.SemaphoreType.DMA((n_all,))] * 2,
    )(*arrays)


def _share_and_all_reduce(wholes, part):
    n = len(wholes)
    n_dev = 8

    def body(*refs):
        ins, part_ref, outs, sum_ref = refs[:n], refs[n], refs[n + 1:2 * n + 1], refs[2 * n + 1]
        slots_ref, send_sem, recv_sem, send_tile, recv_tile = refs[2 * n + 2:]
        x, y, c, _ = _place()
        me = 4 * x + 2 * y + c
        copies = []
        for w in range(n):
            copies.append(_remote(ins[w], outs[w], send_sem.at[w], recv_sem.at[w], (x, y, 1 - c)))
        slots_ref[me] = part_ref[...]
        flips = [(fx, fy, fc) for fx in (0, 1) for fy in (0, 1) for fc in (0, 1)][1:]
        for k, (fx, fy, fc) in enumerate(flips):
            peer = (x + fx - 2 * x * fx, y + fy - 2 * y * fy, c + fc - 2 * c * fc)
            copies.append(_remote(part_ref, slots_ref.at[me], send_tile.at[k], recv_tile.at[k], peer))
        for cp in copies:
            cp.start()
        for cp in copies:
            cp.wait()
        acc = slots_ref[0]
        for k in range(1, n_dev):
            acc = acc + slots_ref[k]
        sum_ref[...] = acc

    vmem = pl.BlockSpec(memory_space=pltpu.VMEM)
    outs = pl.pallas_call(
        body, name="rs_share_late",
        in_specs=[ANY] * n + [vmem], out_specs=[ANY] * n + [vmem],
        out_shape=[jax.ShapeDtypeStruct(a.shape, a.dtype) for a in wholes] + [jax.ShapeDtypeStruct(part.shape, F32)],
        scratch_shapes=[pltpu.VMEM((n_dev,) + part.shape, F32),
                        pltpu.SemaphoreType.DMA((n,)), pltpu.SemaphoreType.DMA((n,)),
                        pltpu.SemaphoreType.DMA((n_dev - 1,)), pltpu.SemaphoreType.DMA((n_dev - 1,))],
    )(*wholes, part)
    return outs[:n], outs[n]


_SMALL = ("norm1_pre_g", "b_in", "pool_scale", "sgu_ln_g", "sgu_ln_b", "w_spatial",
          "norm1_post_g", "norm2_pre_g", "norm2_post_g", "b_spatial")
_MIX = ("w_in", "w_pool", "w_sgu_proj", "w_out")
_FF = ("w_ff1", "w_ff2")
_BIG = _MIX + _FF
_ORDER = ("norm1_pre_g", "w_in", "b_in", "w_pool", "pool_scale", "sgu_ln_g", "sgu_ln_b", "w_spatial", "b_spatial",
          "w_sgu_proj", "w_out", "norm1_post_g", "norm2_pre_g", "w_ff1", "w_ff2", "norm2_post_g")


def _pack(keys, rows, parts):
    flat = jnp.concatenate([parts[k].reshape(-1).astype(F32) for k in keys])
    flat = jnp.pad(flat, (0, rows * LANES - flat.shape[0]))
    return flat.reshape(rows, LANES)


def _halves(a):
    return a.reshape(2, a.shape[0] // 2, a.shape[1])


def _step(p, m, v, x, target):
    t = x.shape[0]
    tm_mm = min(1024, t)
    tm_mix = min(256, t)
    tm_ffn = min(512, t)
    core = lax.axis_index("c").astype(jnp.int32).reshape(1)
    row = lambda a: a.reshape(1, -1)

    chip = (2 * lax.axis_index("x") + lax.axis_index("y")).astype(jnp.int32).reshape(1)

    local2d = {"w_in": p["w_in"], "w_ff1": p["w_ff1"], "w_ff2": p["w_ff2"], "w_out": p["w_out"],
               "w_pool": p["w_pool"].reshape(N_HEADS * 64, POOL_GROUP),
               "w_sgu_proj": p["w_sgu_proj"].reshape(N_HEADS * 64, HEAD)}
    shard = {k: _halves(local2d[k].astype(BF16)) for k in _BIG}
    landing = lambda keys: [lax.empty((N_SHARD,) + shard[k].shape, BF16) for k in keys]
    in_arrays = [shard["w_in"]] + landing(["w_in"])
    send_a, recv_a, in_arrays, token = _split_start("ag_in_send", in_arrays, 4, _plan_gather_send(1))
    z, xn = _inproj_own(x, row(p["norm1_pre_g"]), in_arrays[0].reshape(D, D_IN // N_SHARD), row(p["b_in"]), chip,
                        token, tm_mm)
    mix_keys = _MIX[1:]
    n_mix, n_ff = len(mix_keys), len(_FF)
    in_arrays = _split_wait("ag_in_land", send_a, recv_a, in_arrays, _plan_gather_send(1), 0, z,
                            *[shard[k] for k in mix_keys + _FF])
    rest = in_arrays[1:] + [shard[k] for k in mix_keys] + landing(mix_keys) + [shard[k] for k in _FF] + landing(_FF)
    rest_plan = _plans((_plan_gather_pass(1), 1), (_plan_gather_direct(n_mix), 2 * n_mix),
                       (_plan_gather_send(n_ff), 2 * n_ff))
    send0, recv0, rest, token = _split_start("ag_rest_send", rest, 3 + 7 * n_mix + 4 * n_ff, rest_plan)
    in_full, mix_arrays, ff_arrays = rest[:1], rest[1:1 + 2 * n_mix], rest[1 + 2 * n_mix:]
    in_full = _split_wait("ag_in_passed", send0, recv0, in_full, _plan_gather_pass(1), 0, token)
    w_in4 = in_full[0].reshape(N_SHARD, D, D_IN // N_SHARD)

    pos = jnp.arange(SGU_BLOCK) // 64
    mask = (pos[:, None] >= pos[None, :]).astype(F32)
    ws = (p["w_spatial"] * mask[None]).astype(BF16)
    ws_t = jnp.swapaxes(ws, 1, 2)
    bsp = p["b_spatial"].reshape(N_HEADS, SGU_BLOCK, 1)

    z = _inproj_rest(xn, w_in4, row(p["b_in"]), chip, z, tm_mm)
    mix_arrays = _split_wait("ag_mix_land", send0, recv0, mix_arrays, _plan_gather_direct(n_mix), 3, z)
    full = dict(zip(mix_keys, mix_arrays[len(mix_keys):]))
    w_out = full["w_out"].reshape(D, D)
    regroup = lambda a: a.reshape(N_SHARD, N_HEADS, 64, 256).transpose(1, 0, 2, 3).reshape(N_HEADS, 256, 256)
    w_pool = regroup(full["w_pool"])
    w_proj = regroup(full["w_sgu_proj"])
    pooled, merged, y, h1 = _mix_fwd(z, x, w_pool, row(p["pool_scale"]), row(p["sgu_ln_g"]), row(p["sgu_ln_b"]),
                                     ws, bsp, w_proj, w_out, row(p["norm1_post_g"]), tm_mix)
    ff_arrays = _split_wait("ag_ff_land", send0, recv0, ff_arrays, _plan_gather_send(n_ff), 3 + 7 * n_mix, pooled)
    send2, recv2, ff_full, token = _split_start("ag_ff_pass", ff_arrays[n_ff:], 3 * n_ff, _plan_gather_pass(n_ff))
    ff_full = _split_wait("ag_ff_passed", send2, recv2, ff_full, _plan_gather_pass(n_ff), 0, token)
    w_ff1_4 = ff_full[0].reshape(N_SHARD, D, D)
    w_ff2 = ff_full[1].reshape(N_SHARD, D, D)
    hn, f1, f, df2, dout, vec_a = _ffn_fwd(h1, row(p["norm2_pre_g"]), w_ff1_4, w_ff2, row(p["norm2_post_g"]),
                                           target, tm_ffn)

    tk = min(2048, t)
    g_big = {"w_ff2": _dw(f, df2, "dw_ff2", 2, D, tk, N_SHARD, lambda i, j: i)}
    df1, dh1, dy, vec_b = _ffn_bwd(df2, f1, w_ff1_4, w_ff2, h1, dout, y, row(p["norm2_pre_g"]),
                                   row(p["norm1_post_g"]), tm_ffn)
    g_big["w_ff1"] = _dw(hn, df1, "dw_ff1", 2, D, tk, N_SHARD, lambda i, j: j)

    def reduce_start(name, keys, wholes, also=(), also_plan=None, also_copies=0):
        swapped = _swap_halves("rs_swap_" + name, [g_big[k] for k in keys], wholes)
        sums = _chip_sums(core, [g_big[k] for k in keys], swapped[:len(keys)], "chip_sums_" + name)
        arrays = list(sums)
        if wholes:
            arrays.append(_slot_sum(jnp.stack([wholes[0], swapped[-1]]), "chip_sum_small"))
        arrays += [lax.empty(s.shape, BF16) for s in sums]
        if wholes:
            arrays.append(lax.empty((N_SHARD,) + wholes[0].shape, F32))
        plan = _plan_exchange(len(keys), bool(wholes))
        n_copies, n = 3 * len(keys) + 4 * len(wholes), len(arrays)
        both = plan if also_plan is None else _plans((plan, n), (also_plan, len(also)))
        send, recv, out, tok = _split_start("rs_send_" + name, arrays + list(also), n_copies + also_copies, both)
        return send, recv, out[:n], plan, tok, out[n:], n_copies

    def reduce_finish(name, keys, state, *after):
        send, recv, arrays, plan = state[:4]
        arrays = _split_wait("rs_land_" + name, send, recv, arrays, plan, 0, *after)
        n_in = len(arrays) // 2
        n = len(keys)
        finished = _final_sums(chip, arrays[:n], arrays[n_in:n_in + n], "final_sums_" + name)
        return dict(zip(keys, finished)), arrays

    by_half = lambda a, rows: a.reshape(N_SHARD, 2, rows // 2, a.shape[-1]).transpose(1, 0, 2, 3)
    g_big["w_out"] = _dw(merged, dy, "dw_out", 2 * N_SHARD, D, tk, N_SHARD, lambda i, j: 0)
    early = _FF + ("w_out",)
    late = ("w_in", "w_pool", "w_sgu_proj")
    early_state = reduce_start("early", early, [])
    dz, dwp, dwpr, dws, dbsp, vec_c, dbin = _mix_bwd(z, pooled, dy, w_pool, row(p["pool_scale"]), row(p["sgu_ln_g"]),
                                                     row(p["sgu_ln_b"]), ws, ws_t, bsp, w_proj, w_out, mask, tm_mix,
                                                     early_state[4])
    g_big["w_in"] = _dw(xn, dz, "dw_in", 2, D_IN // N_SHARD, tk, N_SHARD, lambda i, j: j)
    shard_major = lambda a: a.reshape(N_HEADS, N_SHARD, 64, 256).transpose(1, 0, 2, 3).reshape(N_SHARD * 256, 256)
    g_big["w_pool"] = by_half(shard_major(dwp).astype(BF16), 256)
    g_big["w_sgu_proj"] = by_half(shard_major(dwpr).astype(BF16), 256)
    halves, _ = reduce_finish("early", early, early_state, g_big["w_in"])

    small_part = _pack(_SMALL[1:], SMALL_ROWS - 8, {
        "b_in": dbin[0], "pool_scale": vec_c[0], "sgu_ln_g": vec_c[1], "sgu_ln_b": vec_c[2],
        "w_spatial": dws, "b_spatial": dbsp[:, ::LANES].T, "norm1_post_g": vec_b[1], "norm2_pre_g": vec_b[0],
        "norm2_post_g": vec_a[0]})
    share = [halves[k] for k in early] + [lax.empty(halves[k].shape, F32) for k in early]
    late_state = reduce_start("late", late, [small_part], share, _plan_share(len(early)), len(early))
    grad_x, vec_d = _dx(dz, w_in4, x, dh1, row(p["norm1_pre_g"]), late_state[4], tm_ffn)
    share = _split_wait("rs_share_early_land", late_state[0], late_state[1], late_state[5], _plan_share(len(early)),
                        late_state[6], grad_x)

    grads, delta, new_m, new_v = {}, {}, {}, {}

    def update(name, keys, own, shared):
        as2d = lambda a, k: a.reshape(local2d[k].shape)
        outs = _adamw_halves(core, own, shared, [local2d[k] for k in keys],
                             [as2d(m[k], k) for k in keys], [as2d(v[k], k) for k in keys], "adamw_" + name)
        for k, outs_k in zip(keys, outs):
            grads[k], delta[k], new_m[k], new_v[k] = (a.reshape(p[k].shape) for a in outs_k)

    update("early", early, share[:len(early)], share[len(early):])
    late_halves, late_arrays = reduce_finish("late", late, late_state, grad_x, *[new_v[k] for k in early])
    own_late = [late_halves[k] for k in late]
    shared_late, tiles = _share_and_all_reduce(
        own_late, jnp.concatenate([vec_d[0].reshape(8, LANES), jnp.broadcast_to(vec_a[1:2, :LANES], (8, LANES))]))
    g1_grad, loss = tiles[:8], tiles[8, 0]
    update("late", late, own_late, shared_late)
    as_rows = lambda d: [d[k].reshape(-1, LANES) for k in _SMALL]
    small_outs = _adamw_small([g1_grad, late_arrays[-1]], as_rows(p), as_rows(m), as_rows(v))
    for k, outs_k in zip(_SMALL, small_outs):
        grads[k], delta[k], new_m[k], new_v[k] = (a.reshape(p[k].shape) for a in outs_k)

    return (loss, grad_x, *[grads[k] for k in _ORDER], *[delta[k] for k in _ORDER],
            *[new_m[k] for k in _ORDER], *[new_v[k] for k in _ORDER])


def kernel(x, norm1_pre_g, w_in, b_in, w_pool, pool_scale, sgu_ln_g, sgu_ln_b, w_spatial, b_spatial, w_sgu_proj, w_out, norm1_post_g, norm2_pre_g, w_ff1, w_ff2, norm2_post_g, loss_target, m_norm1_pre_g, m_w_in, m_b_in, m_w_pool, m_pool_scale, m_sgu_ln_g, m_sgu_ln_b, m_w_spatial, m_b_spatial, m_w_sgu_proj, m_w_out, m_norm1_post_g, m_norm2_pre_g, m_w_ff1, m_w_ff2, m_norm2_post_g, v_norm1_pre_g, v_w_in, v_b_in, v_w_pool, v_pool_scale, v_sgu_ln_g, v_sgu_ln_b, v_w_spatial, v_b_spatial, v_w_sgu_proj, v_w_out, v_norm1_post_g, v_norm2_pre_g, v_w_ff1, v_w_ff2, v_norm2_post_g):
    p = dict(zip(_ORDER, (norm1_pre_g, w_in, b_in, w_pool, pool_scale, sgu_ln_g, sgu_ln_b, w_spatial, b_spatial,
                          w_sgu_proj, w_out, norm1_post_g, norm2_pre_g, w_ff1, w_ff2, norm2_post_g)))
    m = dict(zip(_ORDER, (m_norm1_pre_g, m_w_in, m_b_in, m_w_pool, m_pool_scale, m_sgu_ln_g, m_sgu_ln_b, m_w_spatial,
                          m_b_spatial, m_w_sgu_proj, m_w_out, m_norm1_post_g, m_norm2_pre_g, m_w_ff1, m_w_ff2,
                          m_norm2_post_g)))
    v = dict(zip(_ORDER, (v_norm1_pre_g, v_w_in, v_b_in, v_w_pool, v_pool_scale, v_sgu_ln_g, v_sgu_ln_b, v_w_spatial,
                          v_b_spatial, v_w_sgu_proj, v_w_out, v_norm1_post_g, v_norm2_pre_g, v_w_ff1, v_w_ff2,
                          v_norm2_post_g)))
    outs = _step(p, m, v, x[0], loss_target[0])
    return (outs[0], outs[1][None], *outs[2:])
```

```python
import math

import jax
import jax.numpy as jnp
from jax import lax
from jax.experimental import pallas as pl
from jax.experimental.pallas import tpu as pltpu

F32 = jnp.float32
BF16 = jnp.bfloat16

D = 1024
D_IN = 5 * D
D_FF = 4 * D
N_SHARD = 4
POOL_WINDOWS = (2, 4, 8, 16)
POOL_GROUP = 256
POOL_HALO = 16
SGU_BLOCK = 128
N_HEADS = 4
HEAD = 256
EPS = 1e-6

ADAM_LR = 0.001
ADAM_B1 = 0.9
ADAM_B2 = 0.999
ADAM_EPS = 1e-08
ADAM_WD = 0.01
ADAM_STEP = 10

V7X_VMEM_BYTES = 64 * 1024 * 1024
VMEM_LIMIT = V7X_VMEM_BYTES - 8 * 1024 * 1024
SMALL_ROWS = 616
LANES = 128

MESH = pl.DeviceIdType.MESH
ANY = pl.BlockSpec(memory_space=pl.ANY)
HBM = pl.BlockSpec(memory_space=pltpu.HBM)
SEM = pl.BlockSpec(memory_space=pltpu.SEMAPHORE)
EFFECT = pltpu.SideEffectType.DATAFLOW_SIDE_EFFECTING

NT_DIMS = (((1,), (1,)), ((), ()))
TN_DIMS = (((0,), (0,)), ((), ()))


def _params(n_axes):
    return pltpu.CompilerParams(dimension_semantics=("arbitrary",) * n_axes, vmem_limit_bytes=VMEM_LIMIT)


def _resident(shape, n_axes):
    zeros = (0,) * len(shape)
    index = (lambda i: zeros) if n_axes == 1 else (lambda i, j: zeros)
    return pl.BlockSpec(shape, index, pipeline_mode=pl.Buffered(1))


def _dot(a, b):
    return jnp.dot(a, b, preferred_element_type=F32)


def _dot_nt(a, b):
    return lax.dot_general(a, b, NT_DIMS, preferred_element_type=F32)


def _dot_tn(a, b):
    return lax.dot_general(a, b, TN_DIMS, preferred_element_type=F32)


def _mean(a):
    return jnp.mean(a, axis=-1, keepdims=True)


def _colsum(a):
    return jnp.sum(a, axis=0, keepdims=True)


_GELU_C0 = math.sqrt(2.0 / math.pi)
_GELU_C1 = 0.044715


def _gelu(a):
    t = jnp.tanh(a * (_GELU_C0 + (_GELU_C0 * _GELU_C1) * (a * a)))
    return a * (0.5 * t + 0.5)


def _gelu_and_grad(a):
    a2 = a * a
    t = jnp.tanh(a * (_GELU_C0 + (_GELU_C0 * _GELU_C1) * a2))
    cdf = 0.5 * t + 0.5
    grad = cdf + (0.5 * a) * (1.0 - t * t) * (_GELU_C0 + (3.0 * _GELU_C0 * _GELU_C1) * a2)
    return a * cdf, grad


def _sigmoid(a):
    return 0.5 * jnp.tanh(0.5 * a) + 0.5


def _rms_norm_bf16(a, g):
    return (a * lax.rsqrt(_mean(a * a) + EPS) * g).astype(BF16)


def _relu_sq_bf16(a):
    r = jnp.maximum(a, 0.0)
    return (r * r).astype(BF16)


def _window_sum(a, w, ahead):
    n = a.shape[0]
    step = 1
    while step < w:
        a = a + pltpu.roll(a, n - step if ahead else step, 0)
        step *= 2
    return a


def _rms_bwd(dn, n, r):
    return r * (dn - n * _mean(dn * n))


def _inproj_own(x, g1, w_own, b_in, chip, dep, tm):
    t = x.shape[0]
    cw = D_IN // N_SHARD

    def body(chip_ref, x_ref, g_ref, w_ref, b_ref, dep_ref, z_ref, xn_ref):
        xn = _rms_norm_bf16(x_ref[...], g_ref[...])
        xn_ref[...] = xn
        z_ref[...] = _dot(xn, w_ref[...]) + b_ref[...]

    return pl.pallas_call(
        body, name="inproj_own",
        grid_spec=pltpu.PrefetchScalarGridSpec(
            num_scalar_prefetch=1, grid=(t // tm,),
            in_specs=[pl.BlockSpec((tm, D), lambda i, chip: (i, 0)),
                      pl.BlockSpec((1, D), lambda i, chip: (0, 0)),
                      pl.BlockSpec((D, cw), lambda i, chip: (0, 0)),
                      pl.BlockSpec((1, cw), lambda i, chip: (0, chip[0])),
                      pl.BlockSpec(dep.shape, lambda i, chip: (0, 0))],
            out_specs=[pl.BlockSpec((tm, cw), lambda i, chip: (i, chip[0])),
                       pl.BlockSpec((tm, D), lambda i, chip: (i, 0))]),
        out_shape=[jax.ShapeDtypeStruct((t, D_IN), F32), jax.ShapeDtypeStruct((t, D), BF16)],
        compiler_params=_params(1),
    )(chip, x, g1, w_own, b_in, dep)


def _inproj_rest(xn, w_in4, b_in, chip, z, tm):
    t = xn.shape[0]
    cw = D_IN // N_SHARD
    shard_of = lambda s, chip: lax.rem(chip[0] + 1 + s, N_SHARD)

    def body(chip_ref, xn_ref, w_ref, b_ref, z_in_ref, z_ref):
        z_ref[...] = _dot(xn_ref[...], w_ref[shard_of(pl.program_id(1), chip_ref)]) + b_ref[...]

    return pl.pallas_call(
        body, name="inproj_rest",
        grid_spec=pltpu.PrefetchScalarGridSpec(
            num_scalar_prefetch=1, grid=(t // tm, N_SHARD - 1),
            in_specs=[pl.BlockSpec((tm, D), lambda i, s, chip: (i, 0)),
                      pl.BlockSpec((N_SHARD, D, cw), lambda i, s, chip: (0, 0, 0), pipeline_mode=pl.Buffered(1)),
                      pl.BlockSpec((1, cw), lambda i, s, chip: (0, shard_of(s, chip))),
                      ANY],
            out_specs=pl.BlockSpec((tm, cw), lambda i, s, chip: (i, shard_of(s, chip)))),
        out_shape=jax.ShapeDtypeStruct(z.shape, F32),
        input_output_aliases={4: 0},
        compiler_params=_params(2),
    )(chip, xn, w_in4, b_in, z)


def _sgu_forward(z_ref, lng_ref, lnb_ref, ws_ref, bsp_ref, sv_ref, tm, with_grad):
    if with_grad:
        u, du_dz = _gelu_and_grad(z_ref[:, D:2 * D])
        gv, dgv_dz = _gelu_and_grad(z_ref[:, 2 * D:3 * D])
    else:
        u = _gelu(z_ref[:, D:2 * D])
        gv = _gelu(z_ref[:, 2 * D:3 * D])
        du_dz = dgv_dz = None
    xc = gv - _mean(gv)
    rstd = lax.rsqrt(_mean(xc * xc) + EPS)
    vhat = xc * rstd
    vb = (vhat * lng_ref[...] + lnb_ref[...]).astype(BF16)
    for b in range(tm // SGU_BLOCK):
        rows = slice(b * SGU_BLOCK, (b + 1) * SGU_BLOCK)
        for h in range(N_HEADS):
            cols = slice(h * HEAD, (h + 1) * HEAD)
            sv_ref[rows, cols] = _dot(ws_ref[h], vb[rows, cols]) + bsp_ref[h]
    return u, du_dz, dgv_dz, vhat, rstd, vb


def _mix_fwd(z, x, w_pool, pool_scale, ln_g, ln_b, ws, bsp, w_proj, w_out, g1post, tm):
    t = x.shape[0]

    def body(z_ref, x_ref, wp_ref, psc_ref, lng_ref, lnb_ref, ws_ref, bsp_ref, wpr_ref, wo_ref, g1p_ref,
             pooled_ref, merged_ref, y_ref, h1_ref, win_ref, sv_ref):
        i = pl.program_id(0)

        @pl.when(i == 0)
        def _():
            win_ref[0:POOL_HALO, :] = jnp.zeros((POOL_HALO, D), F32)

        win_ref[POOL_HALO:POOL_HALO + tm, :] = z_ref[:, 0:D]
        pos1 = lax.broadcasted_iota(jnp.int32, (tm, 1), 0) + (i * tm + 1)
        a_parts = []
        for g, w in enumerate(POOL_WINDOWS):
            cols = slice(g * POOL_GROUP, (g + 1) * POOL_GROUP)
            ts = _window_sum(win_ref[:, cols], w, False)[POOL_HALO:POOL_HALO + tm, :]
            inv_cnt = 1.0 / jnp.minimum(pos1, w).astype(F32)
            pooled = (ts * inv_cnt - z_ref[:, cols]).astype(BF16)
            pooled_ref[:, cols] = pooled
            a_parts.append(_dot(pooled, wp_ref[g]) * psc_ref[:, cols])
        win_ref[0:POOL_HALO, :] = win_ref[tm:tm + POOL_HALO, :]

        u, _, _, _, _, _ = _sgu_forward(z_ref, lng_ref, lnb_ref, ws_ref, bsp_ref, sv_ref, tm, False)
        gated = (u * sv_ref[...]).astype(BF16)
        for h in range(N_HEADS):
            lo = h * HEAD
            bbr = _dot(gated[:, lo:lo + HEAD], wpr_ref[h])
            gate_a = _sigmoid(z_ref[:, 3 * D + lo:3 * D + lo + HEAD])
            gate_b = _sigmoid(z_ref[:, 4 * D + lo:4 * D + lo + HEAD])
            merged_ref[:, lo:lo + HEAD] = (gate_a * a_parts[h] + gate_b * bbr).astype(BF16)
        merged = merged_ref[...]
        y = _dot(merged, wo_ref[...])
        y_ref[...] = y
        r = lax.rsqrt(_mean(y * y) + EPS)
        h1_ref[...] = x_ref[...] + (y * r) * g1p_ref[...]

    row = lambda i: (i, 0)
    fixed2 = lambda i: (0, 0)
    fixed3 = lambda i: (0, 0, 0)
    vec = pl.BlockSpec((1, D), fixed2)
    return pl.pallas_call(
        body, name="mix_fwd",
        grid=(t // tm,),
        in_specs=[pl.BlockSpec((tm, D_IN), row), pl.BlockSpec((tm, D), row),
                  pl.BlockSpec((N_HEADS, POOL_GROUP, POOL_GROUP), fixed3), vec, vec, vec,
                  pl.BlockSpec((N_HEADS, SGU_BLOCK, SGU_BLOCK), fixed3),
                  pl.BlockSpec((N_HEADS, SGU_BLOCK, 1), fixed3),
                  pl.BlockSpec((N_HEADS, HEAD, HEAD), fixed3),
                  pl.BlockSpec((D, D), fixed2), vec],
        out_specs=[pl.BlockSpec((tm, D), row)] * 4,
        out_shape=[jax.ShapeDtypeStruct((t, D), BF16), jax.ShapeDtypeStruct((t, D), BF16),
                   jax.ShapeDtypeStruct((t, D), F32), jax.ShapeDtypeStruct((t, D), F32)],
        scratch_shapes=[pltpu.VMEM((tm + POOL_HALO, D), F32), pltpu.VMEM((tm, D), F32)],
        compiler_params=_params(1),
    )(z, x, w_pool, pool_scale, ln_g, ln_b, ws, bsp, w_proj, w_out, g1post)


def _ffn_fwd(h1, g2pre, w1, w2, g2post, target, tm):
    t = h1.shape[0]
    n_j = D_FF // D

    def body(h1_ref, g2_ref, w1_ref, w2_ref, g2p_ref, tgt_ref,
             hn_ref, f1_ref, f_ref, df2_ref, dout_ref, vec_ref):
        @pl.when(pl.program_id(0) == 0)
        def _():
            vec_ref[...] = jnp.zeros_like(vec_ref)

        hn = _rms_norm_bf16(h1_ref[...], g2_ref[...])
        hn_ref[...] = hn
        f2 = None
        for j in range(n_j):
            cols = slice(j * D, (j + 1) * D)
            f1 = _dot(hn, w1_ref[j])
            f1_ref[:, cols] = f1
            f = _relu_sq_bf16(f1)
            f_ref[:, cols] = f
            part = _dot(f, w2_ref[j])
            f2 = part if f2 is None else f2 + part
        r = lax.rsqrt(_mean(f2 * f2) + EPS)
        n2 = f2 * r
        err = h1_ref[...] + n2 * g2p_ref[...] - tgt_ref[...]
        loss = 0.5 * jnp.sum(_mean(err * err))
        dout = err * (1.0 / D)
        dout_ref[...] = dout
        vec_ref[0:1, :] += _colsum(dout * n2)
        vec_ref[1:2, :] += jnp.full((1, D), loss, F32)
        df2_ref[...] = _rms_bwd(dout * g2p_ref[...], n2, r).astype(BF16)

    row = lambda i: (i, 0)
    fixed = lambda i: (0, 0)
    vec = pl.BlockSpec((1, D), fixed)
    return pl.pallas_call(
        body, name="ffn_fwd",
        grid=(t // tm,),
        in_specs=[pl.BlockSpec((tm, D), row), vec,
                  _resident((n_j, D, D), 1), _resident((n_j, D, D), 1), vec,
                  pl.BlockSpec((tm, D), row)],
        out_specs=[pl.BlockSpec((tm, D), row),
                   pl.BlockSpec((tm, D_FF), row), pl.BlockSpec((tm, D_FF), row),
                   pl.BlockSpec((tm, D), row), pl.BlockSpec((tm, D), row),
                   pl.BlockSpec((8, D), fixed)],
        out_shape=[jax.ShapeDtypeStruct((t, D), BF16), jax.ShapeDtypeStruct((t, D_FF), F32),
                   jax.ShapeDtypeStruct((t, D_FF), BF16), jax.ShapeDtypeStruct((t, D), BF16),
                   jax.ShapeDtypeStruct((t, D), F32), jax.ShapeDtypeStruct((8, D), F32)],
        compiler_params=_params(1),
    )(h1, g2pre, w1, w2, g2post, target)


def _ffn_bwd(df2, f1, w1, w2, h1, dout, y, g2pre, g1post, tm):
    t = h1.shape[0]
    n_j = D_FF // D

    def body(df2_ref, f1_ref, w1_ref, w2_ref, h1_ref, dout_ref, y_ref, g2_ref, g1p_ref,
             df1_ref, dh1_ref, dy_ref, vec_ref):
        @pl.when(pl.program_id(0) == 0)
        def _():
            vec_ref[...] = jnp.zeros_like(vec_ref)

        df2 = df2_ref[...]
        dhn = None
        for j in range(n_j):
            cols = slice(j * D, (j + 1) * D)
            df = _dot_nt(df2, w2_ref[j])
            df1 = (df * (2.0 * jnp.maximum(f1_ref[:, cols], 0.0))).astype(BF16)
            df1_ref[:, cols] = df1
            part = _dot_nt(df1, w1_ref[j])
            dhn = part if dhn is None else dhn + part
        h = h1_ref[...]
        r2 = lax.rsqrt(_mean(h * h) + EPS)
        nh = h * r2
        vec_ref[0:1, :] += _colsum(dhn * nh)
        dh1 = dout_ref[...] + _rms_bwd(dhn * g2_ref[...], nh, r2)
        dh1_ref[...] = dh1
        yv = y_ref[...]
        r1 = lax.rsqrt(_mean(yv * yv) + EPS)
        ny = yv * r1
        vec_ref[1:2, :] += _colsum(dh1 * ny)
        dy_ref[...] = _rms_bwd(dh1 * g1p_ref[...], ny, r1).astype(BF16)

    row = lambda i: (i, 0)
    fixed = lambda i: (0, 0)
    vec = pl.BlockSpec((1, D), fixed)
    return pl.pallas_call(
        body, name="ffn_bwd",
        grid=(t // tm,),
        in_specs=[pl.BlockSpec((tm, D), row), pl.BlockSpec((tm, D_FF), row),
                  _resident((n_j, D, D), 1), _resident((n_j, D, D), 1),
                  pl.BlockSpec((tm, D), row), pl.BlockSpec((tm, D), row), pl.BlockSpec((tm, D), row), vec, vec],
        out_specs=[pl.BlockSpec((tm, D_FF), row),
                   pl.BlockSpec((tm, D), row), pl.BlockSpec((tm, D), row),
                   pl.BlockSpec((8, D), fixed)],
        out_shape=[jax.ShapeDtypeStruct((t, D_FF), BF16), jax.ShapeDtypeStruct((t, D), F32),
                   jax.ShapeDtypeStruct((t, D), BF16), jax.ShapeDtypeStruct((8, D), F32)],
        compiler_params=_params(1),
    )(df2, f1, w1, w2, h1, dout, y, g2pre, g1post)


def _mix_bwd(z, pooled, dy, w_pool, pool_scale, ln_g, ln_b, ws, ws_t, bsp, w_proj, w_out, mask, tm, dep):
    t = z.shape[0]
    n_t = t // tm

    def body(z_ref, pooled_ref, dy_ref, wp_ref, psc_ref, lng_ref, lnb_ref, ws_ref, wst_ref, bsp_ref, wpr_ref,
             wo_ref, mask_ref, dep_ref,
             dz_ref, dwp_ref, dwpr_ref, dws_ref, dbsp_ref, vec_ref, dbin_ref,
             win_ref, sv_ref, dsv_ref, dv_ref, dsvsum_ref):
        i = pl.program_id(0)

        @pl.when(i == 0)
        def _():
            win_ref[tm:tm + POOL_HALO, :] = jnp.zeros((POOL_HALO, D), F32)
            dwp_ref[...] = jnp.zeros_like(dwp_ref)
            dwpr_ref[...] = jnp.zeros_like(dwpr_ref)
            dws_ref[...] = jnp.zeros_like(dws_ref)
            vec_ref[...] = jnp.zeros_like(vec_ref)
            dbin_ref[...] = jnp.zeros_like(dbin_ref)
            dsvsum_ref[...] = jnp.zeros_like(dsvsum_ref)

        dmerged = _dot_nt(dy_ref[...], wo_ref[...])

        u, du_dz, dgv_dz, vhat, rstd, vb = _sgu_forward(z_ref, lng_ref, lnb_ref, ws_ref, bsp_ref, sv_ref, tm, True)
        sv = sv_ref[...]
        gated = (u * sv).astype(BF16)

        sa = _sigmoid(z_ref[:, 3 * D:4 * D])
        sb = _sigmoid(z_ref[:, 4 * D:5 * D])
        da = dmerged * sa
        dbbr = (dmerged * sb).astype(BF16)

        tile = n_t - 1 - i
        pos1 = lax.broadcasted_iota(jnp.int32, (tm, 1), 0) + (tile * tm + 1)
        def put(k, lo, part):
            cols = slice(k * D + lo, k * D + lo + part.shape[1])
            dz_ref[:, cols] = part.astype(BF16)
            dbin_ref[0:1, cols] += _colsum(part)

        dpooled_parts = []
        for g, w in enumerate(POOL_WINDOWS):
            cols = slice(g * POOL_GROUP, (g + 1) * POOL_GROUP)
            pg = pooled_ref[:, cols]
            a_pre = _dot(pg, wp_ref[g])
            da_g = da[:, cols]
            psc = psc_ref[:, cols]
            vec_ref[0:1, cols] += _colsum(da_g * a_pre)
            sa_g = sa[:, cols]
            put(3, g * POOL_GROUP, dmerged[:, cols] * (a_pre * psc) * (sa_g * (1.0 - sa_g)))
            da_pre = (da_g * psc).astype(BF16)
            dwp_ref[g] += _dot_tn(pg, da_pre)
            dpooled = _dot_nt(da_pre, wp_ref[g])
            dpooled_parts.append(dpooled)
            inv_cnt = 1.0 / jnp.minimum(pos1, w).astype(F32)
            win_ref[0:tm, cols] = dpooled * inv_cnt
        for g, w in enumerate(POOL_WINDOWS):
            cols = slice(g * POOL_GROUP, (g + 1) * POOL_GROUP)
            acc = _window_sum(win_ref[:, cols], w, True)[0:tm, :]
            put(0, g * POOL_GROUP, acc - dpooled_parts[g])
        win_ref[tm:tm + POOL_HALO, :] = win_ref[0:POOL_HALO, :]

        for h in range(N_HEADS):
            cols = slice(h * HEAD, (h + 1) * HEAD)
            g_h = gated[:, cols]
            db_h = dbbr[:, cols]
            bbr_h = _dot(g_h, wpr_ref[h])
            sb_h = sb[:, cols]
            put(4, h * HEAD, dmerged[:, cols] * bbr_h * (sb_h * (1.0 - sb_h)))
            dwpr_ref[h] += _dot_tn(g_h, db_h)
            dgated_h = _dot_nt(db_h, wpr_ref[h])
            dsv_ref[:, cols] = dgated_h * u[:, cols]
            sv_ref[:, cols] = dgated_h * sv[:, cols]
        put(1, 0, sv_ref[...] * du_dz)
        dsv = dsv_ref[...]
        dsvb = dsv.astype(BF16)
        blk_sum = dsv[0:SGU_BLOCK, :]
        for b in range(1, tm // SGU_BLOCK):
            blk_sum = blk_sum + dsv[b * SGU_BLOCK:(b + 1) * SGU_BLOCK, :]
        dsvsum_ref[...] += blk_sum
        for b in range(tm // SGU_BLOCK):
            rows = slice(b * SGU_BLOCK, (b + 1) * SGU_BLOCK)
            for h in range(N_HEADS):
                cols = slice(h * HEAD, (h + 1) * HEAD)
                dws_ref[h] += _dot_nt(dsvb[rows, cols], vb[rows, cols])
                dv_ref[rows, cols] = _dot(wst_ref[h], dsvb[rows, cols])
        dv = dv_ref[...]
        vec_ref[1:2, :] += _colsum(dv * vhat)
        vec_ref[2:3, :] += _colsum(dv)
        dvhat = dv * lng_ref[...]
        dgv = rstd * (dvhat - _mean(dvhat) - vhat * _mean(dvhat * vhat))
        put(2, 0, dgv * dgv_dz)

        @pl.when(i == n_t - 1)
        def _():
            for h in range(N_HEADS):
                dws_ref[h] = jnp.where(mask_ref[...] > 0.0, dws_ref[h], 0.0)
                tot = jnp.sum(dsvsum_ref[:, h * HEAD:(h + 1) * HEAD], axis=1, keepdims=True)
                dbsp_ref[:, h * LANES:(h + 1) * LANES] = jnp.broadcast_to(tot, (SGU_BLOCK, LANES))

    row = lambda i: (n_t - 1 - i, 0)
    fixed2 = lambda i: (0, 0)
    fixed3 = lambda i: (0, 0, 0)
    vec = pl.BlockSpec((1, D), fixed2)
    sq = pl.BlockSpec((N_HEADS, SGU_BLOCK, SGU_BLOCK), fixed3)
    grp = pl.BlockSpec((N_HEADS, HEAD, HEAD), fixed3)
    return pl.pallas_call(
        body, name="mix_bwd",
        grid=(n_t,),
        in_specs=[pl.BlockSpec((tm, D_IN), row), pl.BlockSpec((tm, D), row), pl.BlockSpec((tm, D), row),
                  grp, vec, vec, vec, sq, sq, pl.BlockSpec((N_HEADS, SGU_BLOCK, 1), fixed3), grp,
                  pl.BlockSpec((D, D), fixed2), pl.BlockSpec((SGU_BLOCK, SGU_BLOCK), fixed2),
                  pl.BlockSpec(dep.shape, fixed2)],
        out_specs=[pl.BlockSpec((tm, D_IN), row), grp, grp, sq,
                   pl.BlockSpec((SGU_BLOCK, N_HEADS * LANES), fixed2),
                   pl.BlockSpec((8, D), fixed2), pl.BlockSpec((8, D_IN), fixed2)],
        out_shape=[jax.ShapeDtypeStruct((t, D_IN), BF16),
                   jax.ShapeDtypeStruct((N_HEADS, HEAD, HEAD), F32), jax.ShapeDtypeStruct((N_HEADS, HEAD, HEAD), F32),
                   jax.ShapeDtypeStruct((N_HEADS, SGU_BLOCK, SGU_BLOCK), F32),
                   jax.ShapeDtypeStruct((SGU_BLOCK, N_HEADS * LANES), F32),
                   jax.ShapeDtypeStruct((8, D), F32), jax.ShapeDtypeStruct((8, D_IN), F32)],
        scratch_shapes=[pltpu.VMEM((tm + POOL_HALO, D), F32), pltpu.VMEM((tm, D), F32), pltpu.VMEM((tm, D), F32),
                        pltpu.VMEM((tm, D), F32), pltpu.VMEM((SGU_BLOCK, D), F32)],
        compiler_params=_params(1),
    )(z, pooled, dy, w_pool, pool_scale, ln_g, ln_b, ws, ws_t, bsp, w_proj, w_out, mask, dep)


def _dx(dz, w_in4, x, dh1, g1, dep, tm):
    t = x.shape[0]
    cw = D_IN // N_SHARD

    def body(dz_ref, w_ref, x_ref, dh1_ref, g_ref, dep_ref, gx_ref, vec_ref):
        @pl.when(pl.program_id(0) == 0)
        def _():
            vec_ref[...] = jnp.zeros_like(vec_ref)

        dxn = _dot_nt(dz_ref[:, 0:cw], w_ref[0])
        for s in range(1, N_SHARD):
            dxn = dxn + _dot_nt(dz_ref[:, s * cw:(s + 1) * cw], w_ref[s])
        xv = x_ref[...]
        r = lax.rsqrt(_mean(xv * xv) + EPS)
        xh = xv * r
        vec_ref[0:1, :] += _colsum(dxn * xh)
        gx_ref[...] = dh1_ref[...] + _rms_bwd(dxn * g_ref[...], xh, r)

    row = lambda i: (i, 0)
    fixed = lambda i: (0, 0)
    return pl.pallas_call(
        body, name="dx",
        grid=(t // tm,),
        in_specs=[pl.BlockSpec((tm, D_IN), row),
                  _resident((N_SHARD, D, cw), 1),
                  pl.BlockSpec((tm, D), row), pl.BlockSpec((tm, D), row), pl.BlockSpec((1, D), fixed),
                  pl.BlockSpec(dep.shape, fixed)],
        out_specs=[pl.BlockSpec((tm, D), row), pl.BlockSpec((8, D), fixed)],
        out_shape=[jax.ShapeDtypeStruct((t, D), F32), jax.ShapeDtypeStruct((8, D), F32)],
        compiler_params=_params(1),
    )(dz, w_in4, x, dh1, g1, dep)


def _dw(a, b, name, n_split, tn, tk, n_out, out_block):
    t, m = a.shape
    n = b.shape[1]
    tm = min(m, D)
    rows = tm // n_split
    n_k = t // tk

    def body(a_ref, b_ref, o_ref, acc_ref):
        k = pl.program_id(2)

        @pl.when(k == 0)
        def _():
            acc_ref[...] = jnp.zeros_like(acc_ref)

        acc_ref[...] += _dot_tn(a_ref[...], b_ref[...])

        @pl.when(k == n_k - 1)
        def _():
            for q in range(n_split):
                o_ref[q % 2, q // 2] = acc_ref[q * rows:(q + 1) * rows, :].astype(BF16)

    return pl.pallas_call(
        body, name=name,
        grid=(m // tm, n // tn, n_k),
        in_specs=[pl.BlockSpec((tk, tm), lambda i, j, k: (k, i)),
                  pl.BlockSpec((tk, tn), lambda i, j, k: (k, j))],
        out_specs=pl.BlockSpec((2, n_split // 2, rows, tn), lambda i, j, k: (0, out_block(i, j), 0, 0)),
        out_shape=jax.ShapeDtypeStruct((2, n_out, rows, tn), BF16),
        scratch_shapes=[pltpu.VMEM((tm, tn), F32)],
        compiler_params=_params(3),
    )(a, b)


def _row_block(rows, cols, itemsize, n_bufs):
    budget = VMEM_LIMIT // 4
    rb = rows
    while rb % 16 == 0 and rb * cols * itemsize * n_bufs * 2 > budget:
        rb //= 2
    return rb


ROW_STEPS = 2


def _chip_sums(core, grads, recvs, name):
    n = len(grads)

    def body(c_ref, *refs):
        for w in range(n):
            refs[2 * n + w][...] = (refs[w][0].astype(F32) + refs[n + w][...].astype(F32)).astype(BF16)

    def blocks(g):
        _, _, rh, cols = g.shape
        rb = rh // ROW_STEPS
        return (pl.BlockSpec((1, 1, rb, cols), lambda s, r, c: (c[0], s, r, 0)),
                pl.BlockSpec((1, rb, cols), lambda s, r, c: (s, r, 0)))

    specs = [blocks(g) for g in grads]
    return pl.pallas_call(
        body, name=name,
        grid_spec=pltpu.PrefetchScalarGridSpec(
            num_scalar_prefetch=1, grid=(N_SHARD, ROW_STEPS),
            in_specs=[s[0] for s in specs] + [s[1] for s in specs],
            out_specs=[s[1] for s in specs]),
        out_shape=[jax.ShapeDtypeStruct(g.shape[1:], BF16) for g in grads],
        compiler_params=_params(2),
    )(core, *grads, *recvs)


def _slot_sum(slots, name):
    n_s, rows, cols = slots.shape
    rb = _row_block(rows, cols, 4, n_s + 1)

    def body(s_ref, o_ref):
        acc = s_ref[0].astype(F32)
        for k in range(1, n_s):
            acc = acc + s_ref[k].astype(F32)
        o_ref[...] = acc

    return pl.pallas_call(
        body, name=name,
        grid=(rows // rb,),
        in_specs=[pl.BlockSpec((n_s, rb, cols), lambda r: (0, r, 0))],
        out_specs=pl.BlockSpec((rb, cols), lambda r: (r, 0)),
        out_shape=jax.ShapeDtypeStruct((rows, cols), F32),
        compiler_params=_params(1),
    )(slots)


def _adamw_update(g, w, m, v):
    m_new = ADAM_B1 * m + (1.0 - ADAM_B1) * g
    v_new = ADAM_B2 * v + (1.0 - ADAM_B2) * (g * g)
    m_hat = m_new / (1.0 - ADAM_B1 ** ADAM_STEP)
    v_hat = v_new / (1.0 - ADAM_B2 ** ADAM_STEP)
    return -ADAM_LR * (m_hat / (jnp.sqrt(v_hat) + ADAM_EPS) + ADAM_WD * w), m_new, v_new


def _adamw_small(grad_parts, ws, ms, vs):
    n, n_g = len(ws), len(grad_parts)

    def rows_of(ref, lo, rows):
        if len(ref.shape) == 2:
            return ref[lo:lo + rows, :]
        acc = ref[0, lo:lo + rows, :]
        for s in range(1, ref.shape[0]):
            acc = acc + ref[s, lo:lo + rows, :]
        return acc

    def body(*refs):
        g_refs, ins, outs = refs[:n_g], refs[n_g:n_g + 3 * n], refs[n_g + 3 * n:]
        part, off = 0, 0
        for k in range(n):
            rows = ws[k].shape[0]
            if off + rows > grad_parts[part].shape[-2]:
                part, off = part + 1, 0
            g = rows_of(g_refs[part], off, rows)
            off += rows
            delta, m_new, v_new = _adamw_update(g, ins[k][...], ins[n + k][...], ins[2 * n + k][...])
            for o, val in zip(outs[4 * k:4 * k + 4], (g, delta, m_new, v_new)):
                o[...] = val

    vmem = pl.BlockSpec(memory_space=pltpu.VMEM)
    outs = pl.pallas_call(
        body, name="adamw_small",
        in_specs=[vmem] * (n_g + 3 * n), out_specs=[vmem] * (4 * n),
        out_shape=[jax.ShapeDtypeStruct(w.shape, F32) for w in ws for _ in range(4)],
    )(*grad_parts, *ws, *ms, *vs)
    return [outs[4 * k:4 * k + 4] for k in range(n)]


def _final_sums(chip, owns, slots, name):
    n = len(owns)

    def body(chip_ref, *refs):
        for w in range(n):
            own_ref, s1_ref, s2_ref, s3_ref = refs[4 * w:4 * w + 4]
            acc = own_ref[0].astype(F32) + s1_ref[0].astype(F32)
            acc = acc + s2_ref[0].astype(F32)
            refs[4 * n + w][...] = acc + s3_ref[0].astype(F32)

    def slot(a, k):
        n_s, rh, cols = a.shape
        return pl.BlockSpec((1, rh // ROW_STEPS, cols), lambda r, chip: (lax.rem(chip[0] + k, n_s), r, 0))

    operands, in_specs = [], []
    for own, slots_w in zip(owns, slots):
        operands += [own, slots_w, slots_w, slots_w]
        in_specs += [slot(own, k) for k in range(4)]
    return pl.pallas_call(
        body, name=name,
        grid_spec=pltpu.PrefetchScalarGridSpec(
            num_scalar_prefetch=1, grid=(ROW_STEPS,),
            in_specs=in_specs,
            out_specs=[pl.BlockSpec((a.shape[1] // ROW_STEPS, a.shape[2]), lambda r, chip: (r, 0)) for a in owns]),
        out_shape=[jax.ShapeDtypeStruct(a.shape[1:], F32) for a in owns],
        compiler_params=_params(1),
    )(chip, *operands)


def _adamw_halves(core, owns, siblings, ws, ms, vs, name):
    n = len(ws)

    def body(c_ref, *refs):
        ins, outs = refs[:5 * n], refs[5 * n:]
        mine = c_ref[0] == pl.program_id(0)
        for k in range(n):
            own_ref, sib_ref, w_ref, m_ref, v_ref = ins[5 * k:5 * k + 5]
            g_ref, d_ref, mo_ref, vo_ref = outs[4 * k:4 * k + 4]

            @pl.when(mine)
            def _():
                g_ref[...] = own_ref[...]

            @pl.when(jnp.logical_not(mine))
            def _():
                g_ref[...] = sib_ref[...]

            d_ref[...], mo_ref[...], vo_ref[...] = _adamw_update(g_ref[...], w_ref[...], m_ref[...], v_ref[...])

    operands, in_specs, out_specs = [], [], []
    for own, sib, w, m, v in zip(owns, siblings, ws, ms, vs):
        rows, cols = w.shape
        rb = rows // 2 // ROW_STEPS
        half = pl.BlockSpec((rb, cols), lambda h, r, c: (r, 0))
        whole = pl.BlockSpec((rb, cols), lambda h, r, c: (h * ROW_STEPS + r, 0))
        operands += [own, sib, w, m, v]
        in_specs += [half, half, whole, whole, whole]
        out_specs += [whole] * 4
    outs = pl.pallas_call(
        body, name=name,
        grid_spec=pltpu.PrefetchScalarGridSpec(
            num_scalar_prefetch=1, grid=(2, ROW_STEPS), in_specs=in_specs, out_specs=out_specs),
        out_shape=[jax.ShapeDtypeStruct(w.shape, F32) for w in ws for _ in range(4)],
        compiler_params=_params(2),
    )(core, *operands)
    return [outs[4 * k:4 * k + 4] for k in range(n)]


def _place():
    x, y, c = lax.axis_index("x"), lax.axis_index("y"), lax.axis_index("c")
    chips = [(1 - x, y), (x, 1 - y), (1 - x, 1 - y)]
    return x, y, c, chips


def _remote(src, dst, send_sem, recv_sem, device):
    return pltpu.make_async_remote_copy(src_ref=src, dst_ref=dst, send_sem=send_sem, recv_sem=recv_sem,
                                        device_id=device, device_id_type=MESH)


def _split_start(name, arrays, n_copies, plan):
    n = len(arrays)

    def body(*refs):
        ins, send_sem, recv_sem, token = refs[:n], refs[n], refs[n + 1], refs[-1]
        for k, (src, dst, device) in enumerate(plan(ins)):
            _remote(src, dst, send_sem.at[k], recv_sem.at[k], device).start()
        token[...] = jnp.zeros_like(token)

    outs = pl.pallas_call(
        body, name=name,
        in_specs=[HBM] * n,
        out_specs=(SEM, SEM, *[HBM] * n, pl.BlockSpec(memory_space=pltpu.VMEM)),
        out_shape=(pltpu.SemaphoreType.DMA((n_copies,)), pltpu.SemaphoreType.DMA((n_copies,)),
                   *[pltpu.HBM(a.shape, a.dtype) for a in arrays], jax.ShapeDtypeStruct((8, LANES), F32)),
        input_output_aliases={i: i + 2 for i in range(n)},
        compiler_params=pltpu.CompilerParams(has_side_effects=EFFECT),
    )(*[pltpu.with_memory_space_constraint(a, pltpu.HBM) for a in arrays])
    return outs[0], outs[1], list(outs[2:2 + n]), outs[-1]


def _split_wait(name, send_sem, recv_sem, arrays, plan, first, *after):
    n = len(arrays)

    def body(*refs):
        ins, send_ref, recv_ref = refs[:n], refs[n], refs[n + 1]
        for k, (src, dst, device) in enumerate(plan(ins)):
            cp = _remote(src, dst, send_ref.at[first + k], recv_ref.at[first + k], device)
            cp.wait_send()
            cp.wait_recv()

    outs = pl.pallas_call(
        body, name=name,
        in_specs=[HBM] * n + [SEM, SEM] + [ANY] * len(after), out_specs=[HBM] * n,
        out_shape=[pltpu.HBM(a.shape, a.dtype) for a in arrays],
        input_output_aliases={i: i for i in range(n)},
        compiler_params=pltpu.CompilerParams(has_side_effects=EFFECT),
    )(*arrays, send_sem, recv_sem, *after)
    return list(outs)


def _plans(*parts):
    def plan(refs):
        copies, lo = [], 0
        for part, n in parts:
            copies += part(refs[lo:lo + n])
            lo += n
        return copies
    return plan


def _plan_gather_send(n):
    def plan(refs):
        x, y, c, chips = _place()
        me = 2 * x + y
        copies = []
        for w in range(n):
            own, full = refs[w], refs[n + w]
            copies.append((own, full.at[me], (x, y, 1 - c)))
            copies += [(own.at[c], full.at[me, c], (px, py, c)) for px, py in chips]
        return copies
    return plan


def _plan_gather_direct(n):
    def plan(refs):
        x, y, c, chips = _place()
        me = 2 * x + y
        copies = []
        for w in range(n):
            own, full = refs[w], refs[n + w]
            copies.append((own, full.at[me], (x, y, 1 - c)))
            for px, py in chips:
                copies += [(own.at[c], full.at[me, c], (px, py, c)), (own.at[c], full.at[me, c], (px, py, 1 - c))]
        return copies
    return plan


def _plan_gather_pass(n):
    def plan(refs):
        x, y, c, chips = _place()
        copies = []
        for w in range(n):
            for px, py in chips:
                landed = refs[w].at[2 * px + py, c]
                copies.append((landed, landed, (x, y, 1 - c)))
        return copies
    return plan


def _plan_share(n):
    def plan(refs):
        x, y, c, _ = _place()
        return [(refs[w], refs[n + w], (x, y, 1 - c)) for w in range(n)]
    return plan


def _plan_exchange(n, with_small):
    def plan(refs):
        x, y, c, chips = _place()
        me = 2 * x + y
        n_in = n + int(with_small)
        copies = []
        for w in range(n):
            copies += [(refs[w].at[2 * px + py], refs[n_in + w].at[me], (px, py, c)) for px, py in chips]
        if with_small:
            small, slots = refs[n], refs[n_in + n]
            copies.append((small, slots.at[me], (x, y, 1 - c)))
            copies += [(small, slots.at[me], (px, py, c)) for px, py in chips]
        return copies
    return plan


def _swap_halves(name, grads, wholes):
    n = len(grads)
    arrays = list(grads) + list(wholes)
    n_all = len(arrays)

    def body(*refs):
        ins, outs = refs[:n_all], refs[n_all:2 * n_all]
        send_sem, recv_sem = refs[2 * n_all:]
        x, y, c, _ = _place()
        sibling = (x, y, 1 - c)
        copies = []
        for w in range(n_all):
            src = ins[w].at[1 - c] if w < n else ins[w]
            cp = _remote(src, outs[w], send_sem.at[w], recv_sem.at[w], sibling)
            cp.start()
            copies.append(cp)
        for cp in copies:
            cp.wait()

    return pl.pallas_call(
        body, name=name,
        in_specs=[ANY] * n_all, out_specs=[ANY] * n_all,
        out_shape=[jax.ShapeDtypeStruct(g.shape[1:], g.dtype) for g in grads]
        + [jax.ShapeDtypeStruct(a.shape, a.dtype) for a in wholes],
        scratch_shapes=[pltpu.SemaphoreType.DMA((n_all,))] * 2,
    )(*arrays)


def _share_and_all_reduce(wholes, part):
    n = len(wholes)
    n_dev = 8

    def body(*refs):
        ins, part_ref, outs, sum_ref = refs[:n], refs[n], refs[n + 1:2 * n + 1], refs[2 * n + 1]
        slots_ref, send_sem, recv_sem, send_tile, recv_tile = refs[2 * n + 2:]
        x, y, c, _ = _place()
        me = 4 * x + 2 * y + c
        copies = []
        for w in range(n):
            copies.append(_remote(ins[w], outs[w], send_sem.at[w], recv_sem.at[w], (x, y, 1 - c)))
        slots_ref[me] = part_ref[...]
        flips = [(fx, fy, fc) for fx in (0, 1) for fy in (0, 1) for fc in (0, 1)][1:]
        for k, (fx, fy, fc) in enumerate(flips):
            peer = (x + fx - 2 * x * fx, y + fy - 2 * y * fy, c + fc - 2 * c * fc)
            copies.append(_remote(part_ref, slots_ref.at[me], send_tile.at[k], recv_tile.at[k], peer))
        for cp in copies:
            cp.start()
        for cp in copies:
            cp.wait()
        acc = slots_ref[0]
        for k in range(1, n_dev):
            acc = acc + slots_ref[k]
        sum_ref[...] = acc

    vmem = pl.BlockSpec(memory_space=pltpu.VMEM)
    outs = pl.pallas_call(
        body, name="rs_share_late",
        in_specs=[ANY] * n + [vmem], out_specs=[ANY] * n + [vmem],
        out_shape=[jax.ShapeDtypeStruct(a.shape, a.dtype) for a in wholes] + [jax.ShapeDtypeStruct(part.shape, F32)],
        scratch_shapes=[pltpu.VMEM((n_dev,) + part.shape, F32),
                        pltpu.SemaphoreType.DMA((n,)), pltpu.SemaphoreType.DMA((n,)),
                        pltpu.SemaphoreType.DMA((n_dev - 1,)), pltpu.SemaphoreType.DMA((n_dev - 1,))],
    )(*wholes, part)
    return outs[:n], outs[n]


_SMALL = ("norm1_pre_g", "b_in", "pool_scale", "sgu_ln_g", "sgu_ln_b", "w_spatial",
          "norm1_post_g", "norm2_pre_g", "norm2_post_g", "b_spatial")
_MIX = ("w_in", "w_pool", "w_sgu_proj", "w_out")
_FF = ("w_ff1", "w_ff2")
_BIG = _MIX + _FF
_ORDER = ("norm1_pre_g", "w_in", "b_in", "w_pool", "pool_scale", "sgu_ln_g", "sgu_ln_b", "w_spatial", "b_spatial",
          "w_sgu_proj", "w_out", "norm1_post_g", "norm2_pre_g", "w_ff1", "w_ff2", "norm2_post_g")


def _pack(keys, rows, parts):
    flat = jnp.concatenate([parts[k].reshape(-1).astype(F32) for k in keys])
    flat = jnp.pad(flat, (0, rows * LANES - flat.shape[0]))
    return flat.reshape(rows, LANES)


def _halves(a):
    return a.reshape(2, a.shape[0] // 2, a.shape[1])


def _step(p, m, v, x, target):
    t = x.shape[0]
    tm_mm = min(1024, t)
    tm_mix = min(256, t)
    tm_dx = min(512, t)
    core = lax.axis_index("c").astype(jnp.int32).reshape(1)
    row = lambda a: a.reshape(1, -1)

    chip = (2 * lax.axis_index("x") + lax.axis_index("y")).astype(jnp.int32).reshape(1)

    local2d = {"w_in": p["w_in"], "w_ff1": p["w_ff1"], "w_ff2": p["w_ff2"], "w_out": p["w_out"],
               "w_pool": p["w_pool"].reshape(N_HEADS * 64, POOL_GROUP),
               "w_sgu_proj": p["w_sgu_proj"].reshape(N_HEADS * 64, HEAD)}
    shard = {k: _halves(local2d[k].astype(BF16)) for k in _BIG}
    landing = lambda keys: [lax.empty((N_SHARD,) + shard[k].shape, BF16) for k in keys]
    in_arrays = [shard["w_in"]] + landing(["w_in"])
    send_a, recv_a, in_arrays, token = _split_start("ag_in_send", in_arrays, 4, _plan_gather_send(1))
    z, xn = _inproj_own(x, row(p["norm1_pre_g"]), in_arrays[0].reshape(D, D_IN // N_SHARD), row(p["b_in"]), chip,
                        token, tm_mm)
    mix_keys = _MIX[1:]
    n_mix, n_ff = len(mix_keys), len(_FF)
    in_arrays = _split_wait("ag_in_land", send_a, recv_a, in_arrays, _plan_gather_send(1), 0, z,
                            *[shard[k] for k in mix_keys + _FF])
    rest = in_arrays[1:] + [shard[k] for k in mix_keys] + landing(mix_keys) + [shard[k] for k in _FF] + landing(_FF)
    rest_plan = _plans((_plan_gather_pass(1), 1), (_plan_gather_direct(n_mix), 2 * n_mix),
                       (_plan_gather_send(n_ff), 2 * n_ff))
    send0, recv0, rest, token = _split_start("ag_rest_send", rest, 3 + 7 * n_mix + 4 * n_ff, rest_plan)
    in_full, mix_arrays, ff_arrays = rest[:1], rest[1:1 + 2 * n_mix], rest[1 + 2 * n_mix:]
    in_full = _split_wait("ag_in_passed", send0, recv0, in_full, _plan_gather_pass(1), 0, token)
    w_in4 = in_full[0].reshape(N_SHARD, D, D_IN // N_SHARD)

    pos = jnp.arange(SGU_BLOCK) // 64
    mask = (pos[:, None] >= pos[None, :]).astype(F32)
    ws = (p["w_spatial"] * mask[None]).astype(BF16)
    ws_t = jnp.swapaxes(ws, 1, 2)
    bsp = p["b_spatial"].reshape(N_HEADS, SGU_BLOCK, 1)

    z = _inproj_rest(xn, w_in4, row(p["b_in"]), chip, z, tm_mm)
    mix_arrays = _split_wait("ag_mix_land", send0, recv0, mix_arrays, _plan_gather_direct(n_mix), 3, z)
    full = dict(zip(mix_keys, mix_arrays[len(mix_keys):]))
    w_out = full["w_out"].reshape(D, D)
    regroup = lambda a: a.reshape(N_SHARD, N_HEADS, 64, 256).transpose(1, 0, 2, 3).reshape(N_HEADS, 256, 256)
    w_pool = regroup(full["w_pool"])
    w_proj = regroup(full["w_sgu_proj"])
    pooled, merged, y, h1 = _mix_fwd(z, x, w_pool, row(p["pool_scale"]), row(p["sgu_ln_g"]), row(p["sgu_ln_b"]),
                                     ws, bsp, w_proj, w_out, row(p["norm1_post_g"]), tm_mix)
    ff_arrays = _split_wait("ag_ff_land", send0, recv0, ff_arrays, _plan_gather_send(n_ff), 3 + 7 * n_mix, pooled)
    send2, recv2, ff_full, token = _split_start("ag_ff_pass", ff_arrays[n_ff:], 3 * n_ff, _plan_gather_pass(n_ff))
    ff_full = _split_wait("ag_ff_passed", send2, recv2, ff_full, _plan_gather_pass(n_ff), 0, token)
    w_ff1_4 = ff_full[0].reshape(N_SHARD, D, D)
    w_ff2 = ff_full[1].reshape(N_SHARD, D, D)
    hn, f1, f, df2, dout, vec_a = _ffn_fwd(h1, row(p["norm2_pre_g"]), w_ff1_4, w_ff2, row(p["norm2_post_g"]),
                                           target, tm_mix)

    tk = min(2048, t)
    g_big = {"w_ff2": _dw(f, df2, "dw_ff2", 2, D, tk, N_SHARD, lambda i, j: i)}
    df1, dh1, dy, vec_b = _ffn_bwd(df2, f1, w_ff1_4, w_ff2, h1, dout, y, row(p["norm2_pre_g"]),
                                   row(p["norm1_post_g"]), tm_mix)
    g_big["w_ff1"] = _dw(hn, df1, "dw_ff1", 2, D, tk, N_SHARD, lambda i, j: j)

    def reduce_start(name, keys, wholes, also=(), also_plan=None, also_copies=0):
        swapped = _swap_halves("rs_swap_" + name, [g_big[k] for k in keys], wholes)
        sums = _chip_sums(core, [g_big[k] for k in keys], swapped[:len(keys)], "chip_sums_" + name)
        arrays = list(sums)
        if wholes:
            arrays.append(_slot_sum(jnp.stack([wholes[0], swapped[-1]]), "chip_sum_small"))
        arrays += [lax.empty(s.shape, BF16) for s in sums]
        if wholes:
            arrays.append(lax.empty((N_SHARD,) + wholes[0].shape, F32))
        plan = _plan_exchange(len(keys), bool(wholes))
        n_copies, n = 3 * len(keys) + 4 * len(wholes), len(arrays)
        both = plan if also_plan is None else _plans((plan, n), (also_plan, len(also)))
        send, recv, out, tok = _split_start("rs_send_" + name, arrays + list(also), n_copies + also_copies, both)
        return send, recv, out[:n], plan, tok, out[n:], n_copies

    def reduce_finish(name, keys, state, *after):
        send, recv, arrays, plan = state[:4]
        arrays = _split_wait("rs_land_" + name, send, recv, arrays, plan, 0, *after)
        n_in = len(arrays) // 2
        n = len(keys)
        finished = _final_sums(chip, arrays[:n], arrays[n_in:n_in + n], "final_sums_" + name)
        return dict(zip(keys, finished)), arrays

    by_half = lambda a, rows: a.reshape(N_SHARD, 2, rows // 2, a.shape[-1]).transpose(1, 0, 2, 3)
    g_big["w_out"] = _dw(merged, dy, "dw_out", 2 * N_SHARD, D, tk, N_SHARD, lambda i, j: 0)
    early = _FF + ("w_out",)
    late = ("w_in", "w_pool", "w_sgu_proj")
    early_state = reduce_start("early", early, [])
    dz, dwp, dwpr, dws, dbsp, vec_c, dbin = _mix_bwd(z, pooled, dy, w_pool, row(p["pool_scale"]), row(p["sgu_ln_g"]),
                                                     row(p["sgu_ln_b"]), ws, ws_t, bsp, w_proj, w_out, mask, tm_mix,
                                                     early_state[4])
    g_big["w_in"] = _dw(xn, dz, "dw_in", 2, D_IN // N_SHARD, tk, N_SHARD, lambda i, j: j)
    shard_major = lambda a: a.reshape(N_HEADS, N_SHARD, 64, 256).transpose(1, 0, 2, 3).reshape(N_SHARD * 256, 256)
    g_big["w_pool"] = by_half(shard_major(dwp).astype(BF16), 256)
    g_big["w_sgu_proj"] = by_half(shard_major(dwpr).astype(BF16), 256)
    halves, _ = reduce_finish("early", early, early_state, g_big["w_in"])

    small_part = _pack(_SMALL[1:], SMALL_ROWS - 8, {
        "b_in": dbin[0], "pool_scale": vec_c[0], "sgu_ln_g": vec_c[1], "sgu_ln_b": vec_c[2],
        "w_spatial": dws, "b_spatial": dbsp[:, ::LANES].T, "norm1_post_g": vec_b[1], "norm2_pre_g": vec_b[0],
        "norm2_post_g": vec_a[0]})
    share = [halves[k] for k in early] + [lax.empty(halves[k].shape, F32) for k in early]
    late_state = reduce_start("late", late, [small_part], share, _plan_share(len(early)), len(early))
    grad_x, vec_d = _dx(dz, w_in4, x, dh1, row(p["norm1_pre_g"]), late_state[4], tm_dx)
    share = _split_wait("rs_share_early_land", late_state[0], late_state[1], late_state[5], _plan_share(len(early)),
                        late_state[6], grad_x)

    grads, delta, new_m, new_v = {}, {}, {}, {}

    def update(name, keys, own, shared):
        as2d = lambda a, k: a.reshape(local2d[k].shape)
        outs = _adamw_halves(core, own, shared, [local2d[k] for k in keys],
                             [as2d(m[k], k) for k in keys], [as2d(v[k], k) for k in keys], "adamw_" + name)
        for k, outs_k in zip(keys, outs):
            grads[k], delta[k], new_m[k], new_v[k] = (a.reshape(p[k].shape) for a in outs_k)

    update("early", early, share[:len(early)], share[len(early):])
    late_halves, late_arrays = reduce_finish("late", late, late_state, grad_x, *[new_v[k] for k in early])
    own_late = [late_halves[k] for k in late]
    shared_late, tiles = _share_and_all_reduce(
        own_late, jnp.concatenate([vec_d[0].reshape(8, LANES), jnp.broadcast_to(vec_a[1:2, :LANES], (8, LANES))]))
    g1_grad, loss = tiles[:8], tiles[8, 0]
    update("late", late, own_late, shared_late)
    as_rows = lambda d: [d[k].reshape(-1, LANES) for k in _SMALL]
    small_outs = _adamw_small([g1_grad, late_arrays[-1]], as_rows(p), as_rows(m), as_rows(v))
    for k, outs_k in zip(_SMALL, small_outs):
        grads[k], delta[k], new_m[k], new_v[k] = (a.reshape(p[k].shape) for a in outs_k)

    return (loss, grad_x, *[grads[k] for k in _ORDER], *[delta[k] for k in _ORDER],
            *[new_m[k] for k in _ORDER], *[new_v[k] for k in _ORDER])


def kernel(x, norm1_pre_g, w_in, b_in, w_pool, pool_scale, sgu_ln_g, sgu_ln_b, w_spatial, b_spatial, w_sgu_proj, w_out, norm1_post_g, norm2_pre_g, w_ff1, w_ff2, norm2_post_g, loss_target, m_norm1_pre_g, m_w_in, m_b_in, m_w_pool, m_pool_scale, m_sgu_ln_g, m_sgu_ln_b, m_w_spatial, m_b_spatial, m_w_sgu_proj, m_w_out, m_norm1_post_g, m_norm2_pre_g, m_w_ff1, m_w_ff2, m_norm2_post_g, v_norm1_pre_g, v_w_in, v_b_in, v_w_pool, v_pool_scale, v_sgu_ln_g, v_sgu_ln_b, v_w_spatial, v_b_spatial, v_w_sgu_proj, v_w_out, v_norm1_post_g, v_norm2_pre_g, v_w_ff1, v_w_ff2, v_norm2_post_g):
    p = dict(zip(_ORDER, (norm1_pre_g, w_in, b_in, w_pool, pool_scale, sgu_ln_g, sgu_ln_b, w_spatial, b_spatial,
                          w_sgu_proj, w_out, norm1_post_g, norm2_pre_g, w_ff1, w_ff2, norm2_post_g)))
    m = dict(zip(_ORDER, (m_norm1_pre_g, m_w_in, m_b_in, m_w_pool, m_pool_scale, m_sgu_ln_g, m_sgu_ln_b, m_w_spatial,
                          m_b_spatial, m_w_sgu_proj, m_w_out, m_norm1_post_g, m_norm2_pre_g, m_w_ff1, m_w_ff2,
                          m_norm2_post_g)))
    v = dict(zip(_ORDER, (v_norm1_pre_g, v_w_in, v_b_in, v_w_pool, v_pool_scale, v_sgu_ln_g, v_sgu_ln_b, v_w_spatial,
                          v_b_spatial, v_w_sgu_proj, v_w_out, v_norm1_post_g, v_norm2_pre_g, v_w_ff1, v_w_ff2,
                          v_norm2_post_g)))
    outs = _step(p, m, v, x[0], loss_target[0])
    return (outs[0], outs[1][None], *outs[2:])
```

```python
import math

import jax
import jax.numpy as jnp
from jax import lax
from jax.experimental import pallas as pl
from jax.experimental.pallas import tpu as pltpu

F32 = jnp.float32
BF16 = jnp.bfloat16

D = 1024
D_IN = 5 * D
D_FF = 4 * D
N_SHARD = 4
POOL_WINDOWS = (2, 4, 8, 16)
POOL_GROUP = 256
POOL_HALO = 16
SGU_BLOCK = 128
N_HEADS = 4
HEAD = 256
EPS = 1e-6

ADAM_LR = 0.001
ADAM_B1 = 0.9
ADAM_B2 = 0.999
ADAM_EPS = 1e-08
ADAM_WD = 0.01
ADAM_STEP = 10

V7X_VMEM_BYTES = 64 * 1024 * 1024
VMEM_LIMIT = V7X_VMEM_BYTES - 8 * 1024 * 1024
SMALL_ROWS = 616
LANES = 128

MESH = pl.DeviceIdType.MESH
ANY = pl.BlockSpec(memory_space=pl.ANY)
HBM = pl.BlockSpec(memory_space=pltpu.HBM)
SEM = pl.BlockSpec(memory_space=pltpu.SEMAPHORE)
EFFECT = pltpu.SideEffectType.DATAFLOW_SIDE_EFFECTING

NT_DIMS = (((1,), (1,)), ((), ()))
TN_DIMS = (((0,), (0,)), ((), ()))


def _params(n_axes):
    return pltpu.CompilerParams(dimension_semantics=("arbitrary",) * n_axes, vmem_limit_bytes=VMEM_LIMIT)


def _resident(shape, n_axes):
    zeros = (0,) * len(shape)
    index = (lambda i: zeros) if n_axes == 1 else (lambda i, j: zeros)
    return pl.BlockSpec(shape, index, pipeline_mode=pl.Buffered(1))


def _dot(a, b):
    return jnp.dot(a, b, preferred_element_type=F32)


def _dot_nt(a, b):
    return lax.dot_general(a, b, NT_DIMS, preferred_element_type=F32)


def _dot_tn(a, b):
    return lax.dot_general(a, b, TN_DIMS, preferred_element_type=F32)


def _mean(a):
    return jnp.mean(a, axis=-1, keepdims=True)


def _colsum(a):
    return jnp.sum(a, axis=0, keepdims=True)


_GELU_C0 = math.sqrt(2.0 / math.pi)
_GELU_C1 = 0.044715


def _gelu(a):
    t = jnp.tanh(a * (_GELU_C0 + (_GELU_C0 * _GELU_C1) * (a * a)))
    return a * (0.5 * t + 0.5)


def _gelu_and_grad(a):
    a2 = a * a
    t = jnp.tanh(a * (_GELU_C0 + (_GELU_C0 * _GELU_C1) * a2))
    cdf = 0.5 * t + 0.5
    grad = cdf + (0.5 * a) * (1.0 - t * t) * (_GELU_C0 + (3.0 * _GELU_C0 * _GELU_C1) * a2)
    return a * cdf, grad


def _sigmoid(a):
    return 0.5 * jnp.tanh(0.5 * a) + 0.5


def _rms_norm_bf16(a, g):
    return (a * lax.rsqrt(_mean(a * a) + EPS) * g).astype(BF16)


def _relu_sq_bf16(a):
    r = jnp.maximum(a, 0.0)
    return (r * r).astype(BF16)


def _window_sum(a, w, ahead):
    n = a.shape[0]
    step = 1
    while step < w:
        a = a + pltpu.roll(a, n - step if ahead else step, 0)
        step *= 2
    return a


def _rms_bwd(dn, n, r):
    return r * (dn - n * _mean(dn * n))


def _inproj_own(x, g1, w_own, b_in, chip, dep, tm):
    t = x.shape[0]
    cw = D_IN // N_SHARD

    def body(chip_ref, x_ref, g_ref, w_ref, b_ref, dep_ref, z_ref, xn_ref):
        xn = _rms_norm_bf16(x_ref[...], g_ref[...])
        xn_ref[...] = xn
        z_ref[...] = _dot(xn, w_ref[...]) + b_ref[...]

    return pl.pallas_call(
        body, name="inproj_own",
        grid_spec=pltpu.PrefetchScalarGridSpec(
            num_scalar_prefetch=1, grid=(t // tm,),
            in_specs=[pl.BlockSpec((tm, D), lambda i, chip: (i, 0)),
                      pl.BlockSpec((1, D), lambda i, chip: (0, 0)),
                      pl.BlockSpec((D, cw), lambda i, chip: (0, 0)),
                      pl.BlockSpec((1, cw), lambda i, chip: (0, chip[0])),
                      pl.BlockSpec(dep.shape, lambda i, chip: (0, 0))],
            out_specs=[pl.BlockSpec((tm, cw), lambda i, chip: (i, chip[0])),
                       pl.BlockSpec((tm, D), lambda i, chip: (i, 0))]),
        out_shape=[jax.ShapeDtypeStruct((t, D_IN), F32), jax.ShapeDtypeStruct((t, D), BF16)],
        compiler_params=_params(1),
    )(chip, x, g1, w_own, b_in, dep)


def _inproj_rest(xn, w_in4, b_in, chip, z, tm):
    t = xn.shape[0]
    cw = D_IN // N_SHARD
    shard_of = lambda s, chip: lax.rem(chip[0] + 1 + s, N_SHARD)

    def body(chip_ref, xn_ref, w_ref, b_ref, z_in_ref, z_ref):
        z_ref[...] = _dot(xn_ref[...], w_ref[shard_of(pl.program_id(1), chip_ref)]) + b_ref[...]

    return pl.pallas_call(
        body, name="inproj_rest",
        grid_spec=pltpu.PrefetchScalarGridSpec(
            num_scalar_prefetch=1, grid=(t // tm, N_SHARD - 1),
            in_specs=[pl.BlockSpec((tm, D), lambda i, s, chip: (i, 0)),
                      pl.BlockSpec((N_SHARD, D, cw), lambda i, s, chip: (0, 0, 0), pipeline_mode=pl.Buffered(1)),
                      pl.BlockSpec((1, cw), lambda i, s, chip: (0, shard_of(s, chip))),
                      ANY],
            out_specs=pl.BlockSpec((tm, cw), lambda i, s, chip: (i, shard_of(s, chip)))),
        out_shape=jax.ShapeDtypeStruct(z.shape, F32),
        input_output_aliases={4: 0},
        compiler_params=_params(2),
    )(chip, xn, w_in4, b_in, z)


def _sgu_forward(z_ref, lng_ref, lnb_ref, ws_ref, bsp_ref, sv_ref, tm, with_grad):
    if with_grad:
        u, du_dz = _gelu_and_grad(z_ref[:, D:2 * D])
        gv, dgv_dz = _gelu_and_grad(z_ref[:, 2 * D:3 * D])
    else:
        u = _gelu(z_ref[:, D:2 * D])
        gv = _gelu(z_ref[:, 2 * D:3 * D])
        du_dz = dgv_dz = None
    xc = gv - _mean(gv)
    rstd = lax.rsqrt(_mean(xc * xc) + EPS)
    vhat = xc * rstd
    vb = (vhat * lng_ref[...] + lnb_ref[...]).astype(BF16)
    for b in range(tm // SGU_BLOCK):
        rows = slice(b * SGU_BLOCK, (b + 1) * SGU_BLOCK)
        for h in range(N_HEADS):
            cols = slice(h * HEAD, (h + 1) * HEAD)
            sv_ref[rows, cols] = _dot(ws_ref[h], vb[rows, cols]) + bsp_ref[h]
    return u, du_dz, dgv_dz, vhat, rstd, vb


def _mix_fwd(z, x, w_pool, pool_scale, ln_g, ln_b, ws, bsp, w_proj, w_out, g1post, tm):
    t = x.shape[0]

    def body(z_ref, x_ref, wp_ref, psc_ref, lng_ref, lnb_ref, ws_ref, bsp_ref, wpr_ref, wo_ref, g1p_ref,
             pooled_ref, merged_ref, y_ref, h1_ref, win_ref, sv_ref):
        i = pl.program_id(0)

        @pl.when(i == 0)
        def _():
            win_ref[0:POOL_HALO, :] = jnp.zeros((POOL_HALO, D), F32)

        win_ref[POOL_HALO:POOL_HALO + tm, :] = z_ref[:, 0:D]
        pos1 = lax.broadcasted_iota(jnp.int32, (tm, 1), 0) + (i * tm + 1)
        a_parts = []
        for g, w in enumerate(POOL_WINDOWS):
            cols = slice(g * POOL_GROUP, (g + 1) * POOL_GROUP)
            ts = _window_sum(win_ref[:, cols], w, False)[POOL_HALO:POOL_HALO + tm, :]
            inv_cnt = 1.0 / jnp.minimum(pos1, w).astype(F32)
            pooled = (ts * inv_cnt - z_ref[:, cols]).astype(BF16)
            pooled_ref[:, cols] = pooled
            a_parts.append(_dot(pooled, wp_ref[g]) * psc_ref[:, cols])
        win_ref[0:POOL_HALO, :] = win_ref[tm:tm + POOL_HALO, :]

        u, _, _, _, _, _ = _sgu_forward(z_ref, lng_ref, lnb_ref, ws_ref, bsp_ref, sv_ref, tm, False)
        gated = (u * sv_ref[...]).astype(BF16)
        for h in range(N_HEADS):
            lo = h * HEAD
            bbr = _dot(gated[:, lo:lo + HEAD], wpr_ref[h])
            gate_a = _sigmoid(z_ref[:, 3 * D + lo:3 * D + lo + HEAD])
            gate_b = _sigmoid(z_ref[:, 4 * D + lo:4 * D + lo + HEAD])
            merged_ref[:, lo:lo + HEAD] = (gate_a * a_parts[h] + gate_b * bbr).astype(BF16)
        merged = merged_ref[...]
        y = _dot(merged, wo_ref[...])
        y_ref[...] = y
        r = lax.rsqrt(_mean(y * y) + EPS)
        h1_ref[...] = x_ref[...] + (y * r) * g1p_ref[...]

    row = lambda i: (i, 0)
    fixed2 = lambda i: (0, 0)
    fixed3 = lambda i: (0, 0, 0)
    vec = pl.BlockSpec((1, D), fixed2)
    return pl.pallas_call(
        body, name="mix_fwd",
        grid=(t // tm,),
        in_specs=[pl.BlockSpec((tm, D_IN), row), pl.BlockSpec((tm, D), row),
                  pl.BlockSpec((N_HEADS, POOL_GROUP, POOL_GROUP), fixed3), vec, vec, vec,
                  pl.BlockSpec((N_HEADS, SGU_BLOCK, SGU_BLOCK), fixed3),
                  pl.BlockSpec((N_HEADS, SGU_BLOCK, 1), fixed3),
                  pl.BlockSpec((N_HEADS, HEAD, HEAD), fixed3),
                  pl.BlockSpec((D, D), fixed2), vec],
        out_specs=[pl.BlockSpec((tm, D), row)] * 4,
        out_shape=[jax.ShapeDtypeStruct((t, D), BF16), jax.ShapeDtypeStruct((t, D), BF16),
                   jax.ShapeDtypeStruct((t, D), F32), jax.ShapeDtypeStruct((t, D), F32)],
        scratch_shapes=[pltpu.VMEM((tm + POOL_HALO, D), F32), pltpu.VMEM((tm, D), F32)],
        compiler_params=_params(1),
    )(z, x, w_pool, pool_scale, ln_g, ln_b, ws, bsp, w_proj, w_out, g1post)


def _ffn_fwd(h1, g2pre, w1, w2, g2post, target, tm):
    t = h1.shape[0]
    n_j = D_FF // D

    def body(h1_ref, g2_ref, w1_ref, w2_ref, g2p_ref, tgt_ref,
             hn_ref, f1_ref, f_ref, df2_ref, dout_ref, vec_ref):
        @pl.when(pl.program_id(0) == 0)
        def _():
            vec_ref[...] = jnp.zeros_like(vec_ref)

        hn = _rms_norm_bf16(h1_ref[...], g2_ref[...])
        hn_ref[...] = hn
        f2 = None
        for j in range(n_j):
            cols = slice(j * D, (j + 1) * D)
            f1 = _dot(hn, w1_ref[j])
            f1_ref[:, cols] = f1
            f = _relu_sq_bf16(f1)
            f_ref[:, cols] = f
            part = _dot(f, w2_ref[j])
            f2 = part if f2 is None else f2 + part
        r = lax.rsqrt(_mean(f2 * f2) + EPS)
        n2 = f2 * r
        err = h1_ref[...] + n2 * g2p_ref[...] - tgt_ref[...]
        loss = 0.5 * jnp.sum(_mean(err * err))
        dout = err * (1.0 / D)
        dout_ref[...] = dout
        vec_ref[0:1, :] += _colsum(dout * n2)
        vec_ref[1:2, :] += jnp.full((1, D), loss, F32)
        df2_ref[...] = _rms_bwd(dout * g2p_ref[...], n2, r).astype(BF16)

    row = lambda i: (i, 0)
    fixed = lambda i: (0, 0)
    vec = pl.BlockSpec((1, D), fixed)
    return pl.pallas_call(
        body, name="ffn_fwd",
        grid=(t // tm,),
        in_specs=[pl.BlockSpec((tm, D), row), vec,
                  _resident((n_j, D, D), 1), _resident((n_j, D, D), 1), vec,
                  pl.BlockSpec((tm, D), row)],
        out_specs=[pl.BlockSpec((tm, D), row),
                   pl.BlockSpec((tm, D_FF), row), pl.BlockSpec((tm, D_FF), row),
                   pl.BlockSpec((tm, D), row), pl.BlockSpec((tm, D), row),
                   pl.BlockSpec((8, D), fixed)],
        out_shape=[jax.ShapeDtypeStruct((t, D), BF16), jax.ShapeDtypeStruct((t, D_FF), F32),
                   jax.ShapeDtypeStruct((t, D_FF), BF16), jax.ShapeDtypeStruct((t, D), BF16),
                   jax.ShapeDtypeStruct((t, D), F32), jax.ShapeDtypeStruct((8, D), F32)],
        compiler_params=_params(1),
    )(h1, g2pre, w1, w2, g2post, target)


def _ffn_bwd(df2, f1, w1, w2, h1, dout, y, g2pre, g1post, tm):
    t = h1.shape[0]
    n_j = D_FF // D

    def body(df2_ref, f1_ref, w1_ref, w2_ref, h1_ref, dout_ref, y_ref, g2_ref, g1p_ref,
             df1_ref, dh1_ref, dy_ref, vec_ref):
        @pl.when(pl.program_id(0) == 0)
        def _():
            vec_ref[...] = jnp.zeros_like(vec_ref)

        df2 = df2_ref[...]
        dhn = None
        for j in range(n_j):
            cols = slice(j * D, (j + 1) * D)
            df = _dot_nt(df2, w2_ref[j])
            df1 = (df * (2.0 * jnp.maximum(f1_ref[:, cols], 0.0))).astype(BF16)
            df1_ref[:, cols] = df1
            part = _dot_nt(df1, w1_ref[j])
            dhn = part if dhn is None else dhn + part
        h = h1_ref[...]
        r2 = lax.rsqrt(_mean(h * h) + EPS)
        nh = h * r2
        vec_ref[0:1, :] += _colsum(dhn * nh)
        dh1 = dout_ref[...] + _rms_bwd(dhn * g2_ref[...], nh, r2)
        dh1_ref[...] = dh1
        yv = y_ref[...]
        r1 = lax.rsqrt(_mean(yv * yv) + EPS)
        ny = yv * r1
        vec_ref[1:2, :] += _colsum(dh1 * ny)
        dy_ref[...] = _rms_bwd(dh1 * g1p_ref[...], ny, r1).astype(BF16)

    row = lambda i: (i, 0)
    fixed = lambda i: (0, 0)
    vec = pl.BlockSpec((1, D), fixed)
    return pl.pallas_call(
        body, name="ffn_bwd",
        grid=(t // tm,),
        in_specs=[pl.BlockSpec((tm, D), row), pl.BlockSpec((tm, D_FF), row),
                  _resident((n_j, D, D), 1), _resident((n_j, D, D), 1),
                  pl.BlockSpec((tm, D), row), pl.BlockSpec((tm, D), row), pl.BlockSpec((tm, D), row), vec, vec],
        out_specs=[pl.BlockSpec((tm, D_FF), row),
                   pl.BlockSpec((tm, D), row), pl.BlockSpec((tm, D), row),
                   pl.BlockSpec((8, D), fixed)],
        out_shape=[jax.ShapeDtypeStruct((t, D_FF), BF16), jax.ShapeDtypeStruct((t, D), F32),
                   jax.ShapeDtypeStruct((t, D), BF16), jax.ShapeDtypeStruct((8, D), F32)],
        compiler_params=_params(1),
    )(df2, f1, w1, w2, h1, dout, y, g2pre, g1post)


def _mix_bwd(z, pooled, dy, w_pool, pool_scale, ln_g, ln_b, ws, ws_t, bsp, w_proj, w_out, mask, tm, dep):
    t = z.shape[0]
    n_t = t // tm

    def body(z_ref, pooled_ref, dy_ref, wp_ref, psc_ref, lng_ref, lnb_ref, ws_ref, wst_ref, bsp_ref, wpr_ref,
             wo_ref, mask_ref, dep_ref,
             dz_ref, dwp_ref, dwpr_ref, dws_ref, dbsp_ref, vec_ref, dbin_ref,
             win_ref, sv_ref, dsv_ref, dv_ref, dsvsum_ref):
        i = pl.program_id(0)

        @pl.when(i == 0)
        def _():
            win_ref[tm:tm + POOL_HALO, :] = jnp.zeros((POOL_HALO, D), F32)
            dwp_ref[...] = jnp.zeros_like(dwp_ref)
            dwpr_ref[...] = jnp.zeros_like(dwpr_ref)
            dws_ref[...] = jnp.zeros_like(dws_ref)
            vec_ref[...] = jnp.zeros_like(vec_ref)
            dbin_ref[...] = jnp.zeros_like(dbin_ref)
            dsvsum_ref[...] = jnp.zeros_like(dsvsum_ref)

        dmerged = _dot_nt(dy_ref[...], wo_ref[...])

        u, du_dz, dgv_dz, vhat, rstd, vb = _sgu_forward(z_ref, lng_ref, lnb_ref, ws_ref, bsp_ref, sv_ref, tm, True)
        sv = sv_ref[...]
        gated = (u * sv).astype(BF16)

        sa = _sigmoid(z_ref[:, 3 * D:4 * D])
        sb = _sigmoid(z_ref[:, 4 * D:5 * D])
        da = dmerged * sa
        dbbr = (dmerged * sb).astype(BF16)

        tile = n_t - 1 - i
        pos1 = lax.broadcasted_iota(jnp.int32, (tm, 1), 0) + (tile * tm + 1)
        def put(k, lo, part):
            cols = slice(k * D + lo, k * D + lo + part.shape[1])
            dz_ref[:, cols] = part.astype(BF16)
            dbin_ref[0:1, cols] += _colsum(part)

        dpooled_parts = []
        for g, w in enumerate(POOL_WINDOWS):
            cols = slice(g * POOL_GROUP, (g + 1) * POOL_GROUP)
            pg = pooled_ref[:, cols]
            a_pre = _dot(pg, wp_ref[g])
            da_g = da[:, cols]
            psc = psc_ref[:, cols]
            vec_ref[0:1, cols] += _colsum(da_g * a_pre)
            sa_g = sa[:, cols]
            put(3, g * POOL_GROUP, dmerged[:, cols] * (a_pre * psc) * (sa_g * (1.0 - sa_g)))
            da_pre = (da_g * psc).astype(BF16)
            dwp_ref[g] += _dot_tn(pg, da_pre)
            dpooled = _dot_nt(da_pre, wp_ref[g])
            dpooled_parts.append(dpooled)
            inv_cnt = 1.0 / jnp.minimum(pos1, w).astype(F32)
            win_ref[0:tm, cols] = dpooled * inv_cnt
        for g, w in enumerate(POOL_WINDOWS):
            cols = slice(g * POOL_GROUP, (g + 1) * POOL_GROUP)
            acc = _window_sum(win_ref[:, cols], w, True)[0:tm, :]
            put(0, g * POOL_GROUP, acc - dpooled_parts[g])
        win_ref[tm:tm + POOL_HALO, :] = win_ref[0:POOL_HALO, :]

        for h in range(N_HEADS):
            cols = slice(h * HEAD, (h + 1) * HEAD)
            g_h = gated[:, cols]
            db_h = dbbr[:, cols]
            bbr_h = _dot(g_h, wpr_ref[h])
            sb_h = sb[:, cols]
            put(4, h * HEAD, dmerged[:, cols] * bbr_h * (sb_h * (1.0 - sb_h)))
            dwpr_ref[h] += _dot_tn(g_h, db_h)
            dgated_h = _dot_nt(db_h, wpr_ref[h])
            dsv_ref[:, cols] = dgated_h * u[:, cols]
            sv_ref[:, cols] = dgated_h * sv[:, cols]
        put(1, 0, sv_ref[...] * du_dz)
        dsv = dsv_ref[...]
        dsvb = dsv.astype(BF16)
        blk_sum = dsv[0:SGU_BLOCK, :]
        for b in range(1, tm // SGU_BLOCK):
            blk_sum = blk_sum + dsv[b * SGU_BLOCK:(b + 1) * SGU_BLOCK, :]
        dsvsum_ref[...] += blk_sum
        for b in range(tm // SGU_BLOCK):
            rows = slice(b * SGU_BLOCK, (b + 1) * SGU_BLOCK)
            for h in range(N_HEADS):
                cols = slice(h * HEAD, (h + 1) * HEAD)
                dws_ref[h] += _dot_nt(dsvb[rows, cols], vb[rows, cols])
                dv_ref[rows, cols] = _dot(wst_ref[h], dsvb[rows, cols])
        dv = dv_ref[...]
        vec_ref[1:2, :] += _colsum(dv * vhat)
        vec_ref[2:3, :] += _colsum(dv)
        dvhat = dv * lng_ref[...]
        dgv = rstd * (dvhat - _mean(dvhat) - vhat * _mean(dvhat * vhat))
        put(2, 0, dgv * dgv_dz)

        @pl.when(i == n_t - 1)
        def _():
            for h in range(N_HEADS):
                dws_ref[h] = jnp.where(mask_ref[...] > 0.0, dws_ref[h], 0.0)
                tot = jnp.sum(dsvsum_ref[:, h * HEAD:(h + 1) * HEAD], axis=1, keepdims=True)
                dbsp_ref[:, h * LANES:(h + 1) * LANES] = jnp.broadcast_to(tot, (SGU_BLOCK, LANES))

    row = lambda i: (n_t - 1 - i, 0)
    fixed2 = lambda i: (0, 0)
    fixed3 = lambda i: (0, 0, 0)
    vec = pl.BlockSpec((1, D), fixed2)
    sq = pl.BlockSpec((N_HEADS, SGU_BLOCK, SGU_BLOCK), fixed3)
    grp = pl.BlockSpec((N_HEADS, HEAD, HEAD), fixed3)
    tile = pl.BlockSpec((tm, D), row)
    return pl.pallas_call(
        body, name="mix_bwd",
        grid=(n_t,),
        in_specs=[pl.BlockSpec((tm, D_IN), row), tile, tile,
                  grp, vec, vec, vec, sq, sq, pl.BlockSpec((N_HEADS, SGU_BLOCK, 1), fixed3), grp,
                  pl.BlockSpec((D, D), fixed2), pl.BlockSpec((SGU_BLOCK, SGU_BLOCK), fixed2),
                  pl.BlockSpec(dep.shape, fixed2)],
        out_specs=[pl.BlockSpec((tm, D_IN), row), grp, grp, sq,
                   pl.BlockSpec((SGU_BLOCK, N_HEADS * LANES), fixed2),
                   pl.BlockSpec((8, D), fixed2), pl.BlockSpec((8, D_IN), fixed2)],
        out_shape=[jax.ShapeDtypeStruct((t, D_IN), BF16),
                   jax.ShapeDtypeStruct((N_HEADS, HEAD, HEAD), F32), jax.ShapeDtypeStruct((N_HEADS, HEAD, HEAD), F32),
                   jax.ShapeDtypeStruct((N_HEADS, SGU_BLOCK, SGU_BLOCK), F32),
                   jax.ShapeDtypeStruct((SGU_BLOCK, N_HEADS * LANES), F32),
                   jax.ShapeDtypeStruct((8, D), F32), jax.ShapeDtypeStruct((8, D_IN), F32)],
        scratch_shapes=[pltpu.VMEM((tm + POOL_HALO, D), F32), pltpu.VMEM((tm, D), F32), pltpu.VMEM((tm, D), F32),
                        pltpu.VMEM((tm, D), F32), pltpu.VMEM((SGU_BLOCK, D), F32)],
        compiler_params=_params(1),
    )(z, pooled, dy, w_pool, pool_scale, ln_g, ln_b, ws, ws_t, bsp, w_proj, w_out, mask, dep)


def _dx(dz, w_in4, x, dh1, g1, dep, tm):
    t = x.shape[0]
    cw = D_IN // N_SHARD

    def body(dz_ref, w_ref, x_ref, dh1_ref, g_ref, dep_ref, gx_ref, vec_ref):
        @pl.when(pl.program_id(0) == 0)
        def _():
            vec_ref[...] = jnp.zeros_like(vec_ref)

        dxn = _dot_nt(dz_ref[:, 0:cw], w_ref[0])
        for s in range(1, N_SHARD):
            dxn = dxn + _dot_nt(dz_ref[:, s * cw:(s + 1) * cw], w_ref[s])
        xv = x_ref[...]
        r = lax.rsqrt(_mean(xv * xv) + EPS)
        xh = xv * r
        vec_ref[0:1, :] += _colsum(dxn * xh)
        gx_ref[...] = dh1_ref[...] + _rms_bwd(dxn * g_ref[...], xh, r)

    row = lambda i: (i, 0)
    fixed = lambda i: (0, 0)
    return pl.pallas_call(
        body, name="dx",
        grid=(t // tm,),
        in_specs=[pl.BlockSpec((tm, D_IN), row),
                  _resident((N_SHARD, D, cw), 1),
                  pl.BlockSpec((tm, D), row), pl.BlockSpec((tm, D), row), pl.BlockSpec((1, D), fixed),
                  pl.BlockSpec(dep.shape, fixed)],
        out_specs=[pl.BlockSpec((tm, D), row), pl.BlockSpec((8, D), fixed)],
        out_shape=[jax.ShapeDtypeStruct((t, D), F32), jax.ShapeDtypeStruct((8, D), F32)],
        compiler_params=_params(1),
    )(dz, w_in4, x, dh1, g1, dep)


def _dw(a, b, name, n_split, tn, tk, n_out, out_block, dep=None):
    t, m = a.shape
    n = b.shape[1]
    tm = min(m, D)
    rows = tm // n_split
    n_k = t // tk
    deps = [] if dep is None else [dep]

    def body(a_ref, b_ref, *rest):
        o_ref, acc_ref = rest[-2:]
        k = pl.program_id(2)

        @pl.when(k == 0)
        def _():
            acc_ref[...] = jnp.zeros_like(acc_ref)

        acc_ref[...] += _dot_tn(a_ref[...], b_ref[...])

        @pl.when(k == n_k - 1)
        def _():
            for q in range(n_split):
                o_ref[q % 2, q // 2] = acc_ref[q * rows:(q + 1) * rows, :].astype(BF16)

    return pl.pallas_call(
        body, name=name,
        grid=(m // tm, n // tn, n_k),
        in_specs=[pl.BlockSpec((tk, tm), lambda i, j, k: (k, i)),
                  pl.BlockSpec((tk, tn), lambda i, j, k: (k, j))]
        + [pl.BlockSpec(d.shape, lambda i, j, k: (0, 0)) for d in deps],
        out_specs=pl.BlockSpec((2, n_split // 2, rows, tn), lambda i, j, k: (0, out_block(i, j), 0, 0)),
        out_shape=jax.ShapeDtypeStruct((2, n_out, rows, tn), BF16),
        scratch_shapes=[pltpu.VMEM((tm, tn), F32)],
        compiler_params=_params(3),
    )(a, b, *deps)


def _row_block(rows, cols, itemsize, n_bufs):
    budget = VMEM_LIMIT // 4
    rb = rows
    while rb % 16 == 0 and rb * cols * itemsize * n_bufs * 2 > budget:
        rb //= 2
    return rb


ROW_STEPS = 2


def _chip_sums(core, grads, recvs, name):
    n = len(grads)

    def body(c_ref, *refs):
        for w in range(n):
            refs[2 * n + w][...] = (refs[w][0].astype(F32) + refs[n + w][...].astype(F32)).astype(BF16)

    def blocks(g):
        _, _, rh, cols = g.shape
        rb = rh // ROW_STEPS
        return (pl.BlockSpec((1, 1, rb, cols), lambda s, r, c: (c[0], s, r, 0)),
                pl.BlockSpec((1, rb, cols), lambda s, r, c: (s, r, 0)))

    specs = [blocks(g) for g in grads]
    return pl.pallas_call(
        body, name=name,
        grid_spec=pltpu.PrefetchScalarGridSpec(
            num_scalar_prefetch=1, grid=(N_SHARD, ROW_STEPS),
            in_specs=[s[0] for s in specs] + [s[1] for s in specs],
            out_specs=[s[1] for s in specs]),
        out_shape=[jax.ShapeDtypeStruct(g.shape[1:], BF16) for g in grads],
        compiler_params=_params(2),
    )(core, *grads, *recvs)


def _slot_sum(slots, name):
    n_s, rows, cols = slots.shape
    rb = _row_block(rows, cols, 4, n_s + 1)

    def body(s_ref, o_ref):
        acc = s_ref[0].astype(F32)
        for k in range(1, n_s):
            acc = acc + s_ref[k].astype(F32)
        o_ref[...] = acc

    return pl.pallas_call(
        body, name=name,
        grid=(rows // rb,),
        in_specs=[pl.BlockSpec((n_s, rb, cols), lambda r: (0, r, 0))],
        out_specs=pl.BlockSpec((rb, cols), lambda r: (r, 0)),
        out_shape=jax.ShapeDtypeStruct((rows, cols), F32),
        compiler_params=_params(1),
    )(slots)


def _adamw_update(g, w, m, v):
    m_new = ADAM_B1 * m + (1.0 - ADAM_B1) * g
    v_new = ADAM_B2 * v + (1.0 - ADAM_B2) * (g * g)
    m_hat = m_new / (1.0 - ADAM_B1 ** ADAM_STEP)
    v_hat = v_new / (1.0 - ADAM_B2 ** ADAM_STEP)
    return -ADAM_LR * (m_hat / (jnp.sqrt(v_hat) + ADAM_EPS) + ADAM_WD * w), m_new, v_new


def _adamw_small(grad_parts, ws, ms, vs):
    n, n_g = len(ws), len(grad_parts)

    def rows_of(ref, lo, rows):
        if len(ref.shape) == 2:
            return ref[lo:lo + rows, :]
        acc = ref[0, lo:lo + rows, :]
        for s in range(1, ref.shape[0]):
            acc = acc + ref[s, lo:lo + rows, :]
        return acc

    def body(*refs):
        g_refs, ins, outs = refs[:n_g], refs[n_g:n_g + 3 * n], refs[n_g + 3 * n:]
        part, off = 0, 0
        for k in range(n):
            rows = ws[k].shape[0]
            if off + rows > grad_parts[part].shape[-2]:
                part, off = part + 1, 0
            g = rows_of(g_refs[part], off, rows)
            off += rows
            delta, m_new, v_new = _adamw_update(g, ins[k][...], ins[n + k][...], ins[2 * n + k][...])
            for o, val in zip(outs[4 * k:4 * k + 4], (g, delta, m_new, v_new)):
                o[...] = val

    vmem = pl.BlockSpec(memory_space=pltpu.VMEM)
    outs = pl.pallas_call(
        body, name="adamw_small",
        in_specs=[vmem] * (n_g + 3 * n), out_specs=[vmem] * (4 * n),
        out_shape=[jax.ShapeDtypeStruct(w.shape, F32) for w in ws for _ in range(4)],
    )(*grad_parts, *ws, *ms, *vs)
    return [outs[4 * k:4 * k + 4] for k in range(n)]


def _final_sums(chip, owns, slots, name):
    n = len(owns)

    def body(chip_ref, *refs):
        for w in range(n):
            own_ref, s1_ref, s2_ref, s3_ref = refs[4 * w:4 * w + 4]
            acc = own_ref[0].astype(F32) + s1_ref[0].astype(F32)
            acc = acc + s2_ref[0].astype(F32)
            refs[4 * n + w][...] = acc + s3_ref[0].astype(F32)

    def slot(a, k):
        n_s, rh, cols = a.shape
        return pl.BlockSpec((1, rh // ROW_STEPS, cols), lambda r, chip: (lax.rem(chip[0] + k, n_s), r, 0))

    operands, in_specs = [], []
    for own, slots_w in zip(owns, slots):
        operands += [own, slots_w, slots_w, slots_w]
        in_specs += [slot(own, k) for k in range(4)]
    return pl.pallas_call(
        body, name=name,
        grid_spec=pltpu.PrefetchScalarGridSpec(
            num_scalar_prefetch=1, grid=(ROW_STEPS,),
            in_specs=in_specs,
            out_specs=[pl.BlockSpec((a.shape[1] // ROW_STEPS, a.shape[2]), lambda r, chip: (r, 0)) for a in owns]),
        out_shape=[jax.ShapeDtypeStruct(a.shape[1:], F32) for a in owns],
        compiler_params=_params(1),
    )(chip, *operands)


def _adamw_halves(core, owns, siblings, ws, ms, vs, name):
    n = len(ws)

    def body(c_ref, *refs):
        ins, outs = refs[:5 * n], refs[5 * n:]
        mine = c_ref[0] == pl.program_id(0)
        for k in range(n):
            own_ref, sib_ref, w_ref, m_ref, v_ref = ins[5 * k:5 * k + 5]
            g_ref, d_ref, mo_ref, vo_ref = outs[4 * k:4 * k + 4]

            @pl.when(mine)
            def _():
                g_ref[...] = own_ref[...]

            @pl.when(jnp.logical_not(mine))
            def _():
                g_ref[...] = sib_ref[...]

            d_ref[...], mo_ref[...], vo_ref[...] = _adamw_update(g_ref[...], w_ref[...], m_ref[...], v_ref[...])

    operands, in_specs, out_specs = [], [], []
    for own, sib, w, m, v in zip(owns, siblings, ws, ms, vs):
        rows, cols = w.shape
        rb = rows // 2 // ROW_STEPS
        half = pl.BlockSpec((rb, cols), lambda h, r, c: (r, 0))
        whole = pl.BlockSpec((rb, cols), lambda h, r, c: (h * ROW_STEPS + r, 0))
        operands += [own, sib, w, m, v]
        in_specs += [half, half, whole, whole, whole]
        out_specs += [whole] * 4
    outs = pl.pallas_call(
        body, name=name,
        grid_spec=pltpu.PrefetchScalarGridSpec(
            num_scalar_prefetch=1, grid=(2, ROW_STEPS), in_specs=in_specs, out_specs=out_specs),
        out_shape=[jax.ShapeDtypeStruct(w.shape, F32) for w in ws for _ in range(4)],
        compiler_params=_params(2),
    )(core, *operands)
    return [outs[4 * k:4 * k + 4] for k in range(n)]


def _place():
    x, y, c = lax.axis_index("x"), lax.axis_index("y"), lax.axis_index("c")
    chips = [(1 - x, y), (x, 1 - y), (1 - x, 1 - y)]
    return x, y, c, chips


def _remote(src, dst, send_sem, recv_sem, device):
    return pltpu.make_async_remote_copy(src_ref=src, dst_ref=dst, send_sem=send_sem, recv_sem=recv_sem,
                                        device_id=device, device_id_type=MESH)


def _split_start(name, arrays, n_copies, plan):
    n = len(arrays)

    def body(*refs):
        ins, send_sem, recv_sem, token = refs[:n], refs[n], refs[n + 1], refs[-1]
        for k, (src, dst, device) in enumerate(plan(ins)):
            _remote(src, dst, send_sem.at[k], recv_sem.at[k], device).start()
        token[...] = jnp.zeros_like(token)

    outs = pl.pallas_call(
        body, name=name,
        in_specs=[HBM] * n,
        out_specs=(SEM, SEM, *[HBM] * n, pl.BlockSpec(memory_space=pltpu.VMEM)),
        out_shape=(pltpu.SemaphoreType.DMA((n_copies,)), pltpu.SemaphoreType.DMA((n_copies,)),
                   *[pltpu.HBM(a.shape, a.dtype) for a in arrays], jax.ShapeDtypeStruct((8, LANES), F32)),
        input_output_aliases={i: i + 2 for i in range(n)},
        compiler_params=pltpu.CompilerParams(has_side_effects=EFFECT),
    )(*[pltpu.with_memory_space_constraint(a, pltpu.HBM) for a in arrays])
    return outs[0], outs[1], list(outs[2:2 + n]), outs[-1]


def _split_wait(name, send_sem, recv_sem, arrays, plan, first, *after):
    n = len(arrays)

    def body(*refs):
        ins, send_ref, recv_ref = refs[:n], refs[n], refs[n + 1]
        for k, (src, dst, device) in enumerate(plan(ins)):
            cp = _remote(src, dst, send_ref.at[first + k], recv_ref.at[first + k], device)
            cp.wait_send()
            cp.wait_recv()

    outs = pl.pallas_call(
        body, name=name,
        in_specs=[HBM] * n + [SEM, SEM] + [ANY] * len(after), out_specs=[HBM] * n,
        out_shape=[pltpu.HBM(a.shape, a.dtype) for a in arrays],
        input_output_aliases={i: i for i in range(n)},
        compiler_params=pltpu.CompilerParams(has_side_effects=EFFECT),
    )(*arrays, send_sem, recv_sem, *after)
    return list(outs)


def _plans(*parts):
    def plan(refs):
        copies, lo = [], 0
        for part, n in parts:
            copies += part(refs[lo:lo + n])
            lo += n
        return copies
    return plan


def _plan_gather_send(n):
    def plan(refs):
        x, y, c, chips = _place()
        me = 2 * x + y
        copies = []
        for w in range(n):
            own, full = refs[w], refs[n + w]
            copies.append((own, full.at[me], (x, y, 1 - c)))
            copies += [(own.at[c], full.at[me, c], (px, py, c)) for px, py in chips]
        return copies
    return plan


def _plan_gather_direct(n):
    def plan(refs):
        x, y, c, chips = _place()
        me = 2 * x + y
        copies = []
        for w in range(n):
            own, full = refs[w], refs[n + w]
            copies.append((own, full.at[me], (x, y, 1 - c)))
            for px, py in chips:
                copies += [(own.at[c], full.at[me, c], (px, py, c)), (own.at[c], full.at[me, c], (px, py, 1 - c))]
        return copies
    return plan


def _plan_gather_pass(n):
    def plan(refs):
        x, y, c, chips = _place()
        copies = []
        for w in range(n):
            for px, py in chips:
                landed = refs[w].at[2 * px + py, c]
                copies.append((landed, landed, (x, y, 1 - c)))
        return copies
    return plan


def _plan_swap(n):
    def plan(refs):
        x, y, c, _ = _place()
        return [(refs[w].at[1 - c], refs[n + w], (x, y, 1 - c)) for w in range(n)]
    return plan


def _plan_share(n):
    def plan(refs):
        x, y, c, _ = _place()
        return [(refs[w], refs[n + w], (x, y, 1 - c)) for w in range(n)]
    return plan


def _plan_exchange(n, with_small):
    def plan(refs):
        x, y, c, chips = _place()
        me = 2 * x + y
        n_in = n + int(with_small)
        copies = []
        for w in range(n):
            copies += [(refs[w].at[2 * px + py], refs[n_in + w].at[me], (px, py, c)) for px, py in chips]
        if with_small:
            small, slots = refs[n], refs[n_in + n]
            copies.append((small, slots.at[me], (x, y, 1 - c)))
            copies += [(small, slots.at[me], (px, py, c)) for px, py in chips]
        return copies
    return plan


def _swap_halves(name, grads, wholes):
    n = len(grads)
    arrays = list(grads) + list(wholes)
    n_all = len(arrays)

    def body(*refs):
        ins, outs = refs[:n_all], refs[n_all:2 * n_all]
        send_sem, recv_sem = refs[2 * n_all:]
        x, y, c, _ = _place()
        sibling = (x, y, 1 - c)
        copies = []
        for w in range(n_all):
            src = ins[w].at[1 - c] if w < n else ins[w]
            cp = _remote(src, outs[w], send_sem.at[w], recv_sem.at[w], sibling)
            cp.start()
            copies.append(cp)
        for cp in copies:
            cp.wait()

    return pl.pallas_call(
        body, name=name,
        in_specs=[ANY] * n_all, out_specs=[ANY] * n_all,
        out_shape=[jax.ShapeDtypeStruct(g.shape[1:], g.dtype) for g in grads]
        + [jax.ShapeDtypeStruct(a.shape, a.dtype) for a in wholes],
        scratch_shapes=[pltpu.SemaphoreType.DMA((n_all,))] * 2,
    )(*arrays)


def _share_and_all_reduce(wholes, part):
    n = len(wholes)
    n_dev = 8

    def body(*refs):
        ins, part_ref, outs, sum_ref = refs[:n], refs[n], refs[n + 1:2 * n + 1], refs[2 * n + 1]
        slots_ref, send_sem, recv_sem, send_tile, recv_tile = refs[2 * n + 2:]
        x, y, c, _ = _place()
        me = 4 * x + 2 * y + c
        copies = []
        for w in range(n):
            copies.append(_remote(ins[w], outs[w], send_sem.at[w], recv_sem.at[w], (x, y, 1 - c)))
        slots_ref[me] = part_ref[...]
        flips = [(fx, fy, fc) for fx in (0, 1) for fy in (0, 1) for fc in (0, 1)][1:]
        for k, (fx, fy, fc) in enumerate(flips):
            peer = (x + fx - 2 * x * fx, y + fy - 2 * y * fy, c + fc - 2 * c * fc)
            copies.append(_remote(part_ref, slots_ref.at[me], send_tile.at[k], recv_tile.at[k], peer))
        for cp in copies:
            cp.start()
        for cp in copies:
            cp.wait()
        acc = slots_ref[0]
        for k in range(1, n_dev):
            acc = acc + slots_ref[k]
        sum_ref[...] = acc

    vmem = pl.BlockSpec(memory_space=pltpu.VMEM)
    outs = pl.pallas_call(
        body, name="rs_share_late",
        in_specs=[ANY] * n + [vmem], out_specs=[ANY] * n + [vmem],
        out_shape=[jax.ShapeDtypeStruct(a.shape, a.dtype) for a in wholes] + [jax.ShapeDtypeStruct(part.shape, F32)],
        scratch_shapes=[pltpu.VMEM((n_dev,) + part.shape, F32),
                        pltpu.SemaphoreType.DMA((n,)), pltpu.SemaphoreType.DMA((n,)),
                        pltpu.SemaphoreType.DMA((n_dev - 1,)), pltpu.SemaphoreType.DMA((n_dev - 1,))],
    )(*wholes, part)
    return outs[:n], outs[n]


_SMALL = ("norm1_pre_g", "b_in", "pool_scale", "sgu_ln_g", "sgu_ln_b", "w_spatial",
          "norm1_post_g", "norm2_pre_g", "norm2_post_g", "b_spatial")
_MIX = ("w_in", "w_pool", "w_sgu_proj", "w_out")
_FF = ("w_ff1", "w_ff2")
_BIG = _MIX + _FF
_ORDER = ("norm1_pre_g", "w_in", "b_in", "w_pool", "pool_scale", "sgu_ln_g", "sgu_ln_b", "w_spatial", "b_spatial",
          "w_sgu_proj", "w_out", "norm1_post_g", "norm2_pre_g", "w_ff1", "w_ff2", "norm2_post_g")


def _pack(keys, rows, parts):
    flat = jnp.concatenate([parts[k].reshape(-1).astype(F32) for k in keys])
    flat = jnp.pad(flat, (0, rows * LANES - flat.shape[0]))
    return flat.reshape(rows, LANES)


def _halves(a):
    return a.reshape(2, a.shape[0] // 2, a.shape[1])


def _step(p, m, v, x, target):
    t = x.shape[0]
    tm_mm = min(1024, t)
    tm_mix = min(256, t)
    tm_dx = min(512, t)
    core = lax.axis_index("c").astype(jnp.int32).reshape(1)
    row = lambda a: a.reshape(1, -1)

    chip = (2 * lax.axis_index("x") + lax.axis_index("y")).astype(jnp.int32).reshape(1)

    local2d = {"w_in": p["w_in"], "w_ff1": p["w_ff1"], "w_ff2": p["w_ff2"], "w_out": p["w_out"],
               "w_pool": p["w_pool"].reshape(N_HEADS * 64, POOL_GROUP),
               "w_sgu_proj": p["w_sgu_proj"].reshape(N_HEADS * 64, HEAD)}
    shard = {k: _halves(local2d[k].astype(BF16)) for k in _BIG}
    landing = lambda keys: [lax.empty((N_SHARD,) + shard[k].shape, BF16) for k in keys]
    in_arrays = [shard["w_in"]] + landing(["w_in"])
    send_a, recv_a, in_arrays, token = _split_start("ag_in_send", in_arrays, 4, _plan_gather_send(1))
    z, xn = _inproj_own(x, row(p["norm1_pre_g"]), in_arrays[0].reshape(D, D_IN // N_SHARD), row(p["b_in"]), chip,
                        token, tm_mm)
    mix_keys = _MIX[1:]
    n_mix, n_ff = len(mix_keys), len(_FF)
    in_arrays = _split_wait("ag_in_land", send_a, recv_a, in_arrays, _plan_gather_send(1), 0, z,
                            *[shard[k] for k in mix_keys + _FF])
    rest = in_arrays[1:] + [shard[k] for k in mix_keys] + landing(mix_keys) + [shard[k] for k in _FF] + landing(_FF)
    rest_plan = _plans((_plan_gather_pass(1), 1), (_plan_gather_direct(n_mix), 2 * n_mix),
                       (_plan_gather_send(n_ff), 2 * n_ff))
    send0, recv0, rest, token = _split_start("ag_rest_send", rest, 3 + 7 * n_mix + 4 * n_ff, rest_plan)
    in_full, mix_arrays, ff_arrays = rest[:1], rest[1:1 + 2 * n_mix], rest[1 + 2 * n_mix:]
    in_full = _split_wait("ag_in_passed", send0, recv0, in_full, _plan_gather_pass(1), 0, token)
    w_in4 = in_full[0].reshape(N_SHARD, D, D_IN // N_SHARD)

    pos = jnp.arange(SGU_BLOCK) // 64
    mask = (pos[:, None] >= pos[None, :]).astype(F32)
    ws = (p["w_spatial"] * mask[None]).astype(BF16)
    ws_t = jnp.swapaxes(ws, 1, 2)
    bsp = p["b_spatial"].reshape(N_HEADS, SGU_BLOCK, 1)

    z = _inproj_rest(xn, w_in4, row(p["b_in"]), chip, z, tm_mm)
    mix_arrays = _split_wait("ag_mix_land", send0, recv0, mix_arrays, _plan_gather_direct(n_mix), 3, z)
    full = dict(zip(mix_keys, mix_arrays[len(mix_keys):]))
    w_out = full["w_out"].reshape(D, D)
    regroup = lambda a: a.reshape(N_SHARD, N_HEADS, 64, 256).transpose(1, 0, 2, 3).reshape(N_HEADS, 256, 256)
    w_pool = regroup(full["w_pool"])
    w_proj = regroup(full["w_sgu_proj"])
    pooled, merged, y, h1 = _mix_fwd(z, x, w_pool, row(p["pool_scale"]), row(p["sgu_ln_g"]), row(p["sgu_ln_b"]),
                                     ws, bsp, w_proj, w_out, row(p["norm1_post_g"]), tm_mix)
    ff_arrays = _split_wait("ag_ff_land", send0, recv0, ff_arrays, _plan_gather_send(n_ff), 3 + 7 * n_mix, pooled)
    send2, recv2, ff_full, token = _split_start("ag_ff_pass", ff_arrays[n_ff:], 3 * n_ff, _plan_gather_pass(n_ff))
    ff_full = _split_wait("ag_ff_passed", send2, recv2, ff_full, _plan_gather_pass(n_ff), 0, token)
    w_ff1_4 = ff_full[0].reshape(N_SHARD, D, D)
    w_ff2 = ff_full[1].reshape(N_SHARD, D, D)
    hn, f1, f, df2, dout, vec_a = _ffn_fwd(h1, row(p["norm2_pre_g"]), w_ff1_4, w_ff2, row(p["norm2_post_g"]),
                                           target, tm_mix)

    tk = min(2048, t)
    g_big = {"w_ff2": _dw(f, df2, "dw_ff2", 2, D, tk, N_SHARD, lambda i, j: i)}
    df1, dh1, dy, vec_b = _ffn_bwd(df2, f1, w_ff1_4, w_ff2, h1, dout, y, row(p["norm2_pre_g"]),
                                   row(p["norm1_post_g"]), tm_mix)
    g_big["w_ff1"] = _dw(hn, df1, "dw_ff1", 2, D, tk, N_SHARD, lambda i, j: j)

    def reduce_start(name, keys, wholes, also=(), also_plan=None, also_copies=0, meanwhile=None):
        mine = [g_big[k] for k in keys]
        if meanwhile is None:
            swapped = _swap_halves("rs_swap_" + name, mine, wholes)
        else:
            n_k, swap_plan = len(keys), _plan_swap(len(keys))
            crossing = mine + [lax.empty(g.shape[1:], BF16) for g in mine]
            send, recv, crossing, tok = _split_start("rs_swap_" + name + "_send", crossing, n_k, swap_plan)
            crossing = _split_wait("rs_swap_" + name + "_land", send, recv, crossing, swap_plan, 0, meanwhile(tok))
            mine, swapped = crossing[:n_k], crossing[n_k:]
        sums = _chip_sums(core, mine, swapped[:len(keys)], "chip_sums_" + name)
        arrays = list(sums)
        if wholes:
            arrays.append(_slot_sum(jnp.stack([wholes[0], swapped[-1]]), "chip_sum_small"))
        arrays += [lax.empty(s.shape, BF16) for s in sums]
        if wholes:
            arrays.append(lax.empty((N_SHARD,) + wholes[0].shape, F32))
        plan = _plan_exchange(len(keys), bool(wholes))
        n_copies, n = 3 * len(keys) + 4 * len(wholes), len(arrays)
        both = plan if also_plan is None else _plans((plan, n), (also_plan, len(also)))
        send, recv, out, tok = _split_start("rs_send_" + name, arrays + list(also), n_copies + also_copies, both)
        return send, recv, out[:n], plan, tok, out[n:], n_copies

    def reduce_finish(name, keys, state, *after):
        send, recv, arrays, plan = state[:4]
        arrays = _split_wait("rs_land_" + name, send, recv, arrays, plan, 0, *after)
        n_in = len(arrays) // 2
        n = len(keys)
        finished = _final_sums(chip, arrays[:n], arrays[n_in:n_in + n], "final_sums_" + name)
        return dict(zip(keys, finished)), arrays

    by_half = lambda a, rows: a.reshape(N_SHARD, 2, rows // 2, a.shape[-1]).transpose(1, 0, 2, 3)
    early = _FF
    late = ("w_in", "w_pool", "w_sgu_proj", "w_out")

    def dw_out(token):
        g_big["w_out"] = _dw(merged, dy, "dw_out", 2 * N_SHARD, D, tk, N_SHARD, lambda i, j: 0, token)
        return g_big["w_out"]

    early_state = reduce_start("early", early, [], meanwhile=dw_out)
    dz, dwp, dwpr, dws, dbsp, vec_c, dbin = _mix_bwd(z, pooled, dy, w_pool, row(p["pool_scale"]), row(p["sgu_ln_g"]),
                                                     row(p["sgu_ln_b"]), ws, ws_t, bsp, w_proj, w_out, mask, tm_mix,
                                                     early_state[4])
    g_big["w_in"] = _dw(xn, dz, "dw_in", 2, D_IN // N_SHARD, tk, N_SHARD, lambda i, j: j)
    shard_major = lambda a: a.reshape(N_HEADS, N_SHARD, 64, 256).transpose(1, 0, 2, 3).reshape(N_SHARD * 256, 256)
    g_big["w_pool"] = by_half(shard_major(dwp).astype(BF16), 256)
    g_big["w_sgu_proj"] = by_half(shard_major(dwpr).astype(BF16), 256)
    halves, _ = reduce_finish("early", early, early_state, g_big["w_in"])

    small_part = _pack(_SMALL[1:], SMALL_ROWS - 8, {
        "b_in": dbin[0], "pool_scale": vec_c[0], "sgu_ln_g": vec_c[1], "sgu_ln_b": vec_c[2],
        "w_spatial": dws, "b_spatial": dbsp[:, ::LANES].T, "norm1_post_g": vec_b[1], "norm2_pre_g": vec_b[0],
        "norm2_post_g": vec_a[0]})
    share = [halves[k] for k in early] + [lax.empty(halves[k].shape, F32) for k in early]
    late_state = reduce_start("late", late, [small_part], share, _plan_share(len(early)), len(early))
    grad_x, vec_d = _dx(dz, w_in4, x, dh1, row(p["norm1_pre_g"]), late_state[4], tm_dx)
    share = _split_wait("rs_share_early_land", late_state[0], late_state[1], late_state[5], _plan_share(len(early)),
                        late_state[6], grad_x)

    grads, delta, new_m, new_v = {}, {}, {}, {}

    def update(name, keys, own, shared):
        as2d = lambda a, k: a.reshape(local2d[k].shape)
        outs = _adamw_halves(core, own, shared, [local2d[k] for k in keys],
                             [as2d(m[k], k) for k in keys], [as2d(v[k], k) for k in keys], "adamw_" + name)
        for k, outs_k in zip(keys, outs):
            grads[k], delta[k], new_m[k], new_v[k] = (a.reshape(p[k].shape) for a in outs_k)

    update("early", early, share[:len(early)], share[len(early):])
    late_halves, late_arrays = reduce_finish("late", late, late_state, grad_x, *[new_v[k] for k in early])
    own_late = [late_halves[k] for k in late]
    shared_late, tiles = _share_and_all_reduce(
        own_late, jnp.concatenate([vec_d[0].reshape(8, LANES), jnp.broadcast_to(vec_a[1:2, :LANES], (8, LANES))]))
    g1_grad, loss = tiles[:8], tiles[8, 0]
    update("late", late, own_late, shared_late)
    as_rows = lambda d: [d[k].reshape(-1, LANES) for k in _SMALL]
    small_outs = _adamw_small([g1_grad, late_arrays[-1]], as_rows(p), as_rows(m), as_rows(v))
    for k, outs_k in zip(_SMALL, small_outs):
        grads[k], delta[k], new_m[k], new_v[k] = (a.reshape(p[k].shape) for a in outs_k)

    return (loss, grad_x, *[grads[k] for k in _ORDER], *[delta[k] for k in _ORDER],
            *[new_m[k] for k in _ORDER], *[new_v[k] for k in _ORDER])


def kernel(x, norm1_pre_g, w_in, b_in, w_pool, pool_scale, sgu_ln_g, sgu_ln_b, w_spatial, b_spatial, w_sgu_proj, w_out, norm1_post_g, norm2_pre_g, w_ff1, w_ff2, norm2_post_g, loss_target, m_norm1_pre_g, m_w_in, m_b_in, m_w_pool, m_pool_scale, m_sgu_ln_g, m_sgu_ln_b, m_w_spatial, m_b_spatial, m_w_sgu_proj, m_w_out, m_norm1_post_g, m_norm2_pre_g, m_w_ff1, m_w_ff2, m_norm2_post_g, v_norm1_pre_g, v_w_in, v_b_in, v_w_pool, v_pool_scale, v_sgu_ln_g, v_sgu_ln_b, v_w_spatial, v_b_spatial, v_w_sgu_proj, v_w_out, v_norm1_post_g, v_norm2_pre_g, v_w_ff1, v_w_ff2, v_norm2_post_g):
    p = dict(zip(_ORDER, (norm1_pre_g, w_in, b_in, w_pool, pool_scale, sgu_ln_g, sgu_ln_b, w_spatial, b_spatial,
                          w_sgu_proj, w_out, norm1_post_g, norm2_pre_g, w_ff1, w_ff2, norm2_post_g)))
    m = dict(zip(_ORDER, (m_norm1_pre_g, m_w_in, m_b_in, m_w_pool, m_pool_scale, m_sgu_ln_g, m_sgu_ln_b, m_w_spatial,
                          m_b_spatial, m_w_sgu_proj, m_w_out, m_norm1_post_g, m_norm2_pre_g, m_w_ff1, m_w_ff2,
                          m_norm2_post_g)))
    v = dict(zip(_ORDER, (v_norm1_pre_g, v_w_in, v_b_in, v_w_pool, v_pool_scale, v_sgu_ln_g, v_sgu_ln_b, v_w_spatial,
                          v_b_spatial, v_w_sgu_proj, v_w_out, v_norm1_post_g, v_norm2_pre_g, v_w_ff1, v_w_ff2,
                          v_norm2_post_g)))
    outs = _step(p, m, v, x[0], loss_target[0])
    return (outs[0], outs[1][None], *outs[2:])
```

```python
import math

import jax
import jax.numpy as jnp
from jax import lax
from jax.experimental import pallas as pl
from jax.experimental.pallas import tpu as pltpu

F32 = jnp.float32
BF16 = jnp.bfloat16

D = 1024
D_IN = 5 * D
D_FF = 4 * D
N_SHARD = 4
POOL_WINDOWS = (2, 4, 8, 16)
POOL_GROUP = 256
POOL_HALO = 16
SGU_BLOCK = 128
N_HEADS = 4
HEAD = 256
EPS = 1e-6

ADAM_LR = 0.001
ADAM_B1 = 0.9
ADAM_B2 = 0.999
ADAM_EPS = 1e-08
ADAM_WD = 0.01
ADAM_STEP = 10

V7X_VMEM_BYTES = 64 * 1024 * 1024
VMEM_LIMIT = V7X_VMEM_BYTES - 8 * 1024 * 1024
SMALL_ROWS = 616
LANES = 128

MESH = pl.DeviceIdType.MESH
ANY = pl.BlockSpec(memory_space=pl.ANY)
HBM = pl.BlockSpec(memory_space=pltpu.HBM)
SEM = pl.BlockSpec(memory_space=pltpu.SEMAPHORE)
EFFECT = pltpu.SideEffectType.DATAFLOW_SIDE_EFFECTING

NT_DIMS = (((1,), (1,)), ((), ()))
TN_DIMS = (((0,), (0,)), ((), ()))


def _params(n_axes):
    return pltpu.CompilerParams(dimension_semantics=("arbitrary",) * n_axes, vmem_limit_bytes=VMEM_LIMIT)


def _resident(shape, n_axes):
    zeros = (0,) * len(shape)
    index = (lambda i: zeros) if n_axes == 1 else (lambda i, j: zeros)
    return pl.BlockSpec(shape, index, pipeline_mode=pl.Buffered(1))


def _dot(a, b):
    return jnp.dot(a, b, preferred_element_type=F32)


def _dot_nt(a, b):
    return lax.dot_general(a, b, NT_DIMS, preferred_element_type=F32)


def _dot_tn(a, b):
    return lax.dot_general(a, b, TN_DIMS, preferred_element_type=F32)


def _mean(a):
    return jnp.mean(a, axis=-1, keepdims=True)


def _colsum(a):
    return jnp.sum(a, axis=0, keepdims=True)


_GELU_C0 = math.sqrt(2.0 / math.pi)
_GELU_C1 = 0.044715


def _gelu(a):
    t = jnp.tanh(a * (_GELU_C0 + (_GELU_C0 * _GELU_C1) * (a * a)))
    return a * (0.5 * t + 0.5)


def _gelu_and_grad(a):
    a2 = a * a
    t = jnp.tanh(a * (_GELU_C0 + (_GELU_C0 * _GELU_C1) * a2))
    cdf = 0.5 * t + 0.5
    grad = cdf + (0.5 * a) * (1.0 - t * t) * (_GELU_C0 + (3.0 * _GELU_C0 * _GELU_C1) * a2)
    return a * cdf, grad


def _sigmoid(a):
    return 0.5 * jnp.tanh(0.5 * a) + 0.5


def _rms_norm_bf16(a, g):
    return (a * lax.rsqrt(_mean(a * a) + EPS) * g).astype(BF16)


def _relu_sq_bf16(a):
    r = jnp.maximum(a, 0.0)
    return (r * r).astype(BF16)


def _window_sum(a, w, ahead):
    n = a.shape[0]
    step = 1
    while step < w:
        a = a + pltpu.roll(a, n - step if ahead else step, 0)
        step *= 2
    return a


def _rms_bwd(dn, n, r):
    return r * (dn - n * _mean(dn * n))


def _inproj_own(x, g1, w_own, b_in, chip, dep, tm):
    t = x.shape[0]
    cw = D_IN // N_SHARD

    def body(chip_ref, x_ref, g_ref, w_ref, b_ref, dep_ref, z_ref, xn_ref):
        xn = _rms_norm_bf16(x_ref[...], g_ref[...])
        xn_ref[...] = xn
        z_ref[...] = _dot(xn, w_ref[...]) + b_ref[...]

    return pl.pallas_call(
        body, name="inproj_own",
        grid_spec=pltpu.PrefetchScalarGridSpec(
            num_scalar_prefetch=1, grid=(t // tm,),
            in_specs=[pl.BlockSpec((tm, D), lambda i, chip: (i, 0)),
                      pl.BlockSpec((1, D), lambda i, chip: (0, 0)),
                      pl.BlockSpec((D, cw), lambda i, chip: (0, 0)),
                      pl.BlockSpec((1, cw), lambda i, chip: (0, chip[0])),
                      pl.BlockSpec(dep.shape, lambda i, chip: (0, 0))],
            out_specs=[pl.BlockSpec((tm, cw), lambda i, chip: (i, chip[0])),
                       pl.BlockSpec((tm, D), lambda i, chip: (i, 0))]),
        out_shape=[jax.ShapeDtypeStruct((t, D_IN), F32), jax.ShapeDtypeStruct((t, D), BF16)],
        compiler_params=_params(1),
    )(chip, x, g1, w_own, b_in, dep)


def _inproj_rest(xn, w_in4, b_in, chip, z, tm):
    t = xn.shape[0]
    cw = D_IN // N_SHARD
    shard_of = lambda s, chip: lax.rem(chip[0] + 1 + s, N_SHARD)

    def body(chip_ref, xn_ref, w_ref, b_ref, z_in_ref, z_ref):
        z_ref[...] = _dot(xn_ref[...], w_ref[shard_of(pl.program_id(1), chip_ref)]) + b_ref[...]

    return pl.pallas_call(
        body, name="inproj_rest",
        grid_spec=pltpu.PrefetchScalarGridSpec(
            num_scalar_prefetch=1, grid=(t // tm, N_SHARD - 1),
            in_specs=[pl.BlockSpec((tm, D), lambda i, s, chip: (i, 0)),
                      pl.BlockSpec((N_SHARD, D, cw), lambda i, s, chip: (0, 0, 0), pipeline_mode=pl.Buffered(1)),
                      pl.BlockSpec((1, cw), lambda i, s, chip: (0, shard_of(s, chip))),
                      ANY],
            out_specs=pl.BlockSpec((tm, cw), lambda i, s, chip: (i, shard_of(s, chip)))),
        out_shape=jax.ShapeDtypeStruct(z.shape, F32),
        input_output_aliases={4: 0},
        compiler_params=_params(2),
    )(chip, xn, w_in4, b_in, z)


def _sgu_forward(z_ref, lng_ref, lnb_ref, ws_ref, bsp_ref, sv_ref, tm, with_grad):
    if with_grad:
        u, du_dz = _gelu_and_grad(z_ref[:, D:2 * D])
        gv, dgv_dz = _gelu_and_grad(z_ref[:, 2 * D:3 * D])
    else:
        u = _gelu(z_ref[:, D:2 * D])
        gv = _gelu(z_ref[:, 2 * D:3 * D])
        du_dz = dgv_dz = None
    xc = gv - _mean(gv)
    rstd = lax.rsqrt(_mean(xc * xc) + EPS)
    vhat = xc * rstd
    vb = (vhat * lng_ref[...] + lnb_ref[...]).astype(BF16)
    for b in range(tm // SGU_BLOCK):
        rows = slice(b * SGU_BLOCK, (b + 1) * SGU_BLOCK)
        for h in range(N_HEADS):
            cols = slice(h * HEAD, (h + 1) * HEAD)
            sv_ref[rows, cols] = _dot(ws_ref[h], vb[rows, cols]) + bsp_ref[h]
    return u, du_dz, dgv_dz, vhat, rstd, vb


def _mix_fwd(z, x, w_pool, pool_scale, ln_g, ln_b, ws, bsp, w_proj, w_out, g1post, tm):
    t = x.shape[0]

    def body(z_ref, x_ref, wp_ref, psc_ref, lng_ref, lnb_ref, ws_ref, bsp_ref, wpr_ref, wo_ref, g1p_ref,
             pooled_ref, merged_ref, y_ref, h1_ref, win_ref, sv_ref):
        i = pl.program_id(0)

        @pl.when(i == 0)
        def _():
            win_ref[0:POOL_HALO, :] = jnp.zeros((POOL_HALO, D), F32)

        win_ref[POOL_HALO:POOL_HALO + tm, :] = z_ref[:, 0:D]
        pos1 = lax.broadcasted_iota(jnp.int32, (tm, 1), 0) + (i * tm + 1)
        a_parts = []
        for g, w in enumerate(POOL_WINDOWS):
            cols = slice(g * POOL_GROUP, (g + 1) * POOL_GROUP)
            ts = _window_sum(win_ref[:, cols], w, False)[POOL_HALO:POOL_HALO + tm, :]
            inv_cnt = 1.0 / jnp.minimum(pos1, w).astype(F32)
            pooled = (ts * inv_cnt - z_ref[:, cols]).astype(BF16)
            pooled_ref[:, cols] = pooled
            a_parts.append(_dot(pooled, wp_ref[g]) * psc_ref[:, cols])
        win_ref[0:POOL_HALO, :] = win_ref[tm:tm + POOL_HALO, :]

        u, _, _, _, _, _ = _sgu_forward(z_ref, lng_ref, lnb_ref, ws_ref, bsp_ref, sv_ref, tm, False)
        gated = (u * sv_ref[...]).astype(BF16)
        for h in range(N_HEADS):
            lo = h * HEAD
            bbr = _dot(gated[:, lo:lo + HEAD], wpr_ref[h])
            gate_a = _sigmoid(z_ref[:, 3 * D + lo:3 * D + lo + HEAD])
            gate_b = _sigmoid(z_ref[:, 4 * D + lo:4 * D + lo + HEAD])
            merged_ref[:, lo:lo + HEAD] = (gate_a * a_parts[h] + gate_b * bbr).astype(BF16)
        merged = merged_ref[...]
        y = _dot(merged, wo_ref[...])
        y_ref[...] = y
        r = lax.rsqrt(_mean(y * y) + EPS)
        h1_ref[...] = x_ref[...] + (y * r) * g1p_ref[...]

    row = lambda i: (i, 0)
    fixed2 = lambda i: (0, 0)
    fixed3 = lambda i: (0, 0, 0)
    vec = pl.BlockSpec((1, D), fixed2)
    return pl.pallas_call(
        body, name="mix_fwd",
        grid=(t // tm,),
        in_specs=[pl.BlockSpec((tm, D_IN), row), pl.BlockSpec((tm, D), row),
                  pl.BlockSpec((N_HEADS, POOL_GROUP, POOL_GROUP), fixed3), vec, vec, vec,
                  pl.BlockSpec((N_HEADS, SGU_BLOCK, SGU_BLOCK), fixed3),
                  pl.BlockSpec((N_HEADS, SGU_BLOCK, 1), fixed3),
                  pl.BlockSpec((N_HEADS, HEAD, HEAD), fixed3),
                  pl.BlockSpec((D, D), fixed2), vec],
        out_specs=[pl.BlockSpec((tm, D), row)] * 4,
        out_shape=[jax.ShapeDtypeStruct((t, D), BF16), jax.ShapeDtypeStruct((t, D), BF16),
                   jax.ShapeDtypeStruct((t, D), F32), jax.ShapeDtypeStruct((t, D), F32)],
        scratch_shapes=[pltpu.VMEM((tm + POOL_HALO, D), F32), pltpu.VMEM((tm, D), F32)],
        compiler_params=_params(1),
    )(z, x, w_pool, pool_scale, ln_g, ln_b, ws, bsp, w_proj, w_out, g1post)


def _ffn_fwd(h1, g2pre, w1, w2, g2post, target, tm):
    t = h1.shape[0]
    n_j = D_FF // D

    def body(h1_ref, g2_ref, w1_ref, w2_ref, g2p_ref, tgt_ref,
             hn_ref, f1_ref, f_ref, df2_ref, dout_ref, vec_ref):
        @pl.when(pl.program_id(0) == 0)
        def _():
            vec_ref[...] = jnp.zeros_like(vec_ref)

        hn = _rms_norm_bf16(h1_ref[...], g2_ref[...])
        hn_ref[...] = hn
        f2 = None
        for j in range(n_j):
            cols = slice(j * D, (j + 1) * D)
            f1 = _dot(hn, w1_ref[j])
            f1_ref[:, cols] = f1
            f = _relu_sq_bf16(f1)
            f_ref[:, cols] = f
            part = _dot(f, w2_ref[j])
            f2 = part if f2 is None else f2 + part
        r = lax.rsqrt(_mean(f2 * f2) + EPS)
        n2 = f2 * r
        err = h1_ref[...] + n2 * g2p_ref[...] - tgt_ref[...]
        loss = 0.5 * jnp.sum(_mean(err * err))
        dout = err * (1.0 / D)
        dout_ref[...] = dout
        vec_ref[0:1, :] += _colsum(dout * n2)
        vec_ref[1:2, :] += jnp.full((1, D), loss, F32)
        df2_ref[...] = _rms_bwd(dout * g2p_ref[...], n2, r).astype(BF16)

    row = lambda i: (i, 0)
    fixed = lambda i: (0, 0)
    vec = pl.BlockSpec((1, D), fixed)
    return pl.pallas_call(
        body, name="ffn_fwd",
        grid=(t // tm,),
        in_specs=[pl.BlockSpec((tm, D), row), vec,
                  _resident((n_j, D, D), 1), _resident((n_j, D, D), 1), vec,
                  pl.BlockSpec((tm, D), row)],
        out_specs=[pl.BlockSpec((tm, D), row),
                   pl.BlockSpec((tm, D_FF), row), pl.BlockSpec((tm, D_FF), row),
                   pl.BlockSpec((tm, D), row), pl.BlockSpec((tm, D), row),
                   pl.BlockSpec((8, D), fixed)],
        out_shape=[jax.ShapeDtypeStruct((t, D), BF16), jax.ShapeDtypeStruct((t, D_FF), F32),
                   jax.ShapeDtypeStruct((t, D_FF), BF16), jax.ShapeDtypeStruct((t, D), BF16),
                   jax.ShapeDtypeStruct((t, D), F32), jax.ShapeDtypeStruct((8, D), F32)],
        compiler_params=_params(1),
    )(h1, g2pre, w1, w2, g2post, target)


def _ffn_bwd(df2, f1, w1, w2, h1, dout, y, g2pre, g1post, tm):
    t = h1.shape[0]
    n_j = D_FF // D

    def body(df2_ref, f1_ref, w1_ref, w2_ref, h1_ref, dout_ref, y_ref, g2_ref, g1p_ref,
             df1_ref, dh1_ref, dy_ref, vec_ref):
        @pl.when(pl.program_id(0) == 0)
        def _():
            vec_ref[...] = jnp.zeros_like(vec_ref)

        df2 = df2_ref[...]
        dhn = None
        for j in range(n_j):
            cols = slice(j * D, (j + 1) * D)
            df = _dot_nt(df2, w2_ref[j])
            df1 = (df * (2.0 * jnp.maximum(f1_ref[:, cols], 0.0))).astype(BF16)
            df1_ref[:, cols] = df1
            part = _dot_nt(df1, w1_ref[j])
            dhn = part if dhn is None else dhn + part
        h = h1_ref[...]
        r2 = lax.rsqrt(_mean(h * h) + EPS)
        nh = h * r2
        vec_ref[0:1, :] += _colsum(dhn * nh)
        dh1 = dout_ref[...] + _rms_bwd(dhn * g2_ref[...], nh, r2)
        dh1_ref[...] = dh1
        yv = y_ref[...]
        r1 = lax.rsqrt(_mean(yv * yv) + EPS)
        ny = yv * r1
        vec_ref[1:2, :] += _colsum(dh1 * ny)
        dy_ref[...] = _rms_bwd(dh1 * g1p_ref[...], ny, r1).astype(BF16)

    row = lambda i: (i, 0)
    fixed = lambda i: (0, 0)
    vec = pl.BlockSpec((1, D), fixed)
    return pl.pallas_call(
        body, name="ffn_bwd",
        grid=(t // tm,),
        in_specs=[pl.BlockSpec((tm, D), row), pl.BlockSpec((tm, D_FF), row),
                  _resident((n_j, D, D), 1), _resident((n_j, D, D), 1),
                  pl.BlockSpec((tm, D), row), pl.BlockSpec((tm, D), row), pl.BlockSpec((tm, D), row), vec, vec],
        out_specs=[pl.BlockSpec((tm, D_FF), row),
                   pl.BlockSpec((tm, D), row), pl.BlockSpec((tm, D), row),
                   pl.BlockSpec((8, D), fixed)],
        out_shape=[jax.ShapeDtypeStruct((t, D_FF), BF16), jax.ShapeDtypeStruct((t, D), F32),
                   jax.ShapeDtypeStruct((t, D), BF16), jax.ShapeDtypeStruct((8, D), F32)],
        compiler_params=_params(1),
    )(df2, f1, w1, w2, h1, dout, y, g2pre, g1post)


def _mix_bwd(z, pooled, dy, w_pool, pool_scale, ln_g, ln_b, ws, ws_t, bsp, w_proj, w_out, mask, tm, dep):
    t = z.shape[0]
    n_t = t // tm
    shard_rows = POOL_GROUP // N_SHARD

    def body(z_ref, pooled_ref, dy_ref, wp_ref, psc_ref, lng_ref, lnb_ref, ws_ref, wst_ref, bsp_ref, wpr_ref,
             wo_ref, mask_ref, dep_ref,
             dz_ref, dwp_ref, dwpr_ref, dws_ref, dbsp_ref, vec_ref, dbin_ref,
             win_ref, sv_ref, dsv_ref, dv_ref, dsvsum_ref, dwp_acc, dwpr_acc):
        i = pl.program_id(0)

        @pl.when(i == 0)
        def _():
            win_ref[tm:tm + POOL_HALO, :] = jnp.zeros((POOL_HALO, D), F32)
            dwp_acc[...] = jnp.zeros_like(dwp_acc)
            dwpr_acc[...] = jnp.zeros_like(dwpr_acc)
            dws_ref[...] = jnp.zeros_like(dws_ref)
            vec_ref[...] = jnp.zeros_like(vec_ref)
            dbin_ref[...] = jnp.zeros_like(dbin_ref)
            dsvsum_ref[...] = jnp.zeros_like(dsvsum_ref)

        dmerged = _dot_nt(dy_ref[...], wo_ref[...])

        u, du_dz, dgv_dz, vhat, rstd, vb = _sgu_forward(z_ref, lng_ref, lnb_ref, ws_ref, bsp_ref, sv_ref, tm, True)
        sv = sv_ref[...]
        gated = (u * sv).astype(BF16)

        sa = _sigmoid(z_ref[:, 3 * D:4 * D])
        sb = _sigmoid(z_ref[:, 4 * D:5 * D])
        da = dmerged * sa
        dbbr = (dmerged * sb).astype(BF16)

        tile = n_t - 1 - i
        pos1 = lax.broadcasted_iota(jnp.int32, (tm, 1), 0) + (tile * tm + 1)
        def put(k, lo, part):
            cols = slice(k * D + lo, k * D + lo + part.shape[1])
            dz_ref[:, cols] = part.astype(BF16)
            dbin_ref[0:1, cols] += _colsum(part)

        dpooled_parts = []
        for g, w in enumerate(POOL_WINDOWS):
            cols = slice(g * POOL_GROUP, (g + 1) * POOL_GROUP)
            pg = pooled_ref[:, cols]
            a_pre = _dot(pg, wp_ref[g])
            da_g = da[:, cols]
            psc = psc_ref[:, cols]
            vec_ref[0:1, cols] += _colsum(da_g * a_pre)
            sa_g = sa[:, cols]
            put(3, g * POOL_GROUP, dmerged[:, cols] * (a_pre * psc) * (sa_g * (1.0 - sa_g)))
            da_pre = (da_g * psc).astype(BF16)
            dwp_acc[g] += _dot_tn(pg, da_pre)
            dpooled = _dot_nt(da_pre, wp_ref[g])
            dpooled_parts.append(dpooled)
            inv_cnt = 1.0 / jnp.minimum(pos1, w).astype(F32)
            win_ref[0:tm, cols] = dpooled * inv_cnt
        for g, w in enumerate(POOL_WINDOWS):
            cols = slice(g * POOL_GROUP, (g + 1) * POOL_GROUP)
            acc = _window_sum(win_ref[:, cols], w, True)[0:tm, :]
            put(0, g * POOL_GROUP, acc - dpooled_parts[g])
        win_ref[tm:tm + POOL_HALO, :] = win_ref[0:POOL_HALO, :]

        for h in range(N_HEADS):
            cols = slice(h * HEAD, (h + 1) * HEAD)
            g_h = gated[:, cols]
            db_h = dbbr[:, cols]
            bbr_h = _dot(g_h, wpr_ref[h])
            sb_h = sb[:, cols]
            put(4, h * HEAD, dmerged[:, cols] * bbr_h * (sb_h * (1.0 - sb_h)))
            dwpr_acc[h] += _dot_tn(g_h, db_h)
            dgated_h = _dot_nt(db_h, wpr_ref[h])
            dsv_ref[:, cols] = dgated_h * u[:, cols]
            sv_ref[:, cols] = dgated_h * sv[:, cols]
        put(1, 0, sv_ref[...] * du_dz)
        dsv = dsv_ref[...]
        dsvb = dsv.astype(BF16)
        blk_sum = dsv[0:SGU_BLOCK, :]
        for b in range(1, tm // SGU_BLOCK):
            blk_sum = blk_sum + dsv[b * SGU_BLOCK:(b + 1) * SGU_BLOCK, :]
        dsvsum_ref[...] += blk_sum
        for b in range(tm // SGU_BLOCK):
            rows = slice(b * SGU_BLOCK, (b + 1) * SGU_BLOCK)
            for h in range(N_HEADS):
                cols = slice(h * HEAD, (h + 1) * HEAD)
                dws_ref[h] += _dot_nt(dsvb[rows, cols], vb[rows, cols])
                dv_ref[rows, cols] = _dot(wst_ref[h], dsvb[rows, cols])
        dv = dv_ref[...]
        vec_ref[1:2, :] += _colsum(dv * vhat)
        vec_ref[2:3, :] += _colsum(dv)
        dvhat = dv * lng_ref[...]
        dgv = rstd * (dvhat - _mean(dvhat) - vhat * _mean(dvhat * vhat))
        put(2, 0, dgv * dgv_dz)

        @pl.when(i == n_t - 1)
        def _():
            for h in range(N_HEADS):
                dws_ref[h] = jnp.where(mask_ref[...] > 0.0, dws_ref[h], 0.0)
                tot = jnp.sum(dsvsum_ref[:, h * HEAD:(h + 1) * HEAD], axis=1, keepdims=True)
                dbsp_ref[:, h * LANES:(h + 1) * LANES] = jnp.broadcast_to(tot, (SGU_BLOCK, LANES))
            for acc, out in ((dwp_acc, dwp_ref), (dwpr_acc, dwpr_ref)):
                for g in range(N_HEADS):
                    for s in range(N_SHARD):
                        out[g // 2, s, (g % 2) * shard_rows:(g % 2 + 1) * shard_rows, :] = (
                            acc[g, s * shard_rows:(s + 1) * shard_rows, :].astype(BF16))

    row = lambda i: (n_t - 1 - i, 0)
    fixed2 = lambda i: (0, 0)
    fixed3 = lambda i: (0, 0, 0)
    fixed4 = lambda i: (0, 0, 0, 0)
    vec = pl.BlockSpec((1, D), fixed2)
    sq = pl.BlockSpec((N_HEADS, SGU_BLOCK, SGU_BLOCK), fixed3)
    grp = pl.BlockSpec((N_HEADS, HEAD, HEAD), fixed3)
    grp_out = pl.BlockSpec((2, N_SHARD, 2 * shard_rows, HEAD), fixed4)
    tile = pl.BlockSpec((tm, D), row)
    return pl.pallas_call(
        body, name="mix_bwd",
        grid=(n_t,),
        in_specs=[pl.BlockSpec((tm, D_IN), row), tile, tile,
                  grp, vec, vec, vec, sq, sq, pl.BlockSpec((N_HEADS, SGU_BLOCK, 1), fixed3), grp,
                  pl.BlockSpec((D, D), fixed2), pl.BlockSpec((SGU_BLOCK, SGU_BLOCK), fixed2),
                  pl.BlockSpec(dep.shape, fixed2)],
        out_specs=[pl.BlockSpec((tm, D_IN), row), grp_out, grp_out, sq,
                   pl.BlockSpec((SGU_BLOCK, N_HEADS * LANES), fixed2),
                   pl.BlockSpec((8, D), fixed2), pl.BlockSpec((8, D_IN), fixed2)],
        out_shape=[jax.ShapeDtypeStruct((t, D_IN), BF16),
                   jax.ShapeDtypeStruct((2, N_SHARD, 2 * shard_rows, HEAD), BF16),
                   jax.ShapeDtypeStruct((2, N_SHARD, 2 * shard_rows, HEAD), BF16),
                   jax.ShapeDtypeStruct((N_HEADS, SGU_BLOCK, SGU_BLOCK), F32),
                   jax.ShapeDtypeStruct((SGU_BLOCK, N_HEADS * LANES), F32),
                   jax.ShapeDtypeStruct((8, D), F32), jax.ShapeDtypeStruct((8, D_IN), F32)],
        scratch_shapes=[pltpu.VMEM((tm + POOL_HALO, D), F32), pltpu.VMEM((tm, D), F32), pltpu.VMEM((tm, D), F32),
                        pltpu.VMEM((tm, D), F32), pltpu.VMEM((SGU_BLOCK, D), F32),
                        pltpu.VMEM((N_HEADS, HEAD, HEAD), F32), pltpu.VMEM((N_HEADS, HEAD, HEAD), F32)],
        compiler_params=_params(1),
    )(z, pooled, dy, w_pool, pool_scale, ln_g, ln_b, ws, ws_t, bsp, w_proj, w_out, mask, dep)


def _dx(dz, w_in4, x, dh1, g1, dep, tm):
    t = x.shape[0]
    cw = D_IN // N_SHARD

    def body(dz_ref, w_ref, x_ref, dh1_ref, g_ref, dep_ref, gx_ref, vec_ref):
        @pl.when(pl.program_id(0) == 0)
        def _():
            vec_ref[...] = jnp.zeros_like(vec_ref)

        dxn = _dot_nt(dz_ref[:, 0:cw], w_ref[0])
        for s in range(1, N_SHARD):
            dxn = dxn + _dot_nt(dz_ref[:, s * cw:(s + 1) * cw], w_ref[s])
        xv = x_ref[...]
        r = lax.rsqrt(_mean(xv * xv) + EPS)
        xh = xv * r
        vec_ref[0:1, :] += _colsum(dxn * xh)
        gx_ref[...] = dh1_ref[...] + _rms_bwd(dxn * g_ref[...], xh, r)

    row = lambda i: (i, 0)
    fixed = lambda i: (0, 0)
    return pl.pallas_call(
        body, name="dx",
        grid=(t // tm,),
        in_specs=[pl.BlockSpec((tm, D_IN), row),
                  _resident((N_SHARD, D, cw), 1),
                  pl.BlockSpec((tm, D), row), pl.BlockSpec((tm, D), row), pl.BlockSpec((1, D), fixed),
                  pl.BlockSpec(dep.shape, fixed)],
        out_specs=[pl.BlockSpec((tm, D), row), pl.BlockSpec((8, D), fixed)],
        out_shape=[jax.ShapeDtypeStruct((t, D), F32), jax.ShapeDtypeStruct((8, D), F32)],
        compiler_params=_params(1),
    )(dz, w_in4, x, dh1, g1, dep)


def _dw(a, b, name, n_split, tn, tk, n_out, out_block, dep=None):
    t, m = a.shape
    n = b.shape[1]
    tm = min(m, D)
    rows = tm // n_split
    n_k = t // tk
    deps = [] if dep is None else [dep]

    def body(a_ref, b_ref, *rest):
        o_ref, acc_ref = rest[-2:]
        k = pl.program_id(2)

        @pl.when(k == 0)
        def _():
            acc_ref[...] = jnp.zeros_like(acc_ref)

        acc_ref[...] += _dot_tn(a_ref[...], b_ref[...])

        @pl.when(k == n_k - 1)
        def _():
            for q in range(n_split):
                o_ref[q % 2, q // 2] = acc_ref[q * rows:(q + 1) * rows, :].astype(BF16)

    return pl.pallas_call(
        body, name=name,
        grid=(m // tm, n // tn, n_k),
        in_specs=[pl.BlockSpec((tk, tm), lambda i, j, k: (k, i)),
                  pl.BlockSpec((tk, tn), lambda i, j, k: (k, j))]
        + [pl.BlockSpec(d.shape, lambda i, j, k: (0, 0)) for d in deps],
        out_specs=pl.BlockSpec((2, n_split // 2, rows, tn), lambda i, j, k: (0, out_block(i, j), 0, 0)),
        out_shape=jax.ShapeDtypeStruct((2, n_out, rows, tn), BF16),
        scratch_shapes=[pltpu.VMEM((tm, tn), F32)],
        compiler_params=_params(3),
    )(a, b, *deps)


def _row_block(rows, cols, itemsize, n_bufs):
    budget = VMEM_LIMIT // 4
    rb = rows
    while rb % 16 == 0 and rb * cols * itemsize * n_bufs * 2 > budget:
        rb //= 2
    return rb


ROW_STEPS = 2


def _chip_sums(core, grads, recvs, name):
    n = len(grads)

    def body(c_ref, *refs):
        for w in range(n):
            refs[2 * n + w][...] = (refs[w][0].astype(F32) + refs[n + w][...].astype(F32)).astype(BF16)

    def blocks(g):
        _, _, rh, cols = g.shape
        rb = rh // ROW_STEPS
        return (pl.BlockSpec((1, 1, rb, cols), lambda s, r, c: (c[0], s, r, 0)),
                pl.BlockSpec((1, rb, cols), lambda s, r, c: (s, r, 0)))

    specs = [blocks(g) for g in grads]
    return pl.pallas_call(
        body, name=name,
        grid_spec=pltpu.PrefetchScalarGridSpec(
            num_scalar_prefetch=1, grid=(N_SHARD, ROW_STEPS),
            in_specs=[s[0] for s in specs] + [s[1] for s in specs],
            out_specs=[s[1] for s in specs]),
        out_shape=[jax.ShapeDtypeStruct(g.shape[1:], BF16) for g in grads],
        compiler_params=_params(2),
    )(core, *grads, *recvs)


def _slot_sum(slots, name):
    n_s, rows, cols = slots.shape
    rb = _row_block(rows, cols, 4, n_s + 1)

    def body(s_ref, o_ref):
        acc = s_ref[0].astype(F32)
        for k in range(1, n_s):
            acc = acc + s_ref[k].astype(F32)
        o_ref[...] = acc

    return pl.pallas_call(
        body, name=name,
        grid=(rows // rb,),
        in_specs=[pl.BlockSpec((n_s, rb, cols), lambda r: (0, r, 0))],
        out_specs=pl.BlockSpec((rb, cols), lambda r: (r, 0)),
        out_shape=jax.ShapeDtypeStruct((rows, cols), F32),
        compiler_params=_params(1),
    )(slots)


def _adamw_update(g, w, m, v):
    m_new = ADAM_B1 * m + (1.0 - ADAM_B1) * g
    v_new = ADAM_B2 * v + (1.0 - ADAM_B2) * (g * g)
    m_hat = m_new / (1.0 - ADAM_B1 ** ADAM_STEP)
    v_hat = v_new / (1.0 - ADAM_B2 ** ADAM_STEP)
    return -ADAM_LR * (m_hat / (jnp.sqrt(v_hat) + ADAM_EPS) + ADAM_WD * w), m_new, v_new


def _adamw_small(grad_parts, ws, ms, vs):
    n, n_g = len(ws), len(grad_parts)

    def rows_of(ref, lo, rows):
        if len(ref.shape) == 2:
            return ref[lo:lo + rows, :]
        acc = ref[0, lo:lo + rows, :]
        for s in range(1, ref.shape[0]):
            acc = acc + ref[s, lo:lo + rows, :]
        return acc

    def body(*refs):
        g_refs, ins, outs = refs[:n_g], refs[n_g:n_g + 3 * n], refs[n_g + 3 * n:]
        part, off = 0, 0
        for k in range(n):
            rows = ws[k].shape[0]
            if off + rows > grad_parts[part].shape[-2]:
                part, off = part + 1, 0
            g = rows_of(g_refs[part], off, rows)
            off += rows
            delta, m_new, v_new = _adamw_update(g, ins[k][...], ins[n + k][...], ins[2 * n + k][...])
            for o, val in zip(outs[4 * k:4 * k + 4], (g, delta, m_new, v_new)):
                o[...] = val

    vmem = pl.BlockSpec(memory_space=pltpu.VMEM)
    outs = pl.pallas_call(
        body, name="adamw_small",
        in_specs=[vmem] * (n_g + 3 * n), out_specs=[vmem] * (4 * n),
        out_shape=[jax.ShapeDtypeStruct(w.shape, F32) for w in ws for _ in range(4)],
    )(*grad_parts, *ws, *ms, *vs)
    return [outs[4 * k:4 * k + 4] for k in range(n)]


def _final_sums(chip, owns, slots, name):
    n = len(owns)

    def body(chip_ref, *refs):
        for w in range(n):
            own_ref, s1_ref, s2_ref, s3_ref = refs[4 * w:4 * w + 4]
            acc = own_ref[0].astype(F32) + s1_ref[0].astype(F32)
            acc = acc + s2_ref[0].astype(F32)
            refs[4 * n + w][...] = acc + s3_ref[0].astype(F32)

    def slot(a, k):
        n_s, rh, cols = a.shape
        return pl.BlockSpec((1, rh // ROW_STEPS, cols), lambda r, chip: (lax.rem(chip[0] + k, n_s), r, 0))

    operands, in_specs = [], []
    for own, slots_w in zip(owns, slots):
        operands += [own, slots_w, slots_w, slots_w]
        in_specs += [slot(own, k) for k in range(4)]
    return pl.pallas_call(
        body, name=name,
        grid_spec=pltpu.PrefetchScalarGridSpec(
            num_scalar_prefetch=1, grid=(ROW_STEPS,),
            in_specs=in_specs,
            out_specs=[pl.BlockSpec((a.shape[1] // ROW_STEPS, a.shape[2]), lambda r, chip: (r, 0)) for a in owns]),
        out_shape=[jax.ShapeDtypeStruct(a.shape[1:], F32) for a in owns],
        compiler_params=_params(1),
    )(chip, *operands)


def _adamw_halves(core, owns, siblings, ws, ms, vs, name):
    n = len(ws)

    def body(c_ref, *refs):
        ins, outs = refs[:5 * n], refs[5 * n:]
        mine = c_ref[0] == pl.program_id(0)
        for k in range(n):
            own_ref, sib_ref, w_ref, m_ref, v_ref = ins[5 * k:5 * k + 5]
            g_ref, d_ref, mo_ref, vo_ref = outs[4 * k:4 * k + 4]

            @pl.when(mine)
            def _():
                g_ref[...] = own_ref[...]

            @pl.when(jnp.logical_not(mine))
            def _():
                g_ref[...] = sib_ref[...]

            d_ref[...], mo_ref[...], vo_ref[...] = _adamw_update(g_ref[...], w_ref[...], m_ref[...], v_ref[...])

    operands, in_specs, out_specs = [], [], []
    for own, sib, w, m, v in zip(owns, siblings, ws, ms, vs):
        rows, cols = w.shape
        rb = rows // 2 // ROW_STEPS
        half = pl.BlockSpec((rb, cols), lambda h, r, c: (r, 0))
        whole = pl.BlockSpec((rb, cols), lambda h, r, c: (h * ROW_STEPS + r, 0))
        operands += [own, sib, w, m, v]
        in_specs += [half, half, whole, whole, whole]
        out_specs += [whole] * 4
    outs = pl.pallas_call(
        body, name=name,
        grid_spec=pltpu.PrefetchScalarGridSpec(
            num_scalar_prefetch=1, grid=(2, ROW_STEPS), in_specs=in_specs, out_specs=out_specs),
        out_shape=[jax.ShapeDtypeStruct(w.shape, F32) for w in ws for _ in range(4)],
        compiler_params=_params(2),
    )(core, *operands)
    return [outs[4 * k:4 * k + 4] for k in range(n)]


def _place():
    x, y, c = lax.axis_index("x"), lax.axis_index("y"), lax.axis_index("c")
    chips = [(1 - x, y), (x, 1 - y), (1 - x, 1 - y)]
    return x, y, c, chips


def _remote(src, dst, send_sem, recv_sem, device):
    return pltpu.make_async_remote_copy(src_ref=src, dst_ref=dst, send_sem=send_sem, recv_sem=recv_sem,
                                        device_id=device, device_id_type=MESH)


def _split_start(name, arrays, n_copies, plan):
    n = len(arrays)

    def body(*refs):
        ins, send_sem, recv_sem, token = refs[:n], refs[n], refs[n + 1], refs[-1]
        for k, (src, dst, device) in enumerate(plan(ins)):
            _remote(src, dst, send_sem.at[k], recv_sem.at[k], device).start()
        token[...] = jnp.zeros_like(token)

    outs = pl.pallas_call(
        body, name=name,
        in_specs=[HBM] * n,
        out_specs=(SEM, SEM, *[HBM] * n, pl.BlockSpec(memory_space=pltpu.VMEM)),
        out_shape=(pltpu.SemaphoreType.DMA((n_copies,)), pltpu.SemaphoreType.DMA((n_copies,)),
                   *[pltpu.HBM(a.shape, a.dtype) for a in arrays], jax.ShapeDtypeStruct((8, LANES), F32)),
        input_output_aliases={i: i + 2 for i in range(n)},
        compiler_params=pltpu.CompilerParams(has_side_effects=EFFECT),
    )(*[pltpu.with_memory_space_constraint(a, pltpu.HBM) for a in arrays])
    return outs[0], outs[1], list(outs[2:2 + n]), outs[-1]


def _split_wait(name, send_sem, recv_sem, arrays, plan, first, *after):
    n = len(arrays)

    def body(*refs):
        ins, send_ref, recv_ref = refs[:n], refs[n], refs[n + 1]
        for k, (src, dst, device) in enumerate(plan(ins)):
            cp = _remote(src, dst, send_ref.at[first + k], recv_ref.at[first + k], device)
            cp.wait_send()
            cp.wait_recv()

    outs = pl.pallas_call(
        body, name=name,
        in_specs=[HBM] * n + [SEM, SEM] + [ANY] * len(after), out_specs=[HBM] * n,
        out_shape=[pltpu.HBM(a.shape, a.dtype) for a in arrays],
        input_output_aliases={i: i for i in range(n)},
        compiler_params=pltpu.CompilerParams(has_side_effects=EFFECT),
    )(*arrays, send_sem, recv_sem, *after)
    return list(outs)


def _plans(*parts):
    def plan(refs):
        copies, lo = [], 0
        for part, n in parts:
            copies += part(refs[lo:lo + n])
            lo += n
        return copies
    return plan


def _plan_gather_send(n):
    def plan(refs):
        x, y, c, chips = _place()
        me = 2 * x + y
        copies = []
        for w in range(n):
            own, full = refs[w], refs[n + w]
            copies.append((own, full.at[me], (x, y, 1 - c)))
            copies += [(own.at[c], full.at[me, c], (px, py, c)) for px, py in chips]
        return copies
    return plan


def _plan_gather_direct(n):
    def plan(refs):
        x, y, c, chips = _place()
        me = 2 * x + y
        copies = []
        for w in range(n):
            own, full = refs[w], refs[n + w]
            copies.append((own, full.at[me], (x, y, 1 - c)))
            for px, py in chips:
                copies += [(own.at[c], full.at[me, c], (px, py, c)), (own.at[c], full.at[me, c], (px, py, 1 - c))]
        return copies
    return plan


def _plan_gather_pass(n):
    def plan(refs):
        x, y, c, chips = _place()
        copies = []
        for w in range(n):
            for px, py in chips:
                landed = refs[w].at[2 * px + py, c]
                copies.append((landed, landed, (x, y, 1 - c)))
        return copies
    return plan


def _plan_swap(n, n_wholes):
    def plan(refs):
        x, y, c, _ = _place()
        n_all = n + n_wholes
        return [(refs[w].at[1 - c] if w < n else refs[w], refs[n_all + w], (x, y, 1 - c)) for w in range(n_all)]
    return plan


def _plan_share(n):
    def plan(refs):
        x, y, c, _ = _place()
        return [(refs[w], refs[n + w], (x, y, 1 - c)) for w in range(n)]
    return plan


def _plan_exchange(n, with_small):
    def plan(refs):
        x, y, c, chips = _place()
        me = 2 * x + y
        n_in = n + int(with_small)
        copies = []
        for w in range(n):
            copies += [(refs[w].at[2 * px + py], refs[n_in + w].at[me], (px, py, c)) for px, py in chips]
        if with_small:
            small, slots = refs[n], refs[n_in + n]
            copies.append((small, slots.at[me], (x, y, 1 - c)))
            copies += [(small, slots.at[me], (px, py, c)) for px, py in chips]
        return copies
    return plan


def _share_and_all_reduce(wholes, part):
    n = len(wholes)
    n_dev = 8

    def body(*refs):
        ins, part_ref, outs, sum_ref = refs[:n], refs[n], refs[n + 1:2 * n + 1], refs[2 * n + 1]
        slots_ref, send_sem, recv_sem, send_tile, recv_tile = refs[2 * n + 2:]
        x, y, c, _ = _place()
        me = 4 * x + 2 * y + c
        copies = []
        for w in range(n):
            copies.append(_remote(ins[w], outs[w], send_sem.at[w], recv_sem.at[w], (x, y, 1 - c)))
        slots_ref[me] = part_ref[...]
        flips = [(fx, fy, fc) for fx in (0, 1) for fy in (0, 1) for fc in (0, 1)][1:]
        for k, (fx, fy, fc) in enumerate(flips):
            peer = (x + fx - 2 * x * fx, y + fy - 2 * y * fy, c + fc - 2 * c * fc)
            copies.append(_remote(part_ref, slots_ref.at[me], send_tile.at[k], recv_tile.at[k], peer))
        for cp in copies:
            cp.start()
        for cp in copies:
            cp.wait()
        acc = slots_ref[0]
        for k in range(1, n_dev):
            acc = acc + slots_ref[k]
        sum_ref[...] = acc

    vmem = pl.BlockSpec(memory_space=pltpu.VMEM)
    outs = pl.pallas_call(
        body, name="rs_share_late",
        in_specs=[ANY] * n + [vmem], out_specs=[ANY] * n + [vmem],
        out_shape=[jax.ShapeDtypeStruct(a.shape, a.dtype) for a in wholes] + [jax.ShapeDtypeStruct(part.shape, F32)],
        scratch_shapes=[pltpu.VMEM((n_dev,) + part.shape, F32),
                        pltpu.SemaphoreType.DMA((n,)), pltpu.SemaphoreType.DMA((n,)),
                        pltpu.SemaphoreType.DMA((n_dev - 1,)), pltpu.SemaphoreType.DMA((n_dev - 1,))],
    )(*wholes, part)
    return outs[:n], outs[n]


_SMALL = ("norm1_pre_g", "b_in", "pool_scale", "sgu_ln_g", "sgu_ln_b", "w_spatial",
          "norm1_post_g", "norm2_pre_g", "norm2_post_g", "b_spatial")
_MIX = ("w_in", "w_pool", "w_sgu_proj", "w_out")
_FF = ("w_ff1", "w_ff2")
_BIG = _MIX + _FF
_ORDER = ("norm1_pre_g", "w_in", "b_in", "w_pool", "pool_scale", "sgu_ln_g", "sgu_ln_b", "w_spatial", "b_spatial",
          "w_sgu_proj", "w_out", "norm1_post_g", "norm2_pre_g", "w_ff1", "w_ff2", "norm2_post_g")


def _pack(keys, rows, parts):
    flat = jnp.concatenate([parts[k].reshape(-1).astype(F32) for k in keys])
    flat = jnp.pad(flat, (0, rows * LANES - flat.shape[0]))
    return flat.reshape(rows, LANES)


def _halves(a):
    return a.reshape(2, a.shape[0] // 2, a.shape[1])


def _step(p, m, v, x, target):
    t = x.shape[0]
    tm_mm = min(1024, t)
    tm_mix = min(256, t)
    tm_dx = min(512, t)
    core = lax.axis_index("c").astype(jnp.int32).reshape(1)
    row = lambda a: a.reshape(1, -1)

    chip = (2 * lax.axis_index("x") + lax.axis_index("y")).astype(jnp.int32).reshape(1)

    local2d = {"w_in": p["w_in"], "w_ff1": p["w_ff1"], "w_ff2": p["w_ff2"], "w_out": p["w_out"],
               "w_pool": p["w_pool"].reshape(N_HEADS * 64, POOL_GROUP),
               "w_sgu_proj": p["w_sgu_proj"].reshape(N_HEADS * 64, HEAD)}
    shard = {k: _halves(local2d[k].astype(BF16)) for k in _BIG}
    landing = lambda keys: [lax.empty((N_SHARD,) + shard[k].shape, BF16) for k in keys]
    in_arrays = [shard["w_in"]] + landing(["w_in"])
    send_a, recv_a, in_arrays, token = _split_start("ag_in_send", in_arrays, 4, _plan_gather_send(1))
    z, xn = _inproj_own(x, row(p["norm1_pre_g"]), in_arrays[0].reshape(D, D_IN // N_SHARD), row(p["b_in"]), chip,
                        token, tm_mm)
    mix_keys = _MIX[1:]
    n_mix, n_ff = len(mix_keys), len(_FF)
    in_arrays = _split_wait("ag_in_land", send_a, recv_a, in_arrays, _plan_gather_send(1), 0, z,
                            *[shard[k] for k in mix_keys + _FF])
    rest = in_arrays[1:] + [shard[k] for k in mix_keys] + landing(mix_keys) + [shard[k] for k in _FF] + landing(_FF)
    rest_plan = _plans((_plan_gather_pass(1), 1), (_plan_gather_direct(n_mix), 2 * n_mix),
                       (_plan_gather_send(n_ff), 2 * n_ff))
    send0, recv0, rest, token = _split_start("ag_rest_send", rest, 3 + 7 * n_mix + 4 * n_ff, rest_plan)
    in_full, mix_arrays, ff_arrays = rest[:1], rest[1:1 + 2 * n_mix], rest[1 + 2 * n_mix:]
    in_full = _split_wait("ag_in_passed", send0, recv0, in_full, _plan_gather_pass(1), 0, token)
    w_in4 = in_full[0].reshape(N_SHARD, D, D_IN // N_SHARD)

    pos = jnp.arange(SGU_BLOCK) // 64
    mask = (pos[:, None] >= pos[None, :]).astype(F32)
    ws = (p["w_spatial"] * mask[None]).astype(BF16)
    ws_t = jnp.swapaxes(ws, 1, 2)
    bsp = p["b_spatial"].reshape(N_HEADS, SGU_BLOCK, 1)

    z = _inproj_rest(xn, w_in4, row(p["b_in"]), chip, z, tm_mm)
    mix_arrays = _split_wait("ag_mix_land", send0, recv0, mix_arrays, _plan_gather_direct(n_mix), 3, z)
    full = dict(zip(mix_keys, mix_arrays[len(mix_keys):]))
    w_out = full["w_out"].reshape(D, D)
    regroup = lambda a: a.reshape(N_SHARD, N_HEADS, 64, 256).transpose(1, 0, 2, 3).reshape(N_HEADS, 256, 256)
    w_pool = regroup(full["w_pool"])
    w_proj = regroup(full["w_sgu_proj"])
    pooled, merged, y, h1 = _mix_fwd(z, x, w_pool, row(p["pool_scale"]), row(p["sgu_ln_g"]), row(p["sgu_ln_b"]),
                                     ws, bsp, w_proj, w_out, row(p["norm1_post_g"]), tm_mix)
    ff_arrays = _split_wait("ag_ff_land", send0, recv0, ff_arrays, _plan_gather_send(n_ff), 3 + 7 * n_mix, pooled)
    send2, recv2, ff_full, token = _split_start("ag_ff_pass", ff_arrays[n_ff:], 3 * n_ff, _plan_gather_pass(n_ff))
    ff_full = _split_wait("ag_ff_passed", send2, recv2, ff_full, _plan_gather_pass(n_ff), 0, token)
    w_ff1_4 = ff_full[0].reshape(N_SHARD, D, D)
    w_ff2 = ff_full[1].reshape(N_SHARD, D, D)
    hn, f1, f, df2, dout, vec_a = _ffn_fwd(h1, row(p["norm2_pre_g"]), w_ff1_4, w_ff2, row(p["norm2_post_g"]),
                                           target, tm_mix)

    tk = min(2048, t)
    g_big = {"w_ff2": _dw(f, df2, "dw_ff2", 2, D, tk, N_SHARD, lambda i, j: i)}
    df1, dh1, dy, vec_b = _ffn_bwd(df2, f1, w_ff1_4, w_ff2, h1, dout, y, row(p["norm2_pre_g"]),
                                   row(p["norm1_post_g"]), tm_mix)
    g_big["w_ff1"] = _dw(hn, df1, "dw_ff1", 2, D, tk, N_SHARD, lambda i, j: j)

    def reduce_start(name, keys, wholes, meanwhile, also=(), also_plan=None, also_copies=0):
        n_k, n_all = len(keys), len(keys) + len(wholes)
        swap_plan = _plan_swap(n_k, len(wholes))
        crossing = [g_big[k] for k in keys] + list(wholes)
        crossing += [lax.empty(a.shape[1:] if w < n_k else a.shape, a.dtype) for w, a in enumerate(crossing)]
        send, recv, crossing, tok = _split_start("rs_swap_" + name + "_send", crossing, n_all, swap_plan)
        crossing = _split_wait("rs_swap_" + name + "_land", send, recv, crossing, swap_plan, 0, meanwhile(tok))
        mine, wholes, swapped = crossing[:n_k], crossing[n_k:n_all], crossing[n_all:]
        sums = _chip_sums(core, mine, swapped[:n_k], "chip_sums_" + name)
        arrays = list(sums)
        if wholes:
            arrays.append(_slot_sum(jnp.stack([wholes[0], swapped[-1]]), "chip_sum_small"))
        arrays += [lax.empty(s.shape, BF16) for s in sums]
        if wholes:
            arrays.append(lax.empty((N_SHARD,) + wholes[0].shape, F32))
        plan = _plan_exchange(len(keys), bool(wholes))
        n_copies, n = 3 * len(keys) + 4 * len(wholes), len(arrays)
        also = also() if callable(also) else also
        both = plan if also_plan is None else _plans((plan, n), (also_plan, len(also)))
        send, recv, out, tok = _split_start("rs_send_" + name, arrays + list(also), n_copies + also_copies, both)
        return send, recv, out[:n], plan, tok, out[n:], n_copies

    def reduce_finish(name, keys, state, *after):
        send, recv, arrays, plan = state[:4]
        arrays = _split_wait("rs_land_" + name, send, recv, arrays, plan, 0, *after)
        n_in = len(arrays) // 2
        n = len(keys)
        finished = _final_sums(chip, arrays[:n], arrays[n_in:n_in + n], "final_sums_" + name)
        return dict(zip(keys, finished)), arrays

    early = _FF
    late = ("w_in", "w_pool", "w_sgu_proj", "w_out")

    def dw_out(token):
        g_big["w_out"] = _dw(merged, dy, "dw_out", 2 * N_SHARD, D, tk, N_SHARD, lambda i, j: 0, token)
        return g_big["w_out"]

    early_state = reduce_start("early", early, [], dw_out)
    dz, g_big["w_pool"], g_big["w_sgu_proj"], dws, dbsp, vec_c, dbin = _mix_bwd(
        z, pooled, dy, w_pool, row(p["pool_scale"]), row(p["sgu_ln_g"]), row(p["sgu_ln_b"]), ws, ws_t, bsp, w_proj,
        w_out, mask, tm_mix, early_state[4])
    g_big["w_in"] = _dw(xn, dz, "dw_in", 2, D_IN // N_SHARD, tk, N_SHARD, lambda i, j: j)
    small_part = _pack(_SMALL[1:], SMALL_ROWS - 8, {
        "b_in": dbin[0], "pool_scale": vec_c[0], "sgu_ln_g": vec_c[1], "sgu_ln_b": vec_c[2],
        "w_spatial": dws, "b_spatial": dbsp[:, ::LANES].T, "norm1_post_g": vec_b[1], "norm2_pre_g": vec_b[0],
        "norm2_post_g": vec_a[0]})
    halves = {}

    def finish_early(token):
        halves.update(reduce_finish("early", early, early_state, token)[0])
        return halves[early[-1]]

    share_early = lambda: [halves[k] for k in early] + [lax.empty(halves[k].shape, F32) for k in early]
    late_state = reduce_start("late", late, [small_part], finish_early, share_early, _plan_share(len(early)),
                              len(early))
    grad_x, vec_d = _dx(dz, w_in4, x, dh1, row(p["norm1_pre_g"]), late_state[4], tm_dx)
    share = _split_wait("rs_share_early_land", late_state[0], late_state[1], late_state[5], _plan_share(len(early)),
                        late_state[6], grad_x)

    grads, delta, new_m, new_v = {}, {}, {}, {}

    def update(name, keys, own, shared):
        as2d = lambda a, k: a.reshape(local2d[k].shape)
        outs = _adamw_halves(core, own, shared, [local2d[k] for k in keys],
                             [as2d(m[k], k) for k in keys], [as2d(v[k], k) for k in keys], "adamw_" + name)
        for k, outs_k in zip(keys, outs):
            grads[k], delta[k], new_m[k], new_v[k] = (a.reshape(p[k].shape) for a in outs_k)

    update("early", early, share[:len(early)], share[len(early):])
    late_halves, late_arrays = reduce_finish("late", late, late_state, grad_x, *[new_v[k] for k in early])
    own_late = [late_halves[k] for k in late]
    shared_late, tiles = _share_and_all_reduce(
        own_late, jnp.concatenate([vec_d[0].reshape(8, LANES), jnp.broadcast_to(vec_a[1:2, :LANES], (8, LANES))]))
    g1_grad, loss = tiles[:8], tiles[8, 0]
    update("late", late, own_late, shared_late)
    as_rows = lambda d: [d[k].reshape(-1, LANES) for k in _SMALL]
    small_outs = _adamw_small([g1_grad, late_arrays[-1]], as_rows(p), as_rows(m), as_rows(v))
    for k, outs_k in zip(_SMALL, small_outs):
        grads[k], delta[k], new_m[k], new_v[k] = (a.reshape(p[k].shape) for a in outs_k)

    return (loss, grad_x, *[grads[k] for k in _ORDER], *[delta[k] for k in _ORDER],
            *[new_m[k] for k in _ORDER], *[new_v[k] for k in _ORDER])


def kernel(x, norm1_pre_g, w_in, b_in, w_pool, pool_scale, sgu_ln_g, sgu_ln_b, w_spatial, b_spatial, w_sgu_proj, w_out, norm1_post_g, norm2_pre_g, w_ff1, w_ff2, norm2_post_g, loss_target, m_norm1_pre_g, m_w_in, m_b_in, m_w_pool, m_pool_scale, m_sgu_ln_g, m_sgu_ln_b, m_w_spatial, m_b_spatial, m_w_sgu_proj, m_w_out, m_norm1_post_g, m_norm2_pre_g, m_w_ff1, m_w_ff2, m_norm2_post_g, v_norm1_pre_g, v_w_in, v_b_in, v_w_pool, v_pool_scale, v_sgu_ln_g, v_sgu_ln_b, v_w_spatial, v_b_spatial, v_w_sgu_proj, v_w_out, v_norm1_post_g, v_norm2_pre_g, v_w_ff1, v_w_ff2, v_norm2_post_g):
    p = dict(zip(_ORDER, (norm1_pre_g, w_in, b_in, w_pool, pool_scale, sgu_ln_g, sgu_ln_b, w_spatial, b_spatial,
                          w_sgu_proj, w_out, norm1_post_g, norm2_pre_g, w_ff1, w_ff2, norm2_post_g)))
    m = dict(zip(_ORDER, (m_norm1_pre_g, m_w_in, m_b_in, m_w_pool, m_pool_scale, m_sgu_ln_g, m_sgu_ln_b, m_w_spatial,
                          m_b_spatial, m_w_sgu_proj, m_w_out, m_norm1_post_g, m_norm2_pre_g, m_w_ff1, m_w_ff2,
                          m_norm2_post_g)))
    v = dict(zip(_ORDER, (v_norm1_pre_g, v_w_in, v_b_in, v_w_pool, v_pool_scale, v_sgu_ln_g, v_sgu_ln_b, v_w_spatial,
                          v_b_spatial, v_w_sgu_proj, v_w_out, v_norm1_post_g, v_norm2_pre_g, v_w_ff1, v_w_ff2,
                          v_norm2_post_g)))
    outs = _step(p, m, v, x[0], loss_target[0])
    return (outs[0], outs[1][None], *outs[2:])
```

```python
import math

import jax
import jax.numpy as jnp
from jax import lax
from jax.experimental import pallas as pl
from jax.experimental.pallas import tpu as pltpu

F32 = jnp.float32
BF16 = jnp.bfloat16

D = 1024
D_IN = 5 * D
D_FF = 4 * D
N_SHARD = 4
POOL_WINDOWS = (2, 4, 8, 16)
POOL_GROUP = 256
POOL_HALO = 16
SGU_BLOCK = 128
N_HEADS = 4
HEAD = 256
EPS = 1e-6

ADAM_LR = 0.001
ADAM_B1 = 0.9
ADAM_B2 = 0.999
ADAM_EPS = 1e-08
ADAM_WD = 0.01
ADAM_STEP = 10

V7X_VMEM_BYTES = 64 * 1024 * 1024
VMEM_LIMIT = V7X_VMEM_BYTES - 8 * 1024 * 1024
SMALL_ROWS = 616
LANES = 128

MESH = pl.DeviceIdType.MESH
ANY = pl.BlockSpec(memory_space=pl.ANY)
HBM = pl.BlockSpec(memory_space=pltpu.HBM)
SEM = pl.BlockSpec(memory_space=pltpu.SEMAPHORE)
EFFECT = pltpu.SideEffectType.DATAFLOW_SIDE_EFFECTING

NT_DIMS = (((1,), (1,)), ((), ()))
TN_DIMS = (((0,), (0,)), ((), ()))


def _params(n_axes):
    return pltpu.CompilerParams(dimension_semantics=("arbitrary",) * n_axes, vmem_limit_bytes=VMEM_LIMIT)


def _resident(shape, n_axes):
    zeros = (0,) * len(shape)
    index = (lambda i: zeros) if n_axes == 1 else (lambda i, j: zeros)
    return pl.BlockSpec(shape, index, pipeline_mode=pl.Buffered(1))


def _dot(a, b):
    return jnp.dot(a, b, preferred_element_type=F32)


def _dot_nt(a, b):
    return lax.dot_general(a, b, NT_DIMS, preferred_element_type=F32)


def _dot_tn(a, b):
    return lax.dot_general(a, b, TN_DIMS, preferred_element_type=F32)


def _mean(a):
    return jnp.mean(a, axis=-1, keepdims=True)


def _colsum(a):
    return jnp.sum(a, axis=0, keepdims=True)


_GELU_C0 = math.sqrt(2.0 / math.pi)
_GELU_C1 = 0.044715


def _gelu(a):
    t = jnp.tanh(a * (_GELU_C0 + (_GELU_C0 * _GELU_C1) * (a * a)))
    return a * (0.5 * t + 0.5)


def _gelu_and_grad(a):
    a2 = a * a
    t = jnp.tanh(a * (_GELU_C0 + (_GELU_C0 * _GELU_C1) * a2))
    cdf = 0.5 * t + 0.5
    val = a * cdf
    grad = cdf + val * (1.0 - cdf) * ((2.0 * _GELU_C0) + (6.0 * _GELU_C0 * _GELU_C1) * a2)
    return val, grad


def _sigmoid(a):
    return 0.5 * jnp.tanh(0.5 * a) + 0.5


def _rms_norm_bf16(a, g):
    return (a * lax.rsqrt(_mean(a * a) + EPS) * g).astype(BF16)


def _relu_sq_bf16(a):
    r = jnp.maximum(a, 0.0)
    return (r * r).astype(BF16)


def _window_sum(a, w, ahead):
    n = a.shape[0]
    step = 1
    while step < w:
        a = a + pltpu.roll(a, n - step if ahead else step, 0)
        step *= 2
    return a


def _rms_bwd(dn, n, r):
    return r * (dn - n * _mean(dn * n))


def _inproj_own(x, g1, w_own, b_in, chip, dep, tm):
    t = x.shape[0]
    cw = D_IN // N_SHARD

    def body(chip_ref, x_ref, g_ref, w_ref, b_ref, dep_ref, z_ref, xn_ref):
        xn = _rms_norm_bf16(x_ref[...], g_ref[...])
        xn_ref[...] = xn
        z_ref[...] = _dot(xn, w_ref[...]) + b_ref[...]

    return pl.pallas_call(
        body, name="inproj_own",
        grid_spec=pltpu.PrefetchScalarGridSpec(
            num_scalar_prefetch=1, grid=(t // tm,),
            in_specs=[pl.BlockSpec((tm, D), lambda i, chip: (i, 0)),
                      pl.BlockSpec((1, D), lambda i, chip: (0, 0)),
                      pl.BlockSpec((D, cw), lambda i, chip: (0, 0)),
                      pl.BlockSpec((1, cw), lambda i, chip: (0, chip[0])),
                      pl.BlockSpec(dep.shape, lambda i, chip: (0, 0))],
            out_specs=[pl.BlockSpec((tm, cw), lambda i, chip: (i, chip[0])),
                       pl.BlockSpec((tm, D), lambda i, chip: (i, 0))]),
        out_shape=[jax.ShapeDtypeStruct((t, D_IN), F32), jax.ShapeDtypeStruct((t, D), BF16)],
        compiler_params=_params(1),
    )(chip, x, g1, w_own, b_in, dep)


def _inproj_rest(xn, w_in4, b_in, chip, z, tm):
    t = xn.shape[0]
    cw = D_IN // N_SHARD
    shard_of = lambda s, chip: lax.rem(chip[0] + 1 + s, N_SHARD)

    def body(chip_ref, xn_ref, w_ref, b_ref, z_in_ref, z_ref):
        z_ref[...] = _dot(xn_ref[...], w_ref[shard_of(pl.program_id(1), chip_ref)]) + b_ref[...]

    return pl.pallas_call(
        body, name="inproj_rest",
        grid_spec=pltpu.PrefetchScalarGridSpec(
            num_scalar_prefetch=1, grid=(t // tm, N_SHARD - 1),
            in_specs=[pl.BlockSpec((tm, D), lambda i, s, chip: (i, 0)),
                      pl.BlockSpec((N_SHARD, D, cw), lambda i, s, chip: (0, 0, 0), pipeline_mode=pl.Buffered(1)),
                      pl.BlockSpec((1, cw), lambda i, s, chip: (0, shard_of(s, chip))),
                      ANY],
            out_specs=pl.BlockSpec((tm, cw), lambda i, s, chip: (i, shard_of(s, chip)))),
        out_shape=jax.ShapeDtypeStruct(z.shape, F32),
        input_output_aliases={4: 0},
        compiler_params=_params(2),
    )(chip, xn, w_in4, b_in, z)


def _sgu_forward(z_ref, lng_ref, lnb_ref, ws_ref, bsp_ref, sv_ref, tm, with_grad):
    if with_grad:
        u, du_dz = _gelu_and_grad(z_ref[:, D:2 * D])
        gv, dgv_dz = _gelu_and_grad(z_ref[:, 2 * D:3 * D])
    else:
        u = _gelu(z_ref[:, D:2 * D])
        gv = _gelu(z_ref[:, 2 * D:3 * D])
        du_dz = dgv_dz = None
    xc = gv - _mean(gv)
    rstd = lax.rsqrt(_mean(xc * xc) + EPS)
    vhat = xc * rstd
    vb = (vhat * lng_ref[...] + lnb_ref[...]).astype(BF16)
    for b in range(tm // SGU_BLOCK):
        rows = slice(b * SGU_BLOCK, (b + 1) * SGU_BLOCK)
        for h in range(N_HEADS):
            cols = slice(h * HEAD, (h + 1) * HEAD)
            sv_ref[rows, cols] = _dot(ws_ref[h], vb[rows, cols]) + bsp_ref[h]
    return u, du_dz, dgv_dz, vhat, rstd, vb


def _mix_fwd(z, x, w_pool, pool_scale, ln_g, ln_b, ws, bsp, w_proj, w_out, g1post, tm):
    t = x.shape[0]

    def body(z_ref, x_ref, wp_ref, psc_ref, lng_ref, lnb_ref, ws_ref, bsp_ref, wpr_ref, wo_ref, g1p_ref,
             pooled_ref, merged_ref, y_ref, h1_ref, win_ref, sv_ref):
        i = pl.program_id(0)

        @pl.when(i == 0)
        def _():
            win_ref[0:POOL_HALO, :] = jnp.zeros((POOL_HALO, D), F32)

        win_ref[POOL_HALO:POOL_HALO + tm, :] = z_ref[:, 0:D]
        pos1 = lax.broadcasted_iota(jnp.int32, (tm, 1), 0) + (i * tm + 1)
        a_parts = []
        for g, w in enumerate(POOL_WINDOWS):
            cols = slice(g * POOL_GROUP, (g + 1) * POOL_GROUP)
            ts = _window_sum(win_ref[:, cols], w, False)[POOL_HALO:POOL_HALO + tm, :]
            inv_cnt = 1.0 / jnp.minimum(pos1, w).astype(F32)
            pooled = (ts * inv_cnt - z_ref[:, cols]).astype(BF16)
            pooled_ref[:, cols] = pooled
            a_parts.append(_dot(pooled, wp_ref[g]) * psc_ref[:, cols])
        win_ref[0:POOL_HALO, :] = win_ref[tm:tm + POOL_HALO, :]

        u, _, _, _, _, _ = _sgu_forward(z_ref, lng_ref, lnb_ref, ws_ref, bsp_ref, sv_ref, tm, False)
        gated = (u * sv_ref[...]).astype(BF16)
        for h in range(N_HEADS):
            lo = h * HEAD
            bbr = _dot(gated[:, lo:lo + HEAD], wpr_ref[h])
            gate_a = _sigmoid(z_ref[:, 3 * D + lo:3 * D + lo + HEAD])
            gate_b = _sigmoid(z_ref[:, 4 * D + lo:4 * D + lo + HEAD])
            merged_ref[:, lo:lo + HEAD] = (gate_a * a_parts[h] + gate_b * bbr).astype(BF16)
        merged = merged_ref[...]
        y = _dot(merged, wo_ref[...])
        y_ref[...] = y
        r = lax.rsqrt(_mean(y * y) + EPS)
        h1_ref[...] = x_ref[...] + (y * r) * g1p_ref[...]

    row = lambda i: (i, 0)
    fixed2 = lambda i: (0, 0)
    fixed3 = lambda i: (0, 0, 0)
    vec = pl.BlockSpec((1, D), fixed2)
    return pl.pallas_call(
        body, name="mix_fwd",
        grid=(t // tm,),
        in_specs=[pl.BlockSpec((tm, D_IN), row), pl.BlockSpec((tm, D), row),
                  pl.BlockSpec((N_HEADS, POOL_GROUP, POOL_GROUP), fixed3), vec, vec, vec,
                  pl.BlockSpec((N_HEADS, SGU_BLOCK, SGU_BLOCK), fixed3),
                  pl.BlockSpec((N_HEADS, SGU_BLOCK, 1), fixed3),
                  pl.BlockSpec((N_HEADS, HEAD, HEAD), fixed3),
                  pl.BlockSpec((D, D), fixed2), vec],
        out_specs=[pl.BlockSpec((tm, D), row)] * 4,
        out_shape=[jax.ShapeDtypeStruct((t, D), BF16), jax.ShapeDtypeStruct((t, D), BF16),
                   jax.ShapeDtypeStruct((t, D), F32), jax.ShapeDtypeStruct((t, D), F32)],
        scratch_shapes=[pltpu.VMEM((tm + POOL_HALO, D), F32), pltpu.VMEM((tm, D), F32)],
        compiler_params=_params(1),
    )(z, x, w_pool, pool_scale, ln_g, ln_b, ws, bsp, w_proj, w_out, g1post)


def _ffn_fwd(h1, g2pre, w1, w2, g2post, target, tm):
    t = h1.shape[0]
    n_j = D_FF // D

    def body(h1_ref, g2_ref, w1_ref, w2_ref, g2p_ref, tgt_ref,
             hn_ref, f1_ref, f_ref, df2_ref, dout_ref, vec_ref):
        @pl.when(pl.program_id(0) == 0)
        def _():
            vec_ref[...] = jnp.zeros_like(vec_ref)

        hn = _rms_norm_bf16(h1_ref[...], g2_ref[...])
        hn_ref[...] = hn
        f2 = None
        for j in range(n_j):
            cols = slice(j * D, (j + 1) * D)
            f1 = _dot(hn, w1_ref[j])
            f1_ref[:, cols] = f1
            f = _relu_sq_bf16(f1)
            f_ref[:, cols] = f
            part = _dot(f, w2_ref[j])
            f2 = part if f2 is None else f2 + part
        r = lax.rsqrt(_mean(f2 * f2) + EPS)
        n2 = f2 * r
        err = h1_ref[...] + n2 * g2p_ref[...] - tgt_ref[...]
        loss = 0.5 * jnp.sum(_mean(err * err))
        dout = err * (1.0 / D)
        dout_ref[...] = dout
        vec_ref[0:1, :] += _colsum(dout * n2)
        vec_ref[1:2, :] += jnp.full((1, D), loss, F32)
        df2_ref[...] = _rms_bwd(dout * g2p_ref[...], n2, r).astype(BF16)

    row = lambda i: (i, 0)
    fixed = lambda i: (0, 0)
    vec = pl.BlockSpec((1, D), fixed)
    return pl.pallas_call(
        body, name="ffn_fwd",
        grid=(t // tm,),
        in_specs=[pl.BlockSpec((tm, D), row), vec,
                  _resident((n_j, D, D), 1), _resident((n_j, D, D), 1), vec,
                  pl.BlockSpec((tm, D), row)],
        out_specs=[pl.BlockSpec((tm, D), row),
                   pl.BlockSpec((tm, D_FF), row), pl.BlockSpec((tm, D_FF), row),
                   pl.BlockSpec((tm, D), row), pl.BlockSpec((tm, D), row),
                   pl.BlockSpec((8, D), fixed)],
        out_shape=[jax.ShapeDtypeStruct((t, D), BF16), jax.ShapeDtypeStruct((t, D_FF), F32),
                   jax.ShapeDtypeStruct((t, D_FF), BF16), jax.ShapeDtypeStruct((t, D), BF16),
                   jax.ShapeDtypeStruct((t, D), F32), jax.ShapeDtypeStruct((8, D), F32)],
        compiler_params=_params(1),
    )(h1, g2pre, w1, w2, g2post, target)


def _ffn_bwd(df2, f1, w1, w2, h1, dout, y, g2pre, g1post, tm):
    t = h1.shape[0]
    n_j = D_FF // D

    def body(df2_ref, f1_ref, w1_ref, w2_ref, h1_ref, dout_ref, y_ref, g2_ref, g1p_ref,
             df1_ref, dh1_ref, dy_ref, vec_ref):
        @pl.when(pl.program_id(0) == 0)
        def _():
            vec_ref[...] = jnp.zeros_like(vec_ref)

        df2 = df2_ref[...]
        dhn = None
        for j in range(n_j):
            cols = slice(j * D, (j + 1) * D)
            df = _dot_nt(df2, w2_ref[j])
            df1 = (df * (2.0 * jnp.maximum(f1_ref[:, cols], 0.0))).astype(BF16)
            df1_ref[:, cols] = df1
            part = _dot_nt(df1, w1_ref[j])
            dhn = part if dhn is None else dhn + part
        h = h1_ref[...]
        r2 = lax.rsqrt(_mean(h * h) + EPS)
        nh = h * r2
        vec_ref[0:1, :] += _colsum(dhn * nh)
        dh1 = dout_ref[...] + _rms_bwd(dhn * g2_ref[...], nh, r2)
        dh1_ref[...] = dh1
        yv = y_ref[...]
        r1 = lax.rsqrt(_mean(yv * yv) + EPS)
        ny = yv * r1
        vec_ref[1:2, :] += _colsum(dh1 * ny)
        dy_ref[...] = _rms_bwd(dh1 * g1p_ref[...], ny, r1).astype(BF16)

    row = lambda i: (i, 0)
    fixed = lambda i: (0, 0)
    vec = pl.BlockSpec((1, D), fixed)
    return pl.pallas_call(
        body, name="ffn_bwd",
        grid=(t // tm,),
        in_specs=[pl.BlockSpec((tm, D), row), pl.BlockSpec((tm, D_FF), row),
                  _resident((n_j, D, D), 1), _resident((n_j, D, D), 1),
                  pl.BlockSpec((tm, D), row), pl.BlockSpec((tm, D), row), pl.BlockSpec((tm, D), row), vec, vec],
        out_specs=[pl.BlockSpec((tm, D_FF), row),
                   pl.BlockSpec((tm, D), row), pl.BlockSpec((tm, D), row),
                   pl.BlockSpec((8, D), fixed)],
        out_shape=[jax.ShapeDtypeStruct((t, D_FF), BF16), jax.ShapeDtypeStruct((t, D), F32),
                   jax.ShapeDtypeStruct((t, D), BF16), jax.ShapeDtypeStruct((8, D), F32)],
        compiler_params=_params(1),
    )(df2, f1, w1, w2, h1, dout, y, g2pre, g1post)


def _mix_bwd(z, pooled, dy, w_pool, pool_scale, ln_g, ln_b, ws, ws_t, bsp, w_proj, w_out, mask, tm, dep):
    t = z.shape[0]
    n_t = t // tm
    shard_rows = POOL_GROUP // N_SHARD

    def body(z_ref, pooled_ref, dy_ref, wp_ref, psc_ref, lng_ref, lnb_ref, ws_ref, wst_ref, bsp_ref, wpr_ref,
             wo_ref, mask_ref, dep_ref,
             dz_ref, dwp_ref, dwpr_ref, dws_ref, dbsp_ref, vec_ref, dbin_ref,
             win_ref, sv_ref, dsv_ref, dv_ref, dsvsum_ref, dwp_acc, dwpr_acc):
        i = pl.program_id(0)

        @pl.when(i == 0)
        def _():
            win_ref[tm:tm + POOL_HALO, :] = jnp.zeros((POOL_HALO, D), F32)
            dwp_acc[...] = jnp.zeros_like(dwp_acc)
            dwpr_acc[...] = jnp.zeros_like(dwpr_acc)
            dws_ref[...] = jnp.zeros_like(dws_ref)
            vec_ref[...] = jnp.zeros_like(vec_ref)
            dbin_ref[...] = jnp.zeros_like(dbin_ref)
            dsvsum_ref[...] = jnp.zeros_like(dsvsum_ref)

        dmerged = _dot_nt(dy_ref[...], wo_ref[...])

        u, du_dz, dgv_dz, vhat, rstd, vb = _sgu_forward(z_ref, lng_ref, lnb_ref, ws_ref, bsp_ref, sv_ref, tm, True)
        sv = sv_ref[...]
        gated = (u * sv).astype(BF16)

        sa = _sigmoid(z_ref[:, 3 * D:4 * D])
        sb = _sigmoid(z_ref[:, 4 * D:5 * D])
        da = dmerged * sa
        db = dmerged * sb
        dbbr = db.astype(BF16)

        tile = n_t - 1 - i
        pos1 = lax.broadcasted_iota(jnp.int32, (tm, 1), 0) + (tile * tm + 1)
        def put(k, lo, part):
            cols = slice(k * D + lo, k * D + lo + part.shape[1])
            dz_ref[:, cols] = part.astype(BF16)
            dbin_ref[0:1, cols] += _colsum(part)

        dpooled_parts = []
        for g, w in enumerate(POOL_WINDOWS):
            cols = slice(g * POOL_GROUP, (g + 1) * POOL_GROUP)
            pg = pooled_ref[:, cols]
            a_pre = _dot(pg, wp_ref[g])
            da_g = da[:, cols]
            psc = psc_ref[:, cols]
            da_a = da_g * a_pre
            vec_ref[0:1, cols] += _colsum(da_a)
            put(3, g * POOL_GROUP, da_a * psc * (1.0 - sa[:, cols]))
            da_pre = (da_g * psc).astype(BF16)
            dwp_acc[g] += _dot_tn(pg, da_pre)
            dpooled = _dot_nt(da_pre, wp_ref[g])
            dpooled_parts.append(dpooled)
            inv_cnt = 1.0 / jnp.minimum(pos1, w).astype(F32)
            win_ref[0:tm, cols] = dpooled * inv_cnt
        for g, w in enumerate(POOL_WINDOWS):
            cols = slice(g * POOL_GROUP, (g + 1) * POOL_GROUP)
            acc = _window_sum(win_ref[:, cols], w, True)[0:tm, :]
            put(0, g * POOL_GROUP, acc - dpooled_parts[g])
        win_ref[tm:tm + POOL_HALO, :] = win_ref[0:POOL_HALO, :]

        for h in range(N_HEADS):
            cols = slice(h * HEAD, (h + 1) * HEAD)
            g_h = gated[:, cols]
            db_h = dbbr[:, cols]
            bbr_h = _dot(g_h, wpr_ref[h])
            put(4, h * HEAD, db[:, cols] * bbr_h * (1.0 - sb[:, cols]))
            dwpr_acc[h] += _dot_tn(g_h, db_h)
            dgated_h = _dot_nt(db_h, wpr_ref[h])
            dsv_ref[:, cols] = dgated_h * u[:, cols]
            sv_ref[:, cols] = dgated_h * sv[:, cols]
        put(1, 0, sv_ref[...] * du_dz)
        dsv = dsv_ref[...]
        dsvb = dsv.astype(BF16)
        blk_sum = dsv[0:SGU_BLOCK, :]
        for b in range(1, tm // SGU_BLOCK):
            blk_sum = blk_sum + dsv[b * SGU_BLOCK:(b + 1) * SGU_BLOCK, :]
        dsvsum_ref[...] += blk_sum
        for b in range(tm // SGU_BLOCK):
            rows = slice(b * SGU_BLOCK, (b + 1) * SGU_BLOCK)
            for h in range(N_HEADS):
                cols = slice(h * HEAD, (h + 1) * HEAD)
                dws_ref[h] += _dot_nt(dsvb[rows, cols], vb[rows, cols])
                dv_ref[rows, cols] = _dot(wst_ref[h], dsvb[rows, cols])
        dv = dv_ref[...]
        vec_ref[1:2, :] += _colsum(dv * vhat)
        vec_ref[2:3, :] += _colsum(dv)
        dvhat = dv * lng_ref[...]
        dgv = rstd * (dvhat - _mean(dvhat) - vhat * _mean(dvhat * vhat))
        put(2, 0, dgv * dgv_dz)

        @pl.when(i == n_t - 1)
        def _():
            for h in range(N_HEADS):
                dws_ref[h] = jnp.where(mask_ref[...] > 0.0, dws_ref[h], 0.0)
                tot = jnp.sum(dsvsum_ref[:, h * HEAD:(h + 1) * HEAD], axis=1, keepdims=True)
                dbsp_ref[:, h * LANES:(h + 1) * LANES] = jnp.broadcast_to(tot, (SGU_BLOCK, LANES))
            for acc, out in ((dwp_acc, dwp_ref), (dwpr_acc, dwpr_ref)):
                for g in range(N_HEADS):
                    for s in range(N_SHARD):
                        out[g // 2, s, (g % 2) * shard_rows:(g % 2 + 1) * shard_rows, :] = (
                            acc[g, s * shard_rows:(s + 1) * shard_rows, :].astype(BF16))

    row = lambda i: (n_t - 1 - i, 0)
    fixed2 = lambda i: (0, 0)
    fixed3 = lambda i: (0, 0, 0)
    fixed4 = lambda i: (0, 0, 0, 0)
    vec = pl.BlockSpec((1, D), fixed2)
    sq = pl.BlockSpec((N_HEADS, SGU_BLOCK, SGU_BLOCK), fixed3)
    grp = pl.BlockSpec((N_HEADS, HEAD, HEAD), fixed3)
    grp_out = pl.BlockSpec((2, N_SHARD, 2 * shard_rows, HEAD), fixed4)
    tile = pl.BlockSpec((tm, D), row)
    return pl.pallas_call(
        body, name="mix_bwd",
        grid=(n_t,),
        in_specs=[pl.BlockSpec((tm, D_IN), row), tile, tile,
                  grp, vec, vec, vec, sq, sq, pl.BlockSpec((N_HEADS, SGU_BLOCK, 1), fixed3), grp,
                  pl.BlockSpec((D, D), fixed2), pl.BlockSpec((SGU_BLOCK, SGU_BLOCK), fixed2),
                  pl.BlockSpec(dep.shape, fixed2)],
        out_specs=[pl.BlockSpec((tm, D_IN), row), grp_out, grp_out, sq,
                   pl.BlockSpec((SGU_BLOCK, N_HEADS * LANES), fixed2),
                   pl.BlockSpec((8, D), fixed2), pl.BlockSpec((8, D_IN), fixed2)],
        out_shape=[jax.ShapeDtypeStruct((t, D_IN), BF16),
                   jax.ShapeDtypeStruct((2, N_SHARD, 2 * shard_rows, HEAD), BF16),
                   jax.ShapeDtypeStruct((2, N_SHARD, 2 * shard_rows, HEAD), BF16),
                   jax.ShapeDtypeStruct((N_HEADS, SGU_BLOCK, SGU_BLOCK), F32),
                   jax.ShapeDtypeStruct((SGU_BLOCK, N_HEADS * LANES), F32),
                   jax.ShapeDtypeStruct((8, D), F32), jax.ShapeDtypeStruct((8, D_IN), F32)],
        scratch_shapes=[pltpu.VMEM((tm + POOL_HALO, D), F32), pltpu.VMEM((tm, D), F32), pltpu.VMEM((tm, D), F32),
                        pltpu.VMEM((tm, D), F32), pltpu.VMEM((SGU_BLOCK, D), F32),
                        pltpu.VMEM((N_HEADS, HEAD, HEAD), F32), pltpu.VMEM((N_HEADS, HEAD, HEAD), F32)],
        compiler_params=_params(1),
    )(z, pooled, dy, w_pool, pool_scale, ln_g, ln_b, ws, ws_t, bsp, w_proj, w_out, mask, dep)


def _dx(dz, w_in4, x, dh1, g1, dep, tm):
    t = x.shape[0]
    cw = D_IN // N_SHARD

    def body(dz_ref, w_ref, x_ref, dh1_ref, g_ref, dep_ref, gx_ref, vec_ref):
        @pl.when(pl.program_id(0) == 0)
        def _():
            vec_ref[...] = jnp.zeros_like(vec_ref)

        dxn = _dot_nt(dz_ref[:, 0:cw], w_ref[0])
        for s in range(1, N_SHARD):
            dxn = dxn + _dot_nt(dz_ref[:, s * cw:(s + 1) * cw], w_ref[s])
        xv = x_ref[...]
        r = lax.rsqrt(_mean(xv * xv) + EPS)
        xh = xv * r
        vec_ref[0:1, :] += _colsum(dxn * xh)
        gx_ref[...] = dh1_ref[...] + _rms_bwd(dxn * g_ref[...], xh, r)

    row = lambda i: (i, 0)
    fixed = lambda i: (0, 0)
    return pl.pallas_call(
        body, name="dx",
        grid=(t // tm,),
        in_specs=[pl.BlockSpec((tm, D_IN), row),
                  _resident((N_SHARD, D, cw), 1),
                  pl.BlockSpec((tm, D), row), pl.BlockSpec((tm, D), row), pl.BlockSpec((1, D), fixed),
                  pl.BlockSpec(dep.shape, fixed)],
        out_specs=[pl.BlockSpec((tm, D), row), pl.BlockSpec((8, D), fixed)],
        out_shape=[jax.ShapeDtypeStruct((t, D), F32), jax.ShapeDtypeStruct((8, D), F32)],
        compiler_params=_params(1),
    )(dz, w_in4, x, dh1, g1, dep)


def _dw(a, b, name, n_split, tn, tk, n_out, out_block, dep=None):
    t, m = a.shape
    n = b.shape[1]
    tm = min(m, D)
    rows = tm // n_split
    n_k = t // tk
    deps = [] if dep is None else [dep]

    def body(a_ref, b_ref, *rest):
        o_ref, acc_ref = rest[-2:]
        k = pl.program_id(2)

        @pl.when(k == 0)
        def _():
            acc_ref[...] = jnp.zeros_like(acc_ref)

        acc_ref[...] += _dot_tn(a_ref[...], b_ref[...])

        @pl.when(k == n_k - 1)
        def _():
            for q in range(n_split):
                o_ref[q % 2, q // 2] = acc_ref[q * rows:(q + 1) * rows, :].astype(BF16)

    return pl.pallas_call(
        body, name=name,
        grid=(m // tm, n // tn, n_k),
        in_specs=[pl.BlockSpec((tk, tm), lambda i, j, k: (k, i)),
                  pl.BlockSpec((tk, tn), lambda i, j, k: (k, j))]
        + [pl.BlockSpec(d.shape, lambda i, j, k: (0, 0)) for d in deps],
        out_specs=pl.BlockSpec((2, n_split // 2, rows, tn), lambda i, j, k: (0, out_block(i, j), 0, 0)),
        out_shape=jax.ShapeDtypeStruct((2, n_out, rows, tn), BF16),
        scratch_shapes=[pltpu.VMEM((tm, tn), F32)],
        compiler_params=_params(3),
    )(a, b, *deps)


def _row_block(rows, cols, itemsize, n_bufs):
    budget = VMEM_LIMIT // 4
    rb = rows
    while rb % 16 == 0 and rb * cols * itemsize * n_bufs * 2 > budget:
        rb //= 2
    return rb


ROW_STEPS = 2


def _chip_sums(core, grads, recvs, name):
    n = len(grads)

    def body(c_ref, *refs):
        for w in range(n):
            refs[2 * n + w][...] = (refs[w][0].astype(F32) + refs[n + w][...].astype(F32)).astype(BF16)

    def blocks(g):
        _, _, rh, cols = g.shape
        rb = rh // ROW_STEPS
        return (pl.BlockSpec((1, 1, rb, cols), lambda s, r, c: (c[0], s, r, 0)),
                pl.BlockSpec((1, rb, cols), lambda s, r, c: (s, r, 0)))

    specs = [blocks(g) for g in grads]
    return pl.pallas_call(
        body, name=name,
        grid_spec=pltpu.PrefetchScalarGridSpec(
            num_scalar_prefetch=1, grid=(N_SHARD, ROW_STEPS),
            in_specs=[s[0] for s in specs] + [s[1] for s in specs],
            out_specs=[s[1] for s in specs]),
        out_shape=[jax.ShapeDtypeStruct(g.shape[1:], BF16) for g in grads],
        compiler_params=_params(2),
    )(core, *grads, *recvs)


def _slot_sum(slots, name):
    n_s, rows, cols = slots.shape
    rb = _row_block(rows, cols, 4, n_s + 1)

    def body(s_ref, o_ref):
        acc = s_ref[0].astype(F32)
        for k in range(1, n_s):
            acc = acc + s_ref[k].astype(F32)
        o_ref[...] = acc

    return pl.pallas_call(
        body, name=name,
        grid=(rows // rb,),
        in_specs=[pl.BlockSpec((n_s, rb, cols), lambda r: (0, r, 0))],
        out_specs=pl.BlockSpec((rb, cols), lambda r: (r, 0)),
        out_shape=jax.ShapeDtypeStruct((rows, cols), F32),
        compiler_params=_params(1),
    )(slots)


def _adamw_update(g, w, m, v):
    m_new = ADAM_B1 * m + (1.0 - ADAM_B1) * g
    v_new = ADAM_B2 * v + (1.0 - ADAM_B2) * (g * g)
    m_hat = m_new / (1.0 - ADAM_B1 ** ADAM_STEP)
    v_hat = v_new / (1.0 - ADAM_B2 ** ADAM_STEP)
    return -ADAM_LR * (m_hat / (jnp.sqrt(v_hat) + ADAM_EPS) + ADAM_WD * w), m_new, v_new


def _adamw_small(grad_parts, ws, ms, vs):
    n, n_g = len(ws), len(grad_parts)

    def rows_of(ref, lo, rows):
        if len(ref.shape) == 2:
            return ref[lo:lo + rows, :]
        acc = ref[0, lo:lo + rows, :]
        for s in range(1, ref.shape[0]):
            acc = acc + ref[s, lo:lo + rows, :]
        return acc

    def body(*refs):
        g_refs, ins, outs = refs[:n_g], refs[n_g:n_g + 3 * n], refs[n_g + 3 * n:]
        part, off = 0, 0
        for k in range(n):
            rows = ws[k].shape[0]
            if off + rows > grad_parts[part].shape[-2]:
                part, off = part + 1, 0
            g = rows_of(g_refs[part], off, rows)
            off += rows
            delta, m_new, v_new = _adamw_update(g, ins[k][...], ins[n + k][...], ins[2 * n + k][...])
            for o, val in zip(outs[4 * k:4 * k + 4], (g, delta, m_new, v_new)):
                o[...] = val

    vmem = pl.BlockSpec(memory_space=pltpu.VMEM)
    outs = pl.pallas_call(
        body, name="adamw_small",
        in_specs=[vmem] * (n_g + 3 * n), out_specs=[vmem] * (4 * n),
        out_shape=[jax.ShapeDtypeStruct(w.shape, F32) for w in ws for _ in range(4)],
    )(*grad_parts, *ws, *ms, *vs)
    return [outs[4 * k:4 * k + 4] for k in range(n)]


def _final_sums(chip, owns, slots, name):
    n = len(owns)

    def body(chip_ref, *refs):
        for w in range(n):
            own_ref, s1_ref, s2_ref, s3_ref = refs[4 * w:4 * w + 4]
            acc = own_ref[0].astype(F32) + s1_ref[0].astype(F32)
            acc = acc + s2_ref[0].astype(F32)
            refs[4 * n + w][...] = acc + s3_ref[0].astype(F32)

    def slot(a, k):
        n_s, rh, cols = a.shape
        return pl.BlockSpec((1, rh // ROW_STEPS, cols), lambda r, chip: (lax.rem(chip[0] + k, n_s), r, 0))

    operands, in_specs = [], []
    for own, slots_w in zip(owns, slots):
        operands += [own, slots_w, slots_w, slots_w]
        in_specs += [slot(own, k) for k in range(4)]
    return pl.pallas_call(
        body, name=name,
        grid_spec=pltpu.PrefetchScalarGridSpec(
            num_scalar_prefetch=1, grid=(ROW_STEPS,),
            in_specs=in_specs,
            out_specs=[pl.BlockSpec((a.shape[1] // ROW_STEPS, a.shape[2]), lambda r, chip: (r, 0)) for a in owns]),
        out_shape=[jax.ShapeDtypeStruct(a.shape[1:], F32) for a in owns],
        compiler_params=_params(1),
    )(chip, *operands)


def _adamw_halves(core, owns, siblings, ws, ms, vs, name):
    n = len(ws)

    def body(c_ref, *refs):
        ins, outs = refs[:5 * n], refs[5 * n:]
        mine = c_ref[0] == pl.program_id(0)
        for k in range(n):
            own_ref, sib_ref, w_ref, m_ref, v_ref = ins[5 * k:5 * k + 5]
            g_ref, d_ref, mo_ref, vo_ref = outs[4 * k:4 * k + 4]

            @pl.when(mine)
            def _():
                g_ref[...] = own_ref[...]

            @pl.when(jnp.logical_not(mine))
            def _():
                g_ref[...] = sib_ref[...]

            d_ref[...], mo_ref[...], vo_ref[...] = _adamw_update(g_ref[...], w_ref[...], m_ref[...], v_ref[...])

    operands, in_specs, out_specs = [], [], []
    for own, sib, w, m, v in zip(owns, siblings, ws, ms, vs):
        rows, cols = w.shape
        rb = rows // 2 // ROW_STEPS
        half = pl.BlockSpec((rb, cols), lambda h, r, c: (r, 0))
        whole = pl.BlockSpec((rb, cols), lambda h, r, c: (h * ROW_STEPS + r, 0))
        operands += [own, sib, w, m, v]
        in_specs += [half, half, whole, whole, whole]
        out_specs += [whole] * 4
    outs = pl.pallas_call(
        body, name=name,
        grid_spec=pltpu.PrefetchScalarGridSpec(
            num_scalar_prefetch=1, grid=(2, ROW_STEPS), in_specs=in_specs, out_specs=out_specs),
        out_shape=[jax.ShapeDtypeStruct(w.shape, F32) for w in ws for _ in range(4)],
        compiler_params=_params(2),
    )(core, *operands)
    return [outs[4 * k:4 * k + 4] for k in range(n)]


def _place():
    x, y, c = lax.axis_index("x"), lax.axis_index("y"), lax.axis_index("c")
    chips = [(1 - x, y), (x, 1 - y), (1 - x, 1 - y)]
    return x, y, c, chips


def _remote(src, dst, send_sem, recv_sem, device):
    return pltpu.make_async_remote_copy(src_ref=src, dst_ref=dst, send_sem=send_sem, recv_sem=recv_sem,
                                        device_id=device, device_id_type=MESH)


def _split_start(name, arrays, n_copies, plan):
    n = len(arrays)

    def body(*refs):
        ins, send_sem, recv_sem, token = refs[:n], refs[n], refs[n + 1], refs[-1]
        for k, (src, dst, device) in enumerate(plan(ins)):
            _remote(src, dst, send_sem.at[k], recv_sem.at[k], device).start()
        token[...] = jnp.zeros_like(token)

    outs = pl.pallas_call(
        body, name=name,
        in_specs=[HBM] * n,
        out_specs=(SEM, SEM, *[HBM] * n, pl.BlockSpec(memory_space=pltpu.VMEM)),
        out_shape=(pltpu.SemaphoreType.DMA((n_copies,)), pltpu.SemaphoreType.DMA((n_copies,)),
                   *[pltpu.HBM(a.shape, a.dtype) for a in arrays], jax.ShapeDtypeStruct((8, LANES), F32)),
        input_output_aliases={i: i + 2 for i in range(n)},
        compiler_params=pltpu.CompilerParams(has_side_effects=EFFECT),
    )(*[pltpu.with_memory_space_constraint(a, pltpu.HBM) for a in arrays])
    return outs[0], outs[1], list(outs[2:2 + n]), outs[-1]


def _split_wait(name, send_sem, recv_sem, arrays, plan, first, *after):
    n = len(arrays)

    def body(*refs):
        ins, send_ref, recv_ref = refs[:n], refs[n], refs[n + 1]
        for k, (src, dst, device) in enumerate(plan(ins)):
            cp = _remote(src, dst, send_ref.at[first + k], recv_ref.at[first + k], device)
            cp.wait_send()
            cp.wait_recv()

    outs = pl.pallas_call(
        body, name=name,
        in_specs=[HBM] * n + [SEM, SEM] + [ANY] * len(after), out_specs=[HBM] * n,
        out_shape=[pltpu.HBM(a.shape, a.dtype) for a in arrays],
        input_output_aliases={i: i for i in range(n)},
        compiler_params=pltpu.CompilerParams(has_side_effects=EFFECT),
    )(*arrays, send_sem, recv_sem, *after)
    return list(outs)


def _plans(*parts):
    def plan(refs):
        copies, lo = [], 0
        for part, n in parts:
            copies += part(refs[lo:lo + n])
            lo += n
        return copies
    return plan


def _plan_gather_send(n):
    def plan(refs):
        x, y, c, chips = _place()
        me = 2 * x + y
        copies = []
        for w in range(n):
            own, full = refs[w], refs[n + w]
            copies.append((own, full.at[me], (x, y, 1 - c)))
            copies += [(own.at[c], full.at[me, c], (px, py, c)) for px, py in chips]
        return copies
    return plan


def _plan_gather_direct(n):
    def plan(refs):
        x, y, c, chips = _place()
        me = 2 * x + y
        copies = []
        for w in range(n):
            own, full = refs[w], refs[n + w]
            copies.append((own, full.at[me], (x, y, 1 - c)))
            for px, py in chips:
                copies += [(own.at[c], full.at[me, c], (px, py, c)), (own.at[c], full.at[me, c], (px, py, 1 - c))]
        return copies
    return plan


def _plan_gather_pass(n):
    def plan(refs):
        x, y, c, chips = _place()
        copies = []
        for w in range(n):
            for px, py in chips:
                landed = refs[w].at[2 * px + py, c]
                copies.append((landed, landed, (x, y, 1 - c)))
        return copies
    return plan


def _plan_swap(n, n_wholes):
    def plan(refs):
        x, y, c, _ = _place()
        n_all = n + n_wholes
        return [(refs[w].at[1 - c] if w < n else refs[w], refs[n_all + w], (x, y, 1 - c)) for w in range(n_all)]
    return plan


def _plan_share(n):
    def plan(refs):
        x, y, c, _ = _place()
        return [(refs[w], refs[n + w], (x, y, 1 - c)) for w in range(n)]
    return plan


def _plan_exchange(n, with_small):
    def plan(refs):
        x, y, c, chips = _place()
        me = 2 * x + y
        n_in = n + int(with_small)
        copies = []
        for w in range(n):
            copies += [(refs[w].at[2 * px + py], refs[n_in + w].at[me], (px, py, c)) for px, py in chips]
        if with_small:
            small, slots = refs[n], refs[n_in + n]
            copies.append((small, slots.at[me], (x, y, 1 - c)))
            copies += [(small, slots.at[me], (px, py, c)) for px, py in chips]
        return copies
    return plan


def _share_and_all_reduce(wholes, part):
    n = len(wholes)
    n_dev = 8

    def body(*refs):
        ins, part_ref, outs, sum_ref = refs[:n], refs[n], refs[n + 1:2 * n + 1], refs[2 * n + 1]
        slots_ref, send_sem, recv_sem, send_tile, recv_tile = refs[2 * n + 2:]
        x, y, c, _ = _place()
        me = 4 * x + 2 * y + c
        copies = []
        for w in range(n):
            copies.append(_remote(ins[w], outs[w], send_sem.at[w], recv_sem.at[w], (x, y, 1 - c)))
        slots_ref[me] = part_ref[...]
        flips = [(fx, fy, fc) for fx in (0, 1) for fy in (0, 1) for fc in (0, 1)][1:]
        for k, (fx, fy, fc) in enumerate(flips):
            peer = (x + fx - 2 * x * fx, y + fy - 2 * y * fy, c + fc - 2 * c * fc)
            copies.append(_remote(part_ref, slots_ref.at[me], send_tile.at[k], recv_tile.at[k], peer))
        for cp in copies:
            cp.start()
        for cp in copies:
            cp.wait()
        acc = slots_ref[0]
        for k in range(1, n_dev):
            acc = acc + slots_ref[k]
        sum_ref[...] = acc

    vmem = pl.BlockSpec(memory_space=pltpu.VMEM)
    outs = pl.pallas_call(
        body, name="rs_share_late",
        in_specs=[ANY] * n + [vmem], out_specs=[ANY] * n + [vmem],
        out_shape=[jax.ShapeDtypeStruct(a.shape, a.dtype) for a in wholes] + [jax.ShapeDtypeStruct(part.shape, F32)],
        scratch_shapes=[pltpu.VMEM((n_dev,) + part.shape, F32),
                        pltpu.SemaphoreType.DMA((n,)), pltpu.SemaphoreType.DMA((n,)),
                        pltpu.SemaphoreType.DMA((n_dev - 1,)), pltpu.SemaphoreType.DMA((n_dev - 1,))],
    )(*wholes, part)
    return outs[:n], outs[n]


_SMALL = ("norm1_pre_g", "b_in", "pool_scale", "sgu_ln_g", "sgu_ln_b", "w_spatial",
          "norm1_post_g", "norm2_pre_g", "norm2_post_g", "b_spatial")
_MIX = ("w_in", "w_pool", "w_sgu_proj", "w_out")
_FF = ("w_ff1", "w_ff2")
_BIG = _MIX + _FF
_ORDER = ("norm1_pre_g", "w_in", "b_in", "w_pool", "pool_scale", "sgu_ln_g", "sgu_ln_b", "w_spatial", "b_spatial",
          "w_sgu_proj", "w_out", "norm1_post_g", "norm2_pre_g", "w_ff1", "w_ff2", "norm2_post_g")


def _pack(keys, rows, parts):
    flat = jnp.concatenate([parts[k].reshape(-1).astype(F32) for k in keys])
    flat = jnp.pad(flat, (0, rows * LANES - flat.shape[0]))
    return flat.reshape(rows, LANES)


def _halves(a):
    return a.reshape(2, a.shape[0] // 2, a.shape[1])


def _step(p, m, v, x, target):
    t = x.shape[0]
    tm_mm = min(1024, t)
    tm_mix = min(256, t)
    tm_dx = min(512, t)
    core = lax.axis_index("c").astype(jnp.int32).reshape(1)
    row = lambda a: a.reshape(1, -1)

    chip = (2 * lax.axis_index("x") + lax.axis_index("y")).astype(jnp.int32).reshape(1)

    local2d = {"w_in": p["w_in"], "w_ff1": p["w_ff1"], "w_ff2": p["w_ff2"], "w_out": p["w_out"],
               "w_pool": p["w_pool"].reshape(N_HEADS * 64, POOL_GROUP),
               "w_sgu_proj": p["w_sgu_proj"].reshape(N_HEADS * 64, HEAD)}
    shard = {k: _halves(local2d[k].astype(BF16)) for k in _BIG}
    landing = lambda keys: [lax.empty((N_SHARD,) + shard[k].shape, BF16) for k in keys]
    in_arrays = [shard["w_in"]] + landing(["w_in"])
    send_a, recv_a, in_arrays, token = _split_start("ag_in_send", in_arrays, 4, _plan_gather_send(1))
    z, xn = _inproj_own(x, row(p["norm1_pre_g"]), in_arrays[0].reshape(D, D_IN // N_SHARD), row(p["b_in"]), chip,
                        token, tm_mm)
    mix_keys = _MIX[1:]
    n_mix, n_ff = len(mix_keys), len(_FF)
    in_arrays = _split_wait("ag_in_land", send_a, recv_a, in_arrays, _plan_gather_send(1), 0, z,
                            *[shard[k] for k in mix_keys + _FF])
    rest = in_arrays[1:] + [shard[k] for k in mix_keys] + landing(mix_keys) + [shard[k] for k in _FF] + landing(_FF)
    rest_plan = _plans((_plan_gather_pass(1), 1), (_plan_gather_direct(n_mix), 2 * n_mix),
                       (_plan_gather_send(n_ff), 2 * n_ff))
    send0, recv0, rest, token = _split_start("ag_rest_send", rest, 3 + 7 * n_mix + 4 * n_ff, rest_plan)
    in_full, mix_arrays, ff_arrays = rest[:1], rest[1:1 + 2 * n_mix], rest[1 + 2 * n_mix:]
    in_full = _split_wait("ag_in_passed", send0, recv0, in_full, _plan_gather_pass(1), 0, token)
    w_in4 = in_full[0].reshape(N_SHARD, D, D_IN // N_SHARD)

    pos = jnp.arange(SGU_BLOCK) // 64
    mask = (pos[:, None] >= pos[None, :]).astype(F32)
    ws = (p["w_spatial"] * mask[None]).astype(BF16)
    ws_t = jnp.swapaxes(ws, 1, 2)
    bsp = p["b_spatial"].reshape(N_HEADS, SGU_BLOCK, 1)

    z = _inproj_rest(xn, w_in4, row(p["b_in"]), chip, z, tm_mm)
    mix_arrays = _split_wait("ag_mix_land", send0, recv0, mix_arrays, _plan_gather_direct(n_mix), 3, z)
    full = dict(zip(mix_keys, mix_arrays[len(mix_keys):]))
    w_out = full["w_out"].reshape(D, D)
    regroup = lambda a: a.reshape(N_SHARD, N_HEADS, 64, 256).transpose(1, 0, 2, 3).reshape(N_HEADS, 256, 256)
    w_pool = regroup(full["w_pool"])
    w_proj = regroup(full["w_sgu_proj"])
    pooled, merged, y, h1 = _mix_fwd(z, x, w_pool, row(p["pool_scale"]), row(p["sgu_ln_g"]), row(p["sgu_ln_b"]),
                                     ws, bsp, w_proj, w_out, row(p["norm1_post_g"]), tm_mix)
    ff_arrays = _split_wait("ag_ff_land", send0, recv0, ff_arrays, _plan_gather_send(n_ff), 3 + 7 * n_mix, pooled)
    send2, recv2, ff_full, token = _split_start("ag_ff_pass", ff_arrays[n_ff:], 3 * n_ff, _plan_gather_pass(n_ff))
    ff_full = _split_wait("ag_ff_passed", send2, recv2, ff_full, _plan_gather_pass(n_ff), 0, token)
    w_ff1_4 = ff_full[0].reshape(N_SHARD, D, D)
    w_ff2 = ff_full[1].reshape(N_SHARD, D, D)
    hn, f1, f, df2, dout, vec_a = _ffn_fwd(h1, row(p["norm2_pre_g"]), w_ff1_4, w_ff2, row(p["norm2_post_g"]),
                                           target, tm_mix)

    tk = min(2048, t)
    g_big = {"w_ff2": _dw(f, df2, "dw_ff2", 2, D, tk, N_SHARD, lambda i, j: i)}
    df1, dh1, dy, vec_b = _ffn_bwd(df2, f1, w_ff1_4, w_ff2, h1, dout, y, row(p["norm2_pre_g"]),
                                   row(p["norm1_post_g"]), tm_mix)
    g_big["w_ff1"] = _dw(hn, df1, "dw_ff1", 2, D, tk, N_SHARD, lambda i, j: j)

    def reduce_start(name, keys, wholes, meanwhile, also=(), also_plan=None, also_copies=0):
        n_k, n_all = len(keys), len(keys) + len(wholes)
        swap_plan = _plan_swap(n_k, len(wholes))
        crossing = [g_big[k] for k in keys] + list(wholes)
        crossing += [lax.empty(a.shape[1:] if w < n_k else a.shape, a.dtype) for w, a in enumerate(crossing)]
        send, recv, crossing, tok = _split_start("rs_swap_" + name + "_send", crossing, n_all, swap_plan)
        crossing = _split_wait("rs_swap_" + name + "_land", send, recv, crossing, swap_plan, 0, meanwhile(tok))
        mine, wholes, swapped = crossing[:n_k], crossing[n_k:n_all], crossing[n_all:]
        sums = _chip_sums(core, mine, swapped[:n_k], "chip_sums_" + name)
        arrays = list(sums)
        if wholes:
            arrays.append(_slot_sum(jnp.stack([wholes[0], swapped[-1]]), "chip_sum_small"))
        arrays += [lax.empty(s.shape, BF16) for s in sums]
        if wholes:
            arrays.append(lax.empty((N_SHARD,) + wholes[0].shape, F32))
        plan = _plan_exchange(len(keys), bool(wholes))
        n_copies, n = 3 * len(keys) + 4 * len(wholes), len(arrays)
        also = also() if callable(also) else also
        both = plan if also_plan is None else _plans((plan, n), (also_plan, len(also)))
        send, recv, out, tok = _split_start("rs_send_" + name, arrays + list(also), n_copies + also_copies, both)
        return send, recv, out[:n], plan, tok, out[n:], n_copies

    def reduce_finish(name, keys, state, *after):
        send, recv, arrays, plan = state[:4]
        arrays = _split_wait("rs_land_" + name, send, recv, arrays, plan, 0, *after)
        n_in = len(arrays) // 2
        n = len(keys)
        finished = _final_sums(chip, arrays[:n], arrays[n_in:n_in + n], "final_sums_" + name)
        return dict(zip(keys, finished)), arrays

    early = _FF
    late = ("w_in", "w_pool", "w_sgu_proj", "w_out")

    def dw_out(token):
        g_big["w_out"] = _dw(merged, dy, "dw_out", 2 * N_SHARD, D, tk, N_SHARD, lambda i, j: 0, token)
        return g_big["w_out"]

    early_state = reduce_start("early", early, [], dw_out)
    dz, g_big["w_pool"], g_big["w_sgu_proj"], dws, dbsp, vec_c, dbin = _mix_bwd(
        z, pooled, dy, w_pool, row(p["pool_scale"]), row(p["sgu_ln_g"]), row(p["sgu_ln_b"]), ws, ws_t, bsp, w_proj,
        w_out, mask, tm_mix, early_state[4])
    g_big["w_in"] = _dw(xn, dz, "dw_in", 2, D_IN // N_SHARD, tk, N_SHARD, lambda i, j: j)
    small_part = _pack(_SMALL[1:], SMALL_ROWS - 8, {
        "b_in": dbin[0], "pool_scale": vec_c[0], "sgu_ln_g": vec_c[1], "sgu_ln_b": vec_c[2],
        "w_spatial": dws, "b_spatial": dbsp[:, ::LANES].T, "norm1_post_g": vec_b[1], "norm2_pre_g": vec_b[0],
        "norm2_post_g": vec_a[0]})
    halves = {}

    def finish_early(token):
        halves.update(reduce_finish("early", early, early_state, token)[0])
        return halves[early[-1]]

    share_early = lambda: [halves[k] for k in early] + [lax.empty(halves[k].shape, F32) for k in early]
    late_state = reduce_start("late", late, [small_part], finish_early, share_early, _plan_share(len(early)),
                              len(early))
    grad_x, vec_d = _dx(dz, w_in4, x, dh1, row(p["norm1_pre_g"]), late_state[4], tm_dx)
    share = _split_wait("rs_share_early_land", late_state[0], late_state[1], late_state[5], _plan_share(len(early)),
                        late_state[6], grad_x)

    grads, delta, new_m, new_v = {}, {}, {}, {}

    def update(name, keys, own, shared):
        as2d = lambda a, k: a.reshape(local2d[k].shape)
        outs = _adamw_halves(core, own, shared, [local2d[k] for k in keys],
                             [as2d(m[k], k) for k in keys], [as2d(v[k], k) for k in keys], "adamw_" + name)
        for k, outs_k in zip(keys, outs):
            grads[k], delta[k], new_m[k], new_v[k] = (a.reshape(p[k].shape) for a in outs_k)

    update("early", early, share[:len(early)], share[len(early):])
    late_halves, late_arrays = reduce_finish("late", late, late_state, grad_x, *[new_v[k] for k in early])
    own_late = [late_halves[k] for k in late]
    shared_late, tiles = _share_and_all_reduce(
        own_late, jnp.concatenate([vec_d[0].reshape(8, LANES), jnp.broadcast_to(vec_a[1:2, :LANES], (8, LANES))]))
    g1_grad, loss = tiles[:8], tiles[8, 0]
    update("late", late, own_late, shared_late)
    as_rows = lambda d: [d[k].reshape(-1, LANES) for k in _SMALL]
    small_outs = _adamw_small([g1_grad, late_arrays[-1]], as_rows(p), as_rows(m), as_rows(v))
    for k, outs_k in zip(_SMALL, small_outs):
        grads[k], delta[k], new_m[k], new_v[k] = (a.reshape(p[k].shape) for a in outs_k)

    return (loss, grad_x, *[grads[k] for k in _ORDER], *[delta[k] for k in _ORDER],
            *[new_m[k] for k in _ORDER], *[new_v[k] for k in _ORDER])


def kernel(x, norm1_pre_g, w_in, b_in, w_pool, pool_scale, sgu_ln_g, sgu_ln_b, w_spatial, b_spatial, w_sgu_proj, w_out, norm1_post_g, norm2_pre_g, w_ff1, w_ff2, norm2_post_g, loss_target, m_norm1_pre_g, m_w_in, m_b_in, m_w_pool, m_pool_scale, m_sgu_ln_g, m_sgu_ln_b, m_w_spatial, m_b_spatial, m_w_sgu_proj, m_w_out, m_norm1_post_g, m_norm2_pre_g, m_w_ff1, m_w_ff2, m_norm2_post_g, v_norm1_pre_g, v_w_in, v_b_in, v_w_pool, v_pool_scale, v_sgu_ln_g, v_sgu_ln_b, v_w_spatial, v_b_spatial, v_w_sgu_proj, v_w_out, v_norm1_post_g, v_norm2_pre_g, v_w_ff1, v_w_ff2, v_norm2_post_g):
    p = dict(zip(_ORDER, (norm1_pre_g, w_in, b_in, w_pool, pool_scale, sgu_ln_g, sgu_ln_b, w_spatial, b_spatial,
                          w_sgu_proj, w_out, norm1_post_g, norm2_pre_g, w_ff1, w_ff2, norm2_post_g)))
    m = dict(zip(_ORDER, (m_norm1_pre_g, m_w_in, m_b_in, m_w_pool, m_pool_scale, m_sgu_ln_g, m_sgu_ln_b, m_w_spatial,
                          m_b_spatial, m_w_sgu_proj, m_w_out, m_norm1_post_g, m_norm2_pre_g, m_w_ff1, m_w_ff2,
                          m_norm2_post_g)))
    v = dict(zip(_ORDER, (v_norm1_pre_g, v_w_in, v_b_in, v_w_pool, v_pool_scale, v_sgu_ln_g, v_sgu_ln_b, v_w_spatial,
                          v_b_spatial, v_w_sgu_proj, v_w_out, v_norm1_post_g, v_norm2_pre_g, v_w_ff1, v_w_ff2,
                          v_norm2_post_g)))
    outs = _step(p, m, v, x[0], loss_target[0])
    return (outs[0], outs[1][None], *outs[2:])
```

```python
import math

import jax
import jax.numpy as jnp
from jax import lax
from jax.experimental import pallas as pl
from jax.experimental.pallas import tpu as pltpu

F32 = jnp.float32
BF16 = jnp.bfloat16

D = 1024
D_IN = 5 * D
D_FF = 4 * D
N_SHARD = 4
POOL_WINDOWS = (2, 4, 8, 16)
POOL_GROUP = 256
POOL_HALO = 16
SGU_BLOCK = 128
N_HEADS = 4
HEAD = 256
EPS = 1e-6

ADAM_LR = 0.001
ADAM_B1 = 0.9
ADAM_B2 = 0.999
ADAM_EPS = 1e-08
ADAM_WD = 0.01
ADAM_STEP = 10

V7X_VMEM_BYTES = 64 * 1024 * 1024
VMEM_LIMIT = V7X_VMEM_BYTES - 8 * 1024 * 1024
SMALL_ROWS = 616
LANES = 128

MESH = pl.DeviceIdType.MESH
ANY = pl.BlockSpec(memory_space=pl.ANY)
HBM = pl.BlockSpec(memory_space=pltpu.HBM)
SEM = pl.BlockSpec(memory_space=pltpu.SEMAPHORE)
EFFECT = pltpu.SideEffectType.DATAFLOW_SIDE_EFFECTING

NT_DIMS = (((1,), (1,)), ((), ()))
TN_DIMS = (((0,), (0,)), ((), ()))


def _params(n_axes):
    return pltpu.CompilerParams(dimension_semantics=("arbitrary",) * n_axes, vmem_limit_bytes=VMEM_LIMIT)


def _resident(shape, n_axes):
    zeros = (0,) * len(shape)
    index = (lambda i: zeros) if n_axes == 1 else (lambda i, j: zeros)
    return pl.BlockSpec(shape, index, pipeline_mode=pl.Buffered(1))


def _dot(a, b):
    return jnp.dot(a, b, preferred_element_type=F32)


def _dot_nt(a, b):
    return lax.dot_general(a, b, NT_DIMS, preferred_element_type=F32)


def _dot_tn(a, b):
    return lax.dot_general(a, b, TN_DIMS, preferred_element_type=F32)


def _mean(a):
    return jnp.mean(a, axis=-1, keepdims=True)


def _colsum(a):
    return jnp.sum(a, axis=0, keepdims=True)


_GELU_C0 = math.sqrt(2.0 / math.pi)
_GELU_C1 = 0.044715


def _gelu(a):
    t = jnp.tanh(a * (_GELU_C0 + (_GELU_C0 * _GELU_C1) * (a * a)))
    return a * (0.5 * t + 0.5)


def _gelu_and_grad(a):
    a2 = a * a
    t = jnp.tanh(a * (_GELU_C0 + (_GELU_C0 * _GELU_C1) * a2))
    cdf = 0.5 * t + 0.5
    val = a * cdf
    grad = cdf + val * (1.0 - cdf) * ((2.0 * _GELU_C0) + (6.0 * _GELU_C0 * _GELU_C1) * a2)
    return val, grad


def _sigmoid(a):
    return 0.5 * jnp.tanh(0.5 * a) + 0.5


def _rms_norm_bf16(a, g):
    return (a * lax.rsqrt(_mean(a * a) + EPS) * g).astype(BF16)


def _relu_sq_bf16(a):
    r = jnp.maximum(a, 0.0)
    return (r * r).astype(BF16)


def _window_sum(a, w, ahead):
    n = a.shape[0]
    step = 1
    while step < w:
        a = a + pltpu.roll(a, n - step if ahead else step, 0)
        step *= 2
    return a


def _rms_bwd(dn, n, r):
    return r * (dn - n * _mean(dn * n))


def _inproj_own(x, g1, w_own, b_in, chip, dep, tm):
    t = x.shape[0]
    cw = D_IN // N_SHARD

    def body(chip_ref, x_ref, g_ref, w_ref, b_ref, dep_ref, z_ref, xn_ref):
        xn = _rms_norm_bf16(x_ref[...], g_ref[...])
        xn_ref[...] = xn
        z_ref[...] = _dot(xn, w_ref[...]) + b_ref[...]

    return pl.pallas_call(
        body, name="inproj_own",
        grid_spec=pltpu.PrefetchScalarGridSpec(
            num_scalar_prefetch=1, grid=(t // tm,),
            in_specs=[pl.BlockSpec((tm, D), lambda i, chip: (i, 0)),
                      pl.BlockSpec((1, D), lambda i, chip: (0, 0)),
                      pl.BlockSpec((D, cw), lambda i, chip: (0, 0)),
                      pl.BlockSpec((1, cw), lambda i, chip: (0, chip[0])),
                      pl.BlockSpec(dep.shape, lambda i, chip: (0, 0))],
            out_specs=[pl.BlockSpec((tm, cw), lambda i, chip: (i, chip[0])),
                       pl.BlockSpec((tm, D), lambda i, chip: (i, 0))]),
        out_shape=[jax.ShapeDtypeStruct((t, D_IN), F32), jax.ShapeDtypeStruct((t, D), BF16)],
        compiler_params=_params(1),
    )(chip, x, g1, w_own, b_in, dep)


def _inproj_rest(xn, w_in4, b_in, chip, z, tm):
    t = xn.shape[0]
    cw = D_IN // N_SHARD
    shard_of = lambda s, chip: lax.rem(chip[0] + 1 + s, N_SHARD)

    def body(chip_ref, xn_ref, w_ref, b_ref, z_in_ref, z_ref):
        z_ref[...] = _dot(xn_ref[...], w_ref[shard_of(pl.program_id(1), chip_ref)]) + b_ref[...]

    return pl.pallas_call(
        body, name="inproj_rest",
        grid_spec=pltpu.PrefetchScalarGridSpec(
            num_scalar_prefetch=1, grid=(t // tm, N_SHARD - 1),
            in_specs=[pl.BlockSpec((tm, D), lambda i, s, chip: (i, 0)),
                      pl.BlockSpec((N_SHARD, D, cw), lambda i, s, chip: (0, 0, 0), pipeline_mode=pl.Buffered(1)),
                      pl.BlockSpec((1, cw), lambda i, s, chip: (0, shard_of(s, chip))),
                      ANY],
            out_specs=pl.BlockSpec((tm, cw), lambda i, s, chip: (i, shard_of(s, chip)))),
        out_shape=jax.ShapeDtypeStruct(z.shape, F32),
        input_output_aliases={4: 0},
        compiler_params=_params(2),
    )(chip, xn, w_in4, b_in, z)


def _sgu_forward(z_ref, lng_ref, lnb_ref, ws_ref, bsp_ref, sv_ref, tm, with_grad):
    if with_grad:
        gv, dgv_dz = _gelu_and_grad(z_ref[:, 2 * D:3 * D])
    else:
        gv = _gelu(z_ref[:, 2 * D:3 * D])
        dgv_dz = None
    xc = gv - _mean(gv)
    rstd = lax.rsqrt(_mean(xc * xc) + EPS)
    vhat = xc * rstd
    vb = (vhat * lng_ref[...] + lnb_ref[...]).astype(BF16)
    for b in range(tm // SGU_BLOCK):
        rows = slice(b * SGU_BLOCK, (b + 1) * SGU_BLOCK)
        for h in range(N_HEADS):
            cols = slice(h * HEAD, (h + 1) * HEAD)
            sv_ref[rows, cols] = _dot(ws_ref[h], vb[rows, cols]) + bsp_ref[h]
    return dgv_dz, vhat, rstd, vb


def _mix_fwd(z, x, w_pool, pool_scale, ln_g, ln_b, ws, bsp, w_proj, w_out, g1post, tm):
    t = x.shape[0]

    def body(z_ref, x_ref, wp_ref, psc_ref, lng_ref, lnb_ref, ws_ref, bsp_ref, wpr_ref, wo_ref, g1p_ref,
             pooled_ref, merged_ref, y_ref, h1_ref, win_ref, sv_ref):
        i = pl.program_id(0)

        @pl.when(i == 0)
        def _():
            win_ref[0:POOL_HALO, :] = jnp.zeros((POOL_HALO, D), F32)

        win_ref[POOL_HALO:POOL_HALO + tm, :] = z_ref[:, 0:D]
        pos1 = lax.broadcasted_iota(jnp.int32, (tm, 1), 0) + (i * tm + 1)
        a_parts = []
        for g, w in enumerate(POOL_WINDOWS):
            cols = slice(g * POOL_GROUP, (g + 1) * POOL_GROUP)
            ts = _window_sum(win_ref[:, cols], w, False)[POOL_HALO:POOL_HALO + tm, :]
            inv_cnt = 1.0 / jnp.minimum(pos1, w).astype(F32)
            pooled = (ts * inv_cnt - z_ref[:, cols]).astype(BF16)
            pooled_ref[:, cols] = pooled
            a_parts.append(_dot(pooled, wp_ref[g]) * psc_ref[:, cols])
        win_ref[0:POOL_HALO, :] = win_ref[tm:tm + POOL_HALO, :]

        _sgu_forward(z_ref, lng_ref, lnb_ref, ws_ref, bsp_ref, sv_ref, tm, False)
        gated = (_gelu(z_ref[:, D:2 * D]) * sv_ref[...]).astype(BF16)
        for h in range(N_HEADS):
            lo = h * HEAD
            bbr = _dot(gated[:, lo:lo + HEAD], wpr_ref[h])
            gate_a = _sigmoid(z_ref[:, 3 * D + lo:3 * D + lo + HEAD])
            gate_b = _sigmoid(z_ref[:, 4 * D + lo:4 * D + lo + HEAD])
            merged_ref[:, lo:lo + HEAD] = (gate_a * a_parts[h] + gate_b * bbr).astype(BF16)
        merged = merged_ref[...]
        y = _dot(merged, wo_ref[...])
        y_ref[...] = y
        r = lax.rsqrt(_mean(y * y) + EPS)
        h1_ref[...] = x_ref[...] + (y * r) * g1p_ref[...]

    row = lambda i: (i, 0)
    fixed2 = lambda i: (0, 0)
    fixed3 = lambda i: (0, 0, 0)
    vec = pl.BlockSpec((1, D), fixed2)
    return pl.pallas_call(
        body, name="mix_fwd",
        grid=(t // tm,),
        in_specs=[pl.BlockSpec((tm, D_IN), row), pl.BlockSpec((tm, D), row),
                  pl.BlockSpec((N_HEADS, POOL_GROUP, POOL_GROUP), fixed3), vec, vec, vec,
                  pl.BlockSpec((N_HEADS, SGU_BLOCK, SGU_BLOCK), fixed3),
                  pl.BlockSpec((N_HEADS, SGU_BLOCK, 1), fixed3),
                  pl.BlockSpec((N_HEADS, HEAD, HEAD), fixed3),
                  pl.BlockSpec((D, D), fixed2), vec],
        out_specs=[pl.BlockSpec((tm, D), row)] * 4,
        out_shape=[jax.ShapeDtypeStruct((t, D), BF16), jax.ShapeDtypeStruct((t, D), BF16),
                   jax.ShapeDtypeStruct((t, D), F32), jax.ShapeDtypeStruct((t, D), F32)],
        scratch_shapes=[pltpu.VMEM((tm + POOL_HALO, D), F32), pltpu.VMEM((tm, D), F32)],
        compiler_params=_params(1),
    )(z, x, w_pool, pool_scale, ln_g, ln_b, ws, bsp, w_proj, w_out, g1post)


def _ffn_fwd(h1, g2pre, w1, w2, g2post, target, tm):
    t = h1.shape[0]
    n_j = D_FF // D

    def body(h1_ref, g2_ref, w1_ref, w2_ref, g2p_ref, tgt_ref,
             hn_ref, f1_ref, f_ref, df2_ref, dout_ref, vec_ref):
        @pl.when(pl.program_id(0) == 0)
        def _():
            vec_ref[...] = jnp.zeros_like(vec_ref)

        hn = _rms_norm_bf16(h1_ref[...], g2_ref[...])
        hn_ref[...] = hn
        f2 = None
        for j in range(n_j):
            cols = slice(j * D, (j + 1) * D)
            f1 = _dot(hn, w1_ref[j])
            f1_ref[:, cols] = f1
            f = _relu_sq_bf16(f1)
            f_ref[:, cols] = f
            part = _dot(f, w2_ref[j])
            f2 = part if f2 is None else f2 + part
        r = lax.rsqrt(_mean(f2 * f2) + EPS)
        n2 = f2 * r
        err = h1_ref[...] + n2 * g2p_ref[...] - tgt_ref[...]
        loss = 0.5 * jnp.sum(_mean(err * err))
        dout = err * (1.0 / D)
        dout_ref[...] = dout
        vec_ref[0:1, :] += _colsum(dout * n2)
        vec_ref[1:2, :] += jnp.full((1, D), loss, F32)
        df2_ref[...] = _rms_bwd(dout * g2p_ref[...], n2, r).astype(BF16)

    row = lambda i: (i, 0)
    fixed = lambda i: (0, 0)
    vec = pl.BlockSpec((1, D), fixed)
    return pl.pallas_call(
        body, name="ffn_fwd",
        grid=(t // tm,),
        in_specs=[pl.BlockSpec((tm, D), row), vec,
                  _resident((n_j, D, D), 1), _resident((n_j, D, D), 1), vec,
                  pl.BlockSpec((tm, D), row)],
        out_specs=[pl.BlockSpec((tm, D), row),
                   pl.BlockSpec((tm, D_FF), row), pl.BlockSpec((tm, D_FF), row),
                   pl.BlockSpec((tm, D), row), pl.BlockSpec((tm, D), row),
                   pl.BlockSpec((8, D), fixed)],
        out_shape=[jax.ShapeDtypeStruct((t, D), BF16), jax.ShapeDtypeStruct((t, D_FF), F32),
                   jax.ShapeDtypeStruct((t, D_FF), BF16), jax.ShapeDtypeStruct((t, D), BF16),
                   jax.ShapeDtypeStruct((t, D), F32), jax.ShapeDtypeStruct((8, D), F32)],
        compiler_params=_params(1),
    )(h1, g2pre, w1, w2, g2post, target)


def _ffn_bwd(df2, f1, w1, w2, h1, dout, y, g2pre, g1post, tm):
    t = h1.shape[0]
    n_j = D_FF // D

    def body(df2_ref, f1_ref, w1_ref, w2_ref, h1_ref, dout_ref, y_ref, g2_ref, g1p_ref,
             df1_ref, dh1_ref, dy_ref, vec_ref):
        @pl.when(pl.program_id(0) == 0)
        def _():
            vec_ref[...] = jnp.zeros_like(vec_ref)

        df2 = df2_ref[...]
        dhn = None
        for j in range(n_j):
            cols = slice(j * D, (j + 1) * D)
            df = _dot_nt(df2, w2_ref[j])
            df1 = (df * (2.0 * jnp.maximum(f1_ref[:, cols], 0.0))).astype(BF16)
            df1_ref[:, cols] = df1
            part = _dot_nt(df1, w1_ref[j])
            dhn = part if dhn is None else dhn + part
        h = h1_ref[...]
        r2 = lax.rsqrt(_mean(h * h) + EPS)
        nh = h * r2
        vec_ref[0:1, :] += _colsum(dhn * nh)
        dh1 = dout_ref[...] + _rms_bwd(dhn * g2_ref[...], nh, r2)
        dh1_ref[...] = dh1
        yv = y_ref[...]
        r1 = lax.rsqrt(_mean(yv * yv) + EPS)
        ny = yv * r1
        vec_ref[1:2, :] += _colsum(dh1 * ny)
        dy_ref[...] = _rms_bwd(dh1 * g1p_ref[...], ny, r1).astype(BF16)

    row = lambda i: (i, 0)
    fixed = lambda i: (0, 0)
    vec = pl.BlockSpec((1, D), fixed)
    return pl.pallas_call(
        body, name="ffn_bwd",
        grid=(t // tm,),
        in_specs=[pl.BlockSpec((tm, D), row), pl.BlockSpec((tm, D_FF), row),
                  _resident((n_j, D, D), 1), _resident((n_j, D, D), 1),
                  pl.BlockSpec((tm, D), row), pl.BlockSpec((tm, D), row), pl.BlockSpec((tm, D), row), vec, vec],
        out_specs=[pl.BlockSpec((tm, D_FF), row),
                   pl.BlockSpec((tm, D), row), pl.BlockSpec((tm, D), row),
                   pl.BlockSpec((8, D), fixed)],
        out_shape=[jax.ShapeDtypeStruct((t, D_FF), BF16), jax.ShapeDtypeStruct((t, D), F32),
                   jax.ShapeDtypeStruct((t, D), BF16), jax.ShapeDtypeStruct((8, D), F32)],
        compiler_params=_params(1),
    )(df2, f1, w1, w2, h1, dout, y, g2pre, g1post)


def _mix_bwd(z, pooled, dy, w_pool, pool_scale, ln_g, ln_b, ws, ws_t, bsp, w_proj, w_out, mask, tm, dep):
    t = z.shape[0]
    n_t = t // tm
    shard_rows = POOL_GROUP // N_SHARD

    def body(z_ref, pooled_ref, dy_ref, wp_ref, psc_ref, lng_ref, lnb_ref, ws_ref, wst_ref, bsp_ref, wpr_ref,
             wo_ref, mask_ref, dep_ref,
             dz_ref, dwp_ref, dwpr_ref, dws_ref, dbsp_ref, vec_ref, dbin_ref,
             win_ref, sv_ref, dsv_ref, dv_ref, dsvsum_ref, dwp_acc, dwpr_acc):
        i = pl.program_id(0)

        @pl.when(i == 0)
        def _():
            win_ref[tm:tm + POOL_HALO, :] = jnp.zeros((POOL_HALO, D), F32)
            dwp_acc[...] = jnp.zeros_like(dwp_acc)
            dwpr_acc[...] = jnp.zeros_like(dwpr_acc)
            dws_ref[...] = jnp.zeros_like(dws_ref)
            vec_ref[...] = jnp.zeros_like(vec_ref)
            dbin_ref[...] = jnp.zeros_like(dbin_ref)
            dsvsum_ref[...] = jnp.zeros_like(dsvsum_ref)

        dmerged = _dot_nt(dy_ref[...], wo_ref[...])

        dgv_dz, vhat, rstd, vb = _sgu_forward(z_ref, lng_ref, lnb_ref, ws_ref, bsp_ref, sv_ref, tm, True)

        tile = n_t - 1 - i
        pos1 = lax.broadcasted_iota(jnp.int32, (tm, 1), 0) + (tile * tm + 1)
        def put(k, lo, part):
            cols = slice(k * D + lo, k * D + lo + part.shape[1])
            dz_ref[:, cols] = part.astype(BF16)
            dbin_ref[0:1, cols] += _colsum(part)

        dpooled_parts = []
        for g, w in enumerate(POOL_WINDOWS):
            cols = slice(g * POOL_GROUP, (g + 1) * POOL_GROUP)
            pg = pooled_ref[:, cols]
            a_pre = _dot(pg, wp_ref[g])
            sa_g = _sigmoid(z_ref[:, 3 * D + g * POOL_GROUP:3 * D + (g + 1) * POOL_GROUP])
            da_g = dmerged[:, cols] * sa_g
            psc = psc_ref[:, cols]
            da_a = da_g * a_pre
            vec_ref[0:1, cols] += _colsum(da_a)
            put(3, g * POOL_GROUP, da_a * psc * (1.0 - sa_g))
            da_pre = (da_g * psc).astype(BF16)
            dwp_acc[g] += _dot_tn(pg, da_pre)
            dpooled = _dot_nt(da_pre, wp_ref[g])
            dpooled_parts.append(dpooled)
            inv_cnt = 1.0 / jnp.minimum(pos1, w).astype(F32)
            win_ref[0:tm, cols] = dpooled * inv_cnt
        for g, w in enumerate(POOL_WINDOWS):
            cols = slice(g * POOL_GROUP, (g + 1) * POOL_GROUP)
            acc = _window_sum(win_ref[:, cols], w, True)[0:tm, :]
            put(0, g * POOL_GROUP, acc - dpooled_parts[g])
        win_ref[tm:tm + POOL_HALO, :] = win_ref[0:POOL_HALO, :]

        for h in range(N_HEADS):
            cols = slice(h * HEAD, (h + 1) * HEAD)
            u_h, du_dz_h = _gelu_and_grad(z_ref[:, D + h * HEAD:D + (h + 1) * HEAD])
            sv_h = sv_ref[:, cols]
            g_h = (u_h * sv_h).astype(BF16)
            sb_h = _sigmoid(z_ref[:, 4 * D + h * HEAD:4 * D + (h + 1) * HEAD])
            db_f = dmerged[:, cols] * sb_h
            db_h = db_f.astype(BF16)
            bbr_h = _dot(g_h, wpr_ref[h])
            put(4, h * HEAD, db_f * bbr_h * (1.0 - sb_h))
            dwpr_acc[h] += _dot_tn(g_h, db_h)
            dgated_h = _dot_nt(db_h, wpr_ref[h])
            dsv_ref[:, cols] = dgated_h * u_h
            put(1, h * HEAD, dgated_h * sv_h * du_dz_h)
        dsv = dsv_ref[...]
        dsvb = dsv.astype(BF16)
        blk_sum = dsv[0:SGU_BLOCK, :]
        for b in range(1, tm // SGU_BLOCK):
            blk_sum = blk_sum + dsv[b * SGU_BLOCK:(b + 1) * SGU_BLOCK, :]
        dsvsum_ref[...] += blk_sum
        for b in range(tm // SGU_BLOCK):
            rows = slice(b * SGU_BLOCK, (b + 1) * SGU_BLOCK)
            for h in range(N_HEADS):
                cols = slice(h * HEAD, (h + 1) * HEAD)
                dws_ref[h] += _dot_nt(dsvb[rows, cols], vb[rows, cols])
                dv_ref[rows, cols] = _dot(wst_ref[h], dsvb[rows, cols])
        dv = dv_ref[...]
        vec_ref[1:2, :] += _colsum(dv * vhat)
        vec_ref[2:3, :] += _colsum(dv)
        dvhat = dv * lng_ref[...]
        dgv = rstd * (dvhat - _mean(dvhat) - vhat * _mean(dvhat * vhat))
        put(2, 0, dgv * dgv_dz)

        @pl.when(i == n_t - 1)
        def _():
            for h in range(N_HEADS):
                dws_ref[h] = jnp.where(mask_ref[...] > 0.0, dws_ref[h], 0.0)
                tot = jnp.sum(dsvsum_ref[:, h * HEAD:(h + 1) * HEAD], axis=1, keepdims=True)
                dbsp_ref[:, h * LANES:(h + 1) * LANES] = jnp.broadcast_to(tot, (SGU_BLOCK, LANES))
            for acc, out in ((dwp_acc, dwp_ref), (dwpr_acc, dwpr_ref)):
                for g in range(N_HEADS):
                    for s in range(N_SHARD):
                        out[g // 2, s, (g % 2) * shard_rows:(g % 2 + 1) * shard_rows, :] = (
                            acc[g, s * shard_rows:(s + 1) * shard_rows, :].astype(BF16))

    row = lambda i: (n_t - 1 - i, 0)
    fixed2 = lambda i: (0, 0)
    fixed3 = lambda i: (0, 0, 0)
    fixed4 = lambda i: (0, 0, 0, 0)
    vec = pl.BlockSpec((1, D), fixed2)
    sq = pl.BlockSpec((N_HEADS, SGU_BLOCK, SGU_BLOCK), fixed3)
    grp = pl.BlockSpec((N_HEADS, HEAD, HEAD), fixed3)
    grp_out = pl.BlockSpec((2, N_SHARD, 2 * shard_rows, HEAD), fixed4)
    tile = pl.BlockSpec((tm, D), row)
    return pl.pallas_call(
        body, name="mix_bwd",
        grid=(n_t,),
        in_specs=[pl.BlockSpec((tm, D_IN), row), tile, tile,
                  grp, vec, vec, vec, sq, sq, pl.BlockSpec((N_HEADS, SGU_BLOCK, 1), fixed3), grp,
                  pl.BlockSpec((D, D), fixed2), pl.BlockSpec((SGU_BLOCK, SGU_BLOCK), fixed2),
                  pl.BlockSpec(dep.shape, fixed2)],
        out_specs=[pl.BlockSpec((tm, D_IN), row), grp_out, grp_out, sq,
                   pl.BlockSpec((SGU_BLOCK, N_HEADS * LANES), fixed2),
                   pl.BlockSpec((8, D), fixed2), pl.BlockSpec((8, D_IN), fixed2)],
        out_shape=[jax.ShapeDtypeStruct((t, D_IN), BF16),
                   jax.ShapeDtypeStruct((2, N_SHARD, 2 * shard_rows, HEAD), BF16),
                   jax.ShapeDtypeStruct((2, N_SHARD, 2 * shard_rows, HEAD), BF16),
                   jax.ShapeDtypeStruct((N_HEADS, SGU_BLOCK, SGU_BLOCK), F32),
                   jax.ShapeDtypeStruct((SGU_BLOCK, N_HEADS * LANES), F32),
                   jax.ShapeDtypeStruct((8, D), F32), jax.ShapeDtypeStruct((8, D_IN), F32)],
        scratch_shapes=[pltpu.VMEM((tm + POOL_HALO, D), F32), pltpu.VMEM((tm, D), F32), pltpu.VMEM((tm, D), F32),
                        pltpu.VMEM((tm, D), F32), pltpu.VMEM((SGU_BLOCK, D), F32),
                        pltpu.VMEM((N_HEADS, HEAD, HEAD), F32), pltpu.VMEM((N_HEADS, HEAD, HEAD), F32)],
        compiler_params=_params(1),
    )(z, pooled, dy, w_pool, pool_scale, ln_g, ln_b, ws, ws_t, bsp, w_proj, w_out, mask, dep)


def _dx(dz, w_in4, x, dh1, g1, dep, tm):
    t = x.shape[0]
    cw = D_IN // N_SHARD

    def body(dz_ref, w_ref, x_ref, dh1_ref, g_ref, dep_ref, gx_ref, vec_ref):
        @pl.when(pl.program_id(0) == 0)
        def _():
            vec_ref[...] = jnp.zeros_like(vec_ref)

        dxn = _dot_nt(dz_ref[:, 0:cw], w_ref[0])
        for s in range(1, N_SHARD):
            dxn = dxn + _dot_nt(dz_ref[:, s * cw:(s + 1) * cw], w_ref[s])
        xv = x_ref[...]
        r = lax.rsqrt(_mean(xv * xv) + EPS)
        xh = xv * r
        vec_ref[0:1, :] += _colsum(dxn * xh)
        gx_ref[...] = dh1_ref[...] + _rms_bwd(dxn * g_ref[...], xh, r)

    row = lambda i: (i, 0)
    fixed = lambda i: (0, 0)
    return pl.pallas_call(
        body, name="dx",
        grid=(t // tm,),
        in_specs=[pl.BlockSpec((tm, D_IN), row),
                  _resident((N_SHARD, D, cw), 1),
                  pl.BlockSpec((tm, D), row), pl.BlockSpec((tm, D), row), pl.BlockSpec((1, D), fixed),
                  pl.BlockSpec(dep.shape, fixed)],
        out_specs=[pl.BlockSpec((tm, D), row), pl.BlockSpec((8, D), fixed)],
        out_shape=[jax.ShapeDtypeStruct((t, D), F32), jax.ShapeDtypeStruct((8, D), F32)],
        compiler_params=_params(1),
    )(dz, w_in4, x, dh1, g1, dep)


def _dw(a, b, name, n_split, tn, tk, n_out, out_block, dep=None):
    t, m = a.shape
    n = b.shape[1]
    tm = min(m, D)
    rows = tm // n_split
    n_k = t // tk
    deps = [] if dep is None else [dep]

    def body(a_ref, b_ref, *rest):
        o_ref, acc_ref = rest[-2:]
        k = pl.program_id(2)

        @pl.when(k == 0)
        def _():
            acc_ref[...] = jnp.zeros_like(acc_ref)

        acc_ref[...] += _dot_tn(a_ref[...], b_ref[...])

        @pl.when(k == n_k - 1)
        def _():
            for q in range(n_split):
                o_ref[q % 2, q // 2] = acc_ref[q * rows:(q + 1) * rows, :].astype(BF16)

    return pl.pallas_call(
        body, name=name,
        grid=(m // tm, n // tn, n_k),
        in_specs=[pl.BlockSpec((tk, tm), lambda i, j, k: (k, i)),
                  pl.BlockSpec((tk, tn), lambda i, j, k: (k, j))]
        + [pl.BlockSpec(d.shape, lambda i, j, k: (0, 0)) for d in deps],
        out_specs=pl.BlockSpec((2, n_split // 2, rows, tn), lambda i, j, k: (0, out_block(i, j), 0, 0)),
        out_shape=jax.ShapeDtypeStruct((2, n_out, rows, tn), BF16),
        scratch_shapes=[pltpu.VMEM((tm, tn), F32)],
        compiler_params=_params(3),
    )(a, b, *deps)


def _row_block(rows, cols, itemsize, n_bufs):
    budget = VMEM_LIMIT // 4
    rb = rows
    while rb % 16 == 0 and rb * cols * itemsize * n_bufs * 2 > budget:
        rb //= 2
    return rb


ROW_STEPS = 2


def _chip_sums(core, grads, recvs, name):
    n = len(grads)

    def body(c_ref, *refs):
        for w in range(n):
            refs[2 * n + w][...] = (refs[w][0].astype(F32) + refs[n + w][...].astype(F32)).astype(BF16)

    def blocks(g):
        _, _, rh, cols = g.shape
        rb = rh // ROW_STEPS
        return (pl.BlockSpec((1, 1, rb, cols), lambda s, r, c: (c[0], s, r, 0)),
                pl.BlockSpec((1, rb, cols), lambda s, r, c: (s, r, 0)))

    specs = [blocks(g) for g in grads]
    return pl.pallas_call(
        body, name=name,
        grid_spec=pltpu.PrefetchScalarGridSpec(
            num_scalar_prefetch=1, grid=(N_SHARD, ROW_STEPS),
            in_specs=[s[0] for s in specs] + [s[1] for s in specs],
            out_specs=[s[1] for s in specs]),
        out_shape=[jax.ShapeDtypeStruct(g.shape[1:], BF16) for g in grads],
        compiler_params=_params(2),
    )(core, *grads, *recvs)


def _slot_sum(slots, name):
    n_s, rows, cols = slots.shape
    rb = _row_block(rows, cols, 4, n_s + 1)

    def body(s_ref, o_ref):
        acc = s_ref[0].astype(F32)
        for k in range(1, n_s):
            acc = acc + s_ref[k].astype(F32)
        o_ref[...] = acc

    return pl.pallas_call(
        body, name=name,
        grid=(rows // rb,),
        in_specs=[pl.BlockSpec((n_s, rb, cols), lambda r: (0, r, 0))],
        out_specs=pl.BlockSpec((rb, cols), lambda r: (r, 0)),
        out_shape=jax.ShapeDtypeStruct((rows, cols), F32),
        compiler_params=_params(1),
    )(slots)


def _adamw_update(g, w, m, v):
    m_new = ADAM_B1 * m + (1.0 - ADAM_B1) * g
    v_new = ADAM_B2 * v + (1.0 - ADAM_B2) * (g * g)
    m_hat = m_new / (1.0 - ADAM_B1 ** ADAM_STEP)
    v_hat = v_new / (1.0 - ADAM_B2 ** ADAM_STEP)
    return -ADAM_LR * (m_hat / (jnp.sqrt(v_hat) + ADAM_EPS) + ADAM_WD * w), m_new, v_new


def _adamw_small(grad_parts, ws, ms, vs):
    n, n_g = len(ws), len(grad_parts)

    def rows_of(ref, lo, rows):
        if len(ref.shape) == 2:
            return ref[lo:lo + rows, :]
        acc = ref[0, lo:lo + rows, :]
        for s in range(1, ref.shape[0]):
            acc = acc + ref[s, lo:lo + rows, :]
        return acc

    def body(*refs):
        g_refs, ins, outs = refs[:n_g], refs[n_g:n_g + 3 * n], refs[n_g + 3 * n:]
        part, off = 0, 0
        for k in range(n):
            rows = ws[k].shape[0]
            if off + rows > grad_parts[part].shape[-2]:
                part, off = part + 1, 0
            g = rows_of(g_refs[part], off, rows)
            off += rows
            delta, m_new, v_new = _adamw_update(g, ins[k][...], ins[n + k][...], ins[2 * n + k][...])
            for o, val in zip(outs[4 * k:4 * k + 4], (g, delta, m_new, v_new)):
                o[...] = val

    vmem = pl.BlockSpec(memory_space=pltpu.VMEM)
    outs = pl.pallas_call(
        body, name="adamw_small",
        in_specs=[vmem] * (n_g + 3 * n), out_specs=[vmem] * (4 * n),
        out_shape=[jax.ShapeDtypeStruct(w.shape, F32) for w in ws for _ in range(4)],
    )(*grad_parts, *ws, *ms, *vs)
    return [outs[4 * k:4 * k + 4] for k in range(n)]


def _final_sums(chip, owns, slots, name):
    n = len(owns)

    def body(chip_ref, *refs):
        for w in range(n):
            own_ref, s1_ref, s2_ref, s3_ref = refs[4 * w:4 * w + 4]
            acc = own_ref[0].astype(F32) + s1_ref[0].astype(F32)
            acc = acc + s2_ref[0].astype(F32)
            refs[4 * n + w][...] = acc + s3_ref[0].astype(F32)

    def slot(a, k):
        n_s, rh, cols = a.shape
        return pl.BlockSpec((1, rh // ROW_STEPS, cols), lambda r, chip: (lax.rem(chip[0] + k, n_s), r, 0))

    operands, in_specs = [], []
    for own, slots_w in zip(owns, slots):
        operands += [own, slots_w, slots_w, slots_w]
        in_specs += [slot(own, k) for k in range(4)]
    return pl.pallas_call(
        body, name=name,
        grid_spec=pltpu.PrefetchScalarGridSpec(
            num_scalar_prefetch=1, grid=(ROW_STEPS,),
            in_specs=in_specs,
            out_specs=[pl.BlockSpec((a.shape[1] // ROW_STEPS, a.shape[2]), lambda r, chip: (r, 0)) for a in owns]),
        out_shape=[jax.ShapeDtypeStruct(a.shape[1:], F32) for a in owns],
        compiler_params=_params(1),
    )(chip, *operands)


def _adamw_halves(core, owns, siblings, ws, ms, vs, name):
    n = len(ws)

    def body(c_ref, *refs):
        ins, outs = refs[:5 * n], refs[5 * n:]
        mine = c_ref[0] == pl.program_id(0)
        for k in range(n):
            own_ref, sib_ref, w_ref, m_ref, v_ref = ins[5 * k:5 * k + 5]
            g_ref, d_ref, mo_ref, vo_ref = outs[4 * k:4 * k + 4]

            @pl.when(mine)
            def _():
                g_ref[...] = own_ref[...]

            @pl.when(jnp.logical_not(mine))
            def _():
                g_ref[...] = sib_ref[...]

            d_ref[...], mo_ref[...], vo_ref[...] = _adamw_update(g_ref[...], w_ref[...], m_ref[...], v_ref[...])

    operands, in_specs, out_specs = [], [], []
    for own, sib, w, m, v in zip(owns, siblings, ws, ms, vs):
        rows, cols = w.shape
        rb = rows // 2 // ROW_STEPS
        half = pl.BlockSpec((rb, cols), lambda h, r, c: (r, 0))
        whole = pl.BlockSpec((rb, cols), lambda h, r, c: (h * ROW_STEPS + r, 0))
        operands += [own, sib, w, m, v]
        in_specs += [half, half, whole, whole, whole]
        out_specs += [whole] * 4
    outs = pl.pallas_call(
        body, name=name,
        grid_spec=pltpu.PrefetchScalarGridSpec(
            num_scalar_prefetch=1, grid=(2, ROW_STEPS), in_specs=in_specs, out_specs=out_specs),
        out_shape=[jax.ShapeDtypeStruct(w.shape, F32) for w in ws for _ in range(4)],
        compiler_params=_params(2),
    )(core, *operands)
    return [outs[4 * k:4 * k + 4] for k in range(n)]


def _place():
    x, y, c = lax.axis_index("x"), lax.axis_index("y"), lax.axis_index("c")
    chips = [(1 - x, y), (x, 1 - y), (1 - x, 1 - y)]
    return x, y, c, chips


def _remote(src, dst, send_sem, recv_sem, device):
    return pltpu.make_async_remote_copy(src_ref=src, dst_ref=dst, send_sem=send_sem, recv_sem=recv_sem,
                                        device_id=device, device_id_type=MESH)


def _split_start(name, arrays, n_copies, plan):
    n = len(arrays)

    def body(*refs):
        ins, send_sem, recv_sem, token = refs[:n], refs[n], refs[n + 1], refs[-1]
        for k, (src, dst, device) in enumerate(plan(ins)):
            _remote(src, dst, send_sem.at[k], recv_sem.at[k], device).start()
        token[...] = jnp.zeros_like(token)

    outs = pl.pallas_call(
        body, name=name,
        in_specs=[HBM] * n,
        out_specs=(SEM, SEM, *[HBM] * n, pl.BlockSpec(memory_space=pltpu.VMEM)),
        out_shape=(pltpu.SemaphoreType.DMA((n_copies,)), pltpu.SemaphoreType.DMA((n_copies,)),
                   *[pltpu.HBM(a.shape, a.dtype) for a in arrays], jax.ShapeDtypeStruct((8, LANES), F32)),
        input_output_aliases={i: i + 2 for i in range(n)},
        compiler_params=pltpu.CompilerParams(has_side_effects=EFFECT),
    )(*[pltpu.with_memory_space_constraint(a, pltpu.HBM) for a in arrays])
    return outs[0], outs[1], list(outs[2:2 + n]), outs[-1]


def _split_wait(name, send_sem, recv_sem, arrays, plan, first, *after):
    n = len(arrays)

    def body(*refs):
        ins, send_ref, recv_ref = refs[:n], refs[n], refs[n + 1]
        for k, (src, dst, device) in enumerate(plan(ins)):
            cp = _remote(src, dst, send_ref.at[first + k], recv_ref.at[first + k], device)
            cp.wait_send()
            cp.wait_recv()

    outs = pl.pallas_call(
        body, name=name,
        in_specs=[HBM] * n + [SEM, SEM] + [ANY] * len(after), out_specs=[HBM] * n,
        out_shape=[pltpu.HBM(a.shape, a.dtype) for a in arrays],
        input_output_aliases={i: i for i in range(n)},
        compiler_params=pltpu.CompilerParams(has_side_effects=EFFECT),
    )(*arrays, send_sem, recv_sem, *after)
    return list(outs)


def _plans(*parts):
    def plan(refs):
        copies, lo = [], 0
        for part, n in parts:
            copies += part(refs[lo:lo + n])
            lo += n
        return copies
    return plan


def _plan_gather_send(n):
    def plan(refs):
        x, y, c, chips = _place()
        me = 2 * x + y
        copies = []
        for w in range(n):
            own, full = refs[w], refs[n + w]
            copies.append((own, full.at[me], (x, y, 1 - c)))
            copies += [(own.at[c], full.at[me, c], (px, py, c)) for px, py in chips]
        return copies
    return plan


def _plan_gather_direct(n):
    def plan(refs):
        x, y, c, chips = _place()
        me = 2 * x + y
        copies = []
        for w in range(n):
            own, full = refs[w], refs[n + w]
            copies.append((own, full.at[me], (x, y, 1 - c)))
            for px, py in chips:
                copies += [(own.at[c], full.at[me, c], (px, py, c)), (own.at[c], full.at[me, c], (px, py, 1 - c))]
        return copies
    return plan


def _plan_gather_pass(n):
    def plan(refs):
        x, y, c, chips = _place()
        copies = []
        for w in range(n):
            for px, py in chips:
                landed = refs[w].at[2 * px + py, c]
                copies.append((landed, landed, (x, y, 1 - c)))
        return copies
    return plan


def _plan_swap(n, n_wholes):
    def plan(refs):
        x, y, c, _ = _place()
        n_all = n + n_wholes
        return [(refs[w].at[1 - c] if w < n else refs[w], refs[n_all + w], (x, y, 1 - c)) for w in range(n_all)]
    return plan


def _plan_share(n):
    def plan(refs):
        x, y, c, _ = _place()
        return [(refs[w], refs[n + w], (x, y, 1 - c)) for w in range(n)]
    return plan


def _plan_exchange(n, with_small):
    def plan(refs):
        x, y, c, chips = _place()
        me = 2 * x + y
        n_in = n + int(with_small)
        copies = []
        for w in range(n):
            copies += [(refs[w].at[2 * px + py], refs[n_in + w].at[me], (px, py, c)) for px, py in chips]
        if with_small:
            small, slots = refs[n], refs[n_in + n]
            copies.append((small, slots.at[me], (x, y, 1 - c)))
            copies += [(small, slots.at[me], (px, py, c)) for px, py in chips]
        return copies
    return plan


def _share_and_all_reduce(wholes, part):
    n = len(wholes)
    n_dev = 8

    def body(*refs):
        ins, part_ref, outs, sum_ref = refs[:n], refs[n], refs[n + 1:2 * n + 1], refs[2 * n + 1]
        slots_ref, send_sem, recv_sem, send_tile, recv_tile = refs[2 * n + 2:]
        x, y, c, _ = _place()
        me = 4 * x + 2 * y + c
        copies = []
        for w in range(n):
            copies.append(_remote(ins[w], outs[w], send_sem.at[w], recv_sem.at[w], (x, y, 1 - c)))
        slots_ref[me] = part_ref[...]
        flips = [(fx, fy, fc) for fx in (0, 1) for fy in (0, 1) for fc in (0, 1)][1:]
        for k, (fx, fy, fc) in enumerate(flips):
            peer = (x + fx - 2 * x * fx, y + fy - 2 * y * fy, c + fc - 2 * c * fc)
            copies.append(_remote(part_ref, slots_ref.at[me], send_tile.at[k], recv_tile.at[k], peer))
        for cp in copies:
            cp.start()
        for cp in copies:
            cp.wait()
        acc = slots_ref[0]
        for k in range(1, n_dev):
            acc = acc + slots_ref[k]
        sum_ref[...] = acc

    vmem = pl.BlockSpec(memory_space=pltpu.VMEM)
    outs = pl.pallas_call(
        body, name="rs_share_late",
        in_specs=[ANY] * n + [vmem], out_specs=[ANY] * n + [vmem],
        out_shape=[jax.ShapeDtypeStruct(a.shape, a.dtype) for a in wholes] + [jax.ShapeDtypeStruct(part.shape, F32)],
        scratch_shapes=[pltpu.VMEM((n_dev,) + part.shape, F32),
                        pltpu.SemaphoreType.DMA((n,)), pltpu.SemaphoreType.DMA((n,)),
                        pltpu.SemaphoreType.DMA((n_dev - 1,)), pltpu.SemaphoreType.DMA((n_dev - 1,))],
    )(*wholes, part)
    return outs[:n], outs[n]


_SMALL = ("norm1_pre_g", "b_in", "pool_scale", "sgu_ln_g", "sgu_ln_b", "w_spatial",
          "norm1_post_g", "norm2_pre_g", "norm2_post_g", "b_spatial")
_MIX = ("w_in", "w_pool", "w_sgu_proj", "w_out")
_FF = ("w_ff1", "w_ff2")
_BIG = _MIX + _FF
_ORDER = ("norm1_pre_g", "w_in", "b_in", "w_pool", "pool_scale", "sgu_ln_g", "sgu_ln_b", "w_spatial", "b_spatial",
          "w_sgu_proj", "w_out", "norm1_post_g", "norm2_pre_g", "w_ff1", "w_ff2", "norm2_post_g")


def _pack(keys, rows, parts):
    flat = jnp.concatenate([parts[k].reshape(-1).astype(F32) for k in keys])
    flat = jnp.pad(flat, (0, rows * LANES - flat.shape[0]))
    return flat.reshape(rows, LANES)


def _halves(a):
    return a.reshape(2, a.shape[0] // 2, a.shape[1])


def _step(p, m, v, x, target):
    t = x.shape[0]
    tm_mm = min(1024, t)
    tm_mix = min(256, t)
    tm_dx = min(512, t)
    core = lax.axis_index("c").astype(jnp.int32).reshape(1)
    row = lambda a: a.reshape(1, -1)

    chip = (2 * lax.axis_index("x") + lax.axis_index("y")).astype(jnp.int32).reshape(1)

    local2d = {"w_in": p["w_in"], "w_ff1": p["w_ff1"], "w_ff2": p["w_ff2"], "w_out": p["w_out"],
               "w_pool": p["w_pool"].reshape(N_HEADS * 64, POOL_GROUP),
               "w_sgu_proj": p["w_sgu_proj"].reshape(N_HEADS * 64, HEAD)}
    shard = {k: _halves(local2d[k].astype(BF16)) for k in _BIG}
    landing = lambda keys: [lax.empty((N_SHARD,) + shard[k].shape, BF16) for k in keys]
    in_arrays = [shard["w_in"]] + landing(["w_in"])
    send_a, recv_a, in_arrays, token = _split_start("ag_in_send", in_arrays, 4, _plan_gather_send(1))
    z, xn = _inproj_own(x, row(p["norm1_pre_g"]), in_arrays[0].reshape(D, D_IN // N_SHARD), row(p["b_in"]), chip,
                        token, tm_mm)
    mix_keys = _MIX[1:]
    n_mix, n_ff = len(mix_keys), len(_FF)
    in_arrays = _split_wait("ag_in_land", send_a, recv_a, in_arrays, _plan_gather_send(1), 0, z,
                            *[shard[k] for k in mix_keys + _FF])
    rest = in_arrays[1:] + [shard[k] for k in mix_keys] + landing(mix_keys) + [shard[k] for k in _FF] + landing(_FF)
    rest_plan = _plans((_plan_gather_pass(1), 1), (_plan_gather_direct(n_mix), 2 * n_mix),
                       (_plan_gather_send(n_ff), 2 * n_ff))
    send0, recv0, rest, token = _split_start("ag_rest_send", rest, 3 + 7 * n_mix + 4 * n_ff, rest_plan)
    in_full, mix_arrays, ff_arrays = rest[:1], rest[1:1 + 2 * n_mix], rest[1 + 2 * n_mix:]
    in_full = _split_wait("ag_in_passed", send0, recv0, in_full, _plan_gather_pass(1), 0, token)
    w_in4 = in_full[0].reshape(N_SHARD, D, D_IN // N_SHARD)

    pos = jnp.arange(SGU_BLOCK) // 64
    mask = (pos[:, None] >= pos[None, :]).astype(F32)
    ws = (p["w_spatial"] * mask[None]).astype(BF16)
    ws_t = jnp.swapaxes(ws, 1, 2)
    bsp = p["b_spatial"].reshape(N_HEADS, SGU_BLOCK, 1)

    z = _inproj_rest(xn, w_in4, row(p["b_in"]), chip, z, tm_mm)
    mix_arrays = _split_wait("ag_mix_land", send0, recv0, mix_arrays, _plan_gather_direct(n_mix), 3, z)
    full = dict(zip(mix_keys, mix_arrays[len(mix_keys):]))
    w_out = full["w_out"].reshape(D, D)
    regroup = lambda a: a.reshape(N_SHARD, N_HEADS, 64, 256).transpose(1, 0, 2, 3).reshape(N_HEADS, 256, 256)
    w_pool = regroup(full["w_pool"])
    w_proj = regroup(full["w_sgu_proj"])
    pooled, merged, y, h1 = _mix_fwd(z, x, w_pool, row(p["pool_scale"]), row(p["sgu_ln_g"]), row(p["sgu_ln_b"]),
                                     ws, bsp, w_proj, w_out, row(p["norm1_post_g"]), tm_mix)
    ff_arrays = _split_wait("ag_ff_land", send0, recv0, ff_arrays, _plan_gather_send(n_ff), 3 + 7 * n_mix, pooled)
    send2, recv2, ff_full, token = _split_start("ag_ff_pass", ff_arrays[n_ff:], 3 * n_ff, _plan_gather_pass(n_ff))
    ff_full = _split_wait("ag_ff_passed", send2, recv2, ff_full, _plan_gather_pass(n_ff), 0, token)
    w_ff1_4 = ff_full[0].reshape(N_SHARD, D, D)
    w_ff2 = ff_full[1].reshape(N_SHARD, D, D)
    hn, f1, f, df2, dout, vec_a = _ffn_fwd(h1, row(p["norm2_pre_g"]), w_ff1_4, w_ff2, row(p["norm2_post_g"]),
                                           target, tm_mix)

    tk = min(2048, t)
    g_big = {"w_ff2": _dw(f, df2, "dw_ff2", 2, D, tk, N_SHARD, lambda i, j: i)}
    df1, dh1, dy, vec_b = _ffn_bwd(df2, f1, w_ff1_4, w_ff2, h1, dout, y, row(p["norm2_pre_g"]),
                                   row(p["norm1_post_g"]), tm_mix)
    g_big["w_ff1"] = _dw(hn, df1, "dw_ff1", 2, D, tk, N_SHARD, lambda i, j: j)

    def reduce_start(name, keys, wholes, meanwhile, also=(), also_plan=None, also_copies=0):
        n_k, n_all = len(keys), len(keys) + len(wholes)
        swap_plan = _plan_swap(n_k, len(wholes))
        crossing = [g_big[k] for k in keys] + list(wholes)
        crossing += [lax.empty(a.shape[1:] if w < n_k else a.shape, a.dtype) for w, a in enumerate(crossing)]
        send, recv, crossing, tok = _split_start("rs_swap_" + name + "_send", crossing, n_all, swap_plan)
        crossing = _split_wait("rs_swap_" + name + "_land", send, recv, crossing, swap_plan, 0, meanwhile(tok))
        mine, wholes, swapped = crossing[:n_k], crossing[n_k:n_all], crossing[n_all:]
        sums = _chip_sums(core, mine, swapped[:n_k], "chip_sums_" + name)
        arrays = list(sums)
        if wholes:
            arrays.append(_slot_sum(jnp.stack([wholes[0], swapped[-1]]), "chip_sum_small"))
        arrays += [lax.empty(s.shape, BF16) for s in sums]
        if wholes:
            arrays.append(lax.empty((N_SHARD,) + wholes[0].shape, F32))
        plan = _plan_exchange(len(keys), bool(wholes))
        n_copies, n = 3 * len(keys) + 4 * len(wholes), len(arrays)
        also = also() if callable(also) else also
        both = plan if also_plan is None else _plans((plan, n), (also_plan, len(also)))
        send, recv, out, tok = _split_start("rs_send_" + name, arrays + list(also), n_copies + also_copies, both)
        return send, recv, out[:n], plan, tok, out[n:], n_copies

    def reduce_finish(name, keys, state, *after):
        send, recv, arrays, plan = state[:4]
        arrays = _split_wait("rs_land_" + name, send, recv, arrays, plan, 0, *after)
        n_in = len(arrays) // 2
        n = len(keys)
        finished = _final_sums(chip, arrays[:n], arrays[n_in:n_in + n], "final_sums_" + name)
        return dict(zip(keys, finished)), arrays

    early = _FF
    late = ("w_in", "w_pool", "w_sgu_proj", "w_out")

    def dw_out(token):
        g_big["w_out"] = _dw(merged, dy, "dw_out", 2 * N_SHARD, D, tk, N_SHARD, lambda i, j: 0, token)
        return g_big["w_out"]

    early_state = reduce_start("early", early, [], dw_out)
    dz, g_big["w_pool"], g_big["w_sgu_proj"], dws, dbsp, vec_c, dbin = _mix_bwd(
        z, pooled, dy, w_pool, row(p["pool_scale"]), row(p["sgu_ln_g"]), row(p["sgu_ln_b"]), ws, ws_t, bsp, w_proj,
        w_out, mask, tm_mix, early_state[4])
    g_big["w_in"] = _dw(xn, dz, "dw_in", 2, D_IN // N_SHARD, tk, N_SHARD, lambda i, j: j)
    small_part = _pack(_SMALL[1:], SMALL_ROWS - 8, {
        "b_in": dbin[0], "pool_scale": vec_c[0], "sgu_ln_g": vec_c[1], "sgu_ln_b": vec_c[2],
        "w_spatial": dws, "b_spatial": dbsp[:, ::LANES].T, "norm1_post_g": vec_b[1], "norm2_pre_g": vec_b[0],
        "norm2_post_g": vec_a[0]})
    halves = {}

    def finish_early(token):
        halves.update(reduce_finish("early", early, early_state, token)[0])
        return halves[early[-1]]

    share_early = lambda: [halves[k] for k in early] + [lax.empty(halves[k].shape, F32) for k in early]
    late_state = reduce_start("late", late, [small_part], finish_early, share_early, _plan_share(len(early)),
                              len(early))
    grad_x, vec_d = _dx(dz, w_in4, x, dh1, row(p["norm1_pre_g"]), late_state[4], tm_dx)
    share = _split_wait("rs_share_early_land", late_state[0], late_state[1], late_state[5], _plan_share(len(early)),
                        late_state[6], grad_x)

    grads, delta, new_m, new_v = {}, {}, {}, {}

    def update(name, keys, own, shared):
        as2d = lambda a, k: a.reshape(local2d[k].shape)
        outs = _adamw_halves(core, own, shared, [local2d[k] for k in keys],
                             [as2d(m[k], k) for k in keys], [as2d(v[k], k) for k in keys], "adamw_" + name)
        for k, outs_k in zip(keys, outs):
            grads[k], delta[k], new_m[k], new_v[k] = (a.reshape(p[k].shape) for a in outs_k)

    update("early", early, share[:len(early)], share[len(early):])
    late_halves, late_arrays = reduce_finish("late", late, late_state, grad_x, *[new_v[k] for k in early])
    own_late = [late_halves[k] for k in late]
    shared_late, tiles = _share_and_all_reduce(
        own_late, jnp.concatenate([vec_d[0].reshape(8, LANES), jnp.broadcast_to(vec_a[1:2, :LANES], (8, LANES))]))
    g1_grad, loss = tiles[:8], tiles[8, 0]
    update("late", late, own_late, shared_late)
    as_rows = lambda d: [d[k].reshape(-1, LANES) for k in _SMALL]
    small_outs = _adamw_small([g1_grad, late_arrays[-1]], as_rows(p), as_rows(m), as_rows(v))
    for k, outs_k in zip(_SMALL, small_outs):
        grads[k], delta[k], new_m[k], new_v[k] = (a.reshape(p[k].shape) for a in outs_k)

    return (loss, grad_x, *[grads[k] for k in _ORDER], *[delta[k] for k in _ORDER],
            *[new_m[k] for k in _ORDER], *[new_v[k] for k in _ORDER])


def kernel(x, norm1_pre_g, w_in, b_in, w_pool, pool_scale, sgu_ln_g, sgu_ln_b, w_spatial, b_spatial, w_sgu_proj, w_out, norm1_post_g, norm2_pre_g, w_ff1, w_ff2, norm2_post_g, loss_target, m_norm1_pre_g, m_w_in, m_b_in, m_w_pool, m_pool_scale, m_sgu_ln_g, m_sgu_ln_b, m_w_spatial, m_b_spatial, m_w_sgu_proj, m_w_out, m_norm1_post_g, m_norm2_pre_g, m_w_ff1, m_w_ff2, m_norm2_post_g, v_norm1_pre_g, v_w_in, v_b_in, v_w_pool, v_pool_scale, v_sgu_ln_g, v_sgu_ln_b, v_w_spatial, v_b_spatial, v_w_sgu_proj, v_w_out, v_norm1_post_g, v_norm2_pre_g, v_w_ff1, v_w_ff2, v_norm2_post_g):
    p = dict(zip(_ORDER, (norm1_pre_g, w_in, b_in, w_pool, pool_scale, sgu_ln_g, sgu_ln_b, w_spatial, b_spatial,
                          w_sgu_proj, w_out, norm1_post_g, norm2_pre_g, w_ff1, w_ff2, norm2_post_g)))
    m = dict(zip(_ORDER, (m_norm1_pre_g, m_w_in, m_b_in, m_w_pool, m_pool_scale, m_sgu_ln_g, m_sgu_ln_b, m_w_spatial,
                          m_b_spatial, m_w_sgu_proj, m_w_out, m_norm1_post_g, m_norm2_pre_g, m_w_ff1, m_w_ff2,
                          m_norm2_post_g)))
    v = dict(zip(_ORDER, (v_norm1_pre_g, v_w_in, v_b_in, v_w_pool, v_pool_scale, v_sgu_ln_g, v_sgu_ln_b, v_w_spatial,
                          v_b_spatial, v_w_sgu_proj, v_w_out, v_norm1_post_g, v_norm2_pre_g, v_w_ff1, v_w_ff2,
                          v_norm2_post_g)))
    outs = _step(p, m, v, x[0], loss_target[0])
    return (outs[0], outs[1][None], *outs[2:])
```

```python
import math

import jax
import jax.numpy as jnp
from jax import lax
from jax.experimental import pallas as pl
from jax.experimental.pallas import tpu as pltpu

F32 = jnp.float32
BF16 = jnp.bfloat16

D = 1024
D_IN = 5 * D
D_FF = 4 * D
N_SHARD = 4
POOL_WINDOWS = (2, 4, 8, 16)
POOL_GROUP = 256
POOL_HALO = 16
SGU_BLOCK = 128
N_HEADS = 4
HEAD = 256
EPS = 1e-6

ADAM_LR = 0.001
ADAM_B1 = 0.9
ADAM_B2 = 0.999
ADAM_EPS = 1e-08
ADAM_WD = 0.01
ADAM_STEP = 10

V7X_VMEM_BYTES = 64 * 1024 * 1024
VMEM_LIMIT = V7X_VMEM_BYTES - 8 * 1024 * 1024
SMALL_ROWS = 616
LANES = 128

MESH = pl.DeviceIdType.MESH
ANY = pl.BlockSpec(memory_space=pl.ANY)
HBM = pl.BlockSpec(memory_space=pltpu.HBM)
SEM = pl.BlockSpec(memory_space=pltpu.SEMAPHORE)
EFFECT = pltpu.SideEffectType.DATAFLOW_SIDE_EFFECTING

NT_DIMS = (((1,), (1,)), ((), ()))
TN_DIMS = (((0,), (0,)), ((), ()))


def _params(n_axes):
    return pltpu.CompilerParams(dimension_semantics=("arbitrary",) * n_axes, vmem_limit_bytes=VMEM_LIMIT)


def _resident(shape, n_axes):
    zeros = (0,) * len(shape)
    index = (lambda i: zeros) if n_axes == 1 else (lambda i, j: zeros)
    return pl.BlockSpec(shape, index, pipeline_mode=pl.Buffered(1))


def _dot(a, b):
    return jnp.dot(a, b, preferred_element_type=F32)


def _dot_nt(a, b):
    return lax.dot_general(a, b, NT_DIMS, preferred_element_type=F32)


def _dot_tn(a, b):
    return lax.dot_general(a, b, TN_DIMS, preferred_element_type=F32)


def _mean(a):
    return jnp.mean(a, axis=-1, keepdims=True)


def _colsum(a):
    return jnp.sum(a, axis=0, keepdims=True)


_GELU_C0 = math.sqrt(2.0 / math.pi)
_GELU_C1 = 0.044715


def _gelu(a):
    t = jnp.tanh(a * (_GELU_C0 + (_GELU_C0 * _GELU_C1) * (a * a)))
    return a * (0.5 * t + 0.5)


def _gelu_and_grad(a):
    a2 = a * a
    t = jnp.tanh(a * (_GELU_C0 + (_GELU_C0 * _GELU_C1) * a2))
    cdf = 0.5 * t + 0.5
    val = a * cdf
    grad = cdf + val * (1.0 - cdf) * ((2.0 * _GELU_C0) + (6.0 * _GELU_C0 * _GELU_C1) * a2)
    return val, grad


def _sigmoid(a):
    return 0.5 * jnp.tanh(0.5 * a) + 0.5


def _rms_norm_bf16(a, g):
    return (a * lax.rsqrt(_mean(a * a) + EPS) * g).astype(BF16)


def _relu_sq_bf16(a):
    r = jnp.maximum(a, 0.0)
    return (r * r).astype(BF16)


def _window_sum(a, w, ahead):
    n = a.shape[0]
    step = 1
    while step < w:
        a = a + pltpu.roll(a, n - step if ahead else step, 0)
        step *= 2
    return a


def _rms_bwd(dn, n, r):
    return r * (dn - n * _mean(dn * n))


def _inproj_own(x, g1, w_own, b_in, chip, dep, tm):
    t = x.shape[0]
    cw = D_IN // N_SHARD

    def body(chip_ref, x_ref, g_ref, w_ref, b_ref, dep_ref, z_ref, xn_ref):
        xn = _rms_norm_bf16(x_ref[...], g_ref[...])
        xn_ref[...] = xn
        z_ref[...] = _dot(xn, w_ref[...]) + b_ref[...]

    return pl.pallas_call(
        body, name="inproj_own",
        grid_spec=pltpu.PrefetchScalarGridSpec(
            num_scalar_prefetch=1, grid=(t // tm,),
            in_specs=[pl.BlockSpec((tm, D), lambda i, chip: (i, 0)),
                      pl.BlockSpec((1, D), lambda i, chip: (0, 0)),
                      pl.BlockSpec((D, cw), lambda i, chip: (0, 0)),
                      pl.BlockSpec((1, cw), lambda i, chip: (0, chip[0])),
                      pl.BlockSpec(dep.shape, lambda i, chip: (0, 0))],
            out_specs=[pl.BlockSpec((tm, cw), lambda i, chip: (i, chip[0])),
                       pl.BlockSpec((tm, D), lambda i, chip: (i, 0))]),
        out_shape=[jax.ShapeDtypeStruct((t, D_IN), F32), jax.ShapeDtypeStruct((t, D), BF16)],
        compiler_params=_params(1),
    )(chip, x, g1, w_own, b_in, dep)


def _inproj_rest(xn, w_in4, b_in, chip, z, tm):
    t = xn.shape[0]
    cw = D_IN // N_SHARD
    shard_of = lambda s, chip: lax.rem(chip[0] + 1 + s, N_SHARD)

    def body(chip_ref, xn_ref, w_ref, b_ref, z_in_ref, z_ref):
        z_ref[...] = _dot(xn_ref[...], w_ref[shard_of(pl.program_id(1), chip_ref)]) + b_ref[...]

    return pl.pallas_call(
        body, name="inproj_rest",
        grid_spec=pltpu.PrefetchScalarGridSpec(
            num_scalar_prefetch=1, grid=(t // tm, N_SHARD - 1),
            in_specs=[pl.BlockSpec((tm, D), lambda i, s, chip: (i, 0)),
                      pl.BlockSpec((N_SHARD, D, cw), lambda i, s, chip: (0, 0, 0), pipeline_mode=pl.Buffered(1)),
                      pl.BlockSpec((1, cw), lambda i, s, chip: (0, shard_of(s, chip))),
                      ANY],
            out_specs=pl.BlockSpec((tm, cw), lambda i, s, chip: (i, shard_of(s, chip)))),
        out_shape=jax.ShapeDtypeStruct(z.shape, F32),
        input_output_aliases={4: 0},
        compiler_params=_params(2),
    )(chip, xn, w_in4, b_in, z)


def _sgu_forward(z_ref, lng_ref, lnb_ref, ws_ref, bsp_ref, sv_ref, tm, with_grad):
    if with_grad:
        gv, dgv_dz = _gelu_and_grad(z_ref[:, 2 * D:3 * D])
    else:
        gv = _gelu(z_ref[:, 2 * D:3 * D])
        dgv_dz = None
    xc = gv - _mean(gv)
    rstd = lax.rsqrt(_mean(xc * xc) + EPS)
    vhat = xc * rstd
    vb = (vhat * lng_ref[...] + lnb_ref[...]).astype(BF16)
    for b in range(tm // SGU_BLOCK):
        rows = slice(b * SGU_BLOCK, (b + 1) * SGU_BLOCK)
        for h in range(N_HEADS):
            cols = slice(h * HEAD, (h + 1) * HEAD)
            sv_ref[rows, cols] = _dot(ws_ref[h], vb[rows, cols]) + bsp_ref[h]
    return dgv_dz, vhat, rstd, vb


def _mix_fwd(z, x, w_pool, pool_scale, ln_g, ln_b, ws, bsp, w_proj, w_out, g1post, tm):
    t = x.shape[0]

    def body(z_ref, x_ref, wp_ref, psc_ref, lng_ref, lnb_ref, ws_ref, bsp_ref, wpr_ref, wo_ref, g1p_ref,
             pooled_ref, merged_ref, y_ref, h1_ref, win_ref, sv_ref):
        i = pl.program_id(0)

        @pl.when(i == 0)
        def _():
            win_ref[0:POOL_HALO, :] = jnp.zeros((POOL_HALO, D), F32)

        win_ref[POOL_HALO:POOL_HALO + tm, :] = z_ref[:, 0:D]
        pos1 = lax.broadcasted_iota(jnp.int32, (tm, 1), 0) + (i * tm + 1)
        a_parts = []
        for g, w in enumerate(POOL_WINDOWS):
            cols = slice(g * POOL_GROUP, (g + 1) * POOL_GROUP)
            ts = _window_sum(win_ref[:, cols], w, False)[POOL_HALO:POOL_HALO + tm, :]
            inv_cnt = 1.0 / jnp.minimum(pos1, w).astype(F32)
            pooled = (ts * inv_cnt - z_ref[:, cols]).astype(BF16)
            pooled_ref[:, cols] = pooled
            a_parts.append(_dot(pooled, wp_ref[g]) * psc_ref[:, cols])
        win_ref[0:POOL_HALO, :] = win_ref[tm:tm + POOL_HALO, :]

        _sgu_forward(z_ref, lng_ref, lnb_ref, ws_ref, bsp_ref, sv_ref, tm, False)
        gated = (_gelu(z_ref[:, D:2 * D]) * sv_ref[...]).astype(BF16)
        for h in range(N_HEADS):
            lo = h * HEAD
            bbr = _dot(gated[:, lo:lo + HEAD], wpr_ref[h])
            gate_a = _sigmoid(z_ref[:, 3 * D + lo:3 * D + lo + HEAD])
            gate_b = _sigmoid(z_ref[:, 4 * D + lo:4 * D + lo + HEAD])
            merged_ref[:, lo:lo + HEAD] = (gate_a * a_parts[h] + gate_b * bbr).astype(BF16)
        merged = merged_ref[...]
        y = _dot(merged, wo_ref[...])
        y_ref[...] = y
        r = lax.rsqrt(_mean(y * y) + EPS)
        h1_ref[...] = x_ref[...] + (y * r) * g1p_ref[...]

    row = lambda i: (i, 0)
    fixed2 = lambda i: (0, 0)
    fixed3 = lambda i: (0, 0, 0)
    vec = pl.BlockSpec((1, D), fixed2)
    return pl.pallas_call(
        body, name="mix_fwd",
        grid=(t // tm,),
        in_specs=[pl.BlockSpec((tm, D_IN), row), pl.BlockSpec((tm, D), row),
                  pl.BlockSpec((N_HEADS, POOL_GROUP, POOL_GROUP), fixed3), vec, vec, vec,
                  pl.BlockSpec((N_HEADS, SGU_BLOCK, SGU_BLOCK), fixed3),
                  pl.BlockSpec((N_HEADS, SGU_BLOCK, 1), fixed3),
                  pl.BlockSpec((N_HEADS, HEAD, HEAD), fixed3),
                  pl.BlockSpec((D, D), fixed2), vec],
        out_specs=[pl.BlockSpec((tm, D), row)] * 4,
        out_shape=[jax.ShapeDtypeStruct((t, D), BF16), jax.ShapeDtypeStruct((t, D), BF16),
                   jax.ShapeDtypeStruct((t, D), F32), jax.ShapeDtypeStruct((t, D), F32)],
        scratch_shapes=[pltpu.VMEM((tm + POOL_HALO, D), F32), pltpu.VMEM((tm, D), F32)],
        compiler_params=_params(1),
    )(z, x, w_pool, pool_scale, ln_g, ln_b, ws, bsp, w_proj, w_out, g1post)


def _ffn_fwd(h1, g2pre, w1, w2, g2post, target, tm):
    t = h1.shape[0]
    n_j = D_FF // D

    def body(h1_ref, g2_ref, w1_ref, w2_ref, g2p_ref, tgt_ref,
             hn_ref, f1_ref, f_ref, df2_ref, dout_ref, vec_ref):
        @pl.when(pl.program_id(0) == 0)
        def _():
            vec_ref[...] = jnp.zeros_like(vec_ref)

        hn = _rms_norm_bf16(h1_ref[...], g2_ref[...])
        hn_ref[...] = hn
        f2 = None
        for j in range(n_j):
            cols = slice(j * D, (j + 1) * D)
            f1 = _dot(hn, w1_ref[j])
            f1_ref[:, cols] = f1
            f = _relu_sq_bf16(f1)
            f_ref[:, cols] = f
            part = _dot(f, w2_ref[j])
            f2 = part if f2 is None else f2 + part
        r = lax.rsqrt(_mean(f2 * f2) + EPS)
        n2 = f2 * r
        err = h1_ref[...] + n2 * g2p_ref[...] - tgt_ref[...]
        loss = 0.5 * jnp.sum(_mean(err * err))
        dout = err * (1.0 / D)
        dout_ref[...] = dout
        vec_ref[0:1, :] += _colsum(dout * n2)
        vec_ref[1:2, :] += jnp.full((1, D), loss, F32)
        df2_ref[...] = _rms_bwd(dout * g2p_ref[...], n2, r).astype(BF16)

    row = lambda i: (i, 0)
    fixed = lambda i: (0, 0)
    vec = pl.BlockSpec((1, D), fixed)
    return pl.pallas_call(
        body, name="ffn_fwd",
        grid=(t // tm,),
        in_specs=[pl.BlockSpec((tm, D), row), vec,
                  _resident((n_j, D, D), 1), _resident((n_j, D, D), 1), vec,
                  pl.BlockSpec((tm, D), row)],
        out_specs=[pl.BlockSpec((tm, D), row),
                   pl.BlockSpec((tm, D_FF), row), pl.BlockSpec((tm, D_FF), row),
                   pl.BlockSpec((tm, D), row), pl.BlockSpec((tm, D), row),
                   pl.BlockSpec((8, D), fixed)],
        out_shape=[jax.ShapeDtypeStruct((t, D), BF16), jax.ShapeDtypeStruct((t, D_FF), F32),
                   jax.ShapeDtypeStruct((t, D_FF), BF16), jax.ShapeDtypeStruct((t, D), BF16),
                   jax.ShapeDtypeStruct((t, D), F32), jax.ShapeDtypeStruct((8, D), F32)],
        compiler_params=_params(1),
    )(h1, g2pre, w1, w2, g2post, target)


def _ffn_bwd(df2, f1, w1, w2, h1, dout, y, g2pre, g1post, tm):
    t = h1.shape[0]
    n_j = D_FF // D

    def body(df2_ref, f1_ref, w1_ref, w2_ref, h1_ref, dout_ref, y_ref, g2_ref, g1p_ref,
             df1_ref, dh1_ref, dy_ref, vec_ref):
        @pl.when(pl.program_id(0) == 0)
        def _():
            vec_ref[...] = jnp.zeros_like(vec_ref)

        df2 = df2_ref[...]
        dhn = None
        for j in range(n_j):
            cols = slice(j * D, (j + 1) * D)
            df = _dot_nt(df2, w2_ref[j])
            df1 = (df * (2.0 * jnp.maximum(f1_ref[:, cols], 0.0))).astype(BF16)
            df1_ref[:, cols] = df1
            part = _dot_nt(df1, w1_ref[j])
            dhn = part if dhn is None else dhn + part
        h = h1_ref[...]
        r2 = lax.rsqrt(_mean(h * h) + EPS)
        nh = h * r2
        vec_ref[0:1, :] += _colsum(dhn * nh)
        dh1 = dout_ref[...] + _rms_bwd(dhn * g2_ref[...], nh, r2)
        dh1_ref[...] = dh1
        yv = y_ref[...]
        r1 = lax.rsqrt(_mean(yv * yv) + EPS)
        ny = yv * r1
        vec_ref[1:2, :] += _colsum(dh1 * ny)
        dy_ref[...] = _rms_bwd(dh1 * g1p_ref[...], ny, r1).astype(BF16)

    row = lambda i: (i, 0)
    fixed = lambda i: (0, 0)
    vec = pl.BlockSpec((1, D), fixed)
    return pl.pallas_call(
        body, name="ffn_bwd",
        grid=(t // tm,),
        in_specs=[pl.BlockSpec((tm, D), row), pl.BlockSpec((tm, D_FF), row),
                  _resident((n_j, D, D), 1), _resident((n_j, D, D), 1),
                  pl.BlockSpec((tm, D), row), pl.BlockSpec((tm, D), row), pl.BlockSpec((tm, D), row), vec, vec],
        out_specs=[pl.BlockSpec((tm, D_FF), row),
                   pl.BlockSpec((tm, D), row), pl.BlockSpec((tm, D), row),
                   pl.BlockSpec((8, D), fixed)],
        out_shape=[jax.ShapeDtypeStruct((t, D_FF), BF16), jax.ShapeDtypeStruct((t, D), F32),
                   jax.ShapeDtypeStruct((t, D), BF16), jax.ShapeDtypeStruct((8, D), F32)],
        compiler_params=_params(1),
    )(df2, f1, w1, w2, h1, dout, y, g2pre, g1post)


def _mix_bwd(z, pooled, dy, w_pool, pool_scale, ln_g, ln_b, ws, ws_t, bsp, w_proj, w_out, mask, tm, dep):
    t = z.shape[0]
    n_t = t // tm
    shard_rows = POOL_GROUP // N_SHARD

    def body(z_ref, pooled_ref, dy_ref, wp_ref, psc_ref, lng_ref, lnb_ref, ws_ref, wst_ref, bsp_ref, wpr_ref,
             wo_ref, mask_ref, dep_ref,
             dz_ref, dwp_ref, dwpr_ref, dws_ref, dbsp_ref, vec_ref, dbin_ref,
             win_ref, sv_ref, dsv_ref, dv_ref, dsvsum_ref, dwp_acc, dwpr_acc):
        i = pl.program_id(0)

        @pl.when(i == 0)
        def _():
            win_ref[tm:tm + POOL_HALO, :] = jnp.zeros((POOL_HALO, D), F32)
            dwp_acc[...] = jnp.zeros_like(dwp_acc)
            dwpr_acc[...] = jnp.zeros_like(dwpr_acc)
            dws_ref[...] = jnp.zeros_like(dws_ref)
            vec_ref[...] = jnp.zeros_like(vec_ref)
            dbin_ref[...] = jnp.zeros_like(dbin_ref)
            dsvsum_ref[...] = jnp.zeros_like(dsvsum_ref)

        dmerged = _dot_nt(dy_ref[...], wo_ref[...])

        dgv_dz, vhat, rstd, vb = _sgu_forward(z_ref, lng_ref, lnb_ref, ws_ref, bsp_ref, sv_ref, tm, True)

        tile = n_t - 1 - i
        pos1 = lax.broadcasted_iota(jnp.int32, (tm, 1), 0) + (tile * tm + 1)
        def put(k, lo, part):
            cols = slice(k * D + lo, k * D + lo + part.shape[1])
            part = part.astype(BF16)
            dz_ref[:, cols] = part
            dbin_ref[:, cols] += _dot(jnp.ones((8, tm), BF16), part)

        dpooled_parts = []
        for g, w in enumerate(POOL_WINDOWS):
            cols = slice(g * POOL_GROUP, (g + 1) * POOL_GROUP)
            pg = pooled_ref[:, cols]
            a_pre = _dot(pg, wp_ref[g])
            sa_g = _sigmoid(z_ref[:, 3 * D + g * POOL_GROUP:3 * D + (g + 1) * POOL_GROUP])
            da_g = dmerged[:, cols] * sa_g
            psc = psc_ref[:, cols]
            da_a = da_g * a_pre
            vec_ref[0:1, cols] += _colsum(da_a)
            put(3, g * POOL_GROUP, da_a * psc * (1.0 - sa_g))
            da_pre = (da_g * psc).astype(BF16)
            dwp_acc[g] += _dot_tn(pg, da_pre)
            dpooled = _dot_nt(da_pre, wp_ref[g])
            dpooled_parts.append(dpooled)
            inv_cnt = 1.0 / jnp.minimum(pos1, w).astype(F32)
            win_ref[0:tm, cols] = dpooled * inv_cnt
        for g, w in enumerate(POOL_WINDOWS):
            cols = slice(g * POOL_GROUP, (g + 1) * POOL_GROUP)
            acc = _window_sum(win_ref[:, cols], w, True)[0:tm, :]
            put(0, g * POOL_GROUP, acc - dpooled_parts[g])
        win_ref[tm:tm + POOL_HALO, :] = win_ref[0:POOL_HALO, :]

        for h in range(N_HEADS):
            cols = slice(h * HEAD, (h + 1) * HEAD)
            u_h, du_dz_h = _gelu_and_grad(z_ref[:, D + h * HEAD:D + (h + 1) * HEAD])
            sv_h = sv_ref[:, cols]
            g_h = (u_h * sv_h).astype(BF16)
            sb_h = _sigmoid(z_ref[:, 4 * D + h * HEAD:4 * D + (h + 1) * HEAD])
            db_f = dmerged[:, cols] * sb_h
            db_h = db_f.astype(BF16)
            bbr_h = _dot(g_h, wpr_ref[h])
            put(4, h * HEAD, db_f * bbr_h * (1.0 - sb_h))
            dwpr_acc[h] += _dot_tn(g_h, db_h)
            dgated_h = _dot_nt(db_h, wpr_ref[h])
            dsv_ref[:, cols] = dgated_h * u_h
            put(1, h * HEAD, dgated_h * sv_h * du_dz_h)
        dsv = dsv_ref[...]
        dsvb = dsv.astype(BF16)
        blk_sum = dsv[0:SGU_BLOCK, :]
        for b in range(1, tm // SGU_BLOCK):
            blk_sum = blk_sum + dsv[b * SGU_BLOCK:(b + 1) * SGU_BLOCK, :]
        dsvsum_ref[...] += blk_sum
        for b in range(tm // SGU_BLOCK):
            rows = slice(b * SGU_BLOCK, (b + 1) * SGU_BLOCK)
            for h in range(N_HEADS):
                cols = slice(h * HEAD, (h + 1) * HEAD)
                dws_ref[h] += _dot_nt(dsvb[rows, cols], vb[rows, cols])
                dv_ref[rows, cols] = _dot(wst_ref[h], dsvb[rows, cols])
        dv = dv_ref[...]
        vec_ref[1:2, :] += _colsum(dv * vhat)
        vec_ref[2:3, :] += _colsum(dv)
        dvhat = dv * lng_ref[...]
        dgv = rstd * (dvhat - _mean(dvhat) - vhat * _mean(dvhat * vhat))
        put(2, 0, dgv * dgv_dz)

        @pl.when(i == n_t - 1)
        def _():
            for h in range(N_HEADS):
                dws_ref[h] = jnp.where(mask_ref[...] > 0.0, dws_ref[h], 0.0)
                tot = jnp.sum(dsvsum_ref[:, h * HEAD:(h + 1) * HEAD], axis=1, keepdims=True)
                dbsp_ref[:, h * LANES:(h + 1) * LANES] = jnp.broadcast_to(tot, (SGU_BLOCK, LANES))
            for acc, out in ((dwp_acc, dwp_ref), (dwpr_acc, dwpr_ref)):
                for g in range(N_HEADS):
                    for s in range(N_SHARD):
                        out[g // 2, s, (g % 2) * shard_rows:(g % 2 + 1) * shard_rows, :] = (
                            acc[g, s * shard_rows:(s + 1) * shard_rows, :].astype(BF16))

    row = lambda i: (n_t - 1 - i, 0)
    fixed2 = lambda i: (0, 0)
    fixed3 = lambda i: (0, 0, 0)
    fixed4 = lambda i: (0, 0, 0, 0)
    vec = pl.BlockSpec((1, D), fixed2)
    sq = pl.BlockSpec((N_HEADS, SGU_BLOCK, SGU_BLOCK), fixed3)
    grp = pl.BlockSpec((N_HEADS, HEAD, HEAD), fixed3)
    grp_out = pl.BlockSpec((2, N_SHARD, 2 * shard_rows, HEAD), fixed4)
    tile = pl.BlockSpec((tm, D), row)
    return pl.pallas_call(
        body, name="mix_bwd",
        grid=(n_t,),
        in_specs=[pl.BlockSpec((tm, D_IN), row), tile, tile,
                  grp, vec, vec, vec, sq, sq, pl.BlockSpec((N_HEADS, SGU_BLOCK, 1), fixed3), grp,
                  pl.BlockSpec((D, D), fixed2), pl.BlockSpec((SGU_BLOCK, SGU_BLOCK), fixed2),
                  pl.BlockSpec(dep.shape, fixed2)],
        out_specs=[pl.BlockSpec((tm, D_IN), row), grp_out, grp_out, sq,
                   pl.BlockSpec((SGU_BLOCK, N_HEADS * LANES), fixed2),
                   pl.BlockSpec((8, D), fixed2), pl.BlockSpec((8, D_IN), fixed2)],
        out_shape=[jax.ShapeDtypeStruct((t, D_IN), BF16),
                   jax.ShapeDtypeStruct((2, N_SHARD, 2 * shard_rows, HEAD), BF16),
                   jax.ShapeDtypeStruct((2, N_SHARD, 2 * shard_rows, HEAD), BF16),
                   jax.ShapeDtypeStruct((N_HEADS, SGU_BLOCK, SGU_BLOCK), F32),
                   jax.ShapeDtypeStruct((SGU_BLOCK, N_HEADS * LANES), F32),
                   jax.ShapeDtypeStruct((8, D), F32), jax.ShapeDtypeStruct((8, D_IN), F32)],
        scratch_shapes=[pltpu.VMEM((tm + POOL_HALO, D), F32), pltpu.VMEM((tm, D), F32), pltpu.VMEM((tm, D), F32),
                        pltpu.VMEM((tm, D), F32), pltpu.VMEM((SGU_BLOCK, D), F32),
                        pltpu.VMEM((N_HEADS, HEAD, HEAD), F32), pltpu.VMEM((N_HEADS, HEAD, HEAD), F32)],
        compiler_params=_params(1),
    )(z, pooled, dy, w_pool, pool_scale, ln_g, ln_b, ws, ws_t, bsp, w_proj, w_out, mask, dep)


def _dx(dz, w_in4, x, dh1, g1, dep, tm):
    t = x.shape[0]
    cw = D_IN // N_SHARD

    def body(dz_ref, w_ref, x_ref, dh1_ref, g_ref, dep_ref, gx_ref, vec_ref):
        @pl.when(pl.program_id(0) == 0)
        def _():
            vec_ref[...] = jnp.zeros_like(vec_ref)

        dxn = _dot_nt(dz_ref[:, 0:cw], w_ref[0])
        for s in range(1, N_SHARD):
            dxn = dxn + _dot_nt(dz_ref[:, s * cw:(s + 1) * cw], w_ref[s])
        xv = x_ref[...]
        r = lax.rsqrt(_mean(xv * xv) + EPS)
        xh = xv * r
        vec_ref[0:1, :] += _colsum(dxn * xh)
        gx_ref[...] = dh1_ref[...] + _rms_bwd(dxn * g_ref[...], xh, r)

    row = lambda i: (i, 0)
    fixed = lambda i: (0, 0)
    return pl.pallas_call(
        body, name="dx",
        grid=(t // tm,),
        in_specs=[pl.BlockSpec((tm, D_IN), row),
                  _resident((N_SHARD, D, cw), 1),
                  pl.BlockSpec((tm, D), row), pl.BlockSpec((tm, D), row), pl.BlockSpec((1, D), fixed),
                  pl.BlockSpec(dep.shape, fixed)],
        out_specs=[pl.BlockSpec((tm, D), row), pl.BlockSpec((8, D), fixed)],
        out_shape=[jax.ShapeDtypeStruct((t, D), F32), jax.ShapeDtypeStruct((8, D), F32)],
        compiler_params=_params(1),
    )(dz, w_in4, x, dh1, g1, dep)


def _dw(a, b, name, n_split, tn, tk, n_out, out_block, dep=None):
    t, m = a.shape
    n = b.shape[1]
    tm = min(m, D)
    rows = tm // n_split
    n_k = t // tk
    deps = [] if dep is None else [dep]

    def body(a_ref, b_ref, *rest):
        o_ref, acc_ref = rest[-2:]
        k = pl.program_id(2)

        @pl.when(k == 0)
        def _():
            acc_ref[...] = jnp.zeros_like(acc_ref)

        acc_ref[...] += _dot_tn(a_ref[...], b_ref[...])

        @pl.when(k == n_k - 1)
        def _():
            for q in range(n_split):
                o_ref[q % 2, q // 2] = acc_ref[q * rows:(q + 1) * rows, :].astype(BF16)

    return pl.pallas_call(
        body, name=name,
        grid=(m // tm, n // tn, n_k),
        in_specs=[pl.BlockSpec((tk, tm), lambda i, j, k: (k, i)),
                  pl.BlockSpec((tk, tn), lambda i, j, k: (k, j))]
        + [pl.BlockSpec(d.shape, lambda i, j, k: (0, 0)) for d in deps],
        out_specs=pl.BlockSpec((2, n_split // 2, rows, tn), lambda i, j, k: (0, out_block(i, j), 0, 0)),
        out_shape=jax.ShapeDtypeStruct((2, n_out, rows, tn), BF16),
        scratch_shapes=[pltpu.VMEM((tm, tn), F32)],
        compiler_params=_params(3),
    )(a, b, *deps)


def _row_block(rows, cols, itemsize, n_bufs):
    budget = VMEM_LIMIT // 4
    rb = rows
    while rb % 16 == 0 and rb * cols * itemsize * n_bufs * 2 > budget:
        rb //= 2
    return rb


ROW_STEPS = 2


def _chip_sums(core, grads, recvs, name):
    n = len(grads)

    def body(c_ref, *refs):
        for w in range(n):
            refs[2 * n + w][...] = (refs[w][0].astype(F32) + refs[n + w][...].astype(F32)).astype(BF16)

    def blocks(g):
        _, _, rh, cols = g.shape
        rb = rh // ROW_STEPS
        return (pl.BlockSpec((1, 1, rb, cols), lambda s, r, c: (c[0], s, r, 0)),
                pl.BlockSpec((1, rb, cols), lambda s, r, c: (s, r, 0)))

    specs = [blocks(g) for g in grads]
    return pl.pallas_call(
        body, name=name,
        grid_spec=pltpu.PrefetchScalarGridSpec(
            num_scalar_prefetch=1, grid=(N_SHARD, ROW_STEPS),
            in_specs=[s[0] for s in specs] + [s[1] for s in specs],
            out_specs=[s[1] for s in specs]),
        out_shape=[jax.ShapeDtypeStruct(g.shape[1:], BF16) for g in grads],
        compiler_params=_params(2),
    )(core, *grads, *recvs)


def _slot_sum(slots, name):
    n_s, rows, cols = slots.shape
    rb = _row_block(rows, cols, 4, n_s + 1)

    def body(s_ref, o_ref):
        acc = s_ref[0].astype(F32)
        for k in range(1, n_s):
            acc = acc + s_ref[k].astype(F32)
        o_ref[...] = acc

    return pl.pallas_call(
        body, name=name,
        grid=(rows // rb,),
        in_specs=[pl.BlockSpec((n_s, rb, cols), lambda r: (0, r, 0))],
        out_specs=pl.BlockSpec((rb, cols), lambda r: (r, 0)),
        out_shape=jax.ShapeDtypeStruct((rows, cols), F32),
        compiler_params=_params(1),
    )(slots)


def _adamw_update(g, w, m, v):
    m_new = ADAM_B1 * m + (1.0 - ADAM_B1) * g
    v_new = ADAM_B2 * v + (1.0 - ADAM_B2) * (g * g)
    m_hat = m_new / (1.0 - ADAM_B1 ** ADAM_STEP)
    v_hat = v_new / (1.0 - ADAM_B2 ** ADAM_STEP)
    return -ADAM_LR * (m_hat / (jnp.sqrt(v_hat) + ADAM_EPS) + ADAM_WD * w), m_new, v_new


def _adamw_small(grad_parts, ws, ms, vs):
    n, n_g = len(ws), len(grad_parts)

    def rows_of(ref, lo, rows):
        if len(ref.shape) == 2:
            return ref[lo:lo + rows, :]
        acc = ref[0, lo:lo + rows, :]
        for s in range(1, ref.shape[0]):
            acc = acc + ref[s, lo:lo + rows, :]
        return acc

    def body(*refs):
        g_refs, ins, outs = refs[:n_g], refs[n_g:n_g + 3 * n], refs[n_g + 3 * n:]
        part, off = 0, 0
        for k in range(n):
            rows = ws[k].shape[0]
            if off + rows > grad_parts[part].shape[-2]:
                part, off = part + 1, 0
            g = rows_of(g_refs[part], off, rows)
            off += rows
            delta, m_new, v_new = _adamw_update(g, ins[k][...], ins[n + k][...], ins[2 * n + k][...])
            for o, val in zip(outs[4 * k:4 * k + 4], (g, delta, m_new, v_new)):
                o[...] = val

    vmem = pl.BlockSpec(memory_space=pltpu.VMEM)
    outs = pl.pallas_call(
        body, name="adamw_small",
        in_specs=[vmem] * (n_g + 3 * n), out_specs=[vmem] * (4 * n),
        out_shape=[jax.ShapeDtypeStruct(w.shape, F32) for w in ws for _ in range(4)],
    )(*grad_parts, *ws, *ms, *vs)
    return [outs[4 * k:4 * k + 4] for k in range(n)]


def _final_sums(chip, owns, slots, name):
    n = len(owns)

    def body(chip_ref, *refs):
        for w in range(n):
            own_ref, s1_ref, s2_ref, s3_ref = refs[4 * w:4 * w + 4]
            acc = own_ref[0].astype(F32) + s1_ref[0].astype(F32)
            acc = acc + s2_ref[0].astype(F32)
            refs[4 * n + w][...] = acc + s3_ref[0].astype(F32)

    def slot(a, k):
        n_s, rh, cols = a.shape
        return pl.BlockSpec((1, rh // ROW_STEPS, cols), lambda r, chip: (lax.rem(chip[0] + k, n_s), r, 0))

    operands, in_specs = [], []
    for own, slots_w in zip(owns, slots):
        operands += [own, slots_w, slots_w, slots_w]
        in_specs += [slot(own, k) for k in range(4)]
    return pl.pallas_call(
        body, name=name,
        grid_spec=pltpu.PrefetchScalarGridSpec(
            num_scalar_prefetch=1, grid=(ROW_STEPS,),
            in_specs=in_specs,
            out_specs=[pl.BlockSpec((a.shape[1] // ROW_STEPS, a.shape[2]), lambda r, chip: (r, 0)) for a in owns]),
        out_shape=[jax.ShapeDtypeStruct(a.shape[1:], F32) for a in owns],
        compiler_params=_params(1),
    )(chip, *operands)


def _adamw_halves(core, owns, siblings, ws, ms, vs, name):
    n = len(ws)

    def body(c_ref, *refs):
        ins, outs = refs[:5 * n], refs[5 * n:]
        mine = c_ref[0] == pl.program_id(0)
        for k in range(n):
            own_ref, sib_ref, w_ref, m_ref, v_ref = ins[5 * k:5 * k + 5]
            g_ref, d_ref, mo_ref, vo_ref = outs[4 * k:4 * k + 4]

            @pl.when(mine)
            def _():
                g_ref[...] = own_ref[...]

            @pl.when(jnp.logical_not(mine))
            def _():
                g_ref[...] = sib_ref[...]

            d_ref[...], mo_ref[...], vo_ref[...] = _adamw_update(g_ref[...], w_ref[...], m_ref[...], v_ref[...])

    operands, in_specs, out_specs = [], [], []
    for own, sib, w, m, v in zip(owns, siblings, ws, ms, vs):
        rows, cols = w.shape
        rb = rows // 2 // ROW_STEPS
        half = pl.BlockSpec((rb, cols), lambda h, r, c: (r, 0))
        whole = pl.BlockSpec((rb, cols), lambda h, r, c: (h * ROW_STEPS + r, 0))
        operands += [own, sib, w, m, v]
        in_specs += [half, half, whole, whole, whole]
        out_specs += [whole] * 4
    outs = pl.pallas_call(
        body, name=name,
        grid_spec=pltpu.PrefetchScalarGridSpec(
            num_scalar_prefetch=1, grid=(2, ROW_STEPS), in_specs=in_specs, out_specs=out_specs),
        out_shape=[jax.ShapeDtypeStruct(w.shape, F32) for w in ws for _ in range(4)],
        compiler_params=_params(2),
    )(core, *operands)
    return [outs[4 * k:4 * k + 4] for k in range(n)]


def _place():
    x, y, c = lax.axis_index("x"), lax.axis_index("y"), lax.axis_index("c")
    chips = [(1 - x, y), (x, 1 - y), (1 - x, 1 - y)]
    return x, y, c, chips


def _remote(src, dst, send_sem, recv_sem, device):
    return pltpu.make_async_remote_copy(src_ref=src, dst_ref=dst, send_sem=send_sem, recv_sem=recv_sem,
                                        device_id=device, device_id_type=MESH)


def _split_start(name, arrays, n_copies, plan):
    n = len(arrays)

    def body(*refs):
        ins, send_sem, recv_sem, token = refs[:n], refs[n], refs[n + 1], refs[-1]
        for k, (src, dst, device) in enumerate(plan(ins)):
            _remote(src, dst, send_sem.at[k], recv_sem.at[k], device).start()
        token[...] = jnp.zeros_like(token)

    outs = pl.pallas_call(
        body, name=name,
        in_specs=[HBM] * n,
        out_specs=(SEM, SEM, *[HBM] * n, pl.BlockSpec(memory_space=pltpu.VMEM)),
        out_shape=(pltpu.SemaphoreType.DMA((n_copies,)), pltpu.SemaphoreType.DMA((n_copies,)),
                   *[pltpu.HBM(a.shape, a.dtype) for a in arrays], jax.ShapeDtypeStruct((8, LANES), F32)),
        input_output_aliases={i: i + 2 for i in range(n)},
        compiler_params=pltpu.CompilerParams(has_side_effects=EFFECT),
    )(*[pltpu.with_memory_space_constraint(a, pltpu.HBM) for a in arrays])
    return outs[0], outs[1], list(outs[2:2 + n]), outs[-1]


def _split_wait(name, send_sem, recv_sem, arrays, plan, first, *after):
    n = len(arrays)

    def body(*refs):
        ins, send_ref, recv_ref = refs[:n], refs[n], refs[n + 1]
        for k, (src, dst, device) in enumerate(plan(ins)):
            cp = _remote(src, dst, send_ref.at[first + k], recv_ref.at[first + k], device)
            cp.wait_send()
            cp.wait_recv()

    outs = pl.pallas_call(
        body, name=name,
        in_specs=[HBM] * n + [SEM, SEM] + [ANY] * len(after), out_specs=[HBM] * n,
        out_shape=[pltpu.HBM(a.shape, a.dtype) for a in arrays],
        input_output_aliases={i: i for i in range(n)},
        compiler_params=pltpu.CompilerParams(has_side_effects=EFFECT),
    )(*arrays, send_sem, recv_sem, *after)
    return list(outs)


def _plans(*parts):
    def plan(refs):
        copies, lo = [], 0
        for part, n in parts:
            copies += part(refs[lo:lo + n])
            lo += n
        return copies
    return plan


def _plan_gather_send(n):
    def plan(refs):
        x, y, c, chips = _place()
        me = 2 * x + y
        copies = []
        for w in range(n):
            own, full = refs[w], refs[n + w]
            copies.append((own, full.at[me], (x, y, 1 - c)))
            copies += [(own.at[c], full.at[me, c], (px, py, c)) for px, py in chips]
        return copies
    return plan


def _plan_gather_direct(n):
    def plan(refs):
        x, y, c, chips = _place()
        me = 2 * x + y
        copies = []
        for w in range(n):
            own, full = refs[w], refs[n + w]
            copies.append((own, full.at[me], (x, y, 1 - c)))
            for px, py in chips:
                copies += [(own.at[c], full.at[me, c], (px, py, c)), (own.at[c], full.at[me, c], (px, py, 1 - c))]
        return copies
    return plan


def _plan_gather_pass(n):
    def plan(refs):
        x, y, c, chips = _place()
        copies = []
        for w in range(n):
            for px, py in chips:
                landed = refs[w].at[2 * px + py, c]
                copies.append((landed, landed, (x, y, 1 - c)))
        return copies
    return plan


def _plan_swap(n, n_wholes):
    def plan(refs):
        x, y, c, _ = _place()
        n_all = n + n_wholes
        return [(refs[w].at[1 - c] if w < n else refs[w], refs[n_all + w], (x, y, 1 - c)) for w in range(n_all)]
    return plan


def _plan_share(n):
    def plan(refs):
        x, y, c, _ = _place()
        return [(refs[w], refs[n + w], (x, y, 1 - c)) for w in range(n)]
    return plan


def _plan_exchange(n, with_small):
    def plan(refs):
        x, y, c, chips = _place()
        me = 2 * x + y
        n_in = n + int(with_small)
        copies = []
        for w in range(n):
            copies += [(refs[w].at[2 * px + py], refs[n_in + w].at[me], (px, py, c)) for px, py in chips]
        if with_small:
            small, slots = refs[n], refs[n_in + n]
            copies.append((small, slots.at[me], (x, y, 1 - c)))
            copies += [(small, slots.at[me], (px, py, c)) for px, py in chips]
        return copies
    return plan


def _share_and_all_reduce(wholes, part):
    n = len(wholes)
    n_dev = 8

    def body(*refs):
        ins, part_ref, outs, sum_ref = refs[:n], refs[n], refs[n + 1:2 * n + 1], refs[2 * n + 1]
        slots_ref, send_sem, recv_sem, send_tile, recv_tile = refs[2 * n + 2:]
        x, y, c, _ = _place()
        me = 4 * x + 2 * y + c
        copies = []
        for w in range(n):
            copies.append(_remote(ins[w], outs[w], send_sem.at[w], recv_sem.at[w], (x, y, 1 - c)))
        slots_ref[me] = part_ref[...]
        flips = [(fx, fy, fc) for fx in (0, 1) for fy in (0, 1) for fc in (0, 1)][1:]
        for k, (fx, fy, fc) in enumerate(flips):
            peer = (x + fx - 2 * x * fx, y + fy - 2 * y * fy, c + fc - 2 * c * fc)
            copies.append(_remote(part_ref, slots_ref.at[me], send_tile.at[k], recv_tile.at[k], peer))
        for cp in copies:
            cp.start()
        for cp in copies:
            cp.wait()
        acc = slots_ref[0]
        for k in range(1, n_dev):
            acc = acc + slots_ref[k]
        sum_ref[...] = acc

    vmem = pl.BlockSpec(memory_space=pltpu.VMEM)
    outs = pl.pallas_call(
        body, name="rs_share_late",
        in_specs=[ANY] * n + [vmem], out_specs=[ANY] * n + [vmem],
        out_shape=[jax.ShapeDtypeStruct(a.shape, a.dtype) for a in wholes] + [jax.ShapeDtypeStruct(part.shape, F32)],
        scratch_shapes=[pltpu.VMEM((n_dev,) + part.shape, F32),
                        pltpu.SemaphoreType.DMA((n,)), pltpu.SemaphoreType.DMA((n,)),
                        pltpu.SemaphoreType.DMA((n_dev - 1,)), pltpu.SemaphoreType.DMA((n_dev - 1,))],
    )(*wholes, part)
    return outs[:n], outs[n]


_SMALL = ("norm1_pre_g", "b_in", "pool_scale", "sgu_ln_g", "sgu_ln_b", "w_spatial",
          "norm1_post_g", "norm2_pre_g", "norm2_post_g", "b_spatial")
_MIX = ("w_in", "w_pool", "w_sgu_proj", "w_out")
_FF = ("w_ff1", "w_ff2")
_BIG = _MIX + _FF
_ORDER = ("norm1_pre_g", "w_in", "b_in", "w_pool", "pool_scale", "sgu_ln_g", "sgu_ln_b", "w_spatial", "b_spatial",
          "w_sgu_proj", "w_out", "norm1_post_g", "norm2_pre_g", "w_ff1", "w_ff2", "norm2_post_g")


def _pack(keys, rows, parts):
    flat = jnp.concatenate([parts[k].reshape(-1).astype(F32) for k in keys])
    flat = jnp.pad(flat, (0, rows * LANES - flat.shape[0]))
    return flat.reshape(rows, LANES)


def _halves(a):
    return a.reshape(2, a.shape[0] // 2, a.shape[1])


def _step(p, m, v, x, target):
    t = x.shape[0]
    tm_mm = min(1024, t)
    tm_mix = min(256, t)
    tm_dx = min(512, t)
    core = lax.axis_index("c").astype(jnp.int32).reshape(1)
    row = lambda a: a.reshape(1, -1)

    chip = (2 * lax.axis_index("x") + lax.axis_index("y")).astype(jnp.int32).reshape(1)

    local2d = {"w_in": p["w_in"], "w_ff1": p["w_ff1"], "w_ff2": p["w_ff2"], "w_out": p["w_out"],
               "w_pool": p["w_pool"].reshape(N_HEADS * 64, POOL_GROUP),
               "w_sgu_proj": p["w_sgu_proj"].reshape(N_HEADS * 64, HEAD)}
    shard = {k: _halves(local2d[k].astype(BF16)) for k in _BIG}
    landing = lambda keys: [lax.empty((N_SHARD,) + shard[k].shape, BF16) for k in keys]
    in_arrays = [shard["w_in"]] + landing(["w_in"])
    send_a, recv_a, in_arrays, token = _split_start("ag_in_send", in_arrays, 4, _plan_gather_send(1))
    z, xn = _inproj_own(x, row(p["norm1_pre_g"]), in_arrays[0].reshape(D, D_IN // N_SHARD), row(p["b_in"]), chip,
                        token, tm_mm)
    mix_keys = _MIX[1:]
    n_mix, n_ff = len(mix_keys), len(_FF)
    in_arrays = _split_wait("ag_in_land", send_a, recv_a, in_arrays, _plan_gather_send(1), 0, z,
                            *[shard[k] for k in mix_keys + _FF])
    rest = in_arrays[1:] + [shard[k] for k in mix_keys] + landing(mix_keys) + [shard[k] for k in _FF] + landing(_FF)
    rest_plan = _plans((_plan_gather_pass(1), 1), (_plan_gather_direct(n_mix), 2 * n_mix),
                       (_plan_gather_send(n_ff), 2 * n_ff))
    send0, recv0, rest, token = _split_start("ag_rest_send", rest, 3 + 7 * n_mix + 4 * n_ff, rest_plan)
    in_full, mix_arrays, ff_arrays = rest[:1], rest[1:1 + 2 * n_mix], rest[1 + 2 * n_mix:]
    in_full = _split_wait("ag_in_passed", send0, recv0, in_full, _plan_gather_pass(1), 0, token)
    w_in4 = in_full[0].reshape(N_SHARD, D, D_IN // N_SHARD)

    pos = jnp.arange(SGU_BLOCK) // 64
    mask = (pos[:, None] >= pos[None, :]).astype(F32)
    ws = (p["w_spatial"] * mask[None]).astype(BF16)
    ws_t = jnp.swapaxes(ws, 1, 2)
    bsp = p["b_spatial"].reshape(N_HEADS, SGU_BLOCK, 1)

    z = _inproj_rest(xn, w_in4, row(p["b_in"]), chip, z, tm_mm)
    mix_arrays = _split_wait("ag_mix_land", send0, recv0, mix_arrays, _plan_gather_direct(n_mix), 3, z)
    full = dict(zip(mix_keys, mix_arrays[len(mix_keys):]))
    w_out = full["w_out"].reshape(D, D)
    regroup = lambda a: a.reshape(N_SHARD, N_HEADS, 64, 256).transpose(1, 0, 2, 3).reshape(N_HEADS, 256, 256)
    w_pool = regroup(full["w_pool"])
    w_proj = regroup(full["w_sgu_proj"])
    pooled, merged, y, h1 = _mix_fwd(z, x, w_pool, row(p["pool_scale"]), row(p["sgu_ln_g"]), row(p["sgu_ln_b"]),
                                     ws, bsp, w_proj, w_out, row(p["norm1_post_g"]), tm_mix)
    ff_arrays = _split_wait("ag_ff_land", send0, recv0, ff_arrays, _plan_gather_send(n_ff), 3 + 7 * n_mix, pooled)
    send2, recv2, ff_full, token = _split_start("ag_ff_pass", ff_arrays[n_ff:], 3 * n_ff, _plan_gather_pass(n_ff))
    ff_full = _split_wait("ag_ff_passed", send2, recv2, ff_full, _plan_gather_pass(n_ff), 0, token)
    w_ff1_4 = ff_full[0].reshape(N_SHARD, D, D)
    w_ff2 = ff_full[1].reshape(N_SHARD, D, D)
    hn, f1, f, df2, dout, vec_a = _ffn_fwd(h1, row(p["norm2_pre_g"]), w_ff1_4, w_ff2, row(p["norm2_post_g"]),
                                           target, tm_mix)

    tk = min(2048, t)
    g_big = {"w_ff2": _dw(f, df2, "dw_ff2", 2, D, tk, N_SHARD, lambda i, j: i)}
    df1, dh1, dy, vec_b = _ffn_bwd(df2, f1, w_ff1_4, w_ff2, h1, dout, y, row(p["norm2_pre_g"]),
                                   row(p["norm1_post_g"]), tm_mix)
    g_big["w_ff1"] = _dw(hn, df1, "dw_ff1", 2, D, tk, N_SHARD, lambda i, j: j)

    def reduce_start(name, keys, wholes, meanwhile, also=(), also_plan=None, also_copies=0):
        n_k, n_all = len(keys), len(keys) + len(wholes)
        swap_plan = _plan_swap(n_k, len(wholes))
        crossing = [g_big[k] for k in keys] + list(wholes)
        crossing += [lax.empty(a.shape[1:] if w < n_k else a.shape, a.dtype) for w, a in enumerate(crossing)]
        send, recv, crossing, tok = _split_start("rs_swap_" + name + "_send", crossing, n_all, swap_plan)
        crossing = _split_wait("rs_swap_" + name + "_land", send, recv, crossing, swap_plan, 0, meanwhile(tok))
        mine, wholes, swapped = crossing[:n_k], crossing[n_k:n_all], crossing[n_all:]
        sums = _chip_sums(core, mine, swapped[:n_k], "chip_sums_" + name)
        arrays = list(sums)
        if wholes:
            arrays.append(_slot_sum(jnp.stack([wholes[0], swapped[-1]]), "chip_sum_small"))
        arrays += [lax.empty(s.shape, BF16) for s in sums]
        if wholes:
            arrays.append(lax.empty((N_SHARD,) + wholes[0].shape, F32))
        plan = _plan_exchange(len(keys), bool(wholes))
        n_copies, n = 3 * len(keys) + 4 * len(wholes), len(arrays)
        also = also() if callable(also) else also
        both = plan if also_plan is None else _plans((plan, n), (also_plan, len(also)))
        send, recv, out, tok = _split_start("rs_send_" + name, arrays + list(also), n_copies + also_copies, both)
        return send, recv, out[:n], plan, tok, out[n:], n_copies

    def reduce_finish(name, keys, state, *after):
        send, recv, arrays, plan = state[:4]
        arrays = _split_wait("rs_land_" + name, send, recv, arrays, plan, 0, *after)
        n_in = len(arrays) // 2
        n = len(keys)
        finished = _final_sums(chip, arrays[:n], arrays[n_in:n_in + n], "final_sums_" + name)
        return dict(zip(keys, finished)), arrays

    early = _FF
    late = ("w_in", "w_pool", "w_sgu_proj", "w_out")

    def dw_out(token):
        g_big["w_out"] = _dw(merged, dy, "dw_out", 2 * N_SHARD, D, tk, N_SHARD, lambda i, j: 0, token)
        return g_big["w_out"]

    early_state = reduce_start("early", early, [], dw_out)
    dz, g_big["w_pool"], g_big["w_sgu_proj"], dws, dbsp, vec_c, dbin = _mix_bwd(
        z, pooled, dy, w_pool, row(p["pool_scale"]), row(p["sgu_ln_g"]), row(p["sgu_ln_b"]), ws, ws_t, bsp, w_proj,
        w_out, mask, tm_mix, early_state[4])
    g_big["w_in"] = _dw(xn, dz, "dw_in", 2, D_IN // N_SHARD, tk, N_SHARD, lambda i, j: j)
    small_part = _pack(_SMALL[1:], SMALL_ROWS - 8, {
        "b_in": dbin[0], "pool_scale": vec_c[0], "sgu_ln_g": vec_c[1], "sgu_ln_b": vec_c[2],
        "w_spatial": dws, "b_spatial": dbsp[:, ::LANES].T, "norm1_post_g": vec_b[1], "norm2_pre_g": vec_b[0],
        "norm2_post_g": vec_a[0]})
    halves = {}

    def finish_early(token):
        halves.update(reduce_finish("early", early, early_state, token)[0])
        return halves[early[-1]]

    share_early = lambda: [halves[k] for k in early] + [lax.empty(halves[k].shape, F32) for k in early]
    late_state = reduce_start("late", late, [small_part], finish_early, share_early, _plan_share(len(early)),
                              len(early))
    grad_x, vec_d = _dx(dz, w_in4, x, dh1, row(p["norm1_pre_g"]), late_state[4], tm_dx)
    share = _split_wait("rs_share_early_land", late_state[0], late_state[1], late_state[5], _plan_share(len(early)),
                        late_state[6], grad_x)

    grads, delta, new_m, new_v = {}, {}, {}, {}

    def update(name, keys, own, shared):
        as2d = lambda a, k: a.reshape(local2d[k].shape)
        outs = _adamw_halves(core, own, shared, [local2d[k] for k in keys],
                             [as2d(m[k], k) for k in keys], [as2d(v[k], k) for k in keys], "adamw_" + name)
        for k, outs_k in zip(keys, outs):
            grads[k], delta[k], new_m[k], new_v[k] = (a.reshape(p[k].shape) for a in outs_k)

    update("early", early, share[:len(early)], share[len(early):])
    late_halves, late_arrays = reduce_finish("late", late, late_state, grad_x, *[new_v[k] for k in early])
    own_late = [late_halves[k] for k in late]
    shared_late, tiles = _share_and_all_reduce(
        own_late, jnp.concatenate([vec_d[0].reshape(8, LANES), jnp.broadcast_to(vec_a[1:2, :LANES], (8, LANES))]))
    g1_grad, loss = tiles[:8], tiles[8, 0]
    update("late", late, own_late, shared_late)
    as_rows = lambda d: [d[k].reshape(-1, LANES) for k in _SMALL]
    small_outs = _adamw_small([g1_grad, late_arrays[-1]], as_rows(p), as_rows(m), as_rows(v))
    for k, outs_k in zip(_SMALL, small_outs):
        grads[k], delta[k], new_m[k], new_v[k] = (a.reshape(p[k].shape) for a in outs_k)

    return (loss, grad_x, *[grads[k] for k in _ORDER], *[delta[k] for k in _ORDER],
            *[new_m[k] for k in _ORDER], *[new_v[k] for k in _ORDER])


def kernel(x, norm1_pre_g, w_in, b_in, w_pool, pool_scale, sgu_ln_g, sgu_ln_b, w_spatial, b_spatial, w_sgu_proj, w_out, norm1_post_g, norm2_pre_g, w_ff1, w_ff2, norm2_post_g, loss_target, m_norm1_pre_g, m_w_in, m_b_in, m_w_pool, m_pool_scale, m_sgu_ln_g, m_sgu_ln_b, m_w_spatial, m_b_spatial, m_w_sgu_proj, m_w_out, m_norm1_post_g, m_norm2_pre_g, m_w_ff1, m_w_ff2, m_norm2_post_g, v_norm1_pre_g, v_w_in, v_b_in, v_w_pool, v_pool_scale, v_sgu_ln_g, v_sgu_ln_b, v_w_spatial, v_b_spatial, v_w_sgu_proj, v_w_out, v_norm1_post_g, v_norm2_pre_g, v_w_ff1, v_w_ff2, v_norm2_post_g):
    p = dict(zip(_ORDER, (norm1_pre_g, w_in, b_in, w_pool, pool_scale, sgu_ln_g, sgu_ln_b, w_spatial, b_spatial,
                          w_sgu_proj, w_out, norm1_post_g, norm2_pre_g, w_ff1, w_ff2, norm2_post_g)))
    m = dict(zip(_ORDER, (m_norm1_pre_g, m_w_in, m_b_in, m_w_pool, m_pool_scale, m_sgu_ln_g, m_sgu_ln_b, m_w_spatial,
                          m_b_spatial, m_w_sgu_proj, m_w_out, m_norm1_post_g, m_norm2_pre_g, m_w_ff1, m_w_ff2,
                          m_norm2_post_g)))
    v = dict(zip(_ORDER, (v_norm1_pre_g, v_w_in, v_b_in, v_w_pool, v_pool_scale, v_sgu_ln_g, v_sgu_ln_b, v_w_spatial,
                          v_b_spatial, v_w_sgu_proj, v_w_out, v_norm1_post_g, v_norm2_pre_g, v_w_ff1, v_w_ff2,
                          v_norm2_post_g)))
    outs = _step(p, m, v, x[0], loss_target[0])
    return (outs[0], outs[1][None], *outs[2:])
```

```python
import math

import jax
import jax.numpy as jnp
from jax import lax
from jax.experimental import pallas as pl
from jax.experimental.pallas import tpu as pltpu

F32 = jnp.float32
BF16 = jnp.bfloat16

D = 1024
D_IN = 5 * D
D_FF = 4 * D
N_SHARD = 4
POOL_WINDOWS = (2, 4, 8, 16)
POOL_GROUP = 256
POOL_HALO = 16
SGU_BLOCK = 128
N_HEADS = 4
HEAD = 256
EPS = 1e-6

ADAM_LR = 0.001
ADAM_B1 = 0.9
ADAM_B2 = 0.999
ADAM_EPS = 1e-08
ADAM_WD = 0.01
ADAM_STEP = 10

V7X_VMEM_BYTES = 64 * 1024 * 1024
VMEM_LIMIT = V7X_VMEM_BYTES - 8 * 1024 * 1024
SMALL_ROWS = 616
LANES = 128

MESH = pl.DeviceIdType.MESH
ANY = pl.BlockSpec(memory_space=pl.ANY)
HBM = pl.BlockSpec(memory_space=pltpu.HBM)
SEM = pl.BlockSpec(memory_space=pltpu.SEMAPHORE)
EFFECT = pltpu.SideEffectType.DATAFLOW_SIDE_EFFECTING

NT_DIMS = (((1,), (1,)), ((), ()))
TN_DIMS = (((0,), (0,)), ((), ()))


def _params(n_axes):
    return pltpu.CompilerParams(dimension_semantics=("arbitrary",) * n_axes, vmem_limit_bytes=VMEM_LIMIT)


def _resident(shape, n_axes):
    zeros = (0,) * len(shape)
    index = (lambda i: zeros) if n_axes == 1 else (lambda i, j: zeros)
    return pl.BlockSpec(shape, index, pipeline_mode=pl.Buffered(1))


def _dot(a, b):
    return jnp.dot(a, b, preferred_element_type=F32)


def _dot_nt(a, b):
    return lax.dot_general(a, b, NT_DIMS, preferred_element_type=F32)


def _dot_tn(a, b):
    return lax.dot_general(a, b, TN_DIMS, preferred_element_type=F32)


def _mean(a):
    return jnp.mean(a, axis=-1, keepdims=True)


def _colsum(a):
    return jnp.sum(a, axis=0, keepdims=True)


_GELU_C0 = math.sqrt(2.0 / math.pi)
_GELU_C1 = 0.044715


def _gelu(a):
    t = jnp.tanh(a * (_GELU_C0 + (_GELU_C0 * _GELU_C1) * (a * a)))
    return a * (0.5 * t + 0.5)


def _gelu_and_grad(a):
    a2 = a * a
    t = jnp.tanh(a * (_GELU_C0 + (_GELU_C0 * _GELU_C1) * a2))
    cdf = 0.5 * t + 0.5
    val = a * cdf
    grad = cdf + val * (1.0 - cdf) * ((2.0 * _GELU_C0) + (6.0 * _GELU_C0 * _GELU_C1) * a2)
    return val, grad


def _sigmoid(a):
    return 0.5 * jnp.tanh(0.5 * a) + 0.5


def _rms_norm_bf16(a, g):
    return (a * lax.rsqrt(_mean(a * a) + EPS) * g).astype(BF16)


def _relu_sq_bf16(a):
    r = jnp.maximum(a, 0.0)
    return (r * r).astype(BF16)


def _window_sum(a, w, ahead):
    n = a.shape[0]
    step = 1
    while step < w:
        a = a + pltpu.roll(a, n - step if ahead else step, 0)
        step *= 2
    return a


def _rms_bwd(dn, n, r):
    return r * (dn - n * _mean(dn * n))


def _inproj_own(x, g1, w_own, b_in, chip, dep, tm):
    t = x.shape[0]
    cw = D_IN // N_SHARD

    def body(chip_ref, x_ref, g_ref, w_ref, b_ref, dep_ref, z_ref, xn_ref):
        xn = _rms_norm_bf16(x_ref[...], g_ref[...])
        xn_ref[...] = xn
        z_ref[...] = _dot(xn, w_ref[...]) + b_ref[...]

    return pl.pallas_call(
        body, name="inproj_own",
        grid_spec=pltpu.PrefetchScalarGridSpec(
            num_scalar_prefetch=1, grid=(t // tm,),
            in_specs=[pl.BlockSpec((tm, D), lambda i, chip: (i, 0)),
                      pl.BlockSpec((1, D), lambda i, chip: (0, 0)),
                      pl.BlockSpec((D, cw), lambda i, chip: (0, 0)),
                      pl.BlockSpec((1, cw), lambda i, chip: (0, chip[0])),
                      pl.BlockSpec(dep.shape, lambda i, chip: (0, 0))],
            out_specs=[pl.BlockSpec((tm, cw), lambda i, chip: (i, chip[0])),
                       pl.BlockSpec((tm, D), lambda i, chip: (i, 0))]),
        out_shape=[jax.ShapeDtypeStruct((t, D_IN), F32), jax.ShapeDtypeStruct((t, D), BF16)],
        compiler_params=_params(1),
    )(chip, x, g1, w_own, b_in, dep)


def _inproj_rest(xn, w_in4, b_in, chip, z, tm):
    t = xn.shape[0]
    cw = D_IN // N_SHARD
    shard_of = lambda s, chip: lax.rem(chip[0] + 1 + s, N_SHARD)

    def body(chip_ref, xn_ref, w_ref, b_ref, z_in_ref, z_ref):
        z_ref[...] = _dot(xn_ref[...], w_ref[shard_of(pl.program_id(1), chip_ref)]) + b_ref[...]

    return pl.pallas_call(
        body, name="inproj_rest",
        grid_spec=pltpu.PrefetchScalarGridSpec(
            num_scalar_prefetch=1, grid=(t // tm, N_SHARD - 1),
            in_specs=[pl.BlockSpec((tm, D), lambda i, s, chip: (i, 0)),
                      pl.BlockSpec((N_SHARD, D, cw), lambda i, s, chip: (0, 0, 0), pipeline_mode=pl.Buffered(1)),
                      pl.BlockSpec((1, cw), lambda i, s, chip: (0, shard_of(s, chip))),
                      ANY],
            out_specs=pl.BlockSpec((tm, cw), lambda i, s, chip: (i, shard_of(s, chip)))),
        out_shape=jax.ShapeDtypeStruct(z.shape, F32),
        input_output_aliases={4: 0},
        compiler_params=_params(2),
    )(chip, xn, w_in4, b_in, z)


def _blocks_side_by_side(a, cols, tm):
    return jnp.concatenate([a[b * SGU_BLOCK:(b + 1) * SGU_BLOCK, cols] for b in range(tm // SGU_BLOCK)], axis=1)


def _sgu_forward(z_ref, lng_ref, lnb_ref, ws_ref, bsp_ref, sv_ref, tm, with_grad):
    if with_grad:
        gv, dgv_dz = _gelu_and_grad(z_ref[:, 2 * D:3 * D])
    else:
        gv = _gelu(z_ref[:, 2 * D:3 * D])
        dgv_dz = None
    xc = gv - _mean(gv)
    rstd = lax.rsqrt(_mean(xc * xc) + EPS)
    vhat = xc * rstd
    vb = (vhat * lng_ref[...] + lnb_ref[...]).astype(BF16)
    for h in range(N_HEADS):
        cols = slice(h * HEAD, (h + 1) * HEAD)
        sv = _dot(ws_ref[h], _blocks_side_by_side(vb, cols, tm)) + bsp_ref[h]
        for b in range(tm // SGU_BLOCK):
            sv_ref[b * SGU_BLOCK:(b + 1) * SGU_BLOCK, cols] = sv[:, b * HEAD:(b + 1) * HEAD]
    return dgv_dz, vhat, rstd, vb


def _mix_fwd(z, x, w_pool, pool_scale, ln_g, ln_b, ws, bsp, w_proj, w_out, g1post, tm):
    t = x.shape[0]

    def body(z_ref, x_ref, wp_ref, psc_ref, lng_ref, lnb_ref, ws_ref, bsp_ref, wpr_ref, wo_ref, g1p_ref,
             pooled_ref, merged_ref, y_ref, h1_ref, win_ref, sv_ref):
        i = pl.program_id(0)

        @pl.when(i == 0)
        def _():
            win_ref[0:POOL_HALO, :] = jnp.zeros((POOL_HALO, D), F32)

        win_ref[POOL_HALO:POOL_HALO + tm, :] = z_ref[:, 0:D]
        pos1 = lax.broadcasted_iota(jnp.int32, (tm, 1), 0) + (i * tm + 1)
        a_parts = []
        for g, w in enumerate(POOL_WINDOWS):
            cols = slice(g * POOL_GROUP, (g + 1) * POOL_GROUP)
            ts = _window_sum(win_ref[:, cols], w, False)[POOL_HALO:POOL_HALO + tm, :]
            inv_cnt = 1.0 / jnp.minimum(pos1, w).astype(F32)
            pooled = (ts * inv_cnt - z_ref[:, cols]).astype(BF16)
            pooled_ref[:, cols] = pooled
            a_parts.append(_dot(pooled, wp_ref[g]) * psc_ref[:, cols])
        win_ref[0:POOL_HALO, :] = win_ref[tm:tm + POOL_HALO, :]

        _sgu_forward(z_ref, lng_ref, lnb_ref, ws_ref, bsp_ref, sv_ref, tm, False)
        gated = (_gelu(z_ref[:, D:2 * D]) * sv_ref[...]).astype(BF16)
        for h in range(N_HEADS):
            lo = h * HEAD
            bbr = _dot(gated[:, lo:lo + HEAD], wpr_ref[h])
            gate_a = _sigmoid(z_ref[:, 3 * D + lo:3 * D + lo + HEAD])
            gate_b = _sigmoid(z_ref[:, 4 * D + lo:4 * D + lo + HEAD])
            merged_ref[:, lo:lo + HEAD] = (gate_a * a_parts[h] + gate_b * bbr).astype(BF16)
        merged = merged_ref[...]
        y = _dot(merged, wo_ref[...])
        y_ref[...] = y
        r = lax.rsqrt(_mean(y * y) + EPS)
        h1_ref[...] = x_ref[...] + (y * r) * g1p_ref[...]

    row = lambda i: (i, 0)
    fixed2 = lambda i: (0, 0)
    fixed3 = lambda i: (0, 0, 0)
    vec = pl.BlockSpec((1, D), fixed2)
    return pl.pallas_call(
        body, name="mix_fwd",
        grid=(t // tm,),
        in_specs=[pl.BlockSpec((tm, D_IN), row), pl.BlockSpec((tm, D), row),
                  pl.BlockSpec((N_HEADS, POOL_GROUP, POOL_GROUP), fixed3), vec, vec, vec,
                  pl.BlockSpec((N_HEADS, SGU_BLOCK, SGU_BLOCK), fixed3),
                  pl.BlockSpec((N_HEADS, SGU_BLOCK, 1), fixed3),
                  pl.BlockSpec((N_HEADS, HEAD, HEAD), fixed3),
                  pl.BlockSpec((D, D), fixed2), vec],
        out_specs=[pl.BlockSpec((tm, D), row)] * 4,
        out_shape=[jax.ShapeDtypeStruct((t, D), BF16), jax.ShapeDtypeStruct((t, D), BF16),
                   jax.ShapeDtypeStruct((t, D), F32), jax.ShapeDtypeStruct((t, D), F32)],
        scratch_shapes=[pltpu.VMEM((tm + POOL_HALO, D), F32), pltpu.VMEM((tm, D), F32)],
        compiler_params=_params(1),
    )(z, x, w_pool, pool_scale, ln_g, ln_b, ws, bsp, w_proj, w_out, g1post)


def _ffn_fwd(h1, g2pre, w1, w2, g2post, target, tm):
    t = h1.shape[0]
    n_j = D_FF // D

    def body(h1_ref, g2_ref, w1_ref, w2_ref, g2p_ref, tgt_ref,
             hn_ref, f1_ref, f_ref, df2_ref, dout_ref, vec_ref):
        @pl.when(pl.program_id(0) == 0)
        def _():
            vec_ref[...] = jnp.zeros_like(vec_ref)

        hn = _rms_norm_bf16(h1_ref[...], g2_ref[...])
        hn_ref[...] = hn
        f2 = None
        for j in range(n_j):
            cols = slice(j * D, (j + 1) * D)
            f1 = _dot(hn, w1_ref[j])
            f1_ref[:, cols] = f1
            f = _relu_sq_bf16(f1)
            f_ref[:, cols] = f
            part = _dot(f, w2_ref[j])
            f2 = part if f2 is None else f2 + part
        r = lax.rsqrt(_mean(f2 * f2) + EPS)
        n2 = f2 * r
        err = h1_ref[...] + n2 * g2p_ref[...] - tgt_ref[...]
        loss = 0.5 * jnp.sum(_mean(err * err))
        dout = err * (1.0 / D)
        dout_ref[...] = dout
        vec_ref[0:1, :] += _colsum(dout * n2)
        vec_ref[1:2, :] += jnp.full((1, D), loss, F32)
        df2_ref[...] = _rms_bwd(dout * g2p_ref[...], n2, r).astype(BF16)

    row = lambda i: (i, 0)
    fixed = lambda i: (0, 0)
    vec = pl.BlockSpec((1, D), fixed)
    return pl.pallas_call(
        body, name="ffn_fwd",
        grid=(t // tm,),
        in_specs=[pl.BlockSpec((tm, D), row), vec,
                  _resident((n_j, D, D), 1), _resident((n_j, D, D), 1), vec,
                  pl.BlockSpec((tm, D), row)],
        out_specs=[pl.BlockSpec((tm, D), row),
                   pl.BlockSpec((tm, D_FF), row), pl.BlockSpec((tm, D_FF), row),
                   pl.BlockSpec((tm, D), row), pl.BlockSpec((tm, D), row),
                   pl.BlockSpec((8, D), fixed)],
        out_shape=[jax.ShapeDtypeStruct((t, D), BF16), jax.ShapeDtypeStruct((t, D_FF), F32),
                   jax.ShapeDtypeStruct((t, D_FF), BF16), jax.ShapeDtypeStruct((t, D), BF16),
                   jax.ShapeDtypeStruct((t, D), F32), jax.ShapeDtypeStruct((8, D), F32)],
        compiler_params=_params(1),
    )(h1, g2pre, w1, w2, g2post, target)


def _ffn_bwd(df2, f1, w1, w2, h1, dout, y, g2pre, g1post, tm):
    t = h1.shape[0]
    n_j = D_FF // D

    def body(df2_ref, f1_ref, w1_ref, w2_ref, h1_ref, dout_ref, y_ref, g2_ref, g1p_ref,
             df1_ref, dh1_ref, dy_ref, vec_ref):
        @pl.when(pl.program_id(0) == 0)
        def _():
            vec_ref[...] = jnp.zeros_like(vec_ref)

        df2 = df2_ref[...]
        dhn = None
        for j in range(n_j):
            cols = slice(j * D, (j + 1) * D)
            df = _dot_nt(df2, w2_ref[j])
            df1 = (df * (2.0 * jnp.maximum(f1_ref[:, cols], 0.0))).astype(BF16)
            df1_ref[:, cols] = df1
            part = _dot_nt(df1, w1_ref[j])
            dhn = part if dhn is None else dhn + part
        h = h1_ref[...]
        r2 = lax.rsqrt(_mean(h * h) + EPS)
        nh = h * r2
        vec_ref[0:1, :] += _colsum(dhn * nh)
        dh1 = dout_ref[...] + _rms_bwd(dhn * g2_ref[...], nh, r2)
        dh1_ref[...] = dh1
        yv = y_ref[...]
        r1 = lax.rsqrt(_mean(yv * yv) + EPS)
        ny = yv * r1
        vec_ref[1:2, :] += _colsum(dh1 * ny)
        dy_ref[...] = _rms_bwd(dh1 * g1p_ref[...], ny, r1).astype(BF16)

    row = lambda i: (i, 0)
    fixed = lambda i: (0, 0)
    vec = pl.BlockSpec((1, D), fixed)
    return pl.pallas_call(
        body, name="ffn_bwd",
        grid=(t // tm,),
        in_specs=[pl.BlockSpec((tm, D), row), pl.BlockSpec((tm, D_FF), row),
                  _resident((n_j, D, D), 1), _resident((n_j, D, D), 1),
                  pl.BlockSpec((tm, D), row), pl.BlockSpec((tm, D), row), pl.BlockSpec((tm, D), row), vec, vec],
        out_specs=[pl.BlockSpec((tm, D_FF), row),
                   pl.BlockSpec((tm, D), row), pl.BlockSpec((tm, D), row),
                   pl.BlockSpec((8, D), fixed)],
        out_shape=[jax.ShapeDtypeStruct((t, D_FF), BF16), jax.ShapeDtypeStruct((t, D), F32),
                   jax.ShapeDtypeStruct((t, D), BF16), jax.ShapeDtypeStruct((8, D), F32)],
        compiler_params=_params(1),
    )(df2, f1, w1, w2, h1, dout, y, g2pre, g1post)


def _mix_bwd(z, pooled, dy, w_pool, pool_scale, ln_g, ln_b, ws, ws_t, bsp, w_proj, w_out, mask, tm, dep):
    t = z.shape[0]
    n_t = t // tm
    shard_rows = POOL_GROUP // N_SHARD

    def body(z_ref, pooled_ref, dy_ref, wp_ref, psc_ref, lng_ref, lnb_ref, ws_ref, wst_ref, bsp_ref, wpr_ref,
             wo_ref, mask_ref, dep_ref,
             dz_ref, dwp_ref, dwpr_ref, dws_ref, dbsp_ref, vec_ref, dbin_ref,
             win_ref, sv_ref, dsv_ref, dv_ref, dsvsum_ref, dwp_acc, dwpr_acc):
        i = pl.program_id(0)

        @pl.when(i == 0)
        def _():
            win_ref[tm:tm + POOL_HALO, :] = jnp.zeros((POOL_HALO, D), F32)
            dwp_acc[...] = jnp.zeros_like(dwp_acc)
            dwpr_acc[...] = jnp.zeros_like(dwpr_acc)
            dws_ref[...] = jnp.zeros_like(dws_ref)
            vec_ref[...] = jnp.zeros_like(vec_ref)
            dbin_ref[...] = jnp.zeros_like(dbin_ref)
            dsvsum_ref[...] = jnp.zeros_like(dsvsum_ref)

        dmerged = _dot_nt(dy_ref[...], wo_ref[...])

        dgv_dz, vhat, rstd, vb = _sgu_forward(z_ref, lng_ref, lnb_ref, ws_ref, bsp_ref, sv_ref, tm, True)

        tile = n_t - 1 - i
        pos1 = lax.broadcasted_iota(jnp.int32, (tm, 1), 0) + (tile * tm + 1)
        def put(k, lo, part):
            cols = slice(k * D + lo, k * D + lo + part.shape[1])
            dz_ref[:, cols] = part.astype(BF16)
            dbin_ref[0:1, cols] += _colsum(part)

        dpooled_parts = []
        for g, w in enumerate(POOL_WINDOWS):
            cols = slice(g * POOL_GROUP, (g + 1) * POOL_GROUP)
            pg = pooled_ref[:, cols]
            a_pre = _dot(pg, wp_ref[g])
            sa_g = _sigmoid(z_ref[:, 3 * D + g * POOL_GROUP:3 * D + (g + 1) * POOL_GROUP])
            da_g = dmerged[:, cols] * sa_g
            psc = psc_ref[:, cols]
            da_a = da_g * a_pre
            vec_ref[0:1, cols] += _colsum(da_a)
            put(3, g * POOL_GROUP, da_a * psc * (1.0 - sa_g))
            da_pre = (da_g * psc).astype(BF16)
            dwp_acc[g] += _dot_tn(pg, da_pre)
            dpooled = _dot_nt(da_pre, wp_ref[g])
            dpooled_parts.append(dpooled)
            inv_cnt = 1.0 / jnp.minimum(pos1, w).astype(F32)
            win_ref[0:tm, cols] = dpooled * inv_cnt
        for g, w in enumerate(POOL_WINDOWS):
            cols = slice(g * POOL_GROUP, (g + 1) * POOL_GROUP)
            acc = _window_sum(win_ref[:, cols], w, True)[0:tm, :]
            put(0, g * POOL_GROUP, acc - dpooled_parts[g])
        win_ref[tm:tm + POOL_HALO, :] = win_ref[0:POOL_HALO, :]

        for h in range(N_HEADS):
            cols = slice(h * HEAD, (h + 1) * HEAD)
            u_h, du_dz_h = _gelu_and_grad(z_ref[:, D + h * HEAD:D + (h + 1) * HEAD])
            sv_h = sv_ref[:, cols]
            g_h = (u_h * sv_h).astype(BF16)
            sb_h = _sigmoid(z_ref[:, 4 * D + h * HEAD:4 * D + (h + 1) * HEAD])
            db_f = dmerged[:, cols] * sb_h
            db_h = db_f.astype(BF16)
            bbr_h = _dot(g_h, wpr_ref[h])
            put(4, h * HEAD, db_f * bbr_h * (1.0 - sb_h))
            dwpr_acc[h] += _dot_tn(g_h, db_h)
            dgated_h = _dot_nt(db_h, wpr_ref[h])
            dsv_ref[:, cols] = dgated_h * u_h
            put(1, h * HEAD, dgated_h * sv_h * du_dz_h)
        dsv = dsv_ref[...]
        dsvb = dsv.astype(BF16)
        blk_sum = dsv[0:SGU_BLOCK, :]
        for b in range(1, tm // SGU_BLOCK):
            blk_sum = blk_sum + dsv[b * SGU_BLOCK:(b + 1) * SGU_BLOCK, :]
        dsvsum_ref[...] += blk_sum
        for h in range(N_HEADS):
            cols = slice(h * HEAD, (h + 1) * HEAD)
            dsv_h = _blocks_side_by_side(dsvb, cols, tm)
            dws_ref[h] += _dot_nt(dsv_h, _blocks_side_by_side(vb, cols, tm))
            dv_h = _dot(wst_ref[h], dsv_h)
            for b in range(tm // SGU_BLOCK):
                dv_ref[b * SGU_BLOCK:(b + 1) * SGU_BLOCK, cols] = dv_h[:, b * HEAD:(b + 1) * HEAD]
        dv = dv_ref[...]
        vec_ref[1:2, :] += _colsum(dv * vhat)
        vec_ref[2:3, :] += _colsum(dv)
        dvhat = dv * lng_ref[...]
        dgv = rstd * (dvhat - _mean(dvhat) - vhat * _mean(dvhat * vhat))
        put(2, 0, dgv * dgv_dz)

        @pl.when(i == n_t - 1)
        def _():
            for h in range(N_HEADS):
                dws_ref[h] = jnp.where(mask_ref[...] > 0.0, dws_ref[h], 0.0)
                tot = jnp.sum(dsvsum_ref[:, h * HEAD:(h + 1) * HEAD], axis=1, keepdims=True)
                dbsp_ref[:, h * LANES:(h + 1) * LANES] = jnp.broadcast_to(tot, (SGU_BLOCK, LANES))
            for acc, out in ((dwp_acc, dwp_ref), (dwpr_acc, dwpr_ref)):
                for g in range(N_HEADS):
                    for s in range(N_SHARD):
                        out[g // 2, s, (g % 2) * shard_rows:(g % 2 + 1) * shard_rows, :] = (
                            acc[g, s * shard_rows:(s + 1) * shard_rows, :].astype(BF16))

    row = lambda i: (n_t - 1 - i, 0)
    fixed2 = lambda i: (0, 0)
    fixed3 = lambda i: (0, 0, 0)
    fixed4 = lambda i: (0, 0, 0, 0)
    vec = pl.BlockSpec((1, D), fixed2)
    sq = pl.BlockSpec((N_HEADS, SGU_BLOCK, SGU_BLOCK), fixed3)
    grp = pl.BlockSpec((N_HEADS, HEAD, HEAD), fixed3)
    grp_out = pl.BlockSpec((2, N_SHARD, 2 * shard_rows, HEAD), fixed4)
    tile = pl.BlockSpec((tm, D), row)
    return pl.pallas_call(
        body, name="mix_bwd",
        grid=(n_t,),
        in_specs=[pl.BlockSpec((tm, D_IN), row), tile, tile,
                  grp, vec, vec, vec, sq, sq, pl.BlockSpec((N_HEADS, SGU_BLOCK, 1), fixed3), grp,
                  pl.BlockSpec((D, D), fixed2), pl.BlockSpec((SGU_BLOCK, SGU_BLOCK), fixed2),
                  pl.BlockSpec(dep.shape, fixed2)],
        out_specs=[pl.BlockSpec((tm, D_IN), row), grp_out, grp_out, sq,
                   pl.BlockSpec((SGU_BLOCK, N_HEADS * LANES), fixed2),
                   pl.BlockSpec((8, D), fixed2), pl.BlockSpec((8, D_IN), fixed2)],
        out_shape=[jax.ShapeDtypeStruct((t, D_IN), BF16),
                   jax.ShapeDtypeStruct((2, N_SHARD, 2 * shard_rows, HEAD), BF16),
                   jax.ShapeDtypeStruct((2, N_SHARD, 2 * shard_rows, HEAD), BF16),
                   jax.ShapeDtypeStruct((N_HEADS, SGU_BLOCK, SGU_BLOCK), F32),
                   jax.ShapeDtypeStruct((SGU_BLOCK, N_HEADS * LANES), F32),
                   jax.ShapeDtypeStruct((8, D), F32), jax.ShapeDtypeStruct((8, D_IN), F32)],
        scratch_shapes=[pltpu.VMEM((tm + POOL_HALO, D), F32), pltpu.VMEM((tm, D), F32), pltpu.VMEM((tm, D), F32),
                        pltpu.VMEM((tm, D), F32), pltpu.VMEM((SGU_BLOCK, D), F32),
                        pltpu.VMEM((N_HEADS, HEAD, HEAD), F32), pltpu.VMEM((N_HEADS, HEAD, HEAD), F32)],
        compiler_params=_params(1),
    )(z, pooled, dy, w_pool, pool_scale, ln_g, ln_b, ws, ws_t, bsp, w_proj, w_out, mask, dep)


def _dx(dz, w_in4, x, dh1, g1, dep, tm):
    t = x.shape[0]
    cw = D_IN // N_SHARD

    def body(dz_ref, w_ref, x_ref, dh1_ref, g_ref, dep_ref, gx_ref, vec_ref):
        @pl.when(pl.program_id(0) == 0)
        def _():
            vec_ref[...] = jnp.zeros_like(vec_ref)

        dxn = _dot_nt(dz_ref[:, 0:cw], w_ref[0])
        for s in range(1, N_SHARD):
            dxn = dxn + _dot_nt(dz_ref[:, s * cw:(s + 1) * cw], w_ref[s])
        xv = x_ref[...]
        r = lax.rsqrt(_mean(xv * xv) + EPS)
        xh = xv * r
        vec_ref[0:1, :] += _colsum(dxn * xh)
        gx_ref[...] = dh1_ref[...] + _rms_bwd(dxn * g_ref[...], xh, r)

    row = lambda i: (i, 0)
    fixed = lambda i: (0, 0)
    return pl.pallas_call(
        body, name="dx",
        grid=(t // tm,),
        in_specs=[pl.BlockSpec((tm, D_IN), row),
                  _resident((N_SHARD, D, cw), 1),
                  pl.BlockSpec((tm, D), row), pl.BlockSpec((tm, D), row), pl.BlockSpec((1, D), fixed),
                  pl.BlockSpec(dep.shape, fixed)],
        out_specs=[pl.BlockSpec((tm, D), row), pl.BlockSpec((8, D), fixed)],
        out_shape=[jax.ShapeDtypeStruct((t, D), F32), jax.ShapeDtypeStruct((8, D), F32)],
        compiler_params=_params(1),
    )(dz, w_in4, x, dh1, g1, dep)


def _dw(a, b, name, n_split, tn, tk, n_out, out_block, dep=None):
    t, m = a.shape
    n = b.shape[1]
    tm = min(m, D)
    rows = tm // n_split
    n_k = t // tk
    deps = [] if dep is None else [dep]

    def body(a_ref, b_ref, *rest):
        o_ref, acc_ref = rest[-2:]
        k = pl.program_id(2)

        @pl.when(k == 0)
        def _():
            acc_ref[...] = jnp.zeros_like(acc_ref)

        acc_ref[...] += _dot_tn(a_ref[...], b_ref[...])

        @pl.when(k == n_k - 1)
        def _():
            for q in range(n_split):
                o_ref[q % 2, q // 2] = acc_ref[q * rows:(q + 1) * rows, :].astype(BF16)

    return pl.pallas_call(
        body, name=name,
        grid=(m // tm, n // tn, n_k),
        in_specs=[pl.BlockSpec((tk, tm), lambda i, j, k: (k, i)),
                  pl.BlockSpec((tk, tn), lambda i, j, k: (k, j))]
        + [pl.BlockSpec(d.shape, lambda i, j, k: (0, 0)) for d in deps],
        out_specs=pl.BlockSpec((2, n_split // 2, rows, tn), lambda i, j, k: (0, out_block(i, j), 0, 0)),
        out_shape=jax.ShapeDtypeStruct((2, n_out, rows, tn), BF16),
        scratch_shapes=[pltpu.VMEM((tm, tn), F32)],
        compiler_params=_params(3),
    )(a, b, *deps)


def _row_block(rows, cols, itemsize, n_bufs):
    budget = VMEM_LIMIT // 4
    rb = rows
    while rb % 16 == 0 and rb * cols * itemsize * n_bufs * 2 > budget:
        rb //= 2
    return rb


ROW_STEPS = 2


def _chip_sums(core, grads, recvs, name):
    n = len(grads)

    def body(c_ref, *refs):
        for w in range(n):
            refs[2 * n + w][...] = (refs[w][0].astype(F32) + refs[n + w][...].astype(F32)).astype(BF16)

    def blocks(g):
        _, _, rh, cols = g.shape
        rb = rh // ROW_STEPS
        return (pl.BlockSpec((1, 1, rb, cols), lambda s, r, c: (c[0], s, r, 0)),
                pl.BlockSpec((1, rb, cols), lambda s, r, c: (s, r, 0)))

    specs = [blocks(g) for g in grads]
    return pl.pallas_call(
        body, name=name,
        grid_spec=pltpu.PrefetchScalarGridSpec(
            num_scalar_prefetch=1, grid=(N_SHARD, ROW_STEPS),
            in_specs=[s[0] for s in specs] + [s[1] for s in specs],
            out_specs=[s[1] for s in specs]),
        out_shape=[jax.ShapeDtypeStruct(g.shape[1:], BF16) for g in grads],
        compiler_params=_params(2),
    )(core, *grads, *recvs)


def _slot_sum(slots, name):
    n_s, rows, cols = slots.shape
    rb = _row_block(rows, cols, 4, n_s + 1)

    def body(s_ref, o_ref):
        acc = s_ref[0].astype(F32)
        for k in range(1, n_s):
            acc = acc + s_ref[k].astype(F32)
        o_ref[...] = acc

    return pl.pallas_call(
        body, name=name,
        grid=(rows // rb,),
        in_specs=[pl.BlockSpec((n_s, rb, cols), lambda r: (0, r, 0))],
        out_specs=pl.BlockSpec((rb, cols), lambda r: (r, 0)),
        out_shape=jax.ShapeDtypeStruct((rows, cols), F32),
        compiler_params=_params(1),
    )(slots)


def _adamw_update(g, w, m, v):
    m_new = ADAM_B1 * m + (1.0 - ADAM_B1) * g
    v_new = ADAM_B2 * v + (1.0 - ADAM_B2) * (g * g)
    m_hat = m_new / (1.0 - ADAM_B1 ** ADAM_STEP)
    v_hat = v_new / (1.0 - ADAM_B2 ** ADAM_STEP)
    return -ADAM_LR * (m_hat / (jnp.sqrt(v_hat) + ADAM_EPS) + ADAM_WD * w), m_new, v_new


def _adamw_small(grad_parts, ws, ms, vs):
    n, n_g = len(ws), len(grad_parts)

    def rows_of(ref, lo, rows):
        if len(ref.shape) == 2:
            return ref[lo:lo + rows, :]
        acc = ref[0, lo:lo + rows, :]
        for s in range(1, ref.shape[0]):
            acc = acc + ref[s, lo:lo + rows, :]
        return acc

    def body(*refs):
        g_refs, ins, outs = refs[:n_g], refs[n_g:n_g + 3 * n], refs[n_g + 3 * n:]
        part, off = 0, 0
        for k in range(n):
            rows = ws[k].shape[0]
            if off + rows > grad_parts[part].shape[-2]:
                part, off = part + 1, 0
            g = rows_of(g_refs[part], off, rows)
            off += rows
            delta, m_new, v_new = _adamw_update(g, ins[k][...], ins[n + k][...], ins[2 * n + k][...])
            for o, val in zip(outs[4 * k:4 * k + 4], (g, delta, m_new, v_new)):
                o[...] = val

    vmem = pl.BlockSpec(memory_space=pltpu.VMEM)
    outs = pl.pallas_call(
        body, name="adamw_small",
        in_specs=[vmem] * (n_g + 3 * n), out_specs=[vmem] * (4 * n),
        out_shape=[jax.ShapeDtypeStruct(w.shape, F32) for w in ws for _ in range(4)],
    )(*grad_parts, *ws, *ms, *vs)
    return [outs[4 * k:4 * k + 4] for k in range(n)]


def _final_sums(chip, owns, slots, name):
    n = len(owns)

    def body(chip_ref, *refs):
        for w in range(n):
            own_ref, s1_ref, s2_ref, s3_ref = refs[4 * w:4 * w + 4]
            acc = own_ref[0].astype(F32) + s1_ref[0].astype(F32)
            acc = acc + s2_ref[0].astype(F32)
            refs[4 * n + w][...] = acc + s3_ref[0].astype(F32)

    def slot(a, k):
        n_s, rh, cols = a.shape
        return pl.BlockSpec((1, rh // ROW_STEPS, cols), lambda r, chip: (lax.rem(chip[0] + k, n_s), r, 0))

    operands, in_specs = [], []
    for own, slots_w in zip(owns, slots):
        operands += [own, slots_w, slots_w, slots_w]
        in_specs += [slot(own, k) for k in range(4)]
    return pl.pallas_call(
        body, name=name,
        grid_spec=pltpu.PrefetchScalarGridSpec(
            num_scalar_prefetch=1, grid=(ROW_STEPS,),
            in_specs=in_specs,
            out_specs=[pl.BlockSpec((a.shape[1] // ROW_STEPS, a.shape[2]), lambda r, chip: (r, 0)) for a in owns]),
        out_shape=[jax.ShapeDtypeStruct(a.shape[1:], F32) for a in owns],
        compiler_params=_params(1),
    )(chip, *operands)


def _adamw_halves(core, owns, siblings, ws, ms, vs, name):
    n = len(ws)

    def body(c_ref, *refs):
        ins, outs = refs[:5 * n], refs[5 * n:]
        mine = c_ref[0] == pl.program_id(0)
        for k in range(n):
            own_ref, sib_ref, w_ref, m_ref, v_ref = ins[5 * k:5 * k + 5]
            g_ref, d_ref, mo_ref, vo_ref = outs[4 * k:4 * k + 4]

            @pl.when(mine)
            def _():
                g_ref[...] = own_ref[...]

            @pl.when(jnp.logical_not(mine))
            def _():
                g_ref[...] = sib_ref[...]

            d_ref[...], mo_ref[...], vo_ref[...] = _adamw_update(g_ref[...], w_ref[...], m_ref[...], v_ref[...])

    operands, in_specs, out_specs = [], [], []
    for own, sib, w, m, v in zip(owns, siblings, ws, ms, vs):
        rows, cols = w.shape
        rb = rows // 2 // ROW_STEPS
        half = pl.BlockSpec((rb, cols), lambda h, r, c: (r, 0))
        whole = pl.BlockSpec((rb, cols), lambda h, r, c: (h * ROW_STEPS + r, 0))
        operands += [own, sib, w, m, v]
        in_specs += [half, half, whole, whole, whole]
        out_specs += [whole] * 4
    outs = pl.pallas_call(
        body, name=name,
        grid_spec=pltpu.PrefetchScalarGridSpec(
            num_scalar_prefetch=1, grid=(2, ROW_STEPS), in_specs=in_specs, out_specs=out_specs),
        out_shape=[jax.ShapeDtypeStruct(w.shape, F32) for w in ws for _ in range(4)],
        compiler_params=_params(2),
    )(core, *operands)
    return [outs[4 * k:4 * k + 4] for k in range(n)]


def _place():
    x, y, c = lax.axis_index("x"), lax.axis_index("y"), lax.axis_index("c")
    chips = [(1 - x, y), (x, 1 - y), (1 - x, 1 - y)]
    return x, y, c, chips


def _remote(src, dst, send_sem, recv_sem, device):
    return pltpu.make_async_remote_copy(src_ref=src, dst_ref=dst, send_sem=send_sem, recv_sem=recv_sem,
                                        device_id=device, device_id_type=MESH)


def _split_start(name, arrays, n_copies, plan):
    n = len(arrays)

    def body(*refs):
        ins, send_sem, recv_sem, token = refs[:n], refs[n], refs[n + 1], refs[-1]
        for k, (src, dst, device) in enumerate(plan(ins)):
            _remote(src, dst, send_sem.at[k], recv_sem.at[k], device).start()
        token[...] = jnp.zeros_like(token)

    outs = pl.pallas_call(
        body, name=name,
        in_specs=[HBM] * n,
        out_specs=(SEM, SEM, *[HBM] * n, pl.BlockSpec(memory_space=pltpu.VMEM)),
        out_shape=(pltpu.SemaphoreType.DMA((n_copies,)), pltpu.SemaphoreType.DMA((n_copies,)),
                   *[pltpu.HBM(a.shape, a.dtype) for a in arrays], jax.ShapeDtypeStruct((8, LANES), F32)),
        input_output_aliases={i: i + 2 for i in range(n)},
        compiler_params=pltpu.CompilerParams(has_side_effects=EFFECT),
    )(*[pltpu.with_memory_space_constraint(a, pltpu.HBM) for a in arrays])
    return outs[0], outs[1], list(outs[2:2 + n]), outs[-1]


def _split_wait(name, send_sem, recv_sem, arrays, plan, first, *after):
    n = len(arrays)

    def body(*refs):
        ins, send_ref, recv_ref = refs[:n], refs[n], refs[n + 1]
        for k, (src, dst, device) in enumerate(plan(ins)):
            cp = _remote(src, dst, send_ref.at[first + k], recv_ref.at[first + k], device)
            cp.wait_send()
            cp.wait_recv()

    outs = pl.pallas_call(
        body, name=name,
        in_specs=[HBM] * n + [SEM, SEM] + [ANY] * len(after), out_specs=[HBM] * n,
        out_shape=[pltpu.HBM(a.shape, a.dtype) for a in arrays],
        input_output_aliases={i: i for i in range(n)},
        compiler_params=pltpu.CompilerParams(has_side_effects=EFFECT),
    )(*arrays, send_sem, recv_sem, *after)
    return list(outs)


def _plans(*parts):
    def plan(refs):
        copies, lo = [], 0
        for part, n in parts:
            copies += part(refs[lo:lo + n])
            lo += n
        return copies
    return plan


def _plan_gather_send(n):
    def plan(refs):
        x, y, c, chips = _place()
        me = 2 * x + y
        copies = []
        for w in range(n):
            own, full = refs[w], refs[n + w]
            copies.append((own, full.at[me], (x, y, 1 - c)))
            copies += [(own.at[c], full.at[me, c], (px, py, c)) for px, py in chips]
        return copies
    return plan


def _plan_gather_direct(n):
    def plan(refs):
        x, y, c, chips = _place()
        me = 2 * x + y
        copies = []
        for w in range(n):
            own, full = refs[w], refs[n + w]
            copies.append((own, full.at[me], (x, y, 1 - c)))
            for px, py in chips:
                copies += [(own.at[c], full.at[me, c], (px, py, c)), (own.at[c], full.at[me, c], (px, py, 1 - c))]
        return copies
    return plan


def _plan_gather_pass(n):
    def plan(refs):
        x, y, c, chips = _place()
        copies = []
        for w in range(n):
            for px, py in chips:
                landed = refs[w].at[2 * px + py, c]
                copies.append((landed, landed, (x, y, 1 - c)))
        return copies
    return plan


def _plan_swap(n, n_wholes):
    def plan(refs):
        x, y, c, _ = _place()
        n_all = n + n_wholes
        return [(refs[w].at[1 - c] if w < n else refs[w], refs[n_all + w], (x, y, 1 - c)) for w in range(n_all)]
    return plan


def _plan_share(n):
    def plan(refs):
        x, y, c, _ = _place()
        return [(refs[w], refs[n + w], (x, y, 1 - c)) for w in range(n)]
    return plan


def _plan_exchange(n, with_small):
    def plan(refs):
        x, y, c, chips = _place()
        me = 2 * x + y
        n_in = n + int(with_small)
        copies = []
        for w in range(n):
            copies += [(refs[w].at[2 * px + py], refs[n_in + w].at[me], (px, py, c)) for px, py in chips]
        if with_small:
            small, slots = refs[n], refs[n_in + n]
            copies.append((small, slots.at[me], (x, y, 1 - c)))
            copies += [(small, slots.at[me], (px, py, c)) for px, py in chips]
        return copies
    return plan


def _share_and_all_reduce(wholes, part):
    n = len(wholes)
    n_dev = 8

    def body(*refs):
        ins, part_ref, outs, sum_ref = refs[:n], refs[n], refs[n + 1:2 * n + 1], refs[2 * n + 1]
        slots_ref, send_sem, recv_sem, send_tile, recv_tile = refs[2 * n + 2:]
        x, y, c, _ = _place()
        me = 4 * x + 2 * y + c
        copies = []
        for w in range(n):
            copies.append(_remote(ins[w], outs[w], send_sem.at[w], recv_sem.at[w], (x, y, 1 - c)))
        slots_ref[me] = part_ref[...]
        flips = [(fx, fy, fc) for fx in (0, 1) for fy in (0, 1) for fc in (0, 1)][1:]
        for k, (fx, fy, fc) in enumerate(flips):
            peer = (x + fx - 2 * x * fx, y + fy - 2 * y * fy, c + fc - 2 * c * fc)
            copies.append(_remote(part_ref, slots_ref.at[me], send_tile.at[k], recv_tile.at[k], peer))
        for cp in copies:
            cp.start()
        for cp in copies:
            cp.wait()
        acc = slots_ref[0]
        for k in range(1, n_dev):
            acc = acc + slots_ref[k]
        sum_ref[...] = acc

    vmem = pl.BlockSpec(memory_space=pltpu.VMEM)
    outs = pl.pallas_call(
        body, name="rs_share_late",
        in_specs=[ANY] * n + [vmem], out_specs=[ANY] * n + [vmem],
        out_shape=[jax.ShapeDtypeStruct(a.shape, a.dtype) for a in wholes] + [jax.ShapeDtypeStruct(part.shape, F32)],
        scratch_shapes=[pltpu.VMEM((n_dev,) + part.shape, F32),
                        pltpu.SemaphoreType.DMA((n,)), pltpu.SemaphoreType.DMA((n,)),
                        pltpu.SemaphoreType.DMA((n_dev - 1,)), pltpu.SemaphoreType.DMA((n_dev - 1,))],
    )(*wholes, part)
    return outs[:n], outs[n]


_SMALL = ("norm1_pre_g", "b_in", "pool_scale", "sgu_ln_g", "sgu_ln_b", "w_spatial",
          "norm1_post_g", "norm2_pre_g", "norm2_post_g", "b_spatial")
_MIX = ("w_in", "w_pool", "w_sgu_proj", "w_out")
_FF = ("w_ff1", "w_ff2")
_BIG = _MIX + _FF
_ORDER = ("norm1_pre_g", "w_in", "b_in", "w_pool", "pool_scale", "sgu_ln_g", "sgu_ln_b", "w_spatial", "b_spatial",
          "w_sgu_proj", "w_out", "norm1_post_g", "norm2_pre_g", "w_ff1", "w_ff2", "norm2_post_g")


def _pack(keys, rows, parts):
    flat = jnp.concatenate([parts[k].reshape(-1).astype(F32) for k in keys])
    flat = jnp.pad(flat, (0, rows * LANES - flat.shape[0]))
    return flat.reshape(rows, LANES)


def _halves(a):
    return a.reshape(2, a.shape[0] // 2, a.shape[1])


def _step(p, m, v, x, target):
    t = x.shape[0]
    tm_mm = min(1024, t)
    tm_mix = min(256, t)
    tm_dx = min(512, t)
    core = lax.axis_index("c").astype(jnp.int32).reshape(1)
    row = lambda a: a.reshape(1, -1)

    chip = (2 * lax.axis_index("x") + lax.axis_index("y")).astype(jnp.int32).reshape(1)

    local2d = {"w_in": p["w_in"], "w_ff1": p["w_ff1"], "w_ff2": p["w_ff2"], "w_out": p["w_out"],
               "w_pool": p["w_pool"].reshape(N_HEADS * 64, POOL_GROUP),
               "w_sgu_proj": p["w_sgu_proj"].reshape(N_HEADS * 64, HEAD)}
    shard = {k: _halves(local2d[k].astype(BF16)) for k in _BIG}
    landing = lambda keys: [lax.empty((N_SHARD,) + shard[k].shape, BF16) for k in keys]
    in_arrays = [shard["w_in"]] + landing(["w_in"])
    send_a, recv_a, in_arrays, token = _split_start("ag_in_send", in_arrays, 4, _plan_gather_send(1))
    z, xn = _inproj_own(x, row(p["norm1_pre_g"]), in_arrays[0].reshape(D, D_IN // N_SHARD), row(p["b_in"]), chip,
                        token, tm_mm)
    mix_keys = _MIX[1:]
    n_mix, n_ff = len(mix_keys), len(_FF)
    in_arrays = _split_wait("ag_in_land", send_a, recv_a, in_arrays, _plan_gather_send(1), 0, z,
                            *[shard[k] for k in mix_keys + _FF])
    rest = in_arrays[1:] + [shard[k] for k in mix_keys] + landing(mix_keys) + [shard[k] for k in _FF] + landing(_FF)
    rest_plan = _plans((_plan_gather_pass(1), 1), (_plan_gather_direct(n_mix), 2 * n_mix),
                       (_plan_gather_send(n_ff), 2 * n_ff))
    send0, recv0, rest, token = _split_start("ag_rest_send", rest, 3 + 7 * n_mix + 4 * n_ff, rest_plan)
    in_full, mix_arrays, ff_arrays = rest[:1], rest[1:1 + 2 * n_mix], rest[1 + 2 * n_mix:]
    in_full = _split_wait("ag_in_passed", send0, recv0, in_full, _plan_gather_pass(1), 0, token)
    w_in4 = in_full[0].reshape(N_SHARD, D, D_IN // N_SHARD)

    pos = jnp.arange(SGU_BLOCK) // 64
    mask = (pos[:, None] >= pos[None, :]).astype(F32)
    ws = (p["w_spatial"] * mask[None]).astype(BF16)
    ws_t = jnp.swapaxes(ws, 1, 2)
    bsp = p["b_spatial"].reshape(N_HEADS, SGU_BLOCK, 1)

    z = _inproj_rest(xn, w_in4, row(p["b_in"]), chip, z, tm_mm)
    mix_arrays = _split_wait("ag_mix_land", send0, recv0, mix_arrays, _plan_gather_direct(n_mix), 3, z)
    full = dict(zip(mix_keys, mix_arrays[len(mix_keys):]))
    w_out = full["w_out"].reshape(D, D)
    regroup = lambda a: a.reshape(N_SHARD, N_HEADS, 64, 256).transpose(1, 0, 2, 3).reshape(N_HEADS, 256, 256)
    w_pool = regroup(full["w_pool"])
    w_proj = regroup(full["w_sgu_proj"])
    pooled, merged, y, h1 = _mix_fwd(z, x, w_pool, row(p["pool_scale"]), row(p["sgu_ln_g"]), row(p["sgu_ln_b"]),
                                     ws, bsp, w_proj, w_out, row(p["norm1_post_g"]), tm_mix)
    ff_arrays = _split_wait("ag_ff_land", send0, recv0, ff_arrays, _plan_gather_send(n_ff), 3 + 7 * n_mix, pooled)
    send2, recv2, ff_full, token = _split_start("ag_ff_pass", ff_arrays[n_ff:], 3 * n_ff, _plan_gather_pass(n_ff))
    ff_full = _split_wait("ag_ff_passed", send2, recv2, ff_full, _plan_gather_pass(n_ff), 0, token)
    w_ff1_4 = ff_full[0].reshape(N_SHARD, D, D)
    w_ff2 = ff_full[1].reshape(N_SHARD, D, D)
    hn, f1, f, df2, dout, vec_a = _ffn_fwd(h1, row(p["norm2_pre_g"]), w_ff1_4, w_ff2, row(p["norm2_post_g"]),
                                           target, tm_mix)

    tk = min(2048, t)
    g_big = {"w_ff2": _dw(f, df2, "dw_ff2", 2, D, tk, N_SHARD, lambda i, j: i)}
    df1, dh1, dy, vec_b = _ffn_bwd(df2, f1, w_ff1_4, w_ff2, h1, dout, y, row(p["norm2_pre_g"]),
                                   row(p["norm1_post_g"]), tm_mix)
    g_big["w_ff1"] = _dw(hn, df1, "dw_ff1", 2, D, tk, N_SHARD, lambda i, j: j)

    def reduce_start(name, keys, wholes, meanwhile, also=(), also_plan=None, also_copies=0):
        n_k, n_all = len(keys), len(keys) + len(wholes)
        swap_plan = _plan_swap(n_k, len(wholes))
        crossing = [g_big[k] for k in keys] + list(wholes)
        crossing += [lax.empty(a.shape[1:] if w < n_k else a.shape, a.dtype) for w, a in enumerate(crossing)]
        send, recv, crossing, tok = _split_start("rs_swap_" + name + "_send", crossing, n_all, swap_plan)
        crossing = _split_wait("rs_swap_" + name + "_land", send, recv, crossing, swap_plan, 0, meanwhile(tok))
        mine, wholes, swapped = crossing[:n_k], crossing[n_k:n_all], crossing[n_all:]
        sums = _chip_sums(core, mine, swapped[:n_k], "chip_sums_" + name)
        arrays = list(sums)
        if wholes:
            arrays.append(_slot_sum(jnp.stack([wholes[0], swapped[-1]]), "chip_sum_small"))
        arrays += [lax.empty(s.shape, BF16) for s in sums]
        if wholes:
            arrays.append(lax.empty((N_SHARD,) + wholes[0].shape, F32))
        plan = _plan_exchange(len(keys), bool(wholes))
        n_copies, n = 3 * len(keys) + 4 * len(wholes), len(arrays)
        also = also() if callable(also) else also
        both = plan if also_plan is None else _plans((plan, n), (also_plan, len(also)))
        send, recv, out, tok = _split_start("rs_send_" + name, arrays + list(also), n_copies + also_copies, both)
        return send, recv, out[:n], plan, tok, out[n:], n_copies

    def reduce_finish(name, keys, state, *after):
        send, recv, arrays, plan = state[:4]
        arrays = _split_wait("rs_land_" + name, send, recv, arrays, plan, 0, *after)
        n_in = len(arrays) // 2
        n = len(keys)
        finished = _final_sums(chip, arrays[:n], arrays[n_in:n_in + n], "final_sums_" + name)
        return dict(zip(keys, finished)), arrays

    early = _FF
    late = ("w_in", "w_pool", "w_sgu_proj", "w_out")

    def dw_out(token):
        g_big["w_out"] = _dw(merged, dy, "dw_out", 2 * N_SHARD, D, tk, N_SHARD, lambda i, j: 0, token)
        return g_big["w_out"]

    early_state = reduce_start("early", early, [], dw_out)
    dz, g_big["w_pool"], g_big["w_sgu_proj"], dws, dbsp, vec_c, dbin = _mix_bwd(
        z, pooled, dy, w_pool, row(p["pool_scale"]), row(p["sgu_ln_g"]), row(p["sgu_ln_b"]), ws, ws_t, bsp, w_proj,
        w_out, mask, tm_mix, early_state[4])
    g_big["w_in"] = _dw(xn, dz, "dw_in", 2, D_IN // N_SHARD, tk, N_SHARD, lambda i, j: j)
    small_part = _pack(_SMALL[1:], SMALL_ROWS - 8, {
        "b_in": dbin[0], "pool_scale": vec_c[0], "sgu_ln_g": vec_c[1], "sgu_ln_b": vec_c[2],
        "w_spatial": dws, "b_spatial": dbsp[:, ::LANES].T, "norm1_post_g": vec_b[1], "norm2_pre_g": vec_b[0],
        "norm2_post_g": vec_a[0]})
    halves = {}

    def finish_early(token):
        halves.update(reduce_finish("early", early, early_state, token)[0])
        return halves[early[-1]]

    share_early = lambda: [halves[k] for k in early] + [lax.empty(halves[k].shape, F32) for k in early]
    late_state = reduce_start("late", late, [small_part], finish_early, share_early, _plan_share(len(early)),
                              len(early))
    grad_x, vec_d = _dx(dz, w_in4, x, dh1, row(p["norm1_pre_g"]), late_state[4], tm_dx)
    share = _split_wait("rs_share_early_land", late_state[0], late_state[1], late_state[5], _plan_share(len(early)),
                        late_state[6], grad_x)

    grads, delta, new_m, new_v = {}, {}, {}, {}

    def update(name, keys, own, shared):
        as2d = lambda a, k: a.reshape(local2d[k].shape)
        outs = _adamw_halves(core, own, shared, [local2d[k] for k in keys],
                             [as2d(m[k], k) for k in keys], [as2d(v[k], k) for k in keys], "adamw_" + name)
        for k, outs_k in zip(keys, outs):
            grads[k], delta[k], new_m[k], new_v[k] = (a.reshape(p[k].shape) for a in outs_k)

    update("early", early, share[:len(early)], share[len(early):])
    late_halves, late_arrays = reduce_finish("late", late, late_state, grad_x, *[new_v[k] for k in early])
    own_late = [late_halves[k] for k in late]
    shared_late, tiles = _share_and_all_reduce(
        own_late, jnp.concatenate([vec_d[0].reshape(8, LANES), jnp.broadcast_to(vec_a[1:2, :LANES], (8, LANES))]))
    g1_grad, loss = tiles[:8], tiles[8, 0]
    update("late", late, own_late, shared_late)
    as_rows = lambda d: [d[k].reshape(-1, LANES) for k in _SMALL]
    small_outs = _adamw_small([g1_grad, late_arrays[-1]], as_rows(p), as_rows(m), as_rows(v))
    for k, outs_k in zip(_SMALL, small_outs):
        grads[k], delta[k], new_m[k], new_v[k] = (a.reshape(p[k].shape) for a in outs_k)

    return (loss, grad_x, *[grads[k] for k in _ORDER], *[delta[k] for k in _ORDER],
            *[new_m[k] for k in _ORDER], *[new_v[k] for k in _ORDER])


def kernel(x, norm1_pre_g, w_in, b_in, w_pool, pool_scale, sgu_ln_g, sgu_ln_b, w_spatial, b_spatial, w_sgu_proj, w_out, norm1_post_g, norm2_pre_g, w_ff1, w_ff2, norm2_post_g, loss_target, m_norm1_pre_g, m_w_in, m_b_in, m_w_pool, m_pool_scale, m_sgu_ln_g, m_sgu_ln_b, m_w_spatial, m_b_spatial, m_w_sgu_proj, m_w_out, m_norm1_post_g, m_norm2_pre_g, m_w_ff1, m_w_ff2, m_norm2_post_g, v_norm1_pre_g, v_w_in, v_b_in, v_w_pool, v_pool_scale, v_sgu_ln_g, v_sgu_ln_b, v_w_spatial, v_b_spatial, v_w_sgu_proj, v_w_out, v_norm1_post_g, v_norm2_pre_g, v_w_ff1, v_w_ff2, v_norm2_post_g):
    p = dict(zip(_ORDER, (norm1_pre_g, w_in, b_in, w_pool, pool_scale, sgu_ln_g, sgu_ln_b, w_spatial, b_spatial,
                          w_sgu_proj, w_out, norm1_post_g, norm2_pre_g, w_ff1, w_ff2, norm2_post_g)))
    m = dict(zip(_ORDER, (m_norm1_pre_g, m_w_in, m_b_in, m_w_pool, m_pool_scale, m_sgu_ln_g, m_sgu_ln_b, m_w_spatial,
                          m_b_spatial, m_w_sgu_proj, m_w_out, m_norm1_post_g, m_norm2_pre_g, m_w_ff1, m_w_ff2,
                          m_norm2_post_g)))
    v = dict(zip(_ORDER, (v_norm1_pre_g, v_w_in, v_b_in, v_w_pool, v_pool_scale, v_sgu_ln_g, v_sgu_ln_b, v_w_spatial,
                          v_b_spatial, v_w_sgu_proj, v_w_out, v_norm1_post_g, v_norm2_pre_g, v_w_ff1, v_w_ff2,
                          v_norm2_post_g)))
    outs = _step(p, m, v, x[0], loss_target[0])
    return (outs[0], outs[1][None], *outs[2:])
```

```python
import math

import jax
import jax.numpy as jnp
from jax import lax
from jax.experimental import pallas as pl
from jax.experimental.pallas import tpu as pltpu

F32 = jnp.float32
BF16 = jnp.bfloat16

D = 1024
D_IN = 5 * D
D_FF = 4 * D
N_SHARD = 4
POOL_WINDOWS = (2, 4, 8, 16)
POOL_GROUP = 256
POOL_HALO = 16
SGU_BLOCK = 128
N_HEADS = 4
HEAD = 256
EPS = 1e-6

ADAM_LR = 0.001
ADAM_B1 = 0.9
ADAM_B2 = 0.999
ADAM_EPS = 1e-08
ADAM_WD = 0.01
ADAM_STEP = 10

V7X_VMEM_BYTES = 64 * 1024 * 1024
VMEM_LIMIT = V7X_VMEM_BYTES - 8 * 1024 * 1024
SMALL_ROWS = 616
LANES = 128

MESH = pl.DeviceIdType.MESH
ANY = pl.BlockSpec(memory_space=pl.ANY)
HBM = pl.BlockSpec(memory_space=pltpu.HBM)
SEM = pl.BlockSpec(memory_space=pltpu.SEMAPHORE)
EFFECT = pltpu.SideEffectType.DATAFLOW_SIDE_EFFECTING

NT_DIMS = (((1,), (1,)), ((), ()))
TN_DIMS = (((0,), (0,)), ((), ()))


def _params(n_axes):
    return pltpu.CompilerParams(dimension_semantics=("arbitrary",) * n_axes, vmem_limit_bytes=VMEM_LIMIT)


def _resident(shape, n_axes):
    zeros = (0,) * len(shape)
    index = (lambda i: zeros) if n_axes == 1 else (lambda i, j: zeros)
    return pl.BlockSpec(shape, index, pipeline_mode=pl.Buffered(1))


def _dot(a, b):
    return jnp.dot(a, b, preferred_element_type=F32)


def _dot_nt(a, b):
    return lax.dot_general(a, b, NT_DIMS, preferred_element_type=F32)


def _dot_tn(a, b):
    return lax.dot_general(a, b, TN_DIMS, preferred_element_type=F32)


def _mean(a):
    return jnp.mean(a, axis=-1, keepdims=True)


def _colsum(a):
    return jnp.sum(a, axis=0, keepdims=True)


_GELU_C0 = math.sqrt(2.0 / math.pi)
_GELU_C1 = 0.044715


def _gelu(a):
    t = jnp.tanh(a * (_GELU_C0 + (_GELU_C0 * _GELU_C1) * (a * a)))
    return a * (0.5 * t + 0.5)


def _gelu_and_grad(a):
    a2 = a * a
    t = jnp.tanh(a * (_GELU_C0 + (_GELU_C0 * _GELU_C1) * a2))
    cdf = 0.5 * t + 0.5
    val = a * cdf
    grad = cdf + val * (1.0 - cdf) * ((2.0 * _GELU_C0) + (6.0 * _GELU_C0 * _GELU_C1) * a2)
    return val, grad


def _sigmoid(a):
    return 0.5 * jnp.tanh(0.5 * a) + 0.5


def _rms_norm_bf16(a, g):
    return (a * lax.rsqrt(_mean(a * a) + EPS) * g).astype(BF16)


def _relu_sq_bf16(a):
    r = jnp.maximum(a, 0.0)
    return (r * r).astype(BF16)


def _window_sum(a, w, ahead):
    n = a.shape[0]
    step = 1
    while step < w:
        a = a + pltpu.roll(a, n - step if ahead else step, 0)
        step *= 2
    return a


def _rms_bwd(dn, n, r):
    return r * (dn - n * _mean(dn * n))


def _inproj_own(x, g1, w_own, b_in, chip, dep, tm):
    t = x.shape[0]
    cw = D_IN // N_SHARD

    def body(chip_ref, x_ref, g_ref, w_ref, b_ref, dep_ref, z_ref, xn_ref):
        xn = _rms_norm_bf16(x_ref[...], g_ref[...])
        xn_ref[...] = xn
        z_ref[...] = _dot(xn, w_ref[...]) + b_ref[...]

    return pl.pallas_call(
        body, name="inproj_own",
        grid_spec=pltpu.PrefetchScalarGridSpec(
            num_scalar_prefetch=1, grid=(t // tm,),
            in_specs=[pl.BlockSpec((tm, D), lambda i, chip: (i, 0)),
                      pl.BlockSpec((1, D), lambda i, chip: (0, 0)),
                      pl.BlockSpec((D, cw), lambda i, chip: (0, 0)),
                      pl.BlockSpec((1, cw), lambda i, chip: (0, chip[0])),
                      pl.BlockSpec(dep.shape, lambda i, chip: (0, 0))],
            out_specs=[pl.BlockSpec((tm, cw), lambda i, chip: (i, chip[0])),
                       pl.BlockSpec((tm, D), lambda i, chip: (i, 0))]),
        out_shape=[jax.ShapeDtypeStruct((t, D_IN), F32), jax.ShapeDtypeStruct((t, D), BF16)],
        compiler_params=_params(1),
    )(chip, x, g1, w_own, b_in, dep)


def _inproj_rest(xn, w_in4, b_in, chip, z, tm):
    t = xn.shape[0]
    cw = D_IN // N_SHARD
    shard_of = lambda s, chip: lax.rem(chip[0] + 1 + s, N_SHARD)

    def body(chip_ref, xn_ref, w_ref, b_ref, z_in_ref, z_ref):
        z_ref[...] = _dot(xn_ref[...], w_ref[shard_of(pl.program_id(1), chip_ref)]) + b_ref[...]

    return pl.pallas_call(
        body, name="inproj_rest",
        grid_spec=pltpu.PrefetchScalarGridSpec(
            num_scalar_prefetch=1, grid=(t // tm, N_SHARD - 1),
            in_specs=[pl.BlockSpec((tm, D), lambda i, s, chip: (i, 0)),
                      pl.BlockSpec((N_SHARD, D, cw), lambda i, s, chip: (0, 0, 0), pipeline_mode=pl.Buffered(1)),
                      pl.BlockSpec((1, cw), lambda i, s, chip: (0, shard_of(s, chip))),
                      ANY],
            out_specs=pl.BlockSpec((tm, cw), lambda i, s, chip: (i, shard_of(s, chip)))),
        out_shape=jax.ShapeDtypeStruct(z.shape, F32),
        input_output_aliases={4: 0},
        compiler_params=_params(2),
    )(chip, xn, w_in4, b_in, z)


def _sgu_forward(z_ref, lng_ref, lnb_ref, ws_ref, bsp_ref, sv_ref, tm, with_grad):
    if with_grad:
        u, du_dz = _gelu_and_grad(z_ref[:, D:2 * D])
        gv, dgv_dz = _gelu_and_grad(z_ref[:, 2 * D:3 * D])
    else:
        u = _gelu(z_ref[:, D:2 * D])
        gv = _gelu(z_ref[:, 2 * D:3 * D])
        du_dz = dgv_dz = None
    xc = gv - _mean(gv)
    rstd = lax.rsqrt(_mean(xc * xc) + EPS)
    vhat = xc * rstd
    vb = (vhat * lng_ref[...] + lnb_ref[...]).astype(BF16)
    for b in range(tm // SGU_BLOCK):
        rows = slice(b * SGU_BLOCK, (b + 1) * SGU_BLOCK)
        for h in range(N_HEADS):
            cols = slice(h * HEAD, (h + 1) * HEAD)
            sv_ref[rows, cols] = _dot(ws_ref[h], vb[rows, cols]) + bsp_ref[h]
    return u, du_dz, dgv_dz, vhat, rstd, vb


def _mix_fwd(z, x, w_pool, pool_scale, ln_g, ln_b, ws, bsp, w_proj, w_out, g1post, tm):
    t = x.shape[0]

    def body(z_ref, x_ref, wp_ref, psc_ref, lng_ref, lnb_ref, ws_ref, bsp_ref, wpr_ref, wo_ref, g1p_ref,
             pooled_ref, merged_ref, y_ref, h1_ref, win_ref, sv_ref):
        i = pl.program_id(0)

        @pl.when(i == 0)
        def _():
            win_ref[0:POOL_HALO, :] = jnp.zeros((POOL_HALO, D), F32)

        win_ref[POOL_HALO:POOL_HALO + tm, :] = z_ref[:, 0:D]
        pos1 = lax.broadcasted_iota(jnp.int32, (tm, 1), 0) + (i * tm + 1)
        a_parts = []
        for g, w in enumerate(POOL_WINDOWS):
            cols = slice(g * POOL_GROUP, (g + 1) * POOL_GROUP)
            ts = _window_sum(win_ref[:, cols], w, False)[POOL_HALO:POOL_HALO + tm, :]
            inv_cnt = 1.0 / jnp.minimum(pos1, w).astype(F32)
            pooled = (ts * inv_cnt - z_ref[:, cols]).astype(BF16)
            pooled_ref[:, cols] = pooled
            a_parts.append(_dot(pooled, wp_ref[g]) * psc_ref[:, cols])
        win_ref[0:POOL_HALO, :] = win_ref[tm:tm + POOL_HALO, :]

        u, _, _, _, _, _ = _sgu_forward(z_ref, lng_ref, lnb_ref, ws_ref, bsp_ref, sv_ref, tm, False)
        gated = (u * sv_ref[...]).astype(BF16)
        for h in range(N_HEADS):
            lo = h * HEAD
            bbr = _dot(gated[:, lo:lo + HEAD], wpr_ref[h])
            gate_a = _sigmoid(z_ref[:, 3 * D + lo:3 * D + lo + HEAD])
            gate_b = _sigmoid(z_ref[:, 4 * D + lo:4 * D + lo + HEAD])
            merged_ref[:, lo:lo + HEAD] = (gate_a * a_parts[h] + gate_b * bbr).astype(BF16)
        merged = merged_ref[...]
        y = _dot(merged, wo_ref[...])
        y_ref[...] = y
        r = lax.rsqrt(_mean(y * y) + EPS)
        h1_ref[...] = x_ref[...] + (y * r) * g1p_ref[...]

    row = lambda i: (i, 0)
    fixed2 = lambda i: (0, 0)
    fixed3 = lambda i: (0, 0, 0)
    vec = pl.BlockSpec((1, D), fixed2)
    return pl.pallas_call(
        body, name="mix_fwd",
        grid=(t // tm,),
        in_specs=[pl.BlockSpec((tm, D_IN), row), pl.BlockSpec((tm, D), row),
                  pl.BlockSpec((N_HEADS, POOL_GROUP, POOL_GROUP), fixed3), vec, vec, vec,
                  pl.BlockSpec((N_HEADS, SGU_BLOCK, SGU_BLOCK), fixed3),
                  pl.BlockSpec((N_HEADS, SGU_BLOCK, 1), fixed3),
                  pl.BlockSpec((N_HEADS, HEAD, HEAD), fixed3),
                  pl.BlockSpec((D, D), fixed2), vec],
        out_specs=[pl.BlockSpec((tm, D), row)] * 4,
        out_shape=[jax.ShapeDtypeStruct((t, D), BF16), jax.ShapeDtypeStruct((t, D), BF16),
                   jax.ShapeDtypeStruct((t, D), F32), jax.ShapeDtypeStruct((t, D), F32)],
        scratch_shapes=[pltpu.VMEM((tm + POOL_HALO, D), F32), pltpu.VMEM((tm, D), F32)],
        compiler_params=_params(1),
    )(z, x, w_pool, pool_scale, ln_g, ln_b, ws, bsp, w_proj, w_out, g1post)


def _ffn_fwd(h1, g2pre, w1, w2, g2post, target, tm):
    t = h1.shape[0]
    n_j = D_FF // D

    def body(h1_ref, g2_ref, w1_ref, w2_ref, g2p_ref, tgt_ref,
             hn_ref, f1_ref, f_ref, df2_ref, dout_ref, vec_ref):
        @pl.when(pl.program_id(0) == 0)
        def _():
            vec_ref[...] = jnp.zeros_like(vec_ref)

        hn = _rms_norm_bf16(h1_ref[...], g2_ref[...])
        hn_ref[...] = hn
        f2 = None
        for j in range(n_j):
            cols = slice(j * D, (j + 1) * D)
            f1 = _dot(hn, w1_ref[j])
            f1_ref[:, cols] = f1
            f = _relu_sq_bf16(f1)
            f_ref[:, cols] = f
            part = _dot(f, w2_ref[j])
            f2 = part if f2 is None else f2 + part
        r = lax.rsqrt(_mean(f2 * f2) + EPS)
        n2 = f2 * r
        err = h1_ref[...] + n2 * g2p_ref[...] - tgt_ref[...]
        loss = 0.5 * jnp.sum(_mean(err * err))
        dout = err * (1.0 / D)
        dout_ref[...] = dout
        vec_ref[0:1, :] += _colsum(dout * n2)
        vec_ref[1:2, :] += jnp.full((1, D), loss, F32)
        df2_ref[...] = _rms_bwd(dout * g2p_ref[...], n2, r).astype(BF16)

    row = lambda i: (i, 0)
    fixed = lambda i: (0, 0)
    vec = pl.BlockSpec((1, D), fixed)
    return pl.pallas_call(
        body, name="ffn_fwd",
        grid=(t // tm,),
        in_specs=[pl.BlockSpec((tm, D), row), vec,
                  _resident((n_j, D, D), 1), _resident((n_j, D, D), 1), vec,
                  pl.BlockSpec((tm, D), row)],
        out_specs=[pl.BlockSpec((tm, D), row),
                   pl.BlockSpec((tm, D_FF), row), pl.BlockSpec((tm, D_FF), row),
                   pl.BlockSpec((tm, D), row), pl.BlockSpec((tm, D), row),
                   pl.BlockSpec((8, D), fixed)],
        out_shape=[jax.ShapeDtypeStruct((t, D), BF16), jax.ShapeDtypeStruct((t, D_FF), F32),
                   jax.ShapeDtypeStruct((t, D_FF), BF16), jax.ShapeDtypeStruct((t, D), BF16),
                   jax.ShapeDtypeStruct((t, D), F32), jax.ShapeDtypeStruct((8, D), F32)],
        compiler_params=_params(1),
    )(h1, g2pre, w1, w2, g2post, target)


def _ffn_bwd(df2, f1, w1, w2, h1, dout, y, g2pre, g1post, tm):
    t = h1.shape[0]
    n_j = D_FF // D

    def body(df2_ref, f1_ref, w1_ref, w2_ref, h1_ref, dout_ref, y_ref, g2_ref, g1p_ref,
             df1_ref, dh1_ref, dy_ref, vec_ref):
        @pl.when(pl.program_id(0) == 0)
        def _():
            vec_ref[...] = jnp.zeros_like(vec_ref)

        df2 = df2_ref[...]
        dhn = None
        for j in range(n_j):
            cols = slice(j * D, (j + 1) * D)
            df = _dot_nt(df2, w2_ref[j])
            df1 = (df * (2.0 * jnp.maximum(f1_ref[:, cols], 0.0))).astype(BF16)
            df1_ref[:, cols] = df1
            part = _dot_nt(df1, w1_ref[j])
            dhn = part if dhn is None else dhn + part
        h = h1_ref[...]
        r2 = lax.rsqrt(_mean(h * h) + EPS)
        nh = h * r2
        vec_ref[0:1, :] += _colsum(dhn * nh)
        dh1 = dout_ref[...] + _rms_bwd(dhn * g2_ref[...], nh, r2)
        dh1_ref[...] = dh1
        yv = y_ref[...]
        r1 = lax.rsqrt(_mean(yv * yv) + EPS)
        ny = yv * r1
        vec_ref[1:2, :] += _colsum(dh1 * ny)
        dy_ref[...] = _rms_bwd(dh1 * g1p_ref[...], ny, r1).astype(BF16)

    row = lambda i: (i, 0)
    fixed = lambda i: (0, 0)
    vec = pl.BlockSpec((1, D), fixed)
    return pl.pallas_call(
        body, name="ffn_bwd",
        grid=(t // tm,),
        in_specs=[pl.BlockSpec((tm, D), row), pl.BlockSpec((tm, D_FF), row),
                  _resident((n_j, D, D), 1), _resident((n_j, D, D), 1),
                  pl.BlockSpec((tm, D), row), pl.BlockSpec((tm, D), row), pl.BlockSpec((tm, D), row), vec, vec],
        out_specs=[pl.BlockSpec((tm, D_FF), row),
                   pl.BlockSpec((tm, D), row), pl.BlockSpec((tm, D), row),
                   pl.BlockSpec((8, D), fixed)],
        out_shape=[jax.ShapeDtypeStruct((t, D_FF), BF16), jax.ShapeDtypeStruct((t, D), F32),
                   jax.ShapeDtypeStruct((t, D), BF16), jax.ShapeDtypeStruct((8, D), F32)],
        compiler_params=_params(1),
    )(df2, f1, w1, w2, h1, dout, y, g2pre, g1post)


def _mix_bwd(z, pooled, dy, w_pool, pool_scale, ln_g, ln_b, ws, ws_t, bsp, w_proj, w_out, mask, tm, dep):
    t = z.shape[0]
    n_t = t // tm
    shard_rows = POOL_GROUP // N_SHARD

    def body(z_ref, pooled_ref, dy_ref, wp_ref, psc_ref, lng_ref, lnb_ref, ws_ref, wst_ref, bsp_ref, wpr_ref,
             wo_ref, mask_ref, dep_ref,
             dz_ref, dwp_ref, dwpr_ref, dws_ref, dbsp_ref, vec_ref, dbin_ref,
             win_ref, sv_ref, dsv_ref, dv_ref, dsvsum_ref, dwp_acc, dwpr_acc):
        i = pl.program_id(0)

        @pl.when(i == 0)
        def _():
            win_ref[tm:tm + POOL_HALO, :] = jnp.zeros((POOL_HALO, D), F32)
            dwp_acc[...] = jnp.zeros_like(dwp_acc)
            dwpr_acc[...] = jnp.zeros_like(dwpr_acc)
            dws_ref[...] = jnp.zeros_like(dws_ref)
            vec_ref[...] = jnp.zeros_like(vec_ref)
            dbin_ref[...] = jnp.zeros_like(dbin_ref)
            dsvsum_ref[...] = jnp.zeros_like(dsvsum_ref)

        dmerged = _dot_nt(dy_ref[...], wo_ref[...])

        u, du_dz, dgv_dz, vhat, rstd, vb = _sgu_forward(z_ref, lng_ref, lnb_ref, ws_ref, bsp_ref, sv_ref, tm, True)
        sv = sv_ref[...]
        gated = (u * sv).astype(BF16)

        sa = _sigmoid(z_ref[:, 3 * D:4 * D])
        sb = _sigmoid(z_ref[:, 4 * D:5 * D])
        da = dmerged * sa
        db = dmerged * sb
        dbbr = db.astype(BF16)

        tile = n_t - 1 - i
        pos1 = lax.broadcasted_iota(jnp.int32, (tm, 1), 0) + (tile * tm + 1)
        def put(k, lo, part):
            cols = slice(k * D + lo, k * D + lo + part.shape[1])
            dz_ref[:, cols] = part.astype(BF16)
            dbin_ref[0:1, cols] += _colsum(part)

        dpooled_parts = []
        for g, w in enumerate(POOL_WINDOWS):
            cols = slice(g * POOL_GROUP, (g + 1) * POOL_GROUP)
            pg = pooled_ref[:, cols]
            a_pre = _dot(pg, wp_ref[g])
            da_g = da[:, cols]
            psc = psc_ref[:, cols]
            da_a = da_g * a_pre
            vec_ref[0:1, cols] += _colsum(da_a)
            put(3, g * POOL_GROUP, da_a * psc * (1.0 - sa[:, cols]))
            da_pre = (da_g * psc).astype(BF16)
            dwp_acc[g] += _dot_tn(pg, da_pre)
            dpooled = _dot_nt(da_pre, wp_ref[g])
            dpooled_parts.append(dpooled)
            inv_cnt = 1.0 / jnp.minimum(pos1, w).astype(F32)
            win_ref[0:tm, cols] = dpooled * inv_cnt
        for g, w in enumerate(POOL_WINDOWS):
            cols = slice(g * POOL_GROUP, (g + 1) * POOL_GROUP)
            acc = _window_sum(win_ref[:, cols], w, True)[0:tm, :]
            put(0, g * POOL_GROUP, acc - dpooled_parts[g])
        win_ref[tm:tm + POOL_HALO, :] = win_ref[0:POOL_HALO, :]

        for h in range(N_HEADS):
            cols = slice(h * HEAD, (h + 1) * HEAD)
            g_h = gated[:, cols]
            db_h = dbbr[:, cols]
            bbr_h = _dot(g_h, wpr_ref[h])
            put(4, h * HEAD, db[:, cols] * bbr_h * (1.0 - sb[:, cols]))
            dwpr_acc[h] += _dot_tn(g_h, db_h)
            dgated_h = _dot_nt(db_h, wpr_ref[h])
            dsv_ref[:, cols] = dgated_h * u[:, cols]
            sv_ref[:, cols] = dgated_h * sv[:, cols]
        put(1, 0, sv_ref[...] * du_dz)
        dsv = dsv_ref[...]
        dsvb = dsv.astype(BF16)
        blk_sum = dsv[0:SGU_BLOCK, :]
        for b in range(1, tm // SGU_BLOCK):
            blk_sum = blk_sum + dsv[b * SGU_BLOCK:(b + 1) * SGU_BLOCK, :]
        dsvsum_ref[...] += blk_sum
        for b in range(tm // SGU_BLOCK):
            rows = slice(b * SGU_BLOCK, (b + 1) * SGU_BLOCK)
            for h in range(N_HEADS):
                cols = slice(h * HEAD, (h + 1) * HEAD)
                dws_ref[h] += _dot_nt(dsvb[rows, cols], vb[rows, cols])
                dv_ref[rows, cols] = _dot(wst_ref[h], dsvb[rows, cols])
        dv = dv_ref[...]
        vec_ref[1:2, :] += _colsum(dv * vhat)
        vec_ref[2:3, :] += _colsum(dv)
        dvhat = dv * lng_ref[...]
        dgv = rstd * (dvhat - _mean(dvhat) - vhat * _mean(dvhat * vhat))
        put(2, 0, dgv * dgv_dz)

        @pl.when(i == n_t - 1)
        def _():
            for h in range(N_HEADS):
                dws_ref[h] = jnp.where(mask_ref[...] > 0.0, dws_ref[h], 0.0)
                tot = jnp.sum(dsvsum_ref[:, h * HEAD:(h + 1) * HEAD], axis=1, keepdims=True)
                dbsp_ref[:, h * LANES:(h + 1) * LANES] = jnp.broadcast_to(tot, (SGU_BLOCK, LANES))
            for acc, out in ((dwp_acc, dwp_ref), (dwpr_acc, dwpr_ref)):
                for g in range(N_HEADS):
                    for s in range(N_SHARD):
                        out[g // 2, s, (g % 2) * shard_rows:(g % 2 + 1) * shard_rows, :] = (
                            acc[g, s * shard_rows:(s + 1) * shard_rows, :].astype(BF16))

    row = lambda i: (n_t - 1 - i, 0)
    fixed2 = lambda i: (0, 0)
    fixed3 = lambda i: (0, 0, 0)
    fixed4 = lambda i: (0, 0, 0, 0)
    vec = pl.BlockSpec((1, D), fixed2)
    sq = pl.BlockSpec((N_HEADS, SGU_BLOCK, SGU_BLOCK), fixed3)
    grp = pl.BlockSpec((N_HEADS, HEAD, HEAD), fixed3)
    grp_out = pl.BlockSpec((2, N_SHARD, 2 * shard_rows, HEAD), fixed4)
    tile = pl.BlockSpec((tm, D), row)
    return pl.pallas_call(
        body, name="mix_bwd",
        grid=(n_t,),
        in_specs=[pl.BlockSpec((tm, D_IN), row), tile, tile,
                  grp, vec, vec, vec, sq, sq, pl.BlockSpec((N_HEADS, SGU_BLOCK, 1), fixed3), grp,
                  pl.BlockSpec((D, D), fixed2), pl.BlockSpec((SGU_BLOCK, SGU_BLOCK), fixed2),
                  pl.BlockSpec(dep.shape, fixed2)],
        out_specs=[pl.BlockSpec((tm, D_IN), row), grp_out, grp_out, sq,
                   pl.BlockSpec((SGU_BLOCK, N_HEADS * LANES), fixed2),
                   pl.BlockSpec((8, D), fixed2), pl.BlockSpec((8, D_IN), fixed2)],
        out_shape=[jax.ShapeDtypeStruct((t, D_IN), BF16),
                   jax.ShapeDtypeStruct((2, N_SHARD, 2 * shard_rows, HEAD), BF16),
                   jax.ShapeDtypeStruct((2, N_SHARD, 2 * shard_rows, HEAD), BF16),
                   jax.ShapeDtypeStruct((N_HEADS, SGU_BLOCK, SGU_BLOCK), F32),
                   jax.ShapeDtypeStruct((SGU_BLOCK, N_HEADS * LANES), F32),
                   jax.ShapeDtypeStruct((8, D), F32), jax.ShapeDtypeStruct((8, D_IN), F32)],
        scratch_shapes=[pltpu.VMEM((tm + POOL_HALO, D), F32), pltpu.VMEM((tm, D), F32), pltpu.VMEM((tm, D), F32),
                        pltpu.VMEM((tm, D), F32), pltpu.VMEM((SGU_BLOCK, D), F32),
                        pltpu.VMEM((N_HEADS, HEAD, HEAD), F32), pltpu.VMEM((N_HEADS, HEAD, HEAD), F32)],
        compiler_params=_params(1),
    )(z, pooled, dy, w_pool, pool_scale, ln_g, ln_b, ws, ws_t, bsp, w_proj, w_out, mask, dep)


def _dx(dz, w_in4, x, dh1, g1, dep, tm):
    t = x.shape[0]
    cw = D_IN // N_SHARD

    def body(dz_ref, w_ref, x_ref, dh1_ref, g_ref, dep_ref, gx_ref, vec_ref):
        @pl.when(pl.program_id(0) == 0)
        def _():
            vec_ref[...] = jnp.zeros_like(vec_ref)

        dxn = _dot_nt(dz_ref[:, 0:cw], w_ref[0])
        for s in range(1, N_SHARD):
            dxn = dxn + _dot_nt(dz_ref[:, s * cw:(s + 1) * cw], w_ref[s])
        xv = x_ref[...]
        r = lax.rsqrt(_mean(xv * xv) + EPS)
        xh = xv * r
        vec_ref[0:1, :] += _colsum(dxn * xh)
        gx_ref[...] = dh1_ref[...] + _rms_bwd(dxn * g_ref[...], xh, r)

    row = lambda i: (i, 0)
    fixed = lambda i: (0, 0)
    return pl.pallas_call(
        body, name="dx",
        grid=(t // tm,),
        in_specs=[pl.BlockSpec((tm, D_IN), row),
                  _resident((N_SHARD, D, cw), 1),
                  pl.BlockSpec((tm, D), row), pl.BlockSpec((tm, D), row), pl.BlockSpec((1, D), fixed),
                  pl.BlockSpec(dep.shape, fixed)],
        out_specs=[pl.BlockSpec((tm, D), row), pl.BlockSpec((8, D), fixed)],
        out_shape=[jax.ShapeDtypeStruct((t, D), F32), jax.ShapeDtypeStruct((8, D), F32)],
        compiler_params=_params(1),
    )(dz, w_in4, x, dh1, g1, dep)


def _dw(a, b, name, n_split, tn, tk, n_out, out_block, dep=None):
    t, m = a.shape
    n = b.shape[1]
    tm = min(m, D)
    rows = tm // n_split
    n_k = t // tk
    deps = [] if dep is None else [dep]

    def body(a_ref, b_ref, *rest):
        o_ref, acc_ref = rest[-2:]
        k = pl.program_id(2)

        @pl.when(k == 0)
        def _():
            acc_ref[...] = jnp.zeros_like(acc_ref)

        acc_ref[...] += _dot_tn(a_ref[...], b_ref[...])

        @pl.when(k == n_k - 1)
        def _():
            for q in range(n_split):
                o_ref[q % 2, q // 2] = acc_ref[q * rows:(q + 1) * rows, :].astype(BF16)

    return pl.pallas_call(
        body, name=name,
        grid=(m // tm, n // tn, n_k),
        in_specs=[pl.BlockSpec((tk, tm), lambda i, j, k: (k, i)),
                  pl.BlockSpec((tk, tn), lambda i, j, k: (k, j))]
        + [pl.BlockSpec(d.shape, lambda i, j, k: (0, 0)) for d in deps],
        out_specs=pl.BlockSpec((2, n_split // 2, rows, tn), lambda i, j, k: (0, out_block(i, j), 0, 0)),
        out_shape=jax.ShapeDtypeStruct((2, n_out, rows, tn), BF16),
        scratch_shapes=[pltpu.VMEM((tm, tn), F32)],
        compiler_params=_params(3),
    )(a, b, *deps)


ROW_STEPS = 2


def _chip_sums(core, grads, recvs, name):
    n = len(grads)

    def body(c_ref, *refs):
        for w in range(n):
            refs[2 * n + w][...] = (refs[w][0].astype(F32) + refs[n + w][...].astype(F32)).astype(BF16)

    def blocks(g):
        _, _, rh, cols = g.shape
        rb = rh // ROW_STEPS
        return (pl.BlockSpec((1, 1, rb, cols), lambda s, r, c: (c[0], s, r, 0)),
                pl.BlockSpec((1, rb, cols), lambda s, r, c: (s, r, 0)))

    specs = [blocks(g) for g in grads]
    return pl.pallas_call(
        body, name=name,
        grid_spec=pltpu.PrefetchScalarGridSpec(
            num_scalar_prefetch=1, grid=(N_SHARD, ROW_STEPS),
            in_specs=[s[0] for s in specs] + [s[1] for s in specs],
            out_specs=[s[1] for s in specs]),
        out_shape=[jax.ShapeDtypeStruct(g.shape[1:], BF16) for g in grads],
        compiler_params=_params(2),
    )(core, *grads, *recvs)


def _pair_sum(a, b, name):
    def body(a_ref, b_ref, o_ref):
        o_ref[...] = a_ref[...] + b_ref[...]

    vmem = pl.BlockSpec(memory_space=pltpu.VMEM)
    return pl.pallas_call(
        body, name=name, in_specs=[vmem, vmem], out_specs=vmem,
        out_shape=jax.ShapeDtypeStruct(a.shape, F32),
    )(a, b)


def _adamw_update(g, w, m, v):
    m_new = ADAM_B1 * m + (1.0 - ADAM_B1) * g
    v_new = ADAM_B2 * v + (1.0 - ADAM_B2) * (g * g)
    m_hat = m_new / (1.0 - ADAM_B1 ** ADAM_STEP)
    v_hat = v_new / (1.0 - ADAM_B2 ** ADAM_STEP)
    return -ADAM_LR * (m_hat / (jnp.sqrt(v_hat) + ADAM_EPS) + ADAM_WD * w), m_new, v_new


def _adamw_small(grad_parts, ws, ms, vs):
    n, n_g = len(ws), len(grad_parts)

    def rows_of(ref, lo, rows):
        if len(ref.shape) == 2:
            return ref[lo:lo + rows, :]
        acc = ref[0, lo:lo + rows, :]
        for s in range(1, ref.shape[0]):
            acc = acc + ref[s, lo:lo + rows, :]
        return acc

    def body(*refs):
        g_refs, ins, outs = refs[:n_g], refs[n_g:n_g + 3 * n], refs[n_g + 3 * n:]
        part, off = 0, 0
        for k in range(n):
            rows = ws[k].shape[0]
            if off + rows > grad_parts[part].shape[-2]:
                part, off = part + 1, 0
            g = rows_of(g_refs[part], off, rows)
            off += rows
            delta, m_new, v_new = _adamw_update(g, ins[k][...], ins[n + k][...], ins[2 * n + k][...])
            for o, val in zip(outs[4 * k:4 * k + 4], (g, delta, m_new, v_new)):
                o[...] = val

    vmem = pl.BlockSpec(memory_space=pltpu.VMEM)
    outs = pl.pallas_call(
        body, name="adamw_small",
        in_specs=[vmem] * (n_g + 3 * n), out_specs=[vmem] * (4 * n),
        out_shape=[jax.ShapeDtypeStruct(w.shape, F32) for w in ws for _ in range(4)],
    )(*grad_parts, *ws, *ms, *vs)
    return [outs[4 * k:4 * k + 4] for k in range(n)]


def _final_sums(chip, owns, slots, name):
    n = len(owns)

    def body(chip_ref, *refs):
        for w in range(n):
            own_ref, s1_ref, s2_ref, s3_ref = refs[4 * w:4 * w + 4]
            acc = own_ref[0].astype(F32) + s1_ref[0].astype(F32)
            acc = acc + s2_ref[0].astype(F32)
            refs[4 * n + w][...] = acc + s3_ref[0].astype(F32)

    def slot(a, k):
        n_s, rh, cols = a.shape
        return pl.BlockSpec((1, rh // ROW_STEPS, cols), lambda r, chip: (lax.rem(chip[0] + k, n_s), r, 0))

    operands, in_specs = [], []
    for own, slots_w in zip(owns, slots):
        operands += [own, slots_w, slots_w, slots_w]
        in_specs += [slot(own, k) for k in range(4)]
    return pl.pallas_call(
        body, name=name,
        grid_spec=pltpu.PrefetchScalarGridSpec(
            num_scalar_prefetch=1, grid=(ROW_STEPS,),
            in_specs=in_specs,
            out_specs=[pl.BlockSpec((a.shape[1] // ROW_STEPS, a.shape[2]), lambda r, chip: (r, 0)) for a in owns]),
        out_shape=[jax.ShapeDtypeStruct(a.shape[1:], F32) for a in owns],
        compiler_params=_params(1),
    )(chip, *operands)


def _adamw_halves(core, owns, siblings, ws, ms, vs, name):
    n = len(ws)

    def body(c_ref, *refs):
        ins, outs = refs[:5 * n], refs[5 * n:]
        mine = c_ref[0] == pl.program_id(0)
        for k in range(n):
            own_ref, sib_ref, w_ref, m_ref, v_ref = ins[5 * k:5 * k + 5]
            g_ref, d_ref, mo_ref, vo_ref = outs[4 * k:4 * k + 4]

            @pl.when(mine)
            def _():
                g_ref[...] = own_ref[...]

            @pl.when(jnp.logical_not(mine))
            def _():
                g_ref[...] = sib_ref[...]

            d_ref[...], mo_ref[...], vo_ref[...] = _adamw_update(g_ref[...], w_ref[...], m_ref[...], v_ref[...])

    operands, in_specs, out_specs = [], [], []
    for own, sib, w, m, v in zip(owns, siblings, ws, ms, vs):
        rows, cols = w.shape
        rb = rows // 2 // ROW_STEPS
        half = pl.BlockSpec((rb, cols), lambda h, r, c: (r, 0))
        whole = pl.BlockSpec((rb, cols), lambda h, r, c: (h * ROW_STEPS + r, 0))
        operands += [own, sib, w, m, v]
        in_specs += [half, half, whole, whole, whole]
        out_specs += [whole] * 4
    outs = pl.pallas_call(
        body, name=name,
        grid_spec=pltpu.PrefetchScalarGridSpec(
            num_scalar_prefetch=1, grid=(2, ROW_STEPS), in_specs=in_specs, out_specs=out_specs),
        out_shape=[jax.ShapeDtypeStruct(w.shape, F32) for w in ws for _ in range(4)],
        compiler_params=_params(2),
    )(core, *operands)
    return [outs[4 * k:4 * k + 4] for k in range(n)]


def _place():
    x, y, c = lax.axis_index("x"), lax.axis_index("y"), lax.axis_index("c")
    chips = [(1 - x, y), (x, 1 - y), (1 - x, 1 - y)]
    return x, y, c, chips


def _remote(src, dst, send_sem, recv_sem, device):
    return pltpu.make_async_remote_copy(src_ref=src, dst_ref=dst, send_sem=send_sem, recv_sem=recv_sem,
                                        device_id=device, device_id_type=MESH)


def _split_start(name, arrays, n_copies, plan):
    n = len(arrays)

    def body(*refs):
        ins, send_sem, recv_sem, token = refs[:n], refs[n], refs[n + 1], refs[-1]
        for k, (src, dst, device) in enumerate(plan(ins)):
            _remote(src, dst, send_sem.at[k], recv_sem.at[k], device).start()
        token[...] = jnp.zeros_like(token)

    outs = pl.pallas_call(
        body, name=name,
        in_specs=[HBM] * n,
        out_specs=(SEM, SEM, *[HBM] * n, pl.BlockSpec(memory_space=pltpu.VMEM)),
        out_shape=(pltpu.SemaphoreType.DMA((n_copies,)), pltpu.SemaphoreType.DMA((n_copies,)),
                   *[pltpu.HBM(a.shape, a.dtype) for a in arrays], jax.ShapeDtypeStruct((8, LANES), F32)),
        input_output_aliases={i: i + 2 for i in range(n)},
        compiler_params=pltpu.CompilerParams(has_side_effects=EFFECT),
    )(*[pltpu.with_memory_space_constraint(a, pltpu.HBM) for a in arrays])
    return outs[0], outs[1], list(outs[2:2 + n]), outs[-1]


def _split_wait(name, send_sem, recv_sem, arrays, plan, first, *after):
    n = len(arrays)

    def body(*refs):
        ins, send_ref, recv_ref = refs[:n], refs[n], refs[n + 1]
        for k, (src, dst, device) in enumerate(plan(ins)):
            cp = _remote(src, dst, send_ref.at[first + k], recv_ref.at[first + k], device)
            cp.wait_send()
            cp.wait_recv()

    outs = pl.pallas_call(
        body, name=name,
        in_specs=[HBM] * n + [SEM, SEM] + [ANY] * len(after), out_specs=[HBM] * n,
        out_shape=[pltpu.HBM(a.shape, a.dtype) for a in arrays],
        input_output_aliases={i: i for i in range(n)},
        compiler_params=pltpu.CompilerParams(has_side_effects=EFFECT),
    )(*arrays, send_sem, recv_sem, *after)
    return list(outs)


def _plans(*parts):
    def plan(refs):
        copies, lo = [], 0
        for part, n in parts:
            copies += part(refs[lo:lo + n])
            lo += n
        return copies
    return plan


def _plan_gather_send(n):
    def plan(refs):
        x, y, c, chips = _place()
        me = 2 * x + y
        copies = []
        for w in range(n):
            own, full = refs[w], refs[n + w]
            copies.append((own, full.at[me], (x, y, 1 - c)))
            copies += [(own.at[c], full.at[me, c], (px, py, c)) for px, py in chips]
        return copies
    return plan


def _plan_gather_direct(n):
    def plan(refs):
        x, y, c, chips = _place()
        me = 2 * x + y
        copies = []
        for w in range(n):
            own, full = refs[w], refs[n + w]
            copies.append((own, full.at[me], (x, y, 1 - c)))
            for px, py in chips:
                copies += [(own.at[c], full.at[me, c], (px, py, c)), (own.at[c], full.at[me, c], (px, py, 1 - c))]
        return copies
    return plan


def _plan_gather_pass(n):
    def plan(refs):
        x, y, c, chips = _place()
        copies = []
        for w in range(n):
            for px, py in chips:
                landed = refs[w].at[2 * px + py, c]
                copies.append((landed, landed, (x, y, 1 - c)))
        return copies
    return plan


def _plan_swap(n, n_wholes):
    def plan(refs):
        x, y, c, _ = _place()
        n_all = n + n_wholes
        return [(refs[w].at[1 - c] if w < n else refs[w], refs[n_all + w], (x, y, 1 - c)) for w in range(n_all)]
    return plan


def _plan_share(n):
    def plan(refs):
        x, y, c, _ = _place()
        return [(refs[w], refs[n + w], (x, y, 1 - c)) for w in range(n)]
    return plan


def _plan_exchange(n, with_small):
    def plan(refs):
        x, y, c, chips = _place()
        me = 2 * x + y
        n_in = n + int(with_small)
        copies = []
        for w in range(n):
            copies += [(refs[w].at[2 * px + py], refs[n_in + w].at[me], (px, py, c)) for px, py in chips]
        if with_small:
            small, slots = refs[n], refs[n_in + n]
            copies.append((small, slots.at[me], (x, y, 1 - c)))
            copies += [(small, slots.at[me], (px, py, c)) for px, py in chips]
        return copies
    return plan


def _share_and_all_reduce(wholes, part):
    n = len(wholes)
    n_dev = 8

    def body(*refs):
        ins, part_ref, outs, sum_ref = refs[:n], refs[n], refs[n + 1:2 * n + 1], refs[2 * n + 1]
        slots_ref, send_sem, recv_sem, send_tile, recv_tile = refs[2 * n + 2:]
        x, y, c, _ = _place()
        me = 4 * x + 2 * y + c
        copies = []
        for w in range(n):
            copies.append(_remote(ins[w], outs[w], send_sem.at[w], recv_sem.at[w], (x, y, 1 - c)))
        slots_ref[me] = part_ref[...]
        flips = [(fx, fy, fc) for fx in (0, 1) for fy in (0, 1) for fc in (0, 1)][1:]
        for k, (fx, fy, fc) in enumerate(flips):
            peer = (x + fx - 2 * x * fx, y + fy - 2 * y * fy, c + fc - 2 * c * fc)
            copies.append(_remote(part_ref, slots_ref.at[me], send_tile.at[k], recv_tile.at[k], peer))
        for cp in copies:
            cp.start()
        for cp in copies:
            cp.wait()
        acc = slots_ref[0]
        for k in range(1, n_dev):
            acc = acc + slots_ref[k]
        sum_ref[...] = acc

    vmem = pl.BlockSpec(memory_space=pltpu.VMEM)
    outs = pl.pallas_call(
        body, name="rs_share_late",
        in_specs=[ANY] * n + [vmem], out_specs=[ANY] * n + [vmem],
        out_shape=[jax.ShapeDtypeStruct(a.shape, a.dtype) for a in wholes] + [jax.ShapeDtypeStruct(part.shape, F32)],
        scratch_shapes=[pltpu.VMEM((n_dev,) + part.shape, F32),
                        pltpu.SemaphoreType.DMA((n,)), pltpu.SemaphoreType.DMA((n,)),
                        pltpu.SemaphoreType.DMA((n_dev - 1,)), pltpu.SemaphoreType.DMA((n_dev - 1,))],
    )(*wholes, part)
    return outs[:n], outs[n]


_SMALL = ("norm1_pre_g", "b_in", "pool_scale", "sgu_ln_g", "sgu_ln_b", "w_spatial",
          "norm1_post_g", "norm2_pre_g", "norm2_post_g", "b_spatial")
_MIX = ("w_in", "w_pool", "w_sgu_proj", "w_out")
_FF = ("w_ff1", "w_ff2")
_BIG = _MIX + _FF
_ORDER = ("norm1_pre_g", "w_in", "b_in", "w_pool", "pool_scale", "sgu_ln_g", "sgu_ln_b", "w_spatial", "b_spatial",
          "w_sgu_proj", "w_out", "norm1_post_g", "norm2_pre_g", "w_ff1", "w_ff2", "norm2_post_g")


def _pack(keys, rows, parts):
    flat = jnp.concatenate([parts[k].reshape(-1).astype(F32) for k in keys])
    flat = jnp.pad(flat, (0, rows * LANES - flat.shape[0]))
    return flat.reshape(rows, LANES)


def _halves(a):
    return a.reshape(2, a.shape[0] // 2, a.shape[1])


def _step(p, m, v, x, target):
    t = x.shape[0]
    tm_mm = min(1024, t)
    tm_mix = min(256, t)
    tm_dx = min(512, t)
    core = lax.axis_index("c").astype(jnp.int32).reshape(1)
    row = lambda a: a.reshape(1, -1)

    chip = (2 * lax.axis_index("x") + lax.axis_index("y")).astype(jnp.int32).reshape(1)

    local2d = {"w_in": p["w_in"], "w_ff1": p["w_ff1"], "w_ff2": p["w_ff2"], "w_out": p["w_out"],
               "w_pool": p["w_pool"].reshape(N_HEADS * 64, POOL_GROUP),
               "w_sgu_proj": p["w_sgu_proj"].reshape(N_HEADS * 64, HEAD)}
    shard = {k: _halves(local2d[k].astype(BF16)) for k in _BIG}
    landing = lambda keys: [lax.empty((N_SHARD,) + shard[k].shape, BF16) for k in keys]
    in_arrays = [shard["w_in"]] + landing(["w_in"])
    send_a, recv_a, in_arrays, token = _split_start("ag_in_send", in_arrays, 4, _plan_gather_send(1))
    z, xn = _inproj_own(x, row(p["norm1_pre_g"]), in_arrays[0].reshape(D, D_IN // N_SHARD), row(p["b_in"]), chip,
                        token, tm_mm)
    mix_keys = _MIX[1:]
    n_mix, n_ff = len(mix_keys), len(_FF)
    in_arrays = _split_wait("ag_in_land", send_a, recv_a, in_arrays, _plan_gather_send(1), 0, z,
                            *[shard[k] for k in mix_keys + _FF])
    rest = in_arrays[1:] + [shard[k] for k in mix_keys] + landing(mix_keys) + [shard[k] for k in _FF] + landing(_FF)
    rest_plan = _plans((_plan_gather_pass(1), 1), (_plan_gather_direct(n_mix), 2 * n_mix),
                       (_plan_gather_send(n_ff), 2 * n_ff))
    send0, recv0, rest, token = _split_start("ag_rest_send", rest, 3 + 7 * n_mix + 4 * n_ff, rest_plan)
    in_full, mix_arrays, ff_arrays = rest[:1], rest[1:1 + 2 * n_mix], rest[1 + 2 * n_mix:]
    in_full = _split_wait("ag_in_passed", send0, recv0, in_full, _plan_gather_pass(1), 0, token)
    w_in4 = in_full[0].reshape(N_SHARD, D, D_IN // N_SHARD)

    pos = jnp.arange(SGU_BLOCK) // 64
    mask = (pos[:, None] >= pos[None, :]).astype(F32)
    ws = (p["w_spatial"] * mask[None]).astype(BF16)
    ws_t = jnp.swapaxes(ws, 1, 2)
    bsp = p["b_spatial"].reshape(N_HEADS, SGU_BLOCK, 1)

    z = _inproj_rest(xn, w_in4, row(p["b_in"]), chip, z, tm_mm)
    mix_arrays = _split_wait("ag_mix_land", send0, recv0, mix_arrays, _plan_gather_direct(n_mix), 3, z)
    full = dict(zip(mix_keys, mix_arrays[len(mix_keys):]))
    w_out = full["w_out"].reshape(D, D)
    regroup = lambda a: a.reshape(N_SHARD, N_HEADS, 64, 256).transpose(1, 0, 2, 3).reshape(N_HEADS, 256, 256)
    w_pool = regroup(full["w_pool"])
    w_proj = regroup(full["w_sgu_proj"])
    pooled, merged, y, h1 = _mix_fwd(z, x, w_pool, row(p["pool_scale"]), row(p["sgu_ln_g"]), row(p["sgu_ln_b"]),
                                     ws, bsp, w_proj, w_out, row(p["norm1_post_g"]), tm_mix)
    ff_arrays = _split_wait("ag_ff_land", send0, recv0, ff_arrays, _plan_gather_send(n_ff), 3 + 7 * n_mix, pooled)
    send2, recv2, ff_full, token = _split_start("ag_ff_pass", ff_arrays[n_ff:], 3 * n_ff, _plan_gather_pass(n_ff))
    ff_full = _split_wait("ag_ff_passed", send2, recv2, ff_full, _plan_gather_pass(n_ff), 0, token)
    w_ff1_4 = ff_full[0].reshape(N_SHARD, D, D)
    w_ff2 = ff_full[1].reshape(N_SHARD, D, D)
    hn, f1, f, df2, dout, vec_a = _ffn_fwd(h1, row(p["norm2_pre_g"]), w_ff1_4, w_ff2, row(p["norm2_post_g"]),
                                           target, tm_mix)

    tk = min(2048, t)
    g_big = {"w_ff2": _dw(f, df2, "dw_ff2", 2, D, tk, N_SHARD, lambda i, j: i)}
    df1, dh1, dy, vec_b = _ffn_bwd(df2, f1, w_ff1_4, w_ff2, h1, dout, y, row(p["norm2_pre_g"]),
                                   row(p["norm1_post_g"]), tm_mix)
    g_big["w_ff1"] = _dw(hn, df1, "dw_ff1", 2, D, tk, N_SHARD, lambda i, j: j)

    def reduce_start(name, keys, wholes, meanwhile, also=(), also_plan=None, also_copies=0):
        n_k, n_all = len(keys), len(keys) + len(wholes)
        swap_plan = _plan_swap(n_k, len(wholes))
        crossing = [g_big[k] for k in keys] + list(wholes)
        crossing += [lax.empty(a.shape[1:] if w < n_k else a.shape, a.dtype) for w, a in enumerate(crossing)]
        send, recv, crossing, tok = _split_start("rs_swap_" + name + "_send", crossing, n_all, swap_plan)
        crossing = _split_wait("rs_swap_" + name + "_land", send, recv, crossing, swap_plan, 0, meanwhile(tok))
        mine, wholes, swapped = crossing[:n_k], crossing[n_k:n_all], crossing[n_all:]
        sums = _chip_sums(core, mine, swapped[:n_k], "chip_sums_" + name)
        arrays = list(sums)
        if wholes:
            arrays.append(_pair_sum(wholes[0], swapped[-1], "chip_sum_small"))
        arrays += [lax.empty(s.shape, BF16) for s in sums]
        if wholes:
            arrays.append(lax.empty((N_SHARD,) + wholes[0].shape, F32))
        plan = _plan_exchange(len(keys), bool(wholes))
        n_copies, n = 3 * len(keys) + 4 * len(wholes), len(arrays)
        also = also() if callable(also) else also
        both = plan if also_plan is None else _plans((plan, n), (also_plan, len(also)))
        send, recv, out, tok = _split_start("rs_send_" + name, arrays + list(also), n_copies + also_copies, both)
        return send, recv, out[:n], plan, tok, out[n:], n_copies

    def reduce_finish(name, keys, state, *after):
        send, recv, arrays, plan = state[:4]
        arrays = _split_wait("rs_land_" + name, send, recv, arrays, plan, 0, *after)
        n_in = len(arrays) // 2
        n = len(keys)
        finished = _final_sums(chip, arrays[:n], arrays[n_in:n_in + n], "final_sums_" + name)
        return dict(zip(keys, finished)), arrays

    early = _FF
    late = ("w_in", "w_pool", "w_sgu_proj", "w_out")

    def dw_out(token):
        g_big["w_out"] = _dw(merged, dy, "dw_out", 2 * N_SHARD, D, tk, N_SHARD, lambda i, j: 0, token)
        return g_big["w_out"]

    early_state = reduce_start("early", early, [], dw_out)
    dz, g_big["w_pool"], g_big["w_sgu_proj"], dws, dbsp, vec_c, dbin = _mix_bwd(
        z, pooled, dy, w_pool, row(p["pool_scale"]), row(p["sgu_ln_g"]), row(p["sgu_ln_b"]), ws, ws_t, bsp, w_proj,
        w_out, mask, tm_mix, early_state[4])
    g_big["w_in"] = _dw(xn, dz, "dw_in", 2, D_IN // N_SHARD, tk, N_SHARD, lambda i, j: j)
    small_part = _pack(_SMALL[1:], SMALL_ROWS - 8, {
        "b_in": dbin[0], "pool_scale": vec_c[0], "sgu_ln_g": vec_c[1], "sgu_ln_b": vec_c[2],
        "w_spatial": dws, "b_spatial": dbsp[:, ::LANES].T, "norm1_post_g": vec_b[1], "norm2_pre_g": vec_b[0],
        "norm2_post_g": vec_a[0]})
    halves = {}

    def finish_early(token):
        halves.update(reduce_finish("early", early, early_state, token)[0])
        return halves[early[-1]]

    share_early = lambda: [halves[k] for k in early] + [lax.empty(halves[k].shape, F32) for k in early]
    late_state = reduce_start("late", late, [small_part], finish_early, share_early, _plan_share(len(early)),
                              len(early))
    grad_x, vec_d = _dx(dz, w_in4, x, dh1, row(p["norm1_pre_g"]), late_state[4], tm_dx)
    share = _split_wait("rs_share_early_land", late_state[0], late_state[1], late_state[5], _plan_share(len(early)),
                        late_state[6], grad_x)

    grads, delta, new_m, new_v = {}, {}, {}, {}

    def update(name, keys, own, shared):
        as2d = lambda a, k: a.reshape(local2d[k].shape)
        outs = _adamw_halves(core, own, shared, [local2d[k] for k in keys],
                             [as2d(m[k], k) for k in keys], [as2d(v[k], k) for k in keys], "adamw_" + name)
        for k, outs_k in zip(keys, outs):
            grads[k], delta[k], new_m[k], new_v[k] = (a.reshape(p[k].shape) for a in outs_k)

    update("early", early, share[:len(early)], share[len(early):])
    late_halves, late_arrays = reduce_finish("late", late, late_state, grad_x, *[new_v[k] for k in early])
    own_late = [late_halves[k] for k in late]
    shared_late, tiles = _share_and_all_reduce(
        own_late, jnp.concatenate([vec_d[0].reshape(8, LANES), jnp.broadcast_to(vec_a[1:2, :LANES], (8, LANES))]))
    g1_grad, loss = tiles[:8], tiles[8, 0]
    update("late", late, own_late, shared_late)
    as_rows = lambda d: [d[k].reshape(-1, LANES) for k in _SMALL]
    small_outs = _adamw_small([g1_grad, late_arrays[-1]], as_rows(p), as_rows(m), as_rows(v))
    for k, outs_k in zip(_SMALL, small_outs):
        grads[k], delta[k], new_m[k], new_v[k] = (a.reshape(p[k].shape) for a in outs_k)

    return (loss, grad_x, *[grads[k] for k in _ORDER], *[delta[k] for k in _ORDER],
            *[new_m[k] for k in _ORDER], *[new_v[k] for k in _ORDER])


def kernel(x, norm1_pre_g, w_in, b_in, w_pool, pool_scale, sgu_ln_g, sgu_ln_b, w_spatial, b_spatial, w_sgu_proj, w_out, norm1_post_g, norm2_pre_g, w_ff1, w_ff2, norm2_post_g, loss_target, m_norm1_pre_g, m_w_in, m_b_in, m_w_pool, m_pool_scale, m_sgu_ln_g, m_sgu_ln_b, m_w_spatial, m_b_spatial, m_w_sgu_proj, m_w_out, m_norm1_post_g, m_norm2_pre_g, m_w_ff1, m_w_ff2, m_norm2_post_g, v_norm1_pre_g, v_w_in, v_b_in, v_w_pool, v_pool_scale, v_sgu_ln_g, v_sgu_ln_b, v_w_spatial, v_b_spatial, v_w_sgu_proj, v_w_out, v_norm1_post_g, v_norm2_pre_g, v_w_ff1, v_w_ff2, v_norm2_post_g):
    p = dict(zip(_ORDER, (norm1_pre_g, w_in, b_in, w_pool, pool_scale, sgu_ln_g, sgu_ln_b, w_spatial, b_spatial,
                          w_sgu_proj, w_out, norm1_post_g, norm2_pre_g, w_ff1, w_ff2, norm2_post_g)))
    m = dict(zip(_ORDER, (m_norm1_pre_g, m_w_in, m_b_in, m_w_pool, m_pool_scale, m_sgu_ln_g, m_sgu_ln_b, m_w_spatial,
                          m_b_spatial, m_w_sgu_proj, m_w_out, m_norm1_post_g, m_norm2_pre_g, m_w_ff1, m_w_ff2,
                          m_norm2_post_g)))
    v = dict(zip(_ORDER, (v_norm1_pre_g, v_w_in, v_b_in, v_w_pool, v_pool_scale, v_sgu_ln_g, v_sgu_ln_b, v_w_spatial,
                          v_b_spatial, v_w_sgu_proj, v_w_out, v_norm1_post_g, v_norm2_pre_g, v_w_ff1, v_w_ff2,
                          v_norm2_post_g)))
    outs = _step(p, m, v, x[0], loss_target[0])
    return (outs[0], outs[1][None], *outs[2:])
```
